```python
import math
import jax
import jax.numpy as jnp
from jax import lax
import numpy as np

D_MODEL = 1024
BATCH = 4
SEQ = 8192
DEPTH = 2

GRID_W = 64
CTX_LEN = 256
HEAD_DIM = 64
NA_HEADS = 6
NA_WIN_H = 8
NA_WIN_W = 16
DIFF_HEADS = 4
DIFF_QK_DIM = 32
DIFF_V_DIM = 2 * DIFF_QK_DIM
GQA_Q_HEADS = 6
GQA_KV_HEADS = 2
GQA_GROUP = GQA_Q_HEADS // GQA_KV_HEADS
N_GROUPS = 4
EXPERTS_PER_GROUP = 4
N_EXPERTS = N_GROUPS * EXPERTS_PER_GROUP
TOP_K_IN_GROUP = 2
EXPERT_HIDDEN = 512
Q_BLOCK = 128
ROPE_THETA = 10000.0
EPS = 1e-6
W_A = NA_HEADS * HEAD_DIM
W_B = DIFF_HEADS * DIFF_V_DIM
W_C = GQA_Q_HEADS * HEAD_DIM
MIX_WIDTH = W_A + W_B + W_C
PROJ_SIZES = (W_A, W_A, W_A, 2 * DIFF_HEADS * DIFF_QK_DIM, 2 * DIFF_HEADS * DIFF_QK_DIM, W_B, W_C, GQA_KV_HEADS * HEAD_DIM, GQA_KV_HEADS * HEAD_DIM)
IN_WIDTH = sum(PROJ_SIZES)

kernel_name = 'hybrid_na_diff_gqa_hmoe_block'


def rms_norm(x, g):
    xf = x.astype(jnp.float32)
    y = xf * lax.rsqrt(jnp.mean(xf * xf, axis=-1, keepdims=True) + EPS)
    return (y * g.astype(jnp.float32)).astype(x.dtype)


def split_proj(p):
    outs, start = [], 0
    for w in PROJ_SIZES:
        outs.append(p[..., start:start + w])
        start += w
    return outs


def split_heads(t, n):
    b, s, w = t.shape
    return t.reshape(b, s, n, w // n).transpose(0, 2, 1, 3)


def merge_heads(t):
    b, h, s, d = t.shape
    return t.transpose(0, 2, 1, 3).reshape(b, s, h * d)


def diff_split(t):
    b, s, _ = t.shape
    t = t.reshape(b, s, DIFF_HEADS, 2, DIFF_QK_DIM).transpose(3, 0, 2, 1, 4)
    return t[0], t[1]


def axial_rope(n_tokens, dim):
    t = jnp.arange(n_tokens)
    row = (t // GRID_W).astype(jnp.float32)
    col = (t % GRID_W).astype(jnp.float32)
    quarter = dim // 4
    freqs = ROPE_THETA ** (-jnp.arange(quarter, dtype=jnp.float32) / quarter)
    ang = jnp.concatenate([row[:, None] * freqs, col[:, None] * freqs], axis=-1)
    return jnp.cos(ang), jnp.sin(ang)


def apply_rope(x, cos, sin):
    half = x.shape[-1] // 2
    xf = x.astype(jnp.float32)
    x1, x2 = xf[..., :half], xf[..., half:]
    return jnp.concatenate([x1 * cos - x2 * sin, x2 * cos + x1 * sin], axis=-1).astype(x.dtype)


def to_blocks(t):
    *lead, s, d = t.shape
    return jnp.moveaxis(t.reshape(*lead, s // Q_BLOCK, Q_BLOCK, d), -3, 0)


def from_blocks(t):
    t = jnp.moveaxis(t, 0, -3)
    *lead, nb, q, d = t.shape
    return t.reshape(*lead, nb * q, d)


def softmax_attend(q, k, v):
    logits = jnp.einsum('bhqd,bhkd->bhqk', q, k).astype(jnp.float32) * (q.shape[-1] ** -0.5)
    p = jax.nn.softmax(logits, axis=-1)
    return jnp.einsum('bhqk,bhkd->bhqd', p.astype(v.dtype), v)


def diff_attend(q1, q2, k1, k2, v, lam):
    scale = DIFF_QK_DIM ** -0.5
    p1 = jax.nn.softmax(jnp.einsum('bhqd,bhkd->bhqk', q1, k1).astype(jnp.float32) * scale, axis=-1)
    p2 = jax.nn.softmax(jnp.einsum('bhqd,bhkd->bhqk', q2, k2).astype(jnp.float32) * scale, axis=-1)
    a = p1 - lam * p2
    return jnp.einsum('bhqk,bhkd->bhqd', a.astype(v.dtype), v)


def gqa_attend(q, k, v):
    logits = jnp.einsum('bgrqd,bgkd->bgrqk', q, k).astype(jnp.float32) * (HEAD_DIM ** -0.5)
    p = jax.nn.softmax(logits, axis=-1)
    return jnp.einsum('bgrqk,bgkd->bgrqd', p.astype(v.dtype), v)


def neighbourhood_attention(q, k, v, k_ctx, v_ctx, rpb):
    b, h, s, d = q.shape
    rows = s // GRID_W
    kh = min(NA_WIN_H, rows)
    kw = NA_WIN_W
    qg = q.reshape(b, h, rows, GRID_W, d)
    kg = k.reshape(b, h, rows, GRID_W, d)
    vg = v.reshape(b, h, rows, GRID_W, d)
    cols = jnp.arange(GRID_W)
    col_idx = jnp.clip(cols - kw // 2, 0, GRID_W - kw)[:, None] + jnp.arange(kw)[None, :]
    dc = col_idx - cols[:, None] + (NA_WIN_W - 1)
    scale = d ** -0.5
    rpb32 = rpb.astype(jnp.float32)

    def row_block(r):
        r0 = jnp.clip(r - kh // 2, 0, rows - kh)
        k_rows = lax.dynamic_slice_in_dim(kg, r0, kh, axis=2)
        v_rows = lax.dynamic_slice_in_dim(vg, r0, kh, axis=2)
        k_win = k_rows[:, :, :, col_idx, :]
        v_win = v_rows[:, :, :, col_idx, :]
        q_r = lax.dynamic_index_in_dim(qg, r, axis=2, keepdims=False)
        dr = r0 + jnp.arange(kh) - r + (NA_WIN_H - 1)
        bias = rpb32[:, dr[:, None, None], dc[None, :, :]].transpose(0, 2, 1, 3)
        s_win = jnp.einsum('bhcd,bhicjd->bhcij', q_r, k_win).astype(jnp.float32) * scale + bias[None]
        s_ctx = jnp.einsum('bhcd,bhld->bhcl', q_r, k_ctx).astype(jnp.float32) * scale
        logits = jnp.concatenate([s_win.reshape(b, h, GRID_W, kh * kw), s_ctx], axis=-1)
        p = jax.nn.softmax(logits, axis=-1).astype(v.dtype)
        p_win = p[..., :kh * kw].reshape(b, h, GRID_W, kh, kw)
        return (jnp.einsum('bhcij,bhicjd->bhcd', p_win, v_win)
                + jnp.einsum('bhcl,bhld->bhcd', p[..., kh * kw:], v_ctx))

    out = lax.map(row_block, jnp.arange(rows))
    return out.transpose(1, 2, 0, 3, 4).reshape(b, h, s, d)


def token_mixers(h, hc, w_in, na_rpb, lq1, lk1, lq2, lk2, diff_subln, q_norm, k_norm, lam_init, ctx_out):
    b, s, _ = h.shape
    n_ctx = hc.shape[1]
    qa, ka, va, qb, kb, vb, qc, kc, vc = split_proj(h @ w_in)
    qa_c, ka_c, va_c, qb_c, kb_c, vb_c, qc_c, kc_c, vc_c = split_proj(hc @ w_in)
    cos_b, sin_b = axial_rope(s, DIFF_QK_DIM)
    cos_c, sin_c = axial_rope(s, HEAD_DIM)

    ka_ctx, va_ctx = split_heads(ka_c, NA_HEADS), split_heads(va_c, NA_HEADS)
    out_a = neighbourhood_attention(split_heads(qa, NA_HEADS), split_heads(ka, NA_HEADS),
                                    split_heads(va, NA_HEADS), ka_ctx, va_ctx, na_rpb)

    lam = (jnp.exp(jnp.sum(lq1.astype(jnp.float32) * lk1.astype(jnp.float32)))
           - jnp.exp(jnp.sum(lq2.astype(jnp.float32) * lk2.astype(jnp.float32))) + lam_init)
    q1, q2 = diff_split(qb)
    k1, k2 = diff_split(kb)
    k1c, k2c = diff_split(kb_c)
    vb_lat, vb_ctx = split_heads(vb, DIFF_HEADS), split_heads(vb_c, DIFF_HEADS)
    q1, q2, k1, k2 = [apply_rope(t, cos_b, sin_b) for t in (q1, q2, k1, k2)]
    k1_all = jnp.concatenate([k1c, k1], axis=2)
    k2_all = jnp.concatenate([k2c, k2], axis=2)
    vb_all = jnp.concatenate([vb_ctx, vb_lat], axis=2)
    out_b = from_blocks(lax.map(lambda qq: diff_attend(qq[0], qq[1], k1_all, k2_all, vb_all, lam),
                                (to_blocks(q1), to_blocks(q2))))
    out_b = rms_norm(out_b, diff_subln) * (1.0 - lam_init)

    q = apply_rope(rms_norm(split_heads(qc, GQA_Q_HEADS), q_norm), cos_c, sin_c)
    k = apply_rope(rms_norm(split_heads(kc, GQA_KV_HEADS), k_norm), cos_c, sin_c)
    kc_ctx = rms_norm(split_heads(kc_c, GQA_KV_HEADS), k_norm)
    vc_ctx = split_heads(vc_c, GQA_KV_HEADS)
    kc_all = jnp.concatenate([kc_ctx, k], axis=2)
    vc_all = jnp.concatenate([vc_ctx, split_heads(vc, GQA_KV_HEADS)], axis=2)
    qg = q.reshape(b, GQA_KV_HEADS, GQA_GROUP, s, HEAD_DIM)
    out_c = from_blocks(lax.map(lambda qq: gqa_attend(qq, kc_all, vc_all), to_blocks(qg)))
    out_c = out_c.reshape(b, GQA_Q_HEADS, s, HEAD_DIM)

    mixed = jnp.concatenate([merge_heads(out_a), merge_heads(out_b), merge_heads(out_c)], axis=-1)
    if not ctx_out:
        return mixed, None

    oa_c = softmax_attend(split_heads(qa_c, NA_HEADS), ka_ctx, va_ctx)
    q1c, q2c = diff_split(qb_c)
    ob_c = rms_norm(diff_attend(q1c, q2c, k1c, k2c, vb_ctx, lam), diff_subln) * (1.0 - lam_init)
    qg_c = rms_norm(split_heads(qc_c, GQA_Q_HEADS), q_norm).reshape(b, GQA_KV_HEADS, GQA_GROUP, n_ctx, HEAD_DIM)
    oc_c = gqa_attend(qg_c, kc_ctx, vc_ctx).reshape(b, GQA_Q_HEADS, n_ctx, HEAD_DIM)
    mixed_c = jnp.concatenate([merge_heads(oa_c), merge_heads(ob_c), merge_heads(oc_c)], axis=-1)
    return mixed, mixed_c


def hier_moe(h, wr_g, br_g, wr_e, br_e, w_gate, w_up, w_down):
    n = h.shape[0]
    g_logits = (h @ wr_g).astype(jnp.float32) + br_g.astype(jnp.float32)
    g_sel = jnp.argmax(g_logits, axis=-1)
    p_grp = jnp.take_along_axis(jax.nn.softmax(g_logits, axis=-1), g_sel[:, None], axis=-1)
    e_logits = ((h @ wr_e).astype(jnp.float32) + br_e.astype(jnp.float32)).reshape(n, N_GROUPS, EXPERTS_PER_GROUP)
    e_in = jnp.take_along_axis(e_logits, g_sel[:, None, None], axis=1)[:, 0]
    top_val, top_idx = lax.top_k(e_in, TOP_K_IN_GROUP)
    w_top = jax.nn.softmax(top_val, axis=-1) * p_grp
    expert_id = g_sel[:, None] * EXPERTS_PER_GROUP + top_idx
    combine = jnp.sum(jax.nn.one_hot(expert_id, N_EXPERTS, dtype=jnp.float32) * w_top[..., None], axis=1)
    y = jnp.zeros_like(h)
    for e in range(N_EXPERTS):
        hid = jax.nn.silu(h @ w_gate[e]) * (h @ w_up[e])
        y = y + (hid @ w_down[e]) * combine[:, e:e + 1].astype(h.dtype)
    return y


def setup_inputs(seed: int = 0) -> dict:
    key = jax.random.key(seed)
    ks = jax.random.split(key, 26)
    f32 = jnp.float32
    D = D_MODEL

    def nrm(k, shape, s):
        return jax.random.normal(k, shape, f32) * s

    return {
        'x': nrm(ks[0], (BATCH, SEQ, D), 1.0),
        'c': nrm(ks[1], (BATCH, D), 1.0),
        'ctx': nrm(ks[2], (BATCH, CTX_LEN, D), 1.0),
        'c_ctx': nrm(ks[3], (D,), 1.0),
        'w_mod': nrm(ks[4], (DEPTH, D, 6 * D), 0.5 * D ** -0.5),
        'b_mod': nrm(ks[5], (DEPTH, 6 * D), 0.02),
        'norm_attn': 1.0 + nrm(ks[6], (DEPTH, D), 0.05),
        'norm_ffn': 1.0 + nrm(ks[7], (DEPTH, D), 0.05),
        'w_in': nrm(ks[8], (DEPTH, D, IN_WIDTH), D ** -0.5),
        'w_out': nrm(ks[9], (DEPTH, MIX_WIDTH, D), MIX_WIDTH ** -0.5),
        'na_rpb': nrm(ks[10], (DEPTH, NA_HEADS, 2 * NA_WIN_H - 1, 2 * NA_WIN_W - 1), 0.02),
        'diff_lambda_q1': nrm(ks[11], (DEPTH, DIFF_QK_DIM), 0.1),
        'diff_lambda_k1': nrm(ks[12], (DEPTH, DIFF_QK_DIM), 0.1),
        'diff_lambda_q2': nrm(ks[13], (DEPTH, DIFF_QK_DIM), 0.1),
        'diff_lambda_k2': nrm(ks[14], (DEPTH, DIFF_QK_DIM), 0.1),
        'diff_subln': 1.0 + nrm(ks[15], (DEPTH, DIFF_V_DIM), 0.05),
        'gqa_q_norm': 1.0 + nrm(ks[16], (DEPTH, HEAD_DIM), 0.05),
        'gqa_k_norm': 1.0 + nrm(ks[17], (DEPTH, HEAD_DIM), 0.05),
        'router_group_w': nrm(ks[18], (DEPTH, D, N_GROUPS), D ** -0.5),
        'router_group_b': nrm(ks[19], (DEPTH, N_GROUPS), 0.01),
        'router_expert_w': nrm(ks[20], (DEPTH, D, N_EXPERTS), D ** -0.5),
        'router_expert_b': nrm(ks[21], (DEPTH, N_EXPERTS), 0.01),
        'w_gate': nrm(ks[22], (DEPTH, N_EXPERTS, D, EXPERT_HIDDEN), D ** -0.5),
        'w_up': nrm(ks[23], (DEPTH, N_EXPERTS, D, EXPERT_HIDDEN), D ** -0.5),
        'w_down': nrm(ks[24], (DEPTH, N_EXPERTS, EXPERT_HIDDEN, D), EXPERT_HIDDEN ** -0.5),
        'final_norm': 1.0 + nrm(ks[25], (D,), 0.05),
    }


def reference(x, c, ctx, c_ctx, w_mod, b_mod, norm_attn, norm_ffn, w_in, w_out, na_rpb,
              diff_lambda_q1, diff_lambda_k1, diff_lambda_q2, diff_lambda_k2, diff_subln,
              gqa_q_norm, gqa_k_norm, router_group_w, router_group_b, router_expert_w,
              router_expert_b, w_gate, w_up, w_down, final_norm):
    b, s, d = x.shape
    n_ctx = ctx.shape[1]
    xc = ctx
    for l in range(DEPTH):
        ctx_out = l < DEPTH - 1
        lam_init = 0.8 - 0.6 * math.exp(-0.3 * l)
        mod = jax.nn.silu(c) @ w_mod[l] + b_mod[l]
        mod_c = jax.nn.silu(c_ctx) @ w_mod[l] + b_mod[l]
        sh1, sc1, g1, sh2, sc2, g2 = jnp.split(mod[:, None, :], 6, axis=-1)
        csh1, csc1, cg1, csh2, csc2, cg2 = jnp.split(mod_c, 6, axis=-1)
        h = rms_norm(x, norm_attn[l]) * (1.0 + sc1) + sh1
        hc = rms_norm(xc, norm_attn[l]) * (1.0 + csc1) + csh1
        mixed, mixed_c = token_mixers(h, hc, w_in[l], na_rpb[l], diff_lambda_q1[l], diff_lambda_k1[l],
                                      diff_lambda_q2[l], diff_lambda_k2[l], diff_subln[l],
                                      gqa_q_norm[l], gqa_k_norm[l], lam_init, ctx_out)
        x = x + g1 * (mixed @ w_out[l])
        tokens = (rms_norm(x, norm_ffn[l]) * (1.0 + sc2) + sh2).reshape(b * s, d)
        if ctx_out:
            xc = xc + cg1 * (mixed_c @ w_out[l])
            hfc = rms_norm(xc, norm_ffn[l]) * (1.0 + csc2) + csh2
            tokens = jnp.concatenate([tokens, hfc.reshape(b * n_ctx, d)], axis=0)
        y = hier_moe(tokens, router_group_w[l], router_group_b[l], router_expert_w[l], router_expert_b[l],
                     w_gate[l], w_up[l], w_down[l])
        x = x + g2 * y[:b * s].reshape(b, s, d)
        if ctx_out:
            xc = xc + cg2 * y[b * s:].reshape(b, n_ctx, d)
    return rms_norm(x, final_norm)
```

```python
import functools
import math

import numpy as np
import jax
import jax.numpy as jnp
from jax import lax
from jax.experimental import pallas as pl
from jax.experimental.pallas import tpu as pltpu

F32 = jnp.float32
BF16 = jnp.bfloat16

D_MODEL = 1024
DEPTH = 2
GRID_W = 64
CTX_LEN = 256
HEAD_DIM = 64
NA_HEADS = 6
NA_WIN_H = 8
NA_WIN_W = 16
DIFF_HEADS = 4
DIFF_QK_DIM = 32
GQA_Q_HEADS = 6
GQA_KV_HEADS = 2
N_GROUPS = 4
EXPERTS_PER_GROUP = 4
N_EXPERTS = 16
EXPERT_HIDDEN = 512
ROPE_THETA = 10000.0
EPS = 1e-6
W_A = NA_HEADS * HEAD_DIM
W_B = DIFF_HEADS * 2 * DIFF_QK_DIM
W_C = GQA_Q_HEADS * HEAD_DIM
W_KC = GQA_KV_HEADS * HEAD_DIM
IN_WIDTH = 3 * W_A + 3 * W_B + W_C + 2 * W_KC
ROT_WIDTH = 2 * W_B + W_C + W_KC
EXT_WIDTH = IN_WIDTH + ROT_WIDTH

LANES = 128
TILE = CTX_LEN
NA_QROWS = 8
NA_KROWS = 16
NEG = -1e30
LOG2E = 1.4426950408889634
VMEM_LIMIT = 56 * 1024 * 1024

GQA_Q_ORDER = (0, 3, 1, 4, 2, 5)


def _cparams(n_axes):
    return pltpu.CompilerParams(dimension_semantics=("arbitrary",) * n_axes,
                                vmem_limit_bytes=VMEM_LIMIT)


def _split_bf16(a):
    hi = a.astype(BF16)
    lo = (a - hi.astype(F32)).astype(BF16)
    return hi, lo


def _dot(a, b):
    return jnp.dot(a, b, preferred_element_type=F32)


def _dot_nt(a, b):
    return lax.dot_general(a, b, (((1,), (1,)), ((), ())), preferred_element_type=F32)


def _mod_kernel(c_ref, w_ref, b_ref, o_ref):
    c = c_ref[...]
    a = c * jax.nn.sigmoid(c)
    a_hi, a_lo = _split_bf16(a)
    w_hi, w_lo = _split_bf16(w_ref[...])
    o_ref[...] = _dot(a_hi, w_hi) + _dot(a_lo, w_hi) + _dot(a_hi, w_lo) + b_ref[...]


def _modulation(c_rows, w_mod, b_mod):
    depth, d, n = w_mod.shape
    bn = 1536
    return pl.pallas_call(
        _mod_kernel,
        out_shape=jax.ShapeDtypeStruct((depth, 8, n), F32),
        grid=(depth, n // bn),
        in_specs=[pl.BlockSpec((8, d), lambda l, j: (0, 0)),
                  pl.BlockSpec((None, d, bn), lambda l, j: (l, 0, j)),
                  pl.BlockSpec((None, 1, bn), lambda l, j: (l, 0, j))],
        out_specs=pl.BlockSpec((None, 8, bn), lambda l, j: (l, 0, j)),
        compiler_params=_cparams(2),
        name="adaln_mod",
    )(c_rows, w_mod, b_mod.reshape(depth, 1, n))


def _head_mean_sq(t, ones):
    hi, lo = _split_bf16(t * t)
    return (_dot(hi, ones) + _dot(lo, ones)) * (1.0 / HEAD_DIM)


def _inproj_kernel(x_ref, mod_ref, gain_ref, w_ref, tab_ref, gq_ref, gk_ref, ones_ref,
                   qa_ref, ka_ref, va_ref, qb_ref, kb_ref, vb_ref, qc_ref, kc_ref, vc_ref):
    x = x_ref[...]
    mod = mod_ref[...]
    ms = jnp.mean(x * x, axis=-1, keepdims=True)
    h = (x * lax.rsqrt(ms + EPS)) * gain_ref[...]
    h = h * (1.0 + mod[1:2]) + mod[0:1]
    hb = h.astype(BF16)

    def proj(a, b):
        return _dot(hb, w_ref[:, a:b])

    pa = proj(0, 3 * W_A)
    qa_ref[...] = (pa[:, :W_A] * (HEAD_DIM ** -0.5)).astype(BF16)
    ka_ref[...] = pa[:, W_A:2 * W_A].astype(BF16)
    va_ref[...] = pa[:, 2 * W_A:].astype(BF16)

    tab = tab_ref[...]
    cos_b = jnp.concatenate([tab[:, 0:LANES]] * 2, axis=1)
    sin_b = jnp.concatenate([tab[:, LANES:2 * LANES]] * 2, axis=1)
    cos_c1 = tab[:, 2 * LANES:3 * LANES]
    sin_c1 = tab[:, 3 * LANES:4 * LANES]
    cos_c = jnp.concatenate([cos_c1] * 3, axis=1)
    sin_c = jnp.concatenate([sin_c1] * 3, axis=1)

    o_b = 3 * W_A
    pb = proj(o_b, o_b + 3 * W_B)
    pbr = proj(IN_WIDTH, IN_WIDTH + 2 * W_B)
    qb = pb[:, :W_B] * cos_b + pbr[:, :W_B] * sin_b
    qb_ref[...] = (qb * (DIFF_QK_DIM ** -0.5 * LOG2E)).astype(BF16)
    kb_ref[...] = (pb[:, W_B:2 * W_B] * cos_b + pbr[:, W_B:] * sin_b).astype(BF16)
    vb_ref[...] = pb[:, 2 * W_B:].astype(BF16)

    o_c = o_b + 3 * W_B
    pc = proj(o_c, IN_WIDTH)
    pcr = proj(IN_WIDTH + 2 * W_B, EXT_WIDTH)
    ones = ones_ref[...]
    qc = pc[:, :W_C]
    kc = pc[:, W_C:W_C + W_KC]
    nq = lax.rsqrt(_head_mean_sq(qc, ones) + EPS)
    nk = lax.rsqrt(_head_mean_sq(kc, ones[:W_KC, :W_KC]) + EPS)
    gq = gq_ref[...]
    gk = gk_ref[...]
    q = nq * (qc * gq[0:1] * cos_c + pcr[:, :W_C] * gq[1:2] * sin_c)
    qc_ref[...] = (q * (HEAD_DIM ** -0.5 * LOG2E)).astype(BF16)
    k = nk * (kc * gk[0:1] * cos_c1 + pcr[:, W_C:] * gk[1:2] * sin_c1)
    kc_ref[...] = k.astype(BF16)
    vc_ref[...] = pc[:, W_C + W_KC:].astype(BF16)


def _in_projection(xs, modsel, gain, w_ext, tab, gq, gk, ones, n_lat_tiles):
    b, t_all, d = xs.shape
    n_tiles = t_all // TILE
    widths = (W_A, W_A, W_A, W_B, W_B, W_B, W_C, W_KC, W_KC)
    tok = lambda bi, ti: (bi, ti, 0)
    const2 = lambda bi, ti: (0, 0)
    return pl.pallas_call(
        _inproj_kernel,
        out_shape=[jax.ShapeDtypeStruct((b, t_all, w), BF16) for w in widths],
        grid=(b, n_tiles),
        in_specs=[pl.BlockSpec((None, TILE, d), tok),
                  pl.BlockSpec((None, 6, d), lambda bi, ti: (2 * bi + (ti >= n_lat_tiles).astype(jnp.int32), 0, 0)),
                  pl.BlockSpec((1, d), const2),
                  pl.BlockSpec((d, EXT_WIDTH), const2),
                  pl.BlockSpec((TILE, 4 * LANES), lambda bi, ti: (ti, 0)),
                  pl.BlockSpec((2, W_C), const2),
                  pl.BlockSpec((2, W_KC), const2),
                  pl.BlockSpec((W_C, W_C), const2)],
        out_specs=[pl.BlockSpec((None, TILE, w), tok) for w in widths],
        compiler_params=_cparams(2),
        name="in_projection",
    )(xs, modsel, gain, w_ext, tab, gq, gk, ones)


def _flash_kernel(q_ref, k_ref, v_ref, aux_ref, o_ref, va_ref, vb_ref, qs_ref, acc_ref, m_ref, *,
                  n_qblk, n_sub, tk, n_lat_blocks, ctx_tile, qt_off, mode, use_exp2, lam_init):
    exp_fn = jnp.exp2 if use_exp2 else jnp.exp
    qt = pl.program_id(2) + qt_off
    sub_w = LANES // n_sub
    half = LANES // 2
    lane = lax.broadcasted_iota(jnp.int32, (1, LANES), 1)
    lower = lane < half
    n_pieces = n_qblk * n_sub
    ma = (n_pieces // 2) * TILE
    ctx_start = n_lat_blocks * tk

    @pl.when(pl.program_id(2) == 0)
    def _():
        v = v_ref[...].astype(F32)
        va_ref[...] = jnp.where(lower, v, 1.0).astype(BF16)
        vb_ref[...] = jnp.where(lower, 1.0, v).astype(BF16)

    ia, ib = 0, n_pieces // 2
    for blk in range(n_qblk):
        qf = q_ref[:, blk * LANES:(blk + 1) * LANES].astype(F32)
        for sub in range(n_sub):
            msk = (lane >= sub * sub_w) & (lane < (sub + 1) * sub_w)
            piece = jnp.where(msk, qf, 0.0).astype(BF16)
            if sub * sub_w < half:
                qs_ref[ia * TILE:(ia + 1) * TILE, :] = piece
                ia += 1
            else:
                qs_ref[ib * TILE:(ib + 1) * TILE, :] = piece
                ib += 1

    def block(start, size, first):
        s = _dot_nt(qs_ref[...], k_ref[pl.ds(start, size), :])
        mb = jnp.max(s, axis=-1, keepdims=True)
        if first:
            m_new = mb
        else:
            m_old = m_ref[...]
            m_new = jnp.maximum(m_old, mb)
        p = exp_fn(s - m_new).astype(BF16)
        pva = _dot(p[:ma], va_ref[pl.ds(start, size), :])
        pvb = _dot(p[ma:], vb_ref[pl.ds(start, size), :])
        if first:
            acc_ref[:ma, :] = pva
            acc_ref[ma:, :] = pvb
        else:
            alpha = exp_fn(m_old - m_new)
            acc_ref[:ma, :] = alpha[:ma] * acc_ref[:ma, :] + pva
            acc_ref[ma:, :] = alpha[ma:] * acc_ref[ma:, :] + pvb
        m_ref[...] = m_new

    block(ctx_start, CTX_LEN, True)

    def body(j, carry):
        block(pl.multiple_of(j * tk, tk), tk, False)
        return carry

    if ctx_tile is None:
        n_blocks = n_lat_blocks
    else:
        n_blocks = jnp.where(qt == ctx_tile, 0, n_lat_blocks)
    lax.fori_loop(0, n_blocks, body, 0)

    acc = acc_ref[...]
    r = acc / pltpu.roll(acc, half, 1)
    ra, rb = r[:ma], r[ma:]
    if mode == "plain":
        for i in range(n_pieces // 2):
            o = jnp.where(lower, ra[i * TILE:(i + 1) * TILE], rb[i * TILE:(i + 1) * TILE])
            o_ref[:, i * LANES:(i + 1) * LANES] = o.astype(BF16)
    else:
        aux = aux_ref[...]
        l1 = jnp.sum(aux[0:1] * aux[1:2], axis=-1, keepdims=True)
        l2 = jnp.sum(aux[2:3] * aux[3:4], axis=-1, keepdims=True)
        lam = jnp.exp(l1) - jnp.exp(l2) + lam_init
        oa = ra[:TILE] - lam * ra[TILE:]
        ob = rb[:TILE] - lam * rb[TILE:]
        o = jnp.where(lower, oa, ob)
        sq = o * o
        ss_a = jnp.sum(jnp.where(lower, sq, 0.0), axis=-1, keepdims=True)
        ss_b = jnp.sum(jnp.where(lower, 0.0, sq), axis=-1, keepdims=True)
        ms = jnp.where(lower, ss_a, ss_b) * (1.0 / HEAD_DIM)
        o = (o * lax.rsqrt(ms + EPS)) * aux[4:5]
        o_ref[...] = (o * (1.0 - lam_init)).astype(BF16)


def _flash(q, k, v, aux, *, n_qblk, n_sub, n_hp, qt_off, n_qt, n_lat, ctx_tile, mode, use_exp2,
           lam_init=0.0):
    b, t_all, _ = q.shape
    qw = n_qblk * LANES
    tk = 512
    m_rows = n_qblk * n_sub * TILE
    kern = functools.partial(_flash_kernel, n_qblk=n_qblk, n_sub=n_sub, tk=tk,
                             n_lat_blocks=n_lat // tk, ctx_tile=ctx_tile, qt_off=qt_off,
                             mode=mode, use_exp2=use_exp2, lam_init=lam_init)
    return pl.pallas_call(
        kern,
        out_shape=jax.ShapeDtypeStruct((b, n_qt * TILE, n_hp * qw), BF16),
        grid=(b, n_hp, n_qt),
        in_specs=[pl.BlockSpec((None, TILE, qw), lambda bi, hp, qt: (bi, qt + qt_off, hp)),
                  pl.BlockSpec((None, t_all, LANES), lambda bi, hp, qt: (bi, 0, hp)),
                  pl.BlockSpec((None, t_all, LANES), lambda bi, hp, qt: (bi, 0, hp)),
                  pl.BlockSpec((8, LANES), lambda bi, hp, qt: (0, 0))],
        out_specs=pl.BlockSpec((None, TILE, qw), lambda bi, hp, qt: (bi, qt, hp)),
        scratch_shapes=[pltpu.VMEM((t_all, LANES), BF16),
                        pltpu.VMEM((t_all, LANES), BF16),
                        pltpu.VMEM((m_rows, LANES), BF16),
                        pltpu.VMEM((m_rows, LANES), F32),
                        pltpu.VMEM((m_rows, 1), F32)],
        compiler_params=_cparams(3),
        name="flash_" + mode,
    )(q, k, v, aux)


def _na_kernel(q_ref, k0, k1, k2, k3, v0, v1, v2, v3, kc_ref, vc_ref, bias_ref, o_ref):
    lane = lax.broadcasted_iota(jnp.int32, (1, LANES), 1)
    lower = lane < LANES // 2
    qf = q_ref[...].astype(F32)
    kw = jnp.concatenate([k0[...], k1[...], k2[...], k3[...]], axis=0)
    vw = jnp.concatenate([v0[...], v1[...], v2[...], v3[...]], axis=0)
    kc = kc_ref[...]
    vc = vc_ref[...]
    outs = []
    for hh in range(2):
        msk = lower if hh == 0 else jnp.logical_not(lower)
        qh = jnp.where(msk, qf, 0.0).astype(BF16)
        s_w = _dot_nt(qh, kw) + bias_ref[hh]
        s_c = _dot_nt(qh, kc)
        m = jnp.maximum(jnp.max(s_w, axis=-1, keepdims=True), jnp.max(s_c, axis=-1, keepdims=True))
        p_w = jnp.exp(s_w - m)
        p_c = jnp.exp(s_c - m)
        l = jnp.sum(p_w, axis=-1, keepdims=True) + jnp.sum(p_c, axis=-1, keepdims=True)
        o = _dot(p_w.astype(BF16), vw) + _dot(p_c.astype(BF16), vc)
        outs.append(o / l)
    o_ref[...] = jnp.where(lower, outs[0], outs[1]).astype(BF16)


def _neighbourhood_attention(qa, ka, va, bias, n_lat):
    b = qa.shape[0]
    q_tok = NA_QROWS * GRID_W
    v_tok = q_tok // 2
    n_rb = n_lat // q_tok
    n_view = n_lat // v_tok
    ctx_blk = n_lat // v_tok

    def view(j):
        return lambda rb, hp, bi: (bi, jnp.clip(2 * rb - 1 + j, 0, n_view - 1), hp)

    kv_specs = [pl.BlockSpec((None, v_tok, LANES), view(j)) for j in range(4)]
    ctx_spec = pl.BlockSpec((None, CTX_LEN, LANES), lambda rb, hp, bi: (bi, ctx_blk, hp))

    def bias_map(rb, hp, bi):
        pat = jnp.where(rb == 0, 0, jnp.where(rb == n_rb - 1, 2, 1))
        return (pat, hp, 0, 0)

    return pl.pallas_call(
        _na_kernel,
        out_shape=jax.ShapeDtypeStruct((b, n_lat, W_A), BF16),
        grid=(n_rb, NA_HEADS // 2, b),
        in_specs=[pl.BlockSpec((None, q_tok, LANES), lambda rb, hp, bi: (bi, rb, hp))]
                 + kv_specs + kv_specs + [ctx_spec, ctx_spec,
                 pl.BlockSpec((None, 2, q_tok, NA_KROWS * GRID_W), bias_map)],
        out_specs=pl.BlockSpec((None, q_tok, LANES), lambda rb, hp, bi: (bi, rb, hp)),
        compiler_params=_cparams(3),
        name="neighbourhood_attention",
    )(qa, ka, ka, ka, ka, va, va, va, va, ka, va, bias)


def _na_bias_table(rpb, rows):
    cols = np.arange(GRID_W)
    c0 = np.clip(cols - NA_WIN_W // 2, 0, GRID_W - NA_WIN_W)
    cc = cols[None, :]
    col_ok = (cc >= c0[:, None]) & (cc < c0[:, None] + NA_WIN_W)
    dc = np.clip(cc - cols[:, None] + (NA_WIN_W - 1), 0, 2 * NA_WIN_W - 2)
    e = jnp.where(col_ok[None, None], rpb.astype(F32)[:, :, dc], NEG)
    e = jnp.concatenate([e, jnp.full_like(e[:, :1], NEG)], axis=1)
    a = np.arange(NA_QROWS)[:, None]
    i = np.arange(NA_KROWS)[None, :]
    pats = []
    for r_base in (0, NA_QROWS, rows - NA_QROWS):
        r = r_base + a
        key_row = r_base - NA_WIN_H // 2 + i
        r0 = np.clip(r - NA_WIN_H // 2, 0, rows - NA_WIN_H)
        ok = (key_row >= r0) & (key_row < r0 + NA_WIN_H) & (key_row >= 0) & (key_row < rows)
        dr = np.where(ok, key_row - r + (NA_WIN_H - 1), 2 * NA_WIN_H - 1)
        pats.append(dr)
    dr_all = np.stack(pats)
    t = e[:, dr_all]
    t = t.transpose(1, 0, 2, 4, 3, 5)
    return t.reshape(3, NA_HEADS, NA_QROWS * GRID_W, NA_KROWS * GRID_W)


def _outproj_kernel(x_ref, oa_ref, ob_ref, oc_ref, wa_ref, wb_ref, wc_ref, mod_ref, gain_ref,
                    wrh_ref, wrl_ref, br_ref, x1_ref, tok_ref, comb_ref):
    mod = mod_ref[...]
    y = _dot(oa_ref[...], wa_ref[...]) + _dot(ob_ref[...], wb_ref[...]) + _dot(oc_ref[...], wc_ref[...])
    x1 = x_ref[...] + mod[2:3] * y
    x1_ref[...] = x1
    ms = jnp.mean(x1 * x1, axis=-1, keepdims=True)
    t = (x1 * lax.rsqrt(ms + EPS)) * gain_ref[...]
    t = t * (1.0 + mod[4:5]) + mod[3:4]
    tok_ref[...] = t.astype(BF16)

    t_hi, t_lo = _split_bf16(t)
    wrh = wrh_ref[...]
    logits = _dot(t_hi, wrh) + _dot(t_lo, wrh) + _dot(t_hi, wrl_ref[...]) + br_ref[...]

    lane = lax.broadcasted_iota(jnp.int32, logits.shape, 1)
    lane_f = lane.astype(F32)
    is_g = lane < N_GROUPS
    gl = jnp.where(is_g, logits, NEG)
    gmax = jnp.max(gl, axis=-1, keepdims=True)
    g_sel = jnp.min(jnp.where(gl == gmax, lane_f, 1e9), axis=-1, keepdims=True)
    p_grp = 1.0 / jnp.sum(jnp.where(is_g, jnp.exp(gl - gmax), 0.0), axis=-1, keepdims=True)
    grp_of_lane = lax.shift_right_arithmetic(lane - N_GROUPS, 2).astype(F32)
    in_grp = (lane >= N_GROUPS) & (lane < N_GROUPS + N_EXPERTS) & (grp_of_lane == g_sel)
    el = jnp.where(in_grp, logits, NEG)
    v1 = jnp.max(el, axis=-1, keepdims=True)
    i1 = jnp.min(jnp.where(el == v1, lane_f, 1e9), axis=-1, keepdims=True)
    el2 = jnp.where(lane_f == i1, NEG, el)
    v2 = jnp.max(el2, axis=-1, keepdims=True)
    i2 = jnp.min(jnp.where(el2 == v2, lane_f, 1e9), axis=-1, keepdims=True)
    e2 = jnp.exp(v2 - v1)
    den = 1.0 + e2
    w1 = p_grp / den
    w2 = p_grp * e2 / den
    comb_ref[...] = jnp.where(lane_f == i1, w1, 0.0) + jnp.where(lane_f == i2, w2, 0.0)


def _out_projection(xs, oa, ob, oc, wa, wb, wc, modsel, gain, wrh, wrl, br, n_tiles, n_lat_tiles):
    b, _, d = xs.shape
    tok = lambda bi, ti: (bi, ti, 0)
    const2 = lambda bi, ti: (0, 0)
    rows = n_tiles * TILE
    return pl.pallas_call(
        _outproj_kernel,
        out_shape=[jax.ShapeDtypeStruct((b, rows, d), F32),
                   jax.ShapeDtypeStruct((b, rows, d), BF16),
                   jax.ShapeDtypeStruct((b, rows, LANES), F32)],
        grid=(b, n_tiles),
        in_specs=[pl.BlockSpec((None, TILE, d), tok),
                  pl.BlockSpec((None, TILE, W_A), tok),
                  pl.BlockSpec((None, TILE, W_B), tok),
                  pl.BlockSpec((None, TILE, W_C), tok),
                  pl.BlockSpec((W_A, d), const2),
                  pl.BlockSpec((W_B, d), const2),
                  pl.BlockSpec((W_C, d), const2),
                  pl.BlockSpec((None, 6, d), lambda bi, ti: (2 * bi + (ti >= n_lat_tiles).astype(jnp.int32), 0, 0)),
                  pl.BlockSpec((1, d), const2),
                  pl.BlockSpec((d, LANES), const2),
                  pl.BlockSpec((d, LANES), const2),
                  pl.BlockSpec((1, LANES), const2)],
        out_specs=[pl.BlockSpec((None, TILE, d), tok),
                   pl.BlockSpec((None, TILE, d), tok),
                   pl.BlockSpec((None, TILE, LANES), tok)],
        compiler_params=_cparams(2),
        name="out_projection",
    )(xs, oa, ob, oc, wa, wb, wc, modsel, gain, wrh, wrl, br)


def _moe_kernel(tok_ref, comb_ref, x1_ref, wg_ref, wu_ref, wd_ref, mod_ref, fgain_ref, o_ref, acc_ref,
                *, final):
    e = pl.program_id(2)
    t = tok_ref[...]
    hid = jax.nn.silu(_dot(t, wg_ref[...])) * _dot(t, wu_ref[...])
    y = _dot(hid.astype(BF16), wd_ref[...])
    comb = comb_ref[...]
    lane = lax.broadcasted_iota(jnp.int32, comb.shape, 1)
    c = jnp.sum(jnp.where(lane == e + N_GROUPS, comb, 0.0), axis=-1, keepdims=True)

    @pl.when(e == 0)
    def _():
        acc_ref[...] = y * c

    @pl.when(e > 0)
    def _():
        acc_ref[...] += y * c

    @pl.when(e == N_EXPERTS - 1)
    def _():
        x2 = x1_ref[...] + mod_ref[5:6, :] * acc_ref[...]
        if final:
            ms = jnp.mean(x2 * x2, axis=-1, keepdims=True)
            x2 = (x2 * lax.rsqrt(ms + EPS)) * fgain_ref[...]
        o_ref[...] = x2


def _moe(tok, comb, x1, wg, wu, wd, modsel, fgain, *, tm, tile_off, n_t, is_ctx, final):
    b, _, d = x1.shape
    tokm = lambda bi, ti, e: (bi, ti + tile_off, 0)
    return pl.pallas_call(
        functools.partial(_moe_kernel, final=final),
        out_shape=jax.ShapeDtypeStruct((b, n_t * tm, d), F32),
        grid=(b, n_t, N_EXPERTS),
        in_specs=[pl.BlockSpec((None, tm, d), tokm),
                  pl.BlockSpec((None, tm, LANES), tokm),
                  pl.BlockSpec((None, tm, d), tokm),
                  pl.BlockSpec((None, d, EXPERT_HIDDEN), lambda bi, ti, e: (e, 0, 0)),
                  pl.BlockSpec((None, d, EXPERT_HIDDEN), lambda bi, ti, e: (e, 0, 0)),
                  pl.BlockSpec((None, EXPERT_HIDDEN, d), lambda bi, ti, e: (e, 0, 0)),
                  pl.BlockSpec((None, 6, d), lambda bi, ti, e: (2 * bi + is_ctx, 0, 0)),
                  pl.BlockSpec((1, d), lambda bi, ti, e: (0, 0))],
        out_specs=pl.BlockSpec((None, tm, d), lambda bi, ti, e: (bi, ti, 0)),
        scratch_shapes=[pltpu.VMEM((tm, d), F32)],
        compiler_params=_cparams(3),
        name="moe_dense",
    )(tok, comb, x1, wg, wu, wd, modsel, fgain)


def _rot_perm(width, head):
    i = np.arange(width)
    half = head // 2
    first = (i % head) < half
    idx = np.where(first, i + half, i - half)
    sign = np.where(first, -1.0, 1.0).astype(np.float32)
    return idx, sign


def _rope_tables(n_lat):
    t = jnp.arange(n_lat)
    row = (t // GRID_W).astype(F32)
    col = (t % GRID_W).astype(F32)

    def cs(dim):
        quarter = dim // 4
        freqs = ROPE_THETA ** (-jnp.arange(quarter, dtype=F32) / quarter)
        ang = jnp.concatenate([row[:, None] * freqs, col[:, None] * freqs], axis=-1)
        cos = jnp.tile(jnp.cos(ang), (1, 2 * LANES // dim))
        sin = jnp.tile(jnp.sin(ang), (1, 2 * LANES // dim))
        cos = jnp.concatenate([cos, jnp.ones((CTX_LEN, LANES), F32)], axis=0)
        sin = jnp.concatenate([sin, jnp.zeros((CTX_LEN, LANES), F32)], axis=0)
        return cos, sin

    cos_b, sin_b = cs(DIFF_QK_DIM)
    cos_c, sin_c = cs(HEAD_DIM)
    return jnp.concatenate([cos_b, sin_b, cos_c, sin_c], axis=1)


def _extended_w_in(w_in):
    o_b = 3 * W_A
    o_c = o_b + 3 * W_B
    qc_cols = o_c + np.concatenate([h * HEAD_DIM + np.arange(HEAD_DIM) for h in GQA_Q_ORDER])
    cols = np.concatenate([np.arange(o_c), qc_cols, np.arange(o_c + W_C, IN_WIDTH)])
    w = w_in[:, cols]
    idx_b, sign_b = _rot_perm(W_B, DIFF_QK_DIM)
    idx_c, sign_c = _rot_perm(W_C, HEAD_DIM)
    idx_k, sign_k = _rot_perm(W_KC, HEAD_DIM)
    rot = jnp.concatenate([
        w[:, o_b + idx_b] * sign_b, w[:, o_b + W_B + idx_b] * sign_b,
        w[:, o_c + idx_c] * sign_c, w[:, o_c + W_C + idx_k] * sign_k], axis=1)
    return jnp.concatenate([w, rot], axis=1).astype(BF16)


def kernel(x, c, ctx, c_ctx, w_mod, b_mod, norm_attn, norm_ffn, w_in, w_out, na_rpb, diff_lambda_q1, diff_lambda_k1, diff_lambda_q2, diff_lambda_k2, diff_subln, gqa_q_norm, gqa_k_norm, router_group_w, router_group_b, router_expert_w, router_expert_b, w_gate, w_up, w_down, final_norm):
    b, s, d = x.shape
    assert d == D_MODEL and ctx.shape[1] == CTX_LEN and s % (NA_QROWS * GRID_W) == 0
    rows = s // GRID_W
    assert rows >= 2 * NA_QROWS
    t_all = s + CTX_LEN
    n_lat_tiles = s // TILE

    c_rows = jnp.zeros((8, d), F32).at[:b].set(c).at[b].set(c_ctx)
    mod = _modulation(c_rows, w_mod, b_mod)

    tab = _rope_tables(s)
    hidx = np.arange(HEAD_DIM)
    partner = np.where(hidx < HEAD_DIM // 2, hidx + HEAD_DIM // 2, hidx - HEAD_DIM // 2)
    blk = np.arange(W_C) // HEAD_DIM
    ones = jnp.asarray((blk[:, None] == blk[None, :]).astype(np.float32), BF16)
    oc_rows = W_A + W_B + np.concatenate([h * HEAD_DIM + np.arange(HEAD_DIM) for h in GQA_Q_ORDER])
    dummy_aux = jnp.zeros((8, LANES), F32)

    xs = jnp.concatenate([x, ctx], axis=1)
    for l in range(DEPTH):
        ctx_out = l < DEPTH - 1
        lam_init = 0.8 - 0.6 * math.exp(-0.3 * l)
        m_lat = mod[l, :b].reshape(b, 1, 6, d)
        m_ctx = jnp.broadcast_to(mod[l, b].reshape(1, 1, 6, d), (b, 1, 6, d))
        modsel = jnp.concatenate([m_lat, m_ctx], axis=1).reshape(2 * b, 6, d)

        gq = jnp.stack([jnp.tile(gqa_q_norm[l], GQA_Q_HEADS), jnp.tile(gqa_q_norm[l][partner], GQA_Q_HEADS)])
        gk = jnp.stack([jnp.tile(gqa_k_norm[l], GQA_KV_HEADS), jnp.tile(gqa_k_norm[l][partner], GQA_KV_HEADS)])
        qa, ka, va, qb, kb, vb, qc, kc, vc = _in_projection(
            xs, modsel, norm_attn[l][None], _extended_w_in(w_in[l]), tab, gq, gk, ones, n_lat_tiles)

        n_qt = n_lat_tiles + 1 if ctx_out else n_lat_tiles
        ctx_tile = n_lat_tiles if ctx_out else None
        oa = _neighbourhood_attention(qa, ka, va, _na_bias_table(na_rpb[l], rows), s)
        if ctx_out:
            oa_ctx = _flash(qa, ka, va, dummy_aux, n_qblk=1, n_sub=2, n_hp=NA_HEADS // 2,
                            qt_off=n_lat_tiles, n_qt=1, n_lat=s, ctx_tile=n_lat_tiles,
                            mode="plain", use_exp2=False)
            oa = jnp.concatenate([oa, oa_ctx], axis=1)
        pad = lambda v: jnp.pad(v, (0, LANES - v.shape[0]))
        aux = jnp.stack([pad(diff_lambda_q1[l]), pad(diff_lambda_k1[l]), pad(diff_lambda_q2[l]),
                         pad(diff_lambda_k2[l]), jnp.tile(diff_subln[l], 2),
                         jnp.zeros((LANES,), F32), jnp.zeros((LANES,), F32), jnp.zeros((LANES,), F32)])
        ob = _flash(qb, kb, vb, aux, n_qblk=1, n_sub=4, n_hp=DIFF_HEADS // 2, qt_off=0, n_qt=n_qt,
                    n_lat=s, ctx_tile=ctx_tile, mode="diff", use_exp2=True, lam_init=lam_init)
        oc = _flash(qc, kc, vc, dummy_aux, n_qblk=3, n_sub=2, n_hp=1, qt_off=0, n_qt=n_qt,
                    n_lat=s, ctx_tile=ctx_tile, mode="plain", use_exp2=True)

        w_o = w_out[l]
        wr = jnp.zeros((d, LANES), F32)
        wr = wr.at[:, :N_GROUPS].set(router_group_w[l]).at[:, N_GROUPS:N_GROUPS + N_EXPERTS].set(router_expert_w[l])
        wrh, wrl = _split_bf16(wr)
        br = jnp.zeros((1, LANES), F32)
        br = br.at[0, :N_GROUPS].set(router_group_b[l]).at[0, N_GROUPS:N_GROUPS + N_EXPERTS].set(router_expert_b[l])
        x1, tok, comb = _out_projection(
            xs, oa, ob, oc, w_o[:W_A].astype(BF16), w_o[W_A:W_A + W_B].astype(BF16), w_o[oc_rows].astype(BF16),
            modsel, norm_ffn[l][None], wrh, wrl, br, n_qt, n_lat_tiles)

        tm = next(t for t in (1024, 512) if s % t == 0)
        experts = (w_gate[l].astype(BF16), w_up[l].astype(BF16), w_down[l].astype(BF16))
        x_lat = _moe(tok, comb, x1, *experts, modsel, final_norm[None], tm=tm, tile_off=0,
                     n_t=s // tm, is_ctx=0, final=not ctx_out)
        if not ctx_out:
            return x_lat
        x_ctx = _moe(tok, comb, x1, *experts, modsel, final_norm[None], tm=TILE, tile_off=n_lat_tiles,
                     n_t=1, is_ctx=1, final=False)
        xs = jnp.concatenate([x_lat, x_ctx], axis=1)
```

```python
import functools
import math

import numpy as np
import jax
import jax.numpy as jnp
from jax import lax
from jax.experimental import pallas as pl
from jax.experimental.pallas import tpu as pltpu

F32 = jnp.float32
BF16 = jnp.bfloat16

D_MODEL = 1024
DEPTH = 2
GRID_W = 64
CTX_LEN = 256
HEAD_DIM = 64
NA_HEADS = 6
NA_WIN_H = 8
NA_WIN_W = 16
DIFF_HEADS = 4
DIFF_QK_DIM = 32
GQA_Q_HEADS = 6
GQA_KV_HEADS = 2
N_GROUPS = 4
EXPERTS_PER_GROUP = 4
N_EXPERTS = 16
EXPERT_HIDDEN = 512
ROPE_THETA = 10000.0
EPS = 1e-6
W_A = NA_HEADS * HEAD_DIM
W_B = DIFF_HEADS * 2 * DIFF_QK_DIM
W_C = GQA_Q_HEADS * HEAD_DIM
W_KC = GQA_KV_HEADS * HEAD_DIM
IN_WIDTH = 3 * W_A + 3 * W_B + W_C + 2 * W_KC
ROT_WIDTH = 2 * W_B + W_C + W_KC
EXT_WIDTH = IN_WIDTH + ROT_WIDTH

LANES = 128
TILE = CTX_LEN
NA_QROWS = 8
NA_KROWS = 16
NEG = -1e30
LOG2E = 1.4426950408889634
VMEM_LIMIT = 56 * 1024 * 1024

GQA_Q_ORDER = (0, 3, 1, 4, 2, 5)


def _cparams(n_axes):
    return pltpu.CompilerParams(dimension_semantics=("arbitrary",) * n_axes,
                                vmem_limit_bytes=VMEM_LIMIT)


def _split_bf16(a):
    hi = a.astype(BF16)
    lo = (a - hi.astype(F32)).astype(BF16)
    return hi, lo


def _dot(a, b):
    return jnp.dot(a, b, preferred_element_type=F32)


def _dot_nt(a, b):
    return lax.dot_general(a, b, (((1,), (1,)), ((), ())), preferred_element_type=F32)


def _mod_kernel(c_ref, w_ref, b_ref, o_ref):
    c = c_ref[...]
    a = c * jax.nn.sigmoid(c)
    a_hi, a_lo = _split_bf16(a)
    w_hi, w_lo = _split_bf16(w_ref[...])
    o_ref[...] = _dot(a_hi, w_hi) + _dot(a_lo, w_hi) + _dot(a_hi, w_lo) + b_ref[...]


def _modulation(c_rows, w_mod, b_mod):
    depth, d, n = w_mod.shape
    bn = 1536
    return pl.pallas_call(
        _mod_kernel,
        out_shape=jax.ShapeDtypeStruct((depth, 8, n), F32),
        grid=(depth, n // bn),
        in_specs=[pl.BlockSpec((8, d), lambda l, j: (0, 0)),
                  pl.BlockSpec((None, d, bn), lambda l, j: (l, 0, j)),
                  pl.BlockSpec((None, 1, bn), lambda l, j: (l, 0, j))],
        out_specs=pl.BlockSpec((None, 8, bn), lambda l, j: (l, 0, j)),
        compiler_params=_cparams(2),
        name="adaln_mod",
    )(c_rows, w_mod, b_mod.reshape(depth, 1, n))


def _head_mean_sq(t, ones):
    hi, lo = _split_bf16(t * t)
    return (_dot(hi, ones) + _dot(lo, ones)) * (1.0 / HEAD_DIM)


def _inproj_kernel(x_ref, mod_ref, gain_ref, w_ref, tab_ref, gq_ref, gk_ref, ones_ref,
                   qa_ref, ka_ref, va_ref, qb_ref, kb_ref, vb_ref, qc_ref, kc_ref, vc_ref):
    x = x_ref[...]
    mod = mod_ref[...]
    ms = jnp.mean(x * x, axis=-1, keepdims=True)
    h = (x * lax.rsqrt(ms + EPS)) * gain_ref[...]
    h = h * (1.0 + mod[1:2]) + mod[0:1]
    hb = h.astype(BF16)

    def proj(a, b):
        return _dot(hb, w_ref[:, a:b])

    pa = proj(0, 3 * W_A)
    qa_ref[...] = (pa[:, :W_A] * (HEAD_DIM ** -0.5)).astype(BF16)
    ka_ref[...] = pa[:, W_A:2 * W_A].astype(BF16)
    va_ref[...] = pa[:, 2 * W_A:].astype(BF16)

    tab = tab_ref[...]
    cos_b = jnp.concatenate([tab[:, 0:LANES]] * 2, axis=1)
    sin_b = jnp.concatenate([tab[:, LANES:2 * LANES]] * 2, axis=1)
    cos_c1 = tab[:, 2 * LANES:3 * LANES]
    sin_c1 = tab[:, 3 * LANES:4 * LANES]
    cos_c = jnp.concatenate([cos_c1] * 3, axis=1)
    sin_c = jnp.concatenate([sin_c1] * 3, axis=1)

    o_b = 3 * W_A
    pb = proj(o_b, o_b + 3 * W_B)
    pbr = proj(IN_WIDTH, IN_WIDTH + 2 * W_B)
    qb = pb[:, :W_B] * cos_b + pbr[:, :W_B] * sin_b
    qb_ref[...] = (qb * (DIFF_QK_DIM ** -0.5 * LOG2E)).astype(BF16)
    kb_ref[...] = (pb[:, W_B:2 * W_B] * cos_b + pbr[:, W_B:] * sin_b).astype(BF16)
    vb_ref[...] = pb[:, 2 * W_B:].astype(BF16)

    o_c = o_b + 3 * W_B
    pc = proj(o_c, IN_WIDTH)
    pcr = proj(IN_WIDTH + 2 * W_B, EXT_WIDTH)
    ones = ones_ref[...]
    qc = pc[:, :W_C]
    kc = pc[:, W_C:W_C + W_KC]
    nq = lax.rsqrt(_head_mean_sq(qc, ones) + EPS)
    nk = lax.rsqrt(_head_mean_sq(kc, ones[:W_KC, :W_KC]) + EPS)
    gq = gq_ref[...]
    gk = gk_ref[...]
    q = nq * (qc * gq[0:1] * cos_c + pcr[:, :W_C] * gq[1:2] * sin_c)
    qc_ref[...] = (q * (HEAD_DIM ** -0.5 * LOG2E)).astype(BF16)
    k = nk * (kc * gk[0:1] * cos_c1 + pcr[:, W_C:] * gk[1:2] * sin_c1)
    kc_ref[...] = k.astype(BF16)
    vc_ref[...] = pc[:, W_C + W_KC:].astype(BF16)


def _in_projection(xs, modsel, gain, w_ext, tab, gq, gk, ones, n_lat_tiles):
    b, t_all, d = xs.shape
    n_tiles = t_all // TILE
    widths = (W_A, W_A, W_A, W_B, W_B, W_B, W_C, W_KC, W_KC)
    tok = lambda bi, ti: (bi, ti, 0)
    const2 = lambda bi, ti: (0, 0)
    return pl.pallas_call(
        _inproj_kernel,
        out_shape=[jax.ShapeDtypeStruct((b, t_all, w), BF16) for w in widths],
        grid=(b, n_tiles),
        in_specs=[pl.BlockSpec((None, TILE, d), tok),
                  pl.BlockSpec((None, 6, d), lambda bi, ti: (2 * bi + (ti >= n_lat_tiles).astype(jnp.int32), 0, 0)),
                  pl.BlockSpec((1, d), const2),
                  pl.BlockSpec((d, EXT_WIDTH), const2),
                  pl.BlockSpec((TILE, 4 * LANES), lambda bi, ti: (ti, 0)),
                  pl.BlockSpec((2, W_C), const2),
                  pl.BlockSpec((2, W_KC), const2),
                  pl.BlockSpec((W_C, W_C), const2)],
        out_specs=[pl.BlockSpec((None, TILE, w), tok) for w in widths],
        compiler_params=_cparams(2),
        name="in_projection",
    )(xs, modsel, gain, w_ext, tab, gq, gk, ones)


def _flash_kernel(q_ref, k_ref, v_ref, aux_ref, o_ref, va_ref, vb_ref, qs_ref, acc_ref, m_ref,
                  s0_ref, s1_ref, mb0_ref, mb1_ref, *,
                  n_qblk, n_sub, tk, n_lat_blocks, ctx_tile, qt_off, mode, use_exp2, lam_init):
    exp_fn = jnp.exp2 if use_exp2 else jnp.exp
    qt = pl.program_id(2) + qt_off
    sub_w = LANES // n_sub
    half = LANES // 2
    lane = lax.broadcasted_iota(jnp.int32, (1, LANES), 1)
    lower = lane < half
    n_pieces = n_qblk * n_sub
    ma = (n_pieces // 2) * TILE
    ctx_start = n_lat_blocks * tk

    @pl.when(pl.program_id(2) == 0)
    def _():
        v = v_ref[...].astype(F32)
        va_ref[...] = jnp.where(lower, v, 1.0).astype(BF16)
        vb_ref[...] = jnp.where(lower, 1.0, v).astype(BF16)

    ia, ib = 0, n_pieces // 2
    for blk in range(n_qblk):
        qf = q_ref[:, blk * LANES:(blk + 1) * LANES].astype(F32)
        for sub in range(n_sub):
            msk = (lane >= sub * sub_w) & (lane < (sub + 1) * sub_w)
            piece = jnp.where(msk, qf, 0.0).astype(BF16)
            if sub * sub_w < half:
                qs_ref[ia * TILE:(ia + 1) * TILE, :] = piece
                ia += 1
            else:
                qs_ref[ib * TILE:(ib + 1) * TILE, :] = piece
                ib += 1

    s_bufs = (s0_ref, s1_ref)
    mb_bufs = (mb0_ref, mb1_ref)
    m_rows = n_pieces * TILE

    def scores(start, size, slot):
        s = _dot_nt(qs_ref[...], k_ref[pl.ds(start, size), :])
        s_bufs[slot][:, :size] = s
        mb = jnp.max(s, axis=-1, keepdims=True)
        mb_bufs[slot][...] = jnp.broadcast_to(mb, (m_rows, LANES))

    def accumulate(start, size, slot, first):
        mb = mb_bufs[slot][...]
        if first:
            m_new = mb
        else:
            m_old = m_ref[...]
            m_new = jnp.maximum(m_old, mb)
        s_ref = s_bufs[slot]
        p = jnp.concatenate(
            [exp_fn(s_ref[:, c * LANES:(c + 1) * LANES] - m_new).astype(BF16) for c in range(size // LANES)],
            axis=1)
        pva = _dot(p[:ma], va_ref[pl.ds(start, size), :])
        pvb = _dot(p[ma:], vb_ref[pl.ds(start, size), :])
        if first:
            acc_ref[:ma, :] = pva
            acc_ref[ma:, :] = pvb
        else:
            alpha = exp_fn(m_old - m_new)
            acc_ref[:ma, :] = alpha[:ma] * acc_ref[:ma, :] + pva
            acc_ref[ma:, :] = alpha[ma:] * acc_ref[ma:, :] + pvb
        m_ref[...] = m_new

    def lat(j):
        return pl.multiple_of(j * tk, tk)

    def latent_queries():
        scores(ctx_start, CTX_LEN, 0)
        scores(lat(0), tk, 1)
        accumulate(ctx_start, CTX_LEN, 0, True)

        def body(i, carry):
            scores(lat(2 * i + 1), tk, 0)
            accumulate(lat(2 * i), tk, 1, False)
            scores(lat(2 * i + 2), tk, 1)
            accumulate(lat(2 * i + 1), tk, 0, False)
            return carry

        lax.fori_loop(0, (n_lat_blocks - 2) // 2, body, 0)
        scores(lat(n_lat_blocks - 1), tk, 0)
        accumulate(lat(n_lat_blocks - 2), tk, 1, False)
        accumulate(lat(n_lat_blocks - 1), tk, 0, False)

    def context_queries():
        scores(ctx_start, CTX_LEN, 0)
        accumulate(ctx_start, CTX_LEN, 0, True)

    if ctx_tile is None:
        latent_queries()
    else:
        pl.when(qt != ctx_tile)(latent_queries)
        pl.when(qt == ctx_tile)(context_queries)

    acc = acc_ref[...]
    r = acc / pltpu.roll(acc, half, 1)
    ra, rb = r[:ma], r[ma:]
    if mode == "plain":
        for i in range(n_pieces // 2):
            o = jnp.where(lower, ra[i * TILE:(i + 1) * TILE], rb[i * TILE:(i + 1) * TILE])
            o_ref[:, i * LANES:(i + 1) * LANES] = o.astype(BF16)
    else:
        aux = aux_ref[...]
        l1 = jnp.sum(aux[0:1] * aux[1:2], axis=-1, keepdims=True)
        l2 = jnp.sum(aux[2:3] * aux[3:4], axis=-1, keepdims=True)
        lam = jnp.exp(l1) - jnp.exp(l2) + lam_init
        oa = ra[:TILE] - lam * ra[TILE:]
        ob = rb[:TILE] - lam * rb[TILE:]
        o = jnp.where(lower, oa, ob)
        sq = o * o
        ss_a = jnp.sum(jnp.where(lower, sq, 0.0), axis=-1, keepdims=True)
        ss_b = jnp.sum(jnp.where(lower, 0.0, sq), axis=-1, keepdims=True)
        ms = jnp.where(lower, ss_a, ss_b) * (1.0 / HEAD_DIM)
        o = (o * lax.rsqrt(ms + EPS)) * aux[4:5]
        o_ref[...] = (o * (1.0 - lam_init)).astype(BF16)


def _flash(q, k, v, aux, *, n_qblk, n_sub, n_hp, qt_off, n_qt, n_lat, ctx_tile, mode, use_exp2,
           lam_init=0.0):
    b, t_all, _ = q.shape
    qw = n_qblk * LANES
    tk = 512
    assert n_lat % (2 * tk) == 0 and tk >= CTX_LEN
    m_rows = n_qblk * n_sub * TILE
    kern = functools.partial(_flash_kernel, n_qblk=n_qblk, n_sub=n_sub, tk=tk,
                             n_lat_blocks=n_lat // tk, ctx_tile=ctx_tile, qt_off=qt_off,
                             mode=mode, use_exp2=use_exp2, lam_init=lam_init)
    return pl.pallas_call(
        kern,
        out_shape=jax.ShapeDtypeStruct((b, n_qt * TILE, n_hp * qw), BF16),
        grid=(b, n_hp, n_qt),
        in_specs=[pl.BlockSpec((None, TILE, qw), lambda bi, hp, qt: (bi, qt + qt_off, hp)),
                  pl.BlockSpec((None, t_all, LANES), lambda bi, hp, qt: (bi, 0, hp)),
                  pl.BlockSpec((None, t_all, LANES), lambda bi, hp, qt: (bi, 0, hp)),
                  pl.BlockSpec((8, LANES), lambda bi, hp, qt: (0, 0))],
        out_specs=pl.BlockSpec((None, TILE, qw), lambda bi, hp, qt: (bi, qt, hp)),
        scratch_shapes=[pltpu.VMEM((t_all, LANES), BF16),
                        pltpu.VMEM((t_all, LANES), BF16),
                        pltpu.VMEM((m_rows, LANES), BF16),
                        pltpu.VMEM((m_rows, LANES), F32),
                        pltpu.VMEM((m_rows, LANES), F32),
                        pltpu.VMEM((m_rows, tk), F32),
                        pltpu.VMEM((m_rows, tk), F32),
                        pltpu.VMEM((m_rows, LANES), F32),
                        pltpu.VMEM((m_rows, LANES), F32)],
        compiler_params=_cparams(3),
        name="flash_" + mode,
    )(q, k, v, aux)


def _na_kernel(q_ref, k0, k1, k2, k3, v0, v1, v2, v3, kc_ref, vc_ref, bias_ref, o_ref):
    lane = lax.broadcasted_iota(jnp.int32, (1, LANES), 1)
    lower = lane < LANES // 2
    qf = q_ref[...].astype(F32)
    kw = jnp.concatenate([k0[...], k1[...], k2[...], k3[...]], axis=0)
    vw = jnp.concatenate([v0[...], v1[...], v2[...], v3[...]], axis=0)
    kc = kc_ref[...]
    vc = vc_ref[...]
    outs = []
    for hh in range(2):
        msk = lower if hh == 0 else jnp.logical_not(lower)
        qh = jnp.where(msk, qf, 0.0).astype(BF16)
        s_w = _dot_nt(qh, kw) + bias_ref[hh]
        s_c = _dot_nt(qh, kc)
        m = jnp.maximum(jnp.max(s_w, axis=-1, keepdims=True), jnp.max(s_c, axis=-1, keepdims=True))
        p_w = jnp.exp(s_w - m)
        p_c = jnp.exp(s_c - m)
        l = jnp.sum(p_w, axis=-1, keepdims=True) + jnp.sum(p_c, axis=-1, keepdims=True)
        o = _dot(p_w.astype(BF16), vw) + _dot(p_c.astype(BF16), vc)
        outs.append(o / l)
    o_ref[...] = jnp.where(lower, outs[0], outs[1]).astype(BF16)


def _neighbourhood_attention(qa, ka, va, bias, n_lat):
    b = qa.shape[0]
    q_tok = NA_QROWS * GRID_W
    v_tok = q_tok // 2
    n_rb = n_lat // q_tok
    n_view = n_lat // v_tok
    ctx_blk = n_lat // v_tok

    def view(j):
        return lambda rb, hp, bi: (bi, jnp.clip(2 * rb - 1 + j, 0, n_view - 1), hp)

    kv_specs = [pl.BlockSpec((None, v_tok, LANES), view(j)) for j in range(4)]
    ctx_spec = pl.BlockSpec((None, CTX_LEN, LANES), lambda rb, hp, bi: (bi, ctx_blk, hp))

    def bias_map(rb, hp, bi):
        pat = jnp.where(rb == 0, 0, jnp.where(rb == n_rb - 1, 2, 1))
        return (pat, hp, 0, 0)

    return pl.pallas_call(
        _na_kernel,
        out_shape=jax.ShapeDtypeStruct((b, n_lat, W_A), BF16),
        grid=(n_rb, NA_HEADS // 2, b),
        in_specs=[pl.BlockSpec((None, q_tok, LANES), lambda rb, hp, bi: (bi, rb, hp))]
                 + kv_specs + kv_specs + [ctx_spec, ctx_spec,
                 pl.BlockSpec((None, 2, q_tok, NA_KROWS * GRID_W), bias_map)],
        out_specs=pl.BlockSpec((None, q_tok, LANES), lambda rb, hp, bi: (bi, rb, hp)),
        compiler_params=_cparams(3),
        name="neighbourhood_attention",
    )(qa, ka, ka, ka, ka, va, va, va, va, ka, va, bias)


def _na_bias_table(rpb, rows):
    cols = np.arange(GRID_W)
    c0 = np.clip(cols - NA_WIN_W // 2, 0, GRID_W - NA_WIN_W)
    cc = cols[None, :]
    col_ok = (cc >= c0[:, None]) & (cc < c0[:, None] + NA_WIN_W)
    dc = np.clip(cc - cols[:, None] + (NA_WIN_W - 1), 0, 2 * NA_WIN_W - 2)
    e = jnp.where(col_ok[None, None], rpb.astype(F32)[:, :, dc], NEG)
    e = jnp.concatenate([e, jnp.full_like(e[:, :1], NEG)], axis=1)
    a = np.arange(NA_QROWS)[:, None]
    i = np.arange(NA_KROWS)[None, :]
    pats = []
    for r_base in (0, NA_QROWS, rows - NA_QROWS):
        r = r_base + a
        key_row = r_base - NA_WIN_H // 2 + i
        r0 = np.clip(r - NA_WIN_H // 2, 0, rows - NA_WIN_H)
        ok = (key_row >= r0) & (key_row < r0 + NA_WIN_H) & (key_row >= 0) & (key_row < rows)
        dr = np.where(ok, key_row - r + (NA_WIN_H - 1), 2 * NA_WIN_H - 1)
        pats.append(dr)
    dr_all = np.stack(pats)
    t = e[:, dr_all]
    t = t.transpose(1, 0, 2, 4, 3, 5)
    return t.reshape(3, NA_HEADS, NA_QROWS * GRID_W, NA_KROWS * GRID_W)


def _outproj_kernel(x_ref, oa_ref, ob_ref, oc_ref, wa_ref, wb_ref, wc_ref, mod_ref, gain_ref,
                    wrh_ref, wrl_ref, br_ref, x1_ref, tok_ref, comb_ref):
    mod = mod_ref[...]
    y = _dot(oa_ref[...], wa_ref[...]) + _dot(ob_ref[...], wb_ref[...]) + _dot(oc_ref[...], wc_ref[...])
    x1 = x_ref[...] + mod[2:3] * y
    x1_ref[...] = x1
    ms = jnp.mean(x1 * x1, axis=-1, keepdims=True)
    t = (x1 * lax.rsqrt(ms + EPS)) * gain_ref[...]
    t = t * (1.0 + mod[4:5]) + mod[3:4]
    tok_ref[...] = t.astype(BF16)

    t_hi, t_lo = _split_bf16(t)
    wrh = wrh_ref[...]
    logits = _dot(t_hi, wrh) + _dot(t_lo, wrh) + _dot(t_hi, wrl_ref[...]) + br_ref[...]

    lane = lax.broadcasted_iota(jnp.int32, logits.shape, 1)
    lane_f = lane.astype(F32)
    is_g = lane < N_GROUPS
    gl = jnp.where(is_g, logits, NEG)
    gmax = jnp.max(gl, axis=-1, keepdims=True)
    g_sel = jnp.min(jnp.where(gl == gmax, lane_f, 1e9), axis=-1, keepdims=True)
    p_grp = 1.0 / jnp.sum(jnp.where(is_g, jnp.exp(gl - gmax), 0.0), axis=-1, keepdims=True)
    grp_of_lane = lax.shift_right_arithmetic(lane - N_GROUPS, 2).astype(F32)
    in_grp = (lane >= N_GROUPS) & (lane < N_GROUPS + N_EXPERTS) & (grp_of_lane == g_sel)
    el = jnp.where(in_grp, logits, NEG)
    v1 = jnp.max(el, axis=-1, keepdims=True)
    i1 = jnp.min(jnp.where(el == v1, lane_f, 1e9), axis=-1, keepdims=True)
    el2 = jnp.where(lane_f == i1, NEG, el)
    v2 = jnp.max(el2, axis=-1, keepdims=True)
    i2 = jnp.min(jnp.where(el2 == v2, lane_f, 1e9), axis=-1, keepdims=True)
    e2 = jnp.exp(v2 - v1)
    den = 1.0 + e2
    w1 = p_grp / den
    w2 = p_grp * e2 / den
    comb_ref[...] = jnp.where(lane_f == i1, w1, 0.0) + jnp.where(lane_f == i2, w2, 0.0)


def _out_projection(xs, oa, ob, oc, wa, wb, wc, modsel, gain, wrh, wrl, br, n_tiles, n_lat_tiles):
    b, _, d = xs.shape
    tok = lambda bi, ti: (bi, ti, 0)
    const2 = lambda bi, ti: (0, 0)
    rows = n_tiles * TILE
    return pl.pallas_call(
        _outproj_kernel,
        out_shape=[jax.ShapeDtypeStruct((b, rows, d), F32),
                   jax.ShapeDtypeStruct((b, rows, d), BF16),
                   jax.ShapeDtypeStruct((b, rows, LANES), F32)],
        grid=(b, n_tiles),
        in_specs=[pl.BlockSpec((None, TILE, d), tok),
                  pl.BlockSpec((None, TILE, W_A), tok),
                  pl.BlockSpec((None, TILE, W_B), tok),
                  pl.BlockSpec((None, TILE, W_C), tok),
                  pl.BlockSpec((W_A, d), const2),
                  pl.BlockSpec((W_B, d), const2),
                  pl.BlockSpec((W_C, d), const2),
                  pl.BlockSpec((None, 6, d), lambda bi, ti: (2 * bi + (ti >= n_lat_tiles).astype(jnp.int32), 0, 0)),
                  pl.BlockSpec((1, d), const2),
                  pl.BlockSpec((d, LANES), const2),
                  pl.BlockSpec((d, LANES), const2),
                  pl.BlockSpec((1, LANES), const2)],
        out_specs=[pl.BlockSpec((None, TILE, d), tok),
                   pl.BlockSpec((None, TILE, d), tok),
                   pl.BlockSpec((None, TILE, LANES), tok)],
        compiler_params=_cparams(2),
        name="out_projection",
    )(xs, oa, ob, oc, wa, wb, wc, modsel, gain, wrh, wrl, br)


def _moe_kernel(tok_ref, comb_ref, x1_ref, wg_ref, wu_ref, wd_ref, mod_ref, fgain_ref, o_ref, acc_ref,
                *, final):
    e = pl.program_id(2)
    t = tok_ref[...]
    hid = jax.nn.silu(_dot(t, wg_ref[...])) * _dot(t, wu_ref[...])
    y = _dot(hid.astype(BF16), wd_ref[...])
    comb = comb_ref[...]
    lane = lax.broadcasted_iota(jnp.int32, comb.shape, 1)
    c = jnp.sum(jnp.where(lane == e + N_GROUPS, comb, 0.0), axis=-1, keepdims=True)

    @pl.when(e == 0)
    def _():
        acc_ref[...] = y * c

    @pl.when(e > 0)
    def _():
        acc_ref[...] += y * c

    @pl.when(e == N_EXPERTS - 1)
    def _():
        x2 = x1_ref[...] + mod_ref[5:6, :] * acc_ref[...]
        if final:
            ms = jnp.mean(x2 * x2, axis=-1, keepdims=True)
            x2 = (x2 * lax.rsqrt(ms + EPS)) * fgain_ref[...]
        o_ref[...] = x2


def _moe(tok, comb, x1, wg, wu, wd, modsel, fgain, *, tm, tile_off, n_t, is_ctx, final):
    b, _, d = x1.shape
    tokm = lambda bi, ti, e: (bi, ti + tile_off, 0)
    return pl.pallas_call(
        functools.partial(_moe_kernel, final=final),
        out_shape=jax.ShapeDtypeStruct((b, n_t * tm, d), F32),
        grid=(b, n_t, N_EXPERTS),
        in_specs=[pl.BlockSpec((None, tm, d), tokm),
                  pl.BlockSpec((None, tm, LANES), tokm),
                  pl.BlockSpec((None, tm, d), tokm),
                  pl.BlockSpec((None, d, EXPERT_HIDDEN), lambda bi, ti, e: (e, 0, 0)),
                  pl.BlockSpec((None, d, EXPERT_HIDDEN), lambda bi, ti, e: (e, 0, 0)),
                  pl.BlockSpec((None, EXPERT_HIDDEN, d), lambda bi, ti, e: (e, 0, 0)),
                  pl.BlockSpec((None, 6, d), lambda bi, ti, e: (2 * bi + is_ctx, 0, 0)),
                  pl.BlockSpec((1, d), lambda bi, ti, e: (0, 0))],
        out_specs=pl.BlockSpec((None, tm, d), lambda bi, ti, e: (bi, ti, 0)),
        scratch_shapes=[pltpu.VMEM((tm, d), F32)],
        compiler_params=_cparams(3),
        name="moe_dense",
    )(tok, comb, x1, wg, wu, wd, modsel, fgain)


def _rot_perm(width, head):
    i = np.arange(width)
    half = head // 2
    first = (i % head) < half
    idx = np.where(first, i + half, i - half)
    sign = np.where(first, -1.0, 1.0).astype(np.float32)
    return idx, sign


def _rope_tables(n_lat):
    t = jnp.arange(n_lat)
    row = (t // GRID_W).astype(F32)
    col = (t % GRID_W).astype(F32)

    def cs(dim):
        quarter = dim // 4
        freqs = ROPE_THETA ** (-jnp.arange(quarter, dtype=F32) / quarter)
        ang = jnp.concatenate([row[:, None] * freqs, col[:, None] * freqs], axis=-1)
        cos = jnp.tile(jnp.cos(ang), (1, 2 * LANES // dim))
        sin = jnp.tile(jnp.sin(ang), (1, 2 * LANES // dim))
        cos = jnp.concatenate([cos, jnp.ones((CTX_LEN, LANES), F32)], axis=0)
        sin = jnp.concatenate([sin, jnp.zeros((CTX_LEN, LANES), F32)], axis=0)
        return cos, sin

    cos_b, sin_b = cs(DIFF_QK_DIM)
    cos_c, sin_c = cs(HEAD_DIM)
    return jnp.concatenate([cos_b, sin_b, cos_c, sin_c], axis=1)


def _extended_w_in(w_in):
    o_b = 3 * W_A
    o_c = o_b + 3 * W_B
    qc_cols = o_c + np.concatenate([h * HEAD_DIM + np.arange(HEAD_DIM) for h in GQA_Q_ORDER])
    cols = np.concatenate([np.arange(o_c), qc_cols, np.arange(o_c + W_C, IN_WIDTH)])
    w = w_in[:, cols]
    idx_b, sign_b = _rot_perm(W_B, DIFF_QK_DIM)
    idx_c, sign_c = _rot_perm(W_C, HEAD_DIM)
    idx_k, sign_k = _rot_perm(W_KC, HEAD_DIM)
    rot = jnp.concatenate([
        w[:, o_b + idx_b] * sign_b, w[:, o_b + W_B + idx_b] * sign_b,
        w[:, o_c + idx_c] * sign_c, w[:, o_c + W_C + idx_k] * sign_k], axis=1)
    return jnp.concatenate([w, rot], axis=1).astype(BF16)


def kernel(x, c, ctx, c_ctx, w_mod, b_mod, norm_attn, norm_ffn, w_in, w_out, na_rpb, diff_lambda_q1, diff_lambda_k1, diff_lambda_q2, diff_lambda_k2, diff_subln, gqa_q_norm, gqa_k_norm, router_group_w, router_group_b, router_expert_w, router_expert_b, w_gate, w_up, w_down, final_norm):
    b, s, d = x.shape
    assert d == D_MODEL and ctx.shape[1] == CTX_LEN and s % (NA_QROWS * GRID_W) == 0
    rows = s // GRID_W
    assert rows >= 2 * NA_QROWS
    t_all = s + CTX_LEN
    n_lat_tiles = s // TILE

    c_rows = jnp.zeros((8, d), F32).at[:b].set(c).at[b].set(c_ctx)
    mod = _modulation(c_rows, w_mod, b_mod)

    tab = _rope_tables(s)
    hidx = np.arange(HEAD_DIM)
    partner = np.where(hidx < HEAD_DIM // 2, hidx + HEAD_DIM // 2, hidx - HEAD_DIM // 2)
    blk = np.arange(W_C) // HEAD_DIM
    ones = jnp.asarray((blk[:, None] == blk[None, :]).astype(np.float32), BF16)
    oc_rows = W_A + W_B + np.concatenate([h * HEAD_DIM + np.arange(HEAD_DIM) for h in GQA_Q_ORDER])
    dummy_aux = jnp.zeros((8, LANES), F32)

    xs = jnp.concatenate([x, ctx], axis=1)
    for l in range(DEPTH):
        ctx_out = l < DEPTH - 1
        lam_init = 0.8 - 0.6 * math.exp(-0.3 * l)
        m_lat = mod[l, :b].reshape(b, 1, 6, d)
        m_ctx = jnp.broadcast_to(mod[l, b].reshape(1, 1, 6, d), (b, 1, 6, d))
        modsel = jnp.concatenate([m_lat, m_ctx], axis=1).reshape(2 * b, 6, d)

        gq = jnp.stack([jnp.tile(gqa_q_norm[l], GQA_Q_HEADS), jnp.tile(gqa_q_norm[l][partner], GQA_Q_HEADS)])
        gk = jnp.stack([jnp.tile(gqa_k_norm[l], GQA_KV_HEADS), jnp.tile(gqa_k_norm[l][partner], GQA_KV_HEADS)])
        qa, ka, va, qb, kb, vb, qc, kc, vc = _in_projection(
            xs, modsel, norm_attn[l][None], _extended_w_in(w_in[l]), tab, gq, gk, ones, n_lat_tiles)

        n_qt = n_lat_tiles + 1 if ctx_out else n_lat_tiles
        ctx_tile = n_lat_tiles if ctx_out else None
        oa = _neighbourhood_attention(qa, ka, va, _na_bias_table(na_rpb[l], rows), s)
        if ctx_out:
            oa_ctx = _flash(qa, ka, va, dummy_aux, n_qblk=1, n_sub=2, n_hp=NA_HEADS // 2,
                            qt_off=n_lat_tiles, n_qt=1, n_lat=s, ctx_tile=n_lat_tiles,
                            mode="plain", use_exp2=False)
            oa = jnp.concatenate([oa, oa_ctx], axis=1)
        pad = lambda v: jnp.pad(v, (0, LANES - v.shape[0]))
        aux = jnp.stack([pad(diff_lambda_q1[l]), pad(diff_lambda_k1[l]), pad(diff_lambda_q2[l]),
                         pad(diff_lambda_k2[l]), jnp.tile(diff_subln[l], 2),
                         jnp.zeros((LANES,), F32), jnp.zeros((LANES,), F32), jnp.zeros((LANES,), F32)])
        ob = _flash(qb, kb, vb, aux, n_qblk=1, n_sub=4, n_hp=DIFF_HEADS // 2, qt_off=0, n_qt=n_qt,
                    n_lat=s, ctx_tile=ctx_tile, mode="diff", use_exp2=True, lam_init=lam_init)
        oc = _flash(qc, kc, vc, dummy_aux, n_qblk=3, n_sub=2, n_hp=1, qt_off=0, n_qt=n_qt,
                    n_lat=s, ctx_tile=ctx_tile, mode="plain", use_exp2=True)

        w_o = w_out[l]
        wr = jnp.zeros((d, LANES), F32)
        wr = wr.at[:, :N_GROUPS].set(router_group_w[l]).at[:, N_GROUPS:N_GROUPS + N_EXPERTS].set(router_expert_w[l])
        wrh, wrl = _split_bf16(wr)
        br = jnp.zeros((1, LANES), F32)
        br = br.at[0, :N_GROUPS].set(router_group_b[l]).at[0, N_GROUPS:N_GROUPS + N_EXPERTS].set(router_expert_b[l])
        x1, tok, comb = _out_projection(
            xs, oa, ob, oc, w_o[:W_A].astype(BF16), w_o[W_A:W_A + W_B].astype(BF16), w_o[oc_rows].astype(BF16),
            modsel, norm_ffn[l][None], wrh, wrl, br, n_qt, n_lat_tiles)

        tm = next(t for t in (1024, 512) if s % t == 0)
        experts = (w_gate[l].astype(BF16), w_up[l].astype(BF16), w_down[l].astype(BF16))
        x_lat = _moe(tok, comb, x1, *experts, modsel, final_norm[None], tm=tm, tile_off=0,
                     n_t=s // tm, is_ctx=0, final=not ctx_out)
        if not ctx_out:
            return x_lat
        x_ctx = _moe(tok, comb, x1, *experts, modsel, final_norm[None], tm=TILE, tile_off=n_lat_tiles,
                     n_t=1, is_ctx=1, final=False)
        xs = jnp.concatenate([x_lat, x_ctx], axis=1)
```

```python
import functools
import math

import numpy as np
import jax
import jax.numpy as jnp
from jax import lax
from jax.experimental import pallas as pl
from jax.experimental.pallas import tpu as pltpu

F32 = jnp.float32
BF16 = jnp.bfloat16

D_MODEL = 1024
DEPTH = 2
GRID_W = 64
CTX_LEN = 256
HEAD_DIM = 64
NA_HEADS = 6
NA_WIN_H = 8
NA_WIN_W = 16
DIFF_HEADS = 4
DIFF_QK_DIM = 32
GQA_Q_HEADS = 6
GQA_KV_HEADS = 2
N_GROUPS = 4
EXPERTS_PER_GROUP = 4
N_EXPERTS = 16
EXPERT_HIDDEN = 512
ROPE_THETA = 10000.0
EPS = 1e-6
W_A = NA_HEADS * HEAD_DIM
W_B = DIFF_HEADS * 2 * DIFF_QK_DIM
W_C = GQA_Q_HEADS * HEAD_DIM
W_KC = GQA_KV_HEADS * HEAD_DIM
IN_WIDTH = 3 * W_A + 3 * W_B + W_C + 2 * W_KC
ROT_WIDTH = 2 * W_B + W_C + W_KC
EXT_WIDTH = IN_WIDTH + ROT_WIDTH

LANES = 128
TILE = CTX_LEN
NA_QROWS = 8
NA_KROWS = 16
NEG = -1e30
LOG2E = 1.4426950408889634
VMEM_LIMIT = 56 * 1024 * 1024
PAIRS_PER_STEP = 2

GQA_Q_ORDER = (0, 3, 1, 4, 2, 5)


def _cparams(n_axes):
    return pltpu.CompilerParams(dimension_semantics=("arbitrary",) * n_axes,
                                vmem_limit_bytes=VMEM_LIMIT)


def _split_bf16(a):
    hi = a.astype(BF16)
    lo = (a - hi.astype(F32)).astype(BF16)
    return hi, lo


def _dot(a, b):
    return jnp.dot(a, b, preferred_element_type=F32)


def _dot_nt(a, b):
    return lax.dot_general(a, b, (((1,), (1,)), ((), ())), preferred_element_type=F32)


def _mod_kernel(c_ref, w_ref, b_ref, o_ref):
    c = c_ref[...]
    a = c * jax.nn.sigmoid(c)
    a_hi, a_lo = _split_bf16(a)
    w_hi, w_lo = _split_bf16(w_ref[...])
    o_ref[...] = _dot(a_hi, w_hi) + _dot(a_lo, w_hi) + _dot(a_hi, w_lo) + b_ref[...]


def _modulation(c_rows, w_mod, b_mod):
    depth, d, n = w_mod.shape
    bn = 1536
    return pl.pallas_call(
        _mod_kernel,
        out_shape=jax.ShapeDtypeStruct((depth, 8, n), F32),
        grid=(depth, n // bn),
        in_specs=[pl.BlockSpec((8, d), lambda l, j: (0, 0)),
                  pl.BlockSpec((None, d, bn), lambda l, j: (l, 0, j)),
                  pl.BlockSpec((None, 1, bn), lambda l, j: (l, 0, j))],
        out_specs=pl.BlockSpec((None, 8, bn), lambda l, j: (l, 0, j)),
        compiler_params=_cparams(2),
        name="adaln_mod",
    )(c_rows, w_mod, b_mod.reshape(depth, 1, n))


def _head_mean_sq(t, ones):
    hi, lo = _split_bf16(t * t)
    return (_dot(hi, ones) + _dot(lo, ones)) * (1.0 / HEAD_DIM)


def _inproj_kernel(x_ref, mod_ref, gain_ref, w_ref, tab_ref, gq_ref, gk_ref, ones_ref,
                   qa_ref, ka_ref, va_ref, qb_ref, kb_ref, vb_ref, qc_ref, kc_ref, vc_ref):
    x = x_ref[...]
    mod = mod_ref[...]
    ms = jnp.mean(x * x, axis=-1, keepdims=True)
    h = (x * lax.rsqrt(ms + EPS)) * gain_ref[...]
    h = h * (1.0 + mod[1:2]) + mod[0:1]
    hb = h.astype(BF16)

    def proj(a, b):
        return _dot(hb, w_ref[:, a:b])

    pa = proj(0, 3 * W_A)
    qa_ref[...] = (pa[:, :W_A] * (HEAD_DIM ** -0.5)).astype(BF16)
    ka_ref[...] = pa[:, W_A:2 * W_A].astype(BF16)
    va_ref[...] = pa[:, 2 * W_A:].astype(BF16)

    tab = tab_ref[...]
    cos_b = jnp.concatenate([tab[:, 0:LANES]] * 2, axis=1)
    sin_b = jnp.concatenate([tab[:, LANES:2 * LANES]] * 2, axis=1)
    cos_c1 = tab[:, 2 * LANES:3 * LANES]
    sin_c1 = tab[:, 3 * LANES:4 * LANES]
    cos_c = jnp.concatenate([cos_c1] * 3, axis=1)
    sin_c = jnp.concatenate([sin_c1] * 3, axis=1)

    o_b = 3 * W_A
    pb = proj(o_b, o_b + 3 * W_B)
    pbr = proj(IN_WIDTH, IN_WIDTH + 2 * W_B)
    qb = pb[:, :W_B] * cos_b + pbr[:, :W_B] * sin_b
    qb_ref[...] = (qb * (DIFF_QK_DIM ** -0.5 * LOG2E)).astype(BF16)
    kb_ref[...] = (pb[:, W_B:2 * W_B] * cos_b + pbr[:, W_B:] * sin_b).astype(BF16)
    vb_ref[...] = pb[:, 2 * W_B:].astype(BF16)

    o_c = o_b + 3 * W_B
    pc = proj(o_c, IN_WIDTH)
    pcr = proj(IN_WIDTH + 2 * W_B, EXT_WIDTH)
    ones = ones_ref[...]
    qc = pc[:, :W_C]
    kc = pc[:, W_C:W_C + W_KC]
    nq = lax.rsqrt(_head_mean_sq(qc, ones) + EPS)
    nk = lax.rsqrt(_head_mean_sq(kc, ones[:W_KC, :W_KC]) + EPS)
    gq = gq_ref[...]
    gk = gk_ref[...]
    q = nq * (qc * gq[0:1] * cos_c + pcr[:, :W_C] * gq[1:2] * sin_c)
    qc_ref[...] = (q * (HEAD_DIM ** -0.5 * LOG2E)).astype(BF16)
    k = nk * (kc * gk[0:1] * cos_c1 + pcr[:, W_C:] * gk[1:2] * sin_c1)
    kc_ref[...] = k.astype(BF16)
    vc_ref[...] = pc[:, W_C + W_KC:].astype(BF16)


def _in_projection(xs, modsel, gain, w_ext, tab, gq, gk, ones, n_lat_tiles):
    b, t_all, d = xs.shape
    n_tiles = t_all // TILE
    widths = (W_A, W_A, W_A, W_B, W_B, W_B, W_C, W_KC, W_KC)
    tok = lambda bi, ti: (bi, ti, 0)
    const2 = lambda bi, ti: (0, 0)
    return pl.pallas_call(
        _inproj_kernel,
        out_shape=[jax.ShapeDtypeStruct((b, t_all, w), BF16) for w in widths],
        grid=(b, n_tiles),
        in_specs=[pl.BlockSpec((None, TILE, d), tok),
                  pl.BlockSpec((None, 6, d), lambda bi, ti: (2 * bi + (ti >= n_lat_tiles).astype(jnp.int32), 0, 0)),
                  pl.BlockSpec((1, d), const2),
                  pl.BlockSpec((d, EXT_WIDTH), const2),
                  pl.BlockSpec((TILE, 4 * LANES), lambda bi, ti: (ti, 0)),
                  pl.BlockSpec((2, W_C), const2),
                  pl.BlockSpec((2, W_KC), const2),
                  pl.BlockSpec((W_C, W_C), const2)],
        out_specs=[pl.BlockSpec((None, TILE, w), tok) for w in widths],
        compiler_params=_cparams(2),
        name="in_projection",
    )(xs, modsel, gain, w_ext, tab, gq, gk, ones)


def _flash_kernel(q_ref, k_ref, v_ref, aux_ref, o_ref, va_ref, vb_ref, qs_ref, acc_ref, m_ref,
                  s0_ref, s1_ref, mb0_ref, mb1_ref, *,
                  n_qblk, n_sub, tk, n_lat_blocks, ctx_tile, qt_off, mode, use_exp2, lam_init):
    exp_fn = jnp.exp2 if use_exp2 else jnp.exp
    qt = pl.program_id(2) + qt_off
    sub_w = LANES // n_sub
    half = LANES // 2
    lane = lax.broadcasted_iota(jnp.int32, (1, LANES), 1)
    lower = lane < half
    n_pieces = n_qblk * n_sub
    ma = (n_pieces // 2) * TILE
    m_rows = n_pieces * TILE
    ctx_start = n_lat_blocks * tk

    @pl.when(pl.program_id(2) == 0)
    def _():
        v = v_ref[...].astype(F32)
        va_ref[...] = jnp.where(lower, v, 1.0).astype(BF16)
        vb_ref[...] = jnp.where(lower, 1.0, v).astype(BF16)

    ia, ib = 0, n_pieces // 2
    for blk in range(n_qblk):
        qf = q_ref[:, blk * LANES:(blk + 1) * LANES].astype(F32)
        for sub in range(n_sub):
            msk = (lane >= sub * sub_w) & (lane < (sub + 1) * sub_w)
            piece = jnp.where(msk, qf, 0.0).astype(BF16)
            if sub * sub_w < half:
                qs_ref[ia * TILE:(ia + 1) * TILE, :] = piece
                ia += 1
            else:
                qs_ref[ib * TILE:(ib + 1) * TILE, :] = piece
                ib += 1

    s_bufs = (s0_ref, s1_ref)
    mb_bufs = (mb0_ref, mb1_ref)

    def scores(start, size, slot):
        s = _dot_nt(qs_ref[...], k_ref[pl.ds(start, size), :])
        s_bufs[slot][:, :size] = s
        mb = jnp.max(s, axis=-1, keepdims=True)
        mb_bufs[slot][...] = jnp.broadcast_to(mb, (m_rows, LANES))

    def accumulate(start, size, slot, first):
        mb = mb_bufs[slot][...]
        if first:
            m_new = mb
        else:
            m_old = m_ref[...]
            m_new = jnp.maximum(m_old, mb)
        s_ref = s_bufs[slot]
        cols = [s_ref[:, c * LANES:(c + 1) * LANES] - m_new for c in range(size // LANES)]
        if use_exp2:
            p = jnp.concatenate([jnp.exp2(d.astype(BF16)) for d in cols], axis=1)
        else:
            p = jnp.concatenate([jnp.exp(d).astype(BF16) for d in cols], axis=1)
        pva = _dot(p[:ma], va_ref[pl.ds(start, size), :])
        pvb = _dot(p[ma:], vb_ref[pl.ds(start, size), :])
        if first:
            acc_ref[:ma, :] = pva
            acc_ref[ma:, :] = pvb
        else:
            alpha = exp_fn(m_old - m_new)
            acc_ref[:ma, :] = alpha[:ma] * acc_ref[:ma, :] + pva
            acc_ref[ma:, :] = alpha[ma:] * acc_ref[ma:, :] + pvb
        m_ref[...] = m_new

    def lat(j):
        return pl.multiple_of(j * tk, tk)

    def latent_queries():
        scores(ctx_start, CTX_LEN, 0)
        scores(lat(0), tk, 1)
        accumulate(ctx_start, CTX_LEN, 0, True)

        def pair(i):
            scores(lat(2 * i + 1), tk, 0)
            accumulate(lat(2 * i), tk, 1, False)
            scores(lat(2 * i + 2), tk, 1)
            accumulate(lat(2 * i + 1), tk, 0, False)

        def body(i, carry):
            for u in range(PAIRS_PER_STEP):
                pair(i * PAIRS_PER_STEP + u)
            return carry

        n_pairs = (n_lat_blocks - 2) // 2
        n_steps = n_pairs // PAIRS_PER_STEP
        lax.fori_loop(0, n_steps, body, 0)
        for i in range(n_steps * PAIRS_PER_STEP, n_pairs):
            pair(i)
        scores(lat(n_lat_blocks - 1), tk, 0)
        accumulate(lat(n_lat_blocks - 2), tk, 1, False)
        accumulate(lat(n_lat_blocks - 1), tk, 0, False)

    def context_queries():
        scores(ctx_start, CTX_LEN, 0)
        accumulate(ctx_start, CTX_LEN, 0, True)

    if ctx_tile is None:
        latent_queries()
    else:
        pl.when(qt != ctx_tile)(latent_queries)
        pl.when(qt == ctx_tile)(context_queries)

    acc = acc_ref[...]
    r = acc / pltpu.roll(acc, half, 1)
    ra, rb = r[:ma], r[ma:]
    if mode == "plain":
        for i in range(n_pieces // 2):
            o = jnp.where(lower, ra[i * TILE:(i + 1) * TILE], rb[i * TILE:(i + 1) * TILE])
            o_ref[:, i * LANES:(i + 1) * LANES] = o.astype(BF16)
    else:
        aux = aux_ref[...]
        l1 = jnp.sum(aux[0:1] * aux[1:2], axis=-1, keepdims=True)
        l2 = jnp.sum(aux[2:3] * aux[3:4], axis=-1, keepdims=True)
        lam = jnp.exp(l1) - jnp.exp(l2) + lam_init
        oa = ra[:TILE] - lam * ra[TILE:]
        ob = rb[:TILE] - lam * rb[TILE:]
        o = jnp.where(lower, oa, ob)
        sq = o * o
        ss_a = jnp.sum(jnp.where(lower, sq, 0.0), axis=-1, keepdims=True)
        ss_b = jnp.sum(jnp.where(lower, 0.0, sq), axis=-1, keepdims=True)
        ms = jnp.where(lower, ss_a, ss_b) * (1.0 / HEAD_DIM)
        o = (o * lax.rsqrt(ms + EPS)) * aux[4:5]
        o_ref[...] = (o * (1.0 - lam_init)).astype(BF16)


def _flash(q, k, v, aux, *, n_qblk, n_sub, n_hp, qt_off, n_qt, n_lat, ctx_tile, mode, use_exp2,
           lam_init=0.0):
    b, t_all, _ = q.shape
    qw = n_qblk * LANES
    tk = 512
    assert n_lat % (2 * tk) == 0 and tk >= CTX_LEN
    m_rows = n_qblk * n_sub * TILE
    kern = functools.partial(_flash_kernel, n_qblk=n_qblk, n_sub=n_sub, tk=tk,
                             n_lat_blocks=n_lat // tk, ctx_tile=ctx_tile, qt_off=qt_off,
                             mode=mode, use_exp2=use_exp2, lam_init=lam_init)
    return pl.pallas_call(
        kern,
        out_shape=jax.ShapeDtypeStruct((b, n_qt * TILE, n_hp * qw), BF16),
        grid=(b, n_hp, n_qt),
        in_specs=[pl.BlockSpec((None, TILE, qw), lambda bi, hp, qt: (bi, qt + qt_off, hp)),
                  pl.BlockSpec((None, t_all, LANES), lambda bi, hp, qt: (bi, 0, hp)),
                  pl.BlockSpec((None, t_all, LANES), lambda bi, hp, qt: (bi, 0, hp)),
                  pl.BlockSpec((8, LANES), lambda bi, hp, qt: (0, 0))],
        out_specs=pl.BlockSpec((None, TILE, qw), lambda bi, hp, qt: (bi, qt, hp)),
        scratch_shapes=[pltpu.VMEM((t_all, LANES), BF16),
                        pltpu.VMEM((t_all, LANES), BF16),
                        pltpu.VMEM((m_rows, LANES), BF16),
                        pltpu.VMEM((m_rows, LANES), F32),
                        pltpu.VMEM((m_rows, LANES), F32),
                        pltpu.VMEM((m_rows, tk), F32),
                        pltpu.VMEM((m_rows, tk), F32),
                        pltpu.VMEM((m_rows, LANES), F32),
                        pltpu.VMEM((m_rows, LANES), F32)],
        compiler_params=_cparams(3),
        name="flash_" + mode,
    )(q, k, v, aux)


def _na_kernel(q_ref, k0, k1, k2, k3, v0, v1, v2, v3, kc_ref, vc_ref, bias_ref, o_ref):
    lane = lax.broadcasted_iota(jnp.int32, (1, LANES), 1)
    lower = lane < LANES // 2
    qf = q_ref[...].astype(F32)
    kw = jnp.concatenate([k0[...], k1[...], k2[...], k3[...]], axis=0)
    vw = jnp.concatenate([v0[...], v1[...], v2[...], v3[...]], axis=0)
    kc = kc_ref[...]
    vc = vc_ref[...]
    outs = []
    for hh in range(2):
        msk = lower if hh == 0 else jnp.logical_not(lower)
        qh = jnp.where(msk, qf, 0.0).astype(BF16)
        s_w = _dot_nt(qh, kw) + bias_ref[hh]
        s_c = _dot_nt(qh, kc)
        m = jnp.maximum(jnp.max(s_w, axis=-1, keepdims=True), jnp.max(s_c, axis=-1, keepdims=True))
        p_w = jnp.exp(s_w - m)
        p_c = jnp.exp(s_c - m)
        l = jnp.sum(p_w, axis=-1, keepdims=True) + jnp.sum(p_c, axis=-1, keepdims=True)
        o = _dot(p_w.astype(BF16), vw) + _dot(p_c.astype(BF16), vc)
        outs.append(o / l)
    o_ref[...] = jnp.where(lower, outs[0], outs[1]).astype(BF16)


def _neighbourhood_attention(qa, ka, va, bias, n_lat):
    b = qa.shape[0]
    q_tok = NA_QROWS * GRID_W
    v_tok = q_tok // 2
    n_rb = n_lat // q_tok
    n_view = n_lat // v_tok
    ctx_blk = n_lat // v_tok

    def view(j):
        return lambda rb, hp, bi: (bi, jnp.clip(2 * rb - 1 + j, 0, n_view - 1), hp)

    kv_specs = [pl.BlockSpec((None, v_tok, LANES), view(j)) for j in range(4)]
    ctx_spec = pl.BlockSpec((None, CTX_LEN, LANES), lambda rb, hp, bi: (bi, ctx_blk, hp))

    def bias_map(rb, hp, bi):
        pat = jnp.where(rb == 0, 0, jnp.where(rb == n_rb - 1, 2, 1))
        return (pat, hp, 0, 0)

    return pl.pallas_call(
        _na_kernel,
        out_shape=jax.ShapeDtypeStruct((b, n_lat, W_A), BF16),
        grid=(n_rb, NA_HEADS // 2, b),
        in_specs=[pl.BlockSpec((None, q_tok, LANES), lambda rb, hp, bi: (bi, rb, hp))]
                 + kv_specs + kv_specs + [ctx_spec, ctx_spec,
                 pl.BlockSpec((None, 2, q_tok, NA_KROWS * GRID_W), bias_map)],
        out_specs=pl.BlockSpec((None, q_tok, LANES), lambda rb, hp, bi: (bi, rb, hp)),
        compiler_params=_cparams(3),
        name="neighbourhood_attention",
    )(qa, ka, ka, ka, ka, va, va, va, va, ka, va, bias)


def _na_bias_table(rpb, rows):
    cols = np.arange(GRID_W)
    c0 = np.clip(cols - NA_WIN_W // 2, 0, GRID_W - NA_WIN_W)
    cc = cols[None, :]
    col_ok = (cc >= c0[:, None]) & (cc < c0[:, None] + NA_WIN_W)
    dc = np.clip(cc - cols[:, None] + (NA_WIN_W - 1), 0, 2 * NA_WIN_W - 2)
    e = jnp.where(col_ok[None, None], rpb.astype(F32)[:, :, dc], NEG)
    e = jnp.concatenate([e, jnp.full_like(e[:, :1], NEG)], axis=1)
    a = np.arange(NA_QROWS)[:, None]
    i = np.arange(NA_KROWS)[None, :]
    pats = []
    for r_base in (0, NA_QROWS, rows - NA_QROWS):
        r = r_base + a
        key_row = r_base - NA_WIN_H // 2 + i
        r0 = np.clip(r - NA_WIN_H // 2, 0, rows - NA_WIN_H)
        ok = (key_row >= r0) & (key_row < r0 + NA_WIN_H) & (key_row >= 0) & (key_row < rows)
        dr = np.where(ok, key_row - r + (NA_WIN_H - 1), 2 * NA_WIN_H - 1)
        pats.append(dr)
    dr_all = np.stack(pats)
    t = e[:, dr_all]
    t = t.transpose(1, 0, 2, 4, 3, 5)
    return t.reshape(3, NA_HEADS, NA_QROWS * GRID_W, NA_KROWS * GRID_W)


def _outproj_kernel(x_ref, oa_ref, ob_ref, oc_ref, wa_ref, wb_ref, wc_ref, mod_ref, gain_ref,
                    wrh_ref, wrl_ref, br_ref, x1_ref, tok_ref, comb_ref):
    mod = mod_ref[...]
    y = _dot(oa_ref[...], wa_ref[...]) + _dot(ob_ref[...], wb_ref[...]) + _dot(oc_ref[...], wc_ref[...])
    x1 = x_ref[...] + mod[2:3] * y
    x1_ref[...] = x1
    ms = jnp.mean(x1 * x1, axis=-1, keepdims=True)
    t = (x1 * lax.rsqrt(ms + EPS)) * gain_ref[...]
    t = t * (1.0 + mod[4:5]) + mod[3:4]
    tok_ref[...] = t.astype(BF16)

    t_hi, t_lo = _split_bf16(t)
    wrh = wrh_ref[...]
    logits = _dot(t_hi, wrh) + _dot(t_lo, wrh) + _dot(t_hi, wrl_ref[...]) + br_ref[...]

    lane = lax.broadcasted_iota(jnp.int32, logits.shape, 1)
    lane_f = lane.astype(F32)
    is_g = lane < N_GROUPS
    gl = jnp.where(is_g, logits, NEG)
    gmax = jnp.max(gl, axis=-1, keepdims=True)
    g_sel = jnp.min(jnp.where(gl == gmax, lane_f, 1e9), axis=-1, keepdims=True)
    p_grp = 1.0 / jnp.sum(jnp.where(is_g, jnp.exp(gl - gmax), 0.0), axis=-1, keepdims=True)
    grp_of_lane = lax.shift_right_arithmetic(lane - N_GROUPS, 2).astype(F32)
    in_grp = (lane >= N_GROUPS) & (lane < N_GROUPS + N_EXPERTS) & (grp_of_lane == g_sel)
    el = jnp.where(in_grp, logits, NEG)
    v1 = jnp.max(el, axis=-1, keepdims=True)
    i1 = jnp.min(jnp.where(el == v1, lane_f, 1e9), axis=-1, keepdims=True)
    el2 = jnp.where(lane_f == i1, NEG, el)
    v2 = jnp.max(el2, axis=-1, keepdims=True)
    i2 = jnp.min(jnp.where(el2 == v2, lane_f, 1e9), axis=-1, keepdims=True)
    e2 = jnp.exp(v2 - v1)
    den = 1.0 + e2
    w1 = p_grp / den
    w2 = p_grp * e2 / den
    comb_ref[...] = jnp.where(lane_f == i1, w1, 0.0) + jnp.where(lane_f == i2, w2, 0.0)


def _out_projection(xs, oa, ob, oc, wa, wb, wc, modsel, gain, wrh, wrl, br, n_tiles, n_lat_tiles):
    b, _, d = xs.shape
    tok = lambda bi, ti: (bi, ti, 0)
    const2 = lambda bi, ti: (0, 0)
    rows = n_tiles * TILE
    return pl.pallas_call(
        _outproj_kernel,
        out_shape=[jax.ShapeDtypeStruct((b, rows, d), F32),
                   jax.ShapeDtypeStruct((b, rows, d), BF16),
                   jax.ShapeDtypeStruct((b, rows, LANES), F32)],
        grid=(b, n_tiles),
        in_specs=[pl.BlockSpec((None, TILE, d), tok),
                  pl.BlockSpec((None, TILE, W_A), tok),
                  pl.BlockSpec((None, TILE, W_B), tok),
                  pl.BlockSpec((None, TILE, W_C), tok),
                  pl.BlockSpec((W_A, d), const2),
                  pl.BlockSpec((W_B, d), const2),
                  pl.BlockSpec((W_C, d), const2),
                  pl.BlockSpec((None, 6, d), lambda bi, ti: (2 * bi + (ti >= n_lat_tiles).astype(jnp.int32), 0, 0)),
                  pl.BlockSpec((1, d), const2),
                  pl.BlockSpec((d, LANES), const2),
                  pl.BlockSpec((d, LANES), const2),
                  pl.BlockSpec((1, LANES), const2)],
        out_specs=[pl.BlockSpec((None, TILE, d), tok),
                   pl.BlockSpec((None, TILE, d), tok),
                   pl.BlockSpec((None, TILE, LANES), tok)],
        compiler_params=_cparams(2),
        name="out_projection",
    )(xs, oa, ob, oc, wa, wb, wc, modsel, gain, wrh, wrl, br)


def _moe_kernel(tok_ref, comb_ref, x1_ref, wg_ref, wu_ref, wd_ref, mod_ref, fgain_ref, o_ref, acc_ref,
                *, final):
    e = pl.program_id(2)
    t = tok_ref[...]
    hid = jax.nn.silu(_dot(t, wg_ref[...])) * _dot(t, wu_ref[...])
    y = _dot(hid.astype(BF16), wd_ref[...])
    comb = comb_ref[...]
    lane = lax.broadcasted_iota(jnp.int32, comb.shape, 1)
    c = jnp.sum(jnp.where(lane == e + N_GROUPS, comb, 0.0), axis=-1, keepdims=True)

    @pl.when(e == 0)
    def _():
        acc_ref[...] = y * c

    @pl.when(e > 0)
    def _():
        acc_ref[...] += y * c

    @pl.when(e == N_EXPERTS - 1)
    def _():
        x2 = x1_ref[...] + mod_ref[5:6, :] * acc_ref[...]
        if final:
            ms = jnp.mean(x2 * x2, axis=-1, keepdims=True)
            x2 = (x2 * lax.rsqrt(ms + EPS)) * fgain_ref[...]
        o_ref[...] = x2


def _moe(tok, comb, x1, wg, wu, wd, modsel, fgain, *, tm, tile_off, n_t, is_ctx, final):
    b, _, d = x1.shape
    tokm = lambda bi, ti, e: (bi, ti + tile_off, 0)
    return pl.pallas_call(
        functools.partial(_moe_kernel, final=final),
        out_shape=jax.ShapeDtypeStruct((b, n_t * tm, d), F32),
        grid=(b, n_t, N_EXPERTS),
        in_specs=[pl.BlockSpec((None, tm, d), tokm),
                  pl.BlockSpec((None, tm, LANES), tokm),
                  pl.BlockSpec((None, tm, d), tokm),
                  pl.BlockSpec((None, d, EXPERT_HIDDEN), lambda bi, ti, e: (e, 0, 0)),
                  pl.BlockSpec((None, d, EXPERT_HIDDEN), lambda bi, ti, e: (e, 0, 0)),
                  pl.BlockSpec((None, EXPERT_HIDDEN, d), lambda bi, ti, e: (e, 0, 0)),
                  pl.BlockSpec((None, 6, d), lambda bi, ti, e: (2 * bi + is_ctx, 0, 0)),
                  pl.BlockSpec((1, d), lambda bi, ti, e: (0, 0))],
        out_specs=pl.BlockSpec((None, tm, d), lambda bi, ti, e: (bi, ti, 0)),
        scratch_shapes=[pltpu.VMEM((tm, d), F32)],
        compiler_params=_cparams(3),
        name="moe_dense",
    )(tok, comb, x1, wg, wu, wd, modsel, fgain)


def _rot_perm(width, head):
    i = np.arange(width)
    half = head // 2
    first = (i % head) < half
    idx = np.where(first, i + half, i - half)
    sign = np.where(first, -1.0, 1.0).astype(np.float32)
    return idx, sign


def _rope_tables(n_lat):
    t = jnp.arange(n_lat)
    row = (t // GRID_W).astype(F32)
    col = (t % GRID_W).astype(F32)

    def cs(dim):
        quarter = dim // 4
        freqs = ROPE_THETA ** (-jnp.arange(quarter, dtype=F32) / quarter)
        ang = jnp.concatenate([row[:, None] * freqs, col[:, None] * freqs], axis=-1)
        cos = jnp.tile(jnp.cos(ang), (1, 2 * LANES // dim))
        sin = jnp.tile(jnp.sin(ang), (1, 2 * LANES // dim))
        cos = jnp.concatenate([cos, jnp.ones((CTX_LEN, LANES), F32)], axis=0)
        sin = jnp.concatenate([sin, jnp.zeros((CTX_LEN, LANES), F32)], axis=0)
        return cos, sin

    cos_b, sin_b = cs(DIFF_QK_DIM)
    cos_c, sin_c = cs(HEAD_DIM)
    return jnp.concatenate([cos_b, sin_b, cos_c, sin_c], axis=1)


def _extended_w_in(w_in):
    o_b = 3 * W_A
    o_c = o_b + 3 * W_B
    qc_cols = o_c + np.concatenate([h * HEAD_DIM + np.arange(HEAD_DIM) for h in GQA_Q_ORDER])
    cols = np.concatenate([np.arange(o_c), qc_cols, np.arange(o_c + W_C, IN_WIDTH)])
    w = w_in[:, cols]
    idx_b, sign_b = _rot_perm(W_B, DIFF_QK_DIM)
    idx_c, sign_c = _rot_perm(W_C, HEAD_DIM)
    idx_k, sign_k = _rot_perm(W_KC, HEAD_DIM)
    rot = jnp.concatenate([
        w[:, o_b + idx_b] * sign_b, w[:, o_b + W_B + idx_b] * sign_b,
        w[:, o_c + idx_c] * sign_c, w[:, o_c + W_C + idx_k] * sign_k], axis=1)
    return jnp.concatenate([w, rot], axis=1).astype(BF16)


def kernel(x, c, ctx, c_ctx, w_mod, b_mod, norm_attn, norm_ffn, w_in, w_out, na_rpb, diff_lambda_q1, diff_lambda_k1, diff_lambda_q2, diff_lambda_k2, diff_subln, gqa_q_norm, gqa_k_norm, router_group_w, router_group_b, router_expert_w, router_expert_b, w_gate, w_up, w_down, final_norm):
    b, s, d = x.shape
    assert d == D_MODEL and ctx.shape[1] == CTX_LEN and s % (NA_QROWS * GRID_W) == 0
    rows = s // GRID_W
    assert rows >= 2 * NA_QROWS
    t_all = s + CTX_LEN
    n_lat_tiles = s // TILE

    c_rows = jnp.zeros((8, d), F32).at[:b].set(c).at[b].set(c_ctx)
    mod = _modulation(c_rows, w_mod, b_mod)

    tab = _rope_tables(s)
    hidx = np.arange(HEAD_DIM)
    partner = np.where(hidx < HEAD_DIM // 2, hidx + HEAD_DIM // 2, hidx - HEAD_DIM // 2)
    blk = np.arange(W_C) // HEAD_DIM
    ones = jnp.asarray((blk[:, None] == blk[None, :]).astype(np.float32), BF16)
    oc_rows = W_A + W_B + np.concatenate([h * HEAD_DIM + np.arange(HEAD_DIM) for h in GQA_Q_ORDER])
    dummy_aux = jnp.zeros((8, LANES), F32)

    xs = jnp.concatenate([x, ctx], axis=1)
    for l in range(DEPTH):
        ctx_out = l < DEPTH - 1
        lam_init = 0.8 - 0.6 * math.exp(-0.3 * l)
        m_lat = mod[l, :b].reshape(b, 1, 6, d)
        m_ctx = jnp.broadcast_to(mod[l, b].reshape(1, 1, 6, d), (b, 1, 6, d))
        modsel = jnp.concatenate([m_lat, m_ctx], axis=1).reshape(2 * b, 6, d)

        gq = jnp.stack([jnp.tile(gqa_q_norm[l], GQA_Q_HEADS), jnp.tile(gqa_q_norm[l][partner], GQA_Q_HEADS)])
        gk = jnp.stack([jnp.tile(gqa_k_norm[l], GQA_KV_HEADS), jnp.tile(gqa_k_norm[l][partner], GQA_KV_HEADS)])
        qa, ka, va, qb, kb, vb, qc, kc, vc = _in_projection(
            xs, modsel, norm_attn[l][None], _extended_w_in(w_in[l]), tab, gq, gk, ones, n_lat_tiles)

        n_qt = n_lat_tiles + 1 if ctx_out else n_lat_tiles
        ctx_tile = n_lat_tiles if ctx_out else None
        oa = _neighbourhood_attention(qa, ka, va, _na_bias_table(na_rpb[l], rows), s)
        if ctx_out:
            oa_ctx = _flash(qa, ka, va, dummy_aux, n_qblk=1, n_sub=2, n_hp=NA_HEADS // 2,
                            qt_off=n_lat_tiles, n_qt=1, n_lat=s, ctx_tile=n_lat_tiles,
                            mode="plain", use_exp2=False)
            oa = jnp.concatenate([oa, oa_ctx], axis=1)
        pad = lambda v: jnp.pad(v, (0, LANES - v.shape[0]))
        aux = jnp.stack([pad(diff_lambda_q1[l]), pad(diff_lambda_k1[l]), pad(diff_lambda_q2[l]),
                         pad(diff_lambda_k2[l]), jnp.tile(diff_subln[l], 2),
                         jnp.zeros((LANES,), F32), jnp.zeros((LANES,), F32), jnp.zeros((LANES,), F32)])
        ob = _flash(qb, kb, vb, aux, n_qblk=1, n_sub=4, n_hp=DIFF_HEADS // 2, qt_off=0, n_qt=n_qt,
                    n_lat=s, ctx_tile=ctx_tile, mode="diff", use_exp2=True, lam_init=lam_init)
        oc = _flash(qc, kc, vc, dummy_aux, n_qblk=3, n_sub=2, n_hp=1, qt_off=0, n_qt=n_qt,
                    n_lat=s, ctx_tile=ctx_tile, mode="plain", use_exp2=True)

        w_o = w_out[l]
        wr = jnp.zeros((d, LANES), F32)
        wr = wr.at[:, :N_GROUPS].set(router_group_w[l]).at[:, N_GROUPS:N_GROUPS + N_EXPERTS].set(router_expert_w[l])
        wrh, wrl = _split_bf16(wr)
        br = jnp.zeros((1, LANES), F32)
        br = br.at[0, :N_GROUPS].set(router_group_b[l]).at[0, N_GROUPS:N_GROUPS + N_EXPERTS].set(router_expert_b[l])
        x1, tok, comb = _out_projection(
            xs, oa, ob, oc, w_o[:W_A].astype(BF16), w_o[W_A:W_A + W_B].astype(BF16), w_o[oc_rows].astype(BF16),
            modsel, norm_ffn[l][None], wrh, wrl, br, n_qt, n_lat_tiles)

        tm = next(t for t in (1024, 512) if s % t == 0)
        experts = (w_gate[l].astype(BF16), w_up[l].astype(BF16), w_down[l].astype(BF16))
        x_lat = _moe(tok, comb, x1, *experts, modsel, final_norm[None], tm=tm, tile_off=0,
                     n_t=s // tm, is_ctx=0, final=not ctx_out)
        if not ctx_out:
            return x_lat
        x_ctx = _moe(tok, comb, x1, *experts, modsel, final_norm[None], tm=TILE, tile_off=n_lat_tiles,
                     n_t=1, is_ctx=1, final=False)
        xs = jnp.concatenate([x_lat, x_ctx], axis=1)
```

```python
import functools
import math

import numpy as np
import jax
import jax.numpy as jnp
from jax import lax
from jax.experimental import pallas as pl
from jax.experimental.pallas import tpu as pltpu
from jax.experimental.pallas import tpu_sc as plsc

F32 = jnp.float32
BF16 = jnp.bfloat16

D_MODEL = 1024
DEPTH = 2
GRID_W = 64
CTX_LEN = 256
HEAD_DIM = 64
NA_HEADS = 6
NA_WIN_H = 8
NA_WIN_W = 16
DIFF_HEADS = 4
DIFF_QK_DIM = 32
GQA_Q_HEADS = 6
GQA_KV_HEADS = 2
N_GROUPS = 4
EXPERTS_PER_GROUP = 4
N_EXPERTS = 16
EXPERT_HIDDEN = 512
ROPE_THETA = 10000.0
EPS = 1e-6
W_A = NA_HEADS * HEAD_DIM
W_B = DIFF_HEADS * 2 * DIFF_QK_DIM
W_C = GQA_Q_HEADS * HEAD_DIM
W_KC = GQA_KV_HEADS * HEAD_DIM
IN_WIDTH = 3 * W_A + 3 * W_B + W_C + 2 * W_KC
ROT_WIDTH = 2 * W_B + W_C + W_KC
EXT_WIDTH = IN_WIDTH + ROT_WIDTH

LANES = 128
TILE = CTX_LEN
NA_QROWS = 8
NA_KROWS = 16
NEG = -1e30
LOG2E = 1.4426950408889634
VMEM_LIMIT = 56 * 1024 * 1024
PAIRS_PER_STEP = 2
MOE_TILE = 512
SC_ROWS = 32
SC_CORES = 2
SC_SUBCORES = 16

GQA_Q_ORDER = (0, 3, 1, 4, 2, 5)


def _cparams(n_axes):
    return pltpu.CompilerParams(dimension_semantics=("arbitrary",) * n_axes,
                                vmem_limit_bytes=VMEM_LIMIT)


def _split_bf16(a):
    hi = a.astype(BF16)
    lo = (a - hi.astype(F32)).astype(BF16)
    return hi, lo


def _dot(a, b):
    return jnp.dot(a, b, preferred_element_type=F32)


def _dot_nt(a, b):
    return lax.dot_general(a, b, (((1,), (1,)), ((), ())), preferred_element_type=F32)


def _mod_kernel(c_ref, w_ref, b_ref, o_ref):
    c = c_ref[...]
    a = c * jax.nn.sigmoid(c)
    a_hi, a_lo = _split_bf16(a)
    w_hi, w_lo = _split_bf16(w_ref[...])
    o_ref[...] = _dot(a_hi, w_hi) + _dot(a_lo, w_hi) + _dot(a_hi, w_lo) + b_ref[...]


def _modulation(c_rows, w_mod, b_mod):
    depth, d, n = w_mod.shape
    bn = 1536
    return pl.pallas_call(
        _mod_kernel,
        out_shape=jax.ShapeDtypeStruct((depth, 8, n), F32),
        grid=(depth, n // bn),
        in_specs=[pl.BlockSpec((8, d), lambda l, j: (0, 0)),
                  pl.BlockSpec((None, d, bn), lambda l, j: (l, 0, j)),
                  pl.BlockSpec((None, 1, bn), lambda l, j: (l, 0, j))],
        out_specs=pl.BlockSpec((None, 8, bn), lambda l, j: (l, 0, j)),
        compiler_params=_cparams(2),
        name="adaln_mod",
    )(c_rows, w_mod, b_mod.reshape(depth, 1, n))


def _head_mean_sq(t, ones):
    hi, lo = _split_bf16(t * t)
    return (_dot(hi, ones) + _dot(lo, ones)) * (1.0 / HEAD_DIM)


def _inproj_kernel(x_ref, mod_ref, gain_ref, w_ref, tab_ref, gq_ref, gk_ref, ones_ref,
                   qa_ref, ka_ref, va_ref, qb_ref, kb_ref, vb_ref, qc_ref, kc_ref, vc_ref):
    x = x_ref[...]
    mod = mod_ref[...]
    ms = jnp.mean(x * x, axis=-1, keepdims=True)
    h = (x * lax.rsqrt(ms + EPS)) * gain_ref[...]
    h = h * (1.0 + mod[1:2]) + mod[0:1]
    hb = h.astype(BF16)

    def proj(a, b):
        return _dot(hb, w_ref[:, a:b])

    pa = proj(0, 3 * W_A)
    qa_ref[...] = (pa[:, :W_A] * (HEAD_DIM ** -0.5)).astype(BF16)
    ka_ref[...] = pa[:, W_A:2 * W_A].astype(BF16)
    va_ref[...] = pa[:, 2 * W_A:].astype(BF16)

    tab = tab_ref[...]
    cos_b = jnp.concatenate([tab[:, 0:LANES]] * 2, axis=1)
    sin_b = jnp.concatenate([tab[:, LANES:2 * LANES]] * 2, axis=1)
    cos_c1 = tab[:, 2 * LANES:3 * LANES]
    sin_c1 = tab[:, 3 * LANES:4 * LANES]
    cos_c = jnp.concatenate([cos_c1] * 3, axis=1)
    sin_c = jnp.concatenate([sin_c1] * 3, axis=1)

    o_b = 3 * W_A
    pb = proj(o_b, o_b + 3 * W_B)
    pbr = proj(IN_WIDTH, IN_WIDTH + 2 * W_B)
    qb = pb[:, :W_B] * cos_b + pbr[:, :W_B] * sin_b
    qb_ref[...] = (qb * (DIFF_QK_DIM ** -0.5 * LOG2E)).astype(BF16)
    kb_ref[...] = (pb[:, W_B:2 * W_B] * cos_b + pbr[:, W_B:] * sin_b).astype(BF16)
    vb_ref[...] = pb[:, 2 * W_B:].astype(BF16)

    o_c = o_b + 3 * W_B
    pc = proj(o_c, IN_WIDTH)
    pcr = proj(IN_WIDTH + 2 * W_B, EXT_WIDTH)
    ones = ones_ref[...]
    qc = pc[:, :W_C]
    kc = pc[:, W_C:W_C + W_KC]
    nq = lax.rsqrt(_head_mean_sq(qc, ones) + EPS)
    nk = lax.rsqrt(_head_mean_sq(kc, ones[:W_KC, :W_KC]) + EPS)
    gq = gq_ref[...]
    gk = gk_ref[...]
    q = nq * (qc * gq[0:1] * cos_c + pcr[:, :W_C] * gq[1:2] * sin_c)
    qc_ref[...] = (q * (HEAD_DIM ** -0.5 * LOG2E)).astype(BF16)
    k = nk * (kc * gk[0:1] * cos_c1 + pcr[:, W_C:] * gk[1:2] * sin_c1)
    kc_ref[...] = k.astype(BF16)
    vc_ref[...] = pc[:, W_C + W_KC:].astype(BF16)


def _in_projection(xs, modsel, gain, w_ext, tab, gq, gk, ones, n_lat_tiles):
    b, t_all, d = xs.shape
    n_tiles = t_all // TILE
    widths = (W_A, W_A, W_A, W_B, W_B, W_B, W_C, W_KC, W_KC)
    tok = lambda bi, ti: (bi, ti, 0)
    const2 = lambda bi, ti: (0, 0)
    return pl.pallas_call(
        _inproj_kernel,
        out_shape=[jax.ShapeDtypeStruct((b, t_all, w), BF16) for w in widths],
        grid=(b, n_tiles),
        in_specs=[pl.BlockSpec((None, TILE, d), tok),
                  pl.BlockSpec((None, 6, d), lambda bi, ti: (2 * bi + (ti >= n_lat_tiles).astype(jnp.int32), 0, 0)),
                  pl.BlockSpec((1, d), const2),
                  pl.BlockSpec((d, EXT_WIDTH), const2),
                  pl.BlockSpec((TILE, 4 * LANES), lambda bi, ti: (ti, 0)),
                  pl.BlockSpec((2, W_C), const2),
                  pl.BlockSpec((2, W_KC), const2),
                  pl.BlockSpec((W_C, W_C), const2)],
        out_specs=[pl.BlockSpec((None, TILE, w), tok) for w in widths],
        compiler_params=_cparams(2),
        name="in_projection",
    )(xs, modsel, gain, w_ext, tab, gq, gk, ones)


def _flash_kernel(q_ref, k_ref, v_ref, aux_ref, o_ref, va_ref, vb_ref, qs_ref, acc_ref, m_ref,
                  s0_ref, s1_ref, mb0_ref, mb1_ref, *,
                  n_qblk, n_sub, tk, n_lat_blocks, ctx_tile, qt_off, mode, use_exp2, lam_init):
    exp_fn = jnp.exp2 if use_exp2 else jnp.exp
    qt = pl.program_id(2) + qt_off
    sub_w = LANES // n_sub
    half = LANES // 2
    lane = lax.broadcasted_iota(jnp.int32, (1, LANES), 1)
    lower = lane < half
    n_pieces = n_qblk * n_sub
    ma = (n_pieces // 2) * TILE
    m_rows = n_pieces * TILE
    ctx_start = n_lat_blocks * tk

    @pl.when(pl.program_id(2) == 0)
    def _():
        v = v_ref[...].astype(F32)
        va_ref[...] = jnp.where(lower, v, 1.0).astype(BF16)
        vb_ref[...] = jnp.where(lower, 1.0, v).astype(BF16)

    ia, ib = 0, n_pieces // 2
    for blk in range(n_qblk):
        qf = q_ref[:, blk * LANES:(blk + 1) * LANES].astype(F32)
        for sub in range(n_sub):
            msk = (lane >= sub * sub_w) & (lane < (sub + 1) * sub_w)
            piece = jnp.where(msk, qf, 0.0).astype(BF16)
            if sub * sub_w < half:
                qs_ref[ia * TILE:(ia + 1) * TILE, :] = piece
                ia += 1
            else:
                qs_ref[ib * TILE:(ib + 1) * TILE, :] = piece
                ib += 1

    s_bufs = (s0_ref, s1_ref)
    mb_bufs = (mb0_ref, mb1_ref)

    def scores(start, size, slot):
        s = _dot_nt(qs_ref[...], k_ref[pl.ds(start, size), :])
        s_bufs[slot][:, :size] = s
        mb = jnp.max(s, axis=-1, keepdims=True)
        mb_bufs[slot][...] = jnp.broadcast_to(mb, (m_rows, LANES))

    def accumulate(start, size, slot, first):
        mb = mb_bufs[slot][...]
        if first:
            m_new = mb
        else:
            m_old = m_ref[...]
            m_new = jnp.maximum(m_old, mb)
        s_ref = s_bufs[slot]
        cols = [s_ref[:, c * LANES:(c + 1) * LANES] - m_new for c in range(size // LANES)]
        if use_exp2:
            p = jnp.concatenate([jnp.exp2(d.astype(BF16)) for d in cols], axis=1)
        else:
            p = jnp.concatenate([jnp.exp(d).astype(BF16) for d in cols], axis=1)
        pva = _dot(p[:ma], va_ref[pl.ds(start, size), :])
        pvb = _dot(p[ma:], vb_ref[pl.ds(start, size), :])
        if first:
            acc_ref[:ma, :] = pva
            acc_ref[ma:, :] = pvb
        else:
            alpha = exp_fn(m_old - m_new)
            acc_ref[:ma, :] = alpha[:ma] * acc_ref[:ma, :] + pva
            acc_ref[ma:, :] = alpha[ma:] * acc_ref[ma:, :] + pvb
        m_ref[...] = m_new

    def lat(j):
        return pl.multiple_of(j * tk, tk)

    def latent_queries():
        scores(ctx_start, CTX_LEN, 0)
        scores(lat(0), tk, 1)
        accumulate(ctx_start, CTX_LEN, 0, True)

        def pair(i):
            scores(lat(2 * i + 1), tk, 0)
            accumulate(lat(2 * i), tk, 1, False)
            scores(lat(2 * i + 2), tk, 1)
            accumulate(lat(2 * i + 1), tk, 0, False)

        def body(i, carry):
            for u in range(PAIRS_PER_STEP):
                pair(i * PAIRS_PER_STEP + u)
            return carry

        n_pairs = (n_lat_blocks - 2) // 2
        n_steps = n_pairs // PAIRS_PER_STEP
        lax.fori_loop(0, n_steps, body, 0)
        for i in range(n_steps * PAIRS_PER_STEP, n_pairs):
            pair(i)
        scores(lat(n_lat_blocks - 1), tk, 0)
        accumulate(lat(n_lat_blocks - 2), tk, 1, False)
        accumulate(lat(n_lat_blocks - 1), tk, 0, False)

    def context_queries():
        scores(ctx_start, CTX_LEN, 0)
        accumulate(ctx_start, CTX_LEN, 0, True)

    if ctx_tile is None:
        latent_queries()
    else:
        pl.when(qt != ctx_tile)(latent_queries)
        pl.when(qt == ctx_tile)(context_queries)

    acc = acc_ref[...]
    r = acc / pltpu.roll(acc, half, 1)
    ra, rb = r[:ma], r[ma:]
    if mode == "plain":
        for i in range(n_pieces // 2):
            o = jnp.where(lower, ra[i * TILE:(i + 1) * TILE], rb[i * TILE:(i + 1) * TILE])
            o_ref[:, i * LANES:(i + 1) * LANES] = o.astype(BF16)
    else:
        aux = aux_ref[...]
        l1 = jnp.sum(aux[0:1] * aux[1:2], axis=-1, keepdims=True)
        l2 = jnp.sum(aux[2:3] * aux[3:4], axis=-1, keepdims=True)
        lam = jnp.exp(l1) - jnp.exp(l2) + lam_init
        oa = ra[:TILE] - lam * ra[TILE:]
        ob = rb[:TILE] - lam * rb[TILE:]
        o = jnp.where(lower, oa, ob)
        sq = o * o
        ss_a = jnp.sum(jnp.where(lower, sq, 0.0), axis=-1, keepdims=True)
        ss_b = jnp.sum(jnp.where(lower, 0.0, sq), axis=-1, keepdims=True)
        ms = jnp.where(lower, ss_a, ss_b) * (1.0 / HEAD_DIM)
        o = (o * lax.rsqrt(ms + EPS)) * aux[4:5]
        o_ref[...] = (o * (1.0 - lam_init)).astype(BF16)


def _flash(q, k, v, aux, *, n_qblk, n_sub, n_hp, qt_off, n_qt, n_lat, ctx_tile, mode, use_exp2,
           lam_init=0.0):
    b, t_all, _ = q.shape
    qw = n_qblk * LANES
    tk = 512
    assert n_lat % (2 * tk) == 0 and tk >= CTX_LEN
    m_rows = n_qblk * n_sub * TILE
    kern = functools.partial(_flash_kernel, n_qblk=n_qblk, n_sub=n_sub, tk=tk,
                             n_lat_blocks=n_lat // tk, ctx_tile=ctx_tile, qt_off=qt_off,
                             mode=mode, use_exp2=use_exp2, lam_init=lam_init)
    return pl.pallas_call(
        kern,
        out_shape=jax.ShapeDtypeStruct((b, n_qt * TILE, n_hp * qw), BF16),
        grid=(b, n_hp, n_qt),
        in_specs=[pl.BlockSpec((None, TILE, qw), lambda bi, hp, qt: (bi, qt + qt_off, hp)),
                  pl.BlockSpec((None, t_all, LANES), lambda bi, hp, qt: (bi, 0, hp)),
                  pl.BlockSpec((None, t_all, LANES), lambda bi, hp, qt: (bi, 0, hp)),
                  pl.BlockSpec((8, LANES), lambda bi, hp, qt: (0, 0))],
        out_specs=pl.BlockSpec((None, TILE, qw), lambda bi, hp, qt: (bi, qt, hp)),
        scratch_shapes=[pltpu.VMEM((t_all, LANES), BF16),
                        pltpu.VMEM((t_all, LANES), BF16),
                        pltpu.VMEM((m_rows, LANES), BF16),
                        pltpu.VMEM((m_rows, LANES), F32),
                        pltpu.VMEM((m_rows, LANES), F32),
                        pltpu.VMEM((m_rows, tk), F32),
                        pltpu.VMEM((m_rows, tk), F32),
                        pltpu.VMEM((m_rows, LANES), F32),
                        pltpu.VMEM((m_rows, LANES), F32)],
        compiler_params=_cparams(3),
        name="flash_" + mode,
    )(q, k, v, aux)


def _na_kernel(q_ref, k0, k1, k2, k3, v0, v1, v2, v3, kc_ref, vc_ref, bias_ref, o_ref):
    lane = lax.broadcasted_iota(jnp.int32, (1, LANES), 1)
    lower = lane < LANES // 2
    qf = q_ref[...].astype(F32)
    kw = jnp.concatenate([k0[...], k1[...], k2[...], k3[...]], axis=0)
    vw = jnp.concatenate([v0[...], v1[...], v2[...], v3[...]], axis=0)
    kc = kc_ref[...]
    vc = vc_ref[...]
    outs = []
    for hh in range(2):
        msk = lower if hh == 0 else jnp.logical_not(lower)
        qh = jnp.where(msk, qf, 0.0).astype(BF16)
        s_w = _dot_nt(qh, kw) + bias_ref[hh]
        s_c = _dot_nt(qh, kc)
        m = jnp.maximum(jnp.max(s_w, axis=-1, keepdims=True), jnp.max(s_c, axis=-1, keepdims=True))
        p_w = jnp.exp(s_w - m)
        p_c = jnp.exp(s_c - m)
        l = jnp.sum(p_w, axis=-1, keepdims=True) + jnp.sum(p_c, axis=-1, keepdims=True)
        o = _dot(p_w.astype(BF16), vw) + _dot(p_c.astype(BF16), vc)
        outs.append(o / l)
    o_ref[...] = jnp.where(lower, outs[0], outs[1]).astype(BF16)


def _neighbourhood_attention(qa, ka, va, bias, n_lat):
    b = qa.shape[0]
    q_tok = NA_QROWS * GRID_W
    v_tok = q_tok // 2
    n_rb = n_lat // q_tok
    n_view = n_lat // v_tok
    ctx_blk = n_lat // v_tok

    def view(j):
        return lambda rb, hp, bi: (bi, jnp.clip(2 * rb - 1 + j, 0, n_view - 1), hp)

    kv_specs = [pl.BlockSpec((None, v_tok, LANES), view(j)) for j in range(4)]
    ctx_spec = pl.BlockSpec((None, CTX_LEN, LANES), lambda rb, hp, bi: (bi, ctx_blk, hp))

    def bias_map(rb, hp, bi):
        pat = jnp.where(rb == 0, 0, jnp.where(rb == n_rb - 1, 2, 1))
        return (pat, hp, 0, 0)

    return pl.pallas_call(
        _na_kernel,
        out_shape=jax.ShapeDtypeStruct((b, n_lat, W_A), BF16),
        grid=(n_rb, NA_HEADS // 2, b),
        in_specs=[pl.BlockSpec((None, q_tok, LANES), lambda rb, hp, bi: (bi, rb, hp))]
                 + kv_specs + kv_specs + [ctx_spec, ctx_spec,
                 pl.BlockSpec((None, 2, q_tok, NA_KROWS * GRID_W), bias_map)],
        out_specs=pl.BlockSpec((None, q_tok, LANES), lambda rb, hp, bi: (bi, rb, hp)),
        compiler_params=_cparams(3),
        name="neighbourhood_attention",
    )(qa, ka, ka, ka, ka, va, va, va, va, ka, va, bias)


def _na_bias_table(rpb, rows):
    cols = np.arange(GRID_W)
    c0 = np.clip(cols - NA_WIN_W // 2, 0, GRID_W - NA_WIN_W)
    cc = cols[None, :]
    col_ok = (cc >= c0[:, None]) & (cc < c0[:, None] + NA_WIN_W)
    dc = np.clip(cc - cols[:, None] + (NA_WIN_W - 1), 0, 2 * NA_WIN_W - 2)
    e = jnp.where(col_ok[None, None], rpb.astype(F32)[:, :, dc], NEG)
    e = jnp.concatenate([e, jnp.full_like(e[:, :1], NEG)], axis=1)
    a = np.arange(NA_QROWS)[:, None]
    i = np.arange(NA_KROWS)[None, :]
    pats = []
    for r_base in (0, NA_QROWS, rows - NA_QROWS):
        r = r_base + a
        key_row = r_base - NA_WIN_H // 2 + i
        r0 = np.clip(r - NA_WIN_H // 2, 0, rows - NA_WIN_H)
        ok = (key_row >= r0) & (key_row < r0 + NA_WIN_H) & (key_row >= 0) & (key_row < rows)
        dr = np.where(ok, key_row - r + (NA_WIN_H - 1), 2 * NA_WIN_H - 1)
        pats.append(dr)
    dr_all = np.stack(pats)
    t = e[:, dr_all]
    t = t.transpose(1, 0, 2, 4, 3, 5)
    return t.reshape(3, NA_HEADS, NA_QROWS * GRID_W, NA_KROWS * GRID_W)


def _outproj_kernel(x_ref, oa_ref, ob_ref, oc_ref, wa_ref, wb_ref, wc_ref, mod_ref, gain_ref,
                    wrh_ref, wrl_ref, br_ref, tri_ref, x1_ref, tok_ref, route_ref, cnt_ref, run_ref,
                    *, region):
    mod = mod_ref[...]
    y = _dot(oa_ref[...], wa_ref[...]) + _dot(ob_ref[...], wb_ref[...]) + _dot(oc_ref[...], wc_ref[...])
    x1 = x_ref[...] + mod[2:3] * y
    x1_ref[...] = x1
    ms = jnp.mean(x1 * x1, axis=-1, keepdims=True)
    t = (x1 * lax.rsqrt(ms + EPS)) * gain_ref[...]
    t = t * (1.0 + mod[4:5]) + mod[3:4]
    tok_ref[...] = t

    t_hi, t_lo = _split_bf16(t)
    wrh = wrh_ref[...]
    logits = _dot(t_hi, wrh) + _dot(t_lo, wrh) + _dot(t_hi, wrl_ref[...]) + br_ref[...]

    lane = lax.broadcasted_iota(jnp.int32, logits.shape, 1)
    lane_f = lane.astype(F32)
    is_g = lane < N_GROUPS
    gl = jnp.where(is_g, logits, NEG)
    gmax = jnp.max(gl, axis=-1, keepdims=True)
    g_sel = jnp.min(jnp.where(gl == gmax, lane_f, 1e9), axis=-1, keepdims=True)
    p_grp = 1.0 / jnp.sum(jnp.where(is_g, jnp.exp(gl - gmax), 0.0), axis=-1, keepdims=True)
    grp_of_lane = lax.shift_right_arithmetic(lane - N_GROUPS, 2).astype(F32)
    in_grp = (lane >= N_GROUPS) & (lane < N_GROUPS + N_EXPERTS) & (grp_of_lane == g_sel)
    el = jnp.where(in_grp, logits, NEG)
    v1 = jnp.max(el, axis=-1, keepdims=True)
    i1 = jnp.min(jnp.where(el == v1, lane_f, 1e9), axis=-1, keepdims=True)
    el2 = jnp.where(lane_f == i1, NEG, el)
    v2 = jnp.max(el2, axis=-1, keepdims=True)
    i2 = jnp.min(jnp.where(el2 == v2, lane_f, 1e9), axis=-1, keepdims=True)
    e2 = jnp.exp(v2 - v1)
    den = 1.0 + e2
    w1 = p_grp / den
    w2 = p_grp * e2 / den

    @pl.when((pl.program_id(0) == 0) & (pl.program_id(1) == 0))
    def _():
        run_ref[...] = jnp.zeros(run_ref.shape, F32)

    ind = jnp.where(lane_f == i1, 1.0, 0.0) + jnp.where(lane_f == i2, 1.0, 0.0)
    rank = _dot(tri_ref[...], ind.astype(BF16)) + run_ref[0:1, :]

    def pick(m, l):
        return jnp.sum(jnp.where(lane_f == l, m, 0.0), axis=-1, keepdims=True)

    pos1 = (i1 - N_GROUPS) * region + pick(rank, i1)
    pos2 = (i2 - N_GROUPS) * region + pick(rank, i2)
    route_ref[...] = jnp.where(lane == 0, pos1, jnp.where(lane == 1, pos2,
                               jnp.where(lane == 2, w1, jnp.where(lane == 3, w2, 0.0))))
    run = run_ref[...] + jnp.sum(ind, axis=0, keepdims=True)
    run_ref[...] = run
    cnt_ref[...] = run


def _out_projection(xs, oa, ob, oc, wa, wb, wc, modsel, gain, wrh, wrl, br, n_tiles, n_lat_tiles):
    b, _, d = xs.shape
    tok = lambda bi, ti: (bi, ti, 0)
    const2 = lambda bi, ti: (0, 0)
    rows = n_tiles * TILE
    tri = jnp.asarray(np.tril(np.ones((TILE, TILE), np.float32), -1), BF16)
    return pl.pallas_call(
        functools.partial(_outproj_kernel, region=b * rows),
        out_shape=[jax.ShapeDtypeStruct((b, rows, d), F32),
                   jax.ShapeDtypeStruct((b, rows, d), F32),
                   jax.ShapeDtypeStruct((b, rows, LANES), F32),
                   jax.ShapeDtypeStruct((8, LANES), F32)],
        grid=(b, n_tiles),
        in_specs=[pl.BlockSpec((None, TILE, d), tok),
                  pl.BlockSpec((None, TILE, W_A), tok),
                  pl.BlockSpec((None, TILE, W_B), tok),
                  pl.BlockSpec((None, TILE, W_C), tok),
                  pl.BlockSpec((W_A, d), const2),
                  pl.BlockSpec((W_B, d), const2),
                  pl.BlockSpec((W_C, d), const2),
                  pl.BlockSpec((None, 6, d), lambda bi, ti: (2 * bi + (ti >= n_lat_tiles).astype(jnp.int32), 0, 0)),
                  pl.BlockSpec((1, d), const2),
                  pl.BlockSpec((d, LANES), const2),
                  pl.BlockSpec((d, LANES), const2),
                  pl.BlockSpec((1, LANES), const2),
                  pl.BlockSpec((TILE, TILE), const2)],
        out_specs=[pl.BlockSpec((None, TILE, d), tok),
                   pl.BlockSpec((None, TILE, d), tok),
                   pl.BlockSpec((None, TILE, LANES), tok),
                   pl.BlockSpec((8, LANES), const2)],
        scratch_shapes=[pltpu.VMEM((8, LANES), F32)],
        compiler_params=_cparams(2),
        name="out_projection",
    )(xs, oa, ob, oc, wa, wb, wc, modsel, gain, wrh, wrl, br, tri)


def _sc_mesh():
    return plsc.VectorSubcoreMesh(core_axis_name="core", subcore_axis_name="subcore")


def _sc_worker_base(per_worker):
    wid = lax.axis_index("subcore") * SC_CORES + lax.axis_index("core")
    return wid * per_worker


def _sc_scratch(d, dtype):
    return [pltpu.VMEM((SC_ROWS,), jnp.int32), pltpu.VMEM((SC_ROWS, d), dtype), pltpu.SemaphoreType.DMA]


def _sc_scatter_rows(x, idx, n_out):
    n, d = x.shape
    per_worker = 2 * n // (SC_CORES * SC_SUBCORES)
    assert per_worker % SC_ROWS == 0 and n % SC_ROWS == 0

    @functools.partial(pl.kernel, out_type=jax.ShapeDtypeStruct((n_out, d), x.dtype),
                       mesh=_sc_mesh(), scratch_types=_sc_scratch(d, x.dtype))
    def scatter(x_hbm, i_hbm, o_hbm, idx_v, rows_v, sem):
        base = _sc_worker_base(per_worker)

        @pl.loop(0, per_worker // SC_ROWS)
        def _(c):
            a = pl.multiple_of(base + c * SC_ROWS, SC_ROWS)
            t = pl.multiple_of(lax.rem(a, n), SC_ROWS)
            pltpu.sync_copy(i_hbm.at[pl.ds(a, SC_ROWS)], idx_v)
            pltpu.sync_copy(x_hbm.at[pl.ds(t, SC_ROWS)], rows_v)
            pltpu.async_copy(rows_v, o_hbm.at[idx_v], sem).wait()

    return scatter(x, idx)


def _sc_gather_rows(src, idx):
    m = idx.shape[0]
    d = src.shape[1]
    per_worker = m // (SC_CORES * SC_SUBCORES)
    assert per_worker % SC_ROWS == 0

    @functools.partial(pl.kernel, out_type=jax.ShapeDtypeStruct((m, d), src.dtype),
                       mesh=_sc_mesh(), scratch_types=_sc_scratch(d, src.dtype))
    def gather(s_hbm, i_hbm, o_hbm, idx_v, rows_v, sem):
        base = _sc_worker_base(per_worker)

        @pl.loop(0, per_worker // SC_ROWS)
        def _(c):
            a = pl.multiple_of(base + c * SC_ROWS, SC_ROWS)
            pltpu.sync_copy(i_hbm.at[pl.ds(a, SC_ROWS)], idx_v)
            pltpu.async_copy(s_hbm.at[idx_v], rows_v, sem).wait()
            pltpu.sync_copy(rows_v, o_hbm.at[pl.ds(a, SC_ROWS)])

    return gather(src, idx)


def _expert_ffn_kernel(blk_ref, exp_ref, x_ref, wg_ref, wu_ref, wd_ref, y_ref):
    x = x_ref[...].astype(BF16)
    hid = jax.nn.silu(_dot(x, wg_ref[...])) * _dot(x, wu_ref[...])
    y_ref[...] = _dot(hid.astype(BF16), wd_ref[...])


def _expert_ffn(xs, blk, exp, wg, wu, wd):
    rows, d = xs.shape
    return pl.pallas_call(
        _expert_ffn_kernel,
        out_shape=jax.ShapeDtypeStruct((rows, d), F32),
        grid_spec=pltpu.PrefetchScalarGridSpec(
            num_scalar_prefetch=2,
            grid=(blk.shape[0],),
            in_specs=[pl.BlockSpec((MOE_TILE, d), lambda j, blk, exp: (blk[j], 0)),
                      pl.BlockSpec((None, d, EXPERT_HIDDEN), lambda j, blk, exp: (exp[j], 0, 0)),
                      pl.BlockSpec((None, d, EXPERT_HIDDEN), lambda j, blk, exp: (exp[j], 0, 0)),
                      pl.BlockSpec((None, EXPERT_HIDDEN, d), lambda j, blk, exp: (exp[j], 0, 0))],
            out_specs=pl.BlockSpec((MOE_TILE, d), lambda j, blk, exp: (blk[j], 0))),
        compiler_params=_cparams(1),
        name="expert_ffn",
    )(blk, exp, xs, wg, wu, wd)


def _combine_kernel(x1_ref, y1_ref, y2_ref, route_ref, mod_ref, fgain_ref, o_ref, *, final):
    route = route_ref[...]
    y = route[:, 2:3] * y1_ref[...] + route[:, 3:4] * y2_ref[...]
    x2 = x1_ref[...] + mod_ref[5:6, :] * y
    if final:
        ms = jnp.mean(x2 * x2, axis=-1, keepdims=True)
        x2 = (x2 * lax.rsqrt(ms + EPS)) * fgain_ref[...]
    o_ref[...] = x2


def _combine(x1, ys, route, modsel, fgain, n_lat_tiles, final):
    b, rows, d = x1.shape
    n_t = rows // TILE
    tok = lambda bi, ti: (bi, ti, 0)
    return pl.pallas_call(
        functools.partial(_combine_kernel, final=final),
        out_shape=jax.ShapeDtypeStruct((b, rows, d), F32),
        grid=(b, n_t),
        in_specs=[pl.BlockSpec((None, TILE, d), tok),
                  pl.BlockSpec((TILE, d), lambda bi, ti: (bi * n_t + ti, 0)),
                  pl.BlockSpec((TILE, d), lambda bi, ti: (b * n_t + bi * n_t + ti, 0)),
                  pl.BlockSpec((None, TILE, LANES), tok),
                  pl.BlockSpec((None, 6, d), lambda bi, ti: (2 * bi + (ti >= n_lat_tiles).astype(jnp.int32), 0, 0)),
                  pl.BlockSpec((1, d), lambda bi, ti: (0, 0))],
        out_specs=pl.BlockSpec((None, TILE, d), tok),
        compiler_params=_cparams(2),
        name="moe_combine",
    )(x1, ys, ys, route, modsel, fgain)


def _routed_moe(tok, route, cnt, x1, wg, wu, wd, modsel, fgain, n_lat_tiles, final):
    b, rows, d = x1.shape
    n = b * rows
    flat = route.reshape(n, LANES)
    idx = jnp.concatenate([flat[:, 0], flat[:, 1]]).astype(jnp.int32)
    xs = _sc_scatter_rows(tok.reshape(n, d), idx, N_EXPERTS * n)

    counts = cnt[0, N_GROUPS:N_GROUPS + N_EXPERTS].astype(jnp.int32)
    tiles = (counts + MOE_TILE - 1) // MOE_TILE
    ends = jnp.cumsum(tiles)
    n_sched = 2 * n // MOE_TILE + N_EXPERTS
    j = jnp.minimum(jnp.arange(n_sched, dtype=jnp.int32), ends[-1] - 1)
    exp = jnp.sum((j[:, None] >= ends[None, :]).astype(jnp.int32), axis=1)
    blk = exp * (n // MOE_TILE) + j - (ends - tiles)[exp]

    ys = _expert_ffn(xs, blk, exp, wg, wu, wd)
    yg = _sc_gather_rows(ys, idx)
    return _combine(x1, yg, route, modsel, fgain, n_lat_tiles, final)


def _rot_perm(width, head):
    i = np.arange(width)
    half = head // 2
    first = (i % head) < half
    idx = np.where(first, i + half, i - half)
    sign = np.where(first, -1.0, 1.0).astype(np.float32)
    return idx, sign


def _rope_tables(n_lat):
    t = jnp.arange(n_lat)
    row = (t // GRID_W).astype(F32)
    col = (t % GRID_W).astype(F32)

    def cs(dim):
        quarter = dim // 4
        freqs = ROPE_THETA ** (-jnp.arange(quarter, dtype=F32) / quarter)
        ang = jnp.concatenate([row[:, None] * freqs, col[:, None] * freqs], axis=-1)
        cos = jnp.tile(jnp.cos(ang), (1, 2 * LANES // dim))
        sin = jnp.tile(jnp.sin(ang), (1, 2 * LANES // dim))
        cos = jnp.concatenate([cos, jnp.ones((CTX_LEN, LANES), F32)], axis=0)
        sin = jnp.concatenate([sin, jnp.zeros((CTX_LEN, LANES), F32)], axis=0)
        return cos, sin

    cos_b, sin_b = cs(DIFF_QK_DIM)
    cos_c, sin_c = cs(HEAD_DIM)
    return jnp.concatenate([cos_b, sin_b, cos_c, sin_c], axis=1)


def _extended_w_in(w_in):
    o_b = 3 * W_A
    o_c = o_b + 3 * W_B
    qc_cols = o_c + np.concatenate([h * HEAD_DIM + np.arange(HEAD_DIM) for h in GQA_Q_ORDER])
    cols = np.concatenate([np.arange(o_c), qc_cols, np.arange(o_c + W_C, IN_WIDTH)])
    w = w_in[:, cols]
    idx_b, sign_b = _rot_perm(W_B, DIFF_QK_DIM)
    idx_c, sign_c = _rot_perm(W_C, HEAD_DIM)
    idx_k, sign_k = _rot_perm(W_KC, HEAD_DIM)
    rot = jnp.concatenate([
        w[:, o_b + idx_b] * sign_b, w[:, o_b + W_B + idx_b] * sign_b,
        w[:, o_c + idx_c] * sign_c, w[:, o_c + W_C + idx_k] * sign_k], axis=1)
    return jnp.concatenate([w, rot], axis=1).astype(BF16)


def kernel(x, c, ctx, c_ctx, w_mod, b_mod, norm_attn, norm_ffn, w_in, w_out, na_rpb, diff_lambda_q1, diff_lambda_k1, diff_lambda_q2, diff_lambda_k2, diff_subln, gqa_q_norm, gqa_k_norm, router_group_w, router_group_b, router_expert_w, router_expert_b, w_gate, w_up, w_down, final_norm):
    b, s, d = x.shape
    assert d == D_MODEL and ctx.shape[1] == CTX_LEN and s % (NA_QROWS * GRID_W) == 0
    rows = s // GRID_W
    assert rows >= 2 * NA_QROWS
    t_all = s + CTX_LEN
    n_lat_tiles = s // TILE

    c_rows = jnp.zeros((8, d), F32).at[:b].set(c).at[b].set(c_ctx)
    mod = _modulation(c_rows, w_mod, b_mod)

    tab = _rope_tables(s)
    hidx = np.arange(HEAD_DIM)
    partner = np.where(hidx < HEAD_DIM // 2, hidx + HEAD_DIM // 2, hidx - HEAD_DIM // 2)
    blk = np.arange(W_C) // HEAD_DIM
    ones = jnp.asarray((blk[:, None] == blk[None, :]).astype(np.float32), BF16)
    oc_rows = W_A + W_B + np.concatenate([h * HEAD_DIM + np.arange(HEAD_DIM) for h in GQA_Q_ORDER])
    dummy_aux = jnp.zeros((8, LANES), F32)

    xs = jnp.concatenate([x, ctx], axis=1)
    for l in range(DEPTH):
        ctx_out = l < DEPTH - 1
        lam_init = 0.8 - 0.6 * math.exp(-0.3 * l)
        m_lat = mod[l, :b].reshape(b, 1, 6, d)
        m_ctx = jnp.broadcast_to(mod[l, b].reshape(1, 1, 6, d), (b, 1, 6, d))
        modsel = jnp.concatenate([m_lat, m_ctx], axis=1).reshape(2 * b, 6, d)

        gq = jnp.stack([jnp.tile(gqa_q_norm[l], GQA_Q_HEADS), jnp.tile(gqa_q_norm[l][partner], GQA_Q_HEADS)])
        gk = jnp.stack([jnp.tile(gqa_k_norm[l], GQA_KV_HEADS), jnp.tile(gqa_k_norm[l][partner], GQA_KV_HEADS)])
        qa, ka, va, qb, kb, vb, qc, kc, vc = _in_projection(
            xs, modsel, norm_attn[l][None], _extended_w_in(w_in[l]), tab, gq, gk, ones, n_lat_tiles)

        n_qt = n_lat_tiles + 1 if ctx_out else n_lat_tiles
        ctx_tile = n_lat_tiles if ctx_out else None
        oa = _neighbourhood_attention(qa, ka, va, _na_bias_table(na_rpb[l], rows), s)
        if ctx_out:
            oa_ctx = _flash(qa, ka, va, dummy_aux, n_qblk=1, n_sub=2, n_hp=NA_HEADS // 2,
                            qt_off=n_lat_tiles, n_qt=1, n_lat=s, ctx_tile=n_lat_tiles,
                            mode="plain", use_exp2=False)
            oa = jnp.concatenate([oa, oa_ctx], axis=1)
        pad = lambda v: jnp.pad(v, (0, LANES - v.shape[0]))
        aux = jnp.stack([pad(diff_lambda_q1[l]), pad(diff_lambda_k1[l]), pad(diff_lambda_q2[l]),
                         pad(diff_lambda_k2[l]), jnp.tile(diff_subln[l], 2),
                         jnp.zeros((LANES,), F32), jnp.zeros((LANES,), F32), jnp.zeros((LANES,), F32)])
        ob = _flash(qb, kb, vb, aux, n_qblk=1, n_sub=4, n_hp=DIFF_HEADS // 2, qt_off=0, n_qt=n_qt,
                    n_lat=s, ctx_tile=ctx_tile, mode="diff", use_exp2=True, lam_init=lam_init)
        oc = _flash(qc, kc, vc, dummy_aux, n_qblk=3, n_sub=2, n_hp=1, qt_off=0, n_qt=n_qt,
                    n_lat=s, ctx_tile=ctx_tile, mode="plain", use_exp2=True)

        w_o = w_out[l]
        wr = jnp.zeros((d, LANES), F32)
        wr = wr.at[:, :N_GROUPS].set(router_group_w[l]).at[:, N_GROUPS:N_GROUPS + N_EXPERTS].set(router_expert_w[l])
        wrh, wrl = _split_bf16(wr)
        br = jnp.zeros((1, LANES), F32)
        br = br.at[0, :N_GROUPS].set(router_group_b[l]).at[0, N_GROUPS:N_GROUPS + N_EXPERTS].set(router_expert_b[l])
        x1, tok, route, cnt = _out_projection(
            xs, oa, ob, oc, w_o[:W_A].astype(BF16), w_o[W_A:W_A + W_B].astype(BF16), w_o[oc_rows].astype(BF16),
            modsel, norm_ffn[l][None], wrh, wrl, br, n_qt, n_lat_tiles)
        xs = _routed_moe(tok, route, cnt, x1, w_gate[l].astype(BF16), w_up[l].astype(BF16),
                         w_down[l].astype(BF16), modsel, final_norm[None], n_lat_tiles, final=not ctx_out)
    return xs
```

```python
import functools
import math

import numpy as np
import jax
import jax.numpy as jnp
from jax import lax
from jax.experimental import pallas as pl
from jax.experimental.pallas import tpu as pltpu
from jax.experimental.pallas import tpu_sc as plsc

F32 = jnp.float32
BF16 = jnp.bfloat16

D_MODEL = 1024
DEPTH = 2
GRID_W = 64
CTX_LEN = 256
HEAD_DIM = 64
NA_HEADS = 6
NA_WIN_H = 8
NA_WIN_W = 16
DIFF_HEADS = 4
DIFF_QK_DIM = 32
GQA_Q_HEADS = 6
GQA_KV_HEADS = 2
N_GROUPS = 4
EXPERTS_PER_GROUP = 4
N_EXPERTS = 16
EXPERT_HIDDEN = 512
ROPE_THETA = 10000.0
EPS = 1e-6
W_A = NA_HEADS * HEAD_DIM
W_B = DIFF_HEADS * 2 * DIFF_QK_DIM
W_C = GQA_Q_HEADS * HEAD_DIM
W_KC = GQA_KV_HEADS * HEAD_DIM
IN_WIDTH = 3 * W_A + 3 * W_B + W_C + 2 * W_KC
ROT_WIDTH = 2 * W_B + W_C + W_KC
EXT_WIDTH = IN_WIDTH + ROT_WIDTH

LANES = 128
TILE = CTX_LEN
NA_QROWS = 8
NA_KROWS = 16
NEG = -1e30
LOG2E = 1.4426950408889634
VMEM_LIMIT = 56 * 1024 * 1024
FLASH_TK = 512
PAIRS_PER_STEP = 2
MOE_TILE = 512
SC_ROWS = 16
SC_BUFS = 4
SC_CORES = 2
SC_SUBCORES = 16

GQA_Q_ORDER = (0, 3, 1, 4, 2, 5)


def _cparams(n_axes):
    return pltpu.CompilerParams(dimension_semantics=("arbitrary",) * n_axes,
                                vmem_limit_bytes=VMEM_LIMIT)


def _split_bf16(a):
    hi = a.astype(BF16)
    lo = (a - hi.astype(F32)).astype(BF16)
    return hi, lo


def _dot(a, b):
    return jnp.dot(a, b, preferred_element_type=F32)


def _dot_nt(a, b):
    return lax.dot_general(a, b, (((1,), (1,)), ((), ())), preferred_element_type=F32)


def _mod_kernel(c_ref, w_ref, b_ref, o_ref):
    c = c_ref[...]
    a = c * jax.nn.sigmoid(c)
    a_hi, a_lo = _split_bf16(a)
    w_hi, w_lo = _split_bf16(w_ref[...])
    o_ref[...] = _dot(a_hi, w_hi) + _dot(a_lo, w_hi) + _dot(a_hi, w_lo) + b_ref[...]


def _modulation(c_rows, w_mod, b_mod):
    depth, d, n = w_mod.shape
    bn = 1536
    return pl.pallas_call(
        _mod_kernel,
        out_shape=jax.ShapeDtypeStruct((depth, 8, n), F32),
        grid=(depth, n // bn),
        in_specs=[pl.BlockSpec((8, d), lambda l, j: (0, 0)),
                  pl.BlockSpec((None, d, bn), lambda l, j: (l, 0, j)),
                  pl.BlockSpec((None, 1, bn), lambda l, j: (l, 0, j))],
        out_specs=pl.BlockSpec((None, 8, bn), lambda l, j: (l, 0, j)),
        compiler_params=_cparams(2),
        name="adaln_mod",
    )(c_rows, w_mod, b_mod.reshape(depth, 1, n))


def _head_mean_sq(t, ones):
    hi, lo = _split_bf16(t * t)
    return (_dot(hi, ones) + _dot(lo, ones)) * (1.0 / HEAD_DIM)


def _inproj_kernel(x_ref, mod_ref, gain_ref, w_ref, tab_ref, gq_ref, gk_ref, ones_ref,
                   qa_ref, ka_ref, va_ref, qb_ref, kb_ref, vb_ref, qc_ref, kc_ref, vc_ref):
    x = x_ref[...]
    mod = mod_ref[...]
    ms = jnp.mean(x * x, axis=-1, keepdims=True)
    h = (x * lax.rsqrt(ms + EPS)) * gain_ref[...]
    h = h * (1.0 + mod[1:2]) + mod[0:1]
    hb = h.astype(BF16)

    def proj(a, b):
        return _dot(hb, w_ref[:, a:b])

    pa = proj(0, 3 * W_A)
    qa_ref[...] = (pa[:, :W_A] * (HEAD_DIM ** -0.5)).astype(BF16)
    ka_ref[...] = pa[:, W_A:2 * W_A].astype(BF16)
    va_ref[...] = pa[:, 2 * W_A:].astype(BF16)

    tab = tab_ref[...]
    cos_b = jnp.concatenate([tab[:, 0:LANES]] * 2, axis=1)
    sin_b = jnp.concatenate([tab[:, LANES:2 * LANES]] * 2, axis=1)
    cos_c1 = tab[:, 2 * LANES:3 * LANES]
    sin_c1 = tab[:, 3 * LANES:4 * LANES]
    cos_c = jnp.concatenate([cos_c1] * 3, axis=1)
    sin_c = jnp.concatenate([sin_c1] * 3, axis=1)

    o_b = 3 * W_A
    pb = proj(o_b, o_b + 3 * W_B)
    pbr = proj(IN_WIDTH, IN_WIDTH + 2 * W_B)
    qb = pb[:, :W_B] * cos_b + pbr[:, :W_B] * sin_b
    qb_ref[...] = (qb * (DIFF_QK_DIM ** -0.5 * LOG2E)).astype(BF16)
    kb_ref[...] = (pb[:, W_B:2 * W_B] * cos_b + pbr[:, W_B:] * sin_b).astype(BF16)
    vb_ref[...] = pb[:, 2 * W_B:].astype(BF16)

    o_c = o_b + 3 * W_B
    pc = proj(o_c, IN_WIDTH)
    pcr = proj(IN_WIDTH + 2 * W_B, EXT_WIDTH)
    ones = ones_ref[...]
    qc = pc[:, :W_C]
    kc = pc[:, W_C:W_C + W_KC]
    nq = lax.rsqrt(_head_mean_sq(qc, ones) + EPS)
    nk = lax.rsqrt(_head_mean_sq(kc, ones[:W_KC, :W_KC]) + EPS)
    gq = gq_ref[...]
    gk = gk_ref[...]
    q = nq * (qc * gq[0:1] * cos_c + pcr[:, :W_C] * gq[1:2] * sin_c)
    qc_ref[...] = (q * (HEAD_DIM ** -0.5 * LOG2E)).astype(BF16)
    k = nk * (kc * gk[0:1] * cos_c1 + pcr[:, W_C:] * gk[1:2] * sin_c1)
    kc_ref[...] = k.astype(BF16)
    vc_ref[...] = pc[:, W_C + W_KC:].astype(BF16)


def _in_projection(xs, modsel, gain, w_ext, tab, gq, gk, ones, n_lat_tiles):
    b, t_all, d = xs.shape
    n_tiles = t_all // TILE
    widths = (W_A, W_A, W_A, W_B, W_B, W_B, W_C, W_KC, W_KC)
    tok = lambda bi, ti: (bi, ti, 0)
    const2 = lambda bi, ti: (0, 0)
    return pl.pallas_call(
        _inproj_kernel,
        out_shape=[jax.ShapeDtypeStruct((b, t_all, w), BF16) for w in widths],
        grid=(b, n_tiles),
        in_specs=[pl.BlockSpec((None, TILE, d), tok),
                  pl.BlockSpec((None, 6, d), lambda bi, ti: (2 * bi + (ti >= n_lat_tiles).astype(jnp.int32), 0, 0)),
                  pl.BlockSpec((1, d), const2),
                  pl.BlockSpec((d, EXT_WIDTH), const2),
                  pl.BlockSpec((TILE, 4 * LANES), lambda bi, ti: (ti, 0)),
                  pl.BlockSpec((2, W_C), const2),
                  pl.BlockSpec((2, W_KC), const2),
                  pl.BlockSpec((W_C, W_C), const2)],
        out_specs=[pl.BlockSpec((None, TILE, w), tok) for w in widths],
        compiler_params=_cparams(2),
        name="in_projection",
    )(xs, modsel, gain, w_ext, tab, gq, gk, ones)


def _flash_kernel(q_ref, k_ref, v_ref, aux_ref, o_ref, va_ref, vb_ref, qs_ref, acc_ref, m_ref,
                  s0_ref, s1_ref, mb0_ref, mb1_ref, *,
                  n_qblk, n_sub, tk, n_lat_blocks, ctx_tile, qt_off, mode, use_exp2, lam_init):
    exp_fn = jnp.exp2 if use_exp2 else jnp.exp
    qt = pl.program_id(2) + qt_off
    sub_w = LANES // n_sub
    half = LANES // 2
    lane = lax.broadcasted_iota(jnp.int32, (1, LANES), 1)
    lower = lane < half
    n_pieces = n_qblk * n_sub
    ma = (n_pieces // 2) * TILE
    m_rows = n_pieces * TILE
    ctx_start = n_lat_blocks * tk

    @pl.when(pl.program_id(2) == 0)
    def _():
        v = v_ref[...].astype(F32)
        va_ref[...] = jnp.where(lower, v, 1.0).astype(BF16)
        vb_ref[...] = jnp.where(lower, 1.0, v).astype(BF16)

    ia, ib = 0, n_pieces // 2
    for blk in range(n_qblk):
        qf = q_ref[:, blk * LANES:(blk + 1) * LANES].astype(F32)
        for sub in range(n_sub):
            msk = (lane >= sub * sub_w) & (lane < (sub + 1) * sub_w)
            piece = jnp.where(msk, qf, 0.0).astype(BF16)
            if sub * sub_w < half:
                qs_ref[ia * TILE:(ia + 1) * TILE, :] = piece
                ia += 1
            else:
                qs_ref[ib * TILE:(ib + 1) * TILE, :] = piece
                ib += 1

    s_bufs = (s0_ref, s1_ref)
    mb_bufs = (mb0_ref, mb1_ref)

    def scores(start, size, slot):
        s = _dot_nt(qs_ref[...], k_ref[pl.ds(start, size), :])
        s_bufs[slot][:, :size] = s
        mb = jnp.max(s, axis=-1, keepdims=True)
        mb_bufs[slot][...] = jnp.broadcast_to(mb, (m_rows, LANES))

    def accumulate(start, size, slot, first):
        mb = mb_bufs[slot][...]
        if first:
            m_new = mb
        else:
            m_old = m_ref[...]
            m_new = jnp.maximum(m_old, mb)
        s_ref = s_bufs[slot]
        cols = [s_ref[:, c * LANES:(c + 1) * LANES] - m_new for c in range(size // LANES)]
        if use_exp2:
            p = jnp.concatenate([jnp.exp2(d.astype(BF16)) for d in cols], axis=1)
        else:
            p = jnp.concatenate([jnp.exp(d).astype(BF16) for d in cols], axis=1)
        pva = _dot(p[:ma], va_ref[pl.ds(start, size), :])
        pvb = _dot(p[ma:], vb_ref[pl.ds(start, size), :])
        if first:
            acc_ref[:ma, :] = pva
            acc_ref[ma:, :] = pvb
        else:
            alpha = exp_fn(m_old - m_new)
            acc_ref[:ma, :] = alpha[:ma] * acc_ref[:ma, :] + pva
            acc_ref[ma:, :] = alpha[ma:] * acc_ref[ma:, :] + pvb
        m_ref[...] = m_new

    def lat(j):
        return pl.multiple_of(j * tk, tk)

    def latent_queries():
        scores(ctx_start, CTX_LEN, 0)
        scores(lat(0), tk, 1)
        accumulate(ctx_start, CTX_LEN, 0, True)

        def pair(i):
            scores(lat(2 * i + 1), tk, 0)
            accumulate(lat(2 * i), tk, 1, False)
            scores(lat(2 * i + 2), tk, 1)
            accumulate(lat(2 * i + 1), tk, 0, False)

        def body(i, carry):
            for u in range(PAIRS_PER_STEP):
                pair(i * PAIRS_PER_STEP + u)
            return carry

        n_pairs = (n_lat_blocks - 2) // 2
        n_steps = n_pairs // PAIRS_PER_STEP
        lax.fori_loop(0, n_steps, body, 0)
        for i in range(n_steps * PAIRS_PER_STEP, n_pairs):
            pair(i)
        scores(lat(n_lat_blocks - 1), tk, 0)
        accumulate(lat(n_lat_blocks - 2), tk, 1, False)
        accumulate(lat(n_lat_blocks - 1), tk, 0, False)

    def context_queries():
        scores(ctx_start, CTX_LEN, 0)
        accumulate(ctx_start, CTX_LEN, 0, True)

    if ctx_tile is None:
        latent_queries()
    else:
        pl.when(qt != ctx_tile)(latent_queries)
        pl.when(qt == ctx_tile)(context_queries)

    acc = acc_ref[...]
    r = acc / pltpu.roll(acc, half, 1)
    ra, rb = r[:ma], r[ma:]
    if mode == "plain":
        for i in range(n_pieces // 2):
            o = jnp.where(lower, ra[i * TILE:(i + 1) * TILE], rb[i * TILE:(i + 1) * TILE])
            o_ref[:, i * LANES:(i + 1) * LANES] = o.astype(BF16)
    else:
        aux = aux_ref[...]
        l1 = jnp.sum(aux[0:1] * aux[1:2], axis=-1, keepdims=True)
        l2 = jnp.sum(aux[2:3] * aux[3:4], axis=-1, keepdims=True)
        lam = jnp.exp(l1) - jnp.exp(l2) + lam_init
        oa = ra[:TILE] - lam * ra[TILE:]
        ob = rb[:TILE] - lam * rb[TILE:]
        o = jnp.where(lower, oa, ob)
        sq = o * o
        ss_a = jnp.sum(jnp.where(lower, sq, 0.0), axis=-1, keepdims=True)
        ss_b = jnp.sum(jnp.where(lower, 0.0, sq), axis=-1, keepdims=True)
        ms = jnp.where(lower, ss_a, ss_b) * (1.0 / HEAD_DIM)
        o = (o * lax.rsqrt(ms + EPS)) * aux[4:5]
        o_ref[...] = (o * (1.0 - lam_init)).astype(BF16)


def _flash(q, k, v, aux, *, n_qblk, n_sub, n_hp, qt_off, n_qt, n_lat, ctx_tile, mode, use_exp2,
           lam_init=0.0):
    b, t_all, _ = q.shape
    qw = n_qblk * LANES
    tk = FLASH_TK if n_lat % (2 * FLASH_TK) == 0 else 512
    assert n_lat % (2 * tk) == 0 and tk >= CTX_LEN
    m_rows = n_qblk * n_sub * TILE
    kern = functools.partial(_flash_kernel, n_qblk=n_qblk, n_sub=n_sub, tk=tk,
                             n_lat_blocks=n_lat // tk, ctx_tile=ctx_tile, qt_off=qt_off,
                             mode=mode, use_exp2=use_exp2, lam_init=lam_init)
    return pl.pallas_call(
        kern,
        out_shape=jax.ShapeDtypeStruct((b, n_qt * TILE, n_hp * qw), BF16),
        grid=(b, n_hp, n_qt),
        in_specs=[pl.BlockSpec((None, TILE, qw), lambda bi, hp, qt: (bi, qt + qt_off, hp)),
                  pl.BlockSpec((None, t_all, LANES), lambda bi, hp, qt: (bi, 0, hp)),
                  pl.BlockSpec((None, t_all, LANES), lambda bi, hp, qt: (bi, 0, hp)),
                  pl.BlockSpec((8, LANES), lambda bi, hp, qt: (0, 0))],
        out_specs=pl.BlockSpec((None, TILE, qw), lambda bi, hp, qt: (bi, qt, hp)),
        scratch_shapes=[pltpu.VMEM((t_all, LANES), BF16),
                        pltpu.VMEM((t_all, LANES), BF16),
                        pltpu.VMEM((m_rows, LANES), BF16),
                        pltpu.VMEM((m_rows, LANES), F32),
                        pltpu.VMEM((m_rows, LANES), F32),
                        pltpu.VMEM((m_rows, tk), F32),
                        pltpu.VMEM((m_rows, tk), F32),
                        pltpu.VMEM((m_rows, LANES), F32),
                        pltpu.VMEM((m_rows, LANES), F32)],
        compiler_params=_cparams(3),
        name="flash_" + mode,
    )(q, k, v, aux)


def _na_kernel(q_ref, k0, k1, k2, k3, v0, v1, v2, v3, kc_ref, vc_ref, bias_ref, o_ref):
    lane = lax.broadcasted_iota(jnp.int32, (1, LANES), 1)
    lower = lane < LANES // 2
    qf = q_ref[...].astype(F32)
    kw = jnp.concatenate([k0[...], k1[...], k2[...], k3[...]], axis=0)
    vw = jnp.concatenate([v0[...], v1[...], v2[...], v3[...]], axis=0)
    kc = kc_ref[...]
    vc = vc_ref[...]
    outs = []
    for hh in range(2):
        msk = lower if hh == 0 else jnp.logical_not(lower)
        qh = jnp.where(msk, qf, 0.0).astype(BF16)
        s_w = _dot_nt(qh, kw) + bias_ref[hh]
        s_c = _dot_nt(qh, kc)
        m = jnp.maximum(jnp.max(s_w, axis=-1, keepdims=True), jnp.max(s_c, axis=-1, keepdims=True))
        p_w = jnp.exp(s_w - m)
        p_c = jnp.exp(s_c - m)
        l = jnp.sum(p_w, axis=-1, keepdims=True) + jnp.sum(p_c, axis=-1, keepdims=True)
        o = _dot(p_w.astype(BF16), vw) + _dot(p_c.astype(BF16), vc)
        outs.append(o / l)
    o_ref[...] = jnp.where(lower, outs[0], outs[1]).astype(BF16)


def _neighbourhood_attention(qa, ka, va, bias, n_lat):
    b = qa.shape[0]
    q_tok = NA_QROWS * GRID_W
    v_tok = q_tok // 2
    n_rb = n_lat // q_tok
    n_view = n_lat // v_tok
    ctx_blk = n_lat // v_tok

    def view(j):
        return lambda rb, hp, bi: (bi, jnp.clip(2 * rb - 1 + j, 0, n_view - 1), hp)

    kv_specs = [pl.BlockSpec((None, v_tok, LANES), view(j)) for j in range(4)]
    ctx_spec = pl.BlockSpec((None, CTX_LEN, LANES), lambda rb, hp, bi: (bi, ctx_blk, hp))

    def bias_map(rb, hp, bi):
        pat = jnp.where(rb == 0, 0, jnp.where(rb == n_rb - 1, 2, 1))
        return (pat, hp, 0, 0)

    return pl.pallas_call(
        _na_kernel,
        out_shape=jax.ShapeDtypeStruct((b, n_lat, W_A), BF16),
        grid=(n_rb, NA_HEADS // 2, b),
        in_specs=[pl.BlockSpec((None, q_tok, LANES), lambda rb, hp, bi: (bi, rb, hp))]
                 + kv_specs + kv_specs + [ctx_spec, ctx_spec,
                 pl.BlockSpec((None, 2, q_tok, NA_KROWS * GRID_W), bias_map)],
        out_specs=pl.BlockSpec((None, q_tok, LANES), lambda rb, hp, bi: (bi, rb, hp)),
        compiler_params=_cparams(3),
        name="neighbourhood_attention",
    )(qa, ka, ka, ka, ka, va, va, va, va, ka, va, bias)


def _na_bias_table(rpb, rows):
    cols = np.arange(GRID_W)
    c0 = np.clip(cols - NA_WIN_W // 2, 0, GRID_W - NA_WIN_W)
    cc = cols[None, :]
    col_ok = (cc >= c0[:, None]) & (cc < c0[:, None] + NA_WIN_W)
    dc = np.clip(cc - cols[:, None] + (NA_WIN_W - 1), 0, 2 * NA_WIN_W - 2)
    e = jnp.where(col_ok[None, None], rpb.astype(F32)[:, :, dc], NEG)
    e = jnp.concatenate([e, jnp.full_like(e[:, :1], NEG)], axis=1)
    a = np.arange(NA_QROWS)[:, None]
    i = np.arange(NA_KROWS)[None, :]
    pats = []
    for r_base in (0, NA_QROWS, rows - NA_QROWS):
        r = r_base + a
        key_row = r_base - NA_WIN_H // 2 + i
        r0 = np.clip(r - NA_WIN_H // 2, 0, rows - NA_WIN_H)
        ok = (key_row >= r0) & (key_row < r0 + NA_WIN_H) & (key_row >= 0) & (key_row < rows)
        dr = np.where(ok, key_row - r + (NA_WIN_H - 1), 2 * NA_WIN_H - 1)
        pats.append(dr)
    dr_all = np.stack(pats)
    t = e[:, dr_all]
    t = t.transpose(1, 0, 2, 4, 3, 5)
    return t.reshape(3, NA_HEADS, NA_QROWS * GRID_W, NA_KROWS * GRID_W)


def _outproj_kernel(x_ref, oa_ref, ob_ref, oc_ref, wa_ref, wb_ref, wc_ref, mod_ref, gain_ref,
                    wrh_ref, wrl_ref, br_ref, tri_ref, x1_ref, tok_ref, route_ref, cnt_ref, run_ref,
                    *, region):
    mod = mod_ref[...]
    y = _dot(oa_ref[...], wa_ref[...]) + _dot(ob_ref[...], wb_ref[...]) + _dot(oc_ref[...], wc_ref[...])
    x1 = x_ref[...] + mod[2:3] * y
    x1_ref[...] = x1
    ms = jnp.mean(x1 * x1, axis=-1, keepdims=True)
    t = (x1 * lax.rsqrt(ms + EPS)) * gain_ref[...]
    t = t * (1.0 + mod[4:5]) + mod[3:4]
    tok_ref[...] = t

    t_hi, t_lo = _split_bf16(t)
    wrh = wrh_ref[...]
    logits = _dot(t_hi, wrh) + _dot(t_lo, wrh) + _dot(t_hi, wrl_ref[...]) + br_ref[...]

    lane = lax.broadcasted_iota(jnp.int32, logits.shape, 1)
    lane_f = lane.astype(F32)
    is_g = lane < N_GROUPS
    gl = jnp.where(is_g, logits, NEG)
    gmax = jnp.max(gl, axis=-1, keepdims=True)
    g_sel = jnp.min(jnp.where(gl == gmax, lane_f, 1e9), axis=-1, keepdims=True)
    p_grp = 1.0 / jnp.sum(jnp.where(is_g, jnp.exp(gl - gmax), 0.0), axis=-1, keepdims=True)
    grp_of_lane = lax.shift_right_arithmetic(lane - N_GROUPS, 2).astype(F32)
    in_grp = (lane >= N_GROUPS) & (lane < N_GROUPS + N_EXPERTS) & (grp_of_lane == g_sel)
    el = jnp.where(in_grp, logits, NEG)
    v1 = jnp.max(el, axis=-1, keepdims=True)
    i1 = jnp.min(jnp.where(el == v1, lane_f, 1e9), axis=-1, keepdims=True)
    el2 = jnp.where(lane_f == i1, NEG, el)
    v2 = jnp.max(el2, axis=-1, keepdims=True)
    i2 = jnp.min(jnp.where(el2 == v2, lane_f, 1e9), axis=-1, keepdims=True)
    e2 = jnp.exp(v2 - v1)
    den = 1.0 + e2
    w1 = p_grp / den
    w2 = p_grp * e2 / den

    @pl.when((pl.program_id(0) == 0) & (pl.program_id(1) == 0))
    def _():
        run_ref[...] = jnp.zeros(run_ref.shape, F32)

    ind = jnp.where(lane_f == i1, 1.0, 0.0) + jnp.where(lane_f == i2, 1.0, 0.0)
    rank = _dot(tri_ref[...], ind.astype(BF16)) + run_ref[0:1, :]

    def pick(m, l):
        return jnp.sum(jnp.where(lane_f == l, m, 0.0), axis=-1, keepdims=True)

    pos1 = (i1 - N_GROUPS) * region + pick(rank, i1)
    pos2 = (i2 - N_GROUPS) * region + pick(rank, i2)
    route_ref[...] = jnp.where(lane == 0, pos1, jnp.where(lane == 1, pos2,
                               jnp.where(lane == 2, w1, jnp.where(lane == 3, w2, 0.0))))
    run = run_ref[...] + jnp.sum(ind, axis=0, keepdims=True)
    run_ref[...] = run
    cnt_ref[...] = run


def _out_projection(xs, oa, ob, oc, wa, wb, wc, modsel, gain, wrh, wrl, br, n_tiles, n_lat_tiles):
    b, _, d = xs.shape
    tok = lambda bi, ti: (bi, ti, 0)
    const2 = lambda bi, ti: (0, 0)
    rows = n_tiles * TILE
    tri = jnp.asarray(np.tril(np.ones((TILE, TILE), np.float32), -1), BF16)
    return pl.pallas_call(
        functools.partial(_outproj_kernel, region=b * rows),
        out_shape=[jax.ShapeDtypeStruct((b, rows, d), F32),
                   jax.ShapeDtypeStruct((b, rows, d), F32),
                   jax.ShapeDtypeStruct((b, rows, LANES), F32),
                   jax.ShapeDtypeStruct((8, LANES), F32)],
        grid=(b, n_tiles),
        in_specs=[pl.BlockSpec((None, TILE, d), tok),
                  pl.BlockSpec((None, TILE, W_A), tok),
                  pl.BlockSpec((None, TILE, W_B), tok),
                  pl.BlockSpec((None, TILE, W_C), tok),
                  pl.BlockSpec((W_A, d), const2),
                  pl.BlockSpec((W_B, d), const2),
                  pl.BlockSpec((W_C, d), const2),
                  pl.BlockSpec((None, 6, d), lambda bi, ti: (2 * bi + (ti >= n_lat_tiles).astype(jnp.int32), 0, 0)),
                  pl.BlockSpec((1, d), const2),
                  pl.BlockSpec((d, LANES), const2),
                  pl.BlockSpec((d, LANES), const2),
                  pl.BlockSpec((1, LANES), const2),
                  pl.BlockSpec((TILE, TILE), const2)],
        out_specs=[pl.BlockSpec((None, TILE, d), tok),
                   pl.BlockSpec((None, TILE, d), tok),
                   pl.BlockSpec((None, TILE, LANES), tok),
                   pl.BlockSpec((8, LANES), const2)],
        scratch_shapes=[pltpu.VMEM((8, LANES), F32)],
        compiler_params=_cparams(2),
        name="out_projection",
    )(xs, oa, ob, oc, wa, wb, wc, modsel, gain, wrh, wrl, br, tri)


def _sc_mesh():
    return plsc.VectorSubcoreMesh(core_axis_name="core", subcore_axis_name="subcore")


def _sc_worker_base(per_worker):
    wid = lax.axis_index("subcore") * SC_CORES + lax.axis_index("core")
    return wid * per_worker


def _sc_scratch(d, dtype):
    return ([pltpu.VMEM((SC_ROWS,), jnp.int32)] * SC_BUFS + [pltpu.VMEM((SC_ROWS, d), dtype)] * SC_BUFS
            + [pltpu.SemaphoreType.DMA] * (2 * SC_BUFS))


def _sc_split(scratch):
    return (scratch[:SC_BUFS], scratch[SC_BUFS:2 * SC_BUFS], scratch[2 * SC_BUFS:3 * SC_BUFS],
            scratch[3 * SC_BUFS:])


def _sc_scatter_rows(x, idx, n_out):
    n, d = x.shape
    per_worker = 2 * n // (SC_CORES * SC_SUBCORES)
    assert per_worker % (SC_ROWS * SC_BUFS) == 0 and n % SC_ROWS == 0

    @functools.partial(pl.kernel, out_type=jax.ShapeDtypeStruct((n_out, d), x.dtype),
                       mesh=_sc_mesh(), scratch_types=_sc_scratch(d, x.dtype))
    def scatter(x_hbm, i_hbm, o_hbm, *scratch):
        idx_v, rows_v, sem_in, sem_out = _sc_split(scratch)
        base = _sc_worker_base(per_worker)

        @pl.loop(0, per_worker // SC_ROWS, step=SC_BUFS)
        def _(c):
            reads = []
            for u in range(SC_BUFS):
                a = pl.multiple_of(base + (c + u) * SC_ROWS, SC_ROWS)
                t = pl.multiple_of(lax.rem(a, n), SC_ROWS)
                pltpu.sync_copy(i_hbm.at[pl.ds(a, SC_ROWS)], idx_v[u])
                reads.append(pltpu.async_copy(x_hbm.at[pl.ds(t, SC_ROWS)], rows_v[u], sem_in[u]))
            writes = []
            for u in range(SC_BUFS):
                reads[u].wait()
                writes.append(pltpu.async_copy(rows_v[u], o_hbm.at[idx_v[u]], sem_out[u]))
            for w in writes:
                w.wait()

    return scatter(x, idx)


def _sc_gather_rows(src, idx):
    m = idx.shape[0]
    d = src.shape[1]
    per_worker = m // (SC_CORES * SC_SUBCORES)
    assert per_worker % (SC_ROWS * SC_BUFS) == 0

    @functools.partial(pl.kernel, out_type=jax.ShapeDtypeStruct((m, d), src.dtype),
                       mesh=_sc_mesh(), scratch_types=_sc_scratch(d, src.dtype))
    def gather(s_hbm, i_hbm, o_hbm, *scratch):
        idx_v, rows_v, sem_in, sem_out = _sc_split(scratch)
        base = _sc_worker_base(per_worker)

        @pl.loop(0, per_worker // SC_ROWS, step=SC_BUFS)
        def _(c):
            offs, reads = [], []
            for u in range(SC_BUFS):
                a = pl.multiple_of(base + (c + u) * SC_ROWS, SC_ROWS)
                offs.append(a)
                pltpu.sync_copy(i_hbm.at[pl.ds(a, SC_ROWS)], idx_v[u])
                reads.append(pltpu.async_copy(s_hbm.at[idx_v[u]], rows_v[u], sem_in[u]))
            writes = []
            for u in range(SC_BUFS):
                reads[u].wait()
                writes.append(pltpu.async_copy(rows_v[u], o_hbm.at[pl.ds(offs[u], SC_ROWS)], sem_out[u]))
            for w in writes:
                w.wait()

    return gather(src, idx)


def _expert_ffn_kernel(blk_ref, exp_ref, x_ref, wg_ref, wu_ref, wd_ref, y_ref, wgb_ref, wub_ref, wdb_ref):
    j = pl.program_id(0)

    @pl.when((j == 0) | (exp_ref[j] != exp_ref[jnp.maximum(j - 1, 0)]))
    def _():
        wgb_ref[...] = wg_ref[...].astype(BF16)
        wub_ref[...] = wu_ref[...].astype(BF16)
        wdb_ref[...] = wd_ref[...].astype(BF16)

    x = x_ref[...].astype(BF16)
    hid = jax.nn.silu(_dot(x, wgb_ref[...])) * _dot(x, wub_ref[...])
    y_ref[...] = _dot(hid.astype(BF16), wdb_ref[...])


def _expert_ffn(xs, blk, exp, wg, wu, wd, layer):
    rows, d = xs.shape
    w_map = lambda j, blk, exp: (layer, exp[j], 0, 0)
    return pl.pallas_call(
        _expert_ffn_kernel,
        out_shape=jax.ShapeDtypeStruct((rows, d), F32),
        grid_spec=pltpu.PrefetchScalarGridSpec(
            num_scalar_prefetch=2,
            grid=(blk.shape[0],),
            in_specs=[pl.BlockSpec((MOE_TILE, d), lambda j, blk, exp: (blk[j], 0)),
                      pl.BlockSpec((None, None, d, EXPERT_HIDDEN), w_map),
                      pl.BlockSpec((None, None, d, EXPERT_HIDDEN), w_map),
                      pl.BlockSpec((None, None, EXPERT_HIDDEN, d), w_map)],
            out_specs=pl.BlockSpec((MOE_TILE, d), lambda j, blk, exp: (blk[j], 0)),
            scratch_shapes=[pltpu.VMEM((d, EXPERT_HIDDEN), BF16),
                            pltpu.VMEM((d, EXPERT_HIDDEN), BF16),
                            pltpu.VMEM((EXPERT_HIDDEN, d), BF16)]),
        compiler_params=_cparams(1),
        name="expert_ffn",
    )(blk, exp, xs, wg, wu, wd)


def _combine_kernel(x1_ref, y1_ref, y2_ref, route_ref, mod_ref, fgain_ref, o_ref, *, final):
    route = route_ref[...]
    y = route[:, 2:3] * y1_ref[...] + route[:, 3:4] * y2_ref[...]
    x2 = x1_ref[...] + mod_ref[5:6, :] * y
    if final:
        ms = jnp.mean(x2 * x2, axis=-1, keepdims=True)
        x2 = (x2 * lax.rsqrt(ms + EPS)) * fgain_ref[...]
    o_ref[...] = x2


def _combine(x1, ys, route, modsel, fgain, n_lat_tiles, final):
    b, rows, d = x1.shape
    n_t = rows // TILE
    tok = lambda bi, ti: (bi, ti, 0)
    return pl.pallas_call(
        functools.partial(_combine_kernel, final=final),
        out_shape=jax.ShapeDtypeStruct((b, rows, d), F32),
        grid=(b, n_t),
        in_specs=[pl.BlockSpec((None, TILE, d), tok),
                  pl.BlockSpec((TILE, d), lambda bi, ti: (bi * n_t + ti, 0)),
                  pl.BlockSpec((TILE, d), lambda bi, ti: (b * n_t + bi * n_t + ti, 0)),
                  pl.BlockSpec((None, TILE, LANES), tok),
                  pl.BlockSpec((None, 6, d), lambda bi, ti: (2 * bi + (ti >= n_lat_tiles).astype(jnp.int32), 0, 0)),
                  pl.BlockSpec((1, d), lambda bi, ti: (0, 0))],
        out_specs=pl.BlockSpec((None, TILE, d), tok),
        compiler_params=_cparams(2),
        name="moe_combine",
    )(x1, ys, ys, route, modsel, fgain)


def _routed_moe(tok, route, cnt, x1, wg, wu, wd, layer, modsel, fgain, n_lat_tiles, final):
    b, rows, d = x1.shape
    n = b * rows
    flat = route.reshape(n, LANES)
    idx = jnp.concatenate([flat[:, 0], flat[:, 1]]).astype(jnp.int32)
    xs = _sc_scatter_rows(tok.reshape(n, d), idx, N_EXPERTS * n)

    counts = cnt[0, N_GROUPS:N_GROUPS + N_EXPERTS].astype(jnp.int32)
    tiles = (counts + MOE_TILE - 1) // MOE_TILE
    ends = jnp.cumsum(tiles)
    n_sched = 2 * n // MOE_TILE + N_EXPERTS
    j = jnp.minimum(jnp.arange(n_sched, dtype=jnp.int32), ends[-1] - 1)
    exp = jnp.sum((j[:, None] >= ends[None, :]).astype(jnp.int32), axis=1)
    blk = exp * (n // MOE_TILE) + j - (ends - tiles)[exp]

    ys = _expert_ffn(xs, blk, exp, wg, wu, wd, layer)
    yg = _sc_gather_rows(ys, idx)
    return _combine(x1, yg, route, modsel, fgain, n_lat_tiles, final)


def _rot_perm(width, head):
    i = np.arange(width)
    half = head // 2
    first = (i % head) < half
    idx = np.where(first, i + half, i - half)
    sign = np.where(first, -1.0, 1.0).astype(np.float32)
    return idx, sign


def _rope_tables(n_lat):
    t = jnp.arange(n_lat)
    row = (t // GRID_W).astype(F32)
    col = (t % GRID_W).astype(F32)

    def cs(dim):
        quarter = dim // 4
        freqs = ROPE_THETA ** (-jnp.arange(quarter, dtype=F32) / quarter)
        ang = jnp.concatenate([row[:, None] * freqs, col[:, None] * freqs], axis=-1)
        cos = jnp.tile(jnp.cos(ang), (1, 2 * LANES // dim))
        sin = jnp.tile(jnp.sin(ang), (1, 2 * LANES // dim))
        cos = jnp.concatenate([cos, jnp.ones((CTX_LEN, LANES), F32)], axis=0)
        sin = jnp.concatenate([sin, jnp.zeros((CTX_LEN, LANES), F32)], axis=0)
        return cos, sin

    cos_b, sin_b = cs(DIFF_QK_DIM)
    cos_c, sin_c = cs(HEAD_DIM)
    return jnp.concatenate([cos_b, sin_b, cos_c, sin_c], axis=1)


def _extended_w_in(w_in):
    o_b = 3 * W_A
    o_c = o_b + 3 * W_B
    qc_cols = o_c + np.concatenate([h * HEAD_DIM + np.arange(HEAD_DIM) for h in GQA_Q_ORDER])
    cols = np.concatenate([np.arange(o_c), qc_cols, np.arange(o_c + W_C, IN_WIDTH)])
    w = w_in[:, cols]
    idx_b, sign_b = _rot_perm(W_B, DIFF_QK_DIM)
    idx_c, sign_c = _rot_perm(W_C, HEAD_DIM)
    idx_k, sign_k = _rot_perm(W_KC, HEAD_DIM)
    rot = jnp.concatenate([
        w[:, o_b + idx_b] * sign_b, w[:, o_b + W_B + idx_b] * sign_b,
        w[:, o_c + idx_c] * sign_c, w[:, o_c + W_C + idx_k] * sign_k], axis=1)
    return jnp.concatenate([w, rot], axis=1).astype(BF16)


def kernel(x, c, ctx, c_ctx, w_mod, b_mod, norm_attn, norm_ffn, w_in, w_out, na_rpb, diff_lambda_q1, diff_lambda_k1, diff_lambda_q2, diff_lambda_k2, diff_subln, gqa_q_norm, gqa_k_norm, router_group_w, router_group_b, router_expert_w, router_expert_b, w_gate, w_up, w_down, final_norm):
    b, s, d = x.shape
    assert d == D_MODEL and ctx.shape[1] == CTX_LEN and s % (NA_QROWS * GRID_W) == 0
    rows = s // GRID_W
    assert rows >= 2 * NA_QROWS
    t_all = s + CTX_LEN
    n_lat_tiles = s // TILE

    c_rows = jnp.zeros((8, d), F32).at[:b].set(c).at[b].set(c_ctx)
    mod = _modulation(c_rows, w_mod, b_mod)

    tab = _rope_tables(s)
    hidx = np.arange(HEAD_DIM)
    partner = np.where(hidx < HEAD_DIM // 2, hidx + HEAD_DIM // 2, hidx - HEAD_DIM // 2)
    blk = np.arange(W_C) // HEAD_DIM
    ones = jnp.asarray((blk[:, None] == blk[None, :]).astype(np.float32), BF16)
    oc_rows = W_A + W_B + np.concatenate([h * HEAD_DIM + np.arange(HEAD_DIM) for h in GQA_Q_ORDER])
    dummy_aux = jnp.zeros((8, LANES), F32)

    xs = jnp.concatenate([x, ctx], axis=1)
    for l in range(DEPTH):
        ctx_out = l < DEPTH - 1
        lam_init = 0.8 - 0.6 * math.exp(-0.3 * l)
        m_lat = mod[l, :b].reshape(b, 1, 6, d)
        m_ctx = jnp.broadcast_to(mod[l, b].reshape(1, 1, 6, d), (b, 1, 6, d))
        modsel = jnp.concatenate([m_lat, m_ctx], axis=1).reshape(2 * b, 6, d)

        gq = jnp.stack([jnp.tile(gqa_q_norm[l], GQA_Q_HEADS), jnp.tile(gqa_q_norm[l][partner], GQA_Q_HEADS)])
        gk = jnp.stack([jnp.tile(gqa_k_norm[l], GQA_KV_HEADS), jnp.tile(gqa_k_norm[l][partner], GQA_KV_HEADS)])
        qa, ka, va, qb, kb, vb, qc, kc, vc = _in_projection(
            xs, modsel, norm_attn[l][None], _extended_w_in(w_in[l]), tab, gq, gk, ones, n_lat_tiles)

        n_qt = n_lat_tiles + 1 if ctx_out else n_lat_tiles
        ctx_tile = n_lat_tiles if ctx_out else None
        oa = _neighbourhood_attention(qa, ka, va, _na_bias_table(na_rpb[l], rows), s)
        if ctx_out:
            oa_ctx = _flash(qa, ka, va, dummy_aux, n_qblk=1, n_sub=2, n_hp=NA_HEADS // 2,
                            qt_off=n_lat_tiles, n_qt=1, n_lat=s, ctx_tile=n_lat_tiles,
                            mode="plain", use_exp2=False)
            oa = jnp.concatenate([oa, oa_ctx], axis=1)
        pad = lambda v: jnp.pad(v, (0, LANES - v.shape[0]))
        aux = jnp.stack([pad(diff_lambda_q1[l]), pad(diff_lambda_k1[l]), pad(diff_lambda_q2[l]),
                         pad(diff_lambda_k2[l]), jnp.tile(diff_subln[l], 2),
                         jnp.zeros((LANES,), F32), jnp.zeros((LANES,), F32), jnp.zeros((LANES,), F32)])
        ob = _flash(qb, kb, vb, aux, n_qblk=1, n_sub=4, n_hp=DIFF_HEADS // 2, qt_off=0, n_qt=n_qt,
                    n_lat=s, ctx_tile=ctx_tile, mode="diff", use_exp2=True, lam_init=lam_init)
        oc = _flash(qc, kc, vc, dummy_aux, n_qblk=3, n_sub=2, n_hp=1, qt_off=0, n_qt=n_qt,
                    n_lat=s, ctx_tile=ctx_tile, mode="plain", use_exp2=True)

        w_o = w_out[l]
        wr = jnp.zeros((d, LANES), F32)
        wr = wr.at[:, :N_GROUPS].set(router_group_w[l]).at[:, N_GROUPS:N_GROUPS + N_EXPERTS].set(router_expert_w[l])
        wrh, wrl = _split_bf16(wr)
        br = jnp.zeros((1, LANES), F32)
        br = br.at[0, :N_GROUPS].set(router_group_b[l]).at[0, N_GROUPS:N_GROUPS + N_EXPERTS].set(router_expert_b[l])
        x1, tok, route, cnt = _out_projection(
            xs, oa, ob, oc, w_o[:W_A].astype(BF16), w_o[W_A:W_A + W_B].astype(BF16), w_o[oc_rows].astype(BF16),
            modsel, norm_ffn[l][None], wrh, wrl, br, n_qt, n_lat_tiles)
        xs = _routed_moe(tok, route, cnt, x1, w_gate, w_up, w_down, l, modsel, final_norm[None],
                         n_lat_tiles, final=not ctx_out)
    return xs
```

```python
import functools
import math

import numpy as np
import jax
import jax.numpy as jnp
from jax import lax
from jax.experimental import pallas as pl
from jax.experimental.pallas import tpu as pltpu
from jax.experimental.pallas import tpu_sc as plsc

F32 = jnp.float32
BF16 = jnp.bfloat16

D_MODEL = 1024
DEPTH = 2
GRID_W = 64
CTX_LEN = 256
HEAD_DIM = 64
NA_HEADS = 6
NA_WIN_H = 8
NA_WIN_W = 16
DIFF_HEADS = 4
DIFF_QK_DIM = 32
GQA_Q_HEADS = 6
GQA_KV_HEADS = 2
N_GROUPS = 4
EXPERTS_PER_GROUP = 4
N_EXPERTS = 16
EXPERT_HIDDEN = 512
ROPE_THETA = 10000.0
EPS = 1e-6
W_A = NA_HEADS * HEAD_DIM
W_B = DIFF_HEADS * 2 * DIFF_QK_DIM
W_C = GQA_Q_HEADS * HEAD_DIM
W_KC = GQA_KV_HEADS * HEAD_DIM
IN_WIDTH = 3 * W_A + 3 * W_B + W_C + 2 * W_KC

LANES = 128
TILE = CTX_LEN
NA_QROWS = 8
NA_KROWS = 16
NEG = -1e30
LOG2E = 1.4426950408889634
VMEM_LIMIT = 56 * 1024 * 1024
FLASH_TK = 512
PAIRS_PER_STEP = 2
MOE_TILE = 512
SC_ROWS = 16
SC_BUFS = 4
SC_CORES = 2
SC_SUBCORES = 16

GQA_Q_ORDER = (0, 3, 1, 4, 2, 5)


def _cparams(n_axes):
    return pltpu.CompilerParams(dimension_semantics=("arbitrary",) * n_axes,
                                vmem_limit_bytes=VMEM_LIMIT)


def _split_bf16(a):
    hi = a.astype(BF16)
    lo = (a - hi.astype(F32)).astype(BF16)
    return hi, lo


def _dot(a, b):
    return jnp.dot(a, b, preferred_element_type=F32)


def _dot_nt(a, b):
    return lax.dot_general(a, b, (((1,), (1,)), ((), ())), preferred_element_type=F32)


def _mod_kernel(c_ref, w_ref, b_ref, o_ref):
    c = c_ref[...]
    a = c * jax.nn.sigmoid(c)
    a_hi, a_lo = _split_bf16(a)
    w_hi, w_lo = _split_bf16(w_ref[...])
    o_ref[...] = _dot(a_hi, w_hi) + _dot(a_lo, w_hi) + _dot(a_hi, w_lo) + b_ref[...]


def _modulation(c_rows, w_mod, b_mod):
    depth, d, n = w_mod.shape
    bn = 1536
    return pl.pallas_call(
        _mod_kernel,
        out_shape=jax.ShapeDtypeStruct((depth, 8, n), F32),
        grid=(depth, n // bn),
        in_specs=[pl.BlockSpec((8, d), lambda l, j: (0, 0)),
                  pl.BlockSpec((None, d, bn), lambda l, j: (l, 0, j)),
                  pl.BlockSpec((None, 1, bn), lambda l, j: (l, 0, j))],
        out_specs=pl.BlockSpec((None, 8, bn), lambda l, j: (l, 0, j)),
        compiler_params=_cparams(2),
        name="adaln_mod",
    )(c_rows, w_mod, b_mod.reshape(depth, 1, n))


def _head_mean_sq(t, ones):
    hi, lo = _split_bf16(t * t)
    return (_dot(hi, ones) + _dot(lo, ones)) * (1.0 / HEAD_DIM)


def _rotate_half(p, head):
    w = p.shape[1]
    half = head // 2
    lane = lax.broadcasted_iota(jnp.int32, (1, w), 1)
    first = (lane & (head - 1)) < half
    from_right = pltpu.roll(p, w - half, 1)
    from_left = pltpu.roll(p, half, 1)
    return jnp.where(first, -from_right, from_left)


def _inproj_kernel(x_ref, mod_ref, gain_ref, w_ref, tab_ref, gq_ref, gk_ref, ones_ref,
                   qa_ref, ka_ref, va_ref, qb_ref, kb_ref, vb_ref, qc_ref, kc_ref, vc_ref):
    x = x_ref[...]
    mod = mod_ref[...]
    ms = jnp.mean(x * x, axis=-1, keepdims=True)
    h = (x * lax.rsqrt(ms + EPS)) * gain_ref[...]
    h = h * (1.0 + mod[1:2]) + mod[0:1]
    hb = h.astype(BF16)

    def proj(a, b):
        return _dot(hb, w_ref[:, a:b])

    pa = proj(0, 3 * W_A)
    qa_ref[...] = (pa[:, :W_A] * (HEAD_DIM ** -0.5)).astype(BF16)
    ka_ref[...] = pa[:, W_A:2 * W_A].astype(BF16)
    va_ref[...] = pa[:, 2 * W_A:].astype(BF16)

    tab = tab_ref[...]
    cos_b = jnp.concatenate([tab[:, 0:LANES]] * 2, axis=1)
    sin_b = jnp.concatenate([tab[:, LANES:2 * LANES]] * 2, axis=1)
    cos_c1 = tab[:, 2 * LANES:3 * LANES]
    sin_c1 = tab[:, 3 * LANES:4 * LANES]
    cos_c = jnp.concatenate([cos_c1] * 3, axis=1)
    sin_c = jnp.concatenate([sin_c1] * 3, axis=1)

    o_b = 3 * W_A
    pb = proj(o_b, o_b + 3 * W_B)
    qb = pb[:, :W_B]
    kb = pb[:, W_B:2 * W_B]
    qb = qb * cos_b + _rotate_half(qb, DIFF_QK_DIM) * sin_b
    qb_ref[...] = (qb * (DIFF_QK_DIM ** -0.5 * LOG2E)).astype(BF16)
    kb_ref[...] = (kb * cos_b + _rotate_half(kb, DIFF_QK_DIM) * sin_b).astype(BF16)
    vb_ref[...] = pb[:, 2 * W_B:].astype(BF16)

    o_c = o_b + 3 * W_B
    pc = proj(o_c, IN_WIDTH)
    ones = ones_ref[...]
    qc = pc[:, :W_C]
    kc = pc[:, W_C:W_C + W_KC]
    nq = lax.rsqrt(_head_mean_sq(qc, ones) + EPS)
    nk = lax.rsqrt(_head_mean_sq(kc, ones[:W_KC, :W_KC]) + EPS)
    gq = gq_ref[...]
    gk = gk_ref[...]
    q = nq * (qc * gq[0:1] * cos_c + _rotate_half(qc, HEAD_DIM) * gq[1:2] * sin_c)
    qc_ref[...] = (q * (HEAD_DIM ** -0.5 * LOG2E)).astype(BF16)
    k = nk * (kc * gk[0:1] * cos_c1 + _rotate_half(kc, HEAD_DIM) * gk[1:2] * sin_c1)
    kc_ref[...] = k.astype(BF16)
    vc_ref[...] = pc[:, W_C + W_KC:].astype(BF16)


def _in_projection(xs, modsel, gain, w_ext, tab, gq, gk, ones, n_lat_tiles):
    b, t_all, d = xs.shape
    n_tiles = t_all // TILE
    widths = (W_A, W_A, W_A, W_B, W_B, W_B, W_C, W_KC, W_KC)
    tok = lambda bi, ti: (bi, ti, 0)
    const2 = lambda bi, ti: (0, 0)
    return pl.pallas_call(
        _inproj_kernel,
        out_shape=[jax.ShapeDtypeStruct((b, t_all, w), BF16) for w in widths],
        grid=(b, n_tiles),
        in_specs=[pl.BlockSpec((None, TILE, d), tok),
                  pl.BlockSpec((None, 6, d), lambda bi, ti: (2 * bi + (ti >= n_lat_tiles).astype(jnp.int32), 0, 0)),
                  pl.BlockSpec((1, d), const2),
                  pl.BlockSpec((d, IN_WIDTH), const2),
                  pl.BlockSpec((TILE, 4 * LANES), lambda bi, ti: (ti, 0)),
                  pl.BlockSpec((2, W_C), const2),
                  pl.BlockSpec((2, W_KC), const2),
                  pl.BlockSpec((W_C, W_C), const2)],
        out_specs=[pl.BlockSpec((None, TILE, w), tok) for w in widths],
        compiler_params=_cparams(2),
        name="in_projection",
    )(xs, modsel, gain, w_ext, tab, gq, gk, ones)


def _flash_kernel(q_ref, k_ref, v_ref, aux_ref, o_ref, va_ref, vb_ref, qs_ref, acc_ref, m_ref,
                  s0_ref, s1_ref, mb0_ref, mb1_ref, *,
                  n_qblk, n_sub, tk, n_lat_blocks, ctx_tile, qt_off, mode, use_exp2, lam_init):
    exp_fn = jnp.exp2 if use_exp2 else jnp.exp
    qt = pl.program_id(2) + qt_off
    sub_w = LANES // n_sub
    half = LANES // 2
    lane = lax.broadcasted_iota(jnp.int32, (1, LANES), 1)
    lower = lane < half
    n_pieces = n_qblk * n_sub
    ma = (n_pieces // 2) * TILE
    m_rows = n_pieces * TILE
    ctx_start = n_lat_blocks * tk

    @pl.when(pl.program_id(2) == 0)
    def _():
        v = v_ref[...].astype(F32)
        va_ref[...] = jnp.where(lower, v, 1.0).astype(BF16)
        vb_ref[...] = jnp.where(lower, 1.0, v).astype(BF16)

    ia, ib = 0, n_pieces // 2
    for blk in range(n_qblk):
        qf = q_ref[:, blk * LANES:(blk + 1) * LANES].astype(F32)
        for sub in range(n_sub):
            msk = (lane >= sub * sub_w) & (lane < (sub + 1) * sub_w)
            piece = jnp.where(msk, qf, 0.0).astype(BF16)
            if sub * sub_w < half:
                qs_ref[ia * TILE:(ia + 1) * TILE, :] = piece
                ia += 1
            else:
                qs_ref[ib * TILE:(ib + 1) * TILE, :] = piece
                ib += 1

    s_bufs = (s0_ref, s1_ref)
    mb_bufs = (mb0_ref, mb1_ref)

    def scores(start, size, slot):
        s = _dot_nt(qs_ref[...], k_ref[pl.ds(start, size), :])
        s_bufs[slot][:, :size] = s
        mb = jnp.max(s, axis=-1, keepdims=True)
        mb_bufs[slot][...] = jnp.broadcast_to(mb, (m_rows, LANES))

    def accumulate(start, size, slot, first):
        mb = mb_bufs[slot][...]
        if first:
            m_new = mb
        else:
            m_old = m_ref[...]
            m_new = jnp.maximum(m_old, mb)
        s_ref = s_bufs[slot]
        cols = [s_ref[:, c * LANES:(c + 1) * LANES] - m_new for c in range(size // LANES)]
        if use_exp2:
            p = jnp.concatenate([jnp.exp2(d.astype(BF16)) for d in cols], axis=1)
        else:
            p = jnp.concatenate([jnp.exp(d).astype(BF16) for d in cols], axis=1)
        pva = _dot(p[:ma], va_ref[pl.ds(start, size), :])
        pvb = _dot(p[ma:], vb_ref[pl.ds(start, size), :])
        if first:
            acc_ref[:ma, :] = pva
            acc_ref[ma:, :] = pvb
        else:
            alpha = exp_fn(m_old - m_new)
            acc_ref[:ma, :] = alpha[:ma] * acc_ref[:ma, :] + pva
            acc_ref[ma:, :] = alpha[ma:] * acc_ref[ma:, :] + pvb
        m_ref[...] = m_new

    def lat(j):
        return pl.multiple_of(j * tk, tk)

    def latent_queries():
        scores(ctx_start, CTX_LEN, 0)
        scores(lat(0), tk, 1)
        accumulate(ctx_start, CTX_LEN, 0, True)

        def pair(i):
            scores(lat(2 * i + 1), tk, 0)
            accumulate(lat(2 * i), tk, 1, False)
            scores(lat(2 * i + 2), tk, 1)
            accumulate(lat(2 * i + 1), tk, 0, False)

        def body(i, carry):
            for u in range(PAIRS_PER_STEP):
                pair(i * PAIRS_PER_STEP + u)
            return carry

        n_pairs = (n_lat_blocks - 2) // 2
        n_steps = n_pairs // PAIRS_PER_STEP
        lax.fori_loop(0, n_steps, body, 0)
        for i in range(n_steps * PAIRS_PER_STEP, n_pairs):
            pair(i)
        scores(lat(n_lat_blocks - 1), tk, 0)
        accumulate(lat(n_lat_blocks - 2), tk, 1, False)
        accumulate(lat(n_lat_blocks - 1), tk, 0, False)

    def context_queries():
        scores(ctx_start, CTX_LEN, 0)
        accumulate(ctx_start, CTX_LEN, 0, True)

    if ctx_tile is None:
        latent_queries()
    else:
        pl.when(qt != ctx_tile)(latent_queries)
        pl.when(qt == ctx_tile)(context_queries)

    acc = acc_ref[...]
    r = acc / pltpu.roll(acc, half, 1)
    ra, rb = r[:ma], r[ma:]
    if mode == "plain":
        for i in range(n_pieces // 2):
            o = jnp.where(lower, ra[i * TILE:(i + 1) * TILE], rb[i * TILE:(i + 1) * TILE])
            o_ref[:, i * LANES:(i + 1) * LANES] = o.astype(BF16)
    else:
        aux = aux_ref[...]
        l1 = jnp.sum(aux[0:1] * aux[1:2], axis=-1, keepdims=True)
        l2 = jnp.sum(aux[2:3] * aux[3:4], axis=-1, keepdims=True)
        lam = jnp.exp(l1) - jnp.exp(l2) + lam_init
        oa = ra[:TILE] - lam * ra[TILE:]
        ob = rb[:TILE] - lam * rb[TILE:]
        o = jnp.where(lower, oa, ob)
        sq = o * o
        ss_a = jnp.sum(jnp.where(lower, sq, 0.0), axis=-1, keepdims=True)
        ss_b = jnp.sum(jnp.where(lower, 0.0, sq), axis=-1, keepdims=True)
        ms = jnp.where(lower, ss_a, ss_b) * (1.0 / HEAD_DIM)
        o = (o * lax.rsqrt(ms + EPS)) * aux[4:5]
        o_ref[...] = (o * (1.0 - lam_init)).astype(BF16)


def _flash(q, k, v, aux, *, n_qblk, n_sub, n_hp, qt_off, n_qt, n_lat, ctx_tile, mode, use_exp2,
           lam_init=0.0):
    b, t_all, _ = q.shape
    qw = n_qblk * LANES
    tk = FLASH_TK if n_lat % (2 * FLASH_TK) == 0 else 512
    assert n_lat % (2 * tk) == 0 and tk >= CTX_LEN
    m_rows = n_qblk * n_sub * TILE
    kern = functools.partial(_flash_kernel, n_qblk=n_qblk, n_sub=n_sub, tk=tk,
                             n_lat_blocks=n_lat // tk, ctx_tile=ctx_tile, qt_off=qt_off,
                             mode=mode, use_exp2=use_exp2, lam_init=lam_init)
    return pl.pallas_call(
        kern,
        out_shape=jax.ShapeDtypeStruct((b, n_qt * TILE, n_hp * qw), BF16),
        grid=(b, n_hp, n_qt),
        in_specs=[pl.BlockSpec((None, TILE, qw), lambda bi, hp, qt: (bi, qt + qt_off, hp)),
                  pl.BlockSpec((None, t_all, LANES), lambda bi, hp, qt: (bi, 0, hp)),
                  pl.BlockSpec((None, t_all, LANES), lambda bi, hp, qt: (bi, 0, hp)),
                  pl.BlockSpec((8, LANES), lambda bi, hp, qt: (0, 0))],
        out_specs=pl.BlockSpec((None, TILE, qw), lambda bi, hp, qt: (bi, qt, hp)),
        scratch_shapes=[pltpu.VMEM((t_all, LANES), BF16),
                        pltpu.VMEM((t_all, LANES), BF16),
                        pltpu.VMEM((m_rows, LANES), BF16),
                        pltpu.VMEM((m_rows, LANES), F32),
                        pltpu.VMEM((m_rows, LANES), F32),
                        pltpu.VMEM((m_rows, tk), F32),
                        pltpu.VMEM((m_rows, tk), F32),
                        pltpu.VMEM((m_rows, LANES), F32),
                        pltpu.VMEM((m_rows, LANES), F32)],
        compiler_params=_cparams(3),
        name="flash_" + mode,
    )(q, k, v, aux)


def _na_kernel(q_ref, k0, k1, k2, k3, v0, v1, v2, v3, kc_ref, vc_ref, bias_ref, o_ref):
    lane = lax.broadcasted_iota(jnp.int32, (1, LANES), 1)
    lower = lane < LANES // 2
    qf = q_ref[...].astype(F32)
    kw = jnp.concatenate([k0[...], k1[...], k2[...], k3[...]], axis=0)
    vw = jnp.concatenate([v0[...], v1[...], v2[...], v3[...]], axis=0)
    kc = kc_ref[...]
    vc = vc_ref[...]
    outs = []
    for hh in range(2):
        msk = lower if hh == 0 else jnp.logical_not(lower)
        qh = jnp.where(msk, qf, 0.0).astype(BF16)
        s_w = _dot_nt(qh, kw) + bias_ref[hh]
        s_c = _dot_nt(qh, kc)
        m = jnp.maximum(jnp.max(s_w, axis=-1, keepdims=True), jnp.max(s_c, axis=-1, keepdims=True))
        p_w = jnp.exp(s_w - m)
        p_c = jnp.exp(s_c - m)
        l = jnp.sum(p_w, axis=-1, keepdims=True) + jnp.sum(p_c, axis=-1, keepdims=True)
        o = _dot(p_w.astype(BF16), vw) + _dot(p_c.astype(BF16), vc)
        outs.append(o / l)
    o_ref[...] = jnp.where(lower, outs[0], outs[1]).astype(BF16)


def _neighbourhood_attention(qa, ka, va, bias, n_lat):
    b = qa.shape[0]
    q_tok = NA_QROWS * GRID_W
    v_tok = q_tok // 2
    n_rb = n_lat // q_tok
    n_view = n_lat // v_tok
    ctx_blk = n_lat // v_tok

    def view(j):
        return lambda rb, hp, bi: (bi, jnp.clip(2 * rb - 1 + j, 0, n_view - 1), hp)

    kv_specs = [pl.BlockSpec((None, v_tok, LANES), view(j)) for j in range(4)]
    ctx_spec = pl.BlockSpec((None, CTX_LEN, LANES), lambda rb, hp, bi: (bi, ctx_blk, hp))

    def bias_map(rb, hp, bi):
        pat = jnp.where(rb == 0, 0, jnp.where(rb == n_rb - 1, 2, 1))
        return (pat, hp, 0, 0)

    return pl.pallas_call(
        _na_kernel,
        out_shape=jax.ShapeDtypeStruct((b, n_lat, W_A), BF16),
        grid=(n_rb, NA_HEADS // 2, b),
        in_specs=[pl.BlockSpec((None, q_tok, LANES), lambda rb, hp, bi: (bi, rb, hp))]
                 + kv_specs + kv_specs + [ctx_spec, ctx_spec,
                 pl.BlockSpec((None, 2, q_tok, NA_KROWS * GRID_W), bias_map)],
        out_specs=pl.BlockSpec((None, q_tok, LANES), lambda rb, hp, bi: (bi, rb, hp)),
        compiler_params=_cparams(3),
        name="neighbourhood_attention",
    )(qa, ka, ka, ka, ka, va, va, va, va, ka, va, bias)


def _na_bias_table(rpb, rows):
    cols = np.arange(GRID_W)
    c0 = np.clip(cols - NA_WIN_W // 2, 0, GRID_W - NA_WIN_W)
    cc = cols[None, :]
    col_ok = (cc >= c0[:, None]) & (cc < c0[:, None] + NA_WIN_W)
    dc = np.clip(cc - cols[:, None] + (NA_WIN_W - 1), 0, 2 * NA_WIN_W - 2)
    e = jnp.where(col_ok[None, None], rpb.astype(F32)[:, :, dc], NEG)
    e = jnp.concatenate([e, jnp.full_like(e[:, :1], NEG)], axis=1)
    a = np.arange(NA_QROWS)[:, None]
    i = np.arange(NA_KROWS)[None, :]
    pats = []
    for r_base in (0, NA_QROWS, rows - NA_QROWS):
        r = r_base + a
        key_row = r_base - NA_WIN_H // 2 + i
        r0 = np.clip(r - NA_WIN_H // 2, 0, rows - NA_WIN_H)
        ok = (key_row >= r0) & (key_row < r0 + NA_WIN_H) & (key_row >= 0) & (key_row < rows)
        dr = np.where(ok, key_row - r + (NA_WIN_H - 1), 2 * NA_WIN_H - 1)
        pats.append(dr)
    dr_all = np.stack(pats)
    t = e[:, dr_all]
    t = t.transpose(1, 0, 2, 4, 3, 5)
    return t.reshape(3, NA_HEADS, NA_QROWS * GRID_W, NA_KROWS * GRID_W)


def _outproj_kernel(x_ref, oa_ref, ob_ref, oc_ref, wa_ref, wb_ref, wc_ref, mod_ref, gain_ref,
                    wrh_ref, wrl_ref, br_ref, tri_ref, x1_ref, tok_ref, route_ref, cnt_ref, run_ref,
                    *, region):
    mod = mod_ref[...]
    y = _dot(oa_ref[...], wa_ref[...]) + _dot(ob_ref[...], wb_ref[...]) + _dot(oc_ref[...], wc_ref[...])
    x1 = x_ref[...] + mod[2:3] * y
    x1_ref[...] = x1
    ms = jnp.mean(x1 * x1, axis=-1, keepdims=True)
    t = (x1 * lax.rsqrt(ms + EPS)) * gain_ref[...]
    t = t * (1.0 + mod[4:5]) + mod[3:4]
    tok_ref[...] = t

    t_hi, t_lo = _split_bf16(t)
    wrh = wrh_ref[...]
    logits = _dot(t_hi, wrh) + _dot(t_lo, wrh) + _dot(t_hi, wrl_ref[...]) + br_ref[...]

    lane = lax.broadcasted_iota(jnp.int32, logits.shape, 1)
    lane_f = lane.astype(F32)
    is_g = lane < N_GROUPS
    gl = jnp.where(is_g, logits, NEG)
    gmax = jnp.max(gl, axis=-1, keepdims=True)
    g_sel = jnp.min(jnp.where(gl == gmax, lane_f, 1e9), axis=-1, keepdims=True)
    p_grp = 1.0 / jnp.sum(jnp.where(is_g, jnp.exp(gl - gmax), 0.0), axis=-1, keepdims=True)
    grp_of_lane = lax.shift_right_arithmetic(lane - N_GROUPS, 2).astype(F32)
    in_grp = (lane >= N_GROUPS) & (lane < N_GROUPS + N_EXPERTS) & (grp_of_lane == g_sel)
    el = jnp.where(in_grp, logits, NEG)
    v1 = jnp.max(el, axis=-1, keepdims=True)
    i1 = jnp.min(jnp.where(el == v1, lane_f, 1e9), axis=-1, keepdims=True)
    el2 = jnp.where(lane_f == i1, NEG, el)
    v2 = jnp.max(el2, axis=-1, keepdims=True)
    i2 = jnp.min(jnp.where(el2 == v2, lane_f, 1e9), axis=-1, keepdims=True)
    e2 = jnp.exp(v2 - v1)
    den = 1.0 + e2
    w1 = p_grp / den
    w2 = p_grp * e2 / den

    @pl.when((pl.program_id(0) == 0) & (pl.program_id(1) == 0))
    def _():
        run_ref[...] = jnp.zeros(run_ref.shape, F32)

    ind = jnp.where(lane_f == i1, 1.0, 0.0) + jnp.where(lane_f == i2, 1.0, 0.0)
    rank = _dot(tri_ref[...], ind.astype(BF16)) + run_ref[0:1, :]

    def pick(m, l):
        return jnp.sum(jnp.where(lane_f == l, m, 0.0), axis=-1, keepdims=True)

    pos1 = (i1 - N_GROUPS) * region + pick(rank, i1)
    pos2 = (i2 - N_GROUPS) * region + pick(rank, i2)
    route_ref[...] = jnp.where(lane == 0, pos1, jnp.where(lane == 1, pos2,
                               jnp.where(lane == 2, w1, jnp.where(lane == 3, w2, 0.0))))
    run = run_ref[...] + jnp.sum(ind, axis=0, keepdims=True)
    run_ref[...] = run
    cnt_ref[...] = run


def _out_projection(xs, oa, ob, oc, wa, wb, wc, modsel, gain, wrh, wrl, br, n_tiles, n_lat_tiles):
    b, _, d = xs.shape
    tok = lambda bi, ti: (bi, ti, 0)
    const2 = lambda bi, ti: (0, 0)
    rows = n_tiles * TILE
    tri = jnp.asarray(np.tril(np.ones((TILE, TILE), np.float32), -1), BF16)
    return pl.pallas_call(
        functools.partial(_outproj_kernel, region=b * rows),
        out_shape=[jax.ShapeDtypeStruct((b, rows, d), F32),
                   jax.ShapeDtypeStruct((b, rows, d), F32),
                   jax.ShapeDtypeStruct((b, rows, LANES), F32),
                   jax.ShapeDtypeStruct((8, LANES), F32)],
        grid=(b, n_tiles),
        in_specs=[pl.BlockSpec((None, TILE, d), tok),
                  pl.BlockSpec((None, TILE, W_A), tok),
                  pl.BlockSpec((None, TILE, W_B), tok),
                  pl.BlockSpec((None, TILE, W_C), tok),
                  pl.BlockSpec((W_A, d), const2),
                  pl.BlockSpec((W_B, d), const2),
                  pl.BlockSpec((W_C, d), const2),
                  pl.BlockSpec((None, 6, d), lambda bi, ti: (2 * bi + (ti >= n_lat_tiles).astype(jnp.int32), 0, 0)),
                  pl.BlockSpec((1, d), const2),
                  pl.BlockSpec((d, LANES), const2),
                  pl.BlockSpec((d, LANES), const2),
                  pl.BlockSpec((1, LANES), const2),
                  pl.BlockSpec((TILE, TILE), const2)],
        out_specs=[pl.BlockSpec((None, TILE, d), tok),
                   pl.BlockSpec((None, TILE, d), tok),
                   pl.BlockSpec((None, TILE, LANES), tok),
                   pl.BlockSpec((8, LANES), const2)],
        scratch_shapes=[pltpu.VMEM((8, LANES), F32)],
        compiler_params=_cparams(2),
        name="out_projection",
    )(xs, oa, ob, oc, wa, wb, wc, modsel, gain, wrh, wrl, br, tri)


def _sc_mesh():
    return plsc.VectorSubcoreMesh(core_axis_name="core", subcore_axis_name="subcore")


def _sc_worker_base(per_worker):
    wid = lax.axis_index("subcore") * SC_CORES + lax.axis_index("core")
    return wid * per_worker


def _sc_scratch(d, dtype):
    return ([pltpu.VMEM((SC_ROWS,), jnp.int32)] * SC_BUFS + [pltpu.VMEM((SC_ROWS, d), dtype)] * SC_BUFS
            + [pltpu.SemaphoreType.DMA] * (2 * SC_BUFS))


def _sc_split(scratch):
    return (scratch[:SC_BUFS], scratch[SC_BUFS:2 * SC_BUFS], scratch[2 * SC_BUFS:3 * SC_BUFS],
            scratch[3 * SC_BUFS:])


def _sc_scatter_rows(x, idx, n_out):
    n, d = x.shape
    per_worker = 2 * n // (SC_CORES * SC_SUBCORES)
    assert per_worker % (SC_ROWS * SC_BUFS) == 0 and n % SC_ROWS == 0

    @functools.partial(pl.kernel, out_type=jax.ShapeDtypeStruct((n_out, d), x.dtype),
                       mesh=_sc_mesh(), scratch_types=_sc_scratch(d, x.dtype))
    def scatter(x_hbm, i_hbm, o_hbm, *scratch):
        idx_v, rows_v, sem_in, sem_out = _sc_split(scratch)
        base = _sc_worker_base(per_worker)

        @pl.loop(0, per_worker // SC_ROWS, step=SC_BUFS)
        def _(c):
            reads = []
            for u in range(SC_BUFS):
                a = pl.multiple_of(base + (c + u) * SC_ROWS, SC_ROWS)
                t = pl.multiple_of(lax.rem(a, n), SC_ROWS)
                pltpu.sync_copy(i_hbm.at[pl.ds(a, SC_ROWS)], idx_v[u])
                reads.append(pltpu.async_copy(x_hbm.at[pl.ds(t, SC_ROWS)], rows_v[u], sem_in[u]))
            writes = []
            for u in range(SC_BUFS):
                reads[u].wait()
                writes.append(pltpu.async_copy(rows_v[u], o_hbm.at[idx_v[u]], sem_out[u]))
            for w in writes:
                w.wait()

    return scatter(x, idx)


def _sc_gather_rows(src, idx):
    m = idx.shape[0]
    d = src.shape[1]
    per_worker = m // (SC_CORES * SC_SUBCORES)
    assert per_worker % (SC_ROWS * SC_BUFS) == 0

    @functools.partial(pl.kernel, out_type=jax.ShapeDtypeStruct((m, d), src.dtype),
                       mesh=_sc_mesh(), scratch_types=_sc_scratch(d, src.dtype))
    def gather(s_hbm, i_hbm, o_hbm, *scratch):
        idx_v, rows_v, sem_in, sem_out = _sc_split(scratch)
        base = _sc_worker_base(per_worker)

        @pl.loop(0, per_worker // SC_ROWS, step=SC_BUFS)
        def _(c):
            offs, reads = [], []
            for u in range(SC_BUFS):
                a = pl.multiple_of(base + (c + u) * SC_ROWS, SC_ROWS)
                offs.append(a)
                pltpu.sync_copy(i_hbm.at[pl.ds(a, SC_ROWS)], idx_v[u])
                reads.append(pltpu.async_copy(s_hbm.at[idx_v[u]], rows_v[u], sem_in[u]))
            writes = []
            for u in range(SC_BUFS):
                reads[u].wait()
                writes.append(pltpu.async_copy(rows_v[u], o_hbm.at[pl.ds(offs[u], SC_ROWS)], sem_out[u]))
            for w in writes:
                w.wait()

    return gather(src, idx)


def _expert_ffn_kernel(blk_ref, exp_ref, x_ref, wg_ref, wu_ref, wd_ref, y_ref, wgb_ref, wub_ref, wdb_ref):
    j = pl.program_id(0)

    @pl.when((j == 0) | (exp_ref[j] != exp_ref[jnp.maximum(j - 1, 0)]))
    def _():
        wgb_ref[...] = wg_ref[...].astype(BF16)
        wub_ref[...] = wu_ref[...].astype(BF16)
        wdb_ref[...] = wd_ref[...].astype(BF16)

    x = x_ref[...].astype(BF16)
    hid = jax.nn.silu(_dot(x, wgb_ref[...])) * _dot(x, wub_ref[...])
    y_ref[...] = _dot(hid.astype(BF16), wdb_ref[...])


def _expert_ffn(xs, blk, exp, wg, wu, wd, layer):
    rows, d = xs.shape
    w_map = lambda j, blk, exp: (layer, exp[j], 0, 0)
    return pl.pallas_call(
        _expert_ffn_kernel,
        out_shape=jax.ShapeDtypeStruct((rows, d), F32),
        grid_spec=pltpu.PrefetchScalarGridSpec(
            num_scalar_prefetch=2,
            grid=(blk.shape[0],),
            in_specs=[pl.BlockSpec((MOE_TILE, d), lambda j, blk, exp: (blk[j], 0)),
                      pl.BlockSpec((None, None, d, EXPERT_HIDDEN), w_map),
                      pl.BlockSpec((None, None, d, EXPERT_HIDDEN), w_map),
                      pl.BlockSpec((None, None, EXPERT_HIDDEN, d), w_map)],
            out_specs=pl.BlockSpec((MOE_TILE, d), lambda j, blk, exp: (blk[j], 0)),
            scratch_shapes=[pltpu.VMEM((d, EXPERT_HIDDEN), BF16),
                            pltpu.VMEM((d, EXPERT_HIDDEN), BF16),
                            pltpu.VMEM((EXPERT_HIDDEN, d), BF16)]),
        compiler_params=_cparams(1),
        name="expert_ffn",
    )(blk, exp, xs, wg, wu, wd)


def _combine_kernel(x1_ref, y1_ref, y2_ref, route_ref, mod_ref, fgain_ref, o_ref, *, final):
    route = route_ref[...]
    y = route[:, 2:3] * y1_ref[...] + route[:, 3:4] * y2_ref[...]
    x2 = x1_ref[...] + mod_ref[5:6, :] * y
    if final:
        ms = jnp.mean(x2 * x2, axis=-1, keepdims=True)
        x2 = (x2 * lax.rsqrt(ms + EPS)) * fgain_ref[...]
    o_ref[...] = x2


def _combine(x1, ys, route, modsel, fgain, n_lat_tiles, final):
    b, rows, d = x1.shape
    n_t = rows // TILE
    tok = lambda bi, ti: (bi, ti, 0)
    return pl.pallas_call(
        functools.partial(_combine_kernel, final=final),
        out_shape=jax.ShapeDtypeStruct((b, rows, d), F32),
        grid=(b, n_t),
        in_specs=[pl.BlockSpec((None, TILE, d), tok),
                  pl.BlockSpec((TILE, d), lambda bi, ti: (bi * n_t + ti, 0)),
                  pl.BlockSpec((TILE, d), lambda bi, ti: (b * n_t + bi * n_t + ti, 0)),
                  pl.BlockSpec((None, TILE, LANES), tok),
                  pl.BlockSpec((None, 6, d), lambda bi, ti: (2 * bi + (ti >= n_lat_tiles).astype(jnp.int32), 0, 0)),
                  pl.BlockSpec((1, d), lambda bi, ti: (0, 0))],
        out_specs=pl.BlockSpec((None, TILE, d), tok),
        compiler_params=_cparams(2),
        name="moe_combine",
    )(x1, ys, ys, route, modsel, fgain)


def _routed_moe(tok, route, cnt, x1, wg, wu, wd, layer, modsel, fgain, n_lat_tiles, final):
    b, rows, d = x1.shape
    n = b * rows
    flat = route.reshape(n, LANES)
    idx = jnp.concatenate([flat[:, 0], flat[:, 1]]).astype(jnp.int32)
    xs = _sc_scatter_rows(tok.reshape(n, d), idx, N_EXPERTS * n)

    counts = cnt[0, N_GROUPS:N_GROUPS + N_EXPERTS].astype(jnp.int32)
    tiles = (counts + MOE_TILE - 1) // MOE_TILE
    ends = jnp.cumsum(tiles)
    n_sched = 2 * n // MOE_TILE + N_EXPERTS
    j = jnp.minimum(jnp.arange(n_sched, dtype=jnp.int32), ends[-1] - 1)
    exp = jnp.sum((j[:, None] >= ends[None, :]).astype(jnp.int32), axis=1)
    blk = exp * (n // MOE_TILE) + j - (ends - tiles)[exp]

    ys = _expert_ffn(xs, blk, exp, wg, wu, wd, layer)
    yg = _sc_gather_rows(ys, idx)
    return _combine(x1, yg, route, modsel, fgain, n_lat_tiles, final)


def _rope_tables(n_lat):
    t = jnp.arange(n_lat)
    row = (t // GRID_W).astype(F32)
    col = (t % GRID_W).astype(F32)

    def cs(dim):
        quarter = dim // 4
        freqs = ROPE_THETA ** (-jnp.arange(quarter, dtype=F32) / quarter)
        ang = jnp.concatenate([row[:, None] * freqs, col[:, None] * freqs], axis=-1)
        cos = jnp.tile(jnp.cos(ang), (1, 2 * LANES // dim))
        sin = jnp.tile(jnp.sin(ang), (1, 2 * LANES // dim))
        cos = jnp.concatenate([cos, jnp.ones((CTX_LEN, LANES), F32)], axis=0)
        sin = jnp.concatenate([sin, jnp.zeros((CTX_LEN, LANES), F32)], axis=0)
        return cos, sin

    cos_b, sin_b = cs(DIFF_QK_DIM)
    cos_c, sin_c = cs(HEAD_DIM)
    return jnp.concatenate([cos_b, sin_b, cos_c, sin_c], axis=1)


def _reordered_w_in(w_in):
    o_c = 3 * W_A + 3 * W_B
    heads = [w_in[:, o_c + h * HEAD_DIM:o_c + (h + 1) * HEAD_DIM] for h in GQA_Q_ORDER]
    return jnp.concatenate([w_in[:, :o_c]] + heads + [w_in[:, o_c + W_C:]], axis=1).astype(BF16)


def kernel(x, c, ctx, c_ctx, w_mod, b_mod, norm_attn, norm_ffn, w_in, w_out, na_rpb, diff_lambda_q1, diff_lambda_k1, diff_lambda_q2, diff_lambda_k2, diff_subln, gqa_q_norm, gqa_k_norm, router_group_w, router_group_b, router_expert_w, router_expert_b, w_gate, w_up, w_down, final_norm):
    b, s, d = x.shape
    assert d == D_MODEL and ctx.shape[1] == CTX_LEN and s % (NA_QROWS * GRID_W) == 0
    rows = s // GRID_W
    assert rows >= 2 * NA_QROWS
    t_all = s + CTX_LEN
    n_lat_tiles = s // TILE

    c_rows = jnp.zeros((8, d), F32).at[:b].set(c).at[b].set(c_ctx)
    mod = _modulation(c_rows, w_mod, b_mod)

    tab = _rope_tables(s)
    hidx = np.arange(HEAD_DIM)
    partner = np.where(hidx < HEAD_DIM // 2, hidx + HEAD_DIM // 2, hidx - HEAD_DIM // 2)
    blk = np.arange(W_C) // HEAD_DIM
    ones = jnp.asarray((blk[:, None] == blk[None, :]).astype(np.float32), BF16)
    oc_rows = W_A + W_B + np.concatenate([h * HEAD_DIM + np.arange(HEAD_DIM) for h in GQA_Q_ORDER])
    dummy_aux = jnp.zeros((8, LANES), F32)

    xs = jnp.concatenate([x, ctx], axis=1)
    for l in range(DEPTH):
        ctx_out = l < DEPTH - 1
        lam_init = 0.8 - 0.6 * math.exp(-0.3 * l)
        m_lat = mod[l, :b].reshape(b, 1, 6, d)
        m_ctx = jnp.broadcast_to(mod[l, b].reshape(1, 1, 6, d), (b, 1, 6, d))
        modsel = jnp.concatenate([m_lat, m_ctx], axis=1).reshape(2 * b, 6, d)

        gq = jnp.stack([jnp.tile(gqa_q_norm[l], GQA_Q_HEADS), jnp.tile(gqa_q_norm[l][partner], GQA_Q_HEADS)])
        gk = jnp.stack([jnp.tile(gqa_k_norm[l], GQA_KV_HEADS), jnp.tile(gqa_k_norm[l][partner], GQA_KV_HEADS)])
        qa, ka, va, qb, kb, vb, qc, kc, vc = _in_projection(
            xs, modsel, norm_attn[l][None], _reordered_w_in(w_in[l]), tab, gq, gk, ones, n_lat_tiles)

        n_qt = n_lat_tiles + 1 if ctx_out else n_lat_tiles
        ctx_tile = n_lat_tiles if ctx_out else None
        oa = _neighbourhood_attention(qa, ka, va, _na_bias_table(na_rpb[l], rows), s)
        if ctx_out:
            oa_ctx = _flash(qa, ka, va, dummy_aux, n_qblk=1, n_sub=2, n_hp=NA_HEADS // 2,
                            qt_off=n_lat_tiles, n_qt=1, n_lat=s, ctx_tile=n_lat_tiles,
                            mode="plain", use_exp2=False)
            oa = jnp.concatenate([oa, oa_ctx], axis=1)
        pad = lambda v: jnp.pad(v, (0, LANES - v.shape[0]))
        aux = jnp.stack([pad(diff_lambda_q1[l]), pad(diff_lambda_k1[l]), pad(diff_lambda_q2[l]),
                         pad(diff_lambda_k2[l]), jnp.tile(diff_subln[l], 2),
                         jnp.zeros((LANES,), F32), jnp.zeros((LANES,), F32), jnp.zeros((LANES,), F32)])
        ob = _flash(qb, kb, vb, aux, n_qblk=1, n_sub=4, n_hp=DIFF_HEADS // 2, qt_off=0, n_qt=n_qt,
                    n_lat=s, ctx_tile=ctx_tile, mode="diff", use_exp2=True, lam_init=lam_init)
        oc = _flash(qc, kc, vc, dummy_aux, n_qblk=3, n_sub=2, n_hp=1, qt_off=0, n_qt=n_qt,
                    n_lat=s, ctx_tile=ctx_tile, mode="plain", use_exp2=True)

        w_o = w_out[l]
        wr = jnp.zeros((d, LANES), F32)
        wr = wr.at[:, :N_GROUPS].set(router_group_w[l]).at[:, N_GROUPS:N_GROUPS + N_EXPERTS].set(router_expert_w[l])
        wrh, wrl = _split_bf16(wr)
        br = jnp.zeros((1, LANES), F32)
        br = br.at[0, :N_GROUPS].set(router_group_b[l]).at[0, N_GROUPS:N_GROUPS + N_EXPERTS].set(router_expert_b[l])
        x1, tok, route, cnt = _out_projection(
            xs, oa, ob, oc, w_o[:W_A].astype(BF16), w_o[W_A:W_A + W_B].astype(BF16), w_o[oc_rows].astype(BF16),
            modsel, norm_ffn[l][None], wrh, wrl, br, n_qt, n_lat_tiles)
        xs = _routed_moe(tok, route, cnt, x1, w_gate, w_up, w_down, l, modsel, final_norm[None],
                         n_lat_tiles, final=not ctx_out)
    return xs
```

```python
import functools
import math

import numpy as np
import jax
import jax.numpy as jnp
from jax import lax
from jax.experimental import pallas as pl
from jax.experimental.pallas import tpu as pltpu
from jax.experimental.pallas import tpu_sc as plsc

F32 = jnp.float32
BF16 = jnp.bfloat16

D_MODEL = 1024
DEPTH = 2
GRID_W = 64
CTX_LEN = 256
HEAD_DIM = 64
NA_HEADS = 6
NA_WIN_H = 8
NA_WIN_W = 16
DIFF_HEADS = 4
DIFF_QK_DIM = 32
GQA_Q_HEADS = 6
GQA_KV_HEADS = 2
N_GROUPS = 4
EXPERTS_PER_GROUP = 4
N_EXPERTS = 16
EXPERT_HIDDEN = 512
ROPE_THETA = 10000.0
EPS = 1e-6
W_A = NA_HEADS * HEAD_DIM
W_B = DIFF_HEADS * 2 * DIFF_QK_DIM
W_C = GQA_Q_HEADS * HEAD_DIM
W_KC = GQA_KV_HEADS * HEAD_DIM
IN_WIDTH = 3 * W_A + 3 * W_B + W_C + 2 * W_KC

LANES = 128
TILE = CTX_LEN
NA_QROWS = 8
NA_KROWS = 16
NEG = -1e30
LOG2E = 1.4426950408889634
VMEM_LIMIT = 56 * 1024 * 1024
FLASH_TK = 512
PAIRS_PER_STEP = 2
MOE_TILE = 512
SC_ROWS = 16
SC_BUFS = 4
SC_CORES = 2
SC_SUBCORES = 16

GQA_Q_ORDER = (0, 3, 1, 4, 2, 5)


def _cparams(n_axes):
    return pltpu.CompilerParams(dimension_semantics=("arbitrary",) * n_axes,
                                vmem_limit_bytes=VMEM_LIMIT)


def _split_bf16(a):
    hi = a.astype(BF16)
    lo = (a - hi.astype(F32)).astype(BF16)
    return hi, lo


def _dot(a, b):
    return jnp.dot(a, b, preferred_element_type=F32)


def _dot_nt(a, b):
    return lax.dot_general(a, b, (((1,), (1,)), ((), ())), preferred_element_type=F32)


def _mod_kernel(c_ref, w_ref, b_ref, o_ref):
    c = c_ref[...]
    a = c * jax.nn.sigmoid(c)
    a_hi, a_lo = _split_bf16(a)
    w_hi, w_lo = _split_bf16(w_ref[...])
    o_ref[...] = _dot(a_hi, w_hi) + _dot(a_lo, w_hi) + _dot(a_hi, w_lo) + b_ref[...]


def _modulation(c_rows, w_mod, b_mod):
    depth, d, n = w_mod.shape
    bn = 1536
    return pl.pallas_call(
        _mod_kernel,
        out_shape=jax.ShapeDtypeStruct((depth, 8, n), F32),
        grid=(depth, n // bn),
        in_specs=[pl.BlockSpec((8, d), lambda l, j: (0, 0)),
                  pl.BlockSpec((None, d, bn), lambda l, j: (l, 0, j)),
                  pl.BlockSpec((None, 1, bn), lambda l, j: (l, 0, j))],
        out_specs=pl.BlockSpec((None, 8, bn), lambda l, j: (l, 0, j)),
        compiler_params=_cparams(2),
        name="adaln_mod",
    )(c_rows, w_mod, b_mod.reshape(depth, 1, n))


def _head_mean_sq(t, ones):
    hi, lo = _split_bf16(t * t)
    return (_dot(hi, ones) + _dot(lo, ones)) * (1.0 / HEAD_DIM)


def _rotate_half(p, head):
    w = p.shape[1]
    half = head // 2
    lane = lax.broadcasted_iota(jnp.int32, (1, w), 1)
    first = (lane & (head - 1)) < half
    from_right = pltpu.roll(p, w - half, 1)
    from_left = pltpu.roll(p, half, 1)
    return jnp.where(first, -from_right, from_left)


def _inproj_kernel(x_ref, xc_ref, mod_ref, gain_ref, w_ref, tab_ref, gq_ref, gk_ref, ones_ref,
                   qa_ref, ka_ref, va_ref, qb_ref, kb_ref, vb_ref, qc_ref, kc_ref, vc_ref,
                   *, n_lat_tiles):
    x = jnp.where(pl.program_id(1) == n_lat_tiles, xc_ref[...], x_ref[...])
    mod = mod_ref[...]
    ms = jnp.mean(x * x, axis=-1, keepdims=True)
    h = (x * lax.rsqrt(ms + EPS)) * gain_ref[...]
    h = h * (1.0 + mod[1:2]) + mod[0:1]
    hb = h.astype(BF16)

    def proj(a, b):
        return _dot(hb, w_ref[:, a:b])

    pa = proj(0, 3 * W_A)
    qa_ref[...] = (pa[:, :W_A] * (HEAD_DIM ** -0.5 * LOG2E)).astype(BF16)
    ka_ref[...] = pa[:, W_A:2 * W_A].astype(BF16)
    va_ref[...] = pa[:, 2 * W_A:].astype(BF16)

    tab = tab_ref[...]
    cos_b = jnp.concatenate([tab[:, 0:LANES]] * 2, axis=1)
    sin_b = jnp.concatenate([tab[:, LANES:2 * LANES]] * 2, axis=1)
    cos_c1 = tab[:, 2 * LANES:3 * LANES]
    sin_c1 = tab[:, 3 * LANES:4 * LANES]
    cos_c = jnp.concatenate([cos_c1] * 3, axis=1)
    sin_c = jnp.concatenate([sin_c1] * 3, axis=1)

    o_b = 3 * W_A
    pb = proj(o_b, o_b + 3 * W_B)
    qb = pb[:, :W_B]
    kb = pb[:, W_B:2 * W_B]
    qb = qb * cos_b + _rotate_half(qb, DIFF_QK_DIM) * sin_b
    qb_ref[...] = (qb * (DIFF_QK_DIM ** -0.5 * LOG2E)).astype(BF16)
    kb_ref[...] = (kb * cos_b + _rotate_half(kb, DIFF_QK_DIM) * sin_b).astype(BF16)
    vb_ref[...] = pb[:, 2 * W_B:].astype(BF16)

    o_c = o_b + 3 * W_B
    pc = proj(o_c, IN_WIDTH)
    ones = ones_ref[...]
    qc = pc[:, :W_C]
    kc = pc[:, W_C:W_C + W_KC]
    nq = lax.rsqrt(_head_mean_sq(qc, ones) + EPS)
    nk = lax.rsqrt(_head_mean_sq(kc, ones[:W_KC, :W_KC]) + EPS)
    gq = gq_ref[...]
    gk = gk_ref[...]
    q = nq * (qc * gq[0:1] * cos_c + _rotate_half(qc, HEAD_DIM) * gq[1:2] * sin_c)
    qc_ref[...] = (q * (HEAD_DIM ** -0.5 * LOG2E)).astype(BF16)
    k = nk * (kc * gk[0:1] * cos_c1 + _rotate_half(kc, HEAD_DIM) * gk[1:2] * sin_c1)
    kc_ref[...] = k.astype(BF16)
    vc_ref[...] = pc[:, W_C + W_KC:].astype(BF16)


def _token_specs(d, n_lat_tiles, ctx_blk):
    return [pl.BlockSpec((None, TILE, d), lambda bi, ti: (bi, jnp.minimum(ti, n_lat_tiles - 1), 0)),
            pl.BlockSpec((None, TILE, d), lambda bi, ti: (bi, ctx_blk, 0))]


def _in_projection(x_lat, x_ctx, ctx_blk, modsel, gain, w_ext, tab, gq, gk, ones, n_lat_tiles):
    b, _, d = x_lat.shape
    n_tiles = n_lat_tiles + 1
    t_all = n_tiles * TILE
    widths = (W_A, W_A, W_A, W_B, W_B, W_B, W_C, W_KC, W_KC)
    tok = lambda bi, ti: (bi, ti, 0)
    const2 = lambda bi, ti: (0, 0)
    return pl.pallas_call(
        functools.partial(_inproj_kernel, n_lat_tiles=n_lat_tiles),
        out_shape=[jax.ShapeDtypeStruct((b, t_all, w), BF16) for w in widths],
        grid=(b, n_tiles),
        in_specs=_token_specs(d, n_lat_tiles, ctx_blk) + [
                  pl.BlockSpec((None, 6, d), lambda bi, ti: (2 * bi + (ti >= n_lat_tiles).astype(jnp.int32), 0, 0)),
                  pl.BlockSpec((1, d), const2),
                  pl.BlockSpec((d, IN_WIDTH), const2),
                  pl.BlockSpec((TILE, 4 * LANES), lambda bi, ti: (ti, 0)),
                  pl.BlockSpec((2, W_C), const2),
                  pl.BlockSpec((2, W_KC), const2),
                  pl.BlockSpec((W_C, W_C), const2)],
        out_specs=[pl.BlockSpec((None, TILE, w), tok) for w in widths],
        compiler_params=_cparams(2),
        name="in_projection",
    )(x_lat, x_ctx, modsel, gain, w_ext, tab, gq, gk, ones)


def _flash_kernel(q_ref, k_ref, v_ref, aux_ref, o_ref, va_ref, vb_ref, qs_ref, acc_ref, m_ref,
                  s0_ref, s1_ref, mb0_ref, mb1_ref, *,
                  n_qblk, n_sub, tk, n_lat_blocks, ctx_tile, qt_off, mode, use_exp2, lam_init):
    exp_fn = jnp.exp2 if use_exp2 else jnp.exp
    qt = pl.program_id(2) + qt_off
    sub_w = LANES // n_sub
    half = LANES // 2
    lane = lax.broadcasted_iota(jnp.int32, (1, LANES), 1)
    lower = lane < half
    n_pieces = n_qblk * n_sub
    ma = (n_pieces // 2) * TILE
    m_rows = n_pieces * TILE
    ctx_start = n_lat_blocks * tk

    @pl.when(pl.program_id(2) == 0)
    def _():
        v = v_ref[...].astype(F32)
        va_ref[...] = jnp.where(lower, v, 1.0).astype(BF16)
        vb_ref[...] = jnp.where(lower, 1.0, v).astype(BF16)

    ia, ib = 0, n_pieces // 2
    for blk in range(n_qblk):
        qf = q_ref[:, blk * LANES:(blk + 1) * LANES].astype(F32)
        for sub in range(n_sub):
            msk = (lane >= sub * sub_w) & (lane < (sub + 1) * sub_w)
            piece = jnp.where(msk, qf, 0.0).astype(BF16)
            if sub * sub_w < half:
                qs_ref[ia * TILE:(ia + 1) * TILE, :] = piece
                ia += 1
            else:
                qs_ref[ib * TILE:(ib + 1) * TILE, :] = piece
                ib += 1

    s_bufs = (s0_ref, s1_ref)
    mb_bufs = (mb0_ref, mb1_ref)

    def scores(start, size, slot):
        s = _dot_nt(qs_ref[...], k_ref[pl.ds(start, size), :])
        s_bufs[slot][:, :size] = s
        mb = jnp.max(s, axis=-1, keepdims=True)
        mb_bufs[slot][...] = jnp.broadcast_to(mb, (m_rows, LANES))

    def accumulate(start, size, slot, first):
        mb = mb_bufs[slot][...]
        if first:
            m_new = mb
        else:
            m_old = m_ref[...]
            m_new = jnp.maximum(m_old, mb)
        s_ref = s_bufs[slot]
        cols = [s_ref[:, c * LANES:(c + 1) * LANES] - m_new for c in range(size // LANES)]
        if use_exp2:
            p = jnp.concatenate([jnp.exp2(d.astype(BF16)) for d in cols], axis=1)
        else:
            p = jnp.concatenate([jnp.exp(d).astype(BF16) for d in cols], axis=1)
        pva = _dot(p[:ma], va_ref[pl.ds(start, size), :])
        pvb = _dot(p[ma:], vb_ref[pl.ds(start, size), :])
        if first:
            acc_ref[:ma, :] = pva
            acc_ref[ma:, :] = pvb
        else:
            alpha = exp_fn(m_old - m_new)
            acc_ref[:ma, :] = alpha[:ma] * acc_ref[:ma, :] + pva
            acc_ref[ma:, :] = alpha[ma:] * acc_ref[ma:, :] + pvb
        m_ref[...] = m_new

    def lat(j):
        return pl.multiple_of(j * tk, tk)

    def latent_queries():
        scores(ctx_start, CTX_LEN, 0)
        scores(lat(0), tk, 1)
        accumulate(ctx_start, CTX_LEN, 0, True)

        def pair(i):
            scores(lat(2 * i + 1), tk, 0)
            accumulate(lat(2 * i), tk, 1, False)
            scores(lat(2 * i + 2), tk, 1)
            accumulate(lat(2 * i + 1), tk, 0, False)

        def body(i, carry):
            for u in range(PAIRS_PER_STEP):
                pair(i * PAIRS_PER_STEP + u)
            return carry

        n_pairs = (n_lat_blocks - 2) // 2
        n_steps = n_pairs // PAIRS_PER_STEP
        lax.fori_loop(0, n_steps, body, 0)
        for i in range(n_steps * PAIRS_PER_STEP, n_pairs):
            pair(i)
        scores(lat(n_lat_blocks - 1), tk, 0)
        accumulate(lat(n_lat_blocks - 2), tk, 1, False)
        accumulate(lat(n_lat_blocks - 1), tk, 0, False)

    def context_queries():
        scores(ctx_start, CTX_LEN, 0)
        accumulate(ctx_start, CTX_LEN, 0, True)

    if ctx_tile is None:
        latent_queries()
    else:
        pl.when(qt != ctx_tile)(latent_queries)
        pl.when(qt == ctx_tile)(context_queries)

    acc = acc_ref[...]
    r = acc / pltpu.roll(acc, half, 1)
    ra, rb = r[:ma], r[ma:]
    if mode == "plain":
        for i in range(n_pieces // 2):
            o = jnp.where(lower, ra[i * TILE:(i + 1) * TILE], rb[i * TILE:(i + 1) * TILE])
            o_ref[:, i * LANES:(i + 1) * LANES] = o.astype(BF16)
    else:
        aux = aux_ref[...]
        l1 = jnp.sum(aux[0:1] * aux[1:2], axis=-1, keepdims=True)
        l2 = jnp.sum(aux[2:3] * aux[3:4], axis=-1, keepdims=True)
        lam = jnp.exp(l1) - jnp.exp(l2) + lam_init
        oa = ra[:TILE] - lam * ra[TILE:]
        ob = rb[:TILE] - lam * rb[TILE:]
        o = jnp.where(lower, oa, ob)
        sq = o * o
        ss_a = jnp.sum(jnp.where(lower, sq, 0.0), axis=-1, keepdims=True)
        ss_b = jnp.sum(jnp.where(lower, 0.0, sq), axis=-1, keepdims=True)
        ms = jnp.where(lower, ss_a, ss_b) * (1.0 / HEAD_DIM)
        o = (o * lax.rsqrt(ms + EPS)) * aux[4:5]
        o_ref[...] = (o * (1.0 - lam_init)).astype(BF16)


def _flash(q, k, v, aux, *, n_qblk, n_sub, n_hp, qt_off, n_qt, n_lat, ctx_tile, mode, use_exp2,
           lam_init=0.0):
    b, t_all, _ = q.shape
    qw = n_qblk * LANES
    tk = FLASH_TK if n_lat % (2 * FLASH_TK) == 0 else 512
    assert n_lat % (2 * tk) == 0 and tk >= CTX_LEN
    m_rows = n_qblk * n_sub * TILE
    kern = functools.partial(_flash_kernel, n_qblk=n_qblk, n_sub=n_sub, tk=tk,
                             n_lat_blocks=n_lat // tk, ctx_tile=ctx_tile, qt_off=qt_off,
                             mode=mode, use_exp2=use_exp2, lam_init=lam_init)
    return pl.pallas_call(
        kern,
        out_shape=jax.ShapeDtypeStruct((b, n_qt * TILE, n_hp * qw), BF16),
        grid=(b, n_hp, n_qt),
        in_specs=[pl.BlockSpec((None, TILE, qw), lambda bi, hp, qt: (bi, qt + qt_off, hp)),
                  pl.BlockSpec((None, t_all, LANES), lambda bi, hp, qt: (bi, 0, hp)),
                  pl.BlockSpec((None, t_all, LANES), lambda bi, hp, qt: (bi, 0, hp)),
                  pl.BlockSpec((8, LANES), lambda bi, hp, qt: (0, 0))],
        out_specs=pl.BlockSpec((None, TILE, qw), lambda bi, hp, qt: (bi, qt, hp)),
        scratch_shapes=[pltpu.VMEM((t_all, LANES), BF16),
                        pltpu.VMEM((t_all, LANES), BF16),
                        pltpu.VMEM((m_rows, LANES), BF16),
                        pltpu.VMEM((m_rows, LANES), F32),
                        pltpu.VMEM((m_rows, LANES), F32),
                        pltpu.VMEM((m_rows, tk), F32),
                        pltpu.VMEM((m_rows, tk), F32),
                        pltpu.VMEM((m_rows, LANES), F32),
                        pltpu.VMEM((m_rows, LANES), F32)],
        compiler_params=_cparams(3),
        name="flash_" + mode,
    )(q, k, v, aux)


def _na_kernel(q_ref, k0, k1, k2, k3, v0, v1, v2, v3, kc_ref, vc_ref, bias_ref, o_ref):
    lane = lax.broadcasted_iota(jnp.int32, (1, LANES), 1)
    lower = lane < LANES // 2
    qf = q_ref[...].astype(F32)
    kw = jnp.concatenate([k0[...], k1[...], k2[...], k3[...]], axis=0)
    vw = jnp.concatenate([v0[...], v1[...], v2[...], v3[...]], axis=0)
    kc = kc_ref[...]
    vwf = vw.astype(F32)
    vcf = vc_ref[...].astype(F32)
    outs = []
    for hh in range(2):
        msk = lower if hh == 0 else jnp.logical_not(lower)
        qh = jnp.where(msk, qf, 0.0).astype(BF16)
        vw_h = jnp.where(msk, vwf, 1.0).astype(BF16)
        vc_h = jnp.where(msk, vcf, 1.0).astype(BF16)
        n_pair = NA_KROWS // 2
        bias = jnp.concatenate(
            [jnp.concatenate([bias_ref[hh, a * n_pair + j] for j in range(n_pair)], axis=1)
             for a in range(NA_QROWS)], axis=0)
        s_w = _dot_nt(qh, kw) + bias
        s_c = _dot_nt(qh, kc)
        m = jnp.maximum(jnp.max(s_w, axis=-1, keepdims=True), jnp.max(s_c, axis=-1, keepdims=True))
        p_w = jnp.exp2((s_w - m).astype(BF16))
        p_c = jnp.exp2((s_c - m).astype(BF16))
        o = _dot(p_w, vw_h) + _dot(p_c, vc_h)
        outs.append(o / pltpu.roll(o, LANES // 2, 1))
    o_ref[...] = jnp.where(lower, outs[0], outs[1]).astype(BF16)


def _neighbourhood_attention(qa, ka, va, bias, n_lat):
    b = qa.shape[0]
    q_tok = NA_QROWS * GRID_W
    v_tok = q_tok // 2
    n_rb = n_lat // q_tok
    n_view = n_lat // v_tok
    ctx_blk = n_lat // v_tok

    def view(j):
        return lambda rb, hp, bi: (bi, jnp.clip(2 * rb - 1 + j, 0, n_view - 1), hp)

    kv_specs = [pl.BlockSpec((None, v_tok, LANES), view(j)) for j in range(4)]
    ctx_spec = pl.BlockSpec((None, CTX_LEN, LANES), lambda rb, hp, bi: (bi, ctx_blk, hp))

    def bias_map(rb, hp, bi):
        pat = jnp.where(rb == 0, 0, jnp.where(rb == n_rb - 1, 2, 1))
        return (hp, pat, 0, 0, 0)

    return pl.pallas_call(
        _na_kernel,
        out_shape=jax.ShapeDtypeStruct((b, n_lat, W_A), BF16),
        grid=(n_rb, NA_HEADS // 2, b),
        in_specs=[pl.BlockSpec((None, q_tok, LANES), lambda rb, hp, bi: (bi, rb, hp))]
                 + kv_specs + kv_specs + [ctx_spec, ctx_spec,
                 pl.BlockSpec((2, None, NA_QROWS * NA_KROWS // 2, GRID_W, 2 * GRID_W), bias_map)],
        out_specs=pl.BlockSpec((None, q_tok, LANES), lambda rb, hp, bi: (bi, rb, hp)),
        compiler_params=_cparams(3),
        name="neighbourhood_attention",
    )(qa, ka, ka, ka, ka, va, va, va, va, ka, va, bias)


def _na_bias_table(rpb, rows):
    cols = np.arange(GRID_W)
    c0 = np.clip(cols - NA_WIN_W // 2, 0, GRID_W - NA_WIN_W)
    cc = cols[None, :]
    col_ok = (cc >= c0[:, None]) & (cc < c0[:, None] + NA_WIN_W)
    dc = np.clip(cc - cols[:, None] + (NA_WIN_W - 1), 0, 2 * NA_WIN_W - 2)
    e = jnp.where(col_ok[None, None], (rpb.astype(F32) * LOG2E)[:, :, dc], NEG)
    e = jnp.concatenate([e, jnp.full_like(e[:, :1], NEG)], axis=1)
    a = np.arange(NA_QROWS)[:, None]
    i = np.arange(NA_KROWS)[None, :]
    pats = []
    for r_base in (0, NA_QROWS, rows - NA_QROWS):
        r = r_base + a
        key_row = r_base - NA_WIN_H // 2 + i
        r0 = np.clip(r - NA_WIN_H // 2, 0, rows - NA_WIN_H)
        ok = (key_row >= r0) & (key_row < r0 + NA_WIN_H) & (key_row >= 0) & (key_row < rows)
        dr = np.where(ok, key_row - r + (NA_WIN_H - 1), 2 * NA_WIN_H - 1)
        pats.append(dr)
    dr_all = np.stack(pats)
    pairs = dr_all.reshape(-1, 2)
    uniq, inv = np.unique(pairs, axis=0, return_inverse=True)
    pair_blocks = jnp.concatenate([e[:, uniq[:, 0]], e[:, uniq[:, 1]]], axis=-1)
    t = pair_blocks[:, inv.reshape(-1)]
    return t.reshape(NA_HEADS, 3, NA_QROWS * NA_KROWS // 2, GRID_W, 2 * GRID_W)


def _outproj_kernel(x_ref, xc_ref, oa_ref, ob_ref, oc_ref, wa_ref, wb_ref, wc_ref, mod_ref, gain_ref,
                    wrh_ref, wrl_ref, br_ref, tri_ref, x1_ref, tok_ref, route_ref, cnt_ref, run_ref,
                    *, region, n_lat_tiles):
    mod = mod_ref[...]
    y = _dot(oa_ref[...], wa_ref[...]) + _dot(ob_ref[...], wb_ref[...]) + _dot(oc_ref[...], wc_ref[...])
    x1 = jnp.where(pl.program_id(1) == n_lat_tiles, xc_ref[...], x_ref[...]) + mod[2:3] * y
    x1_ref[...] = x1
    ms = jnp.mean(x1 * x1, axis=-1, keepdims=True)
    t = (x1 * lax.rsqrt(ms + EPS)) * gain_ref[...]
    t = t * (1.0 + mod[4:5]) + mod[3:4]
    tok_ref[...] = t

    t_hi, t_lo = _split_bf16(t)
    wrh = wrh_ref[...]
    logits = _dot(t_hi, wrh) + _dot(t_lo, wrh) + _dot(t_hi, wrl_ref[...]) + br_ref[...]

    lane = lax.broadcasted_iota(jnp.int32, logits.shape, 1)
    lane_f = lane.astype(F32)
    is_g = lane < N_GROUPS
    gl = jnp.where(is_g, logits, NEG)
    gmax = jnp.max(gl, axis=-1, keepdims=True)
    g_sel = jnp.min(jnp.where(gl == gmax, lane_f, 1e9), axis=-1, keepdims=True)
    p_grp = 1.0 / jnp.sum(jnp.where(is_g, jnp.exp(gl - gmax), 0.0), axis=-1, keepdims=True)
    grp_of_lane = lax.shift_right_arithmetic(lane - N_GROUPS, 2).astype(F32)
    in_grp = (lane >= N_GROUPS) & (lane < N_GROUPS + N_EXPERTS) & (grp_of_lane == g_sel)
    el = jnp.where(in_grp, logits, NEG)
    v1 = jnp.max(el, axis=-1, keepdims=True)
    i1 = jnp.min(jnp.where(el == v1, lane_f, 1e9), axis=-1, keepdims=True)
    el2 = jnp.where(lane_f == i1, NEG, el)
    v2 = jnp.max(el2, axis=-1, keepdims=True)
    i2 = jnp.min(jnp.where(el2 == v2, lane_f, 1e9), axis=-1, keepdims=True)
    e2 = jnp.exp(v2 - v1)
    den = 1.0 + e2
    w1 = p_grp / den
    w2 = p_grp * e2 / den

    @pl.when((pl.program_id(0) == 0) & (pl.program_id(1) == 0))
    def _():
        run_ref[...] = jnp.zeros(run_ref.shape, F32)

    ind = jnp.where(lane_f == i1, 1.0, 0.0) + jnp.where(lane_f == i2, 1.0, 0.0)
    rank = _dot(tri_ref[...], ind.astype(BF16)) + run_ref[0:1, :]

    def pick(m, l):
        return jnp.sum(jnp.where(lane_f == l, m, 0.0), axis=-1, keepdims=True)

    pos1 = (i1 - N_GROUPS) * region + pick(rank, i1)
    pos2 = (i2 - N_GROUPS) * region + pick(rank, i2)
    route_ref[...] = jnp.where(lane == 0, pos1, jnp.where(lane == 1, pos2,
                               jnp.where(lane == 2, w1, jnp.where(lane == 3, w2, 0.0))))
    run = run_ref[...] + jnp.sum(ind, axis=0, keepdims=True)
    run_ref[...] = run
    cnt_ref[...] = run


def _out_projection(x_lat, x_ctx, ctx_blk, oa, ob, oc, wa, wb, wc, modsel, gain, wrh, wrl, br, n_tiles,
                    n_lat_tiles):
    b, _, d = x_lat.shape
    tok = lambda bi, ti: (bi, ti, 0)
    const2 = lambda bi, ti: (0, 0)
    rows = n_tiles * TILE
    tri = jnp.asarray(np.tril(np.ones((TILE, TILE), np.float32), -1), BF16)
    return pl.pallas_call(
        functools.partial(_outproj_kernel, region=b * rows, n_lat_tiles=n_lat_tiles),
        out_shape=[jax.ShapeDtypeStruct((b, rows, d), F32),
                   jax.ShapeDtypeStruct((b, rows, d), F32),
                   jax.ShapeDtypeStruct((b, rows, LANES), F32),
                   jax.ShapeDtypeStruct((8, LANES), F32)],
        grid=(b, n_tiles),
        in_specs=_token_specs(d, n_lat_tiles, ctx_blk) + [
                  pl.BlockSpec((None, TILE, W_A), tok),
                  pl.BlockSpec((None, TILE, W_B), tok),
                  pl.BlockSpec((None, TILE, W_C), tok),
                  pl.BlockSpec((W_A, d), const2),
                  pl.BlockSpec((W_B, d), const2),
                  pl.BlockSpec((W_C, d), const2),
                  pl.BlockSpec((None, 6, d), lambda bi, ti: (2 * bi + (ti >= n_lat_tiles).astype(jnp.int32), 0, 0)),
                  pl.BlockSpec((1, d), const2),
                  pl.BlockSpec((d, LANES), const2),
                  pl.BlockSpec((d, LANES), const2),
                  pl.BlockSpec((1, LANES), const2),
                  pl.BlockSpec((TILE, TILE), const2)],
        out_specs=[pl.BlockSpec((None, TILE, d), tok),
                   pl.BlockSpec((None, TILE, d), tok),
                   pl.BlockSpec((None, TILE, LANES), tok),
                   pl.BlockSpec((8, LANES), const2)],
        scratch_shapes=[pltpu.VMEM((8, LANES), F32)],
        compiler_params=_cparams(2),
        name="out_projection",
    )(x_lat, x_ctx, oa, ob, oc, wa, wb, wc, modsel, gain, wrh, wrl, br, tri)


def _sc_mesh():
    return plsc.VectorSubcoreMesh(core_axis_name="core", subcore_axis_name="subcore")


def _sc_worker_base(per_worker):
    wid = lax.axis_index("subcore") * SC_CORES + lax.axis_index("core")
    return wid * per_worker


def _sc_scratch(d, dtype):
    return ([pltpu.VMEM((SC_ROWS,), jnp.int32)] * SC_BUFS + [pltpu.VMEM((SC_ROWS, d), dtype)] * SC_BUFS
            + [pltpu.SemaphoreType.DMA] * (2 * SC_BUFS))


def _sc_split(scratch):
    return (scratch[:SC_BUFS], scratch[SC_BUFS:2 * SC_BUFS], scratch[2 * SC_BUFS:3 * SC_BUFS],
            scratch[3 * SC_BUFS:])


def _sc_scatter_rows(x, idx, n_out):
    n, d = x.shape
    per_worker = 2 * n // (SC_CORES * SC_SUBCORES)
    assert per_worker % (SC_ROWS * SC_BUFS) == 0 and n % SC_ROWS == 0

    @functools.partial(pl.kernel, out_type=jax.ShapeDtypeStruct((n_out, d), x.dtype),
                       mesh=_sc_mesh(), scratch_types=_sc_scratch(d, x.dtype))
    def scatter(x_hbm, i_hbm, o_hbm, *scratch):
        idx_v, rows_v, sem_in, sem_out = _sc_split(scratch)
        base = _sc_worker_base(per_worker)

        @pl.loop(0, per_worker // SC_ROWS, step=SC_BUFS)
        def _(c):
            reads = []
            for u in range(SC_BUFS):
                a = pl.multiple_of(base + (c + u) * SC_ROWS, SC_ROWS)
                t = pl.multiple_of(lax.rem(a, n), SC_ROWS)
                pltpu.sync_copy(i_hbm.at[pl.ds(a, SC_ROWS)], idx_v[u])
                reads.append(pltpu.async_copy(x_hbm.at[pl.ds(t, SC_ROWS)], rows_v[u], sem_in[u]))
            writes = []
            for u in range(SC_BUFS):
                reads[u].wait()
                writes.append(pltpu.async_copy(rows_v[u], o_hbm.at[idx_v[u]], sem_out[u]))
            for w in writes:
                w.wait()

    return scatter(x, idx)


def _sc_gather_rows(src, idx):
    m = idx.shape[0]
    d = src.shape[1]
    per_worker = m // (SC_CORES * SC_SUBCORES)
    assert per_worker % (SC_ROWS * SC_BUFS) == 0

    @functools.partial(pl.kernel, out_type=jax.ShapeDtypeStruct((m, d), src.dtype),
                       mesh=_sc_mesh(), scratch_types=_sc_scratch(d, src.dtype))
    def gather(s_hbm, i_hbm, o_hbm, *scratch):
        idx_v, rows_v, sem_in, sem_out = _sc_split(scratch)
        base = _sc_worker_base(per_worker)

        @pl.loop(0, per_worker // SC_ROWS, step=SC_BUFS)
        def _(c):
            offs, reads = [], []
            for u in range(SC_BUFS):
                a = pl.multiple_of(base + (c + u) * SC_ROWS, SC_ROWS)
                offs.append(a)
                pltpu.sync_copy(i_hbm.at[pl.ds(a, SC_ROWS)], idx_v[u])
                reads.append(pltpu.async_copy(s_hbm.at[idx_v[u]], rows_v[u], sem_in[u]))
            writes = []
            for u in range(SC_BUFS):
                reads[u].wait()
                writes.append(pltpu.async_copy(rows_v[u], o_hbm.at[pl.ds(offs[u], SC_ROWS)], sem_out[u]))
            for w in writes:
                w.wait()

    return gather(src, idx)


def _expert_ffn_kernel(blk_ref, exp_ref, x_ref, wg_ref, wu_ref, wd_ref, y_ref, wgb_ref, wub_ref, wdb_ref):
    j = pl.program_id(0)

    @pl.when((j == 0) | (exp_ref[j] != exp_ref[jnp.maximum(j - 1, 0)]))
    def _():
        wgb_ref[...] = wg_ref[...].astype(BF16)
        wub_ref[...] = wu_ref[...].astype(BF16)
        wdb_ref[...] = wd_ref[...].astype(BF16)

    x = x_ref[...].astype(BF16)
    hid = jax.nn.silu(_dot(x, wgb_ref[...])) * _dot(x, wub_ref[...])
    y_ref[...] = _dot(hid.astype(BF16), wdb_ref[...])


def _expert_ffn(xs, blk, exp, wg, wu, wd, layer):
    rows, d = xs.shape
    w_map = lambda j, blk, exp: (layer, exp[j], 0, 0)
    return pl.pallas_call(
        _expert_ffn_kernel,
        out_shape=jax.ShapeDtypeStruct((rows, d), F32),
        grid_spec=pltpu.PrefetchScalarGridSpec(
            num_scalar_prefetch=2,
            grid=(blk.shape[0],),
            in_specs=[pl.BlockSpec((MOE_TILE, d), lambda j, blk, exp: (blk[j], 0)),
                      pl.BlockSpec((None, None, d, EXPERT_HIDDEN), w_map),
                      pl.BlockSpec((None, None, d, EXPERT_HIDDEN), w_map),
                      pl.BlockSpec((None, None, EXPERT_HIDDEN, d), w_map)],
            out_specs=pl.BlockSpec((MOE_TILE, d), lambda j, blk, exp: (blk[j], 0)),
            scratch_shapes=[pltpu.VMEM((d, EXPERT_HIDDEN), BF16),
                            pltpu.VMEM((d, EXPERT_HIDDEN), BF16),
                            pltpu.VMEM((EXPERT_HIDDEN, d), BF16)]),
        compiler_params=_cparams(1),
        name="expert_ffn",
    )(blk, exp, xs, wg, wu, wd)


def _combine_kernel(x1_ref, y1_ref, y2_ref, route_ref, mod_ref, fgain_ref, o_ref, *, final):
    route = route_ref[...]
    y = route[:, 2:3] * y1_ref[...] + route[:, 3:4] * y2_ref[...]
    x2 = x1_ref[...] + mod_ref[5:6, :] * y
    if final:
        ms = jnp.mean(x2 * x2, axis=-1, keepdims=True)
        x2 = (x2 * lax.rsqrt(ms + EPS)) * fgain_ref[...]
    o_ref[...] = x2


def _combine(x1, ys, route, modsel, fgain, n_lat_tiles, final):
    b, rows, d = x1.shape
    n_t = rows // TILE
    tok = lambda bi, ti: (bi, ti, 0)
    return pl.pallas_call(
        functools.partial(_combine_kernel, final=final),
        out_shape=jax.ShapeDtypeStruct((b, rows, d), F32),
        grid=(b, n_t),
        in_specs=[pl.BlockSpec((None, TILE, d), tok),
                  pl.BlockSpec((TILE, d), lambda bi, ti: (bi * n_t + ti, 0)),
                  pl.BlockSpec((TILE, d), lambda bi, ti: (b * n_t + bi * n_t + ti, 0)),
                  pl.BlockSpec((None, TILE, LANES), tok),
                  pl.BlockSpec((None, 6, d), lambda bi, ti: (2 * bi + (ti >= n_lat_tiles).astype(jnp.int32), 0, 0)),
                  pl.BlockSpec((1, d), lambda bi, ti: (0, 0))],
        out_specs=pl.BlockSpec((None, TILE, d), tok),
        compiler_params=_cparams(2),
        name="moe_combine",
    )(x1, ys, ys, route, modsel, fgain)


def _routed_moe(tok, route, cnt, x1, wg, wu, wd, layer, modsel, fgain, n_lat_tiles, final):
    b, rows, d = x1.shape
    n = b * rows
    flat = route.reshape(n, LANES)
    idx = jnp.concatenate([flat[:, 0], flat[:, 1]]).astype(jnp.int32)
    xs = _sc_scatter_rows(tok.reshape(n, d), idx, N_EXPERTS * n)

    counts = cnt[0, N_GROUPS:N_GROUPS + N_EXPERTS].astype(jnp.int32)
    tiles = (counts + MOE_TILE - 1) // MOE_TILE
    ends = jnp.cumsum(tiles)
    n_sched = 2 * n // MOE_TILE + N_EXPERTS
    j = jnp.minimum(jnp.arange(n_sched, dtype=jnp.int32), ends[-1] - 1)
    exp = jnp.sum((j[:, None] >= ends[None, :]).astype(jnp.int32), axis=1)
    blk = exp * (n // MOE_TILE) + j - (ends - tiles)[exp]

    ys = _expert_ffn(xs, blk, exp, wg, wu, wd, layer)
    yg = _sc_gather_rows(ys, idx)
    return _combine(x1, yg, route, modsel, fgain, n_lat_tiles, final)


def _rope_tables(n_lat):
    t = jnp.arange(n_lat)
    row = (t // GRID_W).astype(F32)
    col = (t % GRID_W).astype(F32)

    def cs(dim):
        quarter = dim // 4
        freqs = ROPE_THETA ** (-jnp.arange(quarter, dtype=F32) / quarter)
        ang = jnp.concatenate([row[:, None] * freqs, col[:, None] * freqs], axis=-1)
        cos = jnp.tile(jnp.cos(ang), (1, 2 * LANES // dim))
        sin = jnp.tile(jnp.sin(ang), (1, 2 * LANES // dim))
        cos = jnp.concatenate([cos, jnp.ones((CTX_LEN, LANES), F32)], axis=0)
        sin = jnp.concatenate([sin, jnp.zeros((CTX_LEN, LANES), F32)], axis=0)
        return cos, sin

    cos_b, sin_b = cs(DIFF_QK_DIM)
    cos_c, sin_c = cs(HEAD_DIM)
    return jnp.concatenate([cos_b, sin_b, cos_c, sin_c], axis=1)


def _reordered_w_in(w_in):
    o_c = 3 * W_A + 3 * W_B
    heads = [w_in[:, o_c + h * HEAD_DIM:o_c + (h + 1) * HEAD_DIM] for h in GQA_Q_ORDER]
    return jnp.concatenate([w_in[:, :o_c]] + heads + [w_in[:, o_c + W_C:]], axis=1).astype(BF16)


def kernel(x, c, ctx, c_ctx, w_mod, b_mod, norm_attn, norm_ffn, w_in, w_out, na_rpb, diff_lambda_q1, diff_lambda_k1, diff_lambda_q2, diff_lambda_k2, diff_subln, gqa_q_norm, gqa_k_norm, router_group_w, router_group_b, router_expert_w, router_expert_b, w_gate, w_up, w_down, final_norm):
    b, s, d = x.shape
    assert d == D_MODEL and ctx.shape[1] == CTX_LEN and s % (NA_QROWS * GRID_W) == 0
    rows = s // GRID_W
    assert rows >= 2 * NA_QROWS
    t_all = s + CTX_LEN
    n_lat_tiles = s // TILE

    c_rows = jnp.zeros((8, d), F32).at[:b].set(c).at[b].set(c_ctx)
    mod = _modulation(c_rows, w_mod, b_mod)

    tab = _rope_tables(s)
    hidx = np.arange(HEAD_DIM)
    partner = np.where(hidx < HEAD_DIM // 2, hidx + HEAD_DIM // 2, hidx - HEAD_DIM // 2)
    blk = np.arange(W_C) // HEAD_DIM
    ones = jnp.asarray((blk[:, None] == blk[None, :]).astype(np.float32), BF16)
    oc_rows = W_A + W_B + np.concatenate([h * HEAD_DIM + np.arange(HEAD_DIM) for h in GQA_Q_ORDER])
    dummy_aux = jnp.zeros((8, LANES), F32)

    x_lat, x_ctx, ctx_blk = x, ctx, 0
    for l in range(DEPTH):
        ctx_out = l < DEPTH - 1
        lam_init = 0.8 - 0.6 * math.exp(-0.3 * l)
        m_lat = mod[l, :b].reshape(b, 1, 6, d)
        m_ctx = jnp.broadcast_to(mod[l, b].reshape(1, 1, 6, d), (b, 1, 6, d))
        modsel = jnp.concatenate([m_lat, m_ctx], axis=1).reshape(2 * b, 6, d)

        gq = jnp.stack([jnp.tile(gqa_q_norm[l], GQA_Q_HEADS), jnp.tile(gqa_q_norm[l][partner], GQA_Q_HEADS)])
        gk = jnp.stack([jnp.tile(gqa_k_norm[l], GQA_KV_HEADS), jnp.tile(gqa_k_norm[l][partner], GQA_KV_HEADS)])
        qa, ka, va, qb, kb, vb, qc, kc, vc = _in_projection(
            x_lat, x_ctx, ctx_blk, modsel, norm_attn[l][None], _reordered_w_in(w_in[l]), tab, gq, gk, ones,
            n_lat_tiles)

        n_qt = n_lat_tiles + 1 if ctx_out else n_lat_tiles
        ctx_tile = n_lat_tiles if ctx_out else None
        oa = _neighbourhood_attention(qa, ka, va, _na_bias_table(na_rpb[l], rows), s)
        if ctx_out:
            oa_ctx = _flash(qa, ka, va, dummy_aux, n_qblk=1, n_sub=2, n_hp=NA_HEADS // 2,
                            qt_off=n_lat_tiles, n_qt=1, n_lat=s, ctx_tile=n_lat_tiles,
                            mode="plain", use_exp2=True)
            oa = jnp.concatenate([oa, oa_ctx], axis=1)
        pad = lambda v: jnp.pad(v, (0, LANES - v.shape[0]))
        aux = jnp.stack([pad(diff_lambda_q1[l]), pad(diff_lambda_k1[l]), pad(diff_lambda_q2[l]),
                         pad(diff_lambda_k2[l]), jnp.tile(diff_subln[l], 2),
                         jnp.zeros((LANES,), F32), jnp.zeros((LANES,), F32), jnp.zeros((LANES,), F32)])
        ob = _flash(qb, kb, vb, aux, n_qblk=1, n_sub=4, n_hp=DIFF_HEADS // 2, qt_off=0, n_qt=n_qt,
                    n_lat=s, ctx_tile=ctx_tile, mode="diff", use_exp2=True, lam_init=lam_init)
        oc = _flash(qc, kc, vc, dummy_aux, n_qblk=3, n_sub=2, n_hp=1, qt_off=0, n_qt=n_qt,
                    n_lat=s, ctx_tile=ctx_tile, mode="plain", use_exp2=True)

        w_o = w_out[l]
        wr = jnp.zeros((d, LANES), F32)
        wr = wr.at[:, :N_GROUPS].set(router_group_w[l]).at[:, N_GROUPS:N_GROUPS + N_EXPERTS].set(router_expert_w[l])
        wrh, wrl = _split_bf16(wr)
        br = jnp.zeros((1, LANES), F32)
        br = br.at[0, :N_GROUPS].set(router_group_b[l]).at[0, N_GROUPS:N_GROUPS + N_EXPERTS].set(router_expert_b[l])
        x1, tok, route, cnt = _out_projection(
            x_lat, x_ctx, ctx_blk, oa, ob, oc, w_o[:W_A].astype(BF16), w_o[W_A:W_A + W_B].astype(BF16),
            w_o[oc_rows].astype(BF16), modsel, norm_ffn[l][None], wrh, wrl, br, n_qt, n_lat_tiles)
        xs = _routed_moe(tok, route, cnt, x1, w_gate, w_up, w_down, l, modsel, final_norm[None],
                         n_lat_tiles, final=not ctx_out)
        x_lat, x_ctx, ctx_blk = xs, xs, n_lat_tiles
    return xs
```

```python
import functools
import math

import numpy as np
import jax
import jax.numpy as jnp
from jax import lax
from jax.experimental import pallas as pl
from jax.experimental.pallas import tpu as pltpu
from jax.experimental.pallas import tpu_sc as plsc

F32 = jnp.float32
BF16 = jnp.bfloat16

D_MODEL = 1024
DEPTH = 2
GRID_W = 64
CTX_LEN = 256
HEAD_DIM = 64
NA_HEADS = 6
NA_WIN_H = 8
NA_WIN_W = 16
DIFF_HEADS = 4
DIFF_QK_DIM = 32
GQA_Q_HEADS = 6
GQA_KV_HEADS = 2
N_GROUPS = 4
EXPERTS_PER_GROUP = 4
N_EXPERTS = 16
EXPERT_HIDDEN = 512
ROPE_THETA = 10000.0
EPS = 1e-6
W_A = NA_HEADS * HEAD_DIM
W_B = DIFF_HEADS * 2 * DIFF_QK_DIM
W_C = GQA_Q_HEADS * HEAD_DIM
W_KC = GQA_KV_HEADS * HEAD_DIM
IN_WIDTH = 3 * W_A + 3 * W_B + W_C + 2 * W_KC

LANES = 128
TILE = CTX_LEN
NA_QROWS = 8
NA_KROWS = 16
NEG = -1e30
LOG2E = 1.4426950408889634
VMEM_LIMIT = 56 * 1024 * 1024
FLASH_TK = 512
PAIRS_PER_STEP = 2
MOE_TILE = 512
MOE_GROUPS = 2
SC_ROWS = 16
SC_BUFS = 4
SC_CORES = 2
SC_SUBCORES = 16

GQA_Q_ORDER = (0, 3, 1, 4, 2, 5)


def _cparams(n_axes):
    return pltpu.CompilerParams(dimension_semantics=("arbitrary",) * n_axes,
                                vmem_limit_bytes=VMEM_LIMIT)


def _split_bf16(a):
    hi = a.astype(BF16)
    lo = (a - hi.astype(F32)).astype(BF16)
    return hi, lo


def _dot(a, b):
    return jnp.dot(a, b, preferred_element_type=F32)


def _dot_nt(a, b):
    return lax.dot_general(a, b, (((1,), (1,)), ((), ())), preferred_element_type=F32)


def _mod_kernel(c_ref, w_ref, b_ref, o_ref):
    c = c_ref[...]
    a = c * jax.nn.sigmoid(c)
    a_hi, a_lo = _split_bf16(a)
    w_hi, w_lo = _split_bf16(w_ref[...])
    o_ref[...] = _dot(a_hi, w_hi) + _dot(a_lo, w_hi) + _dot(a_hi, w_lo) + b_ref[...]


def _modulation(c_rows, w_mod, b_mod):
    depth, d, n = w_mod.shape
    bn = 1536
    return pl.pallas_call(
        _mod_kernel,
        out_shape=jax.ShapeDtypeStruct((depth, 8, n), F32),
        grid=(depth, n // bn),
        in_specs=[pl.BlockSpec((8, d), lambda l, j: (0, 0)),
                  pl.BlockSpec((None, d, bn), lambda l, j: (l, 0, j)),
                  pl.BlockSpec((None, 1, bn), lambda l, j: (l, 0, j))],
        out_specs=pl.BlockSpec((None, 8, bn), lambda l, j: (l, 0, j)),
        compiler_params=_cparams(2),
        name="adaln_mod",
    )(c_rows, w_mod, b_mod.reshape(depth, 1, n))


def _head_mean_sq(t, ones):
    hi, lo = _split_bf16(t * t)
    return (_dot(hi, ones) + _dot(lo, ones)) * (1.0 / HEAD_DIM)


def _rotate_half(p, head):
    w = p.shape[1]
    half = head // 2
    lane = lax.broadcasted_iota(jnp.int32, (1, w), 1)
    first = (lane & (head - 1)) < half
    from_right = pltpu.roll(p, w - half, 1)
    from_left = pltpu.roll(p, half, 1)
    return jnp.where(first, -from_right, from_left)


def _inproj_kernel(x_ref, xc_ref, mod_ref, gain_ref, w_ref, tab_ref, gq_ref, gk_ref, ones_ref,
                   qa_ref, ka_ref, va_ref, qb_ref, kb_ref, vb_ref, qc_ref, kc_ref, vc_ref,
                   *, n_lat_tiles):
    x = jnp.where(pl.program_id(1) == n_lat_tiles, xc_ref[...], x_ref[...])
    mod = mod_ref[...]
    ms = jnp.mean(x * x, axis=-1, keepdims=True)
    h = (x * lax.rsqrt(ms + EPS)) * gain_ref[...]
    h = h * (1.0 + mod[1:2]) + mod[0:1]
    hb = h.astype(BF16)

    def proj(a, b):
        return _dot(hb, w_ref[:, a:b])

    pa = proj(0, 3 * W_A)
    qa_ref[...] = (pa[:, :W_A] * (HEAD_DIM ** -0.5 * LOG2E)).astype(BF16)
    ka_ref[...] = pa[:, W_A:2 * W_A].astype(BF16)
    va_ref[...] = pa[:, 2 * W_A:].astype(BF16)

    tab = tab_ref[...]
    cos_b = jnp.concatenate([tab[:, 0:LANES]] * 2, axis=1)
    sin_b = jnp.concatenate([tab[:, LANES:2 * LANES]] * 2, axis=1)
    cos_c1 = tab[:, 2 * LANES:3 * LANES]
    sin_c1 = tab[:, 3 * LANES:4 * LANES]
    cos_c = jnp.concatenate([cos_c1] * 3, axis=1)
    sin_c = jnp.concatenate([sin_c1] * 3, axis=1)

    o_b = 3 * W_A
    pb = proj(o_b, o_b + 3 * W_B)
    qb = pb[:, :W_B]
    kb = pb[:, W_B:2 * W_B]
    qb = qb * cos_b + _rotate_half(qb, DIFF_QK_DIM) * sin_b
    qb_ref[...] = (qb * (DIFF_QK_DIM ** -0.5 * LOG2E)).astype(BF16)
    kb_ref[...] = (kb * cos_b + _rotate_half(kb, DIFF_QK_DIM) * sin_b).astype(BF16)
    vb_ref[...] = pb[:, 2 * W_B:].astype(BF16)

    o_c = o_b + 3 * W_B
    pc = proj(o_c, IN_WIDTH)
    ones = ones_ref[...]
    qc = pc[:, :W_C]
    kc = pc[:, W_C:W_C + W_KC]
    nq = lax.rsqrt(_head_mean_sq(qc, ones) + EPS)
    nk = lax.rsqrt(_head_mean_sq(kc, ones[:W_KC, :W_KC]) + EPS)
    gq = gq_ref[...]
    gk = gk_ref[...]
    q = nq * (qc * gq[0:1] * cos_c + _rotate_half(qc, HEAD_DIM) * gq[1:2] * sin_c)
    qc_ref[...] = (q * (HEAD_DIM ** -0.5 * LOG2E)).astype(BF16)
    k = nk * (kc * gk[0:1] * cos_c1 + _rotate_half(kc, HEAD_DIM) * gk[1:2] * sin_c1)
    kc_ref[...] = k.astype(BF16)
    vc_ref[...] = pc[:, W_C + W_KC:].astype(BF16)


def _token_specs(d, n_lat_tiles, ctx_blk):
    return [pl.BlockSpec((None, TILE, d), lambda bi, ti: (bi, jnp.minimum(ti, n_lat_tiles - 1), 0)),
            pl.BlockSpec((None, TILE, d), lambda bi, ti: (bi, ctx_blk, 0))]


def _in_projection(x_lat, x_ctx, ctx_blk, modsel, gain, w_ext, tab, gq, gk, ones, n_lat_tiles):
    b, _, d = x_lat.shape
    n_tiles = n_lat_tiles + 1
    t_all = n_tiles * TILE
    widths = (W_A, W_A, W_A, W_B, W_B, W_B, W_C, W_KC, W_KC)
    tok = lambda bi, ti: (bi, ti, 0)
    const2 = lambda bi, ti: (0, 0)
    return pl.pallas_call(
        functools.partial(_inproj_kernel, n_lat_tiles=n_lat_tiles),
        out_shape=[jax.ShapeDtypeStruct((b, t_all, w), BF16) for w in widths],
        grid=(b, n_tiles),
        in_specs=_token_specs(d, n_lat_tiles, ctx_blk) + [
                  pl.BlockSpec((None, 6, d), lambda bi, ti: (2 * bi + (ti >= n_lat_tiles).astype(jnp.int32), 0, 0)),
                  pl.BlockSpec((1, d), const2),
                  pl.BlockSpec((d, IN_WIDTH), const2),
                  pl.BlockSpec((TILE, 4 * LANES), lambda bi, ti: (ti, 0)),
                  pl.BlockSpec((2, W_C), const2),
                  pl.BlockSpec((2, W_KC), const2),
                  pl.BlockSpec((W_C, W_C), const2)],
        out_specs=[pl.BlockSpec((None, TILE, w), tok) for w in widths],
        compiler_params=_cparams(2),
        name="in_projection",
    )(x_lat, x_ctx, modsel, gain, w_ext, tab, gq, gk, ones)


def _flash_kernel(q_ref, k_ref, v_ref, aux_ref, o_ref, va_ref, vb_ref, qs_ref, acc_ref, m_ref,
                  s0_ref, s1_ref, mb0_ref, mb1_ref, *,
                  n_qblk, n_sub, tk, n_lat_blocks, ctx_tile, qt_off, mode, lam_init):
    qt = pl.program_id(2) + qt_off
    sub_w = LANES // n_sub
    half = LANES // 2
    lane = lax.broadcasted_iota(jnp.int32, (1, LANES), 1)
    lower = lane < half
    n_pieces = n_qblk * n_sub
    ma = (n_pieces // 2) * TILE
    m_rows = n_pieces * TILE
    ctx_start = n_lat_blocks * tk

    @pl.when(pl.program_id(2) == 0)
    def _():
        v = v_ref[...].astype(F32)
        va_ref[...] = jnp.where(lower, v, 1.0).astype(BF16)
        vb_ref[...] = jnp.where(lower, 1.0, v).astype(BF16)

    ia, ib = 0, n_pieces // 2
    for blk in range(n_qblk):
        qf = q_ref[:, blk * LANES:(blk + 1) * LANES].astype(F32)
        for sub in range(n_sub):
            msk = (lane >= sub * sub_w) & (lane < (sub + 1) * sub_w)
            piece = jnp.where(msk, qf, 0.0).astype(BF16)
            if sub * sub_w < half:
                qs_ref[ia * TILE:(ia + 1) * TILE, :] = piece
                ia += 1
            else:
                qs_ref[ib * TILE:(ib + 1) * TILE, :] = piece
                ib += 1

    s_bufs = (s0_ref, s1_ref)
    mb_bufs = (mb0_ref, mb1_ref)

    def scores(start, size, slot):
        s = _dot_nt(qs_ref[...], k_ref[pl.ds(start, size), :])
        s_bufs[slot][:, :size] = s
        mb = jnp.max(s, axis=-1, keepdims=True)
        mb_bufs[slot][...] = jnp.broadcast_to(mb, (m_rows, LANES))

    def accumulate(start, size, slot, first):
        mb = mb_bufs[slot][...]
        if first:
            m_new = mb
        else:
            m_old = m_ref[...]
            m_new = jnp.maximum(m_old, mb)
        s_ref = s_bufs[slot]
        cols = [s_ref[:, c * LANES:(c + 1) * LANES] - m_new for c in range(size // LANES)]
        p = jnp.concatenate([jnp.exp2(d.astype(BF16)) for d in cols], axis=1)
        pva = _dot(p[:ma], va_ref[pl.ds(start, size), :])
        pvb = _dot(p[ma:], vb_ref[pl.ds(start, size), :])
        if first:
            acc_ref[:ma, :] = pva
            acc_ref[ma:, :] = pvb
        else:
            alpha = jnp.exp2(m_old - m_new)
            acc_ref[:ma, :] = alpha[:ma] * acc_ref[:ma, :] + pva
            acc_ref[ma:, :] = alpha[ma:] * acc_ref[ma:, :] + pvb
        m_ref[...] = m_new

    def lat(j):
        return pl.multiple_of(j * tk, tk)

    def latent_queries():
        scores(ctx_start, CTX_LEN, 0)
        scores(lat(0), tk, 1)
        accumulate(ctx_start, CTX_LEN, 0, True)

        def pair(i):
            scores(lat(2 * i + 1), tk, 0)
            accumulate(lat(2 * i), tk, 1, False)
            scores(lat(2 * i + 2), tk, 1)
            accumulate(lat(2 * i + 1), tk, 0, False)

        def body(i, carry):
            for u in range(PAIRS_PER_STEP):
                pair(i * PAIRS_PER_STEP + u)
            return carry

        n_pairs = (n_lat_blocks - 2) // 2
        n_steps = n_pairs // PAIRS_PER_STEP
        lax.fori_loop(0, n_steps, body, 0)
        for i in range(n_steps * PAIRS_PER_STEP, n_pairs):
            pair(i)
        scores(lat(n_lat_blocks - 1), tk, 0)
        accumulate(lat(n_lat_blocks - 2), tk, 1, False)
        accumulate(lat(n_lat_blocks - 1), tk, 0, False)

    def context_queries():
        scores(ctx_start, CTX_LEN, 0)
        accumulate(ctx_start, CTX_LEN, 0, True)

    if ctx_tile is None:
        latent_queries()
    else:
        pl.when(qt != ctx_tile)(latent_queries)
        pl.when(qt == ctx_tile)(context_queries)

    acc = acc_ref[...]
    r = acc / pltpu.roll(acc, half, 1)
    ra, rb = r[:ma], r[ma:]
    if mode == "plain":
        for i in range(n_pieces // 2):
            o = jnp.where(lower, ra[i * TILE:(i + 1) * TILE], rb[i * TILE:(i + 1) * TILE])
            o_ref[:, i * LANES:(i + 1) * LANES] = o.astype(BF16)
    else:
        aux = aux_ref[...]
        l1 = jnp.sum(aux[0:1] * aux[1:2], axis=-1, keepdims=True)
        l2 = jnp.sum(aux[2:3] * aux[3:4], axis=-1, keepdims=True)
        lam = jnp.exp(l1) - jnp.exp(l2) + lam_init
        oa = ra[:TILE] - lam * ra[TILE:]
        ob = rb[:TILE] - lam * rb[TILE:]
        o = jnp.where(lower, oa, ob)
        sq = o * o
        ss_a = jnp.sum(jnp.where(lower, sq, 0.0), axis=-1, keepdims=True)
        ss_b = jnp.sum(jnp.where(lower, 0.0, sq), axis=-1, keepdims=True)
        ms = jnp.where(lower, ss_a, ss_b) * (1.0 / HEAD_DIM)
        o = (o * lax.rsqrt(ms + EPS)) * aux[4:5]
        o_ref[...] = (o * (1.0 - lam_init)).astype(BF16)


def _flash(q, k, v, aux, *, n_qblk, n_sub, n_hp, qt_off, n_qt, n_lat, ctx_tile, mode, lam_init=0.0):
    b, t_all, _ = q.shape
    qw = n_qblk * LANES
    tk = FLASH_TK if n_lat % (2 * FLASH_TK) == 0 else 512
    assert n_lat % (2 * tk) == 0 and tk >= CTX_LEN
    m_rows = n_qblk * n_sub * TILE
    kern = functools.partial(_flash_kernel, n_qblk=n_qblk, n_sub=n_sub, tk=tk,
                             n_lat_blocks=n_lat // tk, ctx_tile=ctx_tile, qt_off=qt_off,
                             mode=mode, lam_init=lam_init)
    return pl.pallas_call(
        kern,
        out_shape=jax.ShapeDtypeStruct((b, n_qt * TILE, n_hp * qw), BF16),
        grid=(b, n_hp, n_qt),
        in_specs=[pl.BlockSpec((None, TILE, qw), lambda bi, hp, qt: (bi, qt + qt_off, hp)),
                  pl.BlockSpec((None, t_all, LANES), lambda bi, hp, qt: (bi, 0, hp)),
                  pl.BlockSpec((None, t_all, LANES), lambda bi, hp, qt: (bi, 0, hp)),
                  pl.BlockSpec((8, LANES), lambda bi, hp, qt: (0, 0))],
        out_specs=pl.BlockSpec((None, TILE, qw), lambda bi, hp, qt: (bi, qt, hp)),
        scratch_shapes=[pltpu.VMEM((t_all, LANES), BF16),
                        pltpu.VMEM((t_all, LANES), BF16),
                        pltpu.VMEM((m_rows, LANES), BF16),
                        pltpu.VMEM((m_rows, LANES), F32),
                        pltpu.VMEM((m_rows, LANES), F32),
                        pltpu.VMEM((m_rows, tk), F32),
                        pltpu.VMEM((m_rows, tk), F32),
                        pltpu.VMEM((m_rows, LANES), F32),
                        pltpu.VMEM((m_rows, LANES), F32)],
        compiler_params=_cparams(3),
        name="flash_" + mode,
    )(q, k, v, aux)


def _na_kernel(q_ref, k0, k1, k2, k3, v0, v1, v2, v3, kc_ref, vc_ref, bias_ref, o_ref):
    lane = lax.broadcasted_iota(jnp.int32, (1, LANES), 1)
    lower = lane < LANES // 2
    qf = q_ref[...].astype(F32)
    kw = jnp.concatenate([k0[...], k1[...], k2[...], k3[...]], axis=0)
    vw = jnp.concatenate([v0[...], v1[...], v2[...], v3[...]], axis=0)
    kc = kc_ref[...]
    vwf = vw.astype(F32)
    vcf = vc_ref[...].astype(F32)
    outs = []
    for hh in range(2):
        msk = lower if hh == 0 else jnp.logical_not(lower)
        qh = jnp.where(msk, qf, 0.0).astype(BF16)
        vw_h = jnp.where(msk, vwf, 1.0).astype(BF16)
        vc_h = jnp.where(msk, vcf, 1.0).astype(BF16)
        n_pair = NA_KROWS // 2
        bias = jnp.concatenate(
            [jnp.concatenate([bias_ref[hh, a * n_pair + j] for j in range(n_pair)], axis=1)
             for a in range(NA_QROWS)], axis=0)
        s_w = _dot_nt(qh, kw) + bias
        s_c = _dot_nt(qh, kc)
        m = jnp.maximum(jnp.max(s_w, axis=-1, keepdims=True), jnp.max(s_c, axis=-1, keepdims=True))
        p_w = jnp.exp2((s_w - m).astype(BF16))
        p_c = jnp.exp2((s_c - m).astype(BF16))
        o = _dot(p_w, vw_h) + _dot(p_c, vc_h)
        outs.append(o / pltpu.roll(o, LANES // 2, 1))
    o_ref[...] = jnp.where(lower, outs[0], outs[1]).astype(BF16)


def _neighbourhood_attention(qa, ka, va, bias, n_lat):
    b = qa.shape[0]
    q_tok = NA_QROWS * GRID_W
    v_tok = q_tok // 2
    n_rb = n_lat // q_tok
    n_view = n_lat // v_tok
    ctx_blk = n_lat // v_tok

    def view(j):
        return lambda rb, hp, bi: (bi, jnp.clip(2 * rb - 1 + j, 0, n_view - 1), hp)

    kv_specs = [pl.BlockSpec((None, v_tok, LANES), view(j)) for j in range(4)]
    ctx_spec = pl.BlockSpec((None, CTX_LEN, LANES), lambda rb, hp, bi: (bi, ctx_blk, hp))

    def bias_map(rb, hp, bi):
        pat = jnp.where(rb == 0, 0, jnp.where(rb == n_rb - 1, 2, 1))
        return (hp, pat, 0, 0, 0)

    return pl.pallas_call(
        _na_kernel,
        out_shape=jax.ShapeDtypeStruct((b, n_lat, W_A), BF16),
        grid=(n_rb, NA_HEADS // 2, b),
        in_specs=[pl.BlockSpec((None, q_tok, LANES), lambda rb, hp, bi: (bi, rb, hp))]
                 + kv_specs + kv_specs + [ctx_spec, ctx_spec,
                 pl.BlockSpec((2, None, NA_QROWS * NA_KROWS // 2, GRID_W, 2 * GRID_W), bias_map)],
        out_specs=pl.BlockSpec((None, q_tok, LANES), lambda rb, hp, bi: (bi, rb, hp)),
        compiler_params=_cparams(3),
        name="neighbourhood_attention",
    )(qa, ka, ka, ka, ka, va, va, va, va, ka, va, bias)


def _na_bias_table(rpb, rows):
    cols = np.arange(GRID_W)
    c0 = np.clip(cols - NA_WIN_W // 2, 0, GRID_W - NA_WIN_W)
    cc = cols[None, :]
    col_ok = (cc >= c0[:, None]) & (cc < c0[:, None] + NA_WIN_W)
    dc = np.clip(cc - cols[:, None] + (NA_WIN_W - 1), 0, 2 * NA_WIN_W - 2)
    e = jnp.where(col_ok[None, None], (rpb.astype(F32) * LOG2E)[:, :, dc], NEG)
    e = jnp.concatenate([e, jnp.full_like(e[:, :1], NEG)], axis=1)
    a = np.arange(NA_QROWS)[:, None]
    i = np.arange(NA_KROWS)[None, :]
    pats = []
    for r_base in (0, NA_QROWS, rows - NA_QROWS):
        r = r_base + a
        key_row = r_base - NA_WIN_H // 2 + i
        r0 = np.clip(r - NA_WIN_H // 2, 0, rows - NA_WIN_H)
        ok = (key_row >= r0) & (key_row < r0 + NA_WIN_H) & (key_row >= 0) & (key_row < rows)
        dr = np.where(ok, key_row - r + (NA_WIN_H - 1), 2 * NA_WIN_H - 1)
        pats.append(dr)
    dr_all = np.stack(pats)
    pairs = dr_all.reshape(-1, 2)
    uniq, inv = np.unique(pairs, axis=0, return_inverse=True)
    pair_blocks = jnp.concatenate([e[:, uniq[:, 0]], e[:, uniq[:, 1]]], axis=-1)
    t = pair_blocks[:, inv.reshape(-1)]
    return t.reshape(NA_HEADS, 3, NA_QROWS * NA_KROWS // 2, GRID_W, 2 * GRID_W)


def _outproj_kernel(x_ref, xc_ref, oa_ref, ob_ref, oc_ref, wa_ref, wb_ref, wc_ref, mod_ref, gain_ref,
                    wrh_ref, wrl_ref, br_ref, tri_ref, x1_ref, tok_ref, route_ref, cnt_ref, run_ref,
                    *, region, group_batches, n_lat_tiles):
    mod = mod_ref[...]
    y = _dot(oa_ref[...], wa_ref[...]) + _dot(ob_ref[...], wb_ref[...]) + _dot(oc_ref[...], wc_ref[...])
    x1 = jnp.where(pl.program_id(1) == n_lat_tiles, xc_ref[...], x_ref[...]) + mod[2:3] * y
    x1_ref[...] = x1
    ms = jnp.mean(x1 * x1, axis=-1, keepdims=True)
    t = (x1 * lax.rsqrt(ms + EPS)) * gain_ref[...]
    t = t * (1.0 + mod[4:5]) + mod[3:4]
    tok_ref[...] = t

    t_hi, t_lo = _split_bf16(t)
    wrh = wrh_ref[...]
    logits = _dot(t_hi, wrh) + _dot(t_lo, wrh) + _dot(t_hi, wrl_ref[...]) + br_ref[...]

    lane = lax.broadcasted_iota(jnp.int32, logits.shape, 1)
    lane_f = lane.astype(F32)
    is_g = lane < N_GROUPS
    gl = jnp.where(is_g, logits, NEG)
    gmax = jnp.max(gl, axis=-1, keepdims=True)
    g_sel = jnp.min(jnp.where(gl == gmax, lane_f, 1e9), axis=-1, keepdims=True)
    p_grp = 1.0 / jnp.sum(jnp.where(is_g, jnp.exp(gl - gmax), 0.0), axis=-1, keepdims=True)
    grp_of_lane = lax.shift_right_arithmetic(lane - N_GROUPS, 2).astype(F32)
    in_grp = (lane >= N_GROUPS) & (lane < N_GROUPS + N_EXPERTS) & (grp_of_lane == g_sel)
    el = jnp.where(in_grp, logits, NEG)
    v1 = jnp.max(el, axis=-1, keepdims=True)
    i1 = jnp.min(jnp.where(el == v1, lane_f, 1e9), axis=-1, keepdims=True)
    el2 = jnp.where(lane_f == i1, NEG, el)
    v2 = jnp.max(el2, axis=-1, keepdims=True)
    i2 = jnp.min(jnp.where(el2 == v2, lane_f, 1e9), axis=-1, keepdims=True)
    e2 = jnp.exp(v2 - v1)
    den = 1.0 + e2
    w1 = p_grp / den
    w2 = p_grp * e2 / den

    @pl.when((lax.rem(pl.program_id(0), group_batches) == 0) & (pl.program_id(1) == 0))
    def _():
        run_ref[...] = jnp.zeros(run_ref.shape, F32)

    ind = jnp.where(lane_f == i1, 1.0, 0.0) + jnp.where(lane_f == i2, 1.0, 0.0)
    rank = _dot(tri_ref[...], ind.astype(BF16)) + run_ref[0:1, :]

    def pick(m, l):
        return jnp.sum(jnp.where(lane_f == l, m, 0.0), axis=-1, keepdims=True)

    pos1 = (i1 - N_GROUPS) * region + pick(rank, i1)
    pos2 = (i2 - N_GROUPS) * region + pick(rank, i2)
    route_ref[...] = jnp.where(lane == 0, pos1, jnp.where(lane == 1, pos2,
                               jnp.where(lane == 2, w1, jnp.where(lane == 3, w2, 0.0))))
    run = run_ref[...] + jnp.sum(ind, axis=0, keepdims=True)
    run_ref[...] = run
    cnt_ref[...] = run


def _out_projection(x_lat, x_ctx, ctx_blk, oa, ob, oc, wa, wb, wc, modsel, gain, wrh, wrl, br, n_tiles,
                    n_lat_tiles):
    b, _, d = x_lat.shape
    tok = lambda bi, ti: (bi, ti, 0)
    const2 = lambda bi, ti: (0, 0)
    rows = n_tiles * TILE
    nb = b // MOE_GROUPS if b % MOE_GROUPS == 0 else b
    tri = jnp.asarray(np.tril(np.ones((TILE, TILE), np.float32), -1), BF16)
    return pl.pallas_call(
        functools.partial(_outproj_kernel, region=nb * rows, group_batches=nb, n_lat_tiles=n_lat_tiles),
        out_shape=[jax.ShapeDtypeStruct((b, rows, d), F32),
                   jax.ShapeDtypeStruct((b, rows, d), F32),
                   jax.ShapeDtypeStruct((b, rows, LANES), F32),
                   jax.ShapeDtypeStruct((8 * (b // nb), LANES), F32)],
        grid=(b, n_tiles),
        in_specs=_token_specs(d, n_lat_tiles, ctx_blk) + [
                  pl.BlockSpec((None, TILE, W_A), tok),
                  pl.BlockSpec((None, TILE, W_B), tok),
                  pl.BlockSpec((None, TILE, W_C), tok),
                  pl.BlockSpec((W_A, d), const2),
                  pl.BlockSpec((W_B, d), const2),
                  pl.BlockSpec((W_C, d), const2),
                  pl.BlockSpec((None, 6, d), lambda bi, ti: (2 * bi + (ti >= n_lat_tiles).astype(jnp.int32), 0, 0)),
                  pl.BlockSpec((1, d), const2),
                  pl.BlockSpec((d, LANES), const2),
                  pl.BlockSpec((d, LANES), const2),
                  pl.BlockSpec((1, LANES), const2),
                  pl.BlockSpec((TILE, TILE), const2)],
        out_specs=[pl.BlockSpec((None, TILE, d), tok),
                   pl.BlockSpec((None, TILE, d), tok),
                   pl.BlockSpec((None, TILE, LANES), tok),
                   pl.BlockSpec((8, LANES), lambda bi, ti: (bi // nb, 0))],
        scratch_shapes=[pltpu.VMEM((8, LANES), F32)],
        compiler_params=_cparams(2),
        name="out_projection",
    )(x_lat, x_ctx, oa, ob, oc, wa, wb, wc, modsel, gain, wrh, wrl, br, tri)


def _sc_mesh():
    return plsc.VectorSubcoreMesh(core_axis_name="core", subcore_axis_name="subcore")


def _sc_worker_base(per_worker):
    wid = lax.axis_index("subcore") * SC_CORES + lax.axis_index("core")
    return wid * per_worker


def _sc_scratch(d, dtype):
    return ([pltpu.VMEM((SC_ROWS,), jnp.int32)] * SC_BUFS + [pltpu.VMEM((SC_ROWS, d), dtype)] * SC_BUFS
            + [pltpu.SemaphoreType.DMA] * (2 * SC_BUFS))


def _sc_split(scratch):
    return (scratch[:SC_BUFS], scratch[SC_BUFS:2 * SC_BUFS], scratch[2 * SC_BUFS:3 * SC_BUFS],
            scratch[3 * SC_BUFS:])


def _sc_chunk_loop(per_worker, group):
    chunks = per_worker // SC_ROWS
    full = chunks // SC_BUFS * SC_BUFS

    @pl.loop(0, full, step=SC_BUFS)
    def _(c):
        group(c, SC_BUFS)

    if chunks > full:
        group(full, chunks - full)


def _sc_scatter_rows(x, row_off, n, idx, n_out):
    d = x.shape[1]
    per_worker = 2 * n // (SC_CORES * SC_SUBCORES)
    assert per_worker % SC_ROWS == 0 and n % SC_ROWS == 0

    @functools.partial(pl.kernel, out_type=jax.ShapeDtypeStruct((n_out, d), x.dtype),
                       mesh=_sc_mesh(), scratch_types=_sc_scratch(d, x.dtype))
    def scatter(x_hbm, i_hbm, o_hbm, *scratch):
        idx_v, rows_v, sem_in, sem_out = _sc_split(scratch)
        base = _sc_worker_base(per_worker)

        def group(c, n_bufs):
            reads = []
            for u in range(n_bufs):
                a = pl.multiple_of(base + (c + u) * SC_ROWS, SC_ROWS)
                t = pl.multiple_of(row_off + lax.rem(a, n), SC_ROWS)
                pltpu.sync_copy(i_hbm.at[pl.ds(a, SC_ROWS)], idx_v[u])
                reads.append(pltpu.async_copy(x_hbm.at[pl.ds(t, SC_ROWS)], rows_v[u], sem_in[u]))
            writes = []
            for u in range(n_bufs):
                reads[u].wait()
                writes.append(pltpu.async_copy(rows_v[u], o_hbm.at[idx_v[u]], sem_out[u]))
            for w in writes:
                w.wait()

        _sc_chunk_loop(per_worker, group)

    return scatter(x, idx)


def _sc_gather_rows(src, idx):
    m = idx.shape[0]
    d = src.shape[1]
    per_worker = m // (SC_CORES * SC_SUBCORES)
    assert per_worker % SC_ROWS == 0

    @functools.partial(pl.kernel, out_type=jax.ShapeDtypeStruct((m, d), src.dtype),
                       mesh=_sc_mesh(), scratch_types=_sc_scratch(d, src.dtype))
    def gather(s_hbm, i_hbm, o_hbm, *scratch):
        idx_v, rows_v, sem_in, sem_out = _sc_split(scratch)
        base = _sc_worker_base(per_worker)

        def group(c, n_bufs):
            offs, reads = [], []
            for u in range(n_bufs):
                a = pl.multiple_of(base + (c + u) * SC_ROWS, SC_ROWS)
                offs.append(a)
                pltpu.sync_copy(i_hbm.at[pl.ds(a, SC_ROWS)], idx_v[u])
                reads.append(pltpu.async_copy(s_hbm.at[idx_v[u]], rows_v[u], sem_in[u]))
            writes = []
            for u in range(n_bufs):
                reads[u].wait()
                writes.append(pltpu.async_copy(rows_v[u], o_hbm.at[pl.ds(offs[u], SC_ROWS)], sem_out[u]))
            for w in writes:
                w.wait()

        _sc_chunk_loop(per_worker, group)

    return gather(src, idx)


def _expert_ffn_kernel(blk_ref, exp_ref, x_ref, wg_ref, wu_ref, wd_ref, y_ref, wgb_ref, wub_ref, wdb_ref):
    j = pl.program_id(0)

    @pl.when((j == 0) | (exp_ref[j] != exp_ref[jnp.maximum(j - 1, 0)]))
    def _():
        wgb_ref[...] = wg_ref[...].astype(BF16)
        wub_ref[...] = wu_ref[...].astype(BF16)
        wdb_ref[...] = wd_ref[...].astype(BF16)

    x = x_ref[...].astype(BF16)
    hid = jax.nn.silu(_dot(x, wgb_ref[...])) * _dot(x, wub_ref[...])
    y_ref[...] = _dot(hid.astype(BF16), wdb_ref[...])


def _expert_ffn(xs, blk, exp, wg, wu, wd, layer):
    rows, d = xs.shape
    w_map = lambda j, blk, exp: (layer, exp[j], 0, 0)
    return pl.pallas_call(
        _expert_ffn_kernel,
        out_shape=jax.ShapeDtypeStruct((rows, d), F32),
        grid_spec=pltpu.PrefetchScalarGridSpec(
            num_scalar_prefetch=2,
            grid=(blk.shape[0],),
            in_specs=[pl.BlockSpec((MOE_TILE, d), lambda j, blk, exp: (blk[j], 0)),
                      pl.BlockSpec((None, None, d, EXPERT_HIDDEN), w_map),
                      pl.BlockSpec((None, None, d, EXPERT_HIDDEN), w_map),
                      pl.BlockSpec((None, None, EXPERT_HIDDEN, d), w_map)],
            out_specs=pl.BlockSpec((MOE_TILE, d), lambda j, blk, exp: (blk[j], 0)),
            scratch_shapes=[pltpu.VMEM((d, EXPERT_HIDDEN), BF16),
                            pltpu.VMEM((d, EXPERT_HIDDEN), BF16),
                            pltpu.VMEM((EXPERT_HIDDEN, d), BF16)]),
        compiler_params=_cparams(1),
        name="expert_ffn",
    )(blk, exp, xs, wg, wu, wd)


def _combine_kernel(x1_ref, y1_ref, y2_ref, route_ref, mod_ref, fgain_ref, *rest, final):
    o_ref = rest[-1]
    route = route_ref[...]
    y = route[:, 2:3] * y1_ref[...] + route[:, 3:4] * y2_ref[...]
    x2 = x1_ref[...] + mod_ref[5:6, :] * y
    if final:
        ms = jnp.mean(x2 * x2, axis=-1, keepdims=True)
        x2 = (x2 * lax.rsqrt(ms + EPS)) * fgain_ref[...]
    o_ref[...] = x2


def _combine(x1, ys, route, modsel, fgain, prev, b0, nb, n_lat_tiles, final):
    b, rows, d = x1.shape
    n_t = rows // TILE
    tok = lambda bi, ti: (b0 + bi, ti, 0)
    in_specs = [pl.BlockSpec((None, TILE, d), tok),
                pl.BlockSpec((TILE, d), lambda bi, ti: (bi * n_t + ti, 0)),
                pl.BlockSpec((TILE, d), lambda bi, ti: ((nb + bi) * n_t + ti, 0)),
                pl.BlockSpec((None, TILE, LANES), tok),
                pl.BlockSpec((None, 6, d),
                             lambda bi, ti: (2 * (b0 + bi) + (ti >= n_lat_tiles).astype(jnp.int32), 0, 0)),
                pl.BlockSpec((1, d), lambda bi, ti: (0, 0))]
    args = [x1, ys, ys, route, modsel, fgain]
    aliases = {}
    if prev is not None:
        in_specs.append(pl.BlockSpec(memory_space=pl.ANY))
        args.append(prev)
        aliases = {len(args) - 1: 0}
    return pl.pallas_call(
        functools.partial(_combine_kernel, final=final),
        out_shape=jax.ShapeDtypeStruct((b, rows, d), F32),
        grid=(nb, n_t),
        in_specs=in_specs,
        out_specs=pl.BlockSpec((None, TILE, d), tok),
        input_output_aliases=aliases,
        compiler_params=_cparams(2),
        name="moe_combine",
    )(*args)


def _routed_moe(tok, route, cnt, x1, wg, wu, wd, layer, modsel, fgain, n_lat_tiles, final):
    b, rows, d = x1.shape
    n_groups = cnt.shape[0] // 8
    nb = b // n_groups
    n = nb * rows
    flat = route.reshape(b * rows, LANES)
    tok_flat = tok.reshape(b * rows, d)
    out = None
    for g in range(n_groups):
        part = flat[g * n:(g + 1) * n]
        idx = jnp.concatenate([part[:, 0], part[:, 1]]).astype(jnp.int32)
        xs = _sc_scatter_rows(tok_flat, g * n, n, idx, N_EXPERTS * n)

        counts = cnt[8 * g, N_GROUPS:N_GROUPS + N_EXPERTS].astype(jnp.int32)
        tiles = (counts + MOE_TILE - 1) // MOE_TILE
        ends = jnp.cumsum(tiles)
        n_sched = 2 * n // MOE_TILE + N_EXPERTS
        j = jnp.minimum(jnp.arange(n_sched, dtype=jnp.int32), ends[-1] - 1)
        exp = jnp.sum((j[:, None] >= ends[None, :]).astype(jnp.int32), axis=1)
        blk = exp * (n // MOE_TILE) + j - (ends - tiles)[exp]

        ys = _expert_ffn(xs, blk, exp, wg, wu, wd, layer)
        yg = _sc_gather_rows(ys, idx)
        out = _combine(x1, yg, route, modsel, fgain, out, g * nb, nb, n_lat_tiles, final)
    return out


def _rope_tables(n_lat):
    t = jnp.arange(n_lat)
    row = (t // GRID_W).astype(F32)
    col = (t % GRID_W).astype(F32)

    def cs(dim):
        quarter = dim // 4
        freqs = ROPE_THETA ** (-jnp.arange(quarter, dtype=F32) / quarter)
        ang = jnp.concatenate([row[:, None] * freqs, col[:, None] * freqs], axis=-1)
        cos = jnp.tile(jnp.cos(ang), (1, 2 * LANES // dim))
        sin = jnp.tile(jnp.sin(ang), (1, 2 * LANES // dim))
        cos = jnp.concatenate([cos, jnp.ones((CTX_LEN, LANES), F32)], axis=0)
        sin = jnp.concatenate([sin, jnp.zeros((CTX_LEN, LANES), F32)], axis=0)
        return cos, sin

    cos_b, sin_b = cs(DIFF_QK_DIM)
    cos_c, sin_c = cs(HEAD_DIM)
    return jnp.concatenate([cos_b, sin_b, cos_c, sin_c], axis=1)


def _reordered_w_in(w_in):
    o_c = 3 * W_A + 3 * W_B
    heads = [w_in[:, o_c + h * HEAD_DIM:o_c + (h + 1) * HEAD_DIM] for h in GQA_Q_ORDER]
    return jnp.concatenate([w_in[:, :o_c]] + heads + [w_in[:, o_c + W_C:]], axis=1).astype(BF16)


def kernel(x, c, ctx, c_ctx, w_mod, b_mod, norm_attn, norm_ffn, w_in, w_out, na_rpb, diff_lambda_q1, diff_lambda_k1, diff_lambda_q2, diff_lambda_k2, diff_subln, gqa_q_norm, gqa_k_norm, router_group_w, router_group_b, router_expert_w, router_expert_b, w_gate, w_up, w_down, final_norm):
    b, s, d = x.shape
    assert d == D_MODEL and ctx.shape[1] == CTX_LEN and s % (NA_QROWS * GRID_W) == 0
    rows = s // GRID_W
    assert rows >= 2 * NA_QROWS
    t_all = s + CTX_LEN
    n_lat_tiles = s // TILE

    c_rows = jnp.zeros((8, d), F32).at[:b].set(c).at[b].set(c_ctx)
    mod = _modulation(c_rows, w_mod, b_mod)

    tab = _rope_tables(s)
    hidx = np.arange(HEAD_DIM)
    partner = np.where(hidx < HEAD_DIM // 2, hidx + HEAD_DIM // 2, hidx - HEAD_DIM // 2)
    blk = np.arange(W_C) // HEAD_DIM
    ones = jnp.asarray((blk[:, None] == blk[None, :]).astype(np.float32), BF16)
    dummy_aux = jnp.zeros((8, LANES), F32)

    x_lat, x_ctx, ctx_blk = x, ctx, 0
    for l in range(DEPTH):
        ctx_out = l < DEPTH - 1
        lam_init = 0.8 - 0.6 * math.exp(-0.3 * l)
        m_lat = mod[l, :b].reshape(b, 1, 6, d)
        m_ctx = jnp.broadcast_to(mod[l, b].reshape(1, 1, 6, d), (b, 1, 6, d))
        modsel = jnp.concatenate([m_lat, m_ctx], axis=1).reshape(2 * b, 6, d)

        gq = jnp.stack([jnp.tile(gqa_q_norm[l], GQA_Q_HEADS), jnp.tile(gqa_q_norm[l][partner], GQA_Q_HEADS)])
        gk = jnp.stack([jnp.tile(gqa_k_norm[l], GQA_KV_HEADS), jnp.tile(gqa_k_norm[l][partner], GQA_KV_HEADS)])
        qa, ka, va, qb, kb, vb, qc, kc, vc = _in_projection(
            x_lat, x_ctx, ctx_blk, modsel, norm_attn[l][None], _reordered_w_in(w_in[l]), tab, gq, gk, ones,
            n_lat_tiles)

        n_qt = n_lat_tiles + 1 if ctx_out else n_lat_tiles
        ctx_tile = n_lat_tiles if ctx_out else None
        oa = _neighbourhood_attention(qa, ka, va, _na_bias_table(na_rpb[l], rows), s)
        if ctx_out:
            oa_ctx = _flash(qa, ka, va, dummy_aux, n_qblk=1, n_sub=2, n_hp=NA_HEADS // 2,
                            qt_off=n_lat_tiles, n_qt=1, n_lat=s, ctx_tile=n_lat_tiles,
                            mode="plain")
            oa = jnp.concatenate([oa, oa_ctx], axis=1)
        pad = lambda v: jnp.pad(v, (0, LANES - v.shape[0]))
        aux = jnp.stack([pad(diff_lambda_q1[l]), pad(diff_lambda_k1[l]), pad(diff_lambda_q2[l]),
                         pad(diff_lambda_k2[l]), jnp.tile(diff_subln[l], 2),
                         jnp.zeros((LANES,), F32), jnp.zeros((LANES,), F32), jnp.zeros((LANES,), F32)])
        ob = _flash(qb, kb, vb, aux, n_qblk=1, n_sub=4, n_hp=DIFF_HEADS // 2, qt_off=0, n_qt=n_qt,
                    n_lat=s, ctx_tile=ctx_tile, mode="diff", lam_init=lam_init)
        oc = _flash(qc, kc, vc, dummy_aux, n_qblk=3, n_sub=2, n_hp=1, qt_off=0, n_qt=n_qt,
                    n_lat=s, ctx_tile=ctx_tile, mode="plain")

        w_o = w_out[l]
        o_c = W_A + W_B
        w_oc = jnp.concatenate([w_o[o_c + h * HEAD_DIM:o_c + (h + 1) * HEAD_DIM] for h in GQA_Q_ORDER], axis=0)
        wr = jnp.zeros((d, LANES), F32)
        wr = wr.at[:, :N_GROUPS].set(router_group_w[l]).at[:, N_GROUPS:N_GROUPS + N_EXPERTS].set(router_expert_w[l])
        wrh, wrl = _split_bf16(wr)
        br = jnp.zeros((1, LANES), F32)
        br = br.at[0, :N_GROUPS].set(router_group_b[l]).at[0, N_GROUPS:N_GROUPS + N_EXPERTS].set(router_expert_b[l])
        x1, tok, route, cnt = _out_projection(
            x_lat, x_ctx, ctx_blk, oa, ob, oc, w_o[:W_A].astype(BF16), w_o[W_A:W_A + W_B].astype(BF16),
            w_oc.astype(BF16), modsel, norm_ffn[l][None], wrh, wrl, br, n_qt, n_lat_tiles)
        xs = _routed_moe(tok, route, cnt, x1, w_gate, w_up, w_down, l, modsel, final_norm[None],
                         n_lat_tiles, final=not ctx_out)
        x_lat, x_ctx, ctx_blk = xs, xs, n_lat_tiles
    return xs
```

```python
import functools
import math

import numpy as np
import jax
import jax.numpy as jnp
from jax import lax
from jax.experimental import pallas as pl
from jax.experimental.pallas import tpu as pltpu
from jax.experimental.pallas import tpu_sc as plsc

F32 = jnp.float32
BF16 = jnp.bfloat16

D_MODEL = 1024
DEPTH = 2
GRID_W = 64
CTX_LEN = 256
HEAD_DIM = 64
NA_HEADS = 6
NA_WIN_H = 8
NA_WIN_W = 16
DIFF_HEADS = 4
DIFF_QK_DIM = 32
GQA_Q_HEADS = 6
GQA_KV_HEADS = 2
N_GROUPS = 4
EXPERTS_PER_GROUP = 4
N_EXPERTS = 16
EXPERT_HIDDEN = 512
ROPE_THETA = 10000.0
EPS = 1e-6
W_A = NA_HEADS * HEAD_DIM
W_B = DIFF_HEADS * 2 * DIFF_QK_DIM
W_C = GQA_Q_HEADS * HEAD_DIM
W_KC = GQA_KV_HEADS * HEAD_DIM
IN_WIDTH = 3 * W_A + 3 * W_B + W_C + 2 * W_KC

LANES = 128
TILE = CTX_LEN
NA_QROWS = 8
NA_KROWS = 16
NEG = -1e30
LOG2E = 1.4426950408889634
HI16 = -65536
VMEM_LIMIT = 56 * 1024 * 1024
FLASH_TK = 512
PAIRS_PER_STEP = 2
MOE_TILE = 512
MOE_GROUPS = 2
SC_ROWS = 16
SC_BUFS = 4
SC_CORES = 2
SC_SUBCORES = 16

GQA_Q_ORDER = (0, 3, 1, 4, 2, 5)


def _cparams(n_axes):
    return pltpu.CompilerParams(dimension_semantics=("arbitrary",) * n_axes,
                                vmem_limit_bytes=VMEM_LIMIT)


def _split_bf16(a):
    hi = a.astype(BF16)
    lo = (a - hi.astype(F32)).astype(BF16)
    return hi, lo


def _dot(a, b):
    return jnp.dot(a, b, preferred_element_type=F32)


def _dot_nt(a, b):
    return lax.dot_general(a, b, (((1,), (1,)), ((), ())), preferred_element_type=F32)


def _mod_kernel(c_ref, w_ref, b_ref, o_ref):
    c = c_ref[...]
    a = c * jax.nn.sigmoid(c)
    a_hi, a_lo = _split_bf16(a)
    w_hi, w_lo = _split_bf16(w_ref[...])
    o_ref[...] = _dot(a_hi, w_hi) + _dot(a_lo, w_hi) + _dot(a_hi, w_lo) + b_ref[...]


def _modulation(c_rows, w_mod, b_mod):
    depth, d, n = w_mod.shape
    bn = 1536
    return pl.pallas_call(
        _mod_kernel,
        out_shape=jax.ShapeDtypeStruct((depth, 8, n), F32),
        grid=(depth, n // bn),
        in_specs=[pl.BlockSpec((8, d), lambda l, j: (0, 0)),
                  pl.BlockSpec((None, d, bn), lambda l, j: (l, 0, j)),
                  pl.BlockSpec((None, 1, bn), lambda l, j: (l, 0, j))],
        out_specs=pl.BlockSpec((None, 8, bn), lambda l, j: (l, 0, j)),
        compiler_params=_cparams(2),
        name="adaln_mod",
    )(c_rows, w_mod, b_mod.reshape(depth, 1, n))


def _head_mean_sq(t, ones):
    hi, lo = _split_bf16(t * t)
    return (_dot(hi, ones) + _dot(lo, ones)) * (1.0 / HEAD_DIM)


def _rotate_half(p, head):
    w = p.shape[1]
    half = head // 2
    lane = lax.broadcasted_iota(jnp.int32, (1, w), 1)
    first = (lane & (head - 1)) < half
    from_right = pltpu.roll(p, w - half, 1)
    from_left = pltpu.roll(p, half, 1)
    return jnp.where(first, -from_right, from_left)


def _inproj_kernel(x_ref, xc_ref, mod_ref, gain_ref, w_ref, tab_ref, gq_ref, gk_ref, ones_ref,
                   qa_ref, ka_ref, va_ref, qb_ref, kb_ref, vb_ref, qc_ref, kc_ref, vc_ref,
                   *, n_lat_tiles):
    x = jnp.where(pl.program_id(1) == n_lat_tiles, xc_ref[...], x_ref[...])
    mod = mod_ref[...]
    ms = jnp.mean(x * x, axis=-1, keepdims=True)
    h = (x * lax.rsqrt(ms + EPS)) * gain_ref[...]
    h = h * (1.0 + mod[1:2]) + mod[0:1]
    hb = h.astype(BF16)

    def proj(a, b):
        return _dot(hb, w_ref[:, a:b])

    pa = proj(0, 3 * W_A)
    qa_ref[...] = (pa[:, :W_A] * (HEAD_DIM ** -0.5 * LOG2E)).astype(BF16)
    ka_ref[...] = pa[:, W_A:2 * W_A].astype(BF16)
    va_ref[...] = pa[:, 2 * W_A:].astype(BF16)

    tab = tab_ref[...]
    cos_b = jnp.concatenate([tab[:, 0:LANES]] * 2, axis=1)
    sin_b = jnp.concatenate([tab[:, LANES:2 * LANES]] * 2, axis=1)
    cos_c1 = tab[:, 2 * LANES:3 * LANES]
    sin_c1 = tab[:, 3 * LANES:4 * LANES]
    cos_c = jnp.concatenate([cos_c1] * 3, axis=1)
    sin_c = jnp.concatenate([sin_c1] * 3, axis=1)

    o_b = 3 * W_A
    pb = proj(o_b, o_b + 3 * W_B)
    qb = pb[:, :W_B]
    kb = pb[:, W_B:2 * W_B]
    qb = qb * cos_b + _rotate_half(qb, DIFF_QK_DIM) * sin_b
    qb_ref[...] = (qb * (DIFF_QK_DIM ** -0.5 * LOG2E)).astype(BF16)
    kb_ref[...] = (kb * cos_b + _rotate_half(kb, DIFF_QK_DIM) * sin_b).astype(BF16)
    vb_ref[...] = pb[:, 2 * W_B:].astype(BF16)

    o_c = o_b + 3 * W_B
    pc = proj(o_c, IN_WIDTH)
    ones = ones_ref[...]
    qc = pc[:, :W_C]
    kc = pc[:, W_C:W_C + W_KC]
    nq = lax.rsqrt(_head_mean_sq(qc, ones) + EPS)
    nk = lax.rsqrt(_head_mean_sq(kc, ones[:W_KC, :W_KC]) + EPS)
    gq = gq_ref[...]
    gk = gk_ref[...]
    q = nq * (qc * gq[0:1] * cos_c + _rotate_half(qc, HEAD_DIM) * gq[1:2] * sin_c)
    qc_ref[...] = (q * (HEAD_DIM ** -0.5 * LOG2E)).astype(BF16)
    k = nk * (kc * gk[0:1] * cos_c1 + _rotate_half(kc, HEAD_DIM) * gk[1:2] * sin_c1)
    kc_ref[...] = k.astype(BF16)
    vc_ref[...] = pc[:, W_C + W_KC:].astype(BF16)


def _token_specs(d, n_lat_tiles, ctx_blk):
    return [pl.BlockSpec((None, TILE, d), lambda bi, ti: (bi, jnp.minimum(ti, n_lat_tiles - 1), 0)),
            pl.BlockSpec((None, TILE, d), lambda bi, ti: (bi, ctx_blk, 0))]


def _in_projection(x_lat, x_ctx, ctx_blk, modsel, gain, w_ext, tab, gq, gk, ones, n_lat_tiles):
    b, _, d = x_lat.shape
    n_tiles = n_lat_tiles + 1
    t_all = n_tiles * TILE
    widths = (W_A, W_A, W_A, W_B, W_B, W_B, W_C, W_KC, W_KC)
    tok = lambda bi, ti: (bi, ti, 0)
    const2 = lambda bi, ti: (0, 0)
    return pl.pallas_call(
        functools.partial(_inproj_kernel, n_lat_tiles=n_lat_tiles),
        out_shape=[jax.ShapeDtypeStruct((b, t_all, w), BF16) for w in widths],
        grid=(b, n_tiles),
        in_specs=_token_specs(d, n_lat_tiles, ctx_blk) + [
                  pl.BlockSpec((None, 6, d), lambda bi, ti: (2 * bi + (ti >= n_lat_tiles).astype(jnp.int32), 0, 0)),
                  pl.BlockSpec((1, d), const2),
                  pl.BlockSpec((d, IN_WIDTH), const2),
                  pl.BlockSpec((TILE, 4 * LANES), lambda bi, ti: (ti, 0)),
                  pl.BlockSpec((2, W_C), const2),
                  pl.BlockSpec((2, W_KC), const2),
                  pl.BlockSpec((W_C, W_C), const2)],
        out_specs=[pl.BlockSpec((None, TILE, w), tok) for w in widths],
        compiler_params=_cparams(2),
        name="in_projection",
    )(x_lat, x_ctx, modsel, gain, w_ext, tab, gq, gk, ones)


def _flash_kernel(q_ref, k_ref, v_ref, aux_ref, o_ref, va_ref, vb_ref, qs_ref, acc_ref, m_ref,
                  s0_ref, s1_ref, mb0_ref, mb1_ref, *,
                  n_qblk, n_sub, tk, n_lat_blocks, ctx_tile, qt_off, mode, lam_init):
    qt = pl.program_id(2) + qt_off
    sub_w = LANES // n_sub
    half = LANES // 2
    lane = lax.broadcasted_iota(jnp.int32, (1, LANES), 1)
    lower = lane < half
    n_pieces = n_qblk * n_sub
    ma = (n_pieces // 2) * TILE
    m_rows = n_pieces * TILE
    ctx_start = n_lat_blocks * tk

    @pl.when(pl.program_id(2) == 0)
    def _():
        v = v_ref[...].astype(F32)
        va_ref[...] = jnp.where(lower, v, 1.0).astype(BF16)
        vb_ref[...] = jnp.where(lower, 1.0, v).astype(BF16)

    ia, ib = 0, n_pieces // 2
    for blk in range(n_qblk):
        qf = q_ref[:, blk * LANES:(blk + 1) * LANES].astype(F32)
        for sub in range(n_sub):
            msk = (lane >= sub * sub_w) & (lane < (sub + 1) * sub_w)
            piece = jnp.where(msk, qf, 0.0).astype(BF16)
            if sub * sub_w < half:
                qs_ref[ia * TILE:(ia + 1) * TILE, :] = piece
                ia += 1
            else:
                qs_ref[ib * TILE:(ib + 1) * TILE, :] = piece
                ib += 1

    s_bufs = (s0_ref, s1_ref)
    mb_bufs = (mb0_ref, mb1_ref)

    def scores(start, size, slot):
        s = _dot_nt(qs_ref[...], k_ref[pl.ds(start, size), :])
        s_bufs[slot][:, :size] = s
        mb = jnp.max(s, axis=-1, keepdims=True)
        mb_bufs[slot][...] = jnp.broadcast_to(mb, (m_rows, LANES))

    def accumulate(start, size, slot, first):
        mb = mb_bufs[slot][...]
        if first:
            m_new = mb
        else:
            m_old = m_ref[...]
            m_new = jnp.maximum(m_old, mb)
        s_ref = s_bufs[slot]
        cols = [s_ref[:, c * LANES:(c + 1) * LANES] - m_new for c in range(size // LANES)]
        p = jnp.concatenate([jnp.exp2(d.astype(BF16)) for d in cols], axis=1)
        pva = _dot(p[:ma], va_ref[pl.ds(start, size), :])
        pvb = _dot(p[ma:], vb_ref[pl.ds(start, size), :])
        if first:
            acc_ref[:ma, :] = pva
            acc_ref[ma:, :] = pvb
        else:
            alpha = jnp.exp2(m_old - m_new)
            acc_ref[:ma, :] = alpha[:ma] * acc_ref[:ma, :] + pva
            acc_ref[ma:, :] = alpha[ma:] * acc_ref[ma:, :] + pvb
        m_ref[...] = m_new

    def lat(j):
        return pl.multiple_of(j * tk, tk)

    def latent_queries():
        scores(ctx_start, CTX_LEN, 0)
        scores(lat(0), tk, 1)
        accumulate(ctx_start, CTX_LEN, 0, True)

        def pair(i):
            scores(lat(2 * i + 1), tk, 0)
            accumulate(lat(2 * i), tk, 1, False)
            scores(lat(2 * i + 2), tk, 1)
            accumulate(lat(2 * i + 1), tk, 0, False)

        def body(i, carry):
            for u in range(PAIRS_PER_STEP):
                pair(i * PAIRS_PER_STEP + u)
            return carry

        n_pairs = (n_lat_blocks - 2) // 2
        n_steps = n_pairs // PAIRS_PER_STEP
        lax.fori_loop(0, n_steps, body, 0)
        for i in range(n_steps * PAIRS_PER_STEP, n_pairs):
            pair(i)
        scores(lat(n_lat_blocks - 1), tk, 0)
        accumulate(lat(n_lat_blocks - 2), tk, 1, False)
        accumulate(lat(n_lat_blocks - 1), tk, 0, False)

    def context_queries():
        scores(ctx_start, CTX_LEN, 0)
        accumulate(ctx_start, CTX_LEN, 0, True)

    if ctx_tile is None:
        latent_queries()
    else:
        pl.when(qt != ctx_tile)(latent_queries)
        pl.when(qt == ctx_tile)(context_queries)

    acc = acc_ref[...]
    r = acc / pltpu.roll(acc, half, 1)
    ra, rb = r[:ma], r[ma:]
    if mode == "plain":
        for i in range(n_pieces // 2):
            o = jnp.where(lower, ra[i * TILE:(i + 1) * TILE], rb[i * TILE:(i + 1) * TILE])
            o_ref[:, i * LANES:(i + 1) * LANES] = o.astype(BF16)
    else:
        aux = aux_ref[...]
        l1 = jnp.sum(aux[0:1] * aux[1:2], axis=-1, keepdims=True)
        l2 = jnp.sum(aux[2:3] * aux[3:4], axis=-1, keepdims=True)
        lam = jnp.exp(l1) - jnp.exp(l2) + lam_init
        oa = ra[:TILE] - lam * ra[TILE:]
        ob = rb[:TILE] - lam * rb[TILE:]
        o = jnp.where(lower, oa, ob)
        sq = o * o
        ss_a = jnp.sum(jnp.where(lower, sq, 0.0), axis=-1, keepdims=True)
        ss_b = jnp.sum(jnp.where(lower, 0.0, sq), axis=-1, keepdims=True)
        ms = jnp.where(lower, ss_a, ss_b) * (1.0 / HEAD_DIM)
        o = (o * lax.rsqrt(ms + EPS)) * aux[4:5]
        o_ref[...] = (o * (1.0 - lam_init)).astype(BF16)


def _flash(q, k, v, aux, *, n_qblk, n_sub, n_hp, qt_off, n_qt, n_lat, ctx_tile, mode, lam_init=0.0):
    b, t_all, _ = q.shape
    qw = n_qblk * LANES
    tk = FLASH_TK if n_lat % (2 * FLASH_TK) == 0 else 512
    assert n_lat % (2 * tk) == 0 and tk >= CTX_LEN
    m_rows = n_qblk * n_sub * TILE
    kern = functools.partial(_flash_kernel, n_qblk=n_qblk, n_sub=n_sub, tk=tk,
                             n_lat_blocks=n_lat // tk, ctx_tile=ctx_tile, qt_off=qt_off,
                             mode=mode, lam_init=lam_init)
    return pl.pallas_call(
        kern,
        out_shape=jax.ShapeDtypeStruct((b, n_qt * TILE, n_hp * qw), BF16),
        grid=(b, n_hp, n_qt),
        in_specs=[pl.BlockSpec((None, TILE, qw), lambda bi, hp, qt: (bi, qt + qt_off, hp)),
                  pl.BlockSpec((None, t_all, LANES), lambda bi, hp, qt: (bi, 0, hp)),
                  pl.BlockSpec((None, t_all, LANES), lambda bi, hp, qt: (bi, 0, hp)),
                  pl.BlockSpec((8, LANES), lambda bi, hp, qt: (0, 0))],
        out_specs=pl.BlockSpec((None, TILE, qw), lambda bi, hp, qt: (bi, qt, hp)),
        scratch_shapes=[pltpu.VMEM((t_all, LANES), BF16),
                        pltpu.VMEM((t_all, LANES), BF16),
                        pltpu.VMEM((m_rows, LANES), BF16),
                        pltpu.VMEM((m_rows, LANES), F32),
                        pltpu.VMEM((m_rows, LANES), F32),
                        pltpu.VMEM((m_rows, tk), F32),
                        pltpu.VMEM((m_rows, tk), F32),
                        pltpu.VMEM((m_rows, LANES), F32),
                        pltpu.VMEM((m_rows, LANES), F32)],
        compiler_params=_cparams(3),
        name="flash_" + mode,
    )(q, k, v, aux)


def _na_kernel(q_ref, k0, k1, k2, k3, v0, v1, v2, v3, kc_ref, vc_ref, bias_ref, o_ref):
    lane = lax.broadcasted_iota(jnp.int32, (1, LANES), 1)
    lower = lane < LANES // 2
    qf = q_ref[...].astype(F32)
    kw = jnp.concatenate([k0[...], k1[...], k2[...], k3[...]], axis=0)
    vw = jnp.concatenate([v0[...], v1[...], v2[...], v3[...]], axis=0)
    kc = kc_ref[...]
    vwf = vw.astype(F32)
    vcf = vc_ref[...].astype(F32)
    outs = []
    for hh in range(2):
        msk = lower if hh == 0 else jnp.logical_not(lower)
        qh = jnp.where(msk, qf, 0.0).astype(BF16)
        vw_h = jnp.where(msk, vwf, 1.0).astype(BF16)
        vc_h = jnp.where(msk, vcf, 1.0).astype(BF16)
        n_pair = NA_KROWS // 2
        bias = jnp.concatenate(
            [jnp.concatenate([bias_ref[hh, a * n_pair + j] for j in range(n_pair)], axis=1)
             for a in range(NA_QROWS)], axis=0)
        s_w = _dot_nt(qh, kw) + bias
        s_c = _dot_nt(qh, kc)
        m = jnp.maximum(jnp.max(s_w, axis=-1, keepdims=True), jnp.max(s_c, axis=-1, keepdims=True))
        p_w = jnp.exp2((s_w - m).astype(BF16))
        p_c = jnp.exp2((s_c - m).astype(BF16))
        o = _dot(p_w, vw_h) + _dot(p_c, vc_h)
        outs.append(o / pltpu.roll(o, LANES // 2, 1))
    o_ref[...] = jnp.where(lower, outs[0], outs[1]).astype(BF16)


def _neighbourhood_attention(qa, ka, va, bias, n_lat):
    b = qa.shape[0]
    q_tok = NA_QROWS * GRID_W
    v_tok = q_tok // 2
    n_rb = n_lat // q_tok
    n_view = n_lat // v_tok
    ctx_blk = n_lat // v_tok

    def view(j):
        return lambda rb, hp, bi: (bi, jnp.clip(2 * rb - 1 + j, 0, n_view - 1), hp)

    kv_specs = [pl.BlockSpec((None, v_tok, LANES), view(j)) for j in range(4)]
    ctx_spec = pl.BlockSpec((None, CTX_LEN, LANES), lambda rb, hp, bi: (bi, ctx_blk, hp))

    def bias_map(rb, hp, bi):
        pat = jnp.where(rb == 0, 0, jnp.where(rb == n_rb - 1, 2, 1))
        return (hp, pat, 0, 0, 0)

    return pl.pallas_call(
        _na_kernel,
        out_shape=jax.ShapeDtypeStruct((b, n_lat, W_A), BF16),
        grid=(n_rb, NA_HEADS // 2, b),
        in_specs=[pl.BlockSpec((None, q_tok, LANES), lambda rb, hp, bi: (bi, rb, hp))]
                 + kv_specs + kv_specs + [ctx_spec, ctx_spec,
                 pl.BlockSpec((2, None, NA_QROWS * NA_KROWS // 2, GRID_W, 2 * GRID_W), bias_map)],
        out_specs=pl.BlockSpec((None, q_tok, LANES), lambda rb, hp, bi: (bi, rb, hp)),
        compiler_params=_cparams(3),
        name="neighbourhood_attention",
    )(qa, ka, ka, ka, ka, va, va, va, va, ka, va, bias)


def _na_bias_table(rpb, rows):
    cols = np.arange(GRID_W)
    c0 = np.clip(cols - NA_WIN_W // 2, 0, GRID_W - NA_WIN_W)
    cc = cols[None, :]
    col_ok = (cc >= c0[:, None]) & (cc < c0[:, None] + NA_WIN_W)
    dc = np.clip(cc - cols[:, None] + (NA_WIN_W - 1), 0, 2 * NA_WIN_W - 2)
    e = jnp.where(col_ok[None, None], (rpb.astype(F32) * LOG2E)[:, :, dc], NEG)
    e = jnp.concatenate([e, jnp.full_like(e[:, :1], NEG)], axis=1)
    a = np.arange(NA_QROWS)[:, None]
    i = np.arange(NA_KROWS)[None, :]
    pats = []
    for r_base in (0, NA_QROWS, rows - NA_QROWS):
        r = r_base + a
        key_row = r_base - NA_WIN_H // 2 + i
        r0 = np.clip(r - NA_WIN_H // 2, 0, rows - NA_WIN_H)
        ok = (key_row >= r0) & (key_row < r0 + NA_WIN_H) & (key_row >= 0) & (key_row < rows)
        dr = np.where(ok, key_row - r + (NA_WIN_H - 1), 2 * NA_WIN_H - 1)
        pats.append(dr)
    dr_all = np.stack(pats)
    pairs = dr_all.reshape(-1, 2)
    uniq, inv = np.unique(pairs, axis=0, return_inverse=True)
    pair_blocks = jnp.concatenate([e[:, uniq[:, 0]], e[:, uniq[:, 1]]], axis=-1)
    t = pair_blocks[:, inv.reshape(-1)]
    return t.reshape(NA_HEADS, 3, NA_QROWS * NA_KROWS // 2, GRID_W, 2 * GRID_W)


def _outproj_kernel(x_ref, xc_ref, oa_ref, ob_ref, oc_ref, wa_ref, wb_ref, wc_ref, mod_ref, gain_ref,
                    wrh_ref, wrl_ref, br_ref, tri_ref, x1_ref, tok_ref, route_ref, cnt_ref, run_ref,
                    *, region, group_batches, n_lat_tiles):
    mod = mod_ref[...]
    y = _dot(oa_ref[...], wa_ref[...]) + _dot(ob_ref[...], wb_ref[...]) + _dot(oc_ref[...], wc_ref[...])
    x1 = jnp.where(pl.program_id(1) == n_lat_tiles, xc_ref[...], x_ref[...]) + mod[2:3] * y
    x1_ref[...] = x1
    ms = jnp.mean(x1 * x1, axis=-1, keepdims=True)
    t = (x1 * lax.rsqrt(ms + EPS)) * gain_ref[...]
    t = t * (1.0 + mod[4:5]) + mod[3:4]
    bits = lax.bitcast_convert_type(t.astype(BF16).astype(F32), jnp.int32)
    half_d = bits.shape[1] // 2
    tok_ref[...] = lax.shift_right_logical(bits[:, :half_d], 16) | (bits[:, half_d:] & HI16)

    t_hi, t_lo = _split_bf16(t)
    wrh = wrh_ref[...]
    logits = _dot(t_hi, wrh) + _dot(t_lo, wrh) + _dot(t_hi, wrl_ref[...]) + br_ref[...]

    lane = lax.broadcasted_iota(jnp.int32, logits.shape, 1)
    lane_f = lane.astype(F32)
    is_g = lane < N_GROUPS
    gl = jnp.where(is_g, logits, NEG)
    gmax = jnp.max(gl, axis=-1, keepdims=True)
    g_sel = jnp.min(jnp.where(gl == gmax, lane_f, 1e9), axis=-1, keepdims=True)
    p_grp = 1.0 / jnp.sum(jnp.where(is_g, jnp.exp(gl - gmax), 0.0), axis=-1, keepdims=True)
    grp_of_lane = lax.shift_right_arithmetic(lane - N_GROUPS, 2).astype(F32)
    in_grp = (lane >= N_GROUPS) & (lane < N_GROUPS + N_EXPERTS) & (grp_of_lane == g_sel)
    el = jnp.where(in_grp, logits, NEG)
    v1 = jnp.max(el, axis=-1, keepdims=True)
    i1 = jnp.min(jnp.where(el == v1, lane_f, 1e9), axis=-1, keepdims=True)
    el2 = jnp.where(lane_f == i1, NEG, el)
    v2 = jnp.max(el2, axis=-1, keepdims=True)
    i2 = jnp.min(jnp.where(el2 == v2, lane_f, 1e9), axis=-1, keepdims=True)
    e2 = jnp.exp(v2 - v1)
    den = 1.0 + e2
    w1 = p_grp / den
    w2 = p_grp * e2 / den

    @pl.when((lax.rem(pl.program_id(0), group_batches) == 0) & (pl.program_id(1) == 0))
    def _():
        run_ref[...] = jnp.zeros(run_ref.shape, F32)

    ind = jnp.where(lane_f == i1, 1.0, 0.0) + jnp.where(lane_f == i2, 1.0, 0.0)
    rank = _dot(tri_ref[...], ind.astype(BF16)) + run_ref[0:1, :]

    def pick(m, l):
        return jnp.sum(jnp.where(lane_f == l, m, 0.0), axis=-1, keepdims=True)

    pos1 = (i1 - N_GROUPS) * region + pick(rank, i1)
    pos2 = (i2 - N_GROUPS) * region + pick(rank, i2)
    route_ref[...] = jnp.where(lane == 0, pos1, jnp.where(lane == 1, pos2,
                               jnp.where(lane == 2, w1, jnp.where(lane == 3, w2, 0.0))))
    run = run_ref[...] + jnp.sum(ind, axis=0, keepdims=True)
    run_ref[...] = run
    cnt_ref[...] = run


def _out_projection(x_lat, x_ctx, ctx_blk, oa, ob, oc, wa, wb, wc, modsel, gain, wrh, wrl, br, n_tiles,
                    n_lat_tiles):
    b, _, d = x_lat.shape
    tok = lambda bi, ti: (bi, ti, 0)
    const2 = lambda bi, ti: (0, 0)
    rows = n_tiles * TILE
    nb = b // MOE_GROUPS if b % MOE_GROUPS == 0 else b
    tri = jnp.asarray(np.tril(np.ones((TILE, TILE), np.float32), -1), BF16)
    return pl.pallas_call(
        functools.partial(_outproj_kernel, region=nb * rows, group_batches=nb, n_lat_tiles=n_lat_tiles),
        out_shape=[jax.ShapeDtypeStruct((b, rows, d), F32),
                   jax.ShapeDtypeStruct((b, rows, d // 2), jnp.int32),
                   jax.ShapeDtypeStruct((b, rows, LANES), F32),
                   jax.ShapeDtypeStruct((8 * (b // nb), LANES), F32)],
        grid=(b, n_tiles),
        in_specs=_token_specs(d, n_lat_tiles, ctx_blk) + [
                  pl.BlockSpec((None, TILE, W_A), tok),
                  pl.BlockSpec((None, TILE, W_B), tok),
                  pl.BlockSpec((None, TILE, W_C), tok),
                  pl.BlockSpec((W_A, d), const2),
                  pl.BlockSpec((W_B, d), const2),
                  pl.BlockSpec((W_C, d), const2),
                  pl.BlockSpec((None, 6, d), lambda bi, ti: (2 * bi + (ti >= n_lat_tiles).astype(jnp.int32), 0, 0)),
                  pl.BlockSpec((1, d), const2),
                  pl.BlockSpec((d, LANES), const2),
                  pl.BlockSpec((d, LANES), const2),
                  pl.BlockSpec((1, LANES), const2),
                  pl.BlockSpec((TILE, TILE), const2)],
        out_specs=[pl.BlockSpec((None, TILE, d), tok),
                   pl.BlockSpec((None, TILE, d // 2), tok),
                   pl.BlockSpec((None, TILE, LANES), tok),
                   pl.BlockSpec((8, LANES), lambda bi, ti: (bi // nb, 0))],
        scratch_shapes=[pltpu.VMEM((8, LANES), F32)],
        compiler_params=_cparams(2),
        name="out_projection",
    )(x_lat, x_ctx, oa, ob, oc, wa, wb, wc, modsel, gain, wrh, wrl, br, tri)


def _sc_mesh():
    return plsc.VectorSubcoreMesh(core_axis_name="core", subcore_axis_name="subcore")


def _sc_worker_base(per_worker):
    wid = lax.axis_index("subcore") * SC_CORES + lax.axis_index("core")
    return wid * per_worker


def _sc_scratch(d, dtype):
    return ([pltpu.VMEM((SC_ROWS,), jnp.int32)] * SC_BUFS + [pltpu.VMEM((SC_ROWS, d), dtype)] * SC_BUFS
            + [pltpu.SemaphoreType.DMA] * (2 * SC_BUFS))


def _sc_split(scratch):
    return (scratch[:SC_BUFS], scratch[SC_BUFS:2 * SC_BUFS], scratch[2 * SC_BUFS:3 * SC_BUFS],
            scratch[3 * SC_BUFS:])


def _sc_chunk_loop(per_worker, group):
    chunks = per_worker // SC_ROWS
    full = chunks // SC_BUFS * SC_BUFS

    @pl.loop(0, full, step=SC_BUFS)
    def _(c):
        group(c, SC_BUFS)

    if chunks > full:
        group(full, chunks - full)


def _sc_scatter_rows(x, row_off, n, idx, n_out):
    d = x.shape[1]
    per_worker = 2 * n // (SC_CORES * SC_SUBCORES)
    assert per_worker % SC_ROWS == 0 and n % SC_ROWS == 0

    @functools.partial(pl.kernel, out_type=jax.ShapeDtypeStruct((n_out, d), x.dtype),
                       mesh=_sc_mesh(), scratch_types=_sc_scratch(d, x.dtype))
    def scatter(x_hbm, i_hbm, o_hbm, *scratch):
        idx_v, rows_v, sem_in, sem_out = _sc_split(scratch)
        base = _sc_worker_base(per_worker)

        def group(c, n_bufs):
            reads = []
            for u in range(n_bufs):
                a = pl.multiple_of(base + (c + u) * SC_ROWS, SC_ROWS)
                t = pl.multiple_of(row_off + lax.rem(a, n), SC_ROWS)
                pltpu.sync_copy(i_hbm.at[pl.ds(a, SC_ROWS)], idx_v[u])
                reads.append(pltpu.async_copy(x_hbm.at[pl.ds(t, SC_ROWS)], rows_v[u], sem_in[u]))
            writes = []
            for u in range(n_bufs):
                reads[u].wait()
                writes.append(pltpu.async_copy(rows_v[u], o_hbm.at[idx_v[u]], sem_out[u]))
            for w in writes:
                w.wait()

        _sc_chunk_loop(per_worker, group)

    return scatter(x, idx)


def _sc_gather_rows(src, idx):
    m = idx.shape[0]
    d = src.shape[1]
    per_worker = m // (SC_CORES * SC_SUBCORES)
    assert per_worker % SC_ROWS == 0

    @functools.partial(pl.kernel, out_type=jax.ShapeDtypeStruct((m, d), src.dtype),
                       mesh=_sc_mesh(), scratch_types=_sc_scratch(d, src.dtype))
    def gather(s_hbm, i_hbm, o_hbm, *scratch):
        idx_v, rows_v, sem_in, sem_out = _sc_split(scratch)
        base = _sc_worker_base(per_worker)

        def group(c, n_bufs):
            offs, reads = [], []
            for u in range(n_bufs):
                a = pl.multiple_of(base + (c + u) * SC_ROWS, SC_ROWS)
                offs.append(a)
                pltpu.sync_copy(i_hbm.at[pl.ds(a, SC_ROWS)], idx_v[u])
                reads.append(pltpu.async_copy(s_hbm.at[idx_v[u]], rows_v[u], sem_in[u]))
            writes = []
            for u in range(n_bufs):
                reads[u].wait()
                writes.append(pltpu.async_copy(rows_v[u], o_hbm.at[pl.ds(offs[u], SC_ROWS)], sem_out[u]))
            for w in writes:
                w.wait()

        _sc_chunk_loop(per_worker, group)

    return gather(src, idx)


def _expert_ffn_kernel(blk_ref, exp_ref, x_ref, wg_ref, wu_ref, wd_ref, y_ref, wgb_ref, wub_ref, wdb_ref):
    j = pl.program_id(0)

    @pl.when((j == 0) | (exp_ref[j] != exp_ref[jnp.maximum(j - 1, 0)]))
    def _():
        wgb_ref[...] = wg_ref[...].astype(BF16)
        wub_ref[...] = wu_ref[...].astype(BF16)
        wdb_ref[...] = wd_ref[...].astype(BF16)

    w = x_ref[...]
    x = jnp.concatenate([lax.bitcast_convert_type(lax.shift_left(w, 16), F32),
                         lax.bitcast_convert_type(w & HI16, F32)], axis=1).astype(BF16)
    hid = jax.nn.silu(_dot(x, wgb_ref[...])) * _dot(x, wub_ref[...])
    y_ref[...] = _dot(hid.astype(BF16), wdb_ref[...])


def _expert_ffn(xs, blk, exp, wg, wu, wd, layer):
    rows, d_packed = xs.shape
    d = 2 * d_packed
    w_map = lambda j, blk, exp: (layer, exp[j], 0, 0)
    return pl.pallas_call(
        _expert_ffn_kernel,
        out_shape=jax.ShapeDtypeStruct((rows, d), F32),
        grid_spec=pltpu.PrefetchScalarGridSpec(
            num_scalar_prefetch=2,
            grid=(blk.shape[0],),
            in_specs=[pl.BlockSpec((MOE_TILE, d_packed), lambda j, blk, exp: (blk[j], 0)),
                      pl.BlockSpec((None, None, d, EXPERT_HIDDEN), w_map),
                      pl.BlockSpec((None, None, d, EXPERT_HIDDEN), w_map),
                      pl.BlockSpec((None, None, EXPERT_HIDDEN, d), w_map)],
            out_specs=pl.BlockSpec((MOE_TILE, d), lambda j, blk, exp: (blk[j], 0)),
            scratch_shapes=[pltpu.VMEM((d, EXPERT_HIDDEN), BF16),
                            pltpu.VMEM((d, EXPERT_HIDDEN), BF16),
                            pltpu.VMEM((EXPERT_HIDDEN, d), BF16)]),
        compiler_params=_cparams(1),
        name="expert_ffn",
    )(blk, exp, xs, wg, wu, wd)


def _combine_kernel(x1_ref, y1_ref, y2_ref, route_ref, mod_ref, fgain_ref, *rest, final):
    o_ref = rest[-1]
    route = route_ref[...]
    y = route[:, 2:3] * y1_ref[...] + route[:, 3:4] * y2_ref[...]
    x2 = x1_ref[...] + mod_ref[5:6, :] * y
    if final:
        ms = jnp.mean(x2 * x2, axis=-1, keepdims=True)
        x2 = (x2 * lax.rsqrt(ms + EPS)) * fgain_ref[...]
    o_ref[...] = x2


def _combine(x1, ys, route, modsel, fgain, prev, b0, nb, n_lat_tiles, final):
    b, rows, d = x1.shape
    n_t = rows // TILE
    tok = lambda bi, ti: (b0 + bi, ti, 0)
    in_specs = [pl.BlockSpec((None, TILE, d), tok),
                pl.BlockSpec((TILE, d), lambda bi, ti: (bi * n_t + ti, 0)),
                pl.BlockSpec((TILE, d), lambda bi, ti: ((nb + bi) * n_t + ti, 0)),
                pl.BlockSpec((None, TILE, LANES), tok),
                pl.BlockSpec((None, 6, d),
                             lambda bi, ti: (2 * (b0 + bi) + (ti >= n_lat_tiles).astype(jnp.int32), 0, 0)),
                pl.BlockSpec((1, d), lambda bi, ti: (0, 0))]
    args = [x1, ys, ys, route, modsel, fgain]
    aliases = {}
    if prev is not None:
        in_specs.append(pl.BlockSpec(memory_space=pl.ANY))
        args.append(prev)
        aliases = {len(args) - 1: 0}
    return pl.pallas_call(
        functools.partial(_combine_kernel, final=final),
        out_shape=jax.ShapeDtypeStruct((b, rows, d), F32),
        grid=(nb, n_t),
        in_specs=in_specs,
        out_specs=pl.BlockSpec((None, TILE, d), tok),
        input_output_aliases=aliases,
        compiler_params=_cparams(2),
        name="moe_combine",
    )(*args)


def _routed_moe(tok, route, cnt, x1, wg, wu, wd, layer, modsel, fgain, n_lat_tiles, final):
    b, rows, d = x1.shape
    n_groups = cnt.shape[0] // 8
    nb = b // n_groups
    n = nb * rows
    flat = route.reshape(b * rows, LANES)
    tok_flat = tok.reshape(b * rows, tok.shape[2])
    out = None
    for g in range(n_groups):
        part = flat[g * n:(g + 1) * n]
        idx = jnp.concatenate([part[:, 0], part[:, 1]]).astype(jnp.int32)
        xs = _sc_scatter_rows(tok_flat, g * n, n, idx, N_EXPERTS * n)

        counts = cnt[8 * g, N_GROUPS:N_GROUPS + N_EXPERTS].astype(jnp.int32)
        tiles = (counts + MOE_TILE - 1) // MOE_TILE
        ends = jnp.cumsum(tiles)
        n_sched = 2 * n // MOE_TILE + N_EXPERTS
        j = jnp.minimum(jnp.arange(n_sched, dtype=jnp.int32), ends[-1] - 1)
        exp = jnp.sum((j[:, None] >= ends[None, :]).astype(jnp.int32), axis=1)
        blk = exp * (n // MOE_TILE) + j - (ends - tiles)[exp]

        ys = _expert_ffn(xs, blk, exp, wg, wu, wd, layer)
        yg = _sc_gather_rows(ys, idx)
        out = _combine(x1, yg, route, modsel, fgain, out, g * nb, nb, n_lat_tiles, final)
    return out


def _rope_tables(n_lat):
    t = jnp.arange(n_lat)
    row = (t // GRID_W).astype(F32)
    col = (t % GRID_W).astype(F32)

    def cs(dim):
        quarter = dim // 4
        freqs = ROPE_THETA ** (-jnp.arange(quarter, dtype=F32) / quarter)
        ang = jnp.concatenate([row[:, None] * freqs, col[:, None] * freqs], axis=-1)
        cos = jnp.tile(jnp.cos(ang), (1, 2 * LANES // dim))
        sin = jnp.tile(jnp.sin(ang), (1, 2 * LANES // dim))
        cos = jnp.concatenate([cos, jnp.ones((CTX_LEN, LANES), F32)], axis=0)
        sin = jnp.concatenate([sin, jnp.zeros((CTX_LEN, LANES), F32)], axis=0)
        return cos, sin

    cos_b, sin_b = cs(DIFF_QK_DIM)
    cos_c, sin_c = cs(HEAD_DIM)
    return jnp.concatenate([cos_b, sin_b, cos_c, sin_c], axis=1)


def _reordered_w_in(w_in):
    o_c = 3 * W_A + 3 * W_B
    heads = [w_in[:, o_c + h * HEAD_DIM:o_c + (h + 1) * HEAD_DIM] for h in GQA_Q_ORDER]
    return jnp.concatenate([w_in[:, :o_c]] + heads + [w_in[:, o_c + W_C:]], axis=1).astype(BF16)


def kernel(x, c, ctx, c_ctx, w_mod, b_mod, norm_attn, norm_ffn, w_in, w_out, na_rpb, diff_lambda_q1, diff_lambda_k1, diff_lambda_q2, diff_lambda_k2, diff_subln, gqa_q_norm, gqa_k_norm, router_group_w, router_group_b, router_expert_w, router_expert_b, w_gate, w_up, w_down, final_norm):
    b, s, d = x.shape
    assert d == D_MODEL and ctx.shape[1] == CTX_LEN and s % (NA_QROWS * GRID_W) == 0
    rows = s // GRID_W
    assert rows >= 2 * NA_QROWS
    t_all = s + CTX_LEN
    n_lat_tiles = s // TILE

    c_rows = jnp.zeros((8, d), F32).at[:b].set(c).at[b].set(c_ctx)
    mod = _modulation(c_rows, w_mod, b_mod)

    tab = _rope_tables(s)
    hidx = np.arange(HEAD_DIM)
    partner = np.where(hidx < HEAD_DIM // 2, hidx + HEAD_DIM // 2, hidx - HEAD_DIM // 2)
    blk = np.arange(W_C) // HEAD_DIM
    ones = jnp.asarray((blk[:, None] == blk[None, :]).astype(np.float32), BF16)
    dummy_aux = jnp.zeros((8, LANES), F32)

    x_lat, x_ctx, ctx_blk = x, ctx, 0
    for l in range(DEPTH):
        ctx_out = l < DEPTH - 1
        lam_init = 0.8 - 0.6 * math.exp(-0.3 * l)
        m_lat = mod[l, :b].reshape(b, 1, 6, d)
        m_ctx = jnp.broadcast_to(mod[l, b].reshape(1, 1, 6, d), (b, 1, 6, d))
        modsel = jnp.concatenate([m_lat, m_ctx], axis=1).reshape(2 * b, 6, d)

        gq = jnp.stack([jnp.tile(gqa_q_norm[l], GQA_Q_HEADS), jnp.tile(gqa_q_norm[l][partner], GQA_Q_HEADS)])
        gk = jnp.stack([jnp.tile(gqa_k_norm[l], GQA_KV_HEADS), jnp.tile(gqa_k_norm[l][partner], GQA_KV_HEADS)])
        qa, ka, va, qb, kb, vb, qc, kc, vc = _in_projection(
            x_lat, x_ctx, ctx_blk, modsel, norm_attn[l][None], _reordered_w_in(w_in[l]), tab, gq, gk, ones,
            n_lat_tiles)

        n_qt = n_lat_tiles + 1 if ctx_out else n_lat_tiles
        ctx_tile = n_lat_tiles if ctx_out else None
        oa = _neighbourhood_attention(qa, ka, va, _na_bias_table(na_rpb[l], rows), s)
        if ctx_out:
            oa_ctx = _flash(qa, ka, va, dummy_aux, n_qblk=1, n_sub=2, n_hp=NA_HEADS // 2,
                            qt_off=n_lat_tiles, n_qt=1, n_lat=s, ctx_tile=n_lat_tiles,
                            mode="plain")
            oa = jnp.concatenate([oa, oa_ctx], axis=1)
        pad = lambda v: jnp.pad(v, (0, LANES - v.shape[0]))
        aux = jnp.stack([pad(diff_lambda_q1[l]), pad(diff_lambda_k1[l]), pad(diff_lambda_q2[l]),
                         pad(diff_lambda_k2[l]), jnp.tile(diff_subln[l], 2),
                         jnp.zeros((LANES,), F32), jnp.zeros((LANES,), F32), jnp.zeros((LANES,), F32)])
        ob = _flash(qb, kb, vb, aux, n_qblk=1, n_sub=4, n_hp=DIFF_HEADS // 2, qt_off=0, n_qt=n_qt,
                    n_lat=s, ctx_tile=ctx_tile, mode="diff", lam_init=lam_init)
        oc = _flash(qc, kc, vc, dummy_aux, n_qblk=3, n_sub=2, n_hp=1, qt_off=0, n_qt=n_qt,
                    n_lat=s, ctx_tile=ctx_tile, mode="plain")

        w_o = w_out[l]
        o_c = W_A + W_B
        w_oc = jnp.concatenate([w_o[o_c + h * HEAD_DIM:o_c + (h + 1) * HEAD_DIM] for h in GQA_Q_ORDER], axis=0)
        wr = jnp.zeros((d, LANES), F32)
        wr = wr.at[:, :N_GROUPS].set(router_group_w[l]).at[:, N_GROUPS:N_GROUPS + N_EXPERTS].set(router_expert_w[l])
        wrh, wrl = _split_bf16(wr)
        br = jnp.zeros((1, LANES), F32)
        br = br.at[0, :N_GROUPS].set(router_group_b[l]).at[0, N_GROUPS:N_GROUPS + N_EXPERTS].set(router_expert_b[l])
        x1, tok, route, cnt = _out_projection(
            x_lat, x_ctx, ctx_blk, oa, ob, oc, w_o[:W_A].astype(BF16), w_o[W_A:W_A + W_B].astype(BF16),
            w_oc.astype(BF16), modsel, norm_ffn[l][None], wrh, wrl, br, n_qt, n_lat_tiles)
        xs = _routed_moe(tok, route, cnt, x1, w_gate, w_up, w_down, l, modsel, final_norm[None],
                         n_lat_tiles, final=not ctx_out)
        x_lat, x_ctx, ctx_blk = xs, xs, n_lat_tiles
    return xs
```

```python
import functools
import math

import numpy as np
import jax
import jax.numpy as jnp
from jax import lax
from jax.experimental import pallas as pl
from jax.experimental.pallas import tpu as pltpu
from jax.experimental.pallas import tpu_sc as plsc

F32 = jnp.float32
BF16 = jnp.bfloat16

D_MODEL = 1024
DEPTH = 2
GRID_W = 64
CTX_LEN = 256
HEAD_DIM = 64
NA_HEADS = 6
NA_WIN_H = 8
NA_WIN_W = 16
DIFF_HEADS = 4
DIFF_QK_DIM = 32
GQA_Q_HEADS = 6
GQA_KV_HEADS = 2
N_GROUPS = 4
EXPERTS_PER_GROUP = 4
N_EXPERTS = 16
EXPERT_HIDDEN = 512
ROPE_THETA = 10000.0
EPS = 1e-6
W_A = NA_HEADS * HEAD_DIM
W_B = DIFF_HEADS * 2 * DIFF_QK_DIM
W_C = GQA_Q_HEADS * HEAD_DIM
W_KC = GQA_KV_HEADS * HEAD_DIM
IN_WIDTH = 3 * W_A + 3 * W_B + W_C + 2 * W_KC

LANES = 128
TILE = CTX_LEN
NA_QROWS = 8
NA_KROWS = 16
NA_PARTS = 2
NEG = -1e30
LOG2E = 1.4426950408889634
HI16 = -65536
VMEM_LIMIT = 56 * 1024 * 1024
FLASH_TK = 512
PAIRS_PER_STEP = 2
MOE_TILE = 512
MOE_GROUPS = 2
SC_ROWS = 16
SC_BUFS = 4
SC_CORES = 2
SC_SUBCORES = 16

GQA_Q_ORDER = (0, 3, 1, 4, 2, 5)


def _cparams(n_axes):
    return pltpu.CompilerParams(dimension_semantics=("arbitrary",) * n_axes,
                                vmem_limit_bytes=VMEM_LIMIT)


def _split_bf16(a):
    hi = a.astype(BF16)
    lo = (a - hi.astype(F32)).astype(BF16)
    return hi, lo


def _dot(a, b):
    return jnp.dot(a, b, preferred_element_type=F32)


def _dot_nt(a, b):
    return lax.dot_general(a, b, (((1,), (1,)), ((), ())), preferred_element_type=F32)


def _mod_kernel(c_ref, w_ref, b_ref, o_ref):
    c = c_ref[...]
    a = c * jax.nn.sigmoid(c)
    a_hi, a_lo = _split_bf16(a)
    w_hi, w_lo = _split_bf16(w_ref[...])
    o_ref[...] = _dot(a_hi, w_hi) + _dot(a_lo, w_hi) + _dot(a_hi, w_lo) + b_ref[...]


def _modulation(c_rows, w_mod, b_mod):
    depth, d, n = w_mod.shape
    bn = 1536
    return pl.pallas_call(
        _mod_kernel,
        out_shape=jax.ShapeDtypeStruct((depth, 8, n), F32),
        grid=(depth, n // bn),
        in_specs=[pl.BlockSpec((8, d), lambda l, j: (0, 0)),
                  pl.BlockSpec((None, d, bn), lambda l, j: (l, 0, j)),
                  pl.BlockSpec((None, 1, bn), lambda l, j: (l, 0, j))],
        out_specs=pl.BlockSpec((None, 8, bn), lambda l, j: (l, 0, j)),
        compiler_params=_cparams(2),
        name="adaln_mod",
    )(c_rows, w_mod, b_mod.reshape(depth, 1, n))


def _head_mean_sq(t, ones):
    hi, lo = _split_bf16(t * t)
    return (_dot(hi, ones) + _dot(lo, ones)) * (1.0 / HEAD_DIM)


def _rotate_half(p, head):
    w = p.shape[1]
    half = head // 2
    lane = lax.broadcasted_iota(jnp.int32, (1, w), 1)
    first = (lane & (head - 1)) < half
    from_right = pltpu.roll(p, w - half, 1)
    from_left = pltpu.roll(p, half, 1)
    return jnp.where(first, -from_right, from_left)


def _inproj_kernel(x_ref, xc_ref, mod_ref, gain_ref, w_ref, tab_ref, gq_ref, gk_ref, ones_ref,
                   qa_ref, ka_ref, va_ref, qb_ref, kb_ref, vb_ref, qc_ref, kc_ref, vc_ref,
                   *, n_lat_tiles):
    x = jnp.where(pl.program_id(1) == n_lat_tiles, xc_ref[...], x_ref[...])
    mod = mod_ref[...]
    ms = jnp.mean(x * x, axis=-1, keepdims=True)
    h = (x * lax.rsqrt(ms + EPS)) * gain_ref[...]
    h = h * (1.0 + mod[1:2]) + mod[0:1]
    hb = h.astype(BF16)

    def proj(a, b):
        return _dot(hb, w_ref[:, a:b])

    pa = proj(0, 3 * W_A)
    qa_ref[...] = (pa[:, :W_A] * (HEAD_DIM ** -0.5 * LOG2E)).astype(BF16)
    ka_ref[...] = pa[:, W_A:2 * W_A].astype(BF16)
    va_ref[...] = pa[:, 2 * W_A:].astype(BF16)

    tab = tab_ref[...]
    cos_b = jnp.concatenate([tab[:, 0:LANES]] * 2, axis=1)
    sin_b = jnp.concatenate([tab[:, LANES:2 * LANES]] * 2, axis=1)
    cos_c1 = tab[:, 2 * LANES:3 * LANES]
    sin_c1 = tab[:, 3 * LANES:4 * LANES]
    cos_c = jnp.concatenate([cos_c1] * 3, axis=1)
    sin_c = jnp.concatenate([sin_c1] * 3, axis=1)

    o_b = 3 * W_A
    pb = proj(o_b, o_b + 3 * W_B)
    qb = pb[:, :W_B]
    kb = pb[:, W_B:2 * W_B]
    qb = qb * cos_b + _rotate_half(qb, DIFF_QK_DIM) * sin_b
    qb_ref[...] = (qb * (DIFF_QK_DIM ** -0.5 * LOG2E)).astype(BF16)
    kb_ref[...] = (kb * cos_b + _rotate_half(kb, DIFF_QK_DIM) * sin_b).astype(BF16)
    vb_ref[...] = pb[:, 2 * W_B:].astype(BF16)

    o_c = o_b + 3 * W_B
    pc = proj(o_c, IN_WIDTH)
    ones = ones_ref[...]
    qc = pc[:, :W_C]
    kc = pc[:, W_C:W_C + W_KC]
    nq = lax.rsqrt(_head_mean_sq(qc, ones) + EPS)
    nk = lax.rsqrt(_head_mean_sq(kc, ones[:W_KC, :W_KC]) + EPS)
    gq = gq_ref[...]
    gk = gk_ref[...]
    q = nq * (qc * gq[0:1] * cos_c + _rotate_half(qc, HEAD_DIM) * gq[1:2] * sin_c)
    qc_ref[...] = (q * (HEAD_DIM ** -0.5 * LOG2E)).astype(BF16)
    k = nk * (kc * gk[0:1] * cos_c1 + _rotate_half(kc, HEAD_DIM) * gk[1:2] * sin_c1)
    kc_ref[...] = k.astype(BF16)
    vc_ref[...] = pc[:, W_C + W_KC:].astype(BF16)


def _token_specs(d, n_lat_tiles, ctx_blk):
    return [pl.BlockSpec((None, TILE, d), lambda bi, ti: (bi, jnp.minimum(ti, n_lat_tiles - 1), 0)),
            pl.BlockSpec((None, TILE, d), lambda bi, ti: (bi, ctx_blk, 0))]


def _in_projection(x_lat, x_ctx, ctx_blk, modsel, gain, w_ext, tab, gq, gk, ones, n_lat_tiles):
    b, _, d = x_lat.shape
    n_tiles = n_lat_tiles + 1
    t_all = n_tiles * TILE
    widths = (W_A, W_A, W_A, W_B, W_B, W_B, W_C, W_KC, W_KC)
    tok = lambda bi, ti: (bi, ti, 0)
    const2 = lambda bi, ti: (0, 0)
    return pl.pallas_call(
        functools.partial(_inproj_kernel, n_lat_tiles=n_lat_tiles),
        out_shape=[jax.ShapeDtypeStruct((b, t_all, w), BF16) for w in widths],
        grid=(b, n_tiles),
        in_specs=_token_specs(d, n_lat_tiles, ctx_blk) + [
                  pl.BlockSpec((None, 6, d), lambda bi, ti: (2 * bi + (ti >= n_lat_tiles).astype(jnp.int32), 0, 0)),
                  pl.BlockSpec((1, d), const2),
                  pl.BlockSpec((d, IN_WIDTH), const2),
                  pl.BlockSpec((TILE, 4 * LANES), lambda bi, ti: (ti, 0)),
                  pl.BlockSpec((2, W_C), const2),
                  pl.BlockSpec((2, W_KC), const2),
                  pl.BlockSpec((W_C, W_C), const2)],
        out_specs=[pl.BlockSpec((None, TILE, w), tok) for w in widths],
        compiler_params=_cparams(2),
        name="in_projection",
    )(x_lat, x_ctx, modsel, gain, w_ext, tab, gq, gk, ones)


def _flash_kernel(q_ref, k_ref, v_ref, aux_ref, o_ref, va_ref, vb_ref, qs_ref, acc_ref, m_ref,
                  s0_ref, s1_ref, mb0_ref, mb1_ref, *,
                  n_qblk, n_sub, tk, n_lat_blocks, ctx_tile, qt_off, mode, lam_init):
    qt = pl.program_id(2) + qt_off
    sub_w = LANES // n_sub
    half = LANES // 2
    lane = lax.broadcasted_iota(jnp.int32, (1, LANES), 1)
    lower = lane < half
    n_pieces = n_qblk * n_sub
    ma = (n_pieces // 2) * TILE
    m_rows = n_pieces * TILE
    ctx_start = n_lat_blocks * tk

    @pl.when(pl.program_id(2) == 0)
    def _():
        v = v_ref[...].astype(F32)
        va_ref[...] = jnp.where(lower, v, 1.0).astype(BF16)
        vb_ref[...] = jnp.where(lower, 1.0, v).astype(BF16)

    ia, ib = 0, n_pieces // 2
    for blk in range(n_qblk):
        qf = q_ref[:, blk * LANES:(blk + 1) * LANES].astype(F32)
        for sub in range(n_sub):
            msk = (lane >= sub * sub_w) & (lane < (sub + 1) * sub_w)
            piece = jnp.where(msk, qf, 0.0).astype(BF16)
            if sub * sub_w < half:
                qs_ref[ia * TILE:(ia + 1) * TILE, :] = piece
                ia += 1
            else:
                qs_ref[ib * TILE:(ib + 1) * TILE, :] = piece
                ib += 1

    s_bufs = (s0_ref, s1_ref)
    mb_bufs = (mb0_ref, mb1_ref)

    def scores(start, size, slot):
        s = _dot_nt(qs_ref[...], k_ref[pl.ds(start, size), :])
        s_bufs[slot][:, :size] = s
        mb = jnp.max(s, axis=-1, keepdims=True)
        mb_bufs[slot][...] = jnp.broadcast_to(mb, (m_rows, LANES))

    def accumulate(start, size, slot, first):
        mb = mb_bufs[slot][...]
        if first:
            m_new = mb
        else:
            m_old = m_ref[...]
            m_new = jnp.maximum(m_old, mb)
        s_ref = s_bufs[slot]
        cols = [s_ref[:, c * LANES:(c + 1) * LANES] - m_new for c in range(size // LANES)]
        p = jnp.concatenate([jnp.exp2(d.astype(BF16)) for d in cols], axis=1)
        pva = _dot(p[:ma], va_ref[pl.ds(start, size), :])
        pvb = _dot(p[ma:], vb_ref[pl.ds(start, size), :])
        if first:
            acc_ref[:ma, :] = pva
            acc_ref[ma:, :] = pvb
        else:
            alpha = jnp.exp2(m_old - m_new)
            acc_ref[:ma, :] = alpha[:ma] * acc_ref[:ma, :] + pva
            acc_ref[ma:, :] = alpha[ma:] * acc_ref[ma:, :] + pvb
        m_ref[...] = m_new

    def lat(j):
        return pl.multiple_of(j * tk, tk)

    def latent_queries():
        scores(ctx_start, CTX_LEN, 0)
        scores(lat(0), tk, 1)
        accumulate(ctx_start, CTX_LEN, 0, True)

        def pair(i):
            scores(lat(2 * i + 1), tk, 0)
            accumulate(lat(2 * i), tk, 1, False)
            scores(lat(2 * i + 2), tk, 1)
            accumulate(lat(2 * i + 1), tk, 0, False)

        def body(i, carry):
            for u in range(PAIRS_PER_STEP):
                pair(i * PAIRS_PER_STEP + u)
            return carry

        n_pairs = (n_lat_blocks - 2) // 2
        n_steps = n_pairs // PAIRS_PER_STEP
        lax.fori_loop(0, n_steps, body, 0)
        for i in range(n_steps * PAIRS_PER_STEP, n_pairs):
            pair(i)
        scores(lat(n_lat_blocks - 1), tk, 0)
        accumulate(lat(n_lat_blocks - 2), tk, 1, False)
        accumulate(lat(n_lat_blocks - 1), tk, 0, False)

    def context_queries():
        scores(ctx_start, CTX_LEN, 0)
        accumulate(ctx_start, CTX_LEN, 0, True)

    if ctx_tile is None:
        latent_queries()
    else:
        pl.when(qt != ctx_tile)(latent_queries)
        pl.when(qt == ctx_tile)(context_queries)

    acc = acc_ref[...]
    r = acc / pltpu.roll(acc, half, 1)
    ra, rb = r[:ma], r[ma:]
    if mode == "plain":
        for i in range(n_pieces // 2):
            o = jnp.where(lower, ra[i * TILE:(i + 1) * TILE], rb[i * TILE:(i + 1) * TILE])
            o_ref[:, i * LANES:(i + 1) * LANES] = o.astype(BF16)
    else:
        aux = aux_ref[...]
        l1 = jnp.sum(aux[0:1] * aux[1:2], axis=-1, keepdims=True)
        l2 = jnp.sum(aux[2:3] * aux[3:4], axis=-1, keepdims=True)
        lam = jnp.exp(l1) - jnp.exp(l2) + lam_init
        oa = ra[:TILE] - lam * ra[TILE:]
        ob = rb[:TILE] - lam * rb[TILE:]
        o = jnp.where(lower, oa, ob)
        sq = o * o
        ss_a = jnp.sum(jnp.where(lower, sq, 0.0), axis=-1, keepdims=True)
        ss_b = jnp.sum(jnp.where(lower, 0.0, sq), axis=-1, keepdims=True)
        ms = jnp.where(lower, ss_a, ss_b) * (1.0 / HEAD_DIM)
        o = (o * lax.rsqrt(ms + EPS)) * aux[4:5]
        o_ref[...] = (o * (1.0 - lam_init)).astype(BF16)


def _flash(q, k, v, aux, *, n_qblk, n_sub, n_hp, qt_off, n_qt, n_lat, ctx_tile, mode, lam_init=0.0):
    b, t_all, _ = q.shape
    qw = n_qblk * LANES
    tk = FLASH_TK if n_lat % (2 * FLASH_TK) == 0 else 512
    assert n_lat % (2 * tk) == 0 and tk >= CTX_LEN
    m_rows = n_qblk * n_sub * TILE
    kern = functools.partial(_flash_kernel, n_qblk=n_qblk, n_sub=n_sub, tk=tk,
                             n_lat_blocks=n_lat // tk, ctx_tile=ctx_tile, qt_off=qt_off,
                             mode=mode, lam_init=lam_init)
    return pl.pallas_call(
        kern,
        out_shape=jax.ShapeDtypeStruct((b, n_qt * TILE, n_hp * qw), BF16),
        grid=(b, n_hp, n_qt),
        in_specs=[pl.BlockSpec((None, TILE, qw), lambda bi, hp, qt: (bi, qt + qt_off, hp)),
                  pl.BlockSpec((None, t_all, LANES), lambda bi, hp, qt: (bi, 0, hp)),
                  pl.BlockSpec((None, t_all, LANES), lambda bi, hp, qt: (bi, 0, hp)),
                  pl.BlockSpec((8, LANES), lambda bi, hp, qt: (0, 0))],
        out_specs=pl.BlockSpec((None, TILE, qw), lambda bi, hp, qt: (bi, qt, hp)),
        scratch_shapes=[pltpu.VMEM((t_all, LANES), BF16),
                        pltpu.VMEM((t_all, LANES), BF16),
                        pltpu.VMEM((m_rows, LANES), BF16),
                        pltpu.VMEM((m_rows, LANES), F32),
                        pltpu.VMEM((m_rows, LANES), F32),
                        pltpu.VMEM((m_rows, tk), F32),
                        pltpu.VMEM((m_rows, tk), F32),
                        pltpu.VMEM((m_rows, LANES), F32),
                        pltpu.VMEM((m_rows, LANES), F32)],
        compiler_params=_cparams(3),
        name="flash_" + mode,
    )(q, k, v, aux)


def _na_kernel(q_ref, k0, k1, k2, k3, v0, v1, v2, v3, kc_ref, vc_ref, bias_ref, o_ref, s_ref, m_ref):
    lane = lax.broadcasted_iota(jnp.int32, (1, LANES), 1)
    lower = lane < LANES // 2
    qf = q_ref[...].astype(F32)
    k_all = jnp.concatenate([k0[...], k1[...], k2[...], k3[...], kc_ref[...]], axis=0)
    v_all = jnp.concatenate([v0[...], v1[...], v2[...], v3[...], vc_ref[...]], axis=0).astype(F32)
    v_h = [jnp.where(lower, v_all, 1.0).astype(BF16), jnp.where(lower, 1.0, v_all).astype(BF16)]
    q_h = [jnp.where(lower, qf, 0.0).astype(BF16), jnp.where(lower, 0.0, qf).astype(BF16)]
    n_pair = NA_KROWS // 2
    rows_per_part = NA_QROWS // NA_PARTS
    half_q = rows_per_part * GRID_W
    no_bias = jnp.zeros((GRID_W, CTX_LEN), F32)

    def scores(part):
        rows = slice(part * half_q, (part + 1) * half_q)
        qs = jnp.concatenate([q_h[0][rows], q_h[1][rows]], axis=0)
        bias = jnp.concatenate(
            [jnp.concatenate([bias_ref[hh, a * n_pair + j] for j in range(n_pair)] + [no_bias], axis=1)
             for hh in range(2) for a in range(part * rows_per_part, (part + 1) * rows_per_part)],
            axis=0)
        s = _dot_nt(qs, k_all) + bias
        s_ref[part] = s
        m_ref[part] = jnp.broadcast_to(jnp.max(s, axis=-1, keepdims=True), (2 * half_q, LANES))

    def finish(part):
        s = s_ref[part]
        m = m_ref[part]
        p = jnp.concatenate([jnp.exp2((s[:, c * LANES:(c + 1) * LANES] - m).astype(BF16))
                             for c in range(s.shape[1] // LANES)], axis=1)
        o0 = _dot(p[:half_q], v_h[0])
        o1 = _dot(p[half_q:], v_h[1])
        o0 = o0 / pltpu.roll(o0, LANES // 2, 1)
        o1 = o1 / pltpu.roll(o1, LANES // 2, 1)
        o_ref[part * half_q:(part + 1) * half_q, :] = jnp.where(lower, o0, o1).astype(BF16)

    scores(0)
    for part in range(1, NA_PARTS):
        scores(part)
        finish(part - 1)
    finish(NA_PARTS - 1)


def _neighbourhood_attention(qa, ka, va, bias, n_lat):
    b = qa.shape[0]
    q_tok = NA_QROWS * GRID_W
    v_tok = q_tok // 2
    n_rb = n_lat // q_tok
    n_view = n_lat // v_tok
    ctx_blk = n_lat // v_tok

    def view(j):
        return lambda rb, hp, bi: (bi, jnp.clip(2 * rb - 1 + j, 0, n_view - 1), hp)

    kv_specs = [pl.BlockSpec((None, v_tok, LANES), view(j)) for j in range(4)]
    ctx_spec = pl.BlockSpec((None, CTX_LEN, LANES), lambda rb, hp, bi: (bi, ctx_blk, hp))

    def bias_map(rb, hp, bi):
        pat = jnp.where(rb == 0, 0, jnp.where(rb == n_rb - 1, 2, 1))
        return (hp, pat, 0, 0, 0)

    return pl.pallas_call(
        _na_kernel,
        out_shape=jax.ShapeDtypeStruct((b, n_lat, W_A), BF16),
        grid=(n_rb, NA_HEADS // 2, b),
        in_specs=[pl.BlockSpec((None, q_tok, LANES), lambda rb, hp, bi: (bi, rb, hp))]
                 + kv_specs + kv_specs + [ctx_spec, ctx_spec,
                 pl.BlockSpec((2, None, NA_QROWS * NA_KROWS // 2, GRID_W, 2 * GRID_W), bias_map)],
        out_specs=pl.BlockSpec((None, q_tok, LANES), lambda rb, hp, bi: (bi, rb, hp)),
        scratch_shapes=[pltpu.VMEM((NA_PARTS, 2 * q_tok // NA_PARTS, NA_KROWS * GRID_W + CTX_LEN), F32),
                        pltpu.VMEM((NA_PARTS, 2 * q_tok // NA_PARTS, LANES), F32)],
        compiler_params=_cparams(3),
        name="neighbourhood_attention",
    )(qa, ka, ka, ka, ka, va, va, va, va, ka, va, bias)


def _na_bias_table(rpb, rows):
    cols = np.arange(GRID_W)
    c0 = np.clip(cols - NA_WIN_W // 2, 0, GRID_W - NA_WIN_W)
    cc = cols[None, :]
    col_ok = (cc >= c0[:, None]) & (cc < c0[:, None] + NA_WIN_W)
    dc = np.clip(cc - cols[:, None] + (NA_WIN_W - 1), 0, 2 * NA_WIN_W - 2)
    e = jnp.where(col_ok[None, None], (rpb.astype(F32) * LOG2E)[:, :, dc], NEG)
    e = jnp.concatenate([e, jnp.full_like(e[:, :1], NEG)], axis=1)
    a = np.arange(NA_QROWS)[:, None]
    i = np.arange(NA_KROWS)[None, :]
    pats = []
    for r_base in (0, NA_QROWS, rows - NA_QROWS):
        r = r_base + a
        key_row = r_base - NA_WIN_H // 2 + i
        r0 = np.clip(r - NA_WIN_H // 2, 0, rows - NA_WIN_H)
        ok = (key_row >= r0) & (key_row < r0 + NA_WIN_H) & (key_row >= 0) & (key_row < rows)
        dr = np.where(ok, key_row - r + (NA_WIN_H - 1), 2 * NA_WIN_H - 1)
        pats.append(dr)
    dr_all = np.stack(pats)
    pairs = dr_all.reshape(-1, 2)
    uniq, inv = np.unique(pairs, axis=0, return_inverse=True)
    pair_blocks = jnp.concatenate([e[:, uniq[:, 0]], e[:, uniq[:, 1]]], axis=-1)
    t = pair_blocks[:, inv.reshape(-1)]
    return t.reshape(NA_HEADS, 3, NA_QROWS * NA_KROWS // 2, GRID_W, 2 * GRID_W)


def _outproj_kernel(x_ref, xc_ref, oa_ref, ob_ref, oc_ref, wa_ref, wb_ref, wc_ref, mod_ref, gain_ref,
                    wrh_ref, wrl_ref, br_ref, tri_ref, x1_ref, tok_ref, route_ref, cnt_ref, run_ref,
                    *, region, group_batches, n_lat_tiles):
    mod = mod_ref[...]
    y = _dot(oa_ref[...], wa_ref[...]) + _dot(ob_ref[...], wb_ref[...]) + _dot(oc_ref[...], wc_ref[...])
    x1 = jnp.where(pl.program_id(1) == n_lat_tiles, xc_ref[...], x_ref[...]) + mod[2:3] * y
    x1_ref[...] = x1
    ms = jnp.mean(x1 * x1, axis=-1, keepdims=True)
    t = (x1 * lax.rsqrt(ms + EPS)) * gain_ref[...]
    t = t * (1.0 + mod[4:5]) + mod[3:4]
    bits = lax.bitcast_convert_type(t.astype(BF16).astype(F32), jnp.int32)
    half_d = bits.shape[1] // 2
    tok_ref[...] = lax.shift_right_logical(bits[:, :half_d], 16) | (bits[:, half_d:] & HI16)

    t_hi, t_lo = _split_bf16(t)
    wrh = wrh_ref[...]
    logits = _dot(t_hi, wrh) + _dot(t_lo, wrh) + _dot(t_hi, wrl_ref[...]) + br_ref[...]

    lane = lax.broadcasted_iota(jnp.int32, logits.shape, 1)
    lane_f = lane.astype(F32)
    is_g = lane < N_GROUPS
    gl = jnp.where(is_g, logits, NEG)
    gmax = jnp.max(gl, axis=-1, keepdims=True)
    g_sel = jnp.min(jnp.where(gl == gmax, lane_f, 1e9), axis=-1, keepdims=True)
    p_grp = 1.0 / jnp.sum(jnp.where(is_g, jnp.exp(gl - gmax), 0.0), axis=-1, keepdims=True)
    grp_of_lane = lax.shift_right_arithmetic(lane - N_GROUPS, 2).astype(F32)
    in_grp = (lane >= N_GROUPS) & (lane < N_GROUPS + N_EXPERTS) & (grp_of_lane == g_sel)
    el = jnp.where(in_grp, logits, NEG)
    v1 = jnp.max(el, axis=-1, keepdims=True)
    i1 = jnp.min(jnp.where(el == v1, lane_f, 1e9), axis=-1, keepdims=True)
    el2 = jnp.where(lane_f == i1, NEG, el)
    v2 = jnp.max(el2, axis=-1, keepdims=True)
    i2 = jnp.min(jnp.where(el2 == v2, lane_f, 1e9), axis=-1, keepdims=True)
    e2 = jnp.exp(v2 - v1)
    den = 1.0 + e2
    w1 = p_grp / den
    w2 = p_grp * e2 / den

    @pl.when((lax.rem(pl.program_id(0), group_batches) == 0) & (pl.program_id(1) == 0))
    def _():
        run_ref[...] = jnp.zeros(run_ref.shape, F32)

    ind = jnp.where(lane_f == i1, 1.0, 0.0) + jnp.where(lane_f == i2, 1.0, 0.0)
    rank = _dot(tri_ref[...], ind.astype(BF16)) + run_ref[0:1, :]

    def pick(m, l):
        return jnp.sum(jnp.where(lane_f == l, m, 0.0), axis=-1, keepdims=True)

    pos1 = (i1 - N_GROUPS) * region + pick(rank, i1)
    pos2 = (i2 - N_GROUPS) * region + pick(rank, i2)
    route_ref[...] = jnp.where(lane == 0, pos1, jnp.where(lane == 1, pos2,
                               jnp.where(lane == 2, w1, jnp.where(lane == 3, w2, 0.0))))
    run = run_ref[...] + jnp.sum(ind, axis=0, keepdims=True)
    run_ref[...] = run
    cnt_ref[...] = run


def _out_projection(x_lat, x_ctx, ctx_blk, oa, ob, oc, wa, wb, wc, modsel, gain, wrh, wrl, br, n_tiles,
                    n_lat_tiles):
    b, _, d = x_lat.shape
    tok = lambda bi, ti: (bi, ti, 0)
    const2 = lambda bi, ti: (0, 0)
    rows = n_tiles * TILE
    nb = b // MOE_GROUPS if b % MOE_GROUPS == 0 else b
    tri = jnp.asarray(np.tril(np.ones((TILE, TILE), np.float32), -1), BF16)
    return pl.pallas_call(
        functools.partial(_outproj_kernel, region=nb * rows, group_batches=nb, n_lat_tiles=n_lat_tiles),
        out_shape=[jax.ShapeDtypeStruct((b, rows, d), F32),
                   jax.ShapeDtypeStruct((b, rows, d // 2), jnp.int32),
                   jax.ShapeDtypeStruct((b, rows, LANES), F32),
                   jax.ShapeDtypeStruct((8 * (b // nb), LANES), F32)],
        grid=(b, n_tiles),
        in_specs=_token_specs(d, n_lat_tiles, ctx_blk) + [
                  pl.BlockSpec((None, TILE, W_A), tok),
                  pl.BlockSpec((None, TILE, W_B), tok),
                  pl.BlockSpec((None, TILE, W_C), tok),
                  pl.BlockSpec((W_A, d), const2),
                  pl.BlockSpec((W_B, d), const2),
                  pl.BlockSpec((W_C, d), const2),
                  pl.BlockSpec((None, 6, d), lambda bi, ti: (2 * bi + (ti >= n_lat_tiles).astype(jnp.int32), 0, 0)),
                  pl.BlockSpec((1, d), const2),
                  pl.BlockSpec((d, LANES), const2),
                  pl.BlockSpec((d, LANES), const2),
                  pl.BlockSpec((1, LANES), const2),
                  pl.BlockSpec((TILE, TILE), const2)],
        out_specs=[pl.BlockSpec((None, TILE, d), tok),
                   pl.BlockSpec((None, TILE, d // 2), tok),
                   pl.BlockSpec((None, TILE, LANES), tok),
                   pl.BlockSpec((8, LANES), lambda bi, ti: (bi // nb, 0))],
        scratch_shapes=[pltpu.VMEM((8, LANES), F32)],
        compiler_params=_cparams(2),
        name="out_projection",
    )(x_lat, x_ctx, oa, ob, oc, wa, wb, wc, modsel, gain, wrh, wrl, br, tri)


def _sc_mesh():
    return plsc.VectorSubcoreMesh(core_axis_name="core", subcore_axis_name="subcore")


def _sc_worker_base(per_worker):
    wid = lax.axis_index("subcore") * SC_CORES + lax.axis_index("core")
    return wid * per_worker


def _sc_scratch(d, dtype):
    return ([pltpu.VMEM((SC_ROWS,), jnp.int32)] * SC_BUFS + [pltpu.VMEM((SC_ROWS, d), dtype)] * SC_BUFS
            + [pltpu.SemaphoreType.DMA] * (2 * SC_BUFS))


def _sc_split(scratch):
    return (scratch[:SC_BUFS], scratch[SC_BUFS:2 * SC_BUFS], scratch[2 * SC_BUFS:3 * SC_BUFS],
            scratch[3 * SC_BUFS:])


def _sc_chunk_loop(per_worker, group):
    chunks = per_worker // SC_ROWS
    full = chunks // SC_BUFS * SC_BUFS

    @pl.loop(0, full, step=SC_BUFS)
    def _(c):
        group(c, SC_BUFS)

    if chunks > full:
        group(full, chunks - full)


def _sc_scatter_rows(x, row_off, n, idx, n_out):
    d = x.shape[1]
    per_worker = 2 * n // (SC_CORES * SC_SUBCORES)
    assert per_worker % SC_ROWS == 0 and n % SC_ROWS == 0

    @functools.partial(pl.kernel, out_type=jax.ShapeDtypeStruct((n_out, d), x.dtype),
                       mesh=_sc_mesh(), scratch_types=_sc_scratch(d, x.dtype))
    def scatter(x_hbm, i_hbm, o_hbm, *scratch):
        idx_v, rows_v, sem_in, sem_out = _sc_split(scratch)
        base = _sc_worker_base(per_worker)

        def group(c, n_bufs):
            reads = []
            for u in range(n_bufs):
                a = pl.multiple_of(base + (c + u) * SC_ROWS, SC_ROWS)
                t = pl.multiple_of(row_off + lax.rem(a, n), SC_ROWS)
                pltpu.sync_copy(i_hbm.at[pl.ds(a, SC_ROWS)], idx_v[u])
                reads.append(pltpu.async_copy(x_hbm.at[pl.ds(t, SC_ROWS)], rows_v[u], sem_in[u]))
            writes = []
            for u in range(n_bufs):
                reads[u].wait()
                writes.append(pltpu.async_copy(rows_v[u], o_hbm.at[idx_v[u]], sem_out[u]))
            for w in writes:
                w.wait()

        _sc_chunk_loop(per_worker, group)

    return scatter(x, idx)


def _sc_gather_rows(src, idx):
    m = idx.shape[0]
    d = src.shape[1]
    per_worker = m // (SC_CORES * SC_SUBCORES)
    assert per_worker % SC_ROWS == 0

    @functools.partial(pl.kernel, out_type=jax.ShapeDtypeStruct((m, d), src.dtype),
                       mesh=_sc_mesh(), scratch_types=_sc_scratch(d, src.dtype))
    def gather(s_hbm, i_hbm, o_hbm, *scratch):
        idx_v, rows_v, sem_in, sem_out = _sc_split(scratch)
        base = _sc_worker_base(per_worker)

        def group(c, n_bufs):
            offs, reads = [], []
            for u in range(n_bufs):
                a = pl.multiple_of(base + (c + u) * SC_ROWS, SC_ROWS)
                offs.append(a)
                pltpu.sync_copy(i_hbm.at[pl.ds(a, SC_ROWS)], idx_v[u])
                reads.append(pltpu.async_copy(s_hbm.at[idx_v[u]], rows_v[u], sem_in[u]))
            writes = []
            for u in range(n_bufs):
                reads[u].wait()
                writes.append(pltpu.async_copy(rows_v[u], o_hbm.at[pl.ds(offs[u], SC_ROWS)], sem_out[u]))
            for w in writes:
                w.wait()

        _sc_chunk_loop(per_worker, group)

    return gather(src, idx)


def _expert_ffn_kernel(blk_ref, exp_ref, x_ref, wg_ref, wu_ref, wd_ref, y_ref, wgb_ref, wub_ref, wdb_ref):
    j = pl.program_id(0)

    @pl.when((j == 0) | (exp_ref[j] != exp_ref[jnp.maximum(j - 1, 0)]))
    def _():
        wgb_ref[...] = wg_ref[...].astype(BF16)
        wub_ref[...] = wu_ref[...].astype(BF16)
        wdb_ref[...] = wd_ref[...].astype(BF16)

    w = x_ref[...]
    x = jnp.concatenate([lax.bitcast_convert_type(lax.shift_left(w, 16), F32),
                         lax.bitcast_convert_type(w & HI16, F32)], axis=1).astype(BF16)
    hid = jax.nn.silu(_dot(x, wgb_ref[...])) * _dot(x, wub_ref[...])
    y_ref[...] = _dot(hid.astype(BF16), wdb_ref[...])


def _expert_ffn(xs, blk, exp, wg, wu, wd, layer):
    rows, d_packed = xs.shape
    d = 2 * d_packed
    w_map = lambda j, blk, exp: (layer, exp[j], 0, 0)
    return pl.pallas_call(
        _expert_ffn_kernel,
        out_shape=jax.ShapeDtypeStruct((rows, d), F32),
        grid_spec=pltpu.PrefetchScalarGridSpec(
            num_scalar_prefetch=2,
            grid=(blk.shape[0],),
            in_specs=[pl.BlockSpec((MOE_TILE, d_packed), lambda j, blk, exp: (blk[j], 0)),
                      pl.BlockSpec((None, None, d, EXPERT_HIDDEN), w_map),
                      pl.BlockSpec((None, None, d, EXPERT_HIDDEN), w_map),
                      pl.BlockSpec((None, None, EXPERT_HIDDEN, d), w_map)],
            out_specs=pl.BlockSpec((MOE_TILE, d), lambda j, blk, exp: (blk[j], 0)),
            scratch_shapes=[pltpu.VMEM((d, EXPERT_HIDDEN), BF16),
                            pltpu.VMEM((d, EXPERT_HIDDEN), BF16),
                            pltpu.VMEM((EXPERT_HIDDEN, d), BF16)]),
        compiler_params=_cparams(1),
        name="expert_ffn",
    )(blk, exp, xs, wg, wu, wd)


def _combine_kernel(x1_ref, y1_ref, y2_ref, route_ref, mod_ref, fgain_ref, *rest, final):
    o_ref = rest[-1]
    route = route_ref[...]
    y = route[:, 2:3] * y1_ref[...] + route[:, 3:4] * y2_ref[...]
    x2 = x1_ref[...] + mod_ref[5:6, :] * y
    if final:
        ms = jnp.mean(x2 * x2, axis=-1, keepdims=True)
        x2 = (x2 * lax.rsqrt(ms + EPS)) * fgain_ref[...]
    o_ref[...] = x2


def _combine(x1, ys, route, modsel, fgain, prev, b0, nb, n_lat_tiles, final):
    b, rows, d = x1.shape
    n_t = rows // TILE
    tok = lambda bi, ti: (b0 + bi, ti, 0)
    in_specs = [pl.BlockSpec((None, TILE, d), tok),
                pl.BlockSpec((TILE, d), lambda bi, ti: (bi * n_t + ti, 0)),
                pl.BlockSpec((TILE, d), lambda bi, ti: ((nb + bi) * n_t + ti, 0)),
                pl.BlockSpec((None, TILE, LANES), tok),
                pl.BlockSpec((None, 6, d),
                             lambda bi, ti: (2 * (b0 + bi) + (ti >= n_lat_tiles).astype(jnp.int32), 0, 0)),
                pl.BlockSpec((1, d), lambda bi, ti: (0, 0))]
    args = [x1, ys, ys, route, modsel, fgain]
    aliases = {}
    if prev is not None:
        in_specs.append(pl.BlockSpec(memory_space=pl.ANY))
        args.append(prev)
        aliases = {len(args) - 1: 0}
    return pl.pallas_call(
        functools.partial(_combine_kernel, final=final),
        out_shape=jax.ShapeDtypeStruct((b, rows, d), F32),
        grid=(nb, n_t),
        in_specs=in_specs,
        out_specs=pl.BlockSpec((None, TILE, d), tok),
        input_output_aliases=aliases,
        compiler_params=_cparams(2),
        name="moe_combine",
    )(*args)


def _routed_moe(tok, route, cnt, x1, wg, wu, wd, layer, modsel, fgain, n_lat_tiles, final):
    b, rows, d = x1.shape
    n_groups = cnt.shape[0] // 8
    nb = b // n_groups
    n = nb * rows
    flat = route.reshape(b * rows, LANES)
    tok_flat = tok.reshape(b * rows, tok.shape[2])
    out = None
    for g in range(n_groups):
        part = flat[g * n:(g + 1) * n]
        idx = jnp.concatenate([part[:, 0], part[:, 1]]).astype(jnp.int32)
        xs = _sc_scatter_rows(tok_flat, g * n, n, idx, N_EXPERTS * n)

        counts = cnt[8 * g, N_GROUPS:N_GROUPS + N_EXPERTS].astype(jnp.int32)
        tiles = (counts + MOE_TILE - 1) // MOE_TILE
        ends = jnp.cumsum(tiles)
        n_sched = 2 * n // MOE_TILE + N_EXPERTS
        j = jnp.minimum(jnp.arange(n_sched, dtype=jnp.int32), ends[-1] - 1)
        exp = jnp.sum((j[:, None] >= ends[None, :]).astype(jnp.int32), axis=1)
        blk = exp * (n // MOE_TILE) + j - (ends - tiles)[exp]

        ys = _expert_ffn(xs, blk, exp, wg, wu, wd, layer)
        yg = _sc_gather_rows(ys, idx)
        out = _combine(x1, yg, route, modsel, fgain, out, g * nb, nb, n_lat_tiles, final)
    return out


def _rope_tables(n_lat):
    t = jnp.arange(n_lat)
    row = (t // GRID_W).astype(F32)
    col = (t % GRID_W).astype(F32)

    def cs(dim):
        quarter = dim // 4
        freqs = ROPE_THETA ** (-jnp.arange(quarter, dtype=F32) / quarter)
        ang = jnp.concatenate([row[:, None] * freqs, col[:, None] * freqs], axis=-1)
        cos = jnp.tile(jnp.cos(ang), (1, 2 * LANES // dim))
        sin = jnp.tile(jnp.sin(ang), (1, 2 * LANES // dim))
        cos = jnp.concatenate([cos, jnp.ones((CTX_LEN, LANES), F32)], axis=0)
        sin = jnp.concatenate([sin, jnp.zeros((CTX_LEN, LANES), F32)], axis=0)
        return cos, sin

    cos_b, sin_b = cs(DIFF_QK_DIM)
    cos_c, sin_c = cs(HEAD_DIM)
    return jnp.concatenate([cos_b, sin_b, cos_c, sin_c], axis=1)


def _reordered_w_in(w_in):
    o_c = 3 * W_A + 3 * W_B
    heads = [w_in[:, o_c + h * HEAD_DIM:o_c + (h + 1) * HEAD_DIM] for h in GQA_Q_ORDER]
    return jnp.concatenate([w_in[:, :o_c]] + heads + [w_in[:, o_c + W_C:]], axis=1).astype(BF16)


def kernel(x, c, ctx, c_ctx, w_mod, b_mod, norm_attn, norm_ffn, w_in, w_out, na_rpb, diff_lambda_q1, diff_lambda_k1, diff_lambda_q2, diff_lambda_k2, diff_subln, gqa_q_norm, gqa_k_norm, router_group_w, router_group_b, router_expert_w, router_expert_b, w_gate, w_up, w_down, final_norm):
    b, s, d = x.shape
    assert d == D_MODEL and ctx.shape[1] == CTX_LEN and s % (NA_QROWS * GRID_W) == 0
    rows = s // GRID_W
    assert rows >= 2 * NA_QROWS
    t_all = s + CTX_LEN
    n_lat_tiles = s // TILE

    c_rows = jnp.zeros((8, d), F32).at[:b].set(c).at[b].set(c_ctx)
    mod = _modulation(c_rows, w_mod, b_mod)

    tab = _rope_tables(s)
    hidx = np.arange(HEAD_DIM)
    partner = np.where(hidx < HEAD_DIM // 2, hidx + HEAD_DIM // 2, hidx - HEAD_DIM // 2)
    blk = np.arange(W_C) // HEAD_DIM
    ones = jnp.asarray((blk[:, None] == blk[None, :]).astype(np.float32), BF16)
    dummy_aux = jnp.zeros((8, LANES), F32)

    x_lat, x_ctx, ctx_blk = x, ctx, 0
    for l in range(DEPTH):
        ctx_out = l < DEPTH - 1
        lam_init = 0.8 - 0.6 * math.exp(-0.3 * l)
        m_lat = mod[l, :b].reshape(b, 1, 6, d)
        m_ctx = jnp.broadcast_to(mod[l, b].reshape(1, 1, 6, d), (b, 1, 6, d))
        modsel = jnp.concatenate([m_lat, m_ctx], axis=1).reshape(2 * b, 6, d)

        gq = jnp.stack([jnp.tile(gqa_q_norm[l], GQA_Q_HEADS), jnp.tile(gqa_q_norm[l][partner], GQA_Q_HEADS)])
        gk = jnp.stack([jnp.tile(gqa_k_norm[l], GQA_KV_HEADS), jnp.tile(gqa_k_norm[l][partner], GQA_KV_HEADS)])
        qa, ka, va, qb, kb, vb, qc, kc, vc = _in_projection(
            x_lat, x_ctx, ctx_blk, modsel, norm_attn[l][None], _reordered_w_in(w_in[l]), tab, gq, gk, ones,
            n_lat_tiles)

        n_qt = n_lat_tiles + 1 if ctx_out else n_lat_tiles
        ctx_tile = n_lat_tiles if ctx_out else None
        oa = _neighbourhood_attention(qa, ka, va, _na_bias_table(na_rpb[l], rows), s)
        if ctx_out:
            oa_ctx = _flash(qa, ka, va, dummy_aux, n_qblk=1, n_sub=2, n_hp=NA_HEADS // 2,
                            qt_off=n_lat_tiles, n_qt=1, n_lat=s, ctx_tile=n_lat_tiles,
                            mode="plain")
            oa = jnp.concatenate([oa, oa_ctx], axis=1)
        pad = lambda v: jnp.pad(v, (0, LANES - v.shape[0]))
        aux = jnp.stack([pad(diff_lambda_q1[l]), pad(diff_lambda_k1[l]), pad(diff_lambda_q2[l]),
                         pad(diff_lambda_k2[l]), jnp.tile(diff_subln[l], 2),
                         jnp.zeros((LANES,), F32), jnp.zeros((LANES,), F32), jnp.zeros((LANES,), F32)])
        ob = _flash(qb, kb, vb, aux, n_qblk=1, n_sub=4, n_hp=DIFF_HEADS // 2, qt_off=0, n_qt=n_qt,
                    n_lat=s, ctx_tile=ctx_tile, mode="diff", lam_init=lam_init)
        oc = _flash(qc, kc, vc, dummy_aux, n_qblk=3, n_sub=2, n_hp=1, qt_off=0, n_qt=n_qt,
                    n_lat=s, ctx_tile=ctx_tile, mode="plain")

        w_o = w_out[l]
        o_c = W_A + W_B
        w_oc = jnp.concatenate([w_o[o_c + h * HEAD_DIM:o_c + (h + 1) * HEAD_DIM] for h in GQA_Q_ORDER], axis=0)
        wr = jnp.zeros((d, LANES), F32)
        wr = wr.at[:, :N_GROUPS].set(router_group_w[l]).at[:, N_GROUPS:N_GROUPS + N_EXPERTS].set(router_expert_w[l])
        wrh, wrl = _split_bf16(wr)
        br = jnp.zeros((1, LANES), F32)
        br = br.at[0, :N_GROUPS].set(router_group_b[l]).at[0, N_GROUPS:N_GROUPS + N_EXPERTS].set(router_expert_b[l])
        x1, tok, route, cnt = _out_projection(
            x_lat, x_ctx, ctx_blk, oa, ob, oc, w_o[:W_A].astype(BF16), w_o[W_A:W_A + W_B].astype(BF16),
            w_oc.astype(BF16), modsel, norm_ffn[l][None], wrh, wrl, br, n_qt, n_lat_tiles)
        xs = _routed_moe(tok, route, cnt, x1, w_gate, w_up, w_down, l, modsel, final_norm[None],
                         n_lat_tiles, final=not ctx_out)
        x_lat, x_ctx, ctx_blk = xs, xs, n_lat_tiles
    return xs
```

```python
import functools
import math

import numpy as np
import jax
import jax.numpy as jnp
from jax import lax
from jax.experimental import pallas as pl
from jax.experimental.pallas import tpu as pltpu
from jax.experimental.pallas import tpu_sc as plsc

F32 = jnp.float32
BF16 = jnp.bfloat16

D_MODEL = 1024
DEPTH = 2
GRID_W = 64
CTX_LEN = 256
HEAD_DIM = 64
NA_HEADS = 6
NA_WIN_H = 8
NA_WIN_W = 16
DIFF_HEADS = 4
DIFF_QK_DIM = 32
GQA_Q_HEADS = 6
GQA_KV_HEADS = 2
N_GROUPS = 4
EXPERTS_PER_GROUP = 4
N_EXPERTS = 16
EXPERT_HIDDEN = 512
ROPE_THETA = 10000.0
EPS = 1e-6
W_A = NA_HEADS * HEAD_DIM
W_B = DIFF_HEADS * 2 * DIFF_QK_DIM
W_C = GQA_Q_HEADS * HEAD_DIM
W_KC = GQA_KV_HEADS * HEAD_DIM
IN_WIDTH = 3 * W_A + 3 * W_B + W_C + 2 * W_KC

LANES = 128
TILE = CTX_LEN
NA_QROWS = 8
NA_KROWS = 16
NA_PARTS = 2
NEG = -1e30
LOG2E = 1.4426950408889634
HI16 = -65536
VMEM_LIMIT = 56 * 1024 * 1024
FLASH_TK = 512
PAIRS_PER_STEP = 2
MOE_TILE = 512
MOE_GROUPS = 2
SC_ROWS = 32
SC_BUFS = 4
SC_CORES = 2
SC_SUBCORES = 16

GQA_Q_ORDER = (0, 3, 1, 4, 2, 5)


def _cparams(n_axes):
    return pltpu.CompilerParams(dimension_semantics=("arbitrary",) * n_axes,
                                vmem_limit_bytes=VMEM_LIMIT)


def _split_bf16(a):
    hi = a.astype(BF16)
    lo = (a - hi.astype(F32)).astype(BF16)
    return hi, lo


def _dot(a, b):
    return jnp.dot(a, b, preferred_element_type=F32)


def _pack_bf16_pairs(t):
    bits = lax.bitcast_convert_type(t.astype(BF16).astype(F32), jnp.int32)
    half_d = bits.shape[1] // 2
    return lax.shift_right_logical(bits[:, :half_d], 16) | (bits[:, half_d:] & HI16)


def _unpack_bf16_pairs(w):
    return jnp.concatenate([lax.bitcast_convert_type(lax.shift_left(w, 16), F32),
                            lax.bitcast_convert_type(w & HI16, F32)], axis=1)


def _dot_nt(a, b):
    return lax.dot_general(a, b, (((1,), (1,)), ((), ())), preferred_element_type=F32)


def _mod_kernel(c_ref, w_ref, b_ref, o_ref):
    c = c_ref[...]
    a = c * jax.nn.sigmoid(c)
    a_hi, a_lo = _split_bf16(a)
    w_hi, w_lo = _split_bf16(w_ref[...])
    o_ref[...] = _dot(a_hi, w_hi) + _dot(a_lo, w_hi) + _dot(a_hi, w_lo) + b_ref[...]


def _modulation(c_rows, w_mod, b_mod):
    depth, d, n = w_mod.shape
    bn = 1536
    return pl.pallas_call(
        _mod_kernel,
        out_shape=jax.ShapeDtypeStruct((depth, 8, n), F32),
        grid=(depth, n // bn),
        in_specs=[pl.BlockSpec((8, d), lambda l, j: (0, 0)),
                  pl.BlockSpec((None, d, bn), lambda l, j: (l, 0, j)),
                  pl.BlockSpec((None, 1, bn), lambda l, j: (l, 0, j))],
        out_specs=pl.BlockSpec((None, 8, bn), lambda l, j: (l, 0, j)),
        compiler_params=_cparams(2),
        name="adaln_mod",
    )(c_rows, w_mod, b_mod.reshape(depth, 1, n))


def _head_mean_sq(t, ones):
    hi, lo = _split_bf16(t * t)
    return (_dot(hi, ones) + _dot(lo, ones)) * (1.0 / HEAD_DIM)


def _rotate_half(p, head):
    w = p.shape[1]
    half = head // 2
    lane = lax.broadcasted_iota(jnp.int32, (1, w), 1)
    first = (lane & (head - 1)) < half
    from_right = pltpu.roll(p, w - half, 1)
    from_left = pltpu.roll(p, half, 1)
    return jnp.where(first, -from_right, from_left)


def _inproj_kernel(x_ref, xc_ref, mod_ref, gain_ref, w_ref, tab_ref, gq_ref, gk_ref, ones_ref,
                   qa_ref, ka_ref, va_ref, qb_ref, kb_ref, vb_ref, qc_ref, kc_ref, vc_ref,
                   *, n_lat_tiles):
    x = jnp.where(pl.program_id(1) == n_lat_tiles, xc_ref[...], x_ref[...])
    mod = mod_ref[...]
    ms = jnp.mean(x * x, axis=-1, keepdims=True)
    h = (x * lax.rsqrt(ms + EPS)) * gain_ref[...]
    h = h * (1.0 + mod[1:2]) + mod[0:1]
    hb = h.astype(BF16)

    def proj(a, b):
        return _dot(hb, w_ref[:, a:b])

    pa = proj(0, 3 * W_A)
    qa_ref[...] = (pa[:, :W_A] * (HEAD_DIM ** -0.5 * LOG2E)).astype(BF16)
    ka_ref[...] = pa[:, W_A:2 * W_A].astype(BF16)
    va_ref[...] = pa[:, 2 * W_A:].astype(BF16)

    tab = tab_ref[...]
    cos_b = jnp.concatenate([tab[:, 0:LANES]] * 2, axis=1)
    sin_b = jnp.concatenate([tab[:, LANES:2 * LANES]] * 2, axis=1)
    cos_c1 = tab[:, 2 * LANES:3 * LANES]
    sin_c1 = tab[:, 3 * LANES:4 * LANES]
    cos_c = jnp.concatenate([cos_c1] * 3, axis=1)
    sin_c = jnp.concatenate([sin_c1] * 3, axis=1)

    o_b = 3 * W_A
    pb = proj(o_b, o_b + 3 * W_B)
    qb = pb[:, :W_B]
    kb = pb[:, W_B:2 * W_B]
    qb = qb * cos_b + _rotate_half(qb, DIFF_QK_DIM) * sin_b
    qb_ref[...] = (qb * (DIFF_QK_DIM ** -0.5 * LOG2E)).astype(BF16)
    kb_ref[...] = (kb * cos_b + _rotate_half(kb, DIFF_QK_DIM) * sin_b).astype(BF16)
    vb_ref[...] = pb[:, 2 * W_B:].astype(BF16)

    o_c = o_b + 3 * W_B
    pc = proj(o_c, IN_WIDTH)
    ones = ones_ref[...]
    qc = pc[:, :W_C]
    kc = pc[:, W_C:W_C + W_KC]
    nq = lax.rsqrt(_head_mean_sq(qc, ones) + EPS)
    nk = lax.rsqrt(_head_mean_sq(kc, ones[:W_KC, :W_KC]) + EPS)
    gq = gq_ref[...]
    gk = gk_ref[...]
    q = nq * (qc * gq[0:1] * cos_c + _rotate_half(qc, HEAD_DIM) * gq[1:2] * sin_c)
    qc_ref[...] = (q * (HEAD_DIM ** -0.5 * LOG2E)).astype(BF16)
    k = nk * (kc * gk[0:1] * cos_c1 + _rotate_half(kc, HEAD_DIM) * gk[1:2] * sin_c1)
    kc_ref[...] = k.astype(BF16)
    vc_ref[...] = pc[:, W_C + W_KC:].astype(BF16)


def _token_specs(d, n_lat_tiles, ctx_blk):
    return [pl.BlockSpec((None, TILE, d), lambda bi, ti: (bi, jnp.minimum(ti, n_lat_tiles - 1), 0)),
            pl.BlockSpec((None, TILE, d), lambda bi, ti: (bi, ctx_blk, 0))]


def _in_projection(x_lat, x_ctx, ctx_blk, modsel, gain, w_ext, tab, gq, gk, ones, n_lat_tiles):
    b, _, d = x_lat.shape
    n_tiles = n_lat_tiles + 1
    t_all = n_tiles * TILE
    widths = (W_A, W_A, W_A, W_B, W_B, W_B, W_C, W_KC, W_KC)
    tok = lambda bi, ti: (bi, ti, 0)
    const2 = lambda bi, ti: (0, 0)
    return pl.pallas_call(
        functools.partial(_inproj_kernel, n_lat_tiles=n_lat_tiles),
        out_shape=[jax.ShapeDtypeStruct((b, t_all, w), BF16) for w in widths],
        grid=(b, n_tiles),
        in_specs=_token_specs(d, n_lat_tiles, ctx_blk) + [
                  pl.BlockSpec((None, 6, d), lambda bi, ti: (2 * bi + (ti >= n_lat_tiles).astype(jnp.int32), 0, 0)),
                  pl.BlockSpec((1, d), const2),
                  pl.BlockSpec((d, IN_WIDTH), const2),
                  pl.BlockSpec((TILE, 4 * LANES), lambda bi, ti: (ti, 0)),
                  pl.BlockSpec((2, W_C), const2),
                  pl.BlockSpec((2, W_KC), const2),
                  pl.BlockSpec((W_C, W_C), const2)],
        out_specs=[pl.BlockSpec((None, TILE, w), tok) for w in widths],
        compiler_params=_cparams(2),
        name="in_projection",
    )(x_lat, x_ctx, modsel, gain, w_ext, tab, gq, gk, ones)


def _flash_kernel(q_ref, k_ref, v_ref, aux_ref, o_ref, va_ref, vb_ref, qs_ref, acc_ref, m_ref,
                  s0_ref, s1_ref, mb0_ref, mb1_ref, *,
                  n_qblk, n_sub, tk, n_lat_blocks, ctx_tile, qt_off, mode, lam_init):
    qt = pl.program_id(2) + qt_off
    sub_w = LANES // n_sub
    half = LANES // 2
    lane = lax.broadcasted_iota(jnp.int32, (1, LANES), 1)
    lower = lane < half
    n_pieces = n_qblk * n_sub
    ma = (n_pieces // 2) * TILE
    m_rows = n_pieces * TILE
    ctx_start = n_lat_blocks * tk

    @pl.when(pl.program_id(2) == 0)
    def _():
        v = v_ref[...].astype(F32)
        va_ref[...] = jnp.where(lower, v, 1.0).astype(BF16)
        vb_ref[...] = jnp.where(lower, 1.0, v).astype(BF16)

    ia, ib = 0, n_pieces // 2
    for blk in range(n_qblk):
        qf = q_ref[:, blk * LANES:(blk + 1) * LANES].astype(F32)
        for sub in range(n_sub):
            msk = (lane >= sub * sub_w) & (lane < (sub + 1) * sub_w)
            piece = jnp.where(msk, qf, 0.0).astype(BF16)
            if sub * sub_w < half:
                qs_ref[ia * TILE:(ia + 1) * TILE, :] = piece
                ia += 1
            else:
                qs_ref[ib * TILE:(ib + 1) * TILE, :] = piece
                ib += 1

    s_bufs = (s0_ref, s1_ref)
    mb_bufs = (mb0_ref, mb1_ref)

    def scores(start, size, slot):
        s = _dot_nt(qs_ref[...], k_ref[pl.ds(start, size), :])
        s_bufs[slot][:, :size] = s
        mb = jnp.max(s, axis=-1, keepdims=True)
        mb_bufs[slot][...] = jnp.broadcast_to(mb, (m_rows, LANES))

    def accumulate(start, size, slot, first):
        mb = mb_bufs[slot][...]
        if first:
            m_new = mb
        else:
            m_old = m_ref[...]
            m_new = jnp.maximum(m_old, mb)
        s_ref = s_bufs[slot]
        cols = [s_ref[:, c * LANES:(c + 1) * LANES] - m_new for c in range(size // LANES)]
        p = jnp.concatenate([jnp.exp2(d.astype(BF16)) for d in cols], axis=1)
        pva = _dot(p[:ma], va_ref[pl.ds(start, size), :])
        pvb = _dot(p[ma:], vb_ref[pl.ds(start, size), :])
        if first:
            acc_ref[:ma, :] = pva
            acc_ref[ma:, :] = pvb
        else:
            alpha = jnp.exp2(m_old - m_new)
            acc_ref[:ma, :] = alpha[:ma] * acc_ref[:ma, :] + pva
            acc_ref[ma:, :] = alpha[ma:] * acc_ref[ma:, :] + pvb
        m_ref[...] = m_new

    def lat(j):
        return pl.multiple_of(j * tk, tk)

    def latent_queries():
        scores(ctx_start, CTX_LEN, 0)
        scores(lat(0), tk, 1)
        accumulate(ctx_start, CTX_LEN, 0, True)

        def pair(i):
            scores(lat(2 * i + 1), tk, 0)
            accumulate(lat(2 * i), tk, 1, False)
            scores(lat(2 * i + 2), tk, 1)
            accumulate(lat(2 * i + 1), tk, 0, False)

        def body(i, carry):
            for u in range(PAIRS_PER_STEP):
                pair(i * PAIRS_PER_STEP + u)
            return carry

        n_pairs = (n_lat_blocks - 2) // 2
        n_steps = n_pairs // PAIRS_PER_STEP
        lax.fori_loop(0, n_steps, body, 0)
        for i in range(n_steps * PAIRS_PER_STEP, n_pairs):
            pair(i)
        scores(lat(n_lat_blocks - 1), tk, 0)
        accumulate(lat(n_lat_blocks - 2), tk, 1, False)
        accumulate(lat(n_lat_blocks - 1), tk, 0, False)

    def context_queries():
        scores(ctx_start, CTX_LEN, 0)
        accumulate(ctx_start, CTX_LEN, 0, True)

    if ctx_tile is None:
        latent_queries()
    else:
        pl.when(qt != ctx_tile)(latent_queries)
        pl.when(qt == ctx_tile)(context_queries)

    acc = acc_ref[...]
    r = acc / pltpu.roll(acc, half, 1)
    ra, rb = r[:ma], r[ma:]
    if mode == "plain":
        for i in range(n_pieces // 2):
            o = jnp.where(lower, ra[i * TILE:(i + 1) * TILE], rb[i * TILE:(i + 1) * TILE])
            o_ref[:, i * LANES:(i + 1) * LANES] = o.astype(BF16)
    else:
        aux = aux_ref[...]
        l1 = jnp.sum(aux[0:1] * aux[1:2], axis=-1, keepdims=True)
        l2 = jnp.sum(aux[2:3] * aux[3:4], axis=-1, keepdims=True)
        lam = jnp.exp(l1) - jnp.exp(l2) + lam_init
        oa = ra[:TILE] - lam * ra[TILE:]
        ob = rb[:TILE] - lam * rb[TILE:]
        o = jnp.where(lower, oa, ob)
        sq = o * o
        ss_a = jnp.sum(jnp.where(lower, sq, 0.0), axis=-1, keepdims=True)
        ss_b = jnp.sum(jnp.where(lower, 0.0, sq), axis=-1, keepdims=True)
        ms = jnp.where(lower, ss_a, ss_b) * (1.0 / HEAD_DIM)
        o = (o * lax.rsqrt(ms + EPS)) * aux[4:5]
        o_ref[...] = (o * (1.0 - lam_init)).astype(BF16)


def _flash(q, k, v, aux, *, n_qblk, n_sub, n_hp, qt_off, n_qt, n_lat, ctx_tile, mode, lam_init=0.0):
    b, t_all, _ = q.shape
    qw = n_qblk * LANES
    tk = FLASH_TK if n_lat % (2 * FLASH_TK) == 0 else 512
    assert n_lat % (2 * tk) == 0 and tk >= CTX_LEN
    m_rows = n_qblk * n_sub * TILE
    kern = functools.partial(_flash_kernel, n_qblk=n_qblk, n_sub=n_sub, tk=tk,
                             n_lat_blocks=n_lat // tk, ctx_tile=ctx_tile, qt_off=qt_off,
                             mode=mode, lam_init=lam_init)
    return pl.pallas_call(
        kern,
        out_shape=jax.ShapeDtypeStruct((b, n_qt * TILE, n_hp * qw), BF16),
        grid=(b, n_hp, n_qt),
        in_specs=[pl.BlockSpec((None, TILE, qw), lambda bi, hp, qt: (bi, qt + qt_off, hp)),
                  pl.BlockSpec((None, t_all, LANES), lambda bi, hp, qt: (bi, 0, hp)),
                  pl.BlockSpec((None, t_all, LANES), lambda bi, hp, qt: (bi, 0, hp)),
                  pl.BlockSpec((8, LANES), lambda bi, hp, qt: (0, 0))],
        out_specs=pl.BlockSpec((None, TILE, qw), lambda bi, hp, qt: (bi, qt, hp)),
        scratch_shapes=[pltpu.VMEM((t_all, LANES), BF16),
                        pltpu.VMEM((t_all, LANES), BF16),
                        pltpu.VMEM((m_rows, LANES), BF16),
                        pltpu.VMEM((m_rows, LANES), F32),
                        pltpu.VMEM((m_rows, LANES), F32),
                        pltpu.VMEM((m_rows, tk), F32),
                        pltpu.VMEM((m_rows, tk), F32),
                        pltpu.VMEM((m_rows, LANES), F32),
                        pltpu.VMEM((m_rows, LANES), F32)],
        compiler_params=_cparams(3),
        name="flash_" + mode,
    )(q, k, v, aux)


def _na_kernel(q_ref, k0, k1, k2, k3, v0, v1, v2, v3, kc_ref, vc_ref, bias_ref, o_ref, s_ref, m_ref):
    lane = lax.broadcasted_iota(jnp.int32, (1, LANES), 1)
    lower = lane < LANES // 2
    qf = q_ref[...].astype(F32)
    k_all = jnp.concatenate([k0[...], k1[...], k2[...], k3[...], kc_ref[...]], axis=0)
    v_all = jnp.concatenate([v0[...], v1[...], v2[...], v3[...], vc_ref[...]], axis=0).astype(F32)
    v_h = [jnp.where(lower, v_all, 1.0).astype(BF16), jnp.where(lower, 1.0, v_all).astype(BF16)]
    q_h = [jnp.where(lower, qf, 0.0).astype(BF16), jnp.where(lower, 0.0, qf).astype(BF16)]
    n_pair = NA_KROWS // 2
    rows_per_part = NA_QROWS // NA_PARTS
    half_q = rows_per_part * GRID_W
    no_bias = jnp.zeros((GRID_W, CTX_LEN), F32)

    def scores(part):
        rows = slice(part * half_q, (part + 1) * half_q)
        qs = jnp.concatenate([q_h[0][rows], q_h[1][rows]], axis=0)
        bias = jnp.concatenate(
            [jnp.concatenate([bias_ref[hh, a * n_pair + j] for j in range(n_pair)] + [no_bias], axis=1)
             for hh in range(2) for a in range(part * rows_per_part, (part + 1) * rows_per_part)],
            axis=0)
        s = _dot_nt(qs, k_all) + bias
        s_ref[part] = s
        m_ref[part] = jnp.broadcast_to(jnp.max(s, axis=-1, keepdims=True), (2 * half_q, LANES))

    def finish(part):
        s = s_ref[part]
        m = m_ref[part]
        p = jnp.concatenate([jnp.exp2((s[:, c * LANES:(c + 1) * LANES] - m).astype(BF16))
                             for c in range(s.shape[1] // LANES)], axis=1)
        o0 = _dot(p[:half_q], v_h[0])
        o1 = _dot(p[half_q:], v_h[1])
        o0 = o0 / pltpu.roll(o0, LANES // 2, 1)
        o1 = o1 / pltpu.roll(o1, LANES // 2, 1)
        o_ref[part * half_q:(part + 1) * half_q, :] = jnp.where(lower, o0, o1).astype(BF16)

    scores(0)
    for part in range(1, NA_PARTS):
        scores(part)
        finish(part - 1)
    finish(NA_PARTS - 1)


def _neighbourhood_attention(qa, ka, va, bias, n_lat):
    b = qa.shape[0]
    q_tok = NA_QROWS * GRID_W
    v_tok = q_tok // 2
    n_rb = n_lat // q_tok
    n_view = n_lat // v_tok
    ctx_blk = n_lat // v_tok

    def view(j):
        return lambda rb, hp, bi: (bi, jnp.clip(2 * rb - 1 + j, 0, n_view - 1), hp)

    kv_specs = [pl.BlockSpec((None, v_tok, LANES), view(j)) for j in range(4)]
    ctx_spec = pl.BlockSpec((None, CTX_LEN, LANES), lambda rb, hp, bi: (bi, ctx_blk, hp))

    def bias_map(rb, hp, bi):
        pat = jnp.where(rb == 0, 0, jnp.where(rb == n_rb - 1, 2, 1))
        return (hp, pat, 0, 0, 0)

    return pl.pallas_call(
        _na_kernel,
        out_shape=jax.ShapeDtypeStruct((b, n_lat, W_A), BF16),
        grid=(n_rb, NA_HEADS // 2, b),
        in_specs=[pl.BlockSpec((None, q_tok, LANES), lambda rb, hp, bi: (bi, rb, hp))]
                 + kv_specs + kv_specs + [ctx_spec, ctx_spec,
                 pl.BlockSpec((2, None, NA_QROWS * NA_KROWS // 2, GRID_W, 2 * GRID_W), bias_map)],
        out_specs=pl.BlockSpec((None, q_tok, LANES), lambda rb, hp, bi: (bi, rb, hp)),
        scratch_shapes=[pltpu.VMEM((NA_PARTS, 2 * q_tok // NA_PARTS, NA_KROWS * GRID_W + CTX_LEN), F32),
                        pltpu.VMEM((NA_PARTS, 2 * q_tok // NA_PARTS, LANES), F32)],
        compiler_params=_cparams(3),
        name="neighbourhood_attention",
    )(qa, ka, ka, ka, ka, va, va, va, va, ka, va, bias)


def _na_bias_table(rpb, rows):
    cols = np.arange(GRID_W)
    c0 = np.clip(cols - NA_WIN_W // 2, 0, GRID_W - NA_WIN_W)
    cc = cols[None, :]
    col_ok = (cc >= c0[:, None]) & (cc < c0[:, None] + NA_WIN_W)
    dc = np.clip(cc - cols[:, None] + (NA_WIN_W - 1), 0, 2 * NA_WIN_W - 2)
    e = jnp.where(col_ok[None, None], (rpb.astype(F32) * LOG2E)[:, :, dc], NEG)
    e = jnp.concatenate([e, jnp.full_like(e[:, :1], NEG)], axis=1)
    a = np.arange(NA_QROWS)[:, None]
    i = np.arange(NA_KROWS)[None, :]
    pats = []
    for r_base in (0, NA_QROWS, rows - NA_QROWS):
        r = r_base + a
        key_row = r_base - NA_WIN_H // 2 + i
        r0 = np.clip(r - NA_WIN_H // 2, 0, rows - NA_WIN_H)
        ok = (key_row >= r0) & (key_row < r0 + NA_WIN_H) & (key_row >= 0) & (key_row < rows)
        dr = np.where(ok, key_row - r + (NA_WIN_H - 1), 2 * NA_WIN_H - 1)
        pats.append(dr)
    dr_all = np.stack(pats)
    pairs = dr_all.reshape(-1, 2)
    uniq, inv = np.unique(pairs, axis=0, return_inverse=True)
    pair_blocks = jnp.concatenate([e[:, uniq[:, 0]], e[:, uniq[:, 1]]], axis=-1)
    t = pair_blocks[:, inv.reshape(-1)]
    return t.reshape(NA_HEADS, 3, NA_QROWS * NA_KROWS // 2, GRID_W, 2 * GRID_W)


def _outproj_kernel(x_ref, xc_ref, oa_ref, ob_ref, oc_ref, wa_ref, wb_ref, wc_ref, mod_ref, gain_ref,
                    wrh_ref, wrl_ref, br_ref, tri_ref, x1_ref, tok_ref, route_ref, cnt_ref, run_ref,
                    *, region, group_batches, n_lat_tiles):
    mod = mod_ref[...]
    y = _dot(oa_ref[...], wa_ref[...]) + _dot(ob_ref[...], wb_ref[...]) + _dot(oc_ref[...], wc_ref[...])
    x1 = jnp.where(pl.program_id(1) == n_lat_tiles, xc_ref[...], x_ref[...]) + mod[2:3] * y
    x1_ref[...] = x1
    ms = jnp.mean(x1 * x1, axis=-1, keepdims=True)
    t = (x1 * lax.rsqrt(ms + EPS)) * gain_ref[...]
    t = t * (1.0 + mod[4:5]) + mod[3:4]
    tok_ref[...] = _pack_bf16_pairs(t)

    t_hi, t_lo = _split_bf16(t)
    wrh = wrh_ref[...]
    logits = _dot(t_hi, wrh) + _dot(t_lo, wrh) + _dot(t_hi, wrl_ref[...]) + br_ref[...]

    lane = lax.broadcasted_iota(jnp.int32, logits.shape, 1)
    lane_f = lane.astype(F32)
    is_g = lane < N_GROUPS
    gl = jnp.where(is_g, logits, NEG)
    gmax = jnp.max(gl, axis=-1, keepdims=True)
    g_sel = jnp.min(jnp.where(gl == gmax, lane_f, 1e9), axis=-1, keepdims=True)
    p_grp = 1.0 / jnp.sum(jnp.where(is_g, jnp.exp(gl - gmax), 0.0), axis=-1, keepdims=True)
    grp_of_lane = lax.shift_right_arithmetic(lane - N_GROUPS, 2).astype(F32)
    in_grp = (lane >= N_GROUPS) & (lane < N_GROUPS + N_EXPERTS) & (grp_of_lane == g_sel)
    el = jnp.where(in_grp, logits, NEG)
    v1 = jnp.max(el, axis=-1, keepdims=True)
    i1 = jnp.min(jnp.where(el == v1, lane_f, 1e9), axis=-1, keepdims=True)
    el2 = jnp.where(lane_f == i1, NEG, el)
    v2 = jnp.max(el2, axis=-1, keepdims=True)
    i2 = jnp.min(jnp.where(el2 == v2, lane_f, 1e9), axis=-1, keepdims=True)
    e2 = jnp.exp(v2 - v1)
    den = 1.0 + e2
    w1 = p_grp / den
    w2 = p_grp * e2 / den

    @pl.when((lax.rem(pl.program_id(0), group_batches) == 0) & (pl.program_id(1) == 0))
    def _():
        run_ref[...] = jnp.zeros(run_ref.shape, F32)

    ind = jnp.where(lane_f == i1, 1.0, 0.0) + jnp.where(lane_f == i2, 1.0, 0.0)
    rank = _dot(tri_ref[...], ind.astype(BF16)) + run_ref[0:1, :]

    def pick(m, l):
        return jnp.sum(jnp.where(lane_f == l, m, 0.0), axis=-1, keepdims=True)

    pos1 = (i1 - N_GROUPS) * region + pick(rank, i1)
    pos2 = (i2 - N_GROUPS) * region + pick(rank, i2)
    route_ref[...] = jnp.where(lane == 0, pos1, jnp.where(lane == 1, pos2,
                               jnp.where(lane == 2, w1, jnp.where(lane == 3, w2, 0.0))))
    run = run_ref[...] + jnp.sum(ind, axis=0, keepdims=True)
    run_ref[...] = run
    cnt_ref[...] = run


def _out_projection(x_lat, x_ctx, ctx_blk, oa, ob, oc, wa, wb, wc, modsel, gain, wrh, wrl, br, n_tiles,
                    n_lat_tiles):
    b, _, d = x_lat.shape
    tok = lambda bi, ti: (bi, ti, 0)
    const2 = lambda bi, ti: (0, 0)
    rows = n_tiles * TILE
    nb = b // MOE_GROUPS if b % MOE_GROUPS == 0 else b
    tri = jnp.asarray(np.tril(np.ones((TILE, TILE), np.float32), -1), BF16)
    return pl.pallas_call(
        functools.partial(_outproj_kernel, region=nb * rows, group_batches=nb, n_lat_tiles=n_lat_tiles),
        out_shape=[jax.ShapeDtypeStruct((b, rows, d), F32),
                   jax.ShapeDtypeStruct((b, rows, d // 2), jnp.int32),
                   jax.ShapeDtypeStruct((b, rows, LANES), F32),
                   jax.ShapeDtypeStruct((8 * (b // nb), LANES), F32)],
        grid=(b, n_tiles),
        in_specs=_token_specs(d, n_lat_tiles, ctx_blk) + [
                  pl.BlockSpec((None, TILE, W_A), tok),
                  pl.BlockSpec((None, TILE, W_B), tok),
                  pl.BlockSpec((None, TILE, W_C), tok),
                  pl.BlockSpec((W_A, d), const2),
                  pl.BlockSpec((W_B, d), const2),
                  pl.BlockSpec((W_C, d), const2),
                  pl.BlockSpec((None, 6, d), lambda bi, ti: (2 * bi + (ti >= n_lat_tiles).astype(jnp.int32), 0, 0)),
                  pl.BlockSpec((1, d), const2),
                  pl.BlockSpec((d, LANES), const2),
                  pl.BlockSpec((d, LANES), const2),
                  pl.BlockSpec((1, LANES), const2),
                  pl.BlockSpec((TILE, TILE), const2)],
        out_specs=[pl.BlockSpec((None, TILE, d), tok),
                   pl.BlockSpec((None, TILE, d // 2), tok),
                   pl.BlockSpec((None, TILE, LANES), tok),
                   pl.BlockSpec((8, LANES), lambda bi, ti: (bi // nb, 0))],
        scratch_shapes=[pltpu.VMEM((8, LANES), F32)],
        compiler_params=_cparams(2),
        name="out_projection",
    )(x_lat, x_ctx, oa, ob, oc, wa, wb, wc, modsel, gain, wrh, wrl, br, tri)


def _sc_mesh():
    return plsc.VectorSubcoreMesh(core_axis_name="core", subcore_axis_name="subcore")


def _sc_worker_base(per_worker):
    wid = lax.axis_index("subcore") * SC_CORES + lax.axis_index("core")
    return wid * per_worker


def _sc_scratch(d, dtype):
    return ([pltpu.VMEM((SC_ROWS,), jnp.int32)] * SC_BUFS + [pltpu.VMEM((SC_ROWS, d), dtype)] * SC_BUFS
            + [pltpu.SemaphoreType.DMA] * (2 * SC_BUFS))


def _sc_split(scratch):
    return (scratch[:SC_BUFS], scratch[SC_BUFS:2 * SC_BUFS], scratch[2 * SC_BUFS:3 * SC_BUFS],
            scratch[3 * SC_BUFS:])


def _sc_chunk_loop(per_worker, group):
    chunks = per_worker // SC_ROWS
    full = chunks // SC_BUFS * SC_BUFS

    @pl.loop(0, full, step=SC_BUFS)
    def _(c):
        group(c, SC_BUFS)

    if chunks > full:
        group(full, chunks - full)


def _sc_scatter_rows(x, row_off, n, idx, n_out):
    d = x.shape[1]
    per_worker = 2 * n // (SC_CORES * SC_SUBCORES)
    assert per_worker % SC_ROWS == 0 and n % SC_ROWS == 0

    @functools.partial(pl.kernel, out_type=jax.ShapeDtypeStruct((n_out, d), x.dtype),
                       mesh=_sc_mesh(), scratch_types=_sc_scratch(d, x.dtype))
    def scatter(x_hbm, i_hbm, o_hbm, *scratch):
        idx_v, rows_v, sem_in, sem_out = _sc_split(scratch)
        base = _sc_worker_base(per_worker)

        def group(c, n_bufs):
            reads = []
            for u in range(n_bufs):
                a = pl.multiple_of(base + (c + u) * SC_ROWS, SC_ROWS)
                t = pl.multiple_of(row_off + lax.rem(a, n), SC_ROWS)
                pltpu.sync_copy(i_hbm.at[pl.ds(a, SC_ROWS)], idx_v[u])
                reads.append(pltpu.async_copy(x_hbm.at[pl.ds(t, SC_ROWS)], rows_v[u], sem_in[u]))
            writes = []
            for u in range(n_bufs):
                reads[u].wait()
                writes.append(pltpu.async_copy(rows_v[u], o_hbm.at[idx_v[u]], sem_out[u]))
            for w in writes:
                w.wait()

        _sc_chunk_loop(per_worker, group)

    return scatter(x, idx)


def _sc_gather_rows(src, idx):
    m = idx.shape[0]
    d = src.shape[1]
    per_worker = m // (SC_CORES * SC_SUBCORES)
    assert per_worker % SC_ROWS == 0

    @functools.partial(pl.kernel, out_type=jax.ShapeDtypeStruct((m, d), src.dtype),
                       mesh=_sc_mesh(), scratch_types=_sc_scratch(d, src.dtype))
    def gather(s_hbm, i_hbm, o_hbm, *scratch):
        idx_v, rows_v, sem_in, sem_out = _sc_split(scratch)
        base = _sc_worker_base(per_worker)

        def group(c, n_bufs):
            offs, reads = [], []
            for u in range(n_bufs):
                a = pl.multiple_of(base + (c + u) * SC_ROWS, SC_ROWS)
                offs.append(a)
                pltpu.sync_copy(i_hbm.at[pl.ds(a, SC_ROWS)], idx_v[u])
                reads.append(pltpu.async_copy(s_hbm.at[idx_v[u]], rows_v[u], sem_in[u]))
            writes = []
            for u in range(n_bufs):
                reads[u].wait()
                writes.append(pltpu.async_copy(rows_v[u], o_hbm.at[pl.ds(offs[u], SC_ROWS)], sem_out[u]))
            for w in writes:
                w.wait()

        _sc_chunk_loop(per_worker, group)

    return gather(src, idx)


def _expert_ffn_kernel(blk_ref, exp_ref, x_ref, wg_ref, wu_ref, wd_ref, y_ref, wgb_ref, wub_ref, wdb_ref):
    j = pl.program_id(0)

    @pl.when((j == 0) | (exp_ref[j] != exp_ref[jnp.maximum(j - 1, 0)]))
    def _():
        wgb_ref[...] = wg_ref[...].astype(BF16)
        wub_ref[...] = wu_ref[...].astype(BF16)
        wdb_ref[...] = wd_ref[...].astype(BF16)

    x = _unpack_bf16_pairs(x_ref[...]).astype(BF16)
    hid = jax.nn.silu(_dot(x, wgb_ref[...])) * _dot(x, wub_ref[...])
    y_ref[...] = _pack_bf16_pairs(_dot(hid.astype(BF16), wdb_ref[...]))


def _expert_ffn(xs, blk, exp, wg, wu, wd, layer):
    rows, d_packed = xs.shape
    d = 2 * d_packed
    w_map = lambda j, blk, exp: (layer, exp[j], 0, 0)
    return pl.pallas_call(
        _expert_ffn_kernel,
        out_shape=jax.ShapeDtypeStruct((rows, d_packed), jnp.int32),
        grid_spec=pltpu.PrefetchScalarGridSpec(
            num_scalar_prefetch=2,
            grid=(blk.shape[0],),
            in_specs=[pl.BlockSpec((MOE_TILE, d_packed), lambda j, blk, exp: (blk[j], 0)),
                      pl.BlockSpec((None, None, d, EXPERT_HIDDEN), w_map),
                      pl.BlockSpec((None, None, d, EXPERT_HIDDEN), w_map),
                      pl.BlockSpec((None, None, EXPERT_HIDDEN, d), w_map)],
            out_specs=pl.BlockSpec((MOE_TILE, d_packed), lambda j, blk, exp: (blk[j], 0)),
            scratch_shapes=[pltpu.VMEM((d, EXPERT_HIDDEN), BF16),
                            pltpu.VMEM((d, EXPERT_HIDDEN), BF16),
                            pltpu.VMEM((EXPERT_HIDDEN, d), BF16)]),
        compiler_params=_cparams(1),
        name="expert_ffn",
    )(blk, exp, xs, wg, wu, wd)


def _combine_kernel(x1_ref, y1_ref, y2_ref, route_ref, mod_ref, fgain_ref, *rest, final):
    o_ref = rest[-1]
    route = route_ref[...]
    y = route[:, 2:3] * _unpack_bf16_pairs(y1_ref[...]) + route[:, 3:4] * _unpack_bf16_pairs(y2_ref[...])
    x2 = x1_ref[...] + mod_ref[5:6, :] * y
    if final:
        ms = jnp.mean(x2 * x2, axis=-1, keepdims=True)
        x2 = (x2 * lax.rsqrt(ms + EPS)) * fgain_ref[...]
    o_ref[...] = x2


def _combine(x1, ys, route, modsel, fgain, prev, b0, nb, n_lat_tiles, final):
    b, rows, d = x1.shape
    n_t = rows // TILE
    tok = lambda bi, ti: (b0 + bi, ti, 0)
    in_specs = [pl.BlockSpec((None, TILE, d), tok),
                pl.BlockSpec((TILE, d // 2), lambda bi, ti: (bi * n_t + ti, 0)),
                pl.BlockSpec((TILE, d // 2), lambda bi, ti: ((nb + bi) * n_t + ti, 0)),
                pl.BlockSpec((None, TILE, LANES), tok),
                pl.BlockSpec((None, 6, d),
                             lambda bi, ti: (2 * (b0 + bi) + (ti >= n_lat_tiles).astype(jnp.int32), 0, 0)),
                pl.BlockSpec((1, d), lambda bi, ti: (0, 0))]
    args = [x1, ys, ys, route, modsel, fgain]
    aliases = {}
    if prev is not None:
        in_specs.append(pl.BlockSpec(memory_space=pl.ANY))
        args.append(prev)
        aliases = {len(args) - 1: 0}
    return pl.pallas_call(
        functools.partial(_combine_kernel, final=final),
        out_shape=jax.ShapeDtypeStruct((b, rows, d), F32),
        grid=(nb, n_t),
        in_specs=in_specs,
        out_specs=pl.BlockSpec((None, TILE, d), tok),
        input_output_aliases=aliases,
        compiler_params=_cparams(2),
        name="moe_combine",
    )(*args)


def _routed_moe(tok, route, cnt, x1, wg, wu, wd, layer, modsel, fgain, n_lat_tiles, final):
    b, rows, d = x1.shape
    n_groups = cnt.shape[0] // 8
    nb = b // n_groups
    n = nb * rows
    flat = route.reshape(b * rows, LANES)
    tok_flat = tok.reshape(b * rows, tok.shape[2])
    out = None
    for g in range(n_groups):
        part = flat[g * n:(g + 1) * n]
        idx = jnp.concatenate([part[:, 0], part[:, 1]]).astype(jnp.int32)
        xs = _sc_scatter_rows(tok_flat, g * n, n, idx, N_EXPERTS * n)

        counts = cnt[8 * g, N_GROUPS:N_GROUPS + N_EXPERTS].astype(jnp.int32)
        tiles = (counts + MOE_TILE - 1) // MOE_TILE
        ends = jnp.cumsum(tiles)
        n_sched = 2 * n // MOE_TILE + N_EXPERTS
        j = jnp.minimum(jnp.arange(n_sched, dtype=jnp.int32), ends[-1] - 1)
        exp = jnp.sum((j[:, None] >= ends[None, :]).astype(jnp.int32), axis=1)
        blk = exp * (n // MOE_TILE) + j - (ends - tiles)[exp]

        ys = _expert_ffn(xs, blk, exp, wg, wu, wd, layer)
        yg = _sc_gather_rows(ys, idx)
        out = _combine(x1, yg, route, modsel, fgain, out, g * nb, nb, n_lat_tiles, final)
    return out


def _rope_tables(n_lat):
    t = jnp.arange(n_lat)
    row = (t // GRID_W).astype(F32)
    col = (t % GRID_W).astype(F32)

    def cs(dim):
        quarter = dim // 4
        freqs = ROPE_THETA ** (-jnp.arange(quarter, dtype=F32) / quarter)
        ang = jnp.concatenate([row[:, None] * freqs, col[:, None] * freqs], axis=-1)
        cos = jnp.tile(jnp.cos(ang), (1, 2 * LANES // dim))
        sin = jnp.tile(jnp.sin(ang), (1, 2 * LANES // dim))
        cos = jnp.concatenate([cos, jnp.ones((CTX_LEN, LANES), F32)], axis=0)
        sin = jnp.concatenate([sin, jnp.zeros((CTX_LEN, LANES), F32)], axis=0)
        return cos, sin

    cos_b, sin_b = cs(DIFF_QK_DIM)
    cos_c, sin_c = cs(HEAD_DIM)
    return jnp.concatenate([cos_b, sin_b, cos_c, sin_c], axis=1)


def _reordered_w_in(w_in):
    o_c = 3 * W_A + 3 * W_B
    heads = [w_in[:, o_c + h * HEAD_DIM:o_c + (h + 1) * HEAD_DIM] for h in GQA_Q_ORDER]
    return jnp.concatenate([w_in[:, :o_c]] + heads + [w_in[:, o_c + W_C:]], axis=1).astype(BF16)


def kernel(x, c, ctx, c_ctx, w_mod, b_mod, norm_attn, norm_ffn, w_in, w_out, na_rpb, diff_lambda_q1, diff_lambda_k1, diff_lambda_q2, diff_lambda_k2, diff_subln, gqa_q_norm, gqa_k_norm, router_group_w, router_group_b, router_expert_w, router_expert_b, w_gate, w_up, w_down, final_norm):
    b, s, d = x.shape
    assert d == D_MODEL and ctx.shape[1] == CTX_LEN and s % (NA_QROWS * GRID_W) == 0
    rows = s // GRID_W
    assert rows >= 2 * NA_QROWS
    t_all = s + CTX_LEN
    n_lat_tiles = s // TILE

    c_rows = jnp.zeros((8, d), F32).at[:b].set(c).at[b].set(c_ctx)
    mod = _modulation(c_rows, w_mod, b_mod)

    tab = _rope_tables(s)
    hidx = np.arange(HEAD_DIM)
    partner = np.where(hidx < HEAD_DIM // 2, hidx + HEAD_DIM // 2, hidx - HEAD_DIM // 2)
    blk = np.arange(W_C) // HEAD_DIM
    ones = jnp.asarray((blk[:, None] == blk[None, :]).astype(np.float32), BF16)
    dummy_aux = jnp.zeros((8, LANES), F32)

    x_lat, x_ctx, ctx_blk = x, ctx, 0
    for l in range(DEPTH):
        ctx_out = l < DEPTH - 1
        lam_init = 0.8 - 0.6 * math.exp(-0.3 * l)
        m_lat = mod[l, :b].reshape(b, 1, 6, d)
        m_ctx = jnp.broadcast_to(mod[l, b].reshape(1, 1, 6, d), (b, 1, 6, d))
        modsel = jnp.concatenate([m_lat, m_ctx], axis=1).reshape(2 * b, 6, d)

        gq = jnp.stack([jnp.tile(gqa_q_norm[l], GQA_Q_HEADS), jnp.tile(gqa_q_norm[l][partner], GQA_Q_HEADS)])
        gk = jnp.stack([jnp.tile(gqa_k_norm[l], GQA_KV_HEADS), jnp.tile(gqa_k_norm[l][partner], GQA_KV_HEADS)])
        qa, ka, va, qb, kb, vb, qc, kc, vc = _in_projection(
            x_lat, x_ctx, ctx_blk, modsel, norm_attn[l][None], _reordered_w_in(w_in[l]), tab, gq, gk, ones,
            n_lat_tiles)

        n_qt = n_lat_tiles + 1 if ctx_out else n_lat_tiles
        ctx_tile = n_lat_tiles if ctx_out else None
        oa = _neighbourhood_attention(qa, ka, va, _na_bias_table(na_rpb[l], rows), s)
        if ctx_out:
            oa_ctx = _flash(qa, ka, va, dummy_aux, n_qblk=1, n_sub=2, n_hp=NA_HEADS // 2,
                            qt_off=n_lat_tiles, n_qt=1, n_lat=s, ctx_tile=n_lat_tiles,
                            mode="plain")
            oa = jnp.concatenate([oa, oa_ctx], axis=1)
        pad = lambda v: jnp.pad(v, (0, LANES - v.shape[0]))
        aux = jnp.stack([pad(diff_lambda_q1[l]), pad(diff_lambda_k1[l]), pad(diff_lambda_q2[l]),
                         pad(diff_lambda_k2[l]), jnp.tile(diff_subln[l], 2),
                         jnp.zeros((LANES,), F32), jnp.zeros((LANES,), F32), jnp.zeros((LANES,), F32)])
        ob = _flash(qb, kb, vb, aux, n_qblk=1, n_sub=4, n_hp=DIFF_HEADS // 2, qt_off=0, n_qt=n_qt,
                    n_lat=s, ctx_tile=ctx_tile, mode="diff", lam_init=lam_init)
        oc = _flash(qc, kc, vc, dummy_aux, n_qblk=3, n_sub=2, n_hp=1, qt_off=0, n_qt=n_qt,
                    n_lat=s, ctx_tile=ctx_tile, mode="plain")

        w_o = w_out[l]
        o_c = W_A + W_B
        w_oc = jnp.concatenate([w_o[o_c + h * HEAD_DIM:o_c + (h + 1) * HEAD_DIM] for h in GQA_Q_ORDER], axis=0)
        wr = jnp.zeros((d, LANES), F32)
        wr = wr.at[:, :N_GROUPS].set(router_group_w[l]).at[:, N_GROUPS:N_GROUPS + N_EXPERTS].set(router_expert_w[l])
        wrh, wrl = _split_bf16(wr)
        br = jnp.zeros((1, LANES), F32)
        br = br.at[0, :N_GROUPS].set(router_group_b[l]).at[0, N_GROUPS:N_GROUPS + N_EXPERTS].set(router_expert_b[l])
        x1, tok, route, cnt = _out_projection(
            x_lat, x_ctx, ctx_blk, oa, ob, oc, w_o[:W_A].astype(BF16), w_o[W_A:W_A + W_B].astype(BF16),
            w_oc.astype(BF16), modsel, norm_ffn[l][None], wrh, wrl, br, n_qt, n_lat_tiles)
        xs = _routed_moe(tok, route, cnt, x1, w_gate, w_up, w_down, l, modsel, final_norm[None],
                         n_lat_tiles, final=not ctx_out)
        x_lat, x_ctx, ctx_blk = xs, xs, n_lat_tiles
    return xs
```

```python
import functools
import math

import numpy as np
import jax
import jax.numpy as jnp
from jax import lax
from jax.experimental import pallas as pl
from jax.experimental.pallas import tpu as pltpu
from jax.experimental.pallas import tpu_sc as plsc

F32 = jnp.float32
BF16 = jnp.bfloat16

D_MODEL = 1024
DEPTH = 2
GRID_W = 64
CTX_LEN = 256
HEAD_DIM = 64
NA_HEADS = 6
NA_WIN_H = 8
NA_WIN_W = 16
DIFF_HEADS = 4
DIFF_QK_DIM = 32
GQA_Q_HEADS = 6
GQA_KV_HEADS = 2
N_GROUPS = 4
EXPERTS_PER_GROUP = 4
N_EXPERTS = 16
EXPERT_HIDDEN = 512
ROPE_THETA = 10000.0
EPS = 1e-6
W_A = NA_HEADS * HEAD_DIM
W_B = DIFF_HEADS * 2 * DIFF_QK_DIM
W_C = GQA_Q_HEADS * HEAD_DIM
W_KC = GQA_KV_HEADS * HEAD_DIM
IN_WIDTH = 3 * W_A + 3 * W_B + W_C + 2 * W_KC

LANES = 128
TILE = CTX_LEN
NA_QROWS = 8
NA_KROWS = 16
NA_PARTS = 2
NEG = -1e30
LOG2E = 1.4426950408889634
HI16 = -65536
VMEM_LIMIT = 56 * 1024 * 1024
FLASH_TK = 512
PAIRS_PER_STEP = 2
FLASH_UNROLL_MAX_ROWS = 1024
SUBLANES = 8
MOE_TILE = 512
MOE_GROUPS = 2
SC_ROWS = 32
SC_BUFS = 4
SC_CORES = 2
SC_SUBCORES = 16

GQA_Q_ORDER = (0, 3, 1, 4, 2, 5)


def _cparams(n_axes):
    return pltpu.CompilerParams(dimension_semantics=("arbitrary",) * n_axes,
                                vmem_limit_bytes=VMEM_LIMIT)


def _split_bf16(a):
    hi = a.astype(BF16)
    lo = (a - hi.astype(F32)).astype(BF16)
    return hi, lo


def _dot(a, b):
    return jnp.dot(a, b, preferred_element_type=F32)


def _pack_bf16_pairs(t):
    bits = lax.bitcast_convert_type(t.astype(BF16).astype(F32), jnp.int32)
    half_d = bits.shape[1] // 2
    return lax.shift_right_logical(bits[:, :half_d], 16) | (bits[:, half_d:] & HI16)


def _unpack_bf16_pairs(w):
    return jnp.concatenate([lax.bitcast_convert_type(lax.shift_left(w, 16), F32),
                            lax.bitcast_convert_type(w & HI16, F32)], axis=1)


def _dot_nt(a, b):
    return lax.dot_general(a, b, (((1,), (1,)), ((), ())), preferred_element_type=F32)


def _mod_kernel(c_ref, w_ref, b_ref, o_ref):
    c = c_ref[...]
    a = c * jax.nn.sigmoid(c)
    a_hi, a_lo = _split_bf16(a)
    w_hi, w_lo = _split_bf16(w_ref[...])
    o_ref[...] = _dot(a_hi, w_hi) + _dot(a_lo, w_hi) + _dot(a_hi, w_lo) + b_ref[...]


def _modulation(c_rows, w_mod, b_mod):
    depth, d, n = w_mod.shape
    bn = 1536
    return pl.pallas_call(
        _mod_kernel,
        out_shape=jax.ShapeDtypeStruct((depth, SUBLANES, n), F32),
        grid=(depth, n // bn),
        in_specs=[pl.BlockSpec((SUBLANES, d), lambda l, j: (0, 0)),
                  pl.BlockSpec((None, d, bn), lambda l, j: (l, 0, j)),
                  pl.BlockSpec((None, 1, bn), lambda l, j: (l, 0, j))],
        out_specs=pl.BlockSpec((None, SUBLANES, bn), lambda l, j: (l, 0, j)),
        compiler_params=_cparams(2),
        name="adaln_mod",
    )(c_rows, w_mod, b_mod.reshape(depth, 1, n))


def _head_mean_sq(t, ones):
    hi, lo = _split_bf16(t * t)
    return (_dot(hi, ones) + _dot(lo, ones)) * (1.0 / HEAD_DIM)


def _rotate_half(p, head):
    w = p.shape[1]
    half = head // 2
    lane = lax.broadcasted_iota(jnp.int32, (1, w), 1)
    first = (lane & (head - 1)) < half
    from_right = pltpu.roll(p, w - half, 1)
    from_left = pltpu.roll(p, half, 1)
    return jnp.where(first, -from_right, from_left)


def _inproj_kernel(x_ref, xc_ref, mod_ref, gain_ref, w_ref, tab_ref, gq_ref, gk_ref, ones_ref,
                   qa_ref, ka_ref, va_ref, qb_ref, kb_ref, vb_ref, qc_ref, kc_ref, vc_ref,
                   *, n_lat_tiles):
    x = jnp.where(pl.program_id(1) == n_lat_tiles, xc_ref[...], x_ref[...])
    mod = mod_ref[...]
    ms = jnp.mean(x * x, axis=-1, keepdims=True)
    h = (x * lax.rsqrt(ms + EPS)) * gain_ref[...]
    h = h * (1.0 + mod[1:2]) + mod[0:1]
    hb = h.astype(BF16)

    def proj(a, b):
        return _dot(hb, w_ref[:, a:b])

    pa = proj(0, 3 * W_A)
    qa_ref[...] = (pa[:, :W_A] * (HEAD_DIM ** -0.5 * LOG2E)).astype(BF16)
    ka_ref[...] = pa[:, W_A:2 * W_A].astype(BF16)
    va_ref[...] = pa[:, 2 * W_A:].astype(BF16)

    tab = tab_ref[...]
    cos_b = jnp.concatenate([tab[:, 0:LANES]] * 2, axis=1)
    sin_b = jnp.concatenate([tab[:, LANES:2 * LANES]] * 2, axis=1)
    cos_c1 = tab[:, 2 * LANES:3 * LANES]
    sin_c1 = tab[:, 3 * LANES:4 * LANES]
    cos_c = jnp.concatenate([cos_c1] * 3, axis=1)
    sin_c = jnp.concatenate([sin_c1] * 3, axis=1)

    o_b = 3 * W_A
    pb = proj(o_b, o_b + 3 * W_B)
    qb = pb[:, :W_B]
    kb = pb[:, W_B:2 * W_B]
    qb = qb * cos_b + _rotate_half(qb, DIFF_QK_DIM) * sin_b
    qb_ref[...] = (qb * (DIFF_QK_DIM ** -0.5 * LOG2E)).astype(BF16)
    kb_ref[...] = (kb * cos_b + _rotate_half(kb, DIFF_QK_DIM) * sin_b).astype(BF16)
    vb_ref[...] = pb[:, 2 * W_B:].astype(BF16)

    o_c = o_b + 3 * W_B
    pc = proj(o_c, IN_WIDTH)
    ones = ones_ref[...]
    qc = pc[:, :W_C]
    kc = pc[:, W_C:W_C + W_KC]
    nq = lax.rsqrt(_head_mean_sq(qc, ones) + EPS)
    nk = lax.rsqrt(_head_mean_sq(kc, ones[:W_KC, :W_KC]) + EPS)
    gq = gq_ref[...]
    gk = gk_ref[...]
    q = nq * (qc * gq[0:1] * cos_c + _rotate_half(qc, HEAD_DIM) * gq[1:2] * sin_c)
    qc_ref[...] = (q * (HEAD_DIM ** -0.5 * LOG2E)).astype(BF16)
    k = nk * (kc * gk[0:1] * cos_c1 + _rotate_half(kc, HEAD_DIM) * gk[1:2] * sin_c1)
    kc_ref[...] = k.astype(BF16)
    vc_ref[...] = pc[:, W_C + W_KC:].astype(BF16)


def _token_specs(d, n_lat_tiles, ctx_blk):
    return [pl.BlockSpec((None, TILE, d), lambda bi, ti: (bi, jnp.minimum(ti, n_lat_tiles - 1), 0)),
            pl.BlockSpec((None, TILE, d), lambda bi, ti: (bi, ctx_blk, 0))]


def _in_projection(x_lat, x_ctx, ctx_blk, modsel, gain, w_ext, tab, gq, gk, ones, n_lat_tiles):
    b, _, d = x_lat.shape
    n_tiles = n_lat_tiles + 1
    t_all = n_tiles * TILE
    widths = (W_A, W_A, W_A, W_B, W_B, W_B, W_C, W_KC, W_KC)
    tok = lambda bi, ti: (bi, ti, 0)
    const2 = lambda bi, ti: (0, 0)
    return pl.pallas_call(
        functools.partial(_inproj_kernel, n_lat_tiles=n_lat_tiles),
        out_shape=[jax.ShapeDtypeStruct((b, t_all, w), BF16) for w in widths],
        grid=(b, n_tiles),
        in_specs=_token_specs(d, n_lat_tiles, ctx_blk) + [
                  pl.BlockSpec((None, 6, d), lambda bi, ti: (2 * bi + (ti >= n_lat_tiles).astype(jnp.int32), 0, 0)),
                  pl.BlockSpec((1, d), const2),
                  pl.BlockSpec((d, IN_WIDTH), const2),
                  pl.BlockSpec((TILE, 4 * LANES), lambda bi, ti: (ti, 0)),
                  pl.BlockSpec((2, W_C), const2),
                  pl.BlockSpec((2, W_KC), const2),
                  pl.BlockSpec((W_C, W_C), const2)],
        out_specs=[pl.BlockSpec((None, TILE, w), tok) for w in widths],
        compiler_params=_cparams(2),
        name="in_projection",
    )(x_lat, x_ctx, modsel, gain, w_ext, tab, gq, gk, ones)


def _flash_kernel(q_ref, k_ref, v_ref, aux_ref, o_ref, va_ref, vb_ref, qs_ref, acc_ref, m_ref,
                  s0_ref, s1_ref, mb0_ref, mb1_ref, *,
                  n_qblk, n_sub, tk, n_lat_blocks, pairs_per_step, ctx_tile, qt_off, mode, lam_init):
    qt = pl.program_id(2) + qt_off
    sub_w = LANES // n_sub
    half = LANES // 2
    lane = lax.broadcasted_iota(jnp.int32, (1, LANES), 1)
    lower = lane < half
    n_pieces = n_qblk * n_sub
    ma = (n_pieces // 2) * TILE
    m_rows = n_pieces * TILE
    ctx_start = n_lat_blocks * tk

    @pl.when(pl.program_id(2) == 0)
    def _():
        v = v_ref[...].astype(F32)
        va_ref[...] = jnp.where(lower, v, 1.0).astype(BF16)
        vb_ref[...] = jnp.where(lower, 1.0, v).astype(BF16)

    ia, ib = 0, n_pieces // 2
    for blk in range(n_qblk):
        qf = q_ref[:, blk * LANES:(blk + 1) * LANES].astype(F32)
        for sub in range(n_sub):
            msk = (lane >= sub * sub_w) & (lane < (sub + 1) * sub_w)
            piece = jnp.where(msk, qf, 0.0).astype(BF16)
            if sub * sub_w < half:
                qs_ref[ia * TILE:(ia + 1) * TILE, :] = piece
                ia += 1
            else:
                qs_ref[ib * TILE:(ib + 1) * TILE, :] = piece
                ib += 1

    s_bufs = (s0_ref, s1_ref)
    mb_bufs = (mb0_ref, mb1_ref)

    def scores(start, size, slot):
        s = _dot_nt(qs_ref[...], k_ref[pl.ds(start, size), :])
        s_bufs[slot][:, :size] = s
        mb = jnp.max(s, axis=-1, keepdims=True)
        mb_bufs[slot][...] = jnp.broadcast_to(mb, (m_rows, LANES))

    def accumulate(start, size, slot, first):
        mb = mb_bufs[slot][...]
        if first:
            m_new = mb
        else:
            m_old = m_ref[...]
            m_new = jnp.maximum(m_old, mb)
        s_ref = s_bufs[slot]
        cols = [s_ref[:, c * LANES:(c + 1) * LANES] - m_new for c in range(size // LANES)]
        p = jnp.concatenate([jnp.exp2(d.astype(BF16)) for d in cols], axis=1)
        pva = _dot(p[:ma], va_ref[pl.ds(start, size), :])
        pvb = _dot(p[ma:], vb_ref[pl.ds(start, size), :])
        if first:
            acc_ref[:ma, :] = pva
            acc_ref[ma:, :] = pvb
        else:
            alpha = jnp.exp2(m_old - m_new)
            acc_ref[:ma, :] = alpha[:ma] * acc_ref[:ma, :] + pva
            acc_ref[ma:, :] = alpha[ma:] * acc_ref[ma:, :] + pvb
        m_ref[...] = m_new

    def lat(j):
        return pl.multiple_of(j * tk, tk)

    def latent_queries():
        scores(ctx_start, CTX_LEN, 0)
        scores(lat(0), tk, 1)
        accumulate(ctx_start, CTX_LEN, 0, True)

        def pair(i):
            scores(lat(2 * i + 1), tk, 0)
            accumulate(lat(2 * i), tk, 1, False)
            scores(lat(2 * i + 2), tk, 1)
            accumulate(lat(2 * i + 1), tk, 0, False)

        def body(i, carry):
            for u in range(pairs_per_step):
                pair(i * pairs_per_step + u)
            return carry

        n_pairs = (n_lat_blocks - 2) // 2
        n_steps = n_pairs // pairs_per_step
        lax.fori_loop(0, n_steps, body, 0)
        for i in range(n_steps * pairs_per_step, n_pairs):
            pair(i)
        scores(lat(n_lat_blocks - 1), tk, 0)
        accumulate(lat(n_lat_blocks - 2), tk, 1, False)
        accumulate(lat(n_lat_blocks - 1), tk, 0, False)

    def context_queries():
        scores(ctx_start, CTX_LEN, 0)
        accumulate(ctx_start, CTX_LEN, 0, True)

    if ctx_tile is None:
        latent_queries()
    else:
        pl.when(qt != ctx_tile)(latent_queries)
        pl.when(qt == ctx_tile)(context_queries)

    acc = acc_ref[...]
    r = acc / pltpu.roll(acc, half, 1)
    ra, rb = r[:ma], r[ma:]
    if mode == "plain":
        for i in range(n_pieces // 2):
            o = jnp.where(lower, ra[i * TILE:(i + 1) * TILE], rb[i * TILE:(i + 1) * TILE])
            o_ref[:, i * LANES:(i + 1) * LANES] = o.astype(BF16)
    else:
        aux = aux_ref[...]
        l1 = jnp.sum(aux[0:1] * aux[1:2], axis=-1, keepdims=True)
        l2 = jnp.sum(aux[2:3] * aux[3:4], axis=-1, keepdims=True)
        lam = jnp.exp(l1) - jnp.exp(l2) + lam_init
        oa = ra[:TILE] - lam * ra[TILE:]
        ob = rb[:TILE] - lam * rb[TILE:]
        o = jnp.where(lower, oa, ob)
        sq = o * o
        ss_a = jnp.sum(jnp.where(lower, sq, 0.0), axis=-1, keepdims=True)
        ss_b = jnp.sum(jnp.where(lower, 0.0, sq), axis=-1, keepdims=True)
        ms = jnp.where(lower, ss_a, ss_b) * (1.0 / HEAD_DIM)
        o = (o * lax.rsqrt(ms + EPS)) * aux[4:5]
        o_ref[...] = (o * (1.0 - lam_init)).astype(BF16)


def _flash(q, k, v, aux, *, n_qblk, n_sub, n_hp, qt_off, n_qt, n_lat, ctx_tile, mode, lam_init=0.0):
    b, t_all, _ = q.shape
    qw = n_qblk * LANES
    tk = FLASH_TK
    assert n_lat % (2 * tk) == 0 and tk >= CTX_LEN
    m_rows = n_qblk * n_sub * TILE
    n_pairs = max((n_lat // tk - 2) // 2, 1)
    pairs = n_pairs if m_rows <= FLASH_UNROLL_MAX_ROWS else PAIRS_PER_STEP
    kern = functools.partial(_flash_kernel, n_qblk=n_qblk, n_sub=n_sub, tk=tk,
                             n_lat_blocks=n_lat // tk, ctx_tile=ctx_tile, qt_off=qt_off,
                             pairs_per_step=pairs, mode=mode, lam_init=lam_init)
    return pl.pallas_call(
        kern,
        out_shape=jax.ShapeDtypeStruct((b, n_qt * TILE, n_hp * qw), BF16),
        grid=(b, n_hp, n_qt),
        in_specs=[pl.BlockSpec((None, TILE, qw), lambda bi, hp, qt: (bi, qt + qt_off, hp)),
                  pl.BlockSpec((None, t_all, LANES), lambda bi, hp, qt: (bi, 0, hp)),
                  pl.BlockSpec((None, t_all, LANES), lambda bi, hp, qt: (bi, 0, hp)),
                  pl.BlockSpec((SUBLANES, LANES), lambda bi, hp, qt: (0, 0))],
        out_specs=pl.BlockSpec((None, TILE, qw), lambda bi, hp, qt: (bi, qt, hp)),
        scratch_shapes=[pltpu.VMEM((t_all, LANES), BF16),
                        pltpu.VMEM((t_all, LANES), BF16),
                        pltpu.VMEM((m_rows, LANES), BF16),
                        pltpu.VMEM((m_rows, LANES), F32),
                        pltpu.VMEM((m_rows, LANES), F32),
                        pltpu.VMEM((m_rows, tk), F32),
                        pltpu.VMEM((m_rows, tk), F32),
                        pltpu.VMEM((m_rows, LANES), F32),
                        pltpu.VMEM((m_rows, LANES), F32)],
        compiler_params=_cparams(3),
        name="flash_" + mode,
    )(q, k, v, aux)


def _na_kernel(q_ref, k0, k1, k2, k3, v0, v1, v2, v3, kc_ref, vc_ref, bias_ref, o_ref, s_ref, m_ref):
    lane = lax.broadcasted_iota(jnp.int32, (1, LANES), 1)
    lower = lane < LANES // 2
    qf = q_ref[...].astype(F32)
    k_all = jnp.concatenate([k0[...], k1[...], k2[...], k3[...], kc_ref[...]], axis=0)
    v_all = jnp.concatenate([v0[...], v1[...], v2[...], v3[...], vc_ref[...]], axis=0).astype(F32)
    v_h = [jnp.where(lower, v_all, 1.0).astype(BF16), jnp.where(lower, 1.0, v_all).astype(BF16)]
    q_h = [jnp.where(lower, qf, 0.0).astype(BF16), jnp.where(lower, 0.0, qf).astype(BF16)]
    n_pair = NA_KROWS // 2
    rows_per_part = NA_QROWS // NA_PARTS
    half_q = rows_per_part * GRID_W
    no_bias = jnp.zeros((GRID_W, CTX_LEN), F32)

    def scores(part):
        rows = slice(part * half_q, (part + 1) * half_q)
        qs = jnp.concatenate([q_h[0][rows], q_h[1][rows]], axis=0)
        bias = jnp.concatenate(
            [jnp.concatenate([bias_ref[hh, a * n_pair + j] for j in range(n_pair)] + [no_bias], axis=1)
             for hh in range(2) for a in range(part * rows_per_part, (part + 1) * rows_per_part)],
            axis=0)
        s = _dot_nt(qs, k_all) + bias
        s_ref[part] = s
        m_ref[part] = jnp.broadcast_to(jnp.max(s, axis=-1, keepdims=True), (2 * half_q, LANES))

    def finish(part):
        s = s_ref[part]
        m = m_ref[part]
        p = jnp.concatenate([jnp.exp2((s[:, c * LANES:(c + 1) * LANES] - m).astype(BF16))
                             for c in range(s.shape[1] // LANES)], axis=1)
        o0 = _dot(p[:half_q], v_h[0])
        o1 = _dot(p[half_q:], v_h[1])
        o0 = o0 / pltpu.roll(o0, LANES // 2, 1)
        o1 = o1 / pltpu.roll(o1, LANES // 2, 1)
        o_ref[part * half_q:(part + 1) * half_q, :] = jnp.where(lower, o0, o1).astype(BF16)

    scores(0)
    for part in range(1, NA_PARTS):
        scores(part)
        finish(part - 1)
    finish(NA_PARTS - 1)


def _neighbourhood_attention(qa, ka, va, bias, n_lat):
    b = qa.shape[0]
    q_tok = NA_QROWS * GRID_W
    v_tok = q_tok // 2
    n_rb = n_lat // q_tok
    n_view = n_lat // v_tok
    ctx_blk = n_lat // v_tok

    def view(j):
        return lambda rb, hp, bi: (bi, jnp.clip(2 * rb - 1 + j, 0, n_view - 1), hp)

    kv_specs = [pl.BlockSpec((None, v_tok, LANES), view(j)) for j in range(4)]
    ctx_spec = pl.BlockSpec((None, CTX_LEN, LANES), lambda rb, hp, bi: (bi, ctx_blk, hp))

    def bias_map(rb, hp, bi):
        pat = jnp.where(rb == 0, 0, jnp.where(rb == n_rb - 1, 2, 1))
        return (hp, pat, 0, 0, 0)

    return pl.pallas_call(
        _na_kernel,
        out_shape=jax.ShapeDtypeStruct((b, n_lat, W_A), BF16),
        grid=(n_rb, NA_HEADS // 2, b),
        in_specs=[pl.BlockSpec((None, q_tok, LANES), lambda rb, hp, bi: (bi, rb, hp))]
                 + kv_specs + kv_specs + [ctx_spec, ctx_spec,
                 pl.BlockSpec((2, None, NA_QROWS * NA_KROWS // 2, GRID_W, 2 * GRID_W), bias_map)],
        out_specs=pl.BlockSpec((None, q_tok, LANES), lambda rb, hp, bi: (bi, rb, hp)),
        scratch_shapes=[pltpu.VMEM((NA_PARTS, 2 * q_tok // NA_PARTS, NA_KROWS * GRID_W + CTX_LEN), F32),
                        pltpu.VMEM((NA_PARTS, 2 * q_tok // NA_PARTS, LANES), F32)],
        compiler_params=_cparams(3),
        name="neighbourhood_attention",
    )(qa, ka, ka, ka, ka, va, va, va, va, ka, va, bias)


def _na_bias_table(rpb, rows):
    cols = np.arange(GRID_W)
    c0 = np.clip(cols - NA_WIN_W // 2, 0, GRID_W - NA_WIN_W)
    cc = cols[None, :]
    col_ok = (cc >= c0[:, None]) & (cc < c0[:, None] + NA_WIN_W)
    dc = np.clip(cc - cols[:, None] + (NA_WIN_W - 1), 0, 2 * NA_WIN_W - 2)
    e = jnp.where(col_ok[None, None], (rpb.astype(F32) * LOG2E)[:, :, dc], NEG)
    e = jnp.concatenate([e, jnp.full_like(e[:, :1], NEG)], axis=1)
    a = np.arange(NA_QROWS)[:, None]
    i = np.arange(NA_KROWS)[None, :]
    pats = []
    for r_base in (0, NA_QROWS, rows - NA_QROWS):
        r = r_base + a
        key_row = r_base - NA_WIN_H // 2 + i
        r0 = np.clip(r - NA_WIN_H // 2, 0, rows - NA_WIN_H)
        ok = (key_row >= r0) & (key_row < r0 + NA_WIN_H) & (key_row >= 0) & (key_row < rows)
        dr = np.where(ok, key_row - r + (NA_WIN_H - 1), 2 * NA_WIN_H - 1)
        pats.append(dr)
    dr_all = np.stack(pats)
    pairs = dr_all.reshape(-1, 2)
    uniq, inv = np.unique(pairs, axis=0, return_inverse=True)
    pair_blocks = jnp.concatenate([e[:, uniq[:, 0]], e[:, uniq[:, 1]]], axis=-1)
    t = pair_blocks[:, inv.reshape(-1)]
    return t.reshape(NA_HEADS, 3, NA_QROWS * NA_KROWS // 2, GRID_W, 2 * GRID_W)


def _outproj_kernel(x_ref, xc_ref, oa_ref, ob_ref, oc_ref, wa_ref, wb_ref, wc_ref, mod_ref, gain_ref,
                    wrh_ref, wrl_ref, br_ref, tri_ref, x1_ref, tok_ref, route_ref, cnt_ref, run_ref,
                    *, region, group_batches, n_lat_tiles):
    mod = mod_ref[...]
    y = _dot(oa_ref[...], wa_ref[...]) + _dot(ob_ref[...], wb_ref[...]) + _dot(oc_ref[...], wc_ref[...])
    x1 = jnp.where(pl.program_id(1) == n_lat_tiles, xc_ref[...], x_ref[...]) + mod[2:3] * y
    x1_ref[...] = x1
    ms = jnp.mean(x1 * x1, axis=-1, keepdims=True)
    t = (x1 * lax.rsqrt(ms + EPS)) * gain_ref[...]
    t = t * (1.0 + mod[4:5]) + mod[3:4]
    tok_ref[...] = _pack_bf16_pairs(t)

    t_hi, t_lo = _split_bf16(t)
    wrh = wrh_ref[...]
    logits = _dot(t_hi, wrh) + _dot(t_lo, wrh) + _dot(t_hi, wrl_ref[...]) + br_ref[...]

    lane = lax.broadcasted_iota(jnp.int32, logits.shape, 1)
    lane_f = lane.astype(F32)
    is_g = lane < N_GROUPS
    gl = jnp.where(is_g, logits, NEG)
    gmax = jnp.max(gl, axis=-1, keepdims=True)
    g_sel = jnp.min(jnp.where(gl == gmax, lane_f, 1e9), axis=-1, keepdims=True)
    p_grp = 1.0 / jnp.sum(jnp.where(is_g, jnp.exp(gl - gmax), 0.0), axis=-1, keepdims=True)
    grp_of_lane = lax.shift_right_arithmetic(lane - N_GROUPS, 2).astype(F32)
    in_grp = (lane >= N_GROUPS) & (lane < N_GROUPS + N_EXPERTS) & (grp_of_lane == g_sel)
    el = jnp.where(in_grp, logits, NEG)
    v1 = jnp.max(el, axis=-1, keepdims=True)
    i1 = jnp.min(jnp.where(el == v1, lane_f, 1e9), axis=-1, keepdims=True)
    el2 = jnp.where(lane_f == i1, NEG, el)
    v2 = jnp.max(el2, axis=-1, keepdims=True)
    i2 = jnp.min(jnp.where(el2 == v2, lane_f, 1e9), axis=-1, keepdims=True)
    e2 = jnp.exp(v2 - v1)
    den = 1.0 + e2
    w1 = p_grp / den
    w2 = p_grp * e2 / den

    @pl.when((lax.rem(pl.program_id(0), group_batches) == 0) & (pl.program_id(1) == 0))
    def _():
        run_ref[...] = jnp.zeros(run_ref.shape, F32)

    ind = jnp.where(lane_f == i1, 1.0, 0.0) + jnp.where(lane_f == i2, 1.0, 0.0)
    rank = _dot(tri_ref[...], ind.astype(BF16)) + run_ref[0:1, :]

    def pick(m, l):
        return jnp.sum(jnp.where(lane_f == l, m, 0.0), axis=-1, keepdims=True)

    pos1 = (i1 - N_GROUPS) * region + pick(rank, i1)
    pos2 = (i2 - N_GROUPS) * region + pick(rank, i2)
    route_ref[...] = jnp.where(lane == 0, pos1, jnp.where(lane == 1, pos2,
                               jnp.where(lane == 2, w1, jnp.where(lane == 3, w2, 0.0))))
    run = run_ref[...] + jnp.sum(ind, axis=0, keepdims=True)
    run_ref[...] = run
    cnt_ref[...] = run


def _out_projection(x_lat, x_ctx, ctx_blk, oa, ob, oc, wa, wb, wc, modsel, gain, wrh, wrl, br, n_tiles,
                    n_lat_tiles):
    b, _, d = x_lat.shape
    tok = lambda bi, ti: (bi, ti, 0)
    const2 = lambda bi, ti: (0, 0)
    rows = n_tiles * TILE
    nb = b // MOE_GROUPS if b % MOE_GROUPS == 0 else b
    tri = jnp.asarray(np.tril(np.ones((TILE, TILE), np.float32), -1), BF16)
    return pl.pallas_call(
        functools.partial(_outproj_kernel, region=nb * rows, group_batches=nb, n_lat_tiles=n_lat_tiles),
        out_shape=[jax.ShapeDtypeStruct((b, rows, d), F32),
                   jax.ShapeDtypeStruct((b, rows, d // 2), jnp.int32),
                   jax.ShapeDtypeStruct((b, rows, LANES), F32),
                   jax.ShapeDtypeStruct((SUBLANES * (b // nb), LANES), F32)],
        grid=(b, n_tiles),
        in_specs=_token_specs(d, n_lat_tiles, ctx_blk) + [
                  pl.BlockSpec((None, TILE, W_A), tok),
                  pl.BlockSpec((None, TILE, W_B), tok),
                  pl.BlockSpec((None, TILE, W_C), tok),
                  pl.BlockSpec((W_A, d), const2),
                  pl.BlockSpec((W_B, d), const2),
                  pl.BlockSpec((W_C, d), const2),
                  pl.BlockSpec((None, 6, d), lambda bi, ti: (2 * bi + (ti >= n_lat_tiles).astype(jnp.int32), 0, 0)),
                  pl.BlockSpec((1, d), const2),
                  pl.BlockSpec((d, LANES), const2),
                  pl.BlockSpec((d, LANES), const2),
                  pl.BlockSpec((1, LANES), const2),
                  pl.BlockSpec((TILE, TILE), const2)],
        out_specs=[pl.BlockSpec((None, TILE, d), tok),
                   pl.BlockSpec((None, TILE, d // 2), tok),
                   pl.BlockSpec((None, TILE, LANES), tok),
                   pl.BlockSpec((SUBLANES, LANES), lambda bi, ti: (bi // nb, 0))],
        scratch_shapes=[pltpu.VMEM((SUBLANES, LANES), F32)],
        compiler_params=_cparams(2),
        name="out_projection",
    )(x_lat, x_ctx, oa, ob, oc, wa, wb, wc, modsel, gain, wrh, wrl, br, tri)


def _sc_mesh():
    return plsc.VectorSubcoreMesh(core_axis_name="core", subcore_axis_name="subcore")


def _sc_worker_base(per_worker):
    wid = lax.axis_index("subcore") * SC_CORES + lax.axis_index("core")
    return wid * per_worker


def _sc_scratch(d, dtype):
    return ([pltpu.VMEM((SC_ROWS,), jnp.int32)] * SC_BUFS + [pltpu.VMEM((SC_ROWS, d), dtype)] * SC_BUFS
            + [pltpu.SemaphoreType.DMA] * (2 * SC_BUFS))


def _sc_split(scratch):
    return (scratch[:SC_BUFS], scratch[SC_BUFS:2 * SC_BUFS], scratch[2 * SC_BUFS:3 * SC_BUFS],
            scratch[3 * SC_BUFS:])


def _sc_chunk_loop(per_worker, group):
    chunks = per_worker // SC_ROWS
    full = chunks // SC_BUFS * SC_BUFS

    @pl.loop(0, full, step=SC_BUFS)
    def _(c):
        group(c, SC_BUFS)

    if chunks > full:
        group(full, chunks - full)


def _sc_scatter_rows(x, row_off, n, idx, n_out):
    d = x.shape[1]
    per_worker = 2 * n // (SC_CORES * SC_SUBCORES)
    assert per_worker % SC_ROWS == 0 and n % SC_ROWS == 0

    @functools.partial(pl.kernel, out_type=jax.ShapeDtypeStruct((n_out, d), x.dtype),
                       mesh=_sc_mesh(), scratch_types=_sc_scratch(d, x.dtype))
    def scatter(x_hbm, i_hbm, o_hbm, *scratch):
        idx_v, rows_v, sem_in, sem_out = _sc_split(scratch)
        base = _sc_worker_base(per_worker)

        def group(c, n_bufs):
            reads = []
            for u in range(n_bufs):
                a = pl.multiple_of(base + (c + u) * SC_ROWS, SC_ROWS)
                t = pl.multiple_of(row_off + lax.rem(a, n), SC_ROWS)
                pltpu.sync_copy(i_hbm.at[pl.ds(a, SC_ROWS)], idx_v[u])
                reads.append(pltpu.async_copy(x_hbm.at[pl.ds(t, SC_ROWS)], rows_v[u], sem_in[u]))
            writes = []
            for u in range(n_bufs):
                reads[u].wait()
                writes.append(pltpu.async_copy(rows_v[u], o_hbm.at[idx_v[u]], sem_out[u]))
            for w in writes:
                w.wait()

        _sc_chunk_loop(per_worker, group)

    return scatter(x, idx)


def _sc_gather_rows(src, idx):
    m = idx.shape[0]
    d = src.shape[1]
    per_worker = m // (SC_CORES * SC_SUBCORES)
    assert per_worker % SC_ROWS == 0

    @functools.partial(pl.kernel, out_type=jax.ShapeDtypeStruct((m, d), src.dtype),
                       mesh=_sc_mesh(), scratch_types=_sc_scratch(d, src.dtype))
    def gather(s_hbm, i_hbm, o_hbm, *scratch):
        idx_v, rows_v, sem_in, sem_out = _sc_split(scratch)
        base = _sc_worker_base(per_worker)

        def group(c, n_bufs):
            offs, reads = [], []
            for u in range(n_bufs):
                a = pl.multiple_of(base + (c + u) * SC_ROWS, SC_ROWS)
                offs.append(a)
                pltpu.sync_copy(i_hbm.at[pl.ds(a, SC_ROWS)], idx_v[u])
                reads.append(pltpu.async_copy(s_hbm.at[idx_v[u]], rows_v[u], sem_in[u]))
            writes = []
            for u in range(n_bufs):
                reads[u].wait()
                writes.append(pltpu.async_copy(rows_v[u], o_hbm.at[pl.ds(offs[u], SC_ROWS)], sem_out[u]))
            for w in writes:
                w.wait()

        _sc_chunk_loop(per_worker, group)

    return gather(src, idx)


def _expert_ffn_kernel(blk_ref, exp_ref, x_ref, wg_ref, wu_ref, wd_ref, y_ref, wgb_ref, wub_ref, wdb_ref):
    j = pl.program_id(0)

    @pl.when((j == 0) | (exp_ref[j] != exp_ref[jnp.maximum(j - 1, 0)]))
    def _():
        wgb_ref[...] = wg_ref[...].astype(BF16)
        wub_ref[...] = wu_ref[...].astype(BF16)
        wdb_ref[...] = wd_ref[...].astype(BF16)

    x = _unpack_bf16_pairs(x_ref[...]).astype(BF16)
    hid = jax.nn.silu(_dot(x, wgb_ref[...])) * _dot(x, wub_ref[...])
    y_ref[...] = _pack_bf16_pairs(_dot(hid.astype(BF16), wdb_ref[...]))


def _expert_ffn(xs, blk, exp, wg, wu, wd, layer):
    rows, d_packed = xs.shape
    d = 2 * d_packed
    w_map = lambda j, blk, exp: (layer, exp[j], 0, 0)
    return pl.pallas_call(
        _expert_ffn_kernel,
        out_shape=jax.ShapeDtypeStruct((rows, d_packed), jnp.int32),
        grid_spec=pltpu.PrefetchScalarGridSpec(
            num_scalar_prefetch=2,
            grid=(blk.shape[0],),
            in_specs=[pl.BlockSpec((MOE_TILE, d_packed), lambda j, blk, exp: (blk[j], 0)),
                      pl.BlockSpec((None, None, d, EXPERT_HIDDEN), w_map),
                      pl.BlockSpec((None, None, d, EXPERT_HIDDEN), w_map),
                      pl.BlockSpec((None, None, EXPERT_HIDDEN, d), w_map)],
            out_specs=pl.BlockSpec((MOE_TILE, d_packed), lambda j, blk, exp: (blk[j], 0)),
            scratch_shapes=[pltpu.VMEM((d, EXPERT_HIDDEN), BF16),
                            pltpu.VMEM((d, EXPERT_HIDDEN), BF16),
                            pltpu.VMEM((EXPERT_HIDDEN, d), BF16)]),
        compiler_params=_cparams(1),
        name="expert_ffn",
    )(blk, exp, xs, wg, wu, wd)


def _combine_kernel(x1_ref, y1_ref, y2_ref, route_ref, mod_ref, fgain_ref, *rest, final):
    o_ref = rest[-1]
    route = route_ref[...]
    y = route[:, 2:3] * _unpack_bf16_pairs(y1_ref[...]) + route[:, 3:4] * _unpack_bf16_pairs(y2_ref[...])
    x2 = x1_ref[...] + mod_ref[5:6, :] * y
    if final:
        ms = jnp.mean(x2 * x2, axis=-1, keepdims=True)
        x2 = (x2 * lax.rsqrt(ms + EPS)) * fgain_ref[...]
    o_ref[...] = x2


def _combine(x1, ys, route, modsel, fgain, prev, b0, nb, n_lat_tiles, final):
    b, rows, d = x1.shape
    n_t = rows // TILE
    tok = lambda bi, ti: (b0 + bi, ti, 0)
    in_specs = [pl.BlockSpec((None, TILE, d), tok),
                pl.BlockSpec((TILE, d // 2), lambda bi, ti: (bi * n_t + ti, 0)),
                pl.BlockSpec((TILE, d // 2), lambda bi, ti: ((nb + bi) * n_t + ti, 0)),
                pl.BlockSpec((None, TILE, LANES), tok),
                pl.BlockSpec((None, 6, d),
                             lambda bi, ti: (2 * (b0 + bi) + (ti >= n_lat_tiles).astype(jnp.int32), 0, 0)),
                pl.BlockSpec((1, d), lambda bi, ti: (0, 0))]
    args = [x1, ys, ys, route, modsel, fgain]
    aliases = {}
    if prev is not None:
        in_specs.append(pl.BlockSpec(memory_space=pl.ANY))
        args.append(prev)
        aliases = {len(args) - 1: 0}
    return pl.pallas_call(
        functools.partial(_combine_kernel, final=final),
        out_shape=jax.ShapeDtypeStruct((b, rows, d), F32),
        grid=(nb, n_t),
        in_specs=in_specs,
        out_specs=pl.BlockSpec((None, TILE, d), tok),
        input_output_aliases=aliases,
        compiler_params=_cparams(2),
        name="moe_combine",
    )(*args)


def _routed_moe(tok, route, cnt, x1, wg, wu, wd, layer, modsel, fgain, n_lat_tiles, final):
    b, rows, d = x1.shape
    n_groups = cnt.shape[0] // SUBLANES
    nb = b // n_groups
    n = nb * rows
    flat = route.reshape(b * rows, LANES)
    tok_flat = tok.reshape(b * rows, tok.shape[2])
    out = None
    for g in range(n_groups):
        part = flat[g * n:(g + 1) * n]
        idx = jnp.concatenate([part[:, 0], part[:, 1]]).astype(jnp.int32)
        xs = _sc_scatter_rows(tok_flat, g * n, n, idx, N_EXPERTS * n)

        counts = cnt[SUBLANES * g, N_GROUPS:N_GROUPS + N_EXPERTS].astype(jnp.int32)
        tiles = (counts + MOE_TILE - 1) // MOE_TILE
        ends = jnp.cumsum(tiles)
        n_sched = 2 * n // MOE_TILE + N_EXPERTS
        j = jnp.minimum(jnp.arange(n_sched, dtype=jnp.int32), ends[-1] - 1)
        exp = jnp.sum((j[:, None] >= ends[None, :]).astype(jnp.int32), axis=1)
        blk = exp * (n // MOE_TILE) + j - (ends - tiles)[exp]

        ys = _expert_ffn(xs, blk, exp, wg, wu, wd, layer)
        yg = _sc_gather_rows(ys, idx)
        out = _combine(x1, yg, route, modsel, fgain, out, g * nb, nb, n_lat_tiles, final)
    return out


def _rope_tables(n_lat):
    t = jnp.arange(n_lat)
    row = (t // GRID_W).astype(F32)
    col = (t % GRID_W).astype(F32)

    def cs(dim):
        quarter = dim // 4
        freqs = ROPE_THETA ** (-jnp.arange(quarter, dtype=F32) / quarter)
        ang = jnp.concatenate([row[:, None] * freqs, col[:, None] * freqs], axis=-1)
        cos = jnp.tile(jnp.cos(ang), (1, 2 * LANES // dim))
        sin = jnp.tile(jnp.sin(ang), (1, 2 * LANES // dim))
        cos = jnp.concatenate([cos, jnp.ones((CTX_LEN, LANES), F32)], axis=0)
        sin = jnp.concatenate([sin, jnp.zeros((CTX_LEN, LANES), F32)], axis=0)
        return cos, sin

    cos_b, sin_b = cs(DIFF_QK_DIM)
    cos_c, sin_c = cs(HEAD_DIM)
    return jnp.concatenate([cos_b, sin_b, cos_c, sin_c], axis=1)


def _reordered_w_in(w_in):
    o_c = 3 * W_A + 3 * W_B
    heads = [w_in[:, o_c + h * HEAD_DIM:o_c + (h + 1) * HEAD_DIM] for h in GQA_Q_ORDER]
    return jnp.concatenate([w_in[:, :o_c]] + heads + [w_in[:, o_c + W_C:]], axis=1).astype(BF16)


def kernel(x, c, ctx, c_ctx, w_mod, b_mod, norm_attn, norm_ffn, w_in, w_out, na_rpb, diff_lambda_q1, diff_lambda_k1, diff_lambda_q2, diff_lambda_k2, diff_subln, gqa_q_norm, gqa_k_norm, router_group_w, router_group_b, router_expert_w, router_expert_b, w_gate, w_up, w_down, final_norm):
    b, s, d = x.shape
    assert d == D_MODEL and ctx.shape[1] == CTX_LEN and s % (NA_QROWS * GRID_W) == 0
    rows = s // GRID_W
    assert rows >= 2 * NA_QROWS
    t_all = s + CTX_LEN
    n_lat_tiles = s // TILE

    assert b + 1 <= SUBLANES
    c_rows = jnp.zeros((SUBLANES, d), F32).at[:b].set(c).at[b].set(c_ctx)
    mod = _modulation(c_rows, w_mod, b_mod)

    tab = _rope_tables(s)
    hidx = np.arange(HEAD_DIM)
    partner = np.where(hidx < HEAD_DIM // 2, hidx + HEAD_DIM // 2, hidx - HEAD_DIM // 2)
    blk = np.arange(W_C) // HEAD_DIM
    ones = jnp.asarray((blk[:, None] == blk[None, :]).astype(np.float32), BF16)
    dummy_aux = jnp.zeros((SUBLANES, LANES), F32)

    x_lat, x_ctx, ctx_blk = x, ctx, 0
    for l in range(DEPTH):
        ctx_out = l < DEPTH - 1
        lam_init = 0.8 - 0.6 * math.exp(-0.3 * l)
        m_lat = mod[l, :b].reshape(b, 1, 6, d)
        m_ctx = jnp.broadcast_to(mod[l, b].reshape(1, 1, 6, d), (b, 1, 6, d))
        modsel = jnp.concatenate([m_lat, m_ctx], axis=1).reshape(2 * b, 6, d)

        gq = jnp.stack([jnp.tile(gqa_q_norm[l], GQA_Q_HEADS), jnp.tile(gqa_q_norm[l][partner], GQA_Q_HEADS)])
        gk = jnp.stack([jnp.tile(gqa_k_norm[l], GQA_KV_HEADS), jnp.tile(gqa_k_norm[l][partner], GQA_KV_HEADS)])
        qa, ka, va, qb, kb, vb, qc, kc, vc = _in_projection(
            x_lat, x_ctx, ctx_blk, modsel, norm_attn[l][None], _reordered_w_in(w_in[l]), tab, gq, gk, ones,
            n_lat_tiles)

        n_qt = n_lat_tiles + 1 if ctx_out else n_lat_tiles
        ctx_tile = n_lat_tiles if ctx_out else None
        oa = _neighbourhood_attention(qa, ka, va, _na_bias_table(na_rpb[l], rows), s)
        if ctx_out:
            oa_ctx = _flash(qa, ka, va, dummy_aux, n_qblk=1, n_sub=2, n_hp=NA_HEADS // 2,
                            qt_off=n_lat_tiles, n_qt=1, n_lat=s, ctx_tile=n_lat_tiles,
                            mode="plain")
            oa = jnp.concatenate([oa, oa_ctx], axis=1)
        pad = lambda v: jnp.pad(v, (0, LANES - v.shape[0]))
        aux = jnp.stack([pad(diff_lambda_q1[l]), pad(diff_lambda_k1[l]), pad(diff_lambda_q2[l]),
                         pad(diff_lambda_k2[l]), jnp.tile(diff_subln[l], 2),
                         jnp.zeros((LANES,), F32), jnp.zeros((LANES,), F32), jnp.zeros((LANES,), F32)])
        ob = _flash(qb, kb, vb, aux, n_qblk=1, n_sub=4, n_hp=DIFF_HEADS // 2, qt_off=0, n_qt=n_qt,
                    n_lat=s, ctx_tile=ctx_tile, mode="diff", lam_init=lam_init)
        oc = _flash(qc, kc, vc, dummy_aux, n_qblk=3, n_sub=2, n_hp=1, qt_off=0, n_qt=n_qt,
                    n_lat=s, ctx_tile=ctx_tile, mode="plain")

        w_o = w_out[l]
        o_c = W_A + W_B
        w_oc = jnp.concatenate([w_o[o_c + h * HEAD_DIM:o_c + (h + 1) * HEAD_DIM] for h in GQA_Q_ORDER], axis=0)
        wr = jnp.zeros((d, LANES), F32)
        wr = wr.at[:, :N_GROUPS].set(router_group_w[l]).at[:, N_GROUPS:N_GROUPS + N_EXPERTS].set(router_expert_w[l])
        wrh, wrl = _split_bf16(wr)
        br = jnp.zeros((1, LANES), F32)
        br = br.at[0, :N_GROUPS].set(router_group_b[l]).at[0, N_GROUPS:N_GROUPS + N_EXPERTS].set(router_expert_b[l])
        x1, tok, route, cnt = _out_projection(
            x_lat, x_ctx, ctx_blk, oa, ob, oc, w_o[:W_A].astype(BF16), w_o[W_A:W_A + W_B].astype(BF16),
            w_oc.astype(BF16), modsel, norm_ffn[l][None], wrh, wrl, br, n_qt, n_lat_tiles)
        xs = _routed_moe(tok, route, cnt, x1, w_gate, w_up, w_down, l, modsel, final_norm[None],
                         n_lat_tiles, final=not ctx_out)
        x_lat, x_ctx, ctx_blk = xs, xs, n_lat_tiles
    return xs
```

```python
import functools
import math

import numpy as np
import jax
import jax.numpy as jnp
from jax import lax
from jax.experimental import pallas as pl
from jax.experimental.pallas import tpu as pltpu
from jax.experimental.pallas import tpu_sc as plsc

F32 = jnp.float32
BF16 = jnp.bfloat16

D_MODEL = 1024
DEPTH = 2
GRID_W = 64
CTX_LEN = 256
HEAD_DIM = 64
NA_HEADS = 6
NA_WIN_H = 8
NA_WIN_W = 16
DIFF_HEADS = 4
DIFF_QK_DIM = 32
GQA_Q_HEADS = 6
GQA_KV_HEADS = 2
N_GROUPS = 4
EXPERTS_PER_GROUP = 4
N_EXPERTS = 16
EXPERT_HIDDEN = 512
ROPE_THETA = 10000.0
EPS = 1e-6
W_A = NA_HEADS * HEAD_DIM
W_B = DIFF_HEADS * 2 * DIFF_QK_DIM
W_C = GQA_Q_HEADS * HEAD_DIM
W_KC = GQA_KV_HEADS * HEAD_DIM
IN_WIDTH = 3 * W_A + 3 * W_B + W_C + 2 * W_KC

LANES = 128
TILE = CTX_LEN
NA_QROWS = 8
NA_KROWS = 16
NA_PARTS = 2
NEG = -1e30
LOG2E = 1.4426950408889634
HI16 = -65536
VMEM_LIMIT = 56 * 1024 * 1024
FLASH_TK = 512
PAIRS_PER_STEP = 2
FLASH_UNROLL_MAX_ROWS = 1536
SUBLANES = 8
MOE_TILE = 512
MOE_GROUPS = 2
SC_ROWS = 32
SC_BUFS = 4
SC_CORES = 2
SC_SUBCORES = 16

GQA_Q_ORDER = (0, 3, 1, 4, 2, 5)


def _cparams(n_axes):
    return pltpu.CompilerParams(dimension_semantics=("arbitrary",) * n_axes,
                                vmem_limit_bytes=VMEM_LIMIT)


def _split_bf16(a):
    hi = a.astype(BF16)
    lo = (a - hi.astype(F32)).astype(BF16)
    return hi, lo


def _dot(a, b):
    return jnp.dot(a, b, preferred_element_type=F32)


def _pack_bf16_pairs(t):
    bits = lax.bitcast_convert_type(t.astype(BF16).astype(F32), jnp.int32)
    half_d = bits.shape[1] // 2
    return lax.shift_right_logical(bits[:, :half_d], 16) | (bits[:, half_d:] & HI16)


def _unpack_bf16_pairs(w):
    return jnp.concatenate([lax.bitcast_convert_type(lax.shift_left(w, 16), F32),
                            lax.bitcast_convert_type(w & HI16, F32)], axis=1)


def _dot_nt(a, b):
    return lax.dot_general(a, b, (((1,), (1,)), ((), ())), preferred_element_type=F32)


def _mod_kernel(c_ref, w_ref, b_ref, o_ref):
    c = c_ref[...]
    a = c * jax.nn.sigmoid(c)
    a_hi, a_lo = _split_bf16(a)
    w_hi, w_lo = _split_bf16(w_ref[...])
    o_ref[...] = _dot(a_hi, w_hi) + _dot(a_lo, w_hi) + _dot(a_hi, w_lo) + b_ref[...]


def _modulation(c_rows, w_mod, b_mod):
    depth, d, n = w_mod.shape
    bn = 1536
    return pl.pallas_call(
        _mod_kernel,
        out_shape=jax.ShapeDtypeStruct((depth, SUBLANES, n), F32),
        grid=(depth, n // bn),
        in_specs=[pl.BlockSpec((SUBLANES, d), lambda l, j: (0, 0)),
                  pl.BlockSpec((None, d, bn), lambda l, j: (l, 0, j)),
                  pl.BlockSpec((None, 1, bn), lambda l, j: (l, 0, j))],
        out_specs=pl.BlockSpec((None, SUBLANES, bn), lambda l, j: (l, 0, j)),
        compiler_params=_cparams(2),
        name="adaln_mod",
    )(c_rows, w_mod, b_mod.reshape(depth, 1, n))


def _head_mean_sq(t, ones):
    hi, lo = _split_bf16(t * t)
    return (_dot(hi, ones) + _dot(lo, ones)) * (1.0 / HEAD_DIM)


def _rotate_half(p, head):
    w = p.shape[1]
    half = head // 2
    lane = lax.broadcasted_iota(jnp.int32, (1, w), 1)
    first = (lane & (head - 1)) < half
    from_right = pltpu.roll(p, w - half, 1)
    from_left = pltpu.roll(p, half, 1)
    return jnp.where(first, -from_right, from_left)


def _inproj_kernel(x_ref, xc_ref, mod_ref, gain_ref, w_ref, tab_ref, gq_ref, gk_ref, ones_ref,
                   qa_ref, ka_ref, va_ref, qb_ref, kb_ref, vb_ref, qc_ref, kc_ref, vc_ref,
                   *, n_lat_tiles):
    x = jnp.where(pl.program_id(1) == n_lat_tiles, xc_ref[...], x_ref[...])
    mod = mod_ref[...]
    ms = jnp.mean(x * x, axis=-1, keepdims=True)
    h = (x * lax.rsqrt(ms + EPS)) * gain_ref[...]
    h = h * (1.0 + mod[1:2]) + mod[0:1]
    hb = h.astype(BF16)

    def proj(a, b):
        return _dot(hb, w_ref[:, a:b])

    pa = proj(0, 3 * W_A)
    qa_ref[...] = (pa[:, :W_A] * (HEAD_DIM ** -0.5 * LOG2E)).astype(BF16)
    ka_ref[...] = pa[:, W_A:2 * W_A].astype(BF16)
    va_ref[...] = pa[:, 2 * W_A:].astype(BF16)

    tab = tab_ref[...]
    cos_b = jnp.concatenate([tab[:, 0:LANES]] * 2, axis=1)
    sin_b = jnp.concatenate([tab[:, LANES:2 * LANES]] * 2, axis=1)
    cos_c1 = tab[:, 2 * LANES:3 * LANES]
    sin_c1 = tab[:, 3 * LANES:4 * LANES]
    cos_c = jnp.concatenate([cos_c1] * 3, axis=1)
    sin_c = jnp.concatenate([sin_c1] * 3, axis=1)

    o_b = 3 * W_A
    pb = proj(o_b, o_b + 3 * W_B)
    qb = pb[:, :W_B]
    kb = pb[:, W_B:2 * W_B]
    qb = qb * cos_b + _rotate_half(qb, DIFF_QK_DIM) * sin_b
    qb_ref[...] = (qb * (DIFF_QK_DIM ** -0.5 * LOG2E)).astype(BF16)
    kb_ref[...] = (kb * cos_b + _rotate_half(kb, DIFF_QK_DIM) * sin_b).astype(BF16)
    vb_ref[...] = pb[:, 2 * W_B:].astype(BF16)

    o_c = o_b + 3 * W_B
    pc = proj(o_c, IN_WIDTH)
    ones = ones_ref[...]
    qc = pc[:, :W_C]
    kc = pc[:, W_C:W_C + W_KC]
    nq = lax.rsqrt(_head_mean_sq(qc, ones) + EPS)
    nk = lax.rsqrt(_head_mean_sq(kc, ones[:W_KC, :W_KC]) + EPS)
    gq = gq_ref[...]
    gk = gk_ref[...]
    q = nq * (qc * gq[0:1] * cos_c + _rotate_half(qc, HEAD_DIM) * gq[1:2] * sin_c)
    qc_ref[...] = (q * (HEAD_DIM ** -0.5 * LOG2E)).astype(BF16)
    k = nk * (kc * gk[0:1] * cos_c1 + _rotate_half(kc, HEAD_DIM) * gk[1:2] * sin_c1)
    kc_ref[...] = k.astype(BF16)
    vc_ref[...] = pc[:, W_C + W_KC:].astype(BF16)


def _token_specs(d, n_lat_tiles, ctx_blk):
    return [pl.BlockSpec((None, TILE, d), lambda bi, ti: (bi, jnp.minimum(ti, n_lat_tiles - 1), 0)),
            pl.BlockSpec((None, TILE, d), lambda bi, ti: (bi, ctx_blk, 0))]


def _in_projection(x_lat, x_ctx, ctx_blk, modsel, gain, w_ext, tab, gq, gk, ones, n_lat_tiles):
    b, _, d = x_lat.shape
    n_tiles = n_lat_tiles + 1
    t_all = n_tiles * TILE
    widths = (W_A, W_A, W_A, W_B, W_B, W_B, W_C, W_KC, W_KC)
    tok = lambda bi, ti: (bi, ti, 0)
    const2 = lambda bi, ti: (0, 0)
    return pl.pallas_call(
        functools.partial(_inproj_kernel, n_lat_tiles=n_lat_tiles),
        out_shape=[jax.ShapeDtypeStruct((b, t_all, w), BF16) for w in widths],
        grid=(b, n_tiles),
        in_specs=_token_specs(d, n_lat_tiles, ctx_blk) + [
                  pl.BlockSpec((None, 6, d), lambda bi, ti: (2 * bi + (ti >= n_lat_tiles).astype(jnp.int32), 0, 0)),
                  pl.BlockSpec((1, d), const2),
                  pl.BlockSpec((d, IN_WIDTH), const2),
                  pl.BlockSpec((TILE, 4 * LANES), lambda bi, ti: (ti, 0)),
                  pl.BlockSpec((2, W_C), const2),
                  pl.BlockSpec((2, W_KC), const2),
                  pl.BlockSpec((W_C, W_C), const2)],
        out_specs=[pl.BlockSpec((None, TILE, w), tok) for w in widths],
        compiler_params=_cparams(2),
        name="in_projection",
    )(x_lat, x_ctx, modsel, gain, w_ext, tab, gq, gk, ones)


def _flash_kernel(q_ref, k_ref, v_ref, aux_ref, o_ref, va_ref, vb_ref, qs_ref, acc_ref, m_ref,
                  s0_ref, s1_ref, mb0_ref, mb1_ref, *,
                  n_qblk, n_sub, tk, n_lat_blocks, pairs_per_step, ctx_tile, qt_off, mode, lam_init):
    qt = pl.program_id(2) + qt_off
    sub_w = LANES // n_sub
    half = LANES // 2
    lane = lax.broadcasted_iota(jnp.int32, (1, LANES), 1)
    lower = lane < half
    n_pieces = n_qblk * n_sub
    ma = (n_pieces // 2) * TILE
    m_rows = n_pieces * TILE
    ctx_start = n_lat_blocks * tk

    @pl.when(pl.program_id(2) == 0)
    def _():
        v = v_ref[...].astype(F32)
        va_ref[...] = jnp.where(lower, v, 1.0).astype(BF16)
        vb_ref[...] = jnp.where(lower, 1.0, v).astype(BF16)

    ia, ib = 0, n_pieces // 2
    for blk in range(n_qblk):
        qf = q_ref[:, blk * LANES:(blk + 1) * LANES].astype(F32)
        for sub in range(n_sub):
            msk = (lane >= sub * sub_w) & (lane < (sub + 1) * sub_w)
            piece = jnp.where(msk, qf, 0.0).astype(BF16)
            if sub * sub_w < half:
                qs_ref[ia * TILE:(ia + 1) * TILE, :] = piece
                ia += 1
            else:
                qs_ref[ib * TILE:(ib + 1) * TILE, :] = piece
                ib += 1

    s_bufs = (s0_ref, s1_ref)
    mb_bufs = (mb0_ref, mb1_ref)

    def scores(start, size, slot):
        s = _dot_nt(qs_ref[...], k_ref[pl.ds(start, size), :])
        s_bufs[slot][:, :size] = s
        mb = jnp.max(s, axis=-1, keepdims=True)
        mb_bufs[slot][...] = jnp.broadcast_to(mb, (m_rows, LANES))

    def accumulate(start, size, slot, first):
        mb = mb_bufs[slot][...]
        if first:
            m_new = mb
        else:
            m_old = m_ref[...]
            m_new = jnp.maximum(m_old, mb)
        s_ref = s_bufs[slot]
        cols = [s_ref[:, c * LANES:(c + 1) * LANES] - m_new for c in range(size // LANES)]
        p = jnp.concatenate([jnp.exp2(d.astype(BF16)) for d in cols], axis=1)
        pva = _dot(p[:ma], va_ref[pl.ds(start, size), :])
        pvb = _dot(p[ma:], vb_ref[pl.ds(start, size), :])
        if first:
            acc_ref[:ma, :] = pva
            acc_ref[ma:, :] = pvb
        else:
            alpha = jnp.exp2(m_old - m_new)
            acc_ref[:ma, :] = alpha[:ma] * acc_ref[:ma, :] + pva
            acc_ref[ma:, :] = alpha[ma:] * acc_ref[ma:, :] + pvb
        m_ref[...] = m_new

    def lat(j):
        return pl.multiple_of(j * tk, tk)

    def latent_queries():
        scores(ctx_start, CTX_LEN, 0)
        scores(lat(0), tk, 1)
        accumulate(ctx_start, CTX_LEN, 0, True)

        def pair(i):
            scores(lat(2 * i + 1), tk, 0)
            accumulate(lat(2 * i), tk, 1, False)
            scores(lat(2 * i + 2), tk, 1)
            accumulate(lat(2 * i + 1), tk, 0, False)

        def body(i, carry):
            for u in range(pairs_per_step):
                pair(i * pairs_per_step + u)
            return carry

        n_pairs = (n_lat_blocks - 2) // 2
        n_steps = n_pairs // pairs_per_step
        lax.fori_loop(0, n_steps, body, 0)
        for i in range(n_steps * pairs_per_step, n_pairs):
            pair(i)
        scores(lat(n_lat_blocks - 1), tk, 0)
        accumulate(lat(n_lat_blocks - 2), tk, 1, False)
        accumulate(lat(n_lat_blocks - 1), tk, 0, False)

    def context_queries():
        scores(ctx_start, CTX_LEN, 0)
        accumulate(ctx_start, CTX_LEN, 0, True)

    if ctx_tile is None:
        latent_queries()
    else:
        pl.when(qt != ctx_tile)(latent_queries)
        pl.when(qt == ctx_tile)(context_queries)

    acc = acc_ref[...]
    r = acc / pltpu.roll(acc, half, 1)
    ra, rb = r[:ma], r[ma:]
    if mode == "plain":
        for i in range(n_pieces // 2):
            o = jnp.where(lower, ra[i * TILE:(i + 1) * TILE], rb[i * TILE:(i + 1) * TILE])
            o_ref[:, i * LANES:(i + 1) * LANES] = o.astype(BF16)
    else:
        aux = aux_ref[...]
        l1 = jnp.sum(aux[0:1] * aux[1:2], axis=-1, keepdims=True)
        l2 = jnp.sum(aux[2:3] * aux[3:4], axis=-1, keepdims=True)
        lam = jnp.exp(l1) - jnp.exp(l2) + lam_init
        oa = ra[:TILE] - lam * ra[TILE:]
        ob = rb[:TILE] - lam * rb[TILE:]
        o = jnp.where(lower, oa, ob)
        sq = o * o
        ss_a = jnp.sum(jnp.where(lower, sq, 0.0), axis=-1, keepdims=True)
        ss_b = jnp.sum(jnp.where(lower, 0.0, sq), axis=-1, keepdims=True)
        ms = jnp.where(lower, ss_a, ss_b) * (1.0 / HEAD_DIM)
        o = (o * lax.rsqrt(ms + EPS)) * aux[4:5]
        o_ref[...] = (o * (1.0 - lam_init)).astype(BF16)


def _flash(q, k, v, aux, *, n_qblk, n_sub, n_hp, qt_off, n_qt, n_lat, ctx_tile, mode, lam_init=0.0):
    b, t_all, _ = q.shape
    qw = n_qblk * LANES
    tk = FLASH_TK
    assert n_lat % (2 * tk) == 0 and tk >= CTX_LEN
    m_rows = n_qblk * n_sub * TILE
    n_pairs = max((n_lat // tk - 2) // 2, 1)
    pairs = n_pairs if m_rows <= FLASH_UNROLL_MAX_ROWS else PAIRS_PER_STEP
    kern = functools.partial(_flash_kernel, n_qblk=n_qblk, n_sub=n_sub, tk=tk,
                             n_lat_blocks=n_lat // tk, ctx_tile=ctx_tile, qt_off=qt_off,
                             pairs_per_step=pairs, mode=mode, lam_init=lam_init)
    return pl.pallas_call(
        kern,
        out_shape=jax.ShapeDtypeStruct((b, n_qt * TILE, n_hp * qw), BF16),
        grid=(b, n_hp, n_qt),
        in_specs=[pl.BlockSpec((None, TILE, qw), lambda bi, hp, qt: (bi, qt + qt_off, hp)),
                  pl.BlockSpec((None, t_all, LANES), lambda bi, hp, qt: (bi, 0, hp)),
                  pl.BlockSpec((None, t_all, LANES), lambda bi, hp, qt: (bi, 0, hp)),
                  pl.BlockSpec((SUBLANES, LANES), lambda bi, hp, qt: (0, 0))],
        out_specs=pl.BlockSpec((None, TILE, qw), lambda bi, hp, qt: (bi, qt, hp)),
        scratch_shapes=[pltpu.VMEM((t_all, LANES), BF16),
                        pltpu.VMEM((t_all, LANES), BF16),
                        pltpu.VMEM((m_rows, LANES), BF16),
                        pltpu.VMEM((m_rows, LANES), F32),
                        pltpu.VMEM((m_rows, LANES), F32),
                        pltpu.VMEM((m_rows, tk), F32),
                        pltpu.VMEM((m_rows, tk), F32),
                        pltpu.VMEM((m_rows, LANES), F32),
                        pltpu.VMEM((m_rows, LANES), F32)],
        compiler_params=_cparams(3),
        name="flash_" + mode,
    )(q, k, v, aux)


def _na_kernel(q_ref, k0, k1, k2, k3, v0, v1, v2, v3, kc_ref, vc_ref, bias_ref, o_ref, s_ref, m_ref):
    lane = lax.broadcasted_iota(jnp.int32, (1, LANES), 1)
    lower = lane < LANES // 2
    qf = q_ref[...].astype(F32)
    k_all = jnp.concatenate([k0[...], k1[...], k2[...], k3[...], kc_ref[...]], axis=0)
    v_all = jnp.concatenate([v0[...], v1[...], v2[...], v3[...], vc_ref[...]], axis=0).astype(F32)
    v_h = [jnp.where(lower, v_all, 1.0).astype(BF16), jnp.where(lower, 1.0, v_all).astype(BF16)]
    q_h = [jnp.where(lower, qf, 0.0).astype(BF16), jnp.where(lower, 0.0, qf).astype(BF16)]
    n_pair = NA_KROWS // 2
    rows_per_part = NA_QROWS // NA_PARTS
    half_q = rows_per_part * GRID_W
    no_bias = jnp.zeros((GRID_W, CTX_LEN), F32)

    def scores(part):
        rows = slice(part * half_q, (part + 1) * half_q)
        qs = jnp.concatenate([q_h[0][rows], q_h[1][rows]], axis=0)
        bias = jnp.concatenate(
            [jnp.concatenate([bias_ref[hh, a * n_pair + j] for j in range(n_pair)] + [no_bias], axis=1)
             for hh in range(2) for a in range(part * rows_per_part, (part + 1) * rows_per_part)],
            axis=0)
        s = _dot_nt(qs, k_all) + bias
        s_ref[part] = s
        m_ref[part] = jnp.broadcast_to(jnp.max(s, axis=-1, keepdims=True), (2 * half_q, LANES))

    def finish(part):
        s = s_ref[part]
        m = m_ref[part]
        p = jnp.concatenate([jnp.exp2((s[:, c * LANES:(c + 1) * LANES] - m).astype(BF16))
                             for c in range(s.shape[1] // LANES)], axis=1)
        o0 = _dot(p[:half_q], v_h[0])
        o1 = _dot(p[half_q:], v_h[1])
        o0 = o0 / pltpu.roll(o0, LANES // 2, 1)
        o1 = o1 / pltpu.roll(o1, LANES // 2, 1)
        o_ref[part * half_q:(part + 1) * half_q, :] = jnp.where(lower, o0, o1).astype(BF16)

    scores(0)
    for part in range(1, NA_PARTS):
        scores(part)
        finish(part - 1)
    finish(NA_PARTS - 1)


def _neighbourhood_attention(qa, ka, va, bias, n_lat):
    b = qa.shape[0]
    q_tok = NA_QROWS * GRID_W
    v_tok = q_tok // 2
    n_rb = n_lat // q_tok
    n_view = n_lat // v_tok
    ctx_blk = n_lat // v_tok

    def view(j):
        return lambda rb, hp, bi: (bi, jnp.clip(2 * rb - 1 + j, 0, n_view - 1), hp)

    kv_specs = [pl.BlockSpec((None, v_tok, LANES), view(j)) for j in range(4)]
    ctx_spec = pl.BlockSpec((None, CTX_LEN, LANES), lambda rb, hp, bi: (bi, ctx_blk, hp))

    def bias_map(rb, hp, bi):
        pat = jnp.where(rb == 0, 0, jnp.where(rb == n_rb - 1, 2, 1))
        return (hp, pat, 0, 0, 0)

    return pl.pallas_call(
        _na_kernel,
        out_shape=jax.ShapeDtypeStruct((b, n_lat, W_A), BF16),
        grid=(n_rb, NA_HEADS // 2, b),
        in_specs=[pl.BlockSpec((None, q_tok, LANES), lambda rb, hp, bi: (bi, rb, hp))]
                 + kv_specs + kv_specs + [ctx_spec, ctx_spec,
                 pl.BlockSpec((2, None, NA_QROWS * NA_KROWS // 2, GRID_W, 2 * GRID_W), bias_map)],
        out_specs=pl.BlockSpec((None, q_tok, LANES), lambda rb, hp, bi: (bi, rb, hp)),
        scratch_shapes=[pltpu.VMEM((NA_PARTS, 2 * q_tok // NA_PARTS, NA_KROWS * GRID_W + CTX_LEN), F32),
                        pltpu.VMEM((NA_PARTS, 2 * q_tok // NA_PARTS, LANES), F32)],
        compiler_params=_cparams(3),
        name="neighbourhood_attention",
    )(qa, ka, ka, ka, ka, va, va, va, va, ka, va, bias)


def _na_bias_table(rpb, rows):
    cols = np.arange(GRID_W)
    c0 = np.clip(cols - NA_WIN_W // 2, 0, GRID_W - NA_WIN_W)
    cc = cols[None, :]
    col_ok = (cc >= c0[:, None]) & (cc < c0[:, None] + NA_WIN_W)
    dc = np.clip(cc - cols[:, None] + (NA_WIN_W - 1), 0, 2 * NA_WIN_W - 2)
    e = jnp.where(col_ok[None, None], (rpb.astype(F32) * LOG2E)[:, :, dc], NEG)
    e = jnp.concatenate([e, jnp.full_like(e[:, :1], NEG)], axis=1)
    a = np.arange(NA_QROWS)[:, None]
    i = np.arange(NA_KROWS)[None, :]
    pats = []
    for r_base in (0, NA_QROWS, rows - NA_QROWS):
        r = r_base + a
        key_row = r_base - NA_WIN_H // 2 + i
        r0 = np.clip(r - NA_WIN_H // 2, 0, rows - NA_WIN_H)
        ok = (key_row >= r0) & (key_row < r0 + NA_WIN_H) & (key_row >= 0) & (key_row < rows)
        dr = np.where(ok, key_row - r + (NA_WIN_H - 1), 2 * NA_WIN_H - 1)
        pats.append(dr)
    dr_all = np.stack(pats)
    pairs = dr_all.reshape(-1, 2)
    uniq, inv = np.unique(pairs, axis=0, return_inverse=True)
    pair_blocks = jnp.concatenate([e[:, uniq[:, 0]], e[:, uniq[:, 1]]], axis=-1)
    t = pair_blocks[:, inv.reshape(-1)]
    return t.reshape(NA_HEADS, 3, NA_QROWS * NA_KROWS // 2, GRID_W, 2 * GRID_W)


def _outproj_kernel(x_ref, xc_ref, oa_ref, ob_ref, oc_ref, wa_ref, wb_ref, wc_ref, mod_ref, gain_ref,
                    wrh_ref, wrl_ref, br_ref, tri_ref, x1_ref, tok_ref, route_ref, cnt_ref, run_ref,
                    *, region, group_batches, n_lat_tiles):
    mod = mod_ref[...]
    y = _dot(oa_ref[...], wa_ref[...]) + _dot(ob_ref[...], wb_ref[...]) + _dot(oc_ref[...], wc_ref[...])
    x1 = jnp.where(pl.program_id(1) == n_lat_tiles, xc_ref[...], x_ref[...]) + mod[2:3] * y
    x1_ref[...] = x1
    ms = jnp.mean(x1 * x1, axis=-1, keepdims=True)
    t = (x1 * lax.rsqrt(ms + EPS)) * gain_ref[...]
    t = t * (1.0 + mod[4:5]) + mod[3:4]
    tok_ref[...] = _pack_bf16_pairs(t)

    t_hi, t_lo = _split_bf16(t)
    wrh = wrh_ref[...]
    logits = _dot(t_hi, wrh) + _dot(t_lo, wrh) + _dot(t_hi, wrl_ref[...]) + br_ref[...]

    lane = lax.broadcasted_iota(jnp.int32, logits.shape, 1)
    lane_f = lane.astype(F32)
    is_g = lane < N_GROUPS
    gl = jnp.where(is_g, logits, NEG)
    gmax = jnp.max(gl, axis=-1, keepdims=True)
    g_sel = jnp.min(jnp.where(gl == gmax, lane_f, 1e9), axis=-1, keepdims=True)
    p_grp = 1.0 / jnp.sum(jnp.where(is_g, jnp.exp(gl - gmax), 0.0), axis=-1, keepdims=True)
    grp_of_lane = lax.shift_right_arithmetic(lane - N_GROUPS, 2).astype(F32)
    in_grp = (lane >= N_GROUPS) & (lane < N_GROUPS + N_EXPERTS) & (grp_of_lane == g_sel)
    el = jnp.where(in_grp, logits, NEG)
    v1 = jnp.max(el, axis=-1, keepdims=True)
    i1 = jnp.min(jnp.where(el == v1, lane_f, 1e9), axis=-1, keepdims=True)
    el2 = jnp.where(lane_f == i1, NEG, el)
    v2 = jnp.max(el2, axis=-1, keepdims=True)
    i2 = jnp.min(jnp.where(el2 == v2, lane_f, 1e9), axis=-1, keepdims=True)
    e2 = jnp.exp(v2 - v1)
    den = 1.0 + e2
    w1 = p_grp / den
    w2 = p_grp * e2 / den

    @pl.when((lax.rem(pl.program_id(0), group_batches) == 0) & (pl.program_id(1) == 0))
    def _():
        run_ref[...] = jnp.zeros(run_ref.shape, F32)

    ind = jnp.where(lane_f == i1, 1.0, 0.0) + jnp.where(lane_f == i2, 1.0, 0.0)
    rank = _dot(tri_ref[...], ind.astype(BF16)) + run_ref[0:1, :]

    def pick(m, l):
        return jnp.sum(jnp.where(lane_f == l, m, 0.0), axis=-1, keepdims=True)

    pos1 = (i1 - N_GROUPS) * region + pick(rank, i1)
    pos2 = (i2 - N_GROUPS) * region + pick(rank, i2)
    route_ref[...] = jnp.where(lane == 0, pos1, jnp.where(lane == 1, pos2,
                               jnp.where(lane == 2, w1, jnp.where(lane == 3, w2, 0.0))))
    run = run_ref[...] + jnp.sum(ind, axis=0, keepdims=True)
    run_ref[...] = run
    cnt_ref[...] = run


def _out_projection(x_lat, x_ctx, ctx_blk, oa, ob, oc, wa, wb, wc, modsel, gain, wrh, wrl, br, n_tiles,
                    n_lat_tiles):
    b, _, d = x_lat.shape
    tok = lambda bi, ti: (bi, ti, 0)
    const2 = lambda bi, ti: (0, 0)
    rows = n_tiles * TILE
    nb = b // MOE_GROUPS if b % MOE_GROUPS == 0 else b
    tri = jnp.asarray(np.tril(np.ones((TILE, TILE), np.float32), -1), BF16)
    return pl.pallas_call(
        functools.partial(_outproj_kernel, region=nb * rows, group_batches=nb, n_lat_tiles=n_lat_tiles),
        out_shape=[jax.ShapeDtypeStruct((b, rows, d), F32),
                   jax.ShapeDtypeStruct((b, rows, d // 2), jnp.int32),
                   jax.ShapeDtypeStruct((b, rows, LANES), F32),
                   jax.ShapeDtypeStruct((SUBLANES * (b // nb), LANES), F32)],
        grid=(b, n_tiles),
        in_specs=_token_specs(d, n_lat_tiles, ctx_blk) + [
                  pl.BlockSpec((None, TILE, W_A), tok),
                  pl.BlockSpec((None, TILE, W_B), tok),
                  pl.BlockSpec((None, TILE, W_C), tok),
                  pl.BlockSpec((W_A, d), const2),
                  pl.BlockSpec((W_B, d), const2),
                  pl.BlockSpec((W_C, d), const2),
                  pl.BlockSpec((None, 6, d), lambda bi, ti: (2 * bi + (ti >= n_lat_tiles).astype(jnp.int32), 0, 0)),
                  pl.BlockSpec((1, d), const2),
                  pl.BlockSpec((d, LANES), const2),
                  pl.BlockSpec((d, LANES), const2),
                  pl.BlockSpec((1, LANES), const2),
                  pl.BlockSpec((TILE, TILE), const2)],
        out_specs=[pl.BlockSpec((None, TILE, d), tok),
                   pl.BlockSpec((None, TILE, d // 2), tok),
                   pl.BlockSpec((None, TILE, LANES), tok),
                   pl.BlockSpec((SUBLANES, LANES), lambda bi, ti: (bi // nb, 0))],
        scratch_shapes=[pltpu.VMEM((SUBLANES, LANES), F32)],
        compiler_params=_cparams(2),
        name="out_projection",
    )(x_lat, x_ctx, oa, ob, oc, wa, wb, wc, modsel, gain, wrh, wrl, br, tri)


def _sc_mesh():
    return plsc.VectorSubcoreMesh(core_axis_name="core", subcore_axis_name="subcore")


def _sc_worker_base(per_worker):
    wid = lax.axis_index("subcore") * SC_CORES + lax.axis_index("core")
    return wid * per_worker


def _sc_scratch(d, dtype):
    return ([pltpu.VMEM((SC_ROWS,), jnp.int32)] * SC_BUFS + [pltpu.VMEM((SC_ROWS, d), dtype)] * SC_BUFS
            + [pltpu.SemaphoreType.DMA] * (2 * SC_BUFS))


def _sc_split(scratch):
    return (scratch[:SC_BUFS], scratch[SC_BUFS:2 * SC_BUFS], scratch[2 * SC_BUFS:3 * SC_BUFS],
            scratch[3 * SC_BUFS:])


def _sc_chunk_loop(per_worker, group):
    chunks = per_worker // SC_ROWS
    full = chunks // SC_BUFS * SC_BUFS

    @pl.loop(0, full, step=SC_BUFS)
    def _(c):
        group(c, SC_BUFS)

    if chunks > full:
        group(full, chunks - full)


def _sc_scatter_rows(x, row_off, n, idx, n_out):
    d = x.shape[1]
    per_worker = 2 * n // (SC_CORES * SC_SUBCORES)
    assert per_worker % SC_ROWS == 0 and n % SC_ROWS == 0

    @functools.partial(pl.kernel, out_type=jax.ShapeDtypeStruct((n_out, d), x.dtype),
                       mesh=_sc_mesh(), scratch_types=_sc_scratch(d, x.dtype))
    def scatter(x_hbm, i_hbm, o_hbm, *scratch):
        idx_v, rows_v, sem_in, sem_out = _sc_split(scratch)
        base = _sc_worker_base(per_worker)

        def group(c, n_bufs):
            reads = []
            for u in range(n_bufs):
                a = pl.multiple_of(base + (c + u) * SC_ROWS, SC_ROWS)
                t = pl.multiple_of(row_off + lax.rem(a, n), SC_ROWS)
                pltpu.sync_copy(i_hbm.at[pl.ds(a, SC_ROWS)], idx_v[u])
                reads.append(pltpu.async_copy(x_hbm.at[pl.ds(t, SC_ROWS)], rows_v[u], sem_in[u]))
            writes = []
            for u in range(n_bufs):
                reads[u].wait()
                writes.append(pltpu.async_copy(rows_v[u], o_hbm.at[idx_v[u]], sem_out[u]))
            for w in writes:
                w.wait()

        _sc_chunk_loop(per_worker, group)

    return scatter(x, idx)


def _sc_gather_rows(src, idx):
    m = idx.shape[0]
    d = src.shape[1]
    per_worker = m // (SC_CORES * SC_SUBCORES)
    assert per_worker % SC_ROWS == 0

    @functools.partial(pl.kernel, out_type=jax.ShapeDtypeStruct((m, d), src.dtype),
                       mesh=_sc_mesh(), scratch_types=_sc_scratch(d, src.dtype))
    def gather(s_hbm, i_hbm, o_hbm, *scratch):
        idx_v, rows_v, sem_in, sem_out = _sc_split(scratch)
        base = _sc_worker_base(per_worker)

        def group(c, n_bufs):
            offs, reads = [], []
            for u in range(n_bufs):
                a = pl.multiple_of(base + (c + u) * SC_ROWS, SC_ROWS)
                offs.append(a)
                pltpu.sync_copy(i_hbm.at[pl.ds(a, SC_ROWS)], idx_v[u])
                reads.append(pltpu.async_copy(s_hbm.at[idx_v[u]], rows_v[u], sem_in[u]))
            writes = []
            for u in range(n_bufs):
                reads[u].wait()
                writes.append(pltpu.async_copy(rows_v[u], o_hbm.at[pl.ds(offs[u], SC_ROWS)], sem_out[u]))
            for w in writes:
                w.wait()

        _sc_chunk_loop(per_worker, group)

    return gather(src, idx)


def _expert_ffn_kernel(blk_ref, exp_ref, x_ref, wg_ref, wu_ref, wd_ref, y_ref, wgb_ref, wub_ref, wdb_ref):
    j = pl.program_id(0)

    @pl.when((j == 0) | (exp_ref[j] != exp_ref[jnp.maximum(j - 1, 0)]))
    def _():
        wgb_ref[...] = wg_ref[...].astype(BF16)
        wub_ref[...] = wu_ref[...].astype(BF16)
        wdb_ref[...] = wd_ref[...].astype(BF16)

    x = _unpack_bf16_pairs(x_ref[...]).astype(BF16)
    hid = jax.nn.silu(_dot(x, wgb_ref[...])) * _dot(x, wub_ref[...])
    y_ref[...] = _pack_bf16_pairs(_dot(hid.astype(BF16), wdb_ref[...]))


def _expert_ffn(xs, blk, exp, wg, wu, wd, layer):
    rows, d_packed = xs.shape
    d = 2 * d_packed
    w_map = lambda j, blk, exp: (layer, exp[j], 0, 0)
    return pl.pallas_call(
        _expert_ffn_kernel,
        out_shape=jax.ShapeDtypeStruct((rows, d_packed), jnp.int32),
        grid_spec=pltpu.PrefetchScalarGridSpec(
            num_scalar_prefetch=2,
            grid=(blk.shape[0],),
            in_specs=[pl.BlockSpec((MOE_TILE, d_packed), lambda j, blk, exp: (blk[j], 0)),
                      pl.BlockSpec((None, None, d, EXPERT_HIDDEN), w_map),
                      pl.BlockSpec((None, None, d, EXPERT_HIDDEN), w_map),
                      pl.BlockSpec((None, None, EXPERT_HIDDEN, d), w_map)],
            out_specs=pl.BlockSpec((MOE_TILE, d_packed), lambda j, blk, exp: (blk[j], 0)),
            scratch_shapes=[pltpu.VMEM((d, EXPERT_HIDDEN), BF16),
                            pltpu.VMEM((d, EXPERT_HIDDEN), BF16),
                            pltpu.VMEM((EXPERT_HIDDEN, d), BF16)]),
        compiler_params=_cparams(1),
        name="expert_ffn",
    )(blk, exp, xs, wg, wu, wd)


def _combine_kernel(x1_ref, y1_ref, y2_ref, route_ref, mod_ref, fgain_ref, *rest, final):
    o_ref = rest[-1]
    route = route_ref[...]
    y = route[:, 2:3] * _unpack_bf16_pairs(y1_ref[...]) + route[:, 3:4] * _unpack_bf16_pairs(y2_ref[...])
    x2 = x1_ref[...] + mod_ref[5:6, :] * y
    if final:
        ms = jnp.mean(x2 * x2, axis=-1, keepdims=True)
        x2 = (x2 * lax.rsqrt(ms + EPS)) * fgain_ref[...]
    o_ref[...] = x2


def _combine(x1, ys, route, modsel, fgain, prev, b0, nb, n_lat_tiles, final):
    b, rows, d = x1.shape
    n_t = rows // TILE
    tok = lambda bi, ti: (b0 + bi, ti, 0)
    in_specs = [pl.BlockSpec((None, TILE, d), tok),
                pl.BlockSpec((TILE, d // 2), lambda bi, ti: (bi * n_t + ti, 0)),
                pl.BlockSpec((TILE, d // 2), lambda bi, ti: ((nb + bi) * n_t + ti, 0)),
                pl.BlockSpec((None, TILE, LANES), tok),
                pl.BlockSpec((None, 6, d),
                             lambda bi, ti: (2 * (b0 + bi) + (ti >= n_lat_tiles).astype(jnp.int32), 0, 0)),
                pl.BlockSpec((1, d), lambda bi, ti: (0, 0))]
    args = [x1, ys, ys, route, modsel, fgain]
    aliases = {}
    if prev is not None:
        in_specs.append(pl.BlockSpec(memory_space=pl.ANY))
        args.append(prev)
        aliases = {len(args) - 1: 0}
    return pl.pallas_call(
        functools.partial(_combine_kernel, final=final),
        out_shape=jax.ShapeDtypeStruct((b, rows, d), F32),
        grid=(nb, n_t),
        in_specs=in_specs,
        out_specs=pl.BlockSpec((None, TILE, d), tok),
        input_output_aliases=aliases,
        compiler_params=_cparams(2),
        name="moe_combine",
    )(*args)


def _routed_moe(tok, route, cnt, x1, wg, wu, wd, layer, modsel, fgain, n_lat_tiles, final):
    b, rows, d = x1.shape
    n_groups = cnt.shape[0] // SUBLANES
    nb = b // n_groups
    n = nb * rows
    flat = route.reshape(b * rows, LANES)
    tok_flat = tok.reshape(b * rows, tok.shape[2])
    out = None
    for g in range(n_groups):
        part = flat[g * n:(g + 1) * n]
        idx = jnp.concatenate([part[:, 0], part[:, 1]]).astype(jnp.int32)
        xs = _sc_scatter_rows(tok_flat, g * n, n, idx, N_EXPERTS * n)

        counts = cnt[SUBLANES * g, N_GROUPS:N_GROUPS + N_EXPERTS].astype(jnp.int32)
        tiles = (counts + MOE_TILE - 1) // MOE_TILE
        ends = jnp.cumsum(tiles)
        n_sched = 2 * n // MOE_TILE + N_EXPERTS
        j = jnp.minimum(jnp.arange(n_sched, dtype=jnp.int32), ends[-1] - 1)
        exp = jnp.sum((j[:, None] >= ends[None, :]).astype(jnp.int32), axis=1)
        blk = exp * (n // MOE_TILE) + j - (ends - tiles)[exp]

        ys = _expert_ffn(xs, blk, exp, wg, wu, wd, layer)
        yg = _sc_gather_rows(ys, idx)
        out = _combine(x1, yg, route, modsel, fgain, out, g * nb, nb, n_lat_tiles, final)
    return out


def _rope_tables(n_lat):
    t = jnp.arange(n_lat)
    row = (t // GRID_W).astype(F32)
    col = (t % GRID_W).astype(F32)

    def cs(dim):
        quarter = dim // 4
        freqs = ROPE_THETA ** (-jnp.arange(quarter, dtype=F32) / quarter)
        ang = jnp.concatenate([row[:, None] * freqs, col[:, None] * freqs], axis=-1)
        cos = jnp.tile(jnp.cos(ang), (1, 2 * LANES // dim))
        sin = jnp.tile(jnp.sin(ang), (1, 2 * LANES // dim))
        cos = jnp.concatenate([cos, jnp.ones((CTX_LEN, LANES), F32)], axis=0)
        sin = jnp.concatenate([sin, jnp.zeros((CTX_LEN, LANES), F32)], axis=0)
        return cos, sin

    cos_b, sin_b = cs(DIFF_QK_DIM)
    cos_c, sin_c = cs(HEAD_DIM)
    return jnp.concatenate([cos_b, sin_b, cos_c, sin_c], axis=1)


def _reordered_w_in(w_in):
    o_c = 3 * W_A + 3 * W_B
    heads = [w_in[:, o_c + h * HEAD_DIM:o_c + (h + 1) * HEAD_DIM] for h in GQA_Q_ORDER]
    return jnp.concatenate([w_in[:, :o_c]] + heads + [w_in[:, o_c + W_C:]], axis=1).astype(BF16)


def kernel(x, c, ctx, c_ctx, w_mod, b_mod, norm_attn, norm_ffn, w_in, w_out, na_rpb, diff_lambda_q1, diff_lambda_k1, diff_lambda_q2, diff_lambda_k2, diff_subln, gqa_q_norm, gqa_k_norm, router_group_w, router_group_b, router_expert_w, router_expert_b, w_gate, w_up, w_down, final_norm):
    b, s, d = x.shape
    assert d == D_MODEL and ctx.shape[1] == CTX_LEN and s % (NA_QROWS * GRID_W) == 0
    rows = s // GRID_W
    assert rows >= 2 * NA_QROWS
    t_all = s + CTX_LEN
    n_lat_tiles = s // TILE

    assert b + 1 <= SUBLANES
    c_rows = jnp.zeros((SUBLANES, d), F32).at[:b].set(c).at[b].set(c_ctx)
    mod = _modulation(c_rows, w_mod, b_mod)

    tab = _rope_tables(s)
    hidx = np.arange(HEAD_DIM)
    partner = np.where(hidx < HEAD_DIM // 2, hidx + HEAD_DIM // 2, hidx - HEAD_DIM // 2)
    blk = np.arange(W_C) // HEAD_DIM
    ones = jnp.asarray((blk[:, None] == blk[None, :]).astype(np.float32), BF16)
    dummy_aux = jnp.zeros((SUBLANES, LANES), F32)

    x_lat, x_ctx, ctx_blk = x, ctx, 0
    for l in range(DEPTH):
        ctx_out = l < DEPTH - 1
        lam_init = 0.8 - 0.6 * math.exp(-0.3 * l)
        m_lat = mod[l, :b].reshape(b, 1, 6, d)
        m_ctx = jnp.broadcast_to(mod[l, b].reshape(1, 1, 6, d), (b, 1, 6, d))
        modsel = jnp.concatenate([m_lat, m_ctx], axis=1).reshape(2 * b, 6, d)

        gq = jnp.stack([jnp.tile(gqa_q_norm[l], GQA_Q_HEADS), jnp.tile(gqa_q_norm[l][partner], GQA_Q_HEADS)])
        gk = jnp.stack([jnp.tile(gqa_k_norm[l], GQA_KV_HEADS), jnp.tile(gqa_k_norm[l][partner], GQA_KV_HEADS)])
        qa, ka, va, qb, kb, vb, qc, kc, vc = _in_projection(
            x_lat, x_ctx, ctx_blk, modsel, norm_attn[l][None], _reordered_w_in(w_in[l]), tab, gq, gk, ones,
            n_lat_tiles)

        n_qt = n_lat_tiles + 1 if ctx_out else n_lat_tiles
        ctx_tile = n_lat_tiles if ctx_out else None
        oa = _neighbourhood_attention(qa, ka, va, _na_bias_table(na_rpb[l], rows), s)
        if ctx_out:
            oa_ctx = _flash(qa, ka, va, dummy_aux, n_qblk=1, n_sub=2, n_hp=NA_HEADS // 2,
                            qt_off=n_lat_tiles, n_qt=1, n_lat=s, ctx_tile=n_lat_tiles,
                            mode="plain")
            oa = jnp.concatenate([oa, oa_ctx], axis=1)
        pad = lambda v: jnp.pad(v, (0, LANES - v.shape[0]))
        aux = jnp.stack([pad(diff_lambda_q1[l]), pad(diff_lambda_k1[l]), pad(diff_lambda_q2[l]),
                         pad(diff_lambda_k2[l]), jnp.tile(diff_subln[l], 2),
                         jnp.zeros((LANES,), F32), jnp.zeros((LANES,), F32), jnp.zeros((LANES,), F32)])
        ob = _flash(qb, kb, vb, aux, n_qblk=1, n_sub=4, n_hp=DIFF_HEADS // 2, qt_off=0, n_qt=n_qt,
                    n_lat=s, ctx_tile=ctx_tile, mode="diff", lam_init=lam_init)
        oc = _flash(qc, kc, vc, dummy_aux, n_qblk=3, n_sub=2, n_hp=1, qt_off=0, n_qt=n_qt,
                    n_lat=s, ctx_tile=ctx_tile, mode="plain")

        w_o = w_out[l]
        o_c = W_A + W_B
        w_oc = jnp.concatenate([w_o[o_c + h * HEAD_DIM:o_c + (h + 1) * HEAD_DIM] for h in GQA_Q_ORDER], axis=0)
        wr = jnp.zeros((d, LANES), F32)
        wr = wr.at[:, :N_GROUPS].set(router_group_w[l]).at[:, N_GROUPS:N_GROUPS + N_EXPERTS].set(router_expert_w[l])
        wrh, wrl = _split_bf16(wr)
        br = jnp.zeros((1, LANES), F32)
        br = br.at[0, :N_GROUPS].set(router_group_b[l]).at[0, N_GROUPS:N_GROUPS + N_EXPERTS].set(router_expert_b[l])
        x1, tok, route, cnt = _out_projection(
            x_lat, x_ctx, ctx_blk, oa, ob, oc, w_o[:W_A].astype(BF16), w_o[W_A:W_A + W_B].astype(BF16),
            w_oc.astype(BF16), modsel, norm_ffn[l][None], wrh, wrl, br, n_qt, n_lat_tiles)
        xs = _routed_moe(tok, route, cnt, x1, w_gate, w_up, w_down, l, modsel, final_norm[None],
                         n_lat_tiles, final=not ctx_out)
        x_lat, x_ctx, ctx_blk = xs, xs, n_lat_tiles
    return xs
```

```python
import functools
import math

import numpy as np
import jax
import jax.numpy as jnp
from jax import lax
from jax.experimental import pallas as pl
from jax.experimental.pallas import tpu as pltpu
from jax.experimental.pallas import tpu_sc as plsc

F32 = jnp.float32
BF16 = jnp.bfloat16

D_MODEL = 1024
DEPTH = 2
GRID_W = 64
CTX_LEN = 256
HEAD_DIM = 64
NA_HEADS = 6
NA_WIN_H = 8
NA_WIN_W = 16
DIFF_HEADS = 4
DIFF_QK_DIM = 32
GQA_Q_HEADS = 6
GQA_KV_HEADS = 2
N_GROUPS = 4
EXPERTS_PER_GROUP = 4
N_EXPERTS = 16
EXPERT_HIDDEN = 512
ROPE_THETA = 10000.0
EPS = 1e-6
W_A = NA_HEADS * HEAD_DIM
W_B = DIFF_HEADS * 2 * DIFF_QK_DIM
W_C = GQA_Q_HEADS * HEAD_DIM
W_KC = GQA_KV_HEADS * HEAD_DIM
IN_WIDTH = 3 * W_A + 3 * W_B + W_C + 2 * W_KC

LANES = 128
TILE = CTX_LEN
NA_QROWS = 8
NA_KROWS = 16
NA_PARTS = 2
NEG = -1e30
LOG2E = 1.4426950408889634
HI16 = -65536
VMEM_LIMIT = 56 * 1024 * 1024
FLASH_TK = 512
PAIRS_PER_STEP = 2
FLASH_UNROLL_MAX_ROWS = 1536
SUBLANES = 8
MOE_TILE = 512
MOE_GROUPS = 2
SC_ROWS = 32
SC_BUFS = 4
SC_CORES = 2
SC_SUBCORES = 16

GQA_Q_ORDER = (0, 3, 1, 4, 2, 5)


def _cparams(n_axes):
    return pltpu.CompilerParams(dimension_semantics=("arbitrary",) * n_axes,
                                vmem_limit_bytes=VMEM_LIMIT)


def _split_bf16(a):
    hi = a.astype(BF16)
    lo = (a - hi.astype(F32)).astype(BF16)
    return hi, lo


def _dot(a, b):
    return jnp.dot(a, b, preferred_element_type=F32)


def _pack_bf16_pairs(t):
    bits = lax.bitcast_convert_type(t.astype(BF16).astype(F32), jnp.int32)
    half_d = bits.shape[1] // 2
    return lax.shift_right_logical(bits[:, :half_d], 16) | (bits[:, half_d:] & HI16)


def _unpack_bf16_pairs(w):
    return jnp.concatenate([lax.bitcast_convert_type(lax.shift_left(w, 16), F32),
                            lax.bitcast_convert_type(w & HI16, F32)], axis=1)


def _dot_nt(a, b):
    return lax.dot_general(a, b, (((1,), (1,)), ((), ())), preferred_element_type=F32)


def _mod_kernel(c_ref, w_ref, b_ref, o_ref):
    c = c_ref[...]
    a = c * jax.nn.sigmoid(c)
    a_hi, a_lo = _split_bf16(a)
    w_hi, w_lo = _split_bf16(w_ref[...])
    o_ref[...] = _dot(a_hi, w_hi) + _dot(a_lo, w_hi) + _dot(a_hi, w_lo) + b_ref[...]


def _modulation(c_rows, w_mod, b_mod):
    depth, d, n = w_mod.shape
    bn = 1536
    return pl.pallas_call(
        _mod_kernel,
        out_shape=jax.ShapeDtypeStruct((depth, SUBLANES, n), F32),
        grid=(depth, n // bn),
        in_specs=[pl.BlockSpec((SUBLANES, d), lambda l, j: (0, 0)),
                  pl.BlockSpec((None, d, bn), lambda l, j: (l, 0, j)),
                  pl.BlockSpec((None, 1, bn), lambda l, j: (l, 0, j))],
        out_specs=pl.BlockSpec((None, SUBLANES, bn), lambda l, j: (l, 0, j)),
        compiler_params=_cparams(2),
        name="adaln_mod",
    )(c_rows, w_mod, b_mod.reshape(depth, 1, n))


def _head_mean_sq(t, ones):
    hi, lo = _split_bf16(t * t)
    return (_dot(hi, ones) + _dot(lo, ones)) * (1.0 / HEAD_DIM)


def _rotate_half(p, head):
    w = p.shape[1]
    half = head // 2
    lane = lax.broadcasted_iota(jnp.int32, (1, w), 1)
    first = (lane & (head - 1)) < half
    from_right = pltpu.roll(p, w - half, 1)
    from_left = pltpu.roll(p, half, 1)
    return jnp.where(first, -from_right, from_left)


def _inproj_kernel(x_ref, xc_ref, mod_ref, gain_ref, w_ref, tab_ref, gq_ref, gk_ref, ones_ref,
                   qa_ref, ka_ref, va_ref, qb_ref, kb_ref, vb_ref, qc_ref, kc_ref, vc_ref,
                   *, n_lat_tiles):
    x = jnp.where(pl.program_id(1) == n_lat_tiles, xc_ref[...], x_ref[...])
    mod = mod_ref[...]
    ms = jnp.mean(x * x, axis=-1, keepdims=True)
    h = (x * lax.rsqrt(ms + EPS)) * gain_ref[...]
    h = h * (1.0 + mod[1:2]) + mod[0:1]
    hb = h.astype(BF16)

    def proj(a, b):
        return _dot(hb, w_ref[:, a:b])

    pa = proj(0, 3 * W_A)
    qa_ref[...] = (pa[:, :W_A] * (HEAD_DIM ** -0.5 * LOG2E)).astype(BF16)
    ka_ref[...] = pa[:, W_A:2 * W_A].astype(BF16)
    va_ref[...] = pa[:, 2 * W_A:].astype(BF16)

    tab = tab_ref[...]
    cos_b = jnp.concatenate([tab[:, 0:LANES]] * 2, axis=1)
    sin_b = jnp.concatenate([tab[:, LANES:2 * LANES]] * 2, axis=1)
    cos_c1 = tab[:, 2 * LANES:3 * LANES]
    sin_c1 = tab[:, 3 * LANES:4 * LANES]
    cos_c = jnp.concatenate([cos_c1] * 3, axis=1)
    sin_c = jnp.concatenate([sin_c1] * 3, axis=1)

    o_b = 3 * W_A
    pb = proj(o_b, o_b + 3 * W_B)
    qb = pb[:, :W_B]
    kb = pb[:, W_B:2 * W_B]
    qb = qb * cos_b + _rotate_half(qb, DIFF_QK_DIM) * sin_b
    qb_ref[...] = (qb * (DIFF_QK_DIM ** -0.5 * LOG2E)).astype(BF16)
    kb_ref[...] = (kb * cos_b + _rotate_half(kb, DIFF_QK_DIM) * sin_b).astype(BF16)
    vb_ref[...] = pb[:, 2 * W_B:].astype(BF16)

    o_c = o_b + 3 * W_B
    pc = proj(o_c, IN_WIDTH)
    ones = ones_ref[...]
    qc = pc[:, :W_C]
    kc = pc[:, W_C:W_C + W_KC]
    nq = lax.rsqrt(_head_mean_sq(qc, ones) + EPS)
    nk = lax.rsqrt(_head_mean_sq(kc, ones[:W_KC, :W_KC]) + EPS)
    gq = gq_ref[...]
    gk = gk_ref[...]
    q = nq * (qc * gq[0:1] * cos_c + _rotate_half(qc, HEAD_DIM) * gq[1:2] * sin_c)
    qc_ref[...] = (q * (HEAD_DIM ** -0.5 * LOG2E)).astype(BF16)
    k = nk * (kc * gk[0:1] * cos_c1 + _rotate_half(kc, HEAD_DIM) * gk[1:2] * sin_c1)
    kc_ref[...] = k.astype(BF16)
    vc_ref[...] = pc[:, W_C + W_KC:].astype(BF16)


def _token_specs(d, n_lat_tiles, ctx_blk):
    return [pl.BlockSpec((None, TILE, d), lambda bi, ti: (bi, jnp.minimum(ti, n_lat_tiles - 1), 0)),
            pl.BlockSpec((None, TILE, d), lambda bi, ti: (bi, ctx_blk, 0))]


def _in_projection(x_lat, x_ctx, ctx_blk, modsel, gain, w_ext, tab, gq, gk, ones, n_lat_tiles):
    b, _, d = x_lat.shape
    n_tiles = n_lat_tiles + 1
    t_all = n_tiles * TILE
    widths = (W_A, W_A, W_A, W_B, W_B, W_B, W_C, W_KC, W_KC)
    tok = lambda bi, ti: (bi, ti, 0)
    const2 = lambda bi, ti: (0, 0)
    return pl.pallas_call(
        functools.partial(_inproj_kernel, n_lat_tiles=n_lat_tiles),
        out_shape=[jax.ShapeDtypeStruct((b, t_all, w), BF16) for w in widths],
        grid=(b, n_tiles),
        in_specs=_token_specs(d, n_lat_tiles, ctx_blk) + [
                  pl.BlockSpec((None, 6, d), lambda bi, ti: (2 * bi + (ti >= n_lat_tiles).astype(jnp.int32), 0, 0)),
                  pl.BlockSpec((1, d), const2),
                  pl.BlockSpec((d, IN_WIDTH), const2),
                  pl.BlockSpec((TILE, 4 * LANES), lambda bi, ti: (ti, 0)),
                  pl.BlockSpec((2, W_C), const2),
                  pl.BlockSpec((2, W_KC), const2),
                  pl.BlockSpec((W_C, W_C), const2)],
        out_specs=[pl.BlockSpec((None, TILE, w), tok) for w in widths],
        compiler_params=_cparams(2),
        name="in_projection",
    )(x_lat, x_ctx, modsel, gain, w_ext, tab, gq, gk, ones)


def _flash_kernel(q_ref, k_ref, v_ref, aux_ref, o_ref, va_ref, vb_ref, qs_ref, acc_ref, m_ref,
                  s0_ref, s1_ref, mb0_ref, mb1_ref, *,
                  n_qblk, n_sub, tk, n_lat_blocks, pairs_per_step, ctx_start, queries, mode, lam_init):
    sub_w = LANES // n_sub
    half = LANES // 2
    lane = lax.broadcasted_iota(jnp.int32, (1, LANES), 1)
    lower = lane < half
    n_pieces = n_qblk * n_sub
    ma = (n_pieces // 2) * TILE
    m_rows = n_pieces * TILE

    @pl.when(pl.program_id(2) == 0)
    def _():
        v = v_ref[...].astype(F32)
        va_ref[...] = jnp.where(lower, v, 1.0).astype(BF16)
        vb_ref[...] = jnp.where(lower, 1.0, v).astype(BF16)

    ia, ib = 0, n_pieces // 2
    for blk in range(n_qblk):
        qf = q_ref[:, blk * LANES:(blk + 1) * LANES].astype(F32)
        for sub in range(n_sub):
            msk = (lane >= sub * sub_w) & (lane < (sub + 1) * sub_w)
            piece = jnp.where(msk, qf, 0.0).astype(BF16)
            if sub * sub_w < half:
                qs_ref[ia * TILE:(ia + 1) * TILE, :] = piece
                ia += 1
            else:
                qs_ref[ib * TILE:(ib + 1) * TILE, :] = piece
                ib += 1

    s_bufs = (s0_ref, s1_ref)
    mb_bufs = (mb0_ref, mb1_ref)

    def scores(start, size, slot):
        s = _dot_nt(qs_ref[...], k_ref[pl.ds(start, size), :])
        s_bufs[slot][:, :size] = s
        mb = jnp.max(s, axis=-1, keepdims=True)
        mb_bufs[slot][...] = jnp.broadcast_to(mb, (m_rows, LANES))

    def accumulate(start, size, slot, first):
        mb = mb_bufs[slot][...]
        if first:
            m_new = mb
        else:
            m_old = m_ref[...]
            m_new = jnp.maximum(m_old, mb)
        s_ref = s_bufs[slot]
        cols = [s_ref[:, c * LANES:(c + 1) * LANES] - m_new for c in range(size // LANES)]
        p = jnp.concatenate([jnp.exp2(d.astype(BF16)) for d in cols], axis=1)
        pva = _dot(p[:ma], va_ref[pl.ds(start, size), :])
        pvb = _dot(p[ma:], vb_ref[pl.ds(start, size), :])
        if first:
            acc_ref[:ma, :] = pva
            acc_ref[ma:, :] = pvb
        else:
            alpha = jnp.exp2(m_old - m_new)
            acc_ref[:ma, :] = alpha[:ma] * acc_ref[:ma, :] + pva
            acc_ref[ma:, :] = alpha[ma:] * acc_ref[ma:, :] + pvb
        m_ref[...] = m_new

    def lat(j):
        return pl.multiple_of(j * tk, tk)

    def latent_queries():
        scores(ctx_start, CTX_LEN, 0)
        scores(lat(0), tk, 1)
        accumulate(ctx_start, CTX_LEN, 0, True)

        def pair(i):
            scores(lat(2 * i + 1), tk, 0)
            accumulate(lat(2 * i), tk, 1, False)
            scores(lat(2 * i + 2), tk, 1)
            accumulate(lat(2 * i + 1), tk, 0, False)

        def body(i, carry):
            for u in range(pairs_per_step):
                pair(i * pairs_per_step + u)
            return carry

        n_pairs = (n_lat_blocks - 2) // 2
        n_steps = n_pairs // pairs_per_step
        lax.fori_loop(0, n_steps, body, 0)
        for i in range(n_steps * pairs_per_step, n_pairs):
            pair(i)
        scores(lat(n_lat_blocks - 1), tk, 0)
        accumulate(lat(n_lat_blocks - 2), tk, 1, False)
        accumulate(lat(n_lat_blocks - 1), tk, 0, False)

    def context_queries():
        scores(ctx_start, CTX_LEN, 0)
        accumulate(ctx_start, CTX_LEN, 0, True)

    if queries == "latent":
        latent_queries()
    else:
        context_queries()

    acc = acc_ref[...]
    r = acc / pltpu.roll(acc, half, 1)
    ra, rb = r[:ma], r[ma:]
    if mode == "plain":
        for i in range(n_pieces // 2):
            o = jnp.where(lower, ra[i * TILE:(i + 1) * TILE], rb[i * TILE:(i + 1) * TILE])
            o_ref[:, i * LANES:(i + 1) * LANES] = o.astype(BF16)
    else:
        aux = aux_ref[...]
        l1 = jnp.sum(aux[0:1] * aux[1:2], axis=-1, keepdims=True)
        l2 = jnp.sum(aux[2:3] * aux[3:4], axis=-1, keepdims=True)
        lam = jnp.exp(l1) - jnp.exp(l2) + lam_init
        oa = ra[:TILE] - lam * ra[TILE:]
        ob = rb[:TILE] - lam * rb[TILE:]
        o = jnp.where(lower, oa, ob)
        sq = o * o
        ss_a = jnp.sum(jnp.where(lower, sq, 0.0), axis=-1, keepdims=True)
        ss_b = jnp.sum(jnp.where(lower, 0.0, sq), axis=-1, keepdims=True)
        ms = jnp.where(lower, ss_a, ss_b) * (1.0 / HEAD_DIM)
        o = (o * lax.rsqrt(ms + EPS)) * aux[4:5]
        o_ref[...] = (o * (1.0 - lam_init)).astype(BF16)


def _flash(q, k, v, aux, *, n_qblk, n_sub, n_hp, n_lat, queries, mode, lam_init=0.0):
    b, t_all, _ = q.shape
    qw = n_qblk * LANES
    tk = FLASH_TK
    assert n_lat % (2 * tk) == 0 and tk >= CTX_LEN
    m_rows = n_qblk * n_sub * TILE
    n_pairs = max((n_lat // tk - 2) // 2, 1)
    pairs = n_pairs if m_rows <= FLASH_UNROLL_MAX_ROWS else PAIRS_PER_STEP
    if queries == "latent":
        n_qt, qt_off, kv_rows, kv_blk, ctx_start = n_lat // TILE, 0, t_all, 0, n_lat
    else:
        n_qt, qt_off, kv_rows, kv_blk, ctx_start = 1, n_lat // TILE, CTX_LEN, n_lat // CTX_LEN, 0
    kern = functools.partial(_flash_kernel, n_qblk=n_qblk, n_sub=n_sub, tk=tk,
                             n_lat_blocks=n_lat // tk, ctx_start=ctx_start, queries=queries,
                             pairs_per_step=pairs, mode=mode, lam_init=lam_init)
    return pl.pallas_call(
        kern,
        out_shape=jax.ShapeDtypeStruct((b, n_qt * TILE, n_hp * qw), BF16),
        grid=(b, n_hp, n_qt),
        in_specs=[pl.BlockSpec((None, TILE, qw), lambda bi, hp, qt: (bi, qt + qt_off, hp)),
                  pl.BlockSpec((None, kv_rows, LANES), lambda bi, hp, qt: (bi, kv_blk, hp)),
                  pl.BlockSpec((None, kv_rows, LANES), lambda bi, hp, qt: (bi, kv_blk, hp)),
                  pl.BlockSpec((SUBLANES, LANES), lambda bi, hp, qt: (0, 0))],
        out_specs=pl.BlockSpec((None, TILE, qw), lambda bi, hp, qt: (bi, qt, hp)),
        scratch_shapes=[pltpu.VMEM((kv_rows, LANES), BF16),
                        pltpu.VMEM((kv_rows, LANES), BF16),
                        pltpu.VMEM((m_rows, LANES), BF16),
                        pltpu.VMEM((m_rows, LANES), F32),
                        pltpu.VMEM((m_rows, LANES), F32),
                        pltpu.VMEM((m_rows, tk), F32),
                        pltpu.VMEM((m_rows, tk), F32),
                        pltpu.VMEM((m_rows, LANES), F32),
                        pltpu.VMEM((m_rows, LANES), F32)],
        compiler_params=_cparams(3),
        name="flash_" + mode,
    )(q, k, v, aux)


def _na_kernel(q_ref, k0, k1, k2, k3, v0, v1, v2, v3, kc_ref, vc_ref, bias_ref, o_ref, s_ref, m_ref):
    lane = lax.broadcasted_iota(jnp.int32, (1, LANES), 1)
    lower = lane < LANES // 2
    qf = q_ref[...].astype(F32)
    k_all = jnp.concatenate([k0[...], k1[...], k2[...], k3[...], kc_ref[...]], axis=0)
    v_all = jnp.concatenate([v0[...], v1[...], v2[...], v3[...], vc_ref[...]], axis=0).astype(F32)
    v_h = [jnp.where(lower, v_all, 1.0).astype(BF16), jnp.where(lower, 1.0, v_all).astype(BF16)]
    q_h = [jnp.where(lower, qf, 0.0).astype(BF16), jnp.where(lower, 0.0, qf).astype(BF16)]
    n_pair = NA_KROWS // 2
    rows_per_part = NA_QROWS // NA_PARTS
    half_q = rows_per_part * GRID_W
    no_bias = jnp.zeros((GRID_W, CTX_LEN), F32)

    def scores(part):
        rows = slice(part * half_q, (part + 1) * half_q)
        qs = jnp.concatenate([q_h[0][rows], q_h[1][rows]], axis=0)
        bias = jnp.concatenate(
            [jnp.concatenate([bias_ref[hh, a * n_pair + j] for j in range(n_pair)] + [no_bias], axis=1)
             for hh in range(2) for a in range(part * rows_per_part, (part + 1) * rows_per_part)],
            axis=0)
        s = _dot_nt(qs, k_all) + bias
        s_ref[part] = s
        m_ref[part] = jnp.broadcast_to(jnp.max(s, axis=-1, keepdims=True), (2 * half_q, LANES))

    def finish(part):
        s = s_ref[part]
        m = m_ref[part]
        p = jnp.concatenate([jnp.exp2((s[:, c * LANES:(c + 1) * LANES] - m).astype(BF16))
                             for c in range(s.shape[1] // LANES)], axis=1)
        o0 = _dot(p[:half_q], v_h[0])
        o1 = _dot(p[half_q:], v_h[1])
        o0 = o0 / pltpu.roll(o0, LANES // 2, 1)
        o1 = o1 / pltpu.roll(o1, LANES // 2, 1)
        o_ref[part * half_q:(part + 1) * half_q, :] = jnp.where(lower, o0, o1).astype(BF16)

    scores(0)
    for part in range(1, NA_PARTS):
        scores(part)
        finish(part - 1)
    finish(NA_PARTS - 1)


def _neighbourhood_attention(qa, ka, va, bias, n_lat):
    b = qa.shape[0]
    q_tok = NA_QROWS * GRID_W
    v_tok = q_tok // 2
    n_rb = n_lat // q_tok
    n_view = n_lat // v_tok
    ctx_blk = n_lat // v_tok

    def view(j):
        return lambda rb, hp, bi: (bi, jnp.clip(2 * rb - 1 + j, 0, n_view - 1), hp)

    kv_specs = [pl.BlockSpec((None, v_tok, LANES), view(j)) for j in range(4)]
    ctx_spec = pl.BlockSpec((None, CTX_LEN, LANES), lambda rb, hp, bi: (bi, ctx_blk, hp))

    def bias_map(rb, hp, bi):
        pat = jnp.where(rb == 0, 0, jnp.where(rb == n_rb - 1, 2, 1))
        return (hp, pat, 0, 0, 0)

    return pl.pallas_call(
        _na_kernel,
        out_shape=jax.ShapeDtypeStruct((b, n_lat, W_A), BF16),
        grid=(n_rb, NA_HEADS // 2, b),
        in_specs=[pl.BlockSpec((None, q_tok, LANES), lambda rb, hp, bi: (bi, rb, hp))]
                 + kv_specs + kv_specs + [ctx_spec, ctx_spec,
                 pl.BlockSpec((2, None, NA_QROWS * NA_KROWS // 2, GRID_W, 2 * GRID_W), bias_map)],
        out_specs=pl.BlockSpec((None, q_tok, LANES), lambda rb, hp, bi: (bi, rb, hp)),
        scratch_shapes=[pltpu.VMEM((NA_PARTS, 2 * q_tok // NA_PARTS, NA_KROWS * GRID_W + CTX_LEN), F32),
                        pltpu.VMEM((NA_PARTS, 2 * q_tok // NA_PARTS, LANES), F32)],
        compiler_params=_cparams(3),
        name="neighbourhood_attention",
    )(qa, ka, ka, ka, ka, va, va, va, va, ka, va, bias)


def _na_bias_table(rpb, rows):
    cols = np.arange(GRID_W)
    c0 = np.clip(cols - NA_WIN_W // 2, 0, GRID_W - NA_WIN_W)
    cc = cols[None, :]
    col_ok = (cc >= c0[:, None]) & (cc < c0[:, None] + NA_WIN_W)
    dc = np.clip(cc - cols[:, None] + (NA_WIN_W - 1), 0, 2 * NA_WIN_W - 2)
    e = jnp.where(col_ok[None, None], (rpb.astype(F32) * LOG2E)[:, :, dc], NEG)
    e = jnp.concatenate([e, jnp.full_like(e[:, :1], NEG)], axis=1)
    a = np.arange(NA_QROWS)[:, None]
    i = np.arange(NA_KROWS)[None, :]
    pats = []
    for r_base in (0, NA_QROWS, rows - NA_QROWS):
        r = r_base + a
        key_row = r_base - NA_WIN_H // 2 + i
        r0 = np.clip(r - NA_WIN_H // 2, 0, rows - NA_WIN_H)
        ok = (key_row >= r0) & (key_row < r0 + NA_WIN_H) & (key_row >= 0) & (key_row < rows)
        dr = np.where(ok, key_row - r + (NA_WIN_H - 1), 2 * NA_WIN_H - 1)
        pats.append(dr)
    dr_all = np.stack(pats)
    pairs = dr_all.reshape(-1, 2)
    uniq, inv = np.unique(pairs, axis=0, return_inverse=True)
    pair_blocks = jnp.concatenate([e[:, uniq[:, 0]], e[:, uniq[:, 1]]], axis=-1)
    t = pair_blocks[:, inv.reshape(-1)]
    return t.reshape(NA_HEADS, 3, NA_QROWS * NA_KROWS // 2, GRID_W, 2 * GRID_W)


def _outproj_kernel(x_ref, xc_ref, oa_ref, ob_ref, oc_ref, wa_ref, wb_ref, wc_ref, mod_ref, gain_ref,
                    wrh_ref, wrl_ref, br_ref, tri_ref, x1_ref, tok_ref, route_ref, cnt_ref, run_ref,
                    *, region, group_batches, n_lat_tiles):
    mod = mod_ref[...]
    y = _dot(oa_ref[...], wa_ref[...]) + _dot(ob_ref[...], wb_ref[...]) + _dot(oc_ref[...], wc_ref[...])
    x1 = jnp.where(pl.program_id(1) == n_lat_tiles, xc_ref[...], x_ref[...]) + mod[2:3] * y
    x1_ref[...] = x1
    ms = jnp.mean(x1 * x1, axis=-1, keepdims=True)
    t = (x1 * lax.rsqrt(ms + EPS)) * gain_ref[...]
    t = t * (1.0 + mod[4:5]) + mod[3:4]
    tok_ref[...] = _pack_bf16_pairs(t)

    t_hi, t_lo = _split_bf16(t)
    wrh = wrh_ref[...]
    logits = _dot(t_hi, wrh) + _dot(t_lo, wrh) + _dot(t_hi, wrl_ref[...]) + br_ref[...]

    lane = lax.broadcasted_iota(jnp.int32, logits.shape, 1)
    lane_f = lane.astype(F32)
    is_g = lane < N_GROUPS
    gl = jnp.where(is_g, logits, NEG)
    gmax = jnp.max(gl, axis=-1, keepdims=True)
    g_sel = jnp.min(jnp.where(gl == gmax, lane_f, 1e9), axis=-1, keepdims=True)
    p_grp = 1.0 / jnp.sum(jnp.where(is_g, jnp.exp(gl - gmax), 0.0), axis=-1, keepdims=True)
    grp_of_lane = lax.shift_right_arithmetic(lane - N_GROUPS, 2).astype(F32)
    in_grp = (lane >= N_GROUPS) & (lane < N_GROUPS + N_EXPERTS) & (grp_of_lane == g_sel)
    el = jnp.where(in_grp, logits, NEG)
    v1 = jnp.max(el, axis=-1, keepdims=True)
    i1 = jnp.min(jnp.where(el == v1, lane_f, 1e9), axis=-1, keepdims=True)
    el2 = jnp.where(lane_f == i1, NEG, el)
    v2 = jnp.max(el2, axis=-1, keepdims=True)
    i2 = jnp.min(jnp.where(el2 == v2, lane_f, 1e9), axis=-1, keepdims=True)
    e2 = jnp.exp(v2 - v1)
    den = 1.0 + e2
    w1 = p_grp / den
    w2 = p_grp * e2 / den

    @pl.when((lax.rem(pl.program_id(0), group_batches) == 0) & (pl.program_id(1) == 0))
    def _():
        run_ref[...] = jnp.zeros(run_ref.shape, F32)

    ind = jnp.where(lane_f == i1, 1.0, 0.0) + jnp.where(lane_f == i2, 1.0, 0.0)
    rank = _dot(tri_ref[...], ind.astype(BF16)) + run_ref[0:1, :]

    def pick(m, l):
        return jnp.sum(jnp.where(lane_f == l, m, 0.0), axis=-1, keepdims=True)

    pos1 = (i1 - N_GROUPS) * region + pick(rank, i1)
    pos2 = (i2 - N_GROUPS) * region + pick(rank, i2)
    route_ref[...] = jnp.where(lane == 0, pos1, jnp.where(lane == 1, pos2,
                               jnp.where(lane == 2, w1, jnp.where(lane == 3, w2, 0.0))))
    run = run_ref[...] + jnp.sum(ind, axis=0, keepdims=True)
    run_ref[...] = run
    cnt_ref[...] = run


def _out_projection(x_lat, x_ctx, ctx_blk, oa, ob, oc, wa, wb, wc, modsel, gain, wrh, wrl, br, n_tiles,
                    n_lat_tiles):
    b, _, d = x_lat.shape
    tok = lambda bi, ti: (bi, ti, 0)
    const2 = lambda bi, ti: (0, 0)
    rows = n_tiles * TILE
    nb = b // MOE_GROUPS if b % MOE_GROUPS == 0 else b
    tri = jnp.asarray(np.tril(np.ones((TILE, TILE), np.float32), -1), BF16)
    return pl.pallas_call(
        functools.partial(_outproj_kernel, region=nb * rows, group_batches=nb, n_lat_tiles=n_lat_tiles),
        out_shape=[jax.ShapeDtypeStruct((b, rows, d), F32),
                   jax.ShapeDtypeStruct((b, rows, d // 2), jnp.int32),
                   jax.ShapeDtypeStruct((b, rows, LANES), F32),
                   jax.ShapeDtypeStruct((SUBLANES * (b // nb), LANES), F32)],
        grid=(b, n_tiles),
        in_specs=_token_specs(d, n_lat_tiles, ctx_blk) + [
                  pl.BlockSpec((None, TILE, W_A), tok),
                  pl.BlockSpec((None, TILE, W_B), tok),
                  pl.BlockSpec((None, TILE, W_C), tok),
                  pl.BlockSpec((W_A, d), const2),
                  pl.BlockSpec((W_B, d), const2),
                  pl.BlockSpec((W_C, d), const2),
                  pl.BlockSpec((None, 6, d), lambda bi, ti: (2 * bi + (ti >= n_lat_tiles).astype(jnp.int32), 0, 0)),
                  pl.BlockSpec((1, d), const2),
                  pl.BlockSpec((d, LANES), const2),
                  pl.BlockSpec((d, LANES), const2),
                  pl.BlockSpec((1, LANES), const2),
                  pl.BlockSpec((TILE, TILE), const2)],
        out_specs=[pl.BlockSpec((None, TILE, d), tok),
                   pl.BlockSpec((None, TILE, d // 2), tok),
                   pl.BlockSpec((None, TILE, LANES), tok),
                   pl.BlockSpec((SUBLANES, LANES), lambda bi, ti: (bi // nb, 0))],
        scratch_shapes=[pltpu.VMEM((SUBLANES, LANES), F32)],
        compiler_params=_cparams(2),
        name="out_projection",
    )(x_lat, x_ctx, oa, ob, oc, wa, wb, wc, modsel, gain, wrh, wrl, br, tri)


def _sc_mesh():
    return plsc.VectorSubcoreMesh(core_axis_name="core", subcore_axis_name="subcore")


def _sc_worker_base(per_worker):
    wid = lax.axis_index("subcore") * SC_CORES + lax.axis_index("core")
    return wid * per_worker


def _sc_scratch(d, dtype):
    return ([pltpu.VMEM((SC_ROWS,), jnp.int32)] * SC_BUFS + [pltpu.VMEM((SC_ROWS, d), dtype)] * SC_BUFS
            + [pltpu.SemaphoreType.DMA] * (2 * SC_BUFS))


def _sc_split(scratch):
    return (scratch[:SC_BUFS], scratch[SC_BUFS:2 * SC_BUFS], scratch[2 * SC_BUFS:3 * SC_BUFS],
            scratch[3 * SC_BUFS:])


def _sc_chunk_loop(per_worker, group):
    chunks = per_worker // SC_ROWS
    full = chunks // SC_BUFS * SC_BUFS

    @pl.loop(0, full, step=SC_BUFS)
    def _(c):
        group(c, SC_BUFS)

    if chunks > full:
        group(full, chunks - full)


def _sc_scatter_rows(x, row_off, n, idx, n_out):
    d = x.shape[1]
    per_worker = 2 * n // (SC_CORES * SC_SUBCORES)
    assert per_worker % SC_ROWS == 0 and n % SC_ROWS == 0

    @functools.partial(pl.kernel, out_type=jax.ShapeDtypeStruct((n_out, d), x.dtype),
                       mesh=_sc_mesh(), scratch_types=_sc_scratch(d, x.dtype))
    def scatter(x_hbm, i_hbm, o_hbm, *scratch):
        idx_v, rows_v, sem_in, sem_out = _sc_split(scratch)
        base = _sc_worker_base(per_worker)

        def group(c, n_bufs):
            reads = []
            for u in range(n_bufs):
                a = pl.multiple_of(base + (c + u) * SC_ROWS, SC_ROWS)
                t = pl.multiple_of(row_off + lax.rem(a, n), SC_ROWS)
                pltpu.sync_copy(i_hbm.at[pl.ds(a, SC_ROWS)], idx_v[u])
                reads.append(pltpu.async_copy(x_hbm.at[pl.ds(t, SC_ROWS)], rows_v[u], sem_in[u]))
            writes = []
            for u in range(n_bufs):
                reads[u].wait()
                writes.append(pltpu.async_copy(rows_v[u], o_hbm.at[idx_v[u]], sem_out[u]))
            for w in writes:
                w.wait()

        _sc_chunk_loop(per_worker, group)

    return scatter(x, idx)


def _sc_gather_rows(src, idx):
    m = idx.shape[0]
    d = src.shape[1]
    per_worker = m // (SC_CORES * SC_SUBCORES)
    assert per_worker % SC_ROWS == 0

    @functools.partial(pl.kernel, out_type=jax.ShapeDtypeStruct((m, d), src.dtype),
                       mesh=_sc_mesh(), scratch_types=_sc_scratch(d, src.dtype))
    def gather(s_hbm, i_hbm, o_hbm, *scratch):
        idx_v, rows_v, sem_in, sem_out = _sc_split(scratch)
        base = _sc_worker_base(per_worker)

        def group(c, n_bufs):
            offs, reads = [], []
            for u in range(n_bufs):
                a = pl.multiple_of(base + (c + u) * SC_ROWS, SC_ROWS)
                offs.append(a)
                pltpu.sync_copy(i_hbm.at[pl.ds(a, SC_ROWS)], idx_v[u])
                reads.append(pltpu.async_copy(s_hbm.at[idx_v[u]], rows_v[u], sem_in[u]))
            writes = []
            for u in range(n_bufs):
                reads[u].wait()
                writes.append(pltpu.async_copy(rows_v[u], o_hbm.at[pl.ds(offs[u], SC_ROWS)], sem_out[u]))
            for w in writes:
                w.wait()

        _sc_chunk_loop(per_worker, group)

    return gather(src, idx)


def _expert_ffn_kernel(blk_ref, exp_ref, x_ref, wg_ref, wu_ref, wd_ref, y_ref, wgb_ref, wub_ref, wdb_ref):
    j = pl.program_id(0)

    @pl.when((j == 0) | (exp_ref[j] != exp_ref[jnp.maximum(j - 1, 0)]))
    def _():
        wgb_ref[...] = wg_ref[...].astype(BF16)
        wub_ref[...] = wu_ref[...].astype(BF16)
        wdb_ref[...] = wd_ref[...].astype(BF16)

    x = _unpack_bf16_pairs(x_ref[...]).astype(BF16)
    hid = jax.nn.silu(_dot(x, wgb_ref[...])) * _dot(x, wub_ref[...])
    y_ref[...] = _pack_bf16_pairs(_dot(hid.astype(BF16), wdb_ref[...]))


def _expert_ffn(xs, blk, exp, wg, wu, wd, layer):
    rows, d_packed = xs.shape
    d = 2 * d_packed
    w_map = lambda j, blk, exp: (layer, exp[j], 0, 0)
    return pl.pallas_call(
        _expert_ffn_kernel,
        out_shape=jax.ShapeDtypeStruct((rows, d_packed), jnp.int32),
        grid_spec=pltpu.PrefetchScalarGridSpec(
            num_scalar_prefetch=2,
            grid=(blk.shape[0],),
            in_specs=[pl.BlockSpec((MOE_TILE, d_packed), lambda j, blk, exp: (blk[j], 0)),
                      pl.BlockSpec((None, None, d, EXPERT_HIDDEN), w_map),
                      pl.BlockSpec((None, None, d, EXPERT_HIDDEN), w_map),
                      pl.BlockSpec((None, None, EXPERT_HIDDEN, d), w_map)],
            out_specs=pl.BlockSpec((MOE_TILE, d_packed), lambda j, blk, exp: (blk[j], 0)),
            scratch_shapes=[pltpu.VMEM((d, EXPERT_HIDDEN), BF16),
                            pltpu.VMEM((d, EXPERT_HIDDEN), BF16),
                            pltpu.VMEM((EXPERT_HIDDEN, d), BF16)]),
        compiler_params=_cparams(1),
        name="expert_ffn",
    )(blk, exp, xs, wg, wu, wd)


def _combine_kernel(x1_ref, y1_ref, y2_ref, route_ref, mod_ref, fgain_ref, *rest, final):
    o_ref = rest[-1]
    route = route_ref[...]
    y = route[:, 2:3] * _unpack_bf16_pairs(y1_ref[...]) + route[:, 3:4] * _unpack_bf16_pairs(y2_ref[...])
    x2 = x1_ref[...] + mod_ref[5:6, :] * y
    if final:
        ms = jnp.mean(x2 * x2, axis=-1, keepdims=True)
        x2 = (x2 * lax.rsqrt(ms + EPS)) * fgain_ref[...]
    o_ref[...] = x2


def _combine(x1, ys, route, modsel, fgain, prev, b0, nb, n_lat_tiles, final):
    b, rows, d = x1.shape
    n_t = rows // TILE
    tok = lambda bi, ti: (b0 + bi, ti, 0)
    in_specs = [pl.BlockSpec((None, TILE, d), tok),
                pl.BlockSpec((TILE, d // 2), lambda bi, ti: (bi * n_t + ti, 0)),
                pl.BlockSpec((TILE, d // 2), lambda bi, ti: ((nb + bi) * n_t + ti, 0)),
                pl.BlockSpec((None, TILE, LANES), tok),
                pl.BlockSpec((None, 6, d),
                             lambda bi, ti: (2 * (b0 + bi) + (ti >= n_lat_tiles).astype(jnp.int32), 0, 0)),
                pl.BlockSpec((1, d), lambda bi, ti: (0, 0))]
    args = [x1, ys, ys, route, modsel, fgain]
    aliases = {}
    if prev is not None:
        in_specs.append(pl.BlockSpec(memory_space=pl.ANY))
        args.append(prev)
        aliases = {len(args) - 1: 0}
    return pl.pallas_call(
        functools.partial(_combine_kernel, final=final),
        out_shape=jax.ShapeDtypeStruct((b, rows, d), F32),
        grid=(nb, n_t),
        in_specs=in_specs,
        out_specs=pl.BlockSpec((None, TILE, d), tok),
        input_output_aliases=aliases,
        compiler_params=_cparams(2),
        name="moe_combine",
    )(*args)


def _routed_moe(tok, route, cnt, x1, wg, wu, wd, layer, modsel, fgain, n_lat_tiles, final):
    b, rows, d = x1.shape
    n_groups = cnt.shape[0] // SUBLANES
    nb = b // n_groups
    n = nb * rows
    flat = route.reshape(b * rows, LANES)
    tok_flat = tok.reshape(b * rows, tok.shape[2])
    out = None
    for g in range(n_groups):
        part = flat[g * n:(g + 1) * n]
        idx = jnp.concatenate([part[:, 0], part[:, 1]]).astype(jnp.int32)
        xs = _sc_scatter_rows(tok_flat, g * n, n, idx, N_EXPERTS * n)

        counts = cnt[SUBLANES * g, N_GROUPS:N_GROUPS + N_EXPERTS].astype(jnp.int32)
        tiles = (counts + MOE_TILE - 1) // MOE_TILE
        ends = jnp.cumsum(tiles)
        n_sched = 2 * n // MOE_TILE + N_EXPERTS
        j = jnp.minimum(jnp.arange(n_sched, dtype=jnp.int32), ends[-1] - 1)
        exp = jnp.sum((j[:, None] >= ends[None, :]).astype(jnp.int32), axis=1)
        blk = exp * (n // MOE_TILE) + j - (ends - tiles)[exp]

        ys = _expert_ffn(xs, blk, exp, wg, wu, wd, layer)
        yg = _sc_gather_rows(ys, idx)
        out = _combine(x1, yg, route, modsel, fgain, out, g * nb, nb, n_lat_tiles, final)
    return out


def _rope_tables(n_lat):
    t = jnp.arange(n_lat)
    row = (t // GRID_W).astype(F32)
    col = (t % GRID_W).astype(F32)

    def cs(dim):
        quarter = dim // 4
        freqs = ROPE_THETA ** (-jnp.arange(quarter, dtype=F32) / quarter)
        ang = jnp.concatenate([row[:, None] * freqs, col[:, None] * freqs], axis=-1)
        cos = jnp.tile(jnp.cos(ang), (1, 2 * LANES // dim))
        sin = jnp.tile(jnp.sin(ang), (1, 2 * LANES // dim))
        cos = jnp.concatenate([cos, jnp.ones((CTX_LEN, LANES), F32)], axis=0)
        sin = jnp.concatenate([sin, jnp.zeros((CTX_LEN, LANES), F32)], axis=0)
        return cos, sin

    cos_b, sin_b = cs(DIFF_QK_DIM)
    cos_c, sin_c = cs(HEAD_DIM)
    return jnp.concatenate([cos_b, sin_b, cos_c, sin_c], axis=1)


def _reordered_w_in(w_in):
    o_c = 3 * W_A + 3 * W_B
    heads = [w_in[:, o_c + h * HEAD_DIM:o_c + (h + 1) * HEAD_DIM] for h in GQA_Q_ORDER]
    return jnp.concatenate([w_in[:, :o_c]] + heads + [w_in[:, o_c + W_C:]], axis=1).astype(BF16)


def kernel(x, c, ctx, c_ctx, w_mod, b_mod, norm_attn, norm_ffn, w_in, w_out, na_rpb, diff_lambda_q1, diff_lambda_k1, diff_lambda_q2, diff_lambda_k2, diff_subln, gqa_q_norm, gqa_k_norm, router_group_w, router_group_b, router_expert_w, router_expert_b, w_gate, w_up, w_down, final_norm):
    b, s, d = x.shape
    assert d == D_MODEL and ctx.shape[1] == CTX_LEN and s % (NA_QROWS * GRID_W) == 0
    rows = s // GRID_W
    assert rows >= 2 * NA_QROWS
    t_all = s + CTX_LEN
    n_lat_tiles = s // TILE

    assert b + 1 <= SUBLANES
    c_rows = jnp.zeros((SUBLANES, d), F32).at[:b].set(c).at[b].set(c_ctx)
    mod = _modulation(c_rows, w_mod, b_mod)

    tab = _rope_tables(s)
    hidx = np.arange(HEAD_DIM)
    partner = np.where(hidx < HEAD_DIM // 2, hidx + HEAD_DIM // 2, hidx - HEAD_DIM // 2)
    blk = np.arange(W_C) // HEAD_DIM
    ones = jnp.asarray((blk[:, None] == blk[None, :]).astype(np.float32), BF16)
    dummy_aux = jnp.zeros((SUBLANES, LANES), F32)

    x_lat, x_ctx, ctx_blk = x, ctx, 0
    for l in range(DEPTH):
        ctx_out = l < DEPTH - 1
        lam_init = 0.8 - 0.6 * math.exp(-0.3 * l)
        m_lat = mod[l, :b].reshape(b, 1, 6, d)
        m_ctx = jnp.broadcast_to(mod[l, b].reshape(1, 1, 6, d), (b, 1, 6, d))
        modsel = jnp.concatenate([m_lat, m_ctx], axis=1).reshape(2 * b, 6, d)

        gq = jnp.stack([jnp.tile(gqa_q_norm[l], GQA_Q_HEADS), jnp.tile(gqa_q_norm[l][partner], GQA_Q_HEADS)])
        gk = jnp.stack([jnp.tile(gqa_k_norm[l], GQA_KV_HEADS), jnp.tile(gqa_k_norm[l][partner], GQA_KV_HEADS)])
        qa, ka, va, qb, kb, vb, qc, kc, vc = _in_projection(
            x_lat, x_ctx, ctx_blk, modsel, norm_attn[l][None], _reordered_w_in(w_in[l]), tab, gq, gk, ones,
            n_lat_tiles)

        n_qt = n_lat_tiles + 1 if ctx_out else n_lat_tiles
        pad = lambda v: jnp.pad(v, (0, LANES - v.shape[0]))
        aux = jnp.stack([pad(diff_lambda_q1[l]), pad(diff_lambda_k1[l]), pad(diff_lambda_q2[l]),
                         pad(diff_lambda_k2[l]), jnp.tile(diff_subln[l], 2),
                         jnp.zeros((LANES,), F32), jnp.zeros((LANES,), F32), jnp.zeros((LANES,), F32)])
        group_a = dict(n_qblk=1, n_sub=2, n_hp=NA_HEADS // 2, n_lat=s, mode="plain")
        group_b = dict(n_qblk=1, n_sub=4, n_hp=DIFF_HEADS // 2, n_lat=s, mode="diff", lam_init=lam_init)
        group_c = dict(n_qblk=3, n_sub=2, n_hp=1, n_lat=s, mode="plain")
        oa = _neighbourhood_attention(qa, ka, va, _na_bias_table(na_rpb[l], rows), s)
        ob = _flash(qb, kb, vb, aux, queries="latent", **group_b)
        oc = _flash(qc, kc, vc, dummy_aux, queries="latent", **group_c)
        if ctx_out:
            oa = jnp.concatenate([oa, _flash(qa, ka, va, dummy_aux, queries="context", **group_a)], axis=1)
            ob = jnp.concatenate([ob, _flash(qb, kb, vb, aux, queries="context", **group_b)], axis=1)
            oc = jnp.concatenate([oc, _flash(qc, kc, vc, dummy_aux, queries="context", **group_c)], axis=1)

        w_o = w_out[l]
        o_c = W_A + W_B
        w_oc = jnp.concatenate([w_o[o_c + h * HEAD_DIM:o_c + (h + 1) * HEAD_DIM] for h in GQA_Q_ORDER], axis=0)
        wr = jnp.zeros((d, LANES), F32)
        wr = wr.at[:, :N_GROUPS].set(router_group_w[l]).at[:, N_GROUPS:N_GROUPS + N_EXPERTS].set(router_expert_w[l])
        wrh, wrl = _split_bf16(wr)
        br = jnp.zeros((1, LANES), F32)
        br = br.at[0, :N_GROUPS].set(router_group_b[l]).at[0, N_GROUPS:N_GROUPS + N_EXPERTS].set(router_expert_b[l])
        x1, tok, route, cnt = _out_projection(
            x_lat, x_ctx, ctx_blk, oa, ob, oc, w_o[:W_A].astype(BF16), w_o[W_A:W_A + W_B].astype(BF16),
            w_oc.astype(BF16), modsel, norm_ffn[l][None], wrh, wrl, br, n_qt, n_lat_tiles)
        xs = _routed_moe(tok, route, cnt, x1, w_gate, w_up, w_down, l, modsel, final_norm[None],
                         n_lat_tiles, final=not ctx_out)
        x_lat, x_ctx, ctx_blk = xs, xs, n_lat_tiles
    return xs
```

```python
import functools
import math

import numpy as np
import jax
import jax.numpy as jnp
from jax import lax
from jax.experimental import pallas as pl
from jax.experimental.pallas import tpu as pltpu
from jax.experimental.pallas import tpu_sc as plsc

F32 = jnp.float32
BF16 = jnp.bfloat16

D_MODEL = 1024
DEPTH = 2
GRID_W = 64
CTX_LEN = 256
HEAD_DIM = 64
NA_HEADS = 6
NA_WIN_H = 8
NA_WIN_W = 16
DIFF_HEADS = 4
DIFF_QK_DIM = 32
GQA_Q_HEADS = 6
GQA_KV_HEADS = 2
N_GROUPS = 4
EXPERTS_PER_GROUP = 4
N_EXPERTS = 16
EXPERT_HIDDEN = 512
ROPE_THETA = 10000.0
EPS = 1e-6
W_A = NA_HEADS * HEAD_DIM
W_B = DIFF_HEADS * 2 * DIFF_QK_DIM
W_C = GQA_Q_HEADS * HEAD_DIM
W_KC = GQA_KV_HEADS * HEAD_DIM
IN_WIDTH = 3 * W_A + 3 * W_B + W_C + 2 * W_KC

LANES = 128
TILE = CTX_LEN
NA_QROWS = 8
NA_KROWS = 16
NA_PARTS = 2
NEG = -1e30
LOG2E = 1.4426950408889634
HI16 = -65536
VMEM_LIMIT = 56 * 1024 * 1024
FLASH_TK = 512
PAIRS_PER_STEP = 2
FLASH_UNROLL_MAX_ROWS = 1536
SUBLANES = 8
MOE_TILE = 512
MOE_GROUPS = 2
SC_ROWS = 32
SC_BUFS = 4
SC_CORES = 2
SC_SUBCORES = 16

GQA_Q_ORDER = (0, 3, 1, 4, 2, 5)


def _cparams(n_axes):
    return pltpu.CompilerParams(dimension_semantics=("arbitrary",) * n_axes,
                                vmem_limit_bytes=VMEM_LIMIT)


def _split_bf16(a):
    hi = a.astype(BF16)
    lo = (a - hi.astype(F32)).astype(BF16)
    return hi, lo


def _dot(a, b):
    return jnp.dot(a, b, preferred_element_type=F32)


def _pack_bf16_pairs(t):
    bits = lax.bitcast_convert_type(t.astype(BF16).astype(F32), jnp.int32)
    half_d = bits.shape[1] // 2
    return lax.shift_right_logical(bits[:, :half_d], 16) | (bits[:, half_d:] & HI16)


def _unpack_bf16_pairs(w):
    return jnp.concatenate([lax.bitcast_convert_type(lax.shift_left(w, 16), F32),
                            lax.bitcast_convert_type(w & HI16, F32)], axis=1)


def _dot_nt(a, b):
    return lax.dot_general(a, b, (((1,), (1,)), ((), ())), preferred_element_type=F32)


def _mod_kernel(c_ref, w_ref, b_ref, o_ref):
    c = c_ref[...]
    a = c * jax.nn.sigmoid(c)
    a_hi, a_lo = _split_bf16(a)
    w_hi, w_lo = _split_bf16(w_ref[...])
    o_ref[...] = _dot(a_hi, w_hi) + _dot(a_lo, w_hi) + _dot(a_hi, w_lo) + b_ref[...]


def _modulation(c_rows, w_mod, b_mod):
    depth, d, n = w_mod.shape
    bn = 1536
    return pl.pallas_call(
        _mod_kernel,
        out_shape=jax.ShapeDtypeStruct((depth, SUBLANES, n), F32),
        grid=(depth, n // bn),
        in_specs=[pl.BlockSpec((SUBLANES, d), lambda l, j: (0, 0)),
                  pl.BlockSpec((None, d, bn), lambda l, j: (l, 0, j)),
                  pl.BlockSpec((None, 1, bn), lambda l, j: (l, 0, j))],
        out_specs=pl.BlockSpec((None, SUBLANES, bn), lambda l, j: (l, 0, j)),
        compiler_params=_cparams(2),
        name="adaln_mod",
    )(c_rows, w_mod, b_mod.reshape(depth, 1, n))


def _head_mean_sq(t, ones):
    hi, lo = _split_bf16(t * t)
    return (_dot(hi, ones) + _dot(lo, ones)) * (1.0 / HEAD_DIM)


def _rotate_half(p, head):
    w = p.shape[1]
    half = head // 2
    lane = lax.broadcasted_iota(jnp.int32, (1, w), 1)
    first = (lane & (head - 1)) < half
    from_right = pltpu.roll(p, w - half, 1)
    from_left = pltpu.roll(p, half, 1)
    return jnp.where(first, -from_right, from_left)


def _inproj_kernel(x_ref, xc_ref, mod_ref, gain_ref, w_ref, tab_ref, gq_ref, gk_ref, ones_ref,
                   qa_ref, ka_ref, va_ref, qb_ref, kb_ref, vb_ref, qc_ref, kc_ref, vc_ref,
                   *, n_lat_tiles):
    x = jnp.where(pl.program_id(1) == n_lat_tiles, xc_ref[...], x_ref[...])
    mod = mod_ref[...]
    ms = jnp.mean(x * x, axis=-1, keepdims=True)
    h = (x * lax.rsqrt(ms + EPS)) * gain_ref[...]
    h = h * (1.0 + mod[1:2]) + mod[0:1]
    hb = h.astype(BF16)

    def proj(a, b):
        return _dot(hb, w_ref[:, a:b])

    pa = proj(0, 3 * W_A)
    qa_ref[...] = (pa[:, :W_A] * (HEAD_DIM ** -0.5 * LOG2E)).astype(BF16)
    ka_ref[...] = pa[:, W_A:2 * W_A].astype(BF16)
    va_ref[...] = pa[:, 2 * W_A:].astype(BF16)

    tab = tab_ref[...]
    cos_b = jnp.concatenate([tab[:, 0:LANES]] * 2, axis=1)
    sin_b = jnp.concatenate([tab[:, LANES:2 * LANES]] * 2, axis=1)
    cos_c1 = tab[:, 2 * LANES:3 * LANES]
    sin_c1 = tab[:, 3 * LANES:4 * LANES]
    cos_c = jnp.concatenate([cos_c1] * 3, axis=1)
    sin_c = jnp.concatenate([sin_c1] * 3, axis=1)

    o_b = 3 * W_A
    pb = proj(o_b, o_b + 3 * W_B)
    qb = pb[:, :W_B]
    kb = pb[:, W_B:2 * W_B]
    qb = qb * cos_b + _rotate_half(qb, DIFF_QK_DIM) * sin_b
    qb_ref[...] = (qb * (DIFF_QK_DIM ** -0.5 * LOG2E)).astype(BF16)
    kb_ref[...] = (kb * cos_b + _rotate_half(kb, DIFF_QK_DIM) * sin_b).astype(BF16)
    vb_ref[...] = pb[:, 2 * W_B:].astype(BF16)

    o_c = o_b + 3 * W_B
    pc = proj(o_c, IN_WIDTH)
    ones = ones_ref[...]
    qc = pc[:, :W_C]
    kc = pc[:, W_C:W_C + W_KC]
    nq = lax.rsqrt(_head_mean_sq(qc, ones) + EPS)
    nk = lax.rsqrt(_head_mean_sq(kc, ones[:W_KC, :W_KC]) + EPS)
    gq = gq_ref[...]
    gk = gk_ref[...]
    q = nq * (qc * gq[0:1] * cos_c + _rotate_half(qc, HEAD_DIM) * gq[1:2] * sin_c)
    qc_ref[...] = (q * (HEAD_DIM ** -0.5 * LOG2E)).astype(BF16)
    k = nk * (kc * gk[0:1] * cos_c1 + _rotate_half(kc, HEAD_DIM) * gk[1:2] * sin_c1)
    kc_ref[...] = k.astype(BF16)
    vc_ref[...] = pc[:, W_C + W_KC:].astype(BF16)


def _token_specs(d, n_lat_tiles, ctx_blk):
    return [pl.BlockSpec((None, TILE, d), lambda bi, ti: (bi, jnp.minimum(ti, n_lat_tiles - 1), 0)),
            pl.BlockSpec((None, TILE, d), lambda bi, ti: (bi, ctx_blk, 0))]


def _in_projection(x_lat, x_ctx, ctx_blk, modsel, gain, w_ext, tab, gq, gk, ones, n_lat_tiles):
    b, _, d = x_lat.shape
    n_tiles = n_lat_tiles + 1
    t_all = n_tiles * TILE
    widths = (W_A, W_A, W_A, W_B, W_B, W_B, W_C, W_KC, W_KC)
    tok = lambda bi, ti: (bi, ti, 0)
    const2 = lambda bi, ti: (0, 0)
    return pl.pallas_call(
        functools.partial(_inproj_kernel, n_lat_tiles=n_lat_tiles),
        out_shape=[jax.ShapeDtypeStruct((b, t_all, w), BF16) for w in widths],
        grid=(b, n_tiles),
        in_specs=_token_specs(d, n_lat_tiles, ctx_blk) + [
                  pl.BlockSpec((None, 6, d), lambda bi, ti: (2 * bi + (ti >= n_lat_tiles).astype(jnp.int32), 0, 0)),
                  pl.BlockSpec((1, d), const2),
                  pl.BlockSpec((d, IN_WIDTH), const2),
                  pl.BlockSpec((TILE, 4 * LANES), lambda bi, ti: (ti, 0)),
                  pl.BlockSpec((2, W_C), const2),
                  pl.BlockSpec((2, W_KC), const2),
                  pl.BlockSpec((W_C, W_C), const2)],
        out_specs=[pl.BlockSpec((None, TILE, w), tok) for w in widths],
        compiler_params=_cparams(2),
        name="in_projection",
    )(x_lat, x_ctx, modsel, gain, w_ext, tab, gq, gk, ones)


def _flash_kernel(q_ref, k_ref, v_ref, aux_ref, o_ref, va_ref, vb_ref, qs_ref, acc_ref, m_ref,
                  s0_ref, s1_ref, mb0_ref, mb1_ref, *,
                  n_qblk, n_sub, tk, n_lat_blocks, pairs_per_step, ctx_start, queries, mode, lam_init):
    sub_w = LANES // n_sub
    half = LANES // 2
    lane = lax.broadcasted_iota(jnp.int32, (1, LANES), 1)
    lower = lane < half
    n_pieces = n_qblk * n_sub
    ma = (n_pieces // 2) * TILE
    m_rows = n_pieces * TILE

    @pl.when(pl.program_id(2) == 0)
    def _():
        v = v_ref[...].astype(F32)
        va_ref[...] = jnp.where(lower, v, 1.0).astype(BF16)
        vb_ref[...] = jnp.where(lower, 1.0, v).astype(BF16)

    ia, ib = 0, n_pieces // 2
    for blk in range(n_qblk):
        qf = q_ref[:, blk * LANES:(blk + 1) * LANES].astype(F32)
        for sub in range(n_sub):
            msk = (lane >= sub * sub_w) & (lane < (sub + 1) * sub_w)
            piece = jnp.where(msk, qf, 0.0).astype(BF16)
            if sub * sub_w < half:
                qs_ref[ia * TILE:(ia + 1) * TILE, :] = piece
                ia += 1
            else:
                qs_ref[ib * TILE:(ib + 1) * TILE, :] = piece
                ib += 1

    s_bufs = (s0_ref, s1_ref)
    mb_bufs = (mb0_ref, mb1_ref)

    def scores(start, size, slot):
        s = _dot_nt(qs_ref[...], k_ref[pl.ds(start, size), :])
        s_bufs[slot][:, :size] = s
        mb = jnp.max(s, axis=-1, keepdims=True)
        mb_bufs[slot][...] = jnp.broadcast_to(mb, (m_rows, LANES))

    def accumulate(start, size, slot, first):
        mb = mb_bufs[slot][...]
        if first:
            m_new = mb
        else:
            m_old = m_ref[...]
            m_new = jnp.maximum(m_old, mb)
        s_ref = s_bufs[slot]
        cols = [s_ref[:, c * LANES:(c + 1) * LANES] - m_new for c in range(size // LANES)]
        p = jnp.concatenate([jnp.exp2(d.astype(BF16)) for d in cols], axis=1)
        pva = _dot(p[:ma], va_ref[pl.ds(start, size), :])
        pvb = _dot(p[ma:], vb_ref[pl.ds(start, size), :])
        if first:
            acc_ref[:ma, :] = pva
            acc_ref[ma:, :] = pvb
        else:
            alpha = jnp.exp2(m_old - m_new)
            acc_ref[:ma, :] = alpha[:ma] * acc_ref[:ma, :] + pva
            acc_ref[ma:, :] = alpha[ma:] * acc_ref[ma:, :] + pvb
        m_ref[...] = m_new

    def lat(j):
        return pl.multiple_of(j * tk, tk)

    def latent_queries():
        scores(ctx_start, CTX_LEN, 0)
        scores(lat(0), tk, 1)
        accumulate(ctx_start, CTX_LEN, 0, True)

        def pair(i):
            scores(lat(2 * i + 1), tk, 0)
            accumulate(lat(2 * i), tk, 1, False)
            scores(lat(2 * i + 2), tk, 1)
            accumulate(lat(2 * i + 1), tk, 0, False)

        def body(i, carry):
            for u in range(pairs_per_step):
                pair(i * pairs_per_step + u)
            return carry

        n_pairs = (n_lat_blocks - 2) // 2
        n_steps = n_pairs // pairs_per_step
        lax.fori_loop(0, n_steps, body, 0)
        for i in range(n_steps * pairs_per_step, n_pairs):
            pair(i)
        scores(lat(n_lat_blocks - 1), tk, 0)
        accumulate(lat(n_lat_blocks - 2), tk, 1, False)
        accumulate(lat(n_lat_blocks - 1), tk, 0, False)

    def context_queries():
        scores(ctx_start, CTX_LEN, 0)
        accumulate(ctx_start, CTX_LEN, 0, True)

    if queries == "latent":
        latent_queries()
    else:
        context_queries()

    acc = acc_ref[...]
    r = acc / pltpu.roll(acc, half, 1)
    ra, rb = r[:ma], r[ma:]
    if mode == "plain":
        for i in range(n_pieces // 2):
            o = jnp.where(lower, ra[i * TILE:(i + 1) * TILE], rb[i * TILE:(i + 1) * TILE])
            o_ref[:, i * LANES:(i + 1) * LANES] = o.astype(BF16)
    else:
        aux = aux_ref[...]
        l1 = jnp.sum(aux[0:1] * aux[1:2], axis=-1, keepdims=True)
        l2 = jnp.sum(aux[2:3] * aux[3:4], axis=-1, keepdims=True)
        lam = jnp.exp(l1) - jnp.exp(l2) + lam_init
        oa = ra[:TILE] - lam * ra[TILE:]
        ob = rb[:TILE] - lam * rb[TILE:]
        o = jnp.where(lower, oa, ob)
        sq = o * o
        ss_a = jnp.sum(jnp.where(lower, sq, 0.0), axis=-1, keepdims=True)
        ss_b = jnp.sum(jnp.where(lower, 0.0, sq), axis=-1, keepdims=True)
        ms = jnp.where(lower, ss_a, ss_b) * (1.0 / HEAD_DIM)
        o = (o * lax.rsqrt(ms + EPS)) * aux[4:5]
        o_ref[...] = (o * (1.0 - lam_init)).astype(BF16)


def _flash(q, k, v, aux, *, n_qblk, n_sub, n_hp, n_lat, queries, mode, lam_init=0.0):
    b, t_all, _ = q.shape
    qw = n_qblk * LANES
    tk = FLASH_TK
    assert n_lat % (2 * tk) == 0 and tk >= CTX_LEN
    m_rows = n_qblk * n_sub * TILE
    n_pairs = max((n_lat // tk - 2) // 2, 1)
    pairs = n_pairs if m_rows <= FLASH_UNROLL_MAX_ROWS else PAIRS_PER_STEP
    if queries == "latent":
        n_qt, qt_off, kv_rows, kv_blk, ctx_start = n_lat // TILE, 0, t_all, 0, n_lat
    else:
        n_qt, qt_off, kv_rows, kv_blk, ctx_start = 1, n_lat // TILE, CTX_LEN, n_lat // CTX_LEN, 0
    kern = functools.partial(_flash_kernel, n_qblk=n_qblk, n_sub=n_sub, tk=tk,
                             n_lat_blocks=n_lat // tk, ctx_start=ctx_start, queries=queries,
                             pairs_per_step=pairs, mode=mode, lam_init=lam_init)
    return pl.pallas_call(
        kern,
        out_shape=jax.ShapeDtypeStruct((b, n_qt * TILE, n_hp * qw), BF16),
        grid=(b, n_hp, n_qt),
        in_specs=[pl.BlockSpec((None, TILE, qw), lambda bi, hp, qt: (bi, qt + qt_off, hp)),
                  pl.BlockSpec((None, kv_rows, LANES), lambda bi, hp, qt: (bi, kv_blk, hp)),
                  pl.BlockSpec((None, kv_rows, LANES), lambda bi, hp, qt: (bi, kv_blk, hp)),
                  pl.BlockSpec((SUBLANES, LANES), lambda bi, hp, qt: (0, 0))],
        out_specs=pl.BlockSpec((None, TILE, qw), lambda bi, hp, qt: (bi, qt, hp)),
        scratch_shapes=[pltpu.VMEM((kv_rows, LANES), BF16),
                        pltpu.VMEM((kv_rows, LANES), BF16),
                        pltpu.VMEM((m_rows, LANES), BF16),
                        pltpu.VMEM((m_rows, LANES), F32),
                        pltpu.VMEM((m_rows, LANES), F32),
                        pltpu.VMEM((m_rows, tk), F32),
                        pltpu.VMEM((m_rows, tk), F32),
                        pltpu.VMEM((m_rows, LANES), F32),
                        pltpu.VMEM((m_rows, LANES), F32)],
        compiler_params=_cparams(3),
        name="flash_" + mode,
    )(q, k, v, aux)


def _na_kernel(q_ref, k0, k1, k2, k3, v0, v1, v2, v3, kc_ref, vc_ref, bias_ref, o_ref, s_ref, m_ref):
    lane = lax.broadcasted_iota(jnp.int32, (1, LANES), 1)
    lower = lane < LANES // 2
    n_pair = NA_KROWS // 2
    rows_per_part = NA_QROWS // NA_PARTS
    half_q = rows_per_part * GRID_W
    no_bias = jnp.zeros((GRID_W, CTX_LEN), F32)

    def head_pair(hp):
        cols = slice(hp * LANES, (hp + 1) * LANES)
        qf = q_ref[:, cols].astype(F32)
        k_all = jnp.concatenate([r[:, cols] for r in (k0, k1, k2, k3, kc_ref)], axis=0)
        v_all = jnp.concatenate([r[:, cols] for r in (v0, v1, v2, v3, vc_ref)], axis=0).astype(F32)
        v_h = [jnp.where(lower, v_all, 1.0).astype(BF16), jnp.where(lower, 1.0, v_all).astype(BF16)]
        q_h = [jnp.where(lower, qf, 0.0).astype(BF16), jnp.where(lower, 0.0, qf).astype(BF16)]
        return q_h, k_all, v_h

    pairs = [head_pair(hp) for hp in range(NA_HEADS // 2)]
    items = [(hp, part) for hp in range(NA_HEADS // 2) for part in range(NA_PARTS)]

    def scores(n):
        hp, part = items[n]
        q_h, k_all, _ = pairs[hp]
        rows = slice(part * half_q, (part + 1) * half_q)
        qs = jnp.concatenate([q_h[0][rows], q_h[1][rows]], axis=0)
        bias = jnp.concatenate(
            [jnp.concatenate([bias_ref[2 * hp + hh, a * n_pair + j] for j in range(n_pair)] + [no_bias],
                             axis=1)
             for hh in range(2) for a in range(part * rows_per_part, (part + 1) * rows_per_part)],
            axis=0)
        s = _dot_nt(qs, k_all) + bias
        s_ref[n % 2] = s
        m_ref[n % 2] = jnp.broadcast_to(jnp.max(s, axis=-1, keepdims=True), (2 * half_q, LANES))

    def finish(n):
        hp, part = items[n]
        v_h = pairs[hp][2]
        s = s_ref[n % 2]
        m = m_ref[n % 2]
        p = jnp.concatenate([jnp.exp2((s[:, c * LANES:(c + 1) * LANES] - m).astype(BF16))
                             for c in range(s.shape[1] // LANES)], axis=1)
        o0 = _dot(p[:half_q], v_h[0])
        o1 = _dot(p[half_q:], v_h[1])
        o0 = o0 / pltpu.roll(o0, LANES // 2, 1)
        o1 = o1 / pltpu.roll(o1, LANES // 2, 1)
        o_ref[part * half_q:(part + 1) * half_q, hp * LANES:(hp + 1) * LANES] = (
            jnp.where(lower, o0, o1).astype(BF16))

    scores(0)
    for n in range(1, len(items)):
        scores(n)
        finish(n - 1)
    finish(len(items) - 1)


def _neighbourhood_attention(qa, ka, va, bias, n_lat):
    b = qa.shape[0]
    q_tok = NA_QROWS * GRID_W
    v_tok = q_tok // 2
    n_rb = n_lat // q_tok
    n_view = n_lat // v_tok
    ctx_blk = n_lat // v_tok

    def view(j):
        return lambda rb, bi: (bi, jnp.clip(2 * rb - 1 + j, 0, n_view - 1), 0)

    kv_specs = [pl.BlockSpec((None, v_tok, W_A), view(j)) for j in range(4)]
    ctx_spec = pl.BlockSpec((None, CTX_LEN, W_A), lambda rb, bi: (bi, ctx_blk, 0))

    def bias_map(rb, bi):
        pat = jnp.where(rb == 0, 0, jnp.where(rb == n_rb - 1, 2, 1))
        return (0, pat, 0, 0, 0)

    part_rows = 2 * q_tok // NA_PARTS
    return pl.pallas_call(
        _na_kernel,
        out_shape=jax.ShapeDtypeStruct((b, n_lat, W_A), BF16),
        grid=(n_rb, b),
        in_specs=[pl.BlockSpec((None, q_tok, W_A), lambda rb, bi: (bi, rb, 0))]
                 + kv_specs + kv_specs + [ctx_spec, ctx_spec,
                 pl.BlockSpec((NA_HEADS, None, NA_QROWS * NA_KROWS // 2, GRID_W, 2 * GRID_W), bias_map)],
        out_specs=pl.BlockSpec((None, q_tok, W_A), lambda rb, bi: (bi, rb, 0)),
        scratch_shapes=[pltpu.VMEM((2, part_rows, NA_KROWS * GRID_W + CTX_LEN), F32),
                        pltpu.VMEM((2, part_rows, LANES), F32)],
        compiler_params=_cparams(2),
        name="neighbourhood_attention",
    )(qa, ka, ka, ka, ka, va, va, va, va, ka, va, bias)


def _na_bias_table(rpb, rows):
    cols = np.arange(GRID_W)
    c0 = np.clip(cols - NA_WIN_W // 2, 0, GRID_W - NA_WIN_W)
    cc = cols[None, :]
    col_ok = (cc >= c0[:, None]) & (cc < c0[:, None] + NA_WIN_W)
    dc = np.clip(cc - cols[:, None] + (NA_WIN_W - 1), 0, 2 * NA_WIN_W - 2)
    e = jnp.where(col_ok[None, None], (rpb.astype(F32) * LOG2E)[:, :, dc], NEG)
    e = jnp.concatenate([e, jnp.full_like(e[:, :1], NEG)], axis=1)
    a = np.arange(NA_QROWS)[:, None]
    i = np.arange(NA_KROWS)[None, :]
    pats = []
    for r_base in (0, NA_QROWS, rows - NA_QROWS):
        r = r_base + a
        key_row = r_base - NA_WIN_H // 2 + i
        r0 = np.clip(r - NA_WIN_H // 2, 0, rows - NA_WIN_H)
        ok = (key_row >= r0) & (key_row < r0 + NA_WIN_H) & (key_row >= 0) & (key_row < rows)
        dr = np.where(ok, key_row - r + (NA_WIN_H - 1), 2 * NA_WIN_H - 1)
        pats.append(dr)
    dr_all = np.stack(pats)
    pairs = dr_all.reshape(-1, 2)
    uniq, inv = np.unique(pairs, axis=0, return_inverse=True)
    pair_blocks = jnp.concatenate([e[:, uniq[:, 0]], e[:, uniq[:, 1]]], axis=-1)
    t = pair_blocks[:, inv.reshape(-1)]
    return t.reshape(NA_HEADS, 3, NA_QROWS * NA_KROWS // 2, GRID_W, 2 * GRID_W)


def _outproj_kernel(x_ref, xc_ref, oa_ref, ob_ref, oc_ref, wa_ref, wb_ref, wc_ref, mod_ref, gain_ref,
                    wrh_ref, wrl_ref, br_ref, tri_ref, x1_ref, tok_ref, route_ref, cnt_ref, run_ref,
                    *, region, group_batches, n_lat_tiles):
    mod = mod_ref[...]
    y = _dot(oa_ref[...], wa_ref[...]) + _dot(ob_ref[...], wb_ref[...]) + _dot(oc_ref[...], wc_ref[...])
    x1 = jnp.where(pl.program_id(1) == n_lat_tiles, xc_ref[...], x_ref[...]) + mod[2:3] * y
    x1_ref[...] = x1
    ms = jnp.mean(x1 * x1, axis=-1, keepdims=True)
    t = (x1 * lax.rsqrt(ms + EPS)) * gain_ref[...]
    t = t * (1.0 + mod[4:5]) + mod[3:4]
    tok_ref[...] = _pack_bf16_pairs(t)

    t_hi, t_lo = _split_bf16(t)
    wrh = wrh_ref[...]
    logits = _dot(t_hi, wrh) + _dot(t_lo, wrh) + _dot(t_hi, wrl_ref[...]) + br_ref[...]

    lane = lax.broadcasted_iota(jnp.int32, logits.shape, 1)
    lane_f = lane.astype(F32)
    is_g = lane < N_GROUPS
    gl = jnp.where(is_g, logits, NEG)
    gmax = jnp.max(gl, axis=-1, keepdims=True)
    g_sel = jnp.min(jnp.where(gl == gmax, lane_f, 1e9), axis=-1, keepdims=True)
    p_grp = 1.0 / jnp.sum(jnp.where(is_g, jnp.exp(gl - gmax), 0.0), axis=-1, keepdims=True)
    grp_of_lane = lax.shift_right_arithmetic(lane - N_GROUPS, 2).astype(F32)
    in_grp = (lane >= N_GROUPS) & (lane < N_GROUPS + N_EXPERTS) & (grp_of_lane == g_sel)
    el = jnp.where(in_grp, logits, NEG)
    v1 = jnp.max(el, axis=-1, keepdims=True)
    i1 = jnp.min(jnp.where(el == v1, lane_f, 1e9), axis=-1, keepdims=True)
    el2 = jnp.where(lane_f == i1, NEG, el)
    v2 = jnp.max(el2, axis=-1, keepdims=True)
    i2 = jnp.min(jnp.where(el2 == v2, lane_f, 1e9), axis=-1, keepdims=True)
    e2 = jnp.exp(v2 - v1)
    den = 1.0 + e2
    w1 = p_grp / den
    w2 = p_grp * e2 / den

    @pl.when((lax.rem(pl.program_id(0), group_batches) == 0) & (pl.program_id(1) == 0))
    def _():
        run_ref[...] = jnp.zeros(run_ref.shape, F32)

    ind = jnp.where(lane_f == i1, 1.0, 0.0) + jnp.where(lane_f == i2, 1.0, 0.0)
    rank = _dot(tri_ref[...], ind.astype(BF16)) + run_ref[0:1, :]

    def pick(m, l):
        return jnp.sum(jnp.where(lane_f == l, m, 0.0), axis=-1, keepdims=True)

    pos1 = (i1 - N_GROUPS) * region + pick(rank, i1)
    pos2 = (i2 - N_GROUPS) * region + pick(rank, i2)
    route_ref[...] = jnp.where(lane == 0, pos1, jnp.where(lane == 1, pos2,
                               jnp.where(lane == 2, w1, jnp.where(lane == 3, w2, 0.0))))
    run = run_ref[...] + jnp.sum(ind, axis=0, keepdims=True)
    run_ref[...] = run
    cnt_ref[...] = run


def _out_projection(x_lat, x_ctx, ctx_blk, oa, ob, oc, wa, wb, wc, modsel, gain, wrh, wrl, br, n_tiles,
                    n_lat_tiles):
    b, _, d = x_lat.shape
    tok = lambda bi, ti: (bi, ti, 0)
    const2 = lambda bi, ti: (0, 0)
    rows = n_tiles * TILE
    nb = b // MOE_GROUPS if b % MOE_GROUPS == 0 else b
    tri = jnp.asarray(np.tril(np.ones((TILE, TILE), np.float32), -1), BF16)
    return pl.pallas_call(
        functools.partial(_outproj_kernel, region=nb * rows, group_batches=nb, n_lat_tiles=n_lat_tiles),
        out_shape=[jax.ShapeDtypeStruct((b, rows, d), F32),
                   jax.ShapeDtypeStruct((b, rows, d // 2), jnp.int32),
                   jax.ShapeDtypeStruct((b, rows, LANES), F32),
                   jax.ShapeDtypeStruct((SUBLANES * (b // nb), LANES), F32)],
        grid=(b, n_tiles),
        in_specs=_token_specs(d, n_lat_tiles, ctx_blk) + [
                  pl.BlockSpec((None, TILE, W_A), tok),
                  pl.BlockSpec((None, TILE, W_B), tok),
                  pl.BlockSpec((None, TILE, W_C), tok),
                  pl.BlockSpec((W_A, d), const2),
                  pl.BlockSpec((W_B, d), const2),
                  pl.BlockSpec((W_C, d), const2),
                  pl.BlockSpec((None, 6, d), lambda bi, ti: (2 * bi + (ti >= n_lat_tiles).astype(jnp.int32), 0, 0)),
                  pl.BlockSpec((1, d), const2),
                  pl.BlockSpec((d, LANES), const2),
                  pl.BlockSpec((d, LANES), const2),
                  pl.BlockSpec((1, LANES), const2),
                  pl.BlockSpec((TILE, TILE), const2)],
        out_specs=[pl.BlockSpec((None, TILE, d), tok),
                   pl.BlockSpec((None, TILE, d // 2), tok),
                   pl.BlockSpec((None, TILE, LANES), tok),
                   pl.BlockSpec((SUBLANES, LANES), lambda bi, ti: (bi // nb, 0))],
        scratch_shapes=[pltpu.VMEM((SUBLANES, LANES), F32)],
        compiler_params=_cparams(2),
        name="out_projection",
    )(x_lat, x_ctx, oa, ob, oc, wa, wb, wc, modsel, gain, wrh, wrl, br, tri)


def _sc_mesh():
    return plsc.VectorSubcoreMesh(core_axis_name="core", subcore_axis_name="subcore")


def _sc_worker_base(per_worker):
    wid = lax.axis_index("subcore") * SC_CORES + lax.axis_index("core")
    return wid * per_worker


def _sc_scratch(d, dtype):
    return ([pltpu.VMEM((SC_ROWS,), jnp.int32)] * SC_BUFS + [pltpu.VMEM((SC_ROWS, d), dtype)] * SC_BUFS
            + [pltpu.SemaphoreType.DMA] * (2 * SC_BUFS))


def _sc_split(scratch):
    return (scratch[:SC_BUFS], scratch[SC_BUFS:2 * SC_BUFS], scratch[2 * SC_BUFS:3 * SC_BUFS],
            scratch[3 * SC_BUFS:])


def _sc_chunk_loop(per_worker, group):
    chunks = per_worker // SC_ROWS
    full = chunks // SC_BUFS * SC_BUFS

    @pl.loop(0, full, step=SC_BUFS)
    def _(c):
        group(c, SC_BUFS)

    if chunks > full:
        group(full, chunks - full)


def _sc_scatter_rows(x, row_off, n, idx, n_out):
    d = x.shape[1]
    per_worker = 2 * n // (SC_CORES * SC_SUBCORES)
    assert per_worker % SC_ROWS == 0 and n % SC_ROWS == 0

    @functools.partial(pl.kernel, out_type=jax.ShapeDtypeStruct((n_out, d), x.dtype),
                       mesh=_sc_mesh(), scratch_types=_sc_scratch(d, x.dtype))
    def scatter(x_hbm, i_hbm, o_hbm, *scratch):
        idx_v, rows_v, sem_in, sem_out = _sc_split(scratch)
        base = _sc_worker_base(per_worker)

        def group(c, n_bufs):
            reads = []
            for u in range(n_bufs):
                a = pl.multiple_of(base + (c + u) * SC_ROWS, SC_ROWS)
                t = pl.multiple_of(row_off + lax.rem(a, n), SC_ROWS)
                pltpu.sync_copy(i_hbm.at[pl.ds(a, SC_ROWS)], idx_v[u])
                reads.append(pltpu.async_copy(x_hbm.at[pl.ds(t, SC_ROWS)], rows_v[u], sem_in[u]))
            writes = []
            for u in range(n_bufs):
                reads[u].wait()
                writes.append(pltpu.async_copy(rows_v[u], o_hbm.at[idx_v[u]], sem_out[u]))
            for w in writes:
                w.wait()

        _sc_chunk_loop(per_worker, group)

    return scatter(x, idx)


def _sc_gather_rows(src, idx):
    m = idx.shape[0]
    d = src.shape[1]
    per_worker = m // (SC_CORES * SC_SUBCORES)
    assert per_worker % SC_ROWS == 0

    @functools.partial(pl.kernel, out_type=jax.ShapeDtypeStruct((m, d), src.dtype),
                       mesh=_sc_mesh(), scratch_types=_sc_scratch(d, src.dtype))
    def gather(s_hbm, i_hbm, o_hbm, *scratch):
        idx_v, rows_v, sem_in, sem_out = _sc_split(scratch)
        base = _sc_worker_base(per_worker)

        def group(c, n_bufs):
            offs, reads = [], []
            for u in range(n_bufs):
                a = pl.multiple_of(base + (c + u) * SC_ROWS, SC_ROWS)
                offs.append(a)
                pltpu.sync_copy(i_hbm.at[pl.ds(a, SC_ROWS)], idx_v[u])
                reads.append(pltpu.async_copy(s_hbm.at[idx_v[u]], rows_v[u], sem_in[u]))
            writes = []
            for u in range(n_bufs):
                reads[u].wait()
                writes.append(pltpu.async_copy(rows_v[u], o_hbm.at[pl.ds(offs[u], SC_ROWS)], sem_out[u]))
            for w in writes:
                w.wait()

        _sc_chunk_loop(per_worker, group)

    return gather(src, idx)


def _expert_ffn_kernel(blk_ref, exp_ref, x_ref, wg_ref, wu_ref, wd_ref, y_ref, wgb_ref, wub_ref, wdb_ref):
    j = pl.program_id(0)

    @pl.when((j == 0) | (exp_ref[j] != exp_ref[jnp.maximum(j - 1, 0)]))
    def _():
        wgb_ref[...] = wg_ref[...].astype(BF16)
        wub_ref[...] = wu_ref[...].astype(BF16)
        wdb_ref[...] = wd_ref[...].astype(BF16)

    x = _unpack_bf16_pairs(x_ref[...]).astype(BF16)
    hid = jax.nn.silu(_dot(x, wgb_ref[...])) * _dot(x, wub_ref[...])
    y_ref[...] = _pack_bf16_pairs(_dot(hid.astype(BF16), wdb_ref[...]))


def _expert_ffn(xs, blk, exp, wg, wu, wd, layer):
    rows, d_packed = xs.shape
    d = 2 * d_packed
    w_map = lambda j, blk, exp: (layer, exp[j], 0, 0)
    return pl.pallas_call(
        _expert_ffn_kernel,
        out_shape=jax.ShapeDtypeStruct((rows, d_packed), jnp.int32),
        grid_spec=pltpu.PrefetchScalarGridSpec(
            num_scalar_prefetch=2,
            grid=(blk.shape[0],),
            in_specs=[pl.BlockSpec((MOE_TILE, d_packed), lambda j, blk, exp: (blk[j], 0)),
                      pl.BlockSpec((None, None, d, EXPERT_HIDDEN), w_map),
                      pl.BlockSpec((None, None, d, EXPERT_HIDDEN), w_map),
                      pl.BlockSpec((None, None, EXPERT_HIDDEN, d), w_map)],
            out_specs=pl.BlockSpec((MOE_TILE, d_packed), lambda j, blk, exp: (blk[j], 0)),
            scratch_shapes=[pltpu.VMEM((d, EXPERT_HIDDEN), BF16),
                            pltpu.VMEM((d, EXPERT_HIDDEN), BF16),
                            pltpu.VMEM((EXPERT_HIDDEN, d), BF16)]),
        compiler_params=_cparams(1),
        name="expert_ffn",
    )(blk, exp, xs, wg, wu, wd)


def _combine_kernel(x1_ref, y1_ref, y2_ref, route_ref, mod_ref, fgain_ref, *rest, final):
    o_ref = rest[-1]
    route = route_ref[...]
    y = route[:, 2:3] * _unpack_bf16_pairs(y1_ref[...]) + route[:, 3:4] * _unpack_bf16_pairs(y2_ref[...])
    x2 = x1_ref[...] + mod_ref[5:6, :] * y
    if final:
        ms = jnp.mean(x2 * x2, axis=-1, keepdims=True)
        x2 = (x2 * lax.rsqrt(ms + EPS)) * fgain_ref[...]
    o_ref[...] = x2


def _combine(x1, ys, route, modsel, fgain, prev, b0, nb, n_lat_tiles, final):
    b, rows, d = x1.shape
    n_t = rows // TILE
    tok = lambda bi, ti: (b0 + bi, ti, 0)
    in_specs = [pl.BlockSpec((None, TILE, d), tok),
                pl.BlockSpec((TILE, d // 2), lambda bi, ti: (bi * n_t + ti, 0)),
                pl.BlockSpec((TILE, d // 2), lambda bi, ti: ((nb + bi) * n_t + ti, 0)),
                pl.BlockSpec((None, TILE, LANES), tok),
                pl.BlockSpec((None, 6, d),
                             lambda bi, ti: (2 * (b0 + bi) + (ti >= n_lat_tiles).astype(jnp.int32), 0, 0)),
                pl.BlockSpec((1, d), lambda bi, ti: (0, 0))]
    args = [x1, ys, ys, route, modsel, fgain]
    aliases = {}
    if prev is not None:
        in_specs.append(pl.BlockSpec(memory_space=pl.ANY))
        args.append(prev)
        aliases = {len(args) - 1: 0}
    return pl.pallas_call(
        functools.partial(_combine_kernel, final=final),
        out_shape=jax.ShapeDtypeStruct((b, rows, d), F32),
        grid=(nb, n_t),
        in_specs=in_specs,
        out_specs=pl.BlockSpec((None, TILE, d), tok),
        input_output_aliases=aliases,
        compiler_params=_cparams(2),
        name="moe_combine",
    )(*args)


def _routed_moe(tok, route, cnt, x1, wg, wu, wd, layer, modsel, fgain, n_lat_tiles, final):
    b, rows, d = x1.shape
    n_groups = cnt.shape[0] // SUBLANES
    nb = b // n_groups
    n = nb * rows
    flat = route.reshape(b * rows, LANES)
    tok_flat = tok.reshape(b * rows, tok.shape[2])
    out = None
    for g in range(n_groups):
        part = flat[g * n:(g + 1) * n]
        idx = jnp.concatenate([part[:, 0], part[:, 1]]).astype(jnp.int32)
        xs = _sc_scatter_rows(tok_flat, g * n, n, idx, N_EXPERTS * n)

        counts = cnt[SUBLANES * g, N_GROUPS:N_GROUPS + N_EXPERTS].astype(jnp.int32)
        tiles = (counts + MOE_TILE - 1) // MOE_TILE
        ends = jnp.cumsum(tiles)
        n_sched = 2 * n // MOE_TILE + N_EXPERTS
        j = jnp.minimum(jnp.arange(n_sched, dtype=jnp.int32), ends[-1] - 1)
        exp = jnp.sum((j[:, None] >= ends[None, :]).astype(jnp.int32), axis=1)
        blk = exp * (n // MOE_TILE) + j - (ends - tiles)[exp]

        ys = _expert_ffn(xs, blk, exp, wg, wu, wd, layer)
        yg = _sc_gather_rows(ys, idx)
        out = _combine(x1, yg, route, modsel, fgain, out, g * nb, nb, n_lat_tiles, final)
    return out


def _rope_tables(n_lat):
    t = jnp.arange(n_lat)
    row = (t // GRID_W).astype(F32)
    col = (t % GRID_W).astype(F32)

    def cs(dim):
        quarter = dim // 4
        freqs = ROPE_THETA ** (-jnp.arange(quarter, dtype=F32) / quarter)
        ang = jnp.concatenate([row[:, None] * freqs, col[:, None] * freqs], axis=-1)
        cos = jnp.tile(jnp.cos(ang), (1, 2 * LANES // dim))
        sin = jnp.tile(jnp.sin(ang), (1, 2 * LANES // dim))
        cos = jnp.concatenate([cos, jnp.ones((CTX_LEN, LANES), F32)], axis=0)
        sin = jnp.concatenate([sin, jnp.zeros((CTX_LEN, LANES), F32)], axis=0)
        return cos, sin

    cos_b, sin_b = cs(DIFF_QK_DIM)
    cos_c, sin_c = cs(HEAD_DIM)
    return jnp.concatenate([cos_b, sin_b, cos_c, sin_c], axis=1)


def _reordered_w_in(w_in):
    o_c = 3 * W_A + 3 * W_B
    heads = [w_in[:, o_c + h * HEAD_DIM:o_c + (h + 1) * HEAD_DIM] for h in GQA_Q_ORDER]
    return jnp.concatenate([w_in[:, :o_c]] + heads + [w_in[:, o_c + W_C:]], axis=1).astype(BF16)


def kernel(x, c, ctx, c_ctx, w_mod, b_mod, norm_attn, norm_ffn, w_in, w_out, na_rpb, diff_lambda_q1, diff_lambda_k1, diff_lambda_q2, diff_lambda_k2, diff_subln, gqa_q_norm, gqa_k_norm, router_group_w, router_group_b, router_expert_w, router_expert_b, w_gate, w_up, w_down, final_norm):
    b, s, d = x.shape
    assert d == D_MODEL and ctx.shape[1] == CTX_LEN and s % (NA_QROWS * GRID_W) == 0
    rows = s // GRID_W
    assert rows >= 2 * NA_QROWS
    t_all = s + CTX_LEN
    n_lat_tiles = s // TILE

    assert b + 1 <= SUBLANES
    c_rows = jnp.zeros((SUBLANES, d), F32).at[:b].set(c).at[b].set(c_ctx)
    mod = _modulation(c_rows, w_mod, b_mod)

    tab = _rope_tables(s)
    hidx = np.arange(HEAD_DIM)
    partner = np.where(hidx < HEAD_DIM // 2, hidx + HEAD_DIM // 2, hidx - HEAD_DIM // 2)
    blk = np.arange(W_C) // HEAD_DIM
    ones = jnp.asarray((blk[:, None] == blk[None, :]).astype(np.float32), BF16)
    dummy_aux = jnp.zeros((SUBLANES, LANES), F32)

    x_lat, x_ctx, ctx_blk = x, ctx, 0
    for l in range(DEPTH):
        ctx_out = l < DEPTH - 1
        lam_init = 0.8 - 0.6 * math.exp(-0.3 * l)
        m_lat = mod[l, :b].reshape(b, 1, 6, d)
        m_ctx = jnp.broadcast_to(mod[l, b].reshape(1, 1, 6, d), (b, 1, 6, d))
        modsel = jnp.concatenate([m_lat, m_ctx], axis=1).reshape(2 * b, 6, d)

        gq = jnp.stack([jnp.tile(gqa_q_norm[l], GQA_Q_HEADS), jnp.tile(gqa_q_norm[l][partner], GQA_Q_HEADS)])
        gk = jnp.stack([jnp.tile(gqa_k_norm[l], GQA_KV_HEADS), jnp.tile(gqa_k_norm[l][partner], GQA_KV_HEADS)])
        qa, ka, va, qb, kb, vb, qc, kc, vc = _in_projection(
            x_lat, x_ctx, ctx_blk, modsel, norm_attn[l][None], _reordered_w_in(w_in[l]), tab, gq, gk, ones,
            n_lat_tiles)

        n_qt = n_lat_tiles + 1 if ctx_out else n_lat_tiles
        pad = lambda v: jnp.pad(v, (0, LANES - v.shape[0]))
        aux = jnp.stack([pad(diff_lambda_q1[l]), pad(diff_lambda_k1[l]), pad(diff_lambda_q2[l]),
                         pad(diff_lambda_k2[l]), jnp.tile(diff_subln[l], 2),
                         jnp.zeros((LANES,), F32), jnp.zeros((LANES,), F32), jnp.zeros((LANES,), F32)])
        group_a = dict(n_qblk=1, n_sub=2, n_hp=NA_HEADS // 2, n_lat=s, mode="plain")
        group_b = dict(n_qblk=1, n_sub=4, n_hp=DIFF_HEADS // 2, n_lat=s, mode="diff", lam_init=lam_init)
        group_c = dict(n_qblk=3, n_sub=2, n_hp=1, n_lat=s, mode="plain")
        oa = _neighbourhood_attention(qa, ka, va, _na_bias_table(na_rpb[l], rows), s)
        ob = _flash(qb, kb, vb, aux, queries="latent", **group_b)
        oc = _flash(qc, kc, vc, dummy_aux, queries="latent", **group_c)
        if ctx_out:
            oa = jnp.concatenate([oa, _flash(qa, ka, va, dummy_aux, queries="context", **group_a)], axis=1)
            ob = jnp.concatenate([ob, _flash(qb, kb, vb, aux, queries="context", **group_b)], axis=1)
            oc = jnp.concatenate([oc, _flash(qc, kc, vc, dummy_aux, queries="context", **group_c)], axis=1)

        w_o = w_out[l]
        o_c = W_A + W_B
        w_oc = jnp.concatenate([w_o[o_c + h * HEAD_DIM:o_c + (h + 1) * HEAD_DIM] for h in GQA_Q_ORDER], axis=0)
        wr = jnp.zeros((d, LANES), F32)
        wr = wr.at[:, :N_GROUPS].set(router_group_w[l]).at[:, N_GROUPS:N_GROUPS + N_EXPERTS].set(router_expert_w[l])
        wrh, wrl = _split_bf16(wr)
        br = jnp.zeros((1, LANES), F32)
        br = br.at[0, :N_GROUPS].set(router_group_b[l]).at[0, N_GROUPS:N_GROUPS + N_EXPERTS].set(router_expert_b[l])
        x1, tok, route, cnt = _out_projection(
            x_lat, x_ctx, ctx_blk, oa, ob, oc, w_o[:W_A].astype(BF16), w_o[W_A:W_A + W_B].astype(BF16),
            w_oc.astype(BF16), modsel, norm_ffn[l][None], wrh, wrl, br, n_qt, n_lat_tiles)
        xs = _routed_moe(tok, route, cnt, x1, w_gate, w_up, w_down, l, modsel, final_norm[None],
                         n_lat_tiles, final=not ctx_out)
        x_lat, x_ctx, ctx_blk = xs, xs, n_lat_tiles
    return xs
```

```python
import functools
import math

import numpy as np
import jax
import jax.numpy as jnp
from jax import lax
from jax.experimental import pallas as pl
from jax.experimental.pallas import tpu as pltpu
from jax.experimental.pallas import tpu_sc as plsc

F32 = jnp.float32
BF16 = jnp.bfloat16

D_MODEL = 1024
DEPTH = 2
GRID_W = 64
CTX_LEN = 256
HEAD_DIM = 64
NA_HEADS = 6
NA_WIN_H = 8
NA_WIN_W = 16
DIFF_HEADS = 4
DIFF_QK_DIM = 32
GQA_Q_HEADS = 6
GQA_KV_HEADS = 2
N_GROUPS = 4
EXPERTS_PER_GROUP = 4
N_EXPERTS = 16
EXPERT_HIDDEN = 512
ROPE_THETA = 10000.0
EPS = 1e-6
W_A = NA_HEADS * HEAD_DIM
W_B = DIFF_HEADS * 2 * DIFF_QK_DIM
W_C = GQA_Q_HEADS * HEAD_DIM
W_KC = GQA_KV_HEADS * HEAD_DIM
IN_WIDTH = 3 * W_A + 3 * W_B + W_C + 2 * W_KC

LANES = 128
TILE = CTX_LEN
NA_QROWS = 8
NA_KROWS = 16
NA_PARTS = 2
NEG = -1e30
LOG2E = 1.4426950408889634
HI16 = -65536
VMEM_LIMIT = 56 * 1024 * 1024
FLASH_TK = 512
PAIRS_PER_STEP = 2
FLASH_UNROLL_MAX_ROWS = 1536
SUBLANES = 8
MOE_TILE = 512
MOE_GROUPS = 2
SC_ROWS = 32
SC_BUFS = 4
SC_CORES = 2
SC_SUBCORES = 16

GQA_Q_ORDER = (0, 3, 1, 4, 2, 5)


def _cparams(n_axes):
    return pltpu.CompilerParams(dimension_semantics=("arbitrary",) * n_axes,
                                vmem_limit_bytes=VMEM_LIMIT)


def _split_bf16(a):
    hi = a.astype(BF16)
    lo = (a - hi.astype(F32)).astype(BF16)
    return hi, lo


def _dot(a, b):
    return jnp.dot(a, b, preferred_element_type=F32)


def _pack_bf16_pairs(t):
    bits = lax.bitcast_convert_type(t.astype(BF16).astype(F32), jnp.int32)
    half_d = bits.shape[1] // 2
    return lax.shift_right_logical(bits[:, :half_d], 16) | (bits[:, half_d:] & HI16)


def _unpack_bf16_pairs(w):
    return jnp.concatenate([lax.bitcast_convert_type(lax.shift_left(w, 16), F32),
                            lax.bitcast_convert_type(w & HI16, F32)], axis=1)


def _dot_nt(a, b):
    return lax.dot_general(a, b, (((1,), (1,)), ((), ())), preferred_element_type=F32)


def _mod_kernel(c_ref, w_ref, b_ref, o_ref):
    c = c_ref[...]
    a = c * jax.nn.sigmoid(c)
    a_hi, a_lo = _split_bf16(a)
    w_hi, w_lo = _split_bf16(w_ref[...])
    o_ref[...] = _dot(a_hi, w_hi) + _dot(a_lo, w_hi) + _dot(a_hi, w_lo) + b_ref[...]


def _modulation(c_rows, w_mod, b_mod):
    depth, d, n = w_mod.shape
    bn = 1536
    return pl.pallas_call(
        _mod_kernel,
        out_shape=jax.ShapeDtypeStruct((depth, SUBLANES, n), F32),
        grid=(depth, n // bn),
        in_specs=[pl.BlockSpec((SUBLANES, d), lambda l, j: (0, 0)),
                  pl.BlockSpec((None, d, bn), lambda l, j: (l, 0, j)),
                  pl.BlockSpec((None, 1, bn), lambda l, j: (l, 0, j))],
        out_specs=pl.BlockSpec((None, SUBLANES, bn), lambda l, j: (l, 0, j)),
        compiler_params=_cparams(2),
        name="adaln_mod",
    )(c_rows, w_mod, b_mod.reshape(depth, 1, n))


def _head_mean_sq(t, ones):
    hi, lo = _split_bf16(t * t)
    return (_dot(hi, ones) + _dot(lo, ones)) * (1.0 / HEAD_DIM)


def _rotate_half(p, head):
    w = p.shape[1]
    half = head // 2
    lane = lax.broadcasted_iota(jnp.int32, (1, w), 1)
    first = (lane & (head - 1)) < half
    from_right = pltpu.roll(p, w - half, 1)
    from_left = pltpu.roll(p, half, 1)
    return jnp.where(first, -from_right, from_left)


def _inproj_kernel(x_ref, xc_ref, mod_ref, gain_ref, w_ref, tab_ref, gq_ref, gk_ref, ones_ref,
                   qa_ref, ka_ref, va_ref, qb_ref, kb_ref, vb_ref, qc_ref, kc_ref, vc_ref,
                   *, n_lat_tiles):
    x = jnp.where(pl.program_id(1) == n_lat_tiles, xc_ref[...], x_ref[...])
    mod = mod_ref[...]
    ms = jnp.mean(x * x, axis=-1, keepdims=True)
    h = (x * lax.rsqrt(ms + EPS)) * gain_ref[...]
    h = h * (1.0 + mod[1:2]) + mod[0:1]
    hb = h.astype(BF16)

    def proj(a, b):
        return _dot(hb, w_ref[:, a:b])

    pa = proj(0, 3 * W_A)
    qa_ref[...] = (pa[:, :W_A] * (HEAD_DIM ** -0.5 * LOG2E)).astype(BF16)
    ka_ref[...] = pa[:, W_A:2 * W_A].astype(BF16)
    va_ref[...] = pa[:, 2 * W_A:].astype(BF16)

    tab = tab_ref[...]
    cos_b = jnp.concatenate([tab[:, 0:LANES]] * 2, axis=1)
    sin_b = jnp.concatenate([tab[:, LANES:2 * LANES]] * 2, axis=1)
    cos_c1 = tab[:, 2 * LANES:3 * LANES]
    sin_c1 = tab[:, 3 * LANES:4 * LANES]
    cos_c = jnp.concatenate([cos_c1] * 3, axis=1)
    sin_c = jnp.concatenate([sin_c1] * 3, axis=1)

    o_b = 3 * W_A
    pb = proj(o_b, o_b + 3 * W_B)
    qb = pb[:, :W_B]
    kb = pb[:, W_B:2 * W_B]
    qb = qb * cos_b + _rotate_half(qb, DIFF_QK_DIM) * sin_b
    qb_ref[...] = (qb * (DIFF_QK_DIM ** -0.5 * LOG2E)).astype(BF16)
    kb_ref[...] = (kb * cos_b + _rotate_half(kb, DIFF_QK_DIM) * sin_b).astype(BF16)
    vb_ref[...] = pb[:, 2 * W_B:].astype(BF16)

    o_c = o_b + 3 * W_B
    pc = proj(o_c, IN_WIDTH)
    ones = ones_ref[...]
    qc = pc[:, :W_C]
    kc = pc[:, W_C:W_C + W_KC]
    nq = lax.rsqrt(_head_mean_sq(qc, ones) + EPS)
    nk = lax.rsqrt(_head_mean_sq(kc, ones[:W_KC, :W_KC]) + EPS)
    gq = gq_ref[...]
    gk = gk_ref[...]
    q = nq * (qc * gq[0:1] * cos_c + _rotate_half(qc, HEAD_DIM) * gq[1:2] * sin_c)
    qc_ref[...] = (q * (HEAD_DIM ** -0.5 * LOG2E)).astype(BF16)
    k = nk * (kc * gk[0:1] * cos_c1 + _rotate_half(kc, HEAD_DIM) * gk[1:2] * sin_c1)
    kc_ref[...] = k.astype(BF16)
    vc_ref[...] = pc[:, W_C + W_KC:].astype(BF16)


def _token_specs(d, n_lat_tiles, ctx_blk):
    return [pl.BlockSpec((None, TILE, d), lambda bi, ti: (bi, jnp.minimum(ti, n_lat_tiles - 1), 0)),
            pl.BlockSpec((None, TILE, d), lambda bi, ti: (bi, ctx_blk, 0))]


def _in_projection(x_lat, x_ctx, ctx_blk, modsel, gain, w_ext, tab, gq, gk, ones, n_lat_tiles):
    b, _, d = x_lat.shape
    n_tiles = n_lat_tiles + 1
    t_all = n_tiles * TILE
    widths = (W_A, W_A, W_A, W_B, W_B, W_B, W_C, W_KC, W_KC)
    tok = lambda bi, ti: (bi, ti, 0)
    const2 = lambda bi, ti: (0, 0)
    return pl.pallas_call(
        functools.partial(_inproj_kernel, n_lat_tiles=n_lat_tiles),
        out_shape=[jax.ShapeDtypeStruct((b, t_all, w), BF16) for w in widths],
        grid=(b, n_tiles),
        in_specs=_token_specs(d, n_lat_tiles, ctx_blk) + [
                  pl.BlockSpec((None, 6, d), lambda bi, ti: (2 * bi + (ti >= n_lat_tiles).astype(jnp.int32), 0, 0)),
                  pl.BlockSpec((1, d), const2),
                  pl.BlockSpec((d, IN_WIDTH), const2),
                  pl.BlockSpec((TILE, 4 * LANES), lambda bi, ti: (ti, 0)),
                  pl.BlockSpec((2, W_C), const2),
                  pl.BlockSpec((2, W_KC), const2),
                  pl.BlockSpec((W_C, W_C), const2)],
        out_specs=[pl.BlockSpec((None, TILE, w), tok) for w in widths],
        compiler_params=_cparams(2),
        name="in_projection",
    )(x_lat, x_ctx, modsel, gain, w_ext, tab, gq, gk, ones)


def _flash_kernel(q_ref, k_ref, v_ref, aux_ref, o_ref, va_ref, vb_ref, qs_ref, acc_ref, m_ref,
                  s0_ref, s1_ref, mb0_ref, mb1_ref, *,
                  n_qblk, n_sub, tk, n_lat_blocks, pairs_per_step, ctx_start, queries, mode, lam_init):
    sub_w = LANES // n_sub
    half = LANES // 2
    lane = lax.broadcasted_iota(jnp.int32, (1, LANES), 1)
    lower = lane < half
    n_pieces = n_qblk * n_sub
    ma = (n_pieces // 2) * TILE
    m_rows = n_pieces * TILE

    @pl.when(pl.program_id(2) == 0)
    def _():
        v = v_ref[...].astype(F32)
        va_ref[...] = jnp.where(lower, v, 1.0).astype(BF16)
        vb_ref[...] = jnp.where(lower, 1.0, v).astype(BF16)

    ia, ib = 0, n_pieces // 2
    for blk in range(n_qblk):
        qf = q_ref[:, blk * LANES:(blk + 1) * LANES].astype(F32)
        for sub in range(n_sub):
            msk = (lane >= sub * sub_w) & (lane < (sub + 1) * sub_w)
            piece = jnp.where(msk, qf, 0.0).astype(BF16)
            if sub * sub_w < half:
                qs_ref[ia * TILE:(ia + 1) * TILE, :] = piece
                ia += 1
            else:
                qs_ref[ib * TILE:(ib + 1) * TILE, :] = piece
                ib += 1

    s_bufs = (s0_ref, s1_ref)
    mb_bufs = (mb0_ref, mb1_ref)

    def scores(start, size, slot):
        s = _dot_nt(qs_ref[...], k_ref[pl.ds(start, size), :])
        s_bufs[slot][:, :size] = s
        mb = jnp.max(s, axis=-1, keepdims=True)
        mb_bufs[slot][...] = jnp.broadcast_to(mb, (m_rows, LANES))

    def accumulate(start, size, slot, first):
        mb = mb_bufs[slot][...]
        if first:
            m_new = mb
        else:
            m_old = m_ref[...]
            m_new = jnp.maximum(m_old, mb)
        s_ref = s_bufs[slot]
        cols = [s_ref[:, c * LANES:(c + 1) * LANES] - m_new for c in range(size // LANES)]
        p = jnp.concatenate([jnp.exp2(d.astype(BF16)) for d in cols], axis=1)
        pva = _dot(p[:ma], va_ref[pl.ds(start, size), :])
        pvb = _dot(p[ma:], vb_ref[pl.ds(start, size), :])
        if first:
            acc_ref[:ma, :] = pva
            acc_ref[ma:, :] = pvb
        else:
            alpha = jnp.exp2(m_old - m_new)
            acc_ref[:ma, :] = alpha[:ma] * acc_ref[:ma, :] + pva
            acc_ref[ma:, :] = alpha[ma:] * acc_ref[ma:, :] + pvb
        m_ref[...] = m_new

    def lat(j):
        return pl.multiple_of(j * tk, tk)

    def latent_queries():
        scores(ctx_start, CTX_LEN, 0)
        scores(lat(0), tk, 1)
        accumulate(ctx_start, CTX_LEN, 0, True)

        def pair(i):
            scores(lat(2 * i + 1), tk, 0)
            accumulate(lat(2 * i), tk, 1, False)
            scores(lat(2 * i + 2), tk, 1)
            accumulate(lat(2 * i + 1), tk, 0, False)

        def body(i, carry):
            for u in range(pairs_per_step):
                pair(i * pairs_per_step + u)
            return carry

        n_pairs = (n_lat_blocks - 2) // 2
        n_steps = n_pairs // pairs_per_step
        lax.fori_loop(0, n_steps, body, 0)
        for i in range(n_steps * pairs_per_step, n_pairs):
            pair(i)
        scores(lat(n_lat_blocks - 1), tk, 0)
        accumulate(lat(n_lat_blocks - 2), tk, 1, False)
        accumulate(lat(n_lat_blocks - 1), tk, 0, False)

    def context_queries():
        scores(ctx_start, CTX_LEN, 0)
        accumulate(ctx_start, CTX_LEN, 0, True)

    if queries == "latent":
        latent_queries()
    else:
        context_queries()

    acc = acc_ref[...]
    r = acc / pltpu.roll(acc, half, 1)
    ra, rb = r[:ma], r[ma:]
    if mode == "plain":
        for i in range(n_pieces // 2):
            o = jnp.where(lower, ra[i * TILE:(i + 1) * TILE], rb[i * TILE:(i + 1) * TILE])
            o_ref[:, i * LANES:(i + 1) * LANES] = o.astype(BF16)
    else:
        aux = aux_ref[...]
        l1 = jnp.sum(aux[0:1] * aux[1:2], axis=-1, keepdims=True)
        l2 = jnp.sum(aux[2:3] * aux[3:4], axis=-1, keepdims=True)
        lam = jnp.exp(l1) - jnp.exp(l2) + lam_init
        oa = ra[:TILE] - lam * ra[TILE:]
        ob = rb[:TILE] - lam * rb[TILE:]
        o = jnp.where(lower, oa, ob)
        sq = o * o
        ss_a = jnp.sum(jnp.where(lower, sq, 0.0), axis=-1, keepdims=True)
        ss_b = jnp.sum(jnp.where(lower, 0.0, sq), axis=-1, keepdims=True)
        ms = jnp.where(lower, ss_a, ss_b) * (1.0 / HEAD_DIM)
        o = (o * lax.rsqrt(ms + EPS)) * aux[4:5]
        o_ref[...] = (o * (1.0 - lam_init)).astype(BF16)


def _flash(q, k, v, aux, *, n_qblk, n_sub, n_hp, n_lat, queries, mode, lam_init=0.0):
    b, t_all, _ = q.shape
    qw = n_qblk * LANES
    tk = FLASH_TK
    assert n_lat % (2 * tk) == 0 and tk >= CTX_LEN
    m_rows = n_qblk * n_sub * TILE
    n_pairs = max((n_lat // tk - 2) // 2, 1)
    pairs = n_pairs if m_rows <= FLASH_UNROLL_MAX_ROWS else PAIRS_PER_STEP
    if queries == "latent":
        n_qt, qt_off, kv_rows, kv_blk, ctx_start = n_lat // TILE, 0, t_all, 0, n_lat
    else:
        n_qt, qt_off, kv_rows, kv_blk, ctx_start = 1, n_lat // TILE, CTX_LEN, n_lat // CTX_LEN, 0
    kern = functools.partial(_flash_kernel, n_qblk=n_qblk, n_sub=n_sub, tk=tk,
                             n_lat_blocks=n_lat // tk, ctx_start=ctx_start, queries=queries,
                             pairs_per_step=pairs, mode=mode, lam_init=lam_init)
    return pl.pallas_call(
        kern,
        out_shape=jax.ShapeDtypeStruct((b, n_qt * TILE, n_hp * qw), BF16),
        grid=(b, n_hp, n_qt),
        in_specs=[pl.BlockSpec((None, TILE, qw), lambda bi, hp, qt: (bi, qt + qt_off, hp)),
                  pl.BlockSpec((None, kv_rows, LANES), lambda bi, hp, qt: (bi, kv_blk, hp)),
                  pl.BlockSpec((None, kv_rows, LANES), lambda bi, hp, qt: (bi, kv_blk, hp)),
                  pl.BlockSpec((SUBLANES, LANES), lambda bi, hp, qt: (0, 0))],
        out_specs=pl.BlockSpec((None, TILE, qw), lambda bi, hp, qt: (bi, qt, hp)),
        scratch_shapes=[pltpu.VMEM((kv_rows, LANES), BF16),
                        pltpu.VMEM((kv_rows, LANES), BF16),
                        pltpu.VMEM((m_rows, LANES), BF16),
                        pltpu.VMEM((m_rows, LANES), F32),
                        pltpu.VMEM((m_rows, LANES), F32),
                        pltpu.VMEM((m_rows, tk), F32),
                        pltpu.VMEM((m_rows, tk), F32),
                        pltpu.VMEM((m_rows, LANES), F32),
                        pltpu.VMEM((m_rows, LANES), F32)],
        compiler_params=_cparams(3),
        name="flash_" + mode,
    )(q, k, v, aux)


def _na_kernel(q_ref, k0, k1, k2, k3, v0, v1, v2, v3, kc_ref, vc_ref, bias_ref, o_ref, s_ref, m_ref):
    lane = lax.broadcasted_iota(jnp.int32, (1, LANES), 1)
    lower = lane < LANES // 2
    n_pair = NA_KROWS // 2
    rows_per_part = NA_QROWS // NA_PARTS
    half_q = rows_per_part * GRID_W
    no_bias = jnp.zeros((GRID_W, CTX_LEN), F32)

    def head_pair(hp):
        cols = slice(hp * LANES, (hp + 1) * LANES)
        qf = q_ref[:, cols].astype(F32)
        k_all = jnp.concatenate([r[:, cols] for r in (k0, k1, k2, k3, kc_ref)], axis=0)
        v_all = jnp.concatenate([r[:, cols] for r in (v0, v1, v2, v3, vc_ref)], axis=0).astype(F32)
        v_h = [jnp.where(lower, v_all, 1.0).astype(BF16), jnp.where(lower, 1.0, v_all).astype(BF16)]
        q_h = [jnp.where(lower, qf, 0.0).astype(BF16), jnp.where(lower, 0.0, qf).astype(BF16)]
        return q_h, k_all, v_h

    pairs = [head_pair(hp) for hp in range(NA_HEADS // 2)]
    items = [(hp, part) for hp in range(NA_HEADS // 2) for part in range(NA_PARTS)]

    def scores(n):
        hp, part = items[n]
        q_h, k_all, _ = pairs[hp]
        rows = slice(part * half_q, (part + 1) * half_q)
        qs = jnp.concatenate([q_h[0][rows], q_h[1][rows]], axis=0)
        bias = jnp.concatenate(
            [jnp.concatenate([bias_ref[2 * hp + hh, a * n_pair + j] for j in range(n_pair)] + [no_bias],
                             axis=1)
             for hh in range(2) for a in range(part * rows_per_part, (part + 1) * rows_per_part)],
            axis=0)
        s = _dot_nt(qs, k_all) + bias
        s_ref[n % 2] = s
        m_ref[n % 2] = jnp.broadcast_to(jnp.max(s, axis=-1, keepdims=True), (2 * half_q, LANES))

    def finish(n):
        hp, part = items[n]
        v_h = pairs[hp][2]
        s = s_ref[n % 2]
        m = m_ref[n % 2]
        p = jnp.concatenate([jnp.exp2((s[:, c * LANES:(c + 1) * LANES] - m).astype(BF16))
                             for c in range(s.shape[1] // LANES)], axis=1)
        o0 = _dot(p[:half_q], v_h[0])
        o1 = _dot(p[half_q:], v_h[1])
        o0 = o0 / pltpu.roll(o0, LANES // 2, 1)
        o1 = o1 / pltpu.roll(o1, LANES // 2, 1)
        o_ref[part * half_q:(part + 1) * half_q, hp * LANES:(hp + 1) * LANES] = (
            jnp.where(lower, o0, o1).astype(BF16))

    scores(0)
    for n in range(1, len(items)):
        scores(n)
        finish(n - 1)
    finish(len(items) - 1)


def _neighbourhood_attention(qa, ka, va, bias, n_lat):
    b = qa.shape[0]
    q_tok = NA_QROWS * GRID_W
    v_tok = q_tok // 2
    n_rb = n_lat // q_tok
    n_view = n_lat // v_tok
    ctx_blk = n_lat // v_tok

    def view(j):
        return lambda rb, bi: (bi, jnp.clip(2 * rb - 1 + j, 0, n_view - 1), 0)

    kv_specs = [pl.BlockSpec((None, v_tok, W_A), view(j)) for j in range(4)]
    ctx_spec = pl.BlockSpec((None, CTX_LEN, W_A), lambda rb, bi: (bi, ctx_blk, 0))

    def bias_map(rb, bi):
        pat = jnp.where(rb == 0, 0, jnp.where(rb == n_rb - 1, 2, 1))
        return (0, pat, 0, 0, 0)

    part_rows = 2 * q_tok // NA_PARTS
    return pl.pallas_call(
        _na_kernel,
        out_shape=jax.ShapeDtypeStruct((b, n_lat, W_A), BF16),
        grid=(n_rb, b),
        in_specs=[pl.BlockSpec((None, q_tok, W_A), lambda rb, bi: (bi, rb, 0))]
                 + kv_specs + kv_specs + [ctx_spec, ctx_spec,
                 pl.BlockSpec((NA_HEADS, None, NA_QROWS * NA_KROWS // 2, GRID_W, 2 * GRID_W), bias_map)],
        out_specs=pl.BlockSpec((None, q_tok, W_A), lambda rb, bi: (bi, rb, 0)),
        scratch_shapes=[pltpu.VMEM((2, part_rows, NA_KROWS * GRID_W + CTX_LEN), F32),
                        pltpu.VMEM((2, part_rows, LANES), F32)],
        compiler_params=_cparams(2),
        name="neighbourhood_attention",
    )(qa, ka, ka, ka, ka, va, va, va, va, ka, va, bias)


def _na_bias_table(rpb, rows):
    cols = np.arange(GRID_W)
    c0 = np.clip(cols - NA_WIN_W // 2, 0, GRID_W - NA_WIN_W)
    cc = cols[None, :]
    col_ok = (cc >= c0[:, None]) & (cc < c0[:, None] + NA_WIN_W)
    dc = np.clip(cc - cols[:, None] + (NA_WIN_W - 1), 0, 2 * NA_WIN_W - 2)
    e = jnp.where(col_ok[None, None], (rpb.astype(F32) * LOG2E)[:, :, dc], NEG)
    e = jnp.concatenate([e, jnp.full_like(e[:, :1], NEG)], axis=1)
    a = np.arange(NA_QROWS)[:, None]
    i = np.arange(NA_KROWS)[None, :]
    pats = []
    for r_base in (0, NA_QROWS, rows - NA_QROWS):
        r = r_base + a
        key_row = r_base - NA_WIN_H // 2 + i
        r0 = np.clip(r - NA_WIN_H // 2, 0, rows - NA_WIN_H)
        ok = (key_row >= r0) & (key_row < r0 + NA_WIN_H) & (key_row >= 0) & (key_row < rows)
        dr = np.where(ok, key_row - r + (NA_WIN_H - 1), 2 * NA_WIN_H - 1)
        pats.append(dr)
    dr_all = np.stack(pats)
    pairs = dr_all.reshape(-1, 2)
    uniq, inv = np.unique(pairs, axis=0, return_inverse=True)
    pair_blocks = jnp.concatenate([e[:, uniq[:, 0]], e[:, uniq[:, 1]]], axis=-1)
    t = pair_blocks[:, inv.reshape(-1)]
    return t.reshape(NA_HEADS, 3, NA_QROWS * NA_KROWS // 2, GRID_W, 2 * GRID_W)


def _outproj_kernel(x_ref, xc_ref, oa_ref, ob_ref, oc_ref, wa_ref, wb_ref, wc_ref, mod_ref, gain_ref,
                    wrh_ref, wrl_ref, br_ref, tri_ref, x1_ref, tok_ref, route_ref, cnt_ref, run_ref,
                    *, region, group_batches, n_lat_tiles):
    mod = mod_ref[...]
    y = _dot(oa_ref[...], wa_ref[...]) + _dot(ob_ref[...], wb_ref[...]) + _dot(oc_ref[...], wc_ref[...])
    x1 = jnp.where(pl.program_id(1) == n_lat_tiles, xc_ref[...], x_ref[...]) + mod[2:3] * y
    x1_ref[...] = x1
    ms = jnp.mean(x1 * x1, axis=-1, keepdims=True)
    t = (x1 * lax.rsqrt(ms + EPS)) * gain_ref[...]
    t = t * (1.0 + mod[4:5]) + mod[3:4]
    tok_ref[...] = _pack_bf16_pairs(t)

    t_hi, t_lo = _split_bf16(t)
    wrh = wrh_ref[...]
    logits = _dot(t_hi, wrh) + _dot(t_lo, wrh) + _dot(t_hi, wrl_ref[...]) + br_ref[...]

    lane = lax.broadcasted_iota(jnp.int32, logits.shape, 1)
    lane_f = lane.astype(F32)
    is_g = lane < N_GROUPS
    gl = jnp.where(is_g, logits, NEG)
    gmax = jnp.max(gl, axis=-1, keepdims=True)
    g_sel = jnp.min(jnp.where(gl == gmax, lane_f, 1e9), axis=-1, keepdims=True)
    p_grp = 1.0 / jnp.sum(jnp.where(is_g, jnp.exp(gl - gmax), 0.0), axis=-1, keepdims=True)
    grp_of_lane = lax.shift_right_arithmetic(lane - N_GROUPS, 2).astype(F32)
    in_grp = (lane >= N_GROUPS) & (lane < N_GROUPS + N_EXPERTS) & (grp_of_lane == g_sel)
    el = jnp.where(in_grp, logits, NEG)
    v1 = jnp.max(el, axis=-1, keepdims=True)
    i1 = jnp.min(jnp.where(el == v1, lane_f, 1e9), axis=-1, keepdims=True)
    el2 = jnp.where(lane_f == i1, NEG, el)
    v2 = jnp.max(el2, axis=-1, keepdims=True)
    i2 = jnp.min(jnp.where(el2 == v2, lane_f, 1e9), axis=-1, keepdims=True)
    e2 = jnp.exp(v2 - v1)
    den = 1.0 + e2
    w1 = p_grp / den
    w2 = p_grp * e2 / den

    @pl.when((lax.rem(pl.program_id(0), group_batches) == 0) & (pl.program_id(1) == 0))
    def _():
        run_ref[...] = jnp.zeros(run_ref.shape, F32)

    ind = jnp.where(lane_f == i1, 1.0, 0.0) + jnp.where(lane_f == i2, 1.0, 0.0)
    rank = _dot(tri_ref[...], ind.astype(BF16)) + run_ref[0:1, :]

    def pick(m, l):
        return jnp.sum(jnp.where(lane_f == l, m, 0.0), axis=-1, keepdims=True)

    pos1 = (i1 - N_GROUPS) * region + pick(rank, i1)
    pos2 = (i2 - N_GROUPS) * region + pick(rank, i2)
    route_ref[...] = jnp.where(lane == 0, pos1, jnp.where(lane == 1, pos2,
                               jnp.where(lane == 2, w1, jnp.where(lane == 3, w2, 0.0))))
    run = run_ref[...] + jnp.sum(ind, axis=0, keepdims=True)
    run_ref[...] = run
    cnt_ref[...] = run


def _out_projection(x_lat, x_ctx, ctx_blk, oa, ob, oc, wa, wb, wc, modsel, gain, wrh, wrl, br, n_tiles,
                    n_lat_tiles):
    b, _, d = x_lat.shape
    tok = lambda bi, ti: (bi, ti, 0)
    const2 = lambda bi, ti: (0, 0)
    rows = n_tiles * TILE
    nb = b // MOE_GROUPS if b % MOE_GROUPS == 0 else b
    tri = jnp.asarray(np.tril(np.ones((TILE, TILE), np.float32), -1), BF16)
    return pl.pallas_call(
        functools.partial(_outproj_kernel, region=nb * rows, group_batches=nb, n_lat_tiles=n_lat_tiles),
        out_shape=[jax.ShapeDtypeStruct((b, rows, d), F32),
                   jax.ShapeDtypeStruct((b, rows, d // 2), jnp.int32),
                   jax.ShapeDtypeStruct((b, rows, LANES), F32),
                   jax.ShapeDtypeStruct((SUBLANES * (b // nb), LANES), F32)],
        grid=(b, n_tiles),
        in_specs=_token_specs(d, n_lat_tiles, ctx_blk) + [
                  pl.BlockSpec((None, TILE, W_A), tok),
                  pl.BlockSpec((None, TILE, W_B), tok),
                  pl.BlockSpec((None, TILE, W_C), tok),
                  pl.BlockSpec((W_A, d), const2),
                  pl.BlockSpec((W_B, d), const2),
                  pl.BlockSpec((W_C, d), const2),
                  pl.BlockSpec((None, 6, d), lambda bi, ti: (2 * bi + (ti >= n_lat_tiles).astype(jnp.int32), 0, 0)),
                  pl.BlockSpec((1, d), const2),
                  pl.BlockSpec((d, LANES), const2),
                  pl.BlockSpec((d, LANES), const2),
                  pl.BlockSpec((1, LANES), const2),
                  pl.BlockSpec((TILE, TILE), const2)],
        out_specs=[pl.BlockSpec((None, TILE, d), tok),
                   pl.BlockSpec((None, TILE, d // 2), tok),
                   pl.BlockSpec((None, TILE, LANES), tok),
                   pl.BlockSpec((SUBLANES, LANES), lambda bi, ti: (bi // nb, 0))],
        scratch_shapes=[pltpu.VMEM((SUBLANES, LANES), F32)],
        compiler_params=_cparams(2),
        name="out_projection",
    )(x_lat, x_ctx, oa, ob, oc, wa, wb, wc, modsel, gain, wrh, wrl, br, tri)


def _sc_mesh():
    return plsc.VectorSubcoreMesh(core_axis_name="core", subcore_axis_name="subcore")


def _sc_worker_base(per_worker):
    wid = lax.axis_index("subcore") * SC_CORES + lax.axis_index("core")
    return wid * per_worker


def _sc_scratch(d, dtype):
    return ([pltpu.VMEM((SC_ROWS,), jnp.int32)] * SC_BUFS + [pltpu.VMEM((SC_ROWS, d), dtype)] * SC_BUFS
            + [pltpu.SemaphoreType.DMA] * (2 * SC_BUFS))


def _sc_split(scratch):
    return (scratch[:SC_BUFS], scratch[SC_BUFS:2 * SC_BUFS], scratch[2 * SC_BUFS:3 * SC_BUFS],
            scratch[3 * SC_BUFS:])


def _sc_chunk_loop(per_worker, group):
    chunks = per_worker // SC_ROWS
    full = chunks // SC_BUFS * SC_BUFS

    @pl.loop(0, full, step=SC_BUFS)
    def _(c):
        group(c, SC_BUFS)

    if chunks > full:
        group(full, chunks - full)


def _sc_scatter_rows(x, row_off, n, idx, n_out):
    d = x.shape[1]
    per_worker = 2 * n // (SC_CORES * SC_SUBCORES)
    assert per_worker % SC_ROWS == 0 and n % SC_ROWS == 0

    @functools.partial(pl.kernel, out_type=jax.ShapeDtypeStruct((n_out, d), x.dtype),
                       mesh=_sc_mesh(), scratch_types=_sc_scratch(d, x.dtype))
    def scatter(x_hbm, i_hbm, o_hbm, *scratch):
        idx_v, rows_v, sem_in, sem_out = _sc_split(scratch)
        base = _sc_worker_base(per_worker)

        def group(c, n_bufs):
            reads = []
            for u in range(n_bufs):
                a = pl.multiple_of(base + (c + u) * SC_ROWS, SC_ROWS)
                t = pl.multiple_of(row_off + lax.rem(a, n), SC_ROWS)
                pltpu.sync_copy(i_hbm.at[pl.ds(a, SC_ROWS)], idx_v[u])
                reads.append(pltpu.async_copy(x_hbm.at[pl.ds(t, SC_ROWS)], rows_v[u], sem_in[u]))
            writes = []
            for u in range(n_bufs):
                reads[u].wait()
                writes.append(pltpu.async_copy(rows_v[u], o_hbm.at[idx_v[u]], sem_out[u]))
            for w in writes:
                w.wait()

        _sc_chunk_loop(per_worker, group)

    return scatter(x, idx)


def _sc_gather_rows(src, idx):
    m = idx.shape[0]
    d = src.shape[1]
    per_worker = m // (SC_CORES * SC_SUBCORES)
    assert per_worker % SC_ROWS == 0

    @functools.partial(pl.kernel, out_type=jax.ShapeDtypeStruct((m, d), src.dtype),
                       mesh=_sc_mesh(), scratch_types=_sc_scratch(d, src.dtype))
    def gather(s_hbm, i_hbm, o_hbm, *scratch):
        idx_v, rows_v, sem_in, sem_out = _sc_split(scratch)
        base = _sc_worker_base(per_worker)

        def group(c, n_bufs):
            offs, reads = [], []
            for u in range(n_bufs):
                a = pl.multiple_of(base + (c + u) * SC_ROWS, SC_ROWS)
                offs.append(a)
                pltpu.sync_copy(i_hbm.at[pl.ds(a, SC_ROWS)], idx_v[u])
                reads.append(pltpu.async_copy(s_hbm.at[idx_v[u]], rows_v[u], sem_in[u]))
            writes = []
            for u in range(n_bufs):
                reads[u].wait()
                writes.append(pltpu.async_copy(rows_v[u], o_hbm.at[pl.ds(offs[u], SC_ROWS)], sem_out[u]))
            for w in writes:
                w.wait()

        _sc_chunk_loop(per_worker, group)

    return gather(src, idx)


def _expert_ffn_kernel(blk_ref, exp_ref, x_ref, wg_ref, wu_ref, wd_ref, y_ref, wgb_ref, wub_ref, wdb_ref):
    j = pl.program_id(0)

    @pl.when((j == 0) | (exp_ref[j] != exp_ref[jnp.maximum(j - 1, 0)]))
    def _():
        wgb_ref[...] = wg_ref[...].astype(BF16)
        wub_ref[...] = wu_ref[...].astype(BF16)
        wdb_ref[...] = wd_ref[...].astype(BF16)

    @pl.when((j == 0) | (blk_ref[j] != blk_ref[jnp.maximum(j - 1, 0)]))
    def _():
        x = _unpack_bf16_pairs(x_ref[...]).astype(BF16)
        hid = jax.nn.silu(_dot(x, wgb_ref[...])) * _dot(x, wub_ref[...])
        y_ref[...] = _pack_bf16_pairs(_dot(hid.astype(BF16), wdb_ref[...]))


def _expert_ffn(xs, blk, exp, wg, wu, wd, layer):
    rows, d_packed = xs.shape
    d = 2 * d_packed
    w_map = lambda j, blk, exp: (layer, exp[j], 0, 0)
    return pl.pallas_call(
        _expert_ffn_kernel,
        out_shape=jax.ShapeDtypeStruct((rows, d_packed), jnp.int32),
        grid_spec=pltpu.PrefetchScalarGridSpec(
            num_scalar_prefetch=2,
            grid=(blk.shape[0],),
            in_specs=[pl.BlockSpec((MOE_TILE, d_packed), lambda j, blk, exp: (blk[j], 0)),
                      pl.BlockSpec((None, None, d, EXPERT_HIDDEN), w_map),
                      pl.BlockSpec((None, None, d, EXPERT_HIDDEN), w_map),
                      pl.BlockSpec((None, None, EXPERT_HIDDEN, d), w_map)],
            out_specs=pl.BlockSpec((MOE_TILE, d_packed), lambda j, blk, exp: (blk[j], 0)),
            scratch_shapes=[pltpu.VMEM((d, EXPERT_HIDDEN), BF16),
                            pltpu.VMEM((d, EXPERT_HIDDEN), BF16),
                            pltpu.VMEM((EXPERT_HIDDEN, d), BF16)]),
        compiler_params=_cparams(1),
        name="expert_ffn",
    )(blk, exp, xs, wg, wu, wd)


def _combine_kernel(x1_ref, y1_ref, y2_ref, route_ref, mod_ref, fgain_ref, *rest, final):
    o_ref = rest[-1]
    route = route_ref[...]
    y = route[:, 2:3] * _unpack_bf16_pairs(y1_ref[...]) + route[:, 3:4] * _unpack_bf16_pairs(y2_ref[...])
    x2 = x1_ref[...] + mod_ref[5:6, :] * y
    if final:
        ms = jnp.mean(x2 * x2, axis=-1, keepdims=True)
        x2 = (x2 * lax.rsqrt(ms + EPS)) * fgain_ref[...]
    o_ref[...] = x2


def _combine(x1, ys, route, modsel, fgain, prev, b0, nb, n_lat_tiles, final):
    b, rows, d = x1.shape
    n_t = rows // TILE
    tok = lambda bi, ti: (b0 + bi, ti, 0)
    in_specs = [pl.BlockSpec((None, TILE, d), tok),
                pl.BlockSpec((TILE, d // 2), lambda bi, ti: (bi * n_t + ti, 0)),
                pl.BlockSpec((TILE, d // 2), lambda bi, ti: ((nb + bi) * n_t + ti, 0)),
                pl.BlockSpec((None, TILE, LANES), tok),
                pl.BlockSpec((None, 6, d),
                             lambda bi, ti: (2 * (b0 + bi) + (ti >= n_lat_tiles).astype(jnp.int32), 0, 0)),
                pl.BlockSpec((1, d), lambda bi, ti: (0, 0))]
    args = [x1, ys, ys, route, modsel, fgain]
    aliases = {}
    if prev is not None:
        in_specs.append(pl.BlockSpec(memory_space=pl.ANY))
        args.append(prev)
        aliases = {len(args) - 1: 0}
    return pl.pallas_call(
        functools.partial(_combine_kernel, final=final),
        out_shape=jax.ShapeDtypeStruct((b, rows, d), F32),
        grid=(nb, n_t),
        in_specs=in_specs,
        out_specs=pl.BlockSpec((None, TILE, d), tok),
        input_output_aliases=aliases,
        compiler_params=_cparams(2),
        name="moe_combine",
    )(*args)


def _routed_moe(tok, route, cnt, x1, wg, wu, wd, layer, modsel, fgain, n_lat_tiles, final):
    b, rows, d = x1.shape
    n_groups = cnt.shape[0] // SUBLANES
    nb = b // n_groups
    n = nb * rows
    flat = route.reshape(b * rows, LANES)
    tok_flat = tok.reshape(b * rows, tok.shape[2])
    out = None
    for g in range(n_groups):
        part = flat[g * n:(g + 1) * n]
        idx = jnp.concatenate([part[:, 0], part[:, 1]]).astype(jnp.int32)
        xs = _sc_scatter_rows(tok_flat, g * n, n, idx, N_EXPERTS * n)

        counts = cnt[SUBLANES * g, N_GROUPS:N_GROUPS + N_EXPERTS].astype(jnp.int32)
        tiles = (counts + MOE_TILE - 1) // MOE_TILE
        ends = jnp.cumsum(tiles)
        n_sched = 2 * n // MOE_TILE + N_EXPERTS
        j = jnp.minimum(jnp.arange(n_sched, dtype=jnp.int32), ends[-1] - 1)
        exp = jnp.sum((j[:, None] >= ends[None, :]).astype(jnp.int32), axis=1)
        blk = exp * (n // MOE_TILE) + j - (ends - tiles)[exp]

        ys = _expert_ffn(xs, blk, exp, wg, wu, wd, layer)
        yg = _sc_gather_rows(ys, idx)
        out = _combine(x1, yg, route, modsel, fgain, out, g * nb, nb, n_lat_tiles, final)
    return out


def _rope_tables(n_lat):
    t = jnp.arange(n_lat)
    row = (t // GRID_W).astype(F32)
    col = (t % GRID_W).astype(F32)

    def cs(dim):
        quarter = dim // 4
        freqs = ROPE_THETA ** (-jnp.arange(quarter, dtype=F32) / quarter)
        ang = jnp.concatenate([row[:, None] * freqs, col[:, None] * freqs], axis=-1)
        cos = jnp.tile(jnp.cos(ang), (1, 2 * LANES // dim))
        sin = jnp.tile(jnp.sin(ang), (1, 2 * LANES // dim))
        cos = jnp.concatenate([cos, jnp.ones((CTX_LEN, LANES), F32)], axis=0)
        sin = jnp.concatenate([sin, jnp.zeros((CTX_LEN, LANES), F32)], axis=0)
        return cos, sin

    cos_b, sin_b = cs(DIFF_QK_DIM)
    cos_c, sin_c = cs(HEAD_DIM)
    return jnp.concatenate([cos_b, sin_b, cos_c, sin_c], axis=1)


def _reordered_w_in(w_in):
    o_c = 3 * W_A + 3 * W_B
    heads = [w_in[:, o_c + h * HEAD_DIM:o_c + (h + 1) * HEAD_DIM] for h in GQA_Q_ORDER]
    return jnp.concatenate([w_in[:, :o_c]] + heads + [w_in[:, o_c + W_C:]], axis=1).astype(BF16)


def kernel(x, c, ctx, c_ctx, w_mod, b_mod, norm_attn, norm_ffn, w_in, w_out, na_rpb, diff_lambda_q1, diff_lambda_k1, diff_lambda_q2, diff_lambda_k2, diff_subln, gqa_q_norm, gqa_k_norm, router_group_w, router_group_b, router_expert_w, router_expert_b, w_gate, w_up, w_down, final_norm):
    b, s, d = x.shape
    assert d == D_MODEL and ctx.shape[1] == CTX_LEN and s % (NA_QROWS * GRID_W) == 0
    rows = s // GRID_W
    assert rows >= 2 * NA_QROWS
    t_all = s + CTX_LEN
    n_lat_tiles = s // TILE

    assert b + 1 <= SUBLANES
    c_rows = jnp.zeros((SUBLANES, d), F32).at[:b].set(c).at[b].set(c_ctx)
    mod = _modulation(c_rows, w_mod, b_mod)

    tab = _rope_tables(s)
    hidx = np.arange(HEAD_DIM)
    partner = np.where(hidx < HEAD_DIM // 2, hidx + HEAD_DIM // 2, hidx - HEAD_DIM // 2)
    blk = np.arange(W_C) // HEAD_DIM
    ones = jnp.asarray((blk[:, None] == blk[None, :]).astype(np.float32), BF16)
    dummy_aux = jnp.zeros((SUBLANES, LANES), F32)

    x_lat, x_ctx, ctx_blk = x, ctx, 0
    for l in range(DEPTH):
        ctx_out = l < DEPTH - 1
        lam_init = 0.8 - 0.6 * math.exp(-0.3 * l)
        m_lat = mod[l, :b].reshape(b, 1, 6, d)
        m_ctx = jnp.broadcast_to(mod[l, b].reshape(1, 1, 6, d), (b, 1, 6, d))
        modsel = jnp.concatenate([m_lat, m_ctx], axis=1).reshape(2 * b, 6, d)

        gq = jnp.stack([jnp.tile(gqa_q_norm[l], GQA_Q_HEADS), jnp.tile(gqa_q_norm[l][partner], GQA_Q_HEADS)])
        gk = jnp.stack([jnp.tile(gqa_k_norm[l], GQA_KV_HEADS), jnp.tile(gqa_k_norm[l][partner], GQA_KV_HEADS)])
        qa, ka, va, qb, kb, vb, qc, kc, vc = _in_projection(
            x_lat, x_ctx, ctx_blk, modsel, norm_attn[l][None], _reordered_w_in(w_in[l]), tab, gq, gk, ones,
            n_lat_tiles)

        n_qt = n_lat_tiles + 1 if ctx_out else n_lat_tiles
        pad = lambda v: jnp.pad(v, (0, LANES - v.shape[0]))
        aux = jnp.stack([pad(diff_lambda_q1[l]), pad(diff_lambda_k1[l]), pad(diff_lambda_q2[l]),
                         pad(diff_lambda_k2[l]), jnp.tile(diff_subln[l], 2),
                         jnp.zeros((LANES,), F32), jnp.zeros((LANES,), F32), jnp.zeros((LANES,), F32)])
        group_a = dict(n_qblk=1, n_sub=2, n_hp=NA_HEADS // 2, n_lat=s, mode="plain")
        group_b = dict(n_qblk=1, n_sub=4, n_hp=DIFF_HEADS // 2, n_lat=s, mode="diff", lam_init=lam_init)
        group_c = dict(n_qblk=3, n_sub=2, n_hp=1, n_lat=s, mode="plain")
        oa = _neighbourhood_attention(qa, ka, va, _na_bias_table(na_rpb[l], rows), s)
        ob = _flash(qb, kb, vb, aux, queries="latent", **group_b)
        oc = _flash(qc, kc, vc, dummy_aux, queries="latent", **group_c)
        if ctx_out:
            oa = jnp.concatenate([oa, _flash(qa, ka, va, dummy_aux, queries="context", **group_a)], axis=1)
            ob = jnp.concatenate([ob, _flash(qb, kb, vb, aux, queries="context", **group_b)], axis=1)
            oc = jnp.concatenate([oc, _flash(qc, kc, vc, dummy_aux, queries="context", **group_c)], axis=1)

        w_o = w_out[l]
        o_c = W_A + W_B
        w_oc = jnp.concatenate([w_o[o_c + h * HEAD_DIM:o_c + (h + 1) * HEAD_DIM] for h in GQA_Q_ORDER], axis=0)
        wr = jnp.zeros((d, LANES), F32)
        wr = wr.at[:, :N_GROUPS].set(router_group_w[l]).at[:, N_GROUPS:N_GROUPS + N_EXPERTS].set(router_expert_w[l])
        wrh, wrl = _split_bf16(wr)
        br = jnp.zeros((1, LANES), F32)
        br = br.at[0, :N_GROUPS].set(router_group_b[l]).at[0, N_GROUPS:N_GROUPS + N_EXPERTS].set(router_expert_b[l])
        x1, tok, route, cnt = _out_projection(
            x_lat, x_ctx, ctx_blk, oa, ob, oc, w_o[:W_A].astype(BF16), w_o[W_A:W_A + W_B].astype(BF16),
            w_oc.astype(BF16), modsel, norm_ffn[l][None], wrh, wrl, br, n_qt, n_lat_tiles)
        xs = _routed_moe(tok, route, cnt, x1, w_gate, w_up, w_down, l, modsel, final_norm[None],
                         n_lat_tiles, final=not ctx_out)
        x_lat, x_ctx, ctx_blk = xs, xs, n_lat_tiles
    return xs
```

```python
import functools
import math

import numpy as np
import jax
import jax.numpy as jnp
from jax import lax
from jax.experimental import pallas as pl
from jax.experimental.pallas import tpu as pltpu
from jax.experimental.pallas import tpu_sc as plsc

F32 = jnp.float32
BF16 = jnp.bfloat16

D_MODEL = 1024
DEPTH = 2
GRID_W = 64
CTX_LEN = 256
HEAD_DIM = 64
NA_HEADS = 6
NA_WIN_H = 8
NA_WIN_W = 16
DIFF_HEADS = 4
DIFF_QK_DIM = 32
GQA_Q_HEADS = 6
GQA_KV_HEADS = 2
N_GROUPS = 4
EXPERTS_PER_GROUP = 4
N_EXPERTS = 16
EXPERT_HIDDEN = 512
ROPE_THETA = 10000.0
EPS = 1e-6
W_A = NA_HEADS * HEAD_DIM
W_B = DIFF_HEADS * 2 * DIFF_QK_DIM
W_C = GQA_Q_HEADS * HEAD_DIM
W_KC = GQA_KV_HEADS * HEAD_DIM
IN_WIDTH = 3 * W_A + 3 * W_B + W_C + 2 * W_KC

LANES = 128
TILE = CTX_LEN
NA_QROWS = 8
NA_KROWS = 16
NA_PARTS = 2
NEG = -1e30
LOG2E = 1.4426950408889634
HI16 = -65536
VMEM_LIMIT = 56 * 1024 * 1024
FLASH_TK = 512
PAIRS_PER_STEP = 2
FLASH_UNROLL_MAX_ROWS = 1536
SUBLANES = 8
ROUTE_ROWS = 32
MOE_TILE = 512
MOE_GROUPS = 2
SC_ROWS = 32
SC_BUFS = 4
SC_CORES = 2
SC_SUBCORES = 16

GQA_Q_ORDER = (0, 3, 1, 4, 2, 5)


def _cparams(n_axes):
    return pltpu.CompilerParams(dimension_semantics=("arbitrary",) * n_axes,
                                vmem_limit_bytes=VMEM_LIMIT)


def _split_bf16(a):
    hi = a.astype(BF16)
    lo = (a - hi.astype(F32)).astype(BF16)
    return hi, lo


def _dot(a, b):
    return jnp.dot(a, b, preferred_element_type=F32)


def _pack_bf16_pairs(t):
    bits = lax.bitcast_convert_type(t.astype(BF16).astype(F32), jnp.int32)
    half_d = bits.shape[1] // 2
    return lax.shift_right_logical(bits[:, :half_d], 16) | (bits[:, half_d:] & HI16)


def _unpack_bf16_pairs(w):
    return jnp.concatenate([lax.bitcast_convert_type(lax.shift_left(w, 16), F32),
                            lax.bitcast_convert_type(w & HI16, F32)], axis=1)


def _dot_nt(a, b):
    return lax.dot_general(a, b, (((1,), (1,)), ((), ())), preferred_element_type=F32)


def _mod_kernel(c_ref, w_ref, b_ref, o_ref):
    c = c_ref[...]
    a = c * jax.nn.sigmoid(c)
    a_hi, a_lo = _split_bf16(a)
    w_hi, w_lo = _split_bf16(w_ref[...])
    o_ref[...] = _dot(a_hi, w_hi) + _dot(a_lo, w_hi) + _dot(a_hi, w_lo) + b_ref[...]


def _modulation(c_rows, w_mod, b_mod):
    depth, d, n = w_mod.shape
    bn = 1536
    return pl.pallas_call(
        _mod_kernel,
        out_shape=jax.ShapeDtypeStruct((depth, SUBLANES, n), F32),
        grid=(depth, n // bn),
        in_specs=[pl.BlockSpec((SUBLANES, d), lambda l, j: (0, 0)),
                  pl.BlockSpec((None, d, bn), lambda l, j: (l, 0, j)),
                  pl.BlockSpec((None, 1, bn), lambda l, j: (l, 0, j))],
        out_specs=pl.BlockSpec((None, SUBLANES, bn), lambda l, j: (l, 0, j)),
        compiler_params=_cparams(2),
        name="adaln_mod",
    )(c_rows, w_mod, b_mod.reshape(depth, 1, n))


def _head_mean_sq(t, ones):
    hi, lo = _split_bf16(t * t)
    return (_dot(hi, ones) + _dot(lo, ones)) * (1.0 / HEAD_DIM)


def _rotate_half(p, head):
    w = p.shape[1]
    half = head // 2
    lane = lax.broadcasted_iota(jnp.int32, (1, w), 1)
    first = (lane & (head - 1)) < half
    from_right = pltpu.roll(p, w - half, 1)
    from_left = pltpu.roll(p, half, 1)
    return jnp.where(first, -from_right, from_left)


def _inproj_kernel(x_ref, xc_ref, mod_ref, gain_ref, w_ref, tab_ref, gq_ref, gk_ref, ones_ref,
                   qa_ref, ka_ref, va_ref, qb_ref, kb_ref, vb_ref, qc_ref, kc_ref, vc_ref,
                   *, n_lat_tiles):
    x = jnp.where(pl.program_id(1) == n_lat_tiles, xc_ref[...], x_ref[...])
    mod = mod_ref[...]
    ms = jnp.mean(x * x, axis=-1, keepdims=True)
    h = (x * lax.rsqrt(ms + EPS)) * gain_ref[...]
    h = h * (1.0 + mod[1:2]) + mod[0:1]
    hb = h.astype(BF16)

    def proj(a, b):
        return _dot(hb, w_ref[:, a:b])

    pa = proj(0, 3 * W_A)
    qa_ref[...] = (pa[:, :W_A] * (HEAD_DIM ** -0.5 * LOG2E)).astype(BF16)
    ka_ref[...] = pa[:, W_A:2 * W_A].astype(BF16)
    va_ref[...] = pa[:, 2 * W_A:].astype(BF16)

    tab = tab_ref[...]
    cos_b = jnp.concatenate([tab[:, 0:LANES]] * 2, axis=1)
    sin_b = jnp.concatenate([tab[:, LANES:2 * LANES]] * 2, axis=1)
    cos_c1 = tab[:, 2 * LANES:3 * LANES]
    sin_c1 = tab[:, 3 * LANES:4 * LANES]
    cos_c = jnp.concatenate([cos_c1] * 3, axis=1)
    sin_c = jnp.concatenate([sin_c1] * 3, axis=1)

    o_b = 3 * W_A
    pb = proj(o_b, o_b + 3 * W_B)
    qb = pb[:, :W_B]
    kb = pb[:, W_B:2 * W_B]
    qb = qb * cos_b + _rotate_half(qb, DIFF_QK_DIM) * sin_b
    qb_ref[...] = (qb * (DIFF_QK_DIM ** -0.5 * LOG2E)).astype(BF16)
    kb_ref[...] = (kb * cos_b + _rotate_half(kb, DIFF_QK_DIM) * sin_b).astype(BF16)
    vb_ref[...] = pb[:, 2 * W_B:].astype(BF16)

    o_c = o_b + 3 * W_B
    pc = proj(o_c, IN_WIDTH)
    ones = ones_ref[...]
    qc = pc[:, :W_C]
    kc = pc[:, W_C:W_C + W_KC]
    nq = lax.rsqrt(_head_mean_sq(qc, ones) + EPS)
    nk = lax.rsqrt(_head_mean_sq(kc, ones[:W_KC, :W_KC]) + EPS)
    gq = gq_ref[...]
    gk = gk_ref[...]
    q = nq * (qc * gq[0:1] * cos_c + _rotate_half(qc, HEAD_DIM) * gq[1:2] * sin_c)
    qc_ref[...] = (q * (HEAD_DIM ** -0.5 * LOG2E)).astype(BF16)
    k = nk * (kc * gk[0:1] * cos_c1 + _rotate_half(kc, HEAD_DIM) * gk[1:2] * sin_c1)
    kc_ref[...] = k.astype(BF16)
    vc_ref[...] = pc[:, W_C + W_KC:].astype(BF16)


def _token_specs(d, n_lat_tiles, ctx_blk):
    return [pl.BlockSpec((None, TILE, d), lambda bi, ti: (bi, jnp.minimum(ti, n_lat_tiles - 1), 0)),
            pl.BlockSpec((None, TILE, d), lambda bi, ti: (bi, ctx_blk, 0))]


def _in_projection(x_lat, x_ctx, ctx_blk, modsel, gain, w_ext, tab, gq, gk, ones, n_lat_tiles):
    b, _, d = x_lat.shape
    n_tiles = n_lat_tiles + 1
    t_all = n_tiles * TILE
    widths = (W_A, W_A, W_A, W_B, W_B, W_B, W_C, W_KC, W_KC)
    tok = lambda bi, ti: (bi, ti, 0)
    const2 = lambda bi, ti: (0, 0)
    return pl.pallas_call(
        functools.partial(_inproj_kernel, n_lat_tiles=n_lat_tiles),
        out_shape=[jax.ShapeDtypeStruct((b, t_all, w), BF16) for w in widths],
        grid=(b, n_tiles),
        in_specs=_token_specs(d, n_lat_tiles, ctx_blk) + [
                  pl.BlockSpec((None, 6, d), lambda bi, ti: (2 * bi + (ti >= n_lat_tiles).astype(jnp.int32), 0, 0)),
                  pl.BlockSpec((1, d), const2),
                  pl.BlockSpec((d, IN_WIDTH), const2),
                  pl.BlockSpec((TILE, 4 * LANES), lambda bi, ti: (ti, 0)),
                  pl.BlockSpec((2, W_C), const2),
                  pl.BlockSpec((2, W_KC), const2),
                  pl.BlockSpec((W_C, W_C), const2)],
        out_specs=[pl.BlockSpec((None, TILE, w), tok) for w in widths],
        compiler_params=_cparams(2),
        name="in_projection",
    )(x_lat, x_ctx, modsel, gain, w_ext, tab, gq, gk, ones)


def _flash_kernel(q_ref, k_ref, v_ref, aux_ref, o_ref, va_ref, vb_ref, qs_ref, acc_ref, m_ref,
                  s0_ref, s1_ref, mb0_ref, mb1_ref, *,
                  n_qblk, n_sub, tk, n_lat_blocks, pairs_per_step, ctx_start, queries, mode, lam_init):
    sub_w = LANES // n_sub
    half = LANES // 2
    lane = lax.broadcasted_iota(jnp.int32, (1, LANES), 1)
    lower = lane < half
    n_pieces = n_qblk * n_sub
    ma = (n_pieces // 2) * TILE
    m_rows = n_pieces * TILE

    @pl.when(pl.program_id(2) == 0)
    def _():
        v = v_ref[...].astype(F32)
        va_ref[...] = jnp.where(lower, v, 1.0).astype(BF16)
        vb_ref[...] = jnp.where(lower, 1.0, v).astype(BF16)

    ia, ib = 0, n_pieces // 2
    for blk in range(n_qblk):
        qf = q_ref[:, blk * LANES:(blk + 1) * LANES].astype(F32)
        for sub in range(n_sub):
            msk = (lane >= sub * sub_w) & (lane < (sub + 1) * sub_w)
            piece = jnp.where(msk, qf, 0.0).astype(BF16)
            if sub * sub_w < half:
                qs_ref[ia * TILE:(ia + 1) * TILE, :] = piece
                ia += 1
            else:
                qs_ref[ib * TILE:(ib + 1) * TILE, :] = piece
                ib += 1

    s_bufs = (s0_ref, s1_ref)
    mb_bufs = (mb0_ref, mb1_ref)

    def scores(start, size, slot):
        s = _dot_nt(qs_ref[...], k_ref[pl.ds(start, size), :])
        s_bufs[slot][:, :size] = s
        mb = jnp.max(s, axis=-1, keepdims=True)
        mb_bufs[slot][...] = jnp.broadcast_to(mb, (m_rows, LANES))

    def accumulate(start, size, slot, first):
        mb = mb_bufs[slot][...]
        if first:
            m_new = mb
        else:
            m_old = m_ref[...]
            m_new = jnp.maximum(m_old, mb)
        s_ref = s_bufs[slot]
        cols = [s_ref[:, c * LANES:(c + 1) * LANES] - m_new for c in range(size // LANES)]
        p = jnp.concatenate([jnp.exp2(d.astype(BF16)) for d in cols], axis=1)
        pva = _dot(p[:ma], va_ref[pl.ds(start, size), :])
        pvb = _dot(p[ma:], vb_ref[pl.ds(start, size), :])
        if first:
            acc_ref[:ma, :] = pva
            acc_ref[ma:, :] = pvb
        else:
            alpha = jnp.exp2(m_old - m_new)
            acc_ref[:ma, :] = alpha[:ma] * acc_ref[:ma, :] + pva
            acc_ref[ma:, :] = alpha[ma:] * acc_ref[ma:, :] + pvb
        m_ref[...] = m_new

    def lat(j):
        return pl.multiple_of(j * tk, tk)

    def latent_queries():
        scores(ctx_start, CTX_LEN, 0)
        scores(lat(0), tk, 1)
        accumulate(ctx_start, CTX_LEN, 0, True)

        def pair(i):
            scores(lat(2 * i + 1), tk, 0)
            accumulate(lat(2 * i), tk, 1, False)
            scores(lat(2 * i + 2), tk, 1)
            accumulate(lat(2 * i + 1), tk, 0, False)

        def body(i, carry):
            for u in range(pairs_per_step):
                pair(i * pairs_per_step + u)
            return carry

        n_pairs = (n_lat_blocks - 2) // 2
        n_steps = n_pairs // pairs_per_step
        lax.fori_loop(0, n_steps, body, 0)
        for i in range(n_steps * pairs_per_step, n_pairs):
            pair(i)
        scores(lat(n_lat_blocks - 1), tk, 0)
        accumulate(lat(n_lat_blocks - 2), tk, 1, False)
        accumulate(lat(n_lat_blocks - 1), tk, 0, False)

    def context_queries():
        scores(ctx_start, CTX_LEN, 0)
        accumulate(ctx_start, CTX_LEN, 0, True)

    if queries == "latent":
        latent_queries()
    else:
        context_queries()

    acc = acc_ref[...]
    r = acc / pltpu.roll(acc, half, 1)
    ra, rb = r[:ma], r[ma:]
    if mode == "plain":
        for i in range(n_pieces // 2):
            o = jnp.where(lower, ra[i * TILE:(i + 1) * TILE], rb[i * TILE:(i + 1) * TILE])
            o_ref[:, i * LANES:(i + 1) * LANES] = o.astype(BF16)
    else:
        aux = aux_ref[...]
        l1 = jnp.sum(aux[0:1] * aux[1:2], axis=-1, keepdims=True)
        l2 = jnp.sum(aux[2:3] * aux[3:4], axis=-1, keepdims=True)
        lam = jnp.exp(l1) - jnp.exp(l2) + lam_init
        oa = ra[:TILE] - lam * ra[TILE:]
        ob = rb[:TILE] - lam * rb[TILE:]
        o = jnp.where(lower, oa, ob)
        sq = o * o
        ss_a = jnp.sum(jnp.where(lower, sq, 0.0), axis=-1, keepdims=True)
        ss_b = jnp.sum(jnp.where(lower, 0.0, sq), axis=-1, keepdims=True)
        ms = jnp.where(lower, ss_a, ss_b) * (1.0 / HEAD_DIM)
        o = (o * lax.rsqrt(ms + EPS)) * aux[4:5]
        o_ref[...] = (o * (1.0 - lam_init)).astype(BF16)


def _flash(q, k, v, aux, *, n_qblk, n_sub, n_hp, n_lat, queries, mode, lam_init=0.0):
    b, t_all, _ = q.shape
    qw = n_qblk * LANES
    tk = FLASH_TK
    assert n_lat % (2 * tk) == 0 and tk >= CTX_LEN
    m_rows = n_qblk * n_sub * TILE
    n_pairs = max((n_lat // tk - 2) // 2, 1)
    pairs = n_pairs if m_rows <= FLASH_UNROLL_MAX_ROWS else PAIRS_PER_STEP
    if queries == "latent":
        n_qt, qt_off, kv_rows, kv_blk, ctx_start = n_lat // TILE, 0, t_all, 0, n_lat
    else:
        n_qt, qt_off, kv_rows, kv_blk, ctx_start = 1, n_lat // TILE, CTX_LEN, n_lat // CTX_LEN, 0
    kern = functools.partial(_flash_kernel, n_qblk=n_qblk, n_sub=n_sub, tk=tk,
                             n_lat_blocks=n_lat // tk, ctx_start=ctx_start, queries=queries,
                             pairs_per_step=pairs, mode=mode, lam_init=lam_init)
    return pl.pallas_call(
        kern,
        out_shape=jax.ShapeDtypeStruct((b, n_qt * TILE, n_hp * qw), BF16),
        grid=(b, n_hp, n_qt),
        in_specs=[pl.BlockSpec((None, TILE, qw), lambda bi, hp, qt: (bi, qt + qt_off, hp)),
                  pl.BlockSpec((None, kv_rows, LANES), lambda bi, hp, qt: (bi, kv_blk, hp)),
                  pl.BlockSpec((None, kv_rows, LANES), lambda bi, hp, qt: (bi, kv_blk, hp)),
                  pl.BlockSpec((SUBLANES, LANES), lambda bi, hp, qt: (0, 0))],
        out_specs=pl.BlockSpec((None, TILE, qw), lambda bi, hp, qt: (bi, qt, hp)),
        scratch_shapes=[pltpu.VMEM((kv_rows, LANES), BF16),
                        pltpu.VMEM((kv_rows, LANES), BF16),
                        pltpu.VMEM((m_rows, LANES), BF16),
                        pltpu.VMEM((m_rows, LANES), F32),
                        pltpu.VMEM((m_rows, LANES), F32),
                        pltpu.VMEM((m_rows, tk), F32),
                        pltpu.VMEM((m_rows, tk), F32),
                        pltpu.VMEM((m_rows, LANES), F32),
                        pltpu.VMEM((m_rows, LANES), F32)],
        compiler_params=_cparams(3),
        name="flash_" + mode,
    )(q, k, v, aux)


def _na_kernel(q_ref, k0, k1, k2, k3, v0, v1, v2, v3, kc_ref, vc_ref, bias_ref, o_ref, s_ref, m_ref):
    lane = lax.broadcasted_iota(jnp.int32, (1, LANES), 1)
    lower = lane < LANES // 2
    n_pair = NA_KROWS // 2
    rows_per_part = NA_QROWS // NA_PARTS
    half_q = rows_per_part * GRID_W
    no_bias = jnp.zeros((GRID_W, CTX_LEN), F32)

    def head_pair(hp):
        cols = slice(hp * LANES, (hp + 1) * LANES)
        qf = q_ref[:, cols].astype(F32)
        k_all = jnp.concatenate([r[:, cols] for r in (k0, k1, k2, k3, kc_ref)], axis=0)
        v_all = jnp.concatenate([r[:, cols] for r in (v0, v1, v2, v3, vc_ref)], axis=0).astype(F32)
        v_h = [jnp.where(lower, v_all, 1.0).astype(BF16), jnp.where(lower, 1.0, v_all).astype(BF16)]
        q_h = [jnp.where(lower, qf, 0.0).astype(BF16), jnp.where(lower, 0.0, qf).astype(BF16)]
        return q_h, k_all, v_h

    pairs = [head_pair(hp) for hp in range(NA_HEADS // 2)]
    items = [(hp, part) for hp in range(NA_HEADS // 2) for part in range(NA_PARTS)]

    def scores(n):
        hp, part = items[n]
        q_h, k_all, _ = pairs[hp]
        rows = slice(part * half_q, (part + 1) * half_q)
        qs = jnp.concatenate([q_h[0][rows], q_h[1][rows]], axis=0)
        bias = jnp.concatenate(
            [jnp.concatenate([bias_ref[2 * hp + hh, a * n_pair + j] for j in range(n_pair)] + [no_bias],
                             axis=1)
             for hh in range(2) for a in range(part * rows_per_part, (part + 1) * rows_per_part)],
            axis=0)
        s = _dot_nt(qs, k_all) + bias
        s_ref[n % 2] = s
        m_ref[n % 2] = jnp.broadcast_to(jnp.max(s, axis=-1, keepdims=True), (2 * half_q, LANES))

    def finish(n):
        hp, part = items[n]
        v_h = pairs[hp][2]
        s = s_ref[n % 2]
        m = m_ref[n % 2]
        p = jnp.concatenate([jnp.exp2((s[:, c * LANES:(c + 1) * LANES] - m).astype(BF16))
                             for c in range(s.shape[1] // LANES)], axis=1)
        o0 = _dot(p[:half_q], v_h[0])
        o1 = _dot(p[half_q:], v_h[1])
        o0 = o0 / pltpu.roll(o0, LANES // 2, 1)
        o1 = o1 / pltpu.roll(o1, LANES // 2, 1)
        o_ref[part * half_q:(part + 1) * half_q, hp * LANES:(hp + 1) * LANES] = (
            jnp.where(lower, o0, o1).astype(BF16))

    scores(0)
    for n in range(1, len(items)):
        scores(n)
        finish(n - 1)
    finish(len(items) - 1)


def _neighbourhood_attention(qa, ka, va, bias, n_lat):
    b = qa.shape[0]
    q_tok = NA_QROWS * GRID_W
    v_tok = q_tok // 2
    n_rb = n_lat // q_tok
    n_view = n_lat // v_tok
    ctx_blk = n_lat // v_tok

    def view(j):
        return lambda rb, bi: (bi, jnp.clip(2 * rb - 1 + j, 0, n_view - 1), 0)

    kv_specs = [pl.BlockSpec((None, v_tok, W_A), view(j)) for j in range(4)]
    ctx_spec = pl.BlockSpec((None, CTX_LEN, W_A), lambda rb, bi: (bi, ctx_blk, 0))

    def bias_map(rb, bi):
        pat = jnp.where(rb == 0, 0, jnp.where(rb == n_rb - 1, 2, 1))
        return (0, pat, 0, 0, 0)

    part_rows = 2 * q_tok // NA_PARTS
    return pl.pallas_call(
        _na_kernel,
        out_shape=jax.ShapeDtypeStruct((b, n_lat, W_A), BF16),
        grid=(n_rb, b),
        in_specs=[pl.BlockSpec((None, q_tok, W_A), lambda rb, bi: (bi, rb, 0))]
                 + kv_specs + kv_specs + [ctx_spec, ctx_spec,
                 pl.BlockSpec((NA_HEADS, None, NA_QROWS * NA_KROWS // 2, GRID_W, 2 * GRID_W), bias_map)],
        out_specs=pl.BlockSpec((None, q_tok, W_A), lambda rb, bi: (bi, rb, 0)),
        scratch_shapes=[pltpu.VMEM((2, part_rows, NA_KROWS * GRID_W + CTX_LEN), F32),
                        pltpu.VMEM((2, part_rows, LANES), F32)],
        compiler_params=_cparams(2),
        name="neighbourhood_attention",
    )(qa, ka, ka, ka, ka, va, va, va, va, ka, va, bias)


def _na_bias_table(rpb, rows):
    cols = np.arange(GRID_W)
    c0 = np.clip(cols - NA_WIN_W // 2, 0, GRID_W - NA_WIN_W)
    cc = cols[None, :]
    col_ok = (cc >= c0[:, None]) & (cc < c0[:, None] + NA_WIN_W)
    dc = np.clip(cc - cols[:, None] + (NA_WIN_W - 1), 0, 2 * NA_WIN_W - 2)
    e = jnp.where(col_ok[None, None], (rpb.astype(F32) * LOG2E)[:, :, dc], NEG)
    e = jnp.concatenate([e, jnp.full_like(e[:, :1], NEG)], axis=1)
    a = np.arange(NA_QROWS)[:, None]
    i = np.arange(NA_KROWS)[None, :]
    pats = []
    for r_base in (0, NA_QROWS, rows - NA_QROWS):
        r = r_base + a
        key_row = r_base - NA_WIN_H // 2 + i
        r0 = np.clip(r - NA_WIN_H // 2, 0, rows - NA_WIN_H)
        ok = (key_row >= r0) & (key_row < r0 + NA_WIN_H) & (key_row >= 0) & (key_row < rows)
        dr = np.where(ok, key_row - r + (NA_WIN_H - 1), 2 * NA_WIN_H - 1)
        pats.append(dr)
    dr_all = np.stack(pats)
    pairs = dr_all.reshape(-1, 2)
    uniq, inv = np.unique(pairs, axis=0, return_inverse=True)
    pair_blocks = jnp.concatenate([e[:, uniq[:, 0]], e[:, uniq[:, 1]]], axis=-1)
    t = pair_blocks[:, inv.reshape(-1)]
    return t.reshape(NA_HEADS, 3, NA_QROWS * NA_KROWS // 2, GRID_W, 2 * GRID_W)


def _outproj_kernel(x_ref, xc_ref, oa_ref, ob_ref, oc_ref, wa_ref, wb_ref, wc_ref, mod_ref, gain_ref,
                    wrh_ref, wrl_ref, br_ref, tri_ref, x1_ref, tok_ref, route_ref, cnt_ref, run_ref,
                    *, region, group_batches, n_lat_tiles):
    mod = mod_ref[...]
    y = _dot(oa_ref[...], wa_ref[...]) + _dot(ob_ref[...], wb_ref[...]) + _dot(oc_ref[...], wc_ref[...])
    x1 = jnp.where(pl.program_id(1) == n_lat_tiles, xc_ref[...], x_ref[...]) + mod[2:3] * y
    x1_ref[...] = x1
    ms = jnp.mean(x1 * x1, axis=-1, keepdims=True)
    t = (x1 * lax.rsqrt(ms + EPS)) * gain_ref[...]
    t = t * (1.0 + mod[4:5]) + mod[3:4]
    tok_ref[...] = _pack_bf16_pairs(t)

    t_hi, t_lo = _split_bf16(t)
    wrh = wrh_ref[...]
    logits = _dot(t_hi, wrh) + _dot(t_lo, wrh) + _dot(t_hi, wrl_ref[...]) + br_ref[...]

    lt = logits.T[:ROUTE_ROWS]
    row = lax.broadcasted_iota(jnp.int32, lt.shape, 0)
    row_f = row.astype(F32)
    is_g = row < N_GROUPS
    gl = jnp.where(is_g, lt, NEG)
    gmax = jnp.max(gl, axis=0, keepdims=True)
    g_sel = jnp.min(jnp.where(gl == gmax, row_f, 1e9), axis=0, keepdims=True)
    p_grp = 1.0 / jnp.sum(jnp.where(is_g, jnp.exp(gl - gmax), 0.0), axis=0, keepdims=True)
    grp_of_row = lax.shift_right_arithmetic(row - N_GROUPS, 2).astype(F32)
    in_grp = (row >= N_GROUPS) & (row < N_GROUPS + N_EXPERTS) & (grp_of_row == g_sel)
    el = jnp.where(in_grp, lt, NEG)
    v1 = jnp.max(el, axis=0, keepdims=True)
    i1 = jnp.min(jnp.where(el == v1, row_f, 1e9), axis=0, keepdims=True)
    el2 = jnp.where(row_f == i1, NEG, el)
    v2 = jnp.max(el2, axis=0, keepdims=True)
    i2 = jnp.min(jnp.where(el2 == v2, row_f, 1e9), axis=0, keepdims=True)
    e2 = jnp.exp(v2 - v1)
    den = 1.0 + e2
    w1 = p_grp / den
    w2 = p_grp * e2 / den

    @pl.when((lax.rem(pl.program_id(0), group_batches) == 0) & (pl.program_id(1) == 0))
    def _():
        run_ref[...] = jnp.zeros(run_ref.shape, F32)

    ind = jnp.where(row_f == i1, 1.0, 0.0) + jnp.where(row_f == i2, 1.0, 0.0)
    rank = _dot(ind.astype(BF16), tri_ref[...]) + run_ref[:, 0:1]

    def pick(m, r):
        return jnp.sum(jnp.where(row_f == r, m, 0.0), axis=0, keepdims=True)

    pos1 = (i1 - N_GROUPS) * region + pick(rank, i1)
    pos2 = (i2 - N_GROUPS) * region + pick(rank, i2)
    r = lax.broadcasted_iota(jnp.int32, (LANES, lt.shape[1]), 0)
    record = jnp.where(r == 0, pos1, jnp.where(r == 1, pos2, jnp.where(r == 2, w1, jnp.where(r == 3, w2, 0.0))))
    route_ref[...] = record.T
    run = run_ref[...] + jnp.sum(ind, axis=1, keepdims=True)
    run_ref[...] = run
    cnt_ref[...] = run


def _out_projection(x_lat, x_ctx, ctx_blk, oa, ob, oc, wa, wb, wc, modsel, gain, wrh, wrl, br, n_tiles,
                    n_lat_tiles):
    b, _, d = x_lat.shape
    tok = lambda bi, ti: (bi, ti, 0)
    const2 = lambda bi, ti: (0, 0)
    rows = n_tiles * TILE
    nb = b // MOE_GROUPS if b % MOE_GROUPS == 0 else b
    tri = jnp.asarray(np.triu(np.ones((TILE, TILE), np.float32), 1), BF16)
    return pl.pallas_call(
        functools.partial(_outproj_kernel, region=nb * rows, group_batches=nb, n_lat_tiles=n_lat_tiles),
        out_shape=[jax.ShapeDtypeStruct((b, rows, d), F32),
                   jax.ShapeDtypeStruct((b, rows, d // 2), jnp.int32),
                   jax.ShapeDtypeStruct((b, rows, LANES), F32),
                   jax.ShapeDtypeStruct((ROUTE_ROWS * (b // nb), LANES), F32)],
        grid=(b, n_tiles),
        in_specs=_token_specs(d, n_lat_tiles, ctx_blk) + [
                  pl.BlockSpec((None, TILE, W_A), tok),
                  pl.BlockSpec((None, TILE, W_B), tok),
                  pl.BlockSpec((None, TILE, W_C), tok),
                  pl.BlockSpec((W_A, d), const2),
                  pl.BlockSpec((W_B, d), const2),
                  pl.BlockSpec((W_C, d), const2),
                  pl.BlockSpec((None, 6, d), lambda bi, ti: (2 * bi + (ti >= n_lat_tiles).astype(jnp.int32), 0, 0)),
                  pl.BlockSpec((1, d), const2),
                  pl.BlockSpec((d, LANES), const2),
                  pl.BlockSpec((d, LANES), const2),
                  pl.BlockSpec((1, LANES), const2),
                  pl.BlockSpec((TILE, TILE), const2)],
        out_specs=[pl.BlockSpec((None, TILE, d), tok),
                   pl.BlockSpec((None, TILE, d // 2), tok),
                   pl.BlockSpec((None, TILE, LANES), tok),
                   pl.BlockSpec((ROUTE_ROWS, LANES), lambda bi, ti: (bi // nb, 0))],
        scratch_shapes=[pltpu.VMEM((ROUTE_ROWS, LANES), F32)],
        compiler_params=_cparams(2),
        name="out_projection",
    )(x_lat, x_ctx, oa, ob, oc, wa, wb, wc, modsel, gain, wrh, wrl, br, tri)


def _sc_mesh():
    return plsc.VectorSubcoreMesh(core_axis_name="core", subcore_axis_name="subcore")


def _sc_worker_base(per_worker):
    wid = lax.axis_index("subcore") * SC_CORES + lax.axis_index("core")
    return wid * per_worker


def _sc_scratch(d, dtype):
    return ([pltpu.VMEM((SC_ROWS,), jnp.int32)] * SC_BUFS + [pltpu.VMEM((SC_ROWS, d), dtype)] * SC_BUFS
            + [pltpu.SemaphoreType.DMA] * (2 * SC_BUFS))


def _sc_split(scratch):
    return (scratch[:SC_BUFS], scratch[SC_BUFS:2 * SC_BUFS], scratch[2 * SC_BUFS:3 * SC_BUFS],
            scratch[3 * SC_BUFS:])


def _sc_chunk_loop(per_worker, group):
    chunks = per_worker // SC_ROWS
    full = chunks // SC_BUFS * SC_BUFS

    @pl.loop(0, full, step=SC_BUFS)
    def _(c):
        group(c, SC_BUFS)

    if chunks > full:
        group(full, chunks - full)


def _sc_scatter_rows(x, row_off, n, idx, n_out):
    d = x.shape[1]
    per_worker = 2 * n // (SC_CORES * SC_SUBCORES)
    assert per_worker % SC_ROWS == 0 and n % SC_ROWS == 0

    @functools.partial(pl.kernel, out_type=jax.ShapeDtypeStruct((n_out, d), x.dtype),
                       mesh=_sc_mesh(), scratch_types=_sc_scratch(d, x.dtype))
    def scatter(x_hbm, i_hbm, o_hbm, *scratch):
        idx_v, rows_v, sem_in, sem_out = _sc_split(scratch)
        base = _sc_worker_base(per_worker)

        def group(c, n_bufs):
            reads = []
            for u in range(n_bufs):
                a = pl.multiple_of(base + (c + u) * SC_ROWS, SC_ROWS)
                t = pl.multiple_of(row_off + lax.rem(a, n), SC_ROWS)
                pltpu.sync_copy(i_hbm.at[pl.ds(a, SC_ROWS)], idx_v[u])
                reads.append(pltpu.async_copy(x_hbm.at[pl.ds(t, SC_ROWS)], rows_v[u], sem_in[u]))
            writes = []
            for u in range(n_bufs):
                reads[u].wait()
                writes.append(pltpu.async_copy(rows_v[u], o_hbm.at[idx_v[u]], sem_out[u]))
            for w in writes:
                w.wait()

        _sc_chunk_loop(per_worker, group)

    return scatter(x, idx)


def _sc_gather_rows(src, idx):
    m = idx.shape[0]
    d = src.shape[1]
    per_worker = m // (SC_CORES * SC_SUBCORES)
    assert per_worker % SC_ROWS == 0

    @functools.partial(pl.kernel, out_type=jax.ShapeDtypeStruct((m, d), src.dtype),
                       mesh=_sc_mesh(), scratch_types=_sc_scratch(d, src.dtype))
    def gather(s_hbm, i_hbm, o_hbm, *scratch):
        idx_v, rows_v, sem_in, sem_out = _sc_split(scratch)
        base = _sc_worker_base(per_worker)

        def group(c, n_bufs):
            offs, reads = [], []
            for u in range(n_bufs):
                a = pl.multiple_of(base + (c + u) * SC_ROWS, SC_ROWS)
                offs.append(a)
                pltpu.sync_copy(i_hbm.at[pl.ds(a, SC_ROWS)], idx_v[u])
                reads.append(pltpu.async_copy(s_hbm.at[idx_v[u]], rows_v[u], sem_in[u]))
            writes = []
            for u in range(n_bufs):
                reads[u].wait()
                writes.append(pltpu.async_copy(rows_v[u], o_hbm.at[pl.ds(offs[u], SC_ROWS)], sem_out[u]))
            for w in writes:
                w.wait()

        _sc_chunk_loop(per_worker, group)

    return gather(src, idx)


def _expert_ffn_kernel(blk_ref, exp_ref, x_ref, wg_ref, wu_ref, wd_ref, y_ref, wgb_ref, wub_ref, wdb_ref):
    j = pl.program_id(0)

    @pl.when((j == 0) | (exp_ref[j] != exp_ref[jnp.maximum(j - 1, 0)]))
    def _():
        wgb_ref[...] = wg_ref[...].astype(BF16)
        wub_ref[...] = wu_ref[...].astype(BF16)
        wdb_ref[...] = wd_ref[...].astype(BF16)

    @pl.when((j == 0) | (blk_ref[j] != blk_ref[jnp.maximum(j - 1, 0)]))
    def _():
        x = _unpack_bf16_pairs(x_ref[...]).astype(BF16)
        hid = jax.nn.silu(_dot(x, wgb_ref[...])) * _dot(x, wub_ref[...])
        y_ref[...] = _pack_bf16_pairs(_dot(hid.astype(BF16), wdb_ref[...]))


def _expert_ffn(xs, blk, exp, wg, wu, wd, layer):
    rows, d_packed = xs.shape
    d = 2 * d_packed
    w_map = lambda j, blk, exp: (layer, exp[j], 0, 0)
    return pl.pallas_call(
        _expert_ffn_kernel,
        out_shape=jax.ShapeDtypeStruct((rows, d_packed), jnp.int32),
        grid_spec=pltpu.PrefetchScalarGridSpec(
            num_scalar_prefetch=2,
            grid=(blk.shape[0],),
            in_specs=[pl.BlockSpec((MOE_TILE, d_packed), lambda j, blk, exp: (blk[j], 0)),
                      pl.BlockSpec((None, None, d, EXPERT_HIDDEN), w_map),
                      pl.BlockSpec((None, None, d, EXPERT_HIDDEN), w_map),
                      pl.BlockSpec((None, None, EXPERT_HIDDEN, d), w_map)],
            out_specs=pl.BlockSpec((MOE_TILE, d_packed), lambda j, blk, exp: (blk[j], 0)),
            scratch_shapes=[pltpu.VMEM((d, EXPERT_HIDDEN), BF16),
                            pltpu.VMEM((d, EXPERT_HIDDEN), BF16),
                            pltpu.VMEM((EXPERT_HIDDEN, d), BF16)]),
        compiler_params=_cparams(1),
        name="expert_ffn",
    )(blk, exp, xs, wg, wu, wd)


def _combine_kernel(x1_ref, y1_ref, y2_ref, route_ref, mod_ref, fgain_ref, *rest, final):
    o_ref = rest[-1]
    route = route_ref[...]
    y = route[:, 2:3] * _unpack_bf16_pairs(y1_ref[...]) + route[:, 3:4] * _unpack_bf16_pairs(y2_ref[...])
    x2 = x1_ref[...] + mod_ref[5:6, :] * y
    if final:
        ms = jnp.mean(x2 * x2, axis=-1, keepdims=True)
        x2 = (x2 * lax.rsqrt(ms + EPS)) * fgain_ref[...]
    o_ref[...] = x2


def _combine(x1, ys, route, modsel, fgain, prev, b0, nb, n_lat_tiles, final):
    b, rows, d = x1.shape
    n_t = rows // TILE
    tok = lambda bi, ti: (b0 + bi, ti, 0)
    in_specs = [pl.BlockSpec((None, TILE, d), tok),
                pl.BlockSpec((TILE, d // 2), lambda bi, ti: (bi * n_t + ti, 0)),
                pl.BlockSpec((TILE, d // 2), lambda bi, ti: ((nb + bi) * n_t + ti, 0)),
                pl.BlockSpec((None, TILE, LANES), tok),
                pl.BlockSpec((None, 6, d),
                             lambda bi, ti: (2 * (b0 + bi) + (ti >= n_lat_tiles).astype(jnp.int32), 0, 0)),
                pl.BlockSpec((1, d), lambda bi, ti: (0, 0))]
    args = [x1, ys, ys, route, modsel, fgain]
    aliases = {}
    if prev is not None:
        in_specs.append(pl.BlockSpec(memory_space=pl.ANY))
        args.append(prev)
        aliases = {len(args) - 1: 0}
    return pl.pallas_call(
        functools.partial(_combine_kernel, final=final),
        out_shape=jax.ShapeDtypeStruct((b, rows, d), F32),
        grid=(nb, n_t),
        in_specs=in_specs,
        out_specs=pl.BlockSpec((None, TILE, d), tok),
        input_output_aliases=aliases,
        compiler_params=_cparams(2),
        name="moe_combine",
    )(*args)


def _routed_moe(tok, route, cnt, x1, wg, wu, wd, layer, modsel, fgain, n_lat_tiles, final):
    b, rows, d = x1.shape
    n_groups = cnt.shape[0] // ROUTE_ROWS
    nb = b // n_groups
    n = nb * rows
    flat = route.reshape(b * rows, LANES)
    tok_flat = tok.reshape(b * rows, tok.shape[2])
    out = None
    for g in range(n_groups):
        part = flat[g * n:(g + 1) * n]
        idx = jnp.concatenate([part[:, 0], part[:, 1]]).astype(jnp.int32)
        xs = _sc_scatter_rows(tok_flat, g * n, n, idx, N_EXPERTS * n)

        e0 = ROUTE_ROWS * g + N_GROUPS
        counts = cnt[e0:e0 + N_EXPERTS, 0].astype(jnp.int32)
        tiles = (counts + MOE_TILE - 1) // MOE_TILE
        ends = jnp.cumsum(tiles)
        n_sched = 2 * n // MOE_TILE + N_EXPERTS
        j = jnp.minimum(jnp.arange(n_sched, dtype=jnp.int32), ends[-1] - 1)
        exp = jnp.sum((j[:, None] >= ends[None, :]).astype(jnp.int32), axis=1)
        blk = exp * (n // MOE_TILE) + j - (ends - tiles)[exp]

        ys = _expert_ffn(xs, blk, exp, wg, wu, wd, layer)
        yg = _sc_gather_rows(ys, idx)
        out = _combine(x1, yg, route, modsel, fgain, out, g * nb, nb, n_lat_tiles, final)
    return out


def _rope_tables(n_lat):
    t = jnp.arange(n_lat)
    row = (t // GRID_W).astype(F32)
    col = (t % GRID_W).astype(F32)

    def cs(dim):
        quarter = dim // 4
        freqs = ROPE_THETA ** (-jnp.arange(quarter, dtype=F32) / quarter)
        ang = jnp.concatenate([row[:, None] * freqs, col[:, None] * freqs], axis=-1)
        cos = jnp.tile(jnp.cos(ang), (1, 2 * LANES // dim))
        sin = jnp.tile(jnp.sin(ang), (1, 2 * LANES // dim))
        cos = jnp.concatenate([cos, jnp.ones((CTX_LEN, LANES), F32)], axis=0)
        sin = jnp.concatenate([sin, jnp.zeros((CTX_LEN, LANES), F32)], axis=0)
        return cos, sin

    cos_b, sin_b = cs(DIFF_QK_DIM)
    cos_c, sin_c = cs(HEAD_DIM)
    return jnp.concatenate([cos_b, sin_b, cos_c, sin_c], axis=1)


def _reordered_w_in(w_in):
    o_c = 3 * W_A + 3 * W_B
    heads = [w_in[:, o_c + h * HEAD_DIM:o_c + (h + 1) * HEAD_DIM] for h in GQA_Q_ORDER]
    return jnp.concatenate([w_in[:, :o_c]] + heads + [w_in[:, o_c + W_C:]], axis=1).astype(BF16)


def kernel(x, c, ctx, c_ctx, w_mod, b_mod, norm_attn, norm_ffn, w_in, w_out, na_rpb, diff_lambda_q1, diff_lambda_k1, diff_lambda_q2, diff_lambda_k2, diff_subln, gqa_q_norm, gqa_k_norm, router_group_w, router_group_b, router_expert_w, router_expert_b, w_gate, w_up, w_down, final_norm):
    b, s, d = x.shape
    assert d == D_MODEL and ctx.shape[1] == CTX_LEN and s % (NA_QROWS * GRID_W) == 0
    rows = s // GRID_W
    assert rows >= 2 * NA_QROWS
    t_all = s + CTX_LEN
    n_lat_tiles = s // TILE

    assert b + 1 <= SUBLANES
    c_rows = jnp.zeros((SUBLANES, d), F32).at[:b].set(c).at[b].set(c_ctx)
    mod = _modulation(c_rows, w_mod, b_mod)

    tab = _rope_tables(s)
    hidx = np.arange(HEAD_DIM)
    partner = np.where(hidx < HEAD_DIM // 2, hidx + HEAD_DIM // 2, hidx - HEAD_DIM // 2)
    blk = np.arange(W_C) // HEAD_DIM
    ones = jnp.asarray((blk[:, None] == blk[None, :]).astype(np.float32), BF16)
    dummy_aux = jnp.zeros((SUBLANES, LANES), F32)

    x_lat, x_ctx, ctx_blk = x, ctx, 0
    for l in range(DEPTH):
        ctx_out = l < DEPTH - 1
        lam_init = 0.8 - 0.6 * math.exp(-0.3 * l)
        m_lat = mod[l, :b].reshape(b, 1, 6, d)
        m_ctx = jnp.broadcast_to(mod[l, b].reshape(1, 1, 6, d), (b, 1, 6, d))
        modsel = jnp.concatenate([m_lat, m_ctx], axis=1).reshape(2 * b, 6, d)

        gq = jnp.stack([jnp.tile(gqa_q_norm[l], GQA_Q_HEADS), jnp.tile(gqa_q_norm[l][partner], GQA_Q_HEADS)])
        gk = jnp.stack([jnp.tile(gqa_k_norm[l], GQA_KV_HEADS), jnp.tile(gqa_k_norm[l][partner], GQA_KV_HEADS)])
        qa, ka, va, qb, kb, vb, qc, kc, vc = _in_projection(
            x_lat, x_ctx, ctx_blk, modsel, norm_attn[l][None], _reordered_w_in(w_in[l]), tab, gq, gk, ones,
            n_lat_tiles)

        n_qt = n_lat_tiles + 1 if ctx_out else n_lat_tiles
        pad = lambda v: jnp.pad(v, (0, LANES - v.shape[0]))
        aux = jnp.stack([pad(diff_lambda_q1[l]), pad(diff_lambda_k1[l]), pad(diff_lambda_q2[l]),
                         pad(diff_lambda_k2[l]), jnp.tile(diff_subln[l], 2),
                         jnp.zeros((LANES,), F32), jnp.zeros((LANES,), F32), jnp.zeros((LANES,), F32)])
        group_a = dict(n_qblk=1, n_sub=2, n_hp=NA_HEADS // 2, n_lat=s, mode="plain")
        group_b = dict(n_qblk=1, n_sub=4, n_hp=DIFF_HEADS // 2, n_lat=s, mode="diff", lam_init=lam_init)
        group_c = dict(n_qblk=3, n_sub=2, n_hp=1, n_lat=s, mode="plain")
        oa = _neighbourhood_attention(qa, ka, va, _na_bias_table(na_rpb[l], rows), s)
        ob = _flash(qb, kb, vb, aux, queries="latent", **group_b)
        oc = _flash(qc, kc, vc, dummy_aux, queries="latent", **group_c)
        if ctx_out:
            oa = jnp.concatenate([oa, _flash(qa, ka, va, dummy_aux, queries="context", **group_a)], axis=1)
            ob = jnp.concatenate([ob, _flash(qb, kb, vb, aux, queries="context", **group_b)], axis=1)
            oc = jnp.concatenate([oc, _flash(qc, kc, vc, dummy_aux, queries="context", **group_c)], axis=1)

        w_o = w_out[l]
        o_c = W_A + W_B
        w_oc = jnp.concatenate([w_o[o_c + h * HEAD_DIM:o_c + (h + 1) * HEAD_DIM] for h in GQA_Q_ORDER], axis=0)
        wr = jnp.zeros((d, LANES), F32)
        wr = wr.at[:, :N_GROUPS].set(router_group_w[l]).at[:, N_GROUPS:N_GROUPS + N_EXPERTS].set(router_expert_w[l])
        wrh, wrl = _split_bf16(wr)
        br = jnp.zeros((1, LANES), F32)
        br = br.at[0, :N_GROUPS].set(router_group_b[l]).at[0, N_GROUPS:N_GROUPS + N_EXPERTS].set(router_expert_b[l])
        x1, tok, route, cnt = _out_projection(
            x_lat, x_ctx, ctx_blk, oa, ob, oc, w_o[:W_A].astype(BF16), w_o[W_A:W_A + W_B].astype(BF16),
            w_oc.astype(BF16), modsel, norm_ffn[l][None], wrh, wrl, br, n_qt, n_lat_tiles)
        xs = _routed_moe(tok, route, cnt, x1, w_gate, w_up, w_down, l, modsel, final_norm[None],
                         n_lat_tiles, final=not ctx_out)
        x_lat, x_ctx, ctx_blk = xs, xs, n_lat_tiles
    return xs
```

```python
import functools
import math

import numpy as np
import jax
import jax.numpy as jnp
from jax import lax
from jax.experimental import pallas as pl
from jax.experimental.pallas import tpu as pltpu
from jax.experimental.pallas import tpu_sc as plsc

F32 = jnp.float32
BF16 = jnp.bfloat16

D_MODEL = 1024
DEPTH = 2
GRID_W = 64
CTX_LEN = 256
HEAD_DIM = 64
NA_HEADS = 6
NA_WIN_H = 8
NA_WIN_W = 16
DIFF_HEADS = 4
DIFF_QK_DIM = 32
GQA_Q_HEADS = 6
GQA_KV_HEADS = 2
N_GROUPS = 4
EXPERTS_PER_GROUP = 4
N_EXPERTS = 16
EXPERT_HIDDEN = 512
ROPE_THETA = 10000.0
EPS = 1e-6
W_A = NA_HEADS * HEAD_DIM
W_B = DIFF_HEADS * 2 * DIFF_QK_DIM
W_C = GQA_Q_HEADS * HEAD_DIM
W_KC = GQA_KV_HEADS * HEAD_DIM
IN_WIDTH = 3 * W_A + 3 * W_B + W_C + 2 * W_KC

LANES = 128
TILE = CTX_LEN
NA_QROWS = 8
NA_KROWS = 16
NA_PARTS = 2
NEG = -1e30
LOG2E = 1.4426950408889634
HI16 = -65536
VMEM_LIMIT = 56 * 1024 * 1024
FLASH_TK = 512
PAIRS_PER_STEP = 2
FLASH_UNROLL_MAX_ROWS = 1536
SUBLANES = 8
ROUTE_ROWS = 32
MOE_TILE = 512
MOE_GROUPS = 2
SC_ROWS = 32
SC_BUFS = 4
SC_CORES = 2
SC_SUBCORES = 16

GQA_Q_ORDER = (0, 3, 1, 4, 2, 5)


def _cparams(n_axes):
    return pltpu.CompilerParams(dimension_semantics=("arbitrary",) * n_axes,
                                vmem_limit_bytes=VMEM_LIMIT)


def _split_bf16(a):
    hi = a.astype(BF16)
    lo = (a - hi.astype(F32)).astype(BF16)
    return hi, lo


def _dot(a, b):
    return jnp.dot(a, b, preferred_element_type=F32)


def _pack_bf16_pairs(t):
    bits = lax.bitcast_convert_type(t.astype(BF16).astype(F32), jnp.int32)
    half_d = bits.shape[1] // 2
    return lax.shift_right_logical(bits[:, :half_d], 16) | (bits[:, half_d:] & HI16)


def _unpack_bf16_pairs(w):
    return jnp.concatenate([lax.bitcast_convert_type(lax.shift_left(w, 16), F32),
                            lax.bitcast_convert_type(w & HI16, F32)], axis=1)


def _dot_nt(a, b):
    return lax.dot_general(a, b, (((1,), (1,)), ((), ())), preferred_element_type=F32)


def _mod_kernel(c_ref, w_ref, b_ref, o_ref):
    c = c_ref[...]
    a = c * jax.nn.sigmoid(c)
    a_hi, a_lo = _split_bf16(a)
    w_hi, w_lo = _split_bf16(w_ref[...])
    o_ref[...] = _dot(a_hi, w_hi) + _dot(a_lo, w_hi) + _dot(a_hi, w_lo) + b_ref[...]


def _modulation(c_rows, w_mod, b_mod):
    depth, d, n = w_mod.shape
    bn = 1536
    return pl.pallas_call(
        _mod_kernel,
        out_shape=jax.ShapeDtypeStruct((depth, SUBLANES, n), F32),
        grid=(depth, n // bn),
        in_specs=[pl.BlockSpec((SUBLANES, d), lambda l, j: (0, 0)),
                  pl.BlockSpec((None, d, bn), lambda l, j: (l, 0, j)),
                  pl.BlockSpec((None, 1, bn), lambda l, j: (l, 0, j))],
        out_specs=pl.BlockSpec((None, SUBLANES, bn), lambda l, j: (l, 0, j)),
        compiler_params=_cparams(2),
        name="adaln_mod",
    )(c_rows, w_mod, b_mod.reshape(depth, 1, n))


def _head_mean_sq(t, ones):
    hi, lo = _split_bf16(t * t)
    return (_dot(hi, ones) + _dot(lo, ones)) * (1.0 / HEAD_DIM)


def _rotate_half(p, head):
    w = p.shape[1]
    half = head // 2
    lane = lax.broadcasted_iota(jnp.int32, (1, w), 1)
    first = (lane & (head - 1)) < half
    from_right = pltpu.roll(p, w - half, 1)
    from_left = pltpu.roll(p, half, 1)
    return jnp.where(first, -from_right, from_left)


def _inproj_kernel(x_ref, xc_ref, mod_ref, gain_ref, w_ref, tab_ref, gq_ref, gk_ref, ones_ref,
                   qa_ref, ka_ref, va_ref, qb_ref, kb_ref, vb_ref, qc_ref, kc_ref, vc_ref,
                   *, n_lat_tiles):
    x = jnp.where(pl.program_id(1) == n_lat_tiles, xc_ref[...], x_ref[...])
    mod = mod_ref[...]
    ms = jnp.mean(x * x, axis=-1, keepdims=True)
    h = (x * lax.rsqrt(ms + EPS)) * gain_ref[...]
    h = h * (1.0 + mod[1:2]) + mod[0:1]
    hb = h.astype(BF16)

    def proj(a, b):
        return _dot(hb, w_ref[:, a:b])

    pa = proj(0, 3 * W_A)
    qa_ref[...] = (pa[:, :W_A] * (HEAD_DIM ** -0.5 * LOG2E)).astype(BF16)
    ka_ref[...] = pa[:, W_A:2 * W_A].astype(BF16)
    va_ref[...] = pa[:, 2 * W_A:].astype(BF16)

    tab = tab_ref[...]
    cos_b = jnp.concatenate([tab[:, 0:LANES]] * 2, axis=1)
    sin_b = jnp.concatenate([tab[:, LANES:2 * LANES]] * 2, axis=1)
    cos_c1 = tab[:, 2 * LANES:3 * LANES]
    sin_c1 = tab[:, 3 * LANES:4 * LANES]
    cos_c = jnp.concatenate([cos_c1] * 3, axis=1)
    sin_c = jnp.concatenate([sin_c1] * 3, axis=1)

    o_b = 3 * W_A
    pb = proj(o_b, o_b + 3 * W_B)
    qb = pb[:, :W_B]
    kb = pb[:, W_B:2 * W_B]
    qb = qb * cos_b + _rotate_half(qb, DIFF_QK_DIM) * sin_b
    qb_ref[...] = (qb * (DIFF_QK_DIM ** -0.5 * LOG2E)).astype(BF16)
    kb_ref[...] = (kb * cos_b + _rotate_half(kb, DIFF_QK_DIM) * sin_b).astype(BF16)
    vb_ref[...] = pb[:, 2 * W_B:].astype(BF16)

    o_c = o_b + 3 * W_B
    pc = proj(o_c, IN_WIDTH)
    ones = ones_ref[...]
    qc = pc[:, :W_C]
    kc = pc[:, W_C:W_C + W_KC]
    nq = lax.rsqrt(_head_mean_sq(qc, ones) + EPS)
    nk = lax.rsqrt(_head_mean_sq(kc, ones[:W_KC, :W_KC]) + EPS)
    gq = gq_ref[...]
    gk = gk_ref[...]
    q = nq * (qc * gq[0:1] * cos_c + _rotate_half(qc, HEAD_DIM) * gq[1:2] * sin_c)
    qc_ref[...] = (q * (HEAD_DIM ** -0.5 * LOG2E)).astype(BF16)
    k = nk * (kc * gk[0:1] * cos_c1 + _rotate_half(kc, HEAD_DIM) * gk[1:2] * sin_c1)
    kc_ref[...] = k.astype(BF16)
    vc_ref[...] = pc[:, W_C + W_KC:].astype(BF16)


def _token_specs(d, n_lat_tiles, ctx_blk):
    return [pl.BlockSpec((None, TILE, d), lambda bi, ti: (bi, jnp.minimum(ti, n_lat_tiles - 1), 0)),
            pl.BlockSpec((None, TILE, d), lambda bi, ti: (bi, ctx_blk, 0))]


def _in_projection(x_lat, x_ctx, ctx_blk, modsel, gain, w_ext, tab, gq, gk, ones, n_lat_tiles):
    b, _, d = x_lat.shape
    n_tiles = n_lat_tiles + 1
    t_all = n_tiles * TILE
    widths = (W_A, W_A, W_A, W_B, W_B, W_B, W_C, W_KC, W_KC)
    tok = lambda bi, ti: (bi, ti, 0)
    const2 = lambda bi, ti: (0, 0)
    return pl.pallas_call(
        functools.partial(_inproj_kernel, n_lat_tiles=n_lat_tiles),
        out_shape=[jax.ShapeDtypeStruct((b, t_all, w), BF16) for w in widths],
        grid=(b, n_tiles),
        in_specs=_token_specs(d, n_lat_tiles, ctx_blk) + [
                  pl.BlockSpec((None, 6, d), lambda bi, ti: (2 * bi + (ti >= n_lat_tiles).astype(jnp.int32), 0, 0)),
                  pl.BlockSpec((1, d), const2),
                  pl.BlockSpec((d, IN_WIDTH), const2),
                  pl.BlockSpec((TILE, 4 * LANES), lambda bi, ti: (ti, 0)),
                  pl.BlockSpec((2, W_C), const2),
                  pl.BlockSpec((2, W_KC), const2),
                  pl.BlockSpec((W_C, W_C), const2)],
        out_specs=[pl.BlockSpec((None, TILE, w), tok) for w in widths],
        compiler_params=_cparams(2),
        name="in_projection",
    )(x_lat, x_ctx, modsel, gain, w_ext, tab, gq, gk, ones)


def _flash_kernel(q_ref, k_ref, v_ref, aux_ref, o_ref, va_ref, vb_ref, qs_ref, acc_ref, m_ref,
                  s0_ref, s1_ref, mb0_ref, mb1_ref, *,
                  n_qblk, n_sub, tk, n_lat_blocks, pairs_per_step, ctx_start, queries, mode, lam_init):
    sub_w = LANES // n_sub
    half = LANES // 2
    lane = lax.broadcasted_iota(jnp.int32, (1, LANES), 1)
    lower = lane < half
    n_pieces = n_qblk * n_sub
    ma = (n_pieces // 2) * TILE
    m_rows = n_pieces * TILE

    @pl.when(pl.program_id(2) == 0)
    def _():
        v = v_ref[...].astype(F32)
        va_ref[...] = jnp.where(lower, v, 1.0).astype(BF16)
        vb_ref[...] = jnp.where(lower, 1.0, v).astype(BF16)

    ia, ib = 0, n_pieces // 2
    for blk in range(n_qblk):
        qf = q_ref[:, blk * LANES:(blk + 1) * LANES].astype(F32)
        for sub in range(n_sub):
            msk = (lane >= sub * sub_w) & (lane < (sub + 1) * sub_w)
            piece = jnp.where(msk, qf, 0.0).astype(BF16)
            if sub * sub_w < half:
                qs_ref[ia * TILE:(ia + 1) * TILE, :] = piece
                ia += 1
            else:
                qs_ref[ib * TILE:(ib + 1) * TILE, :] = piece
                ib += 1

    s_bufs = (s0_ref, s1_ref)
    mb_bufs = (mb0_ref, mb1_ref)

    def scores(start, size, slot):
        s = _dot_nt(qs_ref[...], k_ref[pl.ds(start, size), :])
        s_bufs[slot][:, :size] = s
        mb = jnp.max(s, axis=-1, keepdims=True)
        mb_bufs[slot][...] = jnp.broadcast_to(mb, (m_rows, LANES))

    def accumulate(start, size, slot, first):
        mb = mb_bufs[slot][...]
        if first:
            m_new = mb
        else:
            m_old = m_ref[...]
            m_new = jnp.maximum(m_old, mb)
        s_ref = s_bufs[slot]
        cols = [s_ref[:, c * LANES:(c + 1) * LANES] - m_new for c in range(size // LANES)]
        p = jnp.concatenate([jnp.exp2(d.astype(BF16)) for d in cols], axis=1)
        pva = _dot(p[:ma], va_ref[pl.ds(start, size), :])
        pvb = _dot(p[ma:], vb_ref[pl.ds(start, size), :])
        if first:
            acc_ref[:ma, :] = pva
            acc_ref[ma:, :] = pvb
        else:
            alpha = jnp.exp2(m_old - m_new)
            acc_ref[:ma, :] = alpha[:ma] * acc_ref[:ma, :] + pva
            acc_ref[ma:, :] = alpha[ma:] * acc_ref[ma:, :] + pvb
        m_ref[...] = m_new

    def lat(j):
        return pl.multiple_of(j * tk, tk)

    def latent_queries():
        scores(ctx_start, CTX_LEN, 0)
        scores(lat(0), tk, 1)
        accumulate(ctx_start, CTX_LEN, 0, True)

        def pair(i):
            scores(lat(2 * i + 1), tk, 0)
            accumulate(lat(2 * i), tk, 1, False)
            scores(lat(2 * i + 2), tk, 1)
            accumulate(lat(2 * i + 1), tk, 0, False)

        def body(i, carry):
            for u in range(pairs_per_step):
                pair(i * pairs_per_step + u)
            return carry

        n_pairs = (n_lat_blocks - 2) // 2
        n_steps = n_pairs // pairs_per_step
        lax.fori_loop(0, n_steps, body, 0)
        for i in range(n_steps * pairs_per_step, n_pairs):
            pair(i)
        scores(lat(n_lat_blocks - 1), tk, 0)
        accumulate(lat(n_lat_blocks - 2), tk, 1, False)
        accumulate(lat(n_lat_blocks - 1), tk, 0, False)

    def context_queries():
        scores(ctx_start, CTX_LEN, 0)
        accumulate(ctx_start, CTX_LEN, 0, True)

    if queries == "latent":
        latent_queries()
    else:
        context_queries()

    acc = acc_ref[...]
    r = acc / pltpu.roll(acc, half, 1)
    ra, rb = r[:ma], r[ma:]
    if mode == "plain":
        for i in range(n_pieces // 2):
            o = jnp.where(lower, ra[i * TILE:(i + 1) * TILE], rb[i * TILE:(i + 1) * TILE])
            o_ref[:, i * LANES:(i + 1) * LANES] = o.astype(BF16)
    else:
        aux = aux_ref[...]
        l1 = jnp.sum(aux[0:1] * aux[1:2], axis=-1, keepdims=True)
        l2 = jnp.sum(aux[2:3] * aux[3:4], axis=-1, keepdims=True)
        lam = jnp.exp(l1) - jnp.exp(l2) + lam_init
        oa = ra[:TILE] - lam * ra[TILE:]
        ob = rb[:TILE] - lam * rb[TILE:]
        o = jnp.where(lower, oa, ob)
        sq = o * o
        ss_a = jnp.sum(jnp.where(lower, sq, 0.0), axis=-1, keepdims=True)
        ss_b = jnp.sum(jnp.where(lower, 0.0, sq), axis=-1, keepdims=True)
        ms = jnp.where(lower, ss_a, ss_b) * (1.0 / HEAD_DIM)
        o = (o * lax.rsqrt(ms + EPS)) * aux[4:5]
        o_ref[...] = (o * (1.0 - lam_init)).astype(BF16)


def _flash(q, k, v, aux, *, n_qblk, n_sub, n_hp, n_lat, queries, mode, lam_init=0.0):
    b, t_all, _ = q.shape
    qw = n_qblk * LANES
    tk = FLASH_TK
    assert n_lat % (2 * tk) == 0 and tk >= CTX_LEN
    m_rows = n_qblk * n_sub * TILE
    n_pairs = max((n_lat // tk - 2) // 2, 1)
    pairs = n_pairs if m_rows <= FLASH_UNROLL_MAX_ROWS else PAIRS_PER_STEP
    if queries == "latent":
        n_qt, qt_off, kv_rows, kv_blk, ctx_start = n_lat // TILE, 0, t_all, 0, n_lat
    else:
        n_qt, qt_off, kv_rows, kv_blk, ctx_start = 1, n_lat // TILE, CTX_LEN, n_lat // CTX_LEN, 0
    kern = functools.partial(_flash_kernel, n_qblk=n_qblk, n_sub=n_sub, tk=tk,
                             n_lat_blocks=n_lat // tk, ctx_start=ctx_start, queries=queries,
                             pairs_per_step=pairs, mode=mode, lam_init=lam_init)
    return pl.pallas_call(
        kern,
        out_shape=jax.ShapeDtypeStruct((b, n_qt * TILE, n_hp * qw), BF16),
        grid=(b, n_hp, n_qt),
        in_specs=[pl.BlockSpec((None, TILE, qw), lambda bi, hp, qt: (bi, qt + qt_off, hp)),
                  pl.BlockSpec((None, kv_rows, LANES), lambda bi, hp, qt: (bi, kv_blk, hp)),
                  pl.BlockSpec((None, kv_rows, LANES), lambda bi, hp, qt: (bi, kv_blk, hp)),
                  pl.BlockSpec((SUBLANES, LANES), lambda bi, hp, qt: (0, 0))],
        out_specs=pl.BlockSpec((None, TILE, qw), lambda bi, hp, qt: (bi, qt, hp)),
        scratch_shapes=[pltpu.VMEM((kv_rows, LANES), BF16),
                        pltpu.VMEM((kv_rows, LANES), BF16),
                        pltpu.VMEM((m_rows, LANES), BF16),
                        pltpu.VMEM((m_rows, LANES), F32),
                        pltpu.VMEM((m_rows, LANES), F32),
                        pltpu.VMEM((m_rows, tk), F32),
                        pltpu.VMEM((m_rows, tk), F32),
                        pltpu.VMEM((m_rows, LANES), F32),
                        pltpu.VMEM((m_rows, LANES), F32)],
        compiler_params=_cparams(3),
        name="flash_" + mode,
    )(q, k, v, aux)


def _na_kernel(q_ref, k0, k1, k2, k3, v0, v1, v2, v3, kc_ref, vc_ref, bias_ref, o_ref, s_ref, m_ref):
    lane = lax.broadcasted_iota(jnp.int32, (1, LANES), 1)
    lower = lane < LANES // 2
    n_pair = NA_KROWS // 2
    rows_per_part = NA_QROWS // NA_PARTS
    half_q = rows_per_part * GRID_W
    no_bias = jnp.zeros((GRID_W, CTX_LEN), F32)

    def head_pair(hp):
        cols = slice(hp * LANES, (hp + 1) * LANES)
        qf = q_ref[:, cols].astype(F32)
        k_all = jnp.concatenate([r[:, cols] for r in (k0, k1, k2, k3, kc_ref)], axis=0)
        v_all = jnp.concatenate([r[:, cols] for r in (v0, v1, v2, v3, vc_ref)], axis=0).astype(F32)
        v_h = [jnp.where(lower, v_all, 1.0).astype(BF16), jnp.where(lower, 1.0, v_all).astype(BF16)]
        q_h = [jnp.where(lower, qf, 0.0).astype(BF16), jnp.where(lower, 0.0, qf).astype(BF16)]
        return q_h, k_all, v_h

    pairs = [head_pair(hp) for hp in range(NA_HEADS // 2)]
    items = [(hp, part) for hp in range(NA_HEADS // 2) for part in range(NA_PARTS)]

    def scores(n):
        hp, part = items[n]
        q_h, k_all, _ = pairs[hp]
        rows = slice(part * half_q, (part + 1) * half_q)
        qs = jnp.concatenate([q_h[0][rows], q_h[1][rows]], axis=0)
        bias = jnp.concatenate(
            [jnp.concatenate([bias_ref[2 * hp + hh, a * n_pair + j] for j in range(n_pair)] + [no_bias],
                             axis=1)
             for hh in range(2) for a in range(part * rows_per_part, (part + 1) * rows_per_part)],
            axis=0)
        s = _dot_nt(qs, k_all) + bias
        s_ref[n % 2] = s
        m_ref[n % 2] = jnp.broadcast_to(jnp.max(s, axis=-1, keepdims=True), (2 * half_q, LANES))

    def finish(n):
        hp, part = items[n]
        v_h = pairs[hp][2]
        s = s_ref[n % 2]
        m = m_ref[n % 2]
        p = jnp.concatenate([jnp.exp2((s[:, c * LANES:(c + 1) * LANES] - m).astype(BF16))
                             for c in range(s.shape[1] // LANES)], axis=1)
        o0 = _dot(p[:half_q], v_h[0])
        o1 = _dot(p[half_q:], v_h[1])
        o0 = o0 / pltpu.roll(o0, LANES // 2, 1)
        o1 = o1 / pltpu.roll(o1, LANES // 2, 1)
        o_ref[part * half_q:(part + 1) * half_q, hp * LANES:(hp + 1) * LANES] = (
            jnp.where(lower, o0, o1).astype(BF16))

    scores(0)
    for n in range(1, len(items)):
        scores(n)
        finish(n - 1)
    finish(len(items) - 1)


def _neighbourhood_attention(qa, ka, va, bias, n_lat):
    b = qa.shape[0]
    q_tok = NA_QROWS * GRID_W
    v_tok = q_tok // 2
    n_rb = n_lat // q_tok
    n_view = n_lat // v_tok
    ctx_blk = n_lat // v_tok

    def view(j):
        return lambda rb, bi: (bi, jnp.clip(2 * rb - 1 + j, 0, n_view - 1), 0)

    kv_specs = [pl.BlockSpec((None, v_tok, W_A), view(j)) for j in range(4)]
    ctx_spec = pl.BlockSpec((None, CTX_LEN, W_A), lambda rb, bi: (bi, ctx_blk, 0))

    def bias_map(rb, bi):
        pat = jnp.where(rb == 0, 0, jnp.where(rb == n_rb - 1, 2, 1))
        return (0, pat, 0, 0, 0)

    part_rows = 2 * q_tok // NA_PARTS
    return pl.pallas_call(
        _na_kernel,
        out_shape=jax.ShapeDtypeStruct((b, n_lat, W_A), BF16),
        grid=(n_rb, b),
        in_specs=[pl.BlockSpec((None, q_tok, W_A), lambda rb, bi: (bi, rb, 0))]
                 + kv_specs + kv_specs + [ctx_spec, ctx_spec,
                 pl.BlockSpec((NA_HEADS, None, NA_QROWS * NA_KROWS // 2, GRID_W, 2 * GRID_W), bias_map)],
        out_specs=pl.BlockSpec((None, q_tok, W_A), lambda rb, bi: (bi, rb, 0)),
        scratch_shapes=[pltpu.VMEM((2, part_rows, NA_KROWS * GRID_W + CTX_LEN), F32),
                        pltpu.VMEM((2, part_rows, LANES), F32)],
        compiler_params=_cparams(2),
        name="neighbourhood_attention",
    )(qa, ka, ka, ka, ka, va, va, va, va, ka, va, bias)


def _na_bias_table(rpb, rows):
    cols = np.arange(GRID_W)
    c0 = np.clip(cols - NA_WIN_W // 2, 0, GRID_W - NA_WIN_W)
    cc = cols[None, :]
    col_ok = (cc >= c0[:, None]) & (cc < c0[:, None] + NA_WIN_W)
    dc = np.clip(cc - cols[:, None] + (NA_WIN_W - 1), 0, 2 * NA_WIN_W - 2)
    e = jnp.where(col_ok[None, None], (rpb.astype(F32) * LOG2E)[:, :, dc], NEG)
    e = jnp.concatenate([e, jnp.full_like(e[:, :1], NEG)], axis=1)
    a = np.arange(NA_QROWS)[:, None]
    i = np.arange(NA_KROWS)[None, :]
    pats = []
    for r_base in (0, NA_QROWS, rows - NA_QROWS):
        r = r_base + a
        key_row = r_base - NA_WIN_H // 2 + i
        r0 = np.clip(r - NA_WIN_H // 2, 0, rows - NA_WIN_H)
        ok = (key_row >= r0) & (key_row < r0 + NA_WIN_H) & (key_row >= 0) & (key_row < rows)
        dr = np.where(ok, key_row - r + (NA_WIN_H - 1), 2 * NA_WIN_H - 1)
        pats.append(dr)
    dr_all = np.stack(pats)
    pairs = dr_all.reshape(-1, 2)
    uniq, inv = np.unique(pairs, axis=0, return_inverse=True)
    pair_blocks = jnp.concatenate([e[:, uniq[:, 0]], e[:, uniq[:, 1]]], axis=-1)
    t = pair_blocks[:, inv.reshape(-1)]
    return t.reshape(NA_HEADS, 3, NA_QROWS * NA_KROWS // 2, GRID_W, 2 * GRID_W)


def _outproj_kernel(x_ref, xc_ref, oa_ref, ob_ref, oc_ref, oac_ref, obc_ref, occ_ref, wa_ref, wb_ref, wc_ref,
                    mod_ref, gain_ref, wrh_ref, wrl_ref, br_ref, tri_ref,
                    x1_ref, tok_ref, route_ref, cnt_ref, run_ref, *, region, group_batches, n_lat_tiles):
    mod = mod_ref[...]
    is_ctx = pl.program_id(1) == n_lat_tiles
    def tile(lat_ref, ctx_ref):
        return jnp.where(is_ctx, ctx_ref[...], lat_ref[...])

    y = (_dot(tile(oa_ref, oac_ref), wa_ref[...]) + _dot(tile(ob_ref, obc_ref), wb_ref[...])
         + _dot(tile(oc_ref, occ_ref), wc_ref[...]))
    x1 = tile(x_ref, xc_ref) + mod[2:3] * y
    x1_ref[...] = x1
    ms = jnp.mean(x1 * x1, axis=-1, keepdims=True)
    t = (x1 * lax.rsqrt(ms + EPS)) * gain_ref[...]
    t = t * (1.0 + mod[4:5]) + mod[3:4]
    tok_ref[...] = _pack_bf16_pairs(t)

    t_hi, t_lo = _split_bf16(t)
    wrh = wrh_ref[...]
    logits = _dot(t_hi, wrh) + _dot(t_lo, wrh) + _dot(t_hi, wrl_ref[...]) + br_ref[...]

    lt = logits.T[:ROUTE_ROWS]
    row = lax.broadcasted_iota(jnp.int32, lt.shape, 0)
    row_f = row.astype(F32)
    is_g = row < N_GROUPS
    gl = jnp.where(is_g, lt, NEG)
    gmax = jnp.max(gl, axis=0, keepdims=True)
    g_sel = jnp.min(jnp.where(gl == gmax, row_f, 1e9), axis=0, keepdims=True)
    p_grp = 1.0 / jnp.sum(jnp.where(is_g, jnp.exp(gl - gmax), 0.0), axis=0, keepdims=True)
    grp_of_row = lax.shift_right_arithmetic(row - N_GROUPS, 2).astype(F32)
    in_grp = (row >= N_GROUPS) & (row < N_GROUPS + N_EXPERTS) & (grp_of_row == g_sel)
    el = jnp.where(in_grp, lt, NEG)
    v1 = jnp.max(el, axis=0, keepdims=True)
    i1 = jnp.min(jnp.where(el == v1, row_f, 1e9), axis=0, keepdims=True)
    el2 = jnp.where(row_f == i1, NEG, el)
    v2 = jnp.max(el2, axis=0, keepdims=True)
    i2 = jnp.min(jnp.where(el2 == v2, row_f, 1e9), axis=0, keepdims=True)
    e2 = jnp.exp(v2 - v1)
    den = 1.0 + e2
    w1 = p_grp / den
    w2 = p_grp * e2 / den

    @pl.when((lax.rem(pl.program_id(0), group_batches) == 0) & (pl.program_id(1) == 0))
    def _():
        run_ref[...] = jnp.zeros(run_ref.shape, F32)

    ind = jnp.where(row_f == i1, 1.0, 0.0) + jnp.where(row_f == i2, 1.0, 0.0)
    rank = _dot(ind.astype(BF16), tri_ref[...]) + run_ref[:, 0:1]

    def pick(m, r):
        return jnp.sum(jnp.where(row_f == r, m, 0.0), axis=0, keepdims=True)

    pos1 = (i1 - N_GROUPS) * region + pick(rank, i1)
    pos2 = (i2 - N_GROUPS) * region + pick(rank, i2)
    r = lax.broadcasted_iota(jnp.int32, (LANES, lt.shape[1]), 0)
    record = jnp.where(r == 0, pos1, jnp.where(r == 1, pos2, jnp.where(r == 2, w1, jnp.where(r == 3, w2, 0.0))))
    route_ref[...] = record.T
    run = run_ref[...] + jnp.sum(ind, axis=1, keepdims=True)
    run_ref[...] = run
    cnt_ref[...] = run


def _out_projection(x_lat, x_ctx, ctx_blk, mixed_lat, mixed_ctx, wa, wb, wc, modsel, gain, wrh, wrl, br,
                    n_tiles, n_lat_tiles):
    b, _, d = x_lat.shape
    tok = lambda bi, ti: (bi, ti, 0)
    const2 = lambda bi, ti: (0, 0)
    rows = n_tiles * TILE
    nb = b // MOE_GROUPS if b % MOE_GROUPS == 0 else b
    tri = jnp.asarray(np.triu(np.ones((TILE, TILE), np.float32), 1), BF16)
    lat_tile = lambda bi, ti: (bi, jnp.minimum(ti, n_lat_tiles - 1), 0)
    return pl.pallas_call(
        functools.partial(_outproj_kernel, region=nb * rows, group_batches=nb, n_lat_tiles=n_lat_tiles),
        out_shape=[jax.ShapeDtypeStruct((b, rows, d), F32),
                   jax.ShapeDtypeStruct((b, rows, d // 2), jnp.int32),
                   jax.ShapeDtypeStruct((b, rows, LANES), F32),
                   jax.ShapeDtypeStruct((ROUTE_ROWS * (b // nb), LANES), F32)],
        grid=(b, n_tiles),
        in_specs=_token_specs(d, n_lat_tiles, ctx_blk)
                 + [pl.BlockSpec((None, TILE, w), lat_tile) for w in (W_A, W_B, W_C)]
                 + [pl.BlockSpec((None, TILE, w), lambda bi, ti: (bi, 0, 0)) for w in (W_A, W_B, W_C)] + [
                  pl.BlockSpec((W_A, d), const2),
                  pl.BlockSpec((W_B, d), const2),
                  pl.BlockSpec((W_C, d), const2),
                  pl.BlockSpec((None, 6, d), lambda bi, ti: (2 * bi + (ti >= n_lat_tiles).astype(jnp.int32), 0, 0)),
                  pl.BlockSpec((1, d), const2),
                  pl.BlockSpec((d, LANES), const2),
                  pl.BlockSpec((d, LANES), const2),
                  pl.BlockSpec((1, LANES), const2),
                  pl.BlockSpec((TILE, TILE), const2)],
        out_specs=[pl.BlockSpec((None, TILE, d), tok),
                   pl.BlockSpec((None, TILE, d // 2), tok),
                   pl.BlockSpec((None, TILE, LANES), tok),
                   pl.BlockSpec((ROUTE_ROWS, LANES), lambda bi, ti: (bi // nb, 0))],
        scratch_shapes=[pltpu.VMEM((ROUTE_ROWS, LANES), F32)],
        compiler_params=_cparams(2),
        name="out_projection",
    )(x_lat, x_ctx, *mixed_lat, *mixed_ctx, wa, wb, wc, modsel, gain, wrh, wrl, br, tri)


def _sc_mesh():
    return plsc.VectorSubcoreMesh(core_axis_name="core", subcore_axis_name="subcore")


def _sc_worker_base(per_worker):
    wid = lax.axis_index("subcore") * SC_CORES + lax.axis_index("core")
    return wid * per_worker


def _sc_scratch(d, dtype):
    return ([pltpu.VMEM((SC_ROWS,), jnp.int32)] * SC_BUFS + [pltpu.VMEM((SC_ROWS, d), dtype)] * SC_BUFS
            + [pltpu.SemaphoreType.DMA] * (2 * SC_BUFS))


def _sc_split(scratch):
    return (scratch[:SC_BUFS], scratch[SC_BUFS:2 * SC_BUFS], scratch[2 * SC_BUFS:3 * SC_BUFS],
            scratch[3 * SC_BUFS:])


def _sc_chunk_loop(per_worker, group):
    chunks = per_worker // SC_ROWS
    full = chunks // SC_BUFS * SC_BUFS

    @pl.loop(0, full, step=SC_BUFS)
    def _(c):
        group(c, SC_BUFS)

    if chunks > full:
        group(full, chunks - full)


def _sc_scatter_rows(x, row_off, n, idx, n_out):
    d = x.shape[1]
    per_worker = 2 * n // (SC_CORES * SC_SUBCORES)
    assert per_worker % SC_ROWS == 0 and n % SC_ROWS == 0

    @functools.partial(pl.kernel, out_type=jax.ShapeDtypeStruct((n_out, d), x.dtype),
                       mesh=_sc_mesh(), scratch_types=_sc_scratch(d, x.dtype))
    def scatter(x_hbm, i_hbm, o_hbm, *scratch):
        idx_v, rows_v, sem_in, sem_out = _sc_split(scratch)
        base = _sc_worker_base(per_worker)

        def group(c, n_bufs):
            reads = []
            for u in range(n_bufs):
                a = pl.multiple_of(base + (c + u) * SC_ROWS, SC_ROWS)
                t = pl.multiple_of(row_off + lax.rem(a, n), SC_ROWS)
                pltpu.sync_copy(i_hbm.at[pl.ds(a, SC_ROWS)], idx_v[u])
                reads.append(pltpu.async_copy(x_hbm.at[pl.ds(t, SC_ROWS)], rows_v[u], sem_in[u]))
            writes = []
            for u in range(n_bufs):
                reads[u].wait()
                writes.append(pltpu.async_copy(rows_v[u], o_hbm.at[idx_v[u]], sem_out[u]))
            for w in writes:
                w.wait()

        _sc_chunk_loop(per_worker, group)

    return scatter(x, idx)


def _sc_gather_rows(src, idx):
    m = idx.shape[0]
    d = src.shape[1]
    per_worker = m // (SC_CORES * SC_SUBCORES)
    assert per_worker % SC_ROWS == 0

    @functools.partial(pl.kernel, out_type=jax.ShapeDtypeStruct((m, d), src.dtype),
                       mesh=_sc_mesh(), scratch_types=_sc_scratch(d, src.dtype))
    def gather(s_hbm, i_hbm, o_hbm, *scratch):
        idx_v, rows_v, sem_in, sem_out = _sc_split(scratch)
        base = _sc_worker_base(per_worker)

        def group(c, n_bufs):
            offs, reads = [], []
            for u in range(n_bufs):
                a = pl.multiple_of(base + (c + u) * SC_ROWS, SC_ROWS)
                offs.append(a)
                pltpu.sync_copy(i_hbm.at[pl.ds(a, SC_ROWS)], idx_v[u])
                reads.append(pltpu.async_copy(s_hbm.at[idx_v[u]], rows_v[u], sem_in[u]))
            writes = []
            for u in range(n_bufs):
                reads[u].wait()
                writes.append(pltpu.async_copy(rows_v[u], o_hbm.at[pl.ds(offs[u], SC_ROWS)], sem_out[u]))
            for w in writes:
                w.wait()

        _sc_chunk_loop(per_worker, group)

    return gather(src, idx)


def _expert_ffn_kernel(blk_ref, exp_ref, x_ref, wg_ref, wu_ref, wd_ref, y_ref, wgb_ref, wub_ref, wdb_ref):
    j = pl.program_id(0)

    @pl.when((j == 0) | (exp_ref[j] != exp_ref[jnp.maximum(j - 1, 0)]))
    def _():
        wgb_ref[...] = wg_ref[...].astype(BF16)
        wub_ref[...] = wu_ref[...].astype(BF16)
        wdb_ref[...] = wd_ref[...].astype(BF16)

    @pl.when((j == 0) | (blk_ref[j] != blk_ref[jnp.maximum(j - 1, 0)]))
    def _():
        x = _unpack_bf16_pairs(x_ref[...]).astype(BF16)
        hid = jax.nn.silu(_dot(x, wgb_ref[...])) * _dot(x, wub_ref[...])
        y_ref[...] = _pack_bf16_pairs(_dot(hid.astype(BF16), wdb_ref[...]))


def _expert_ffn(xs, blk, exp, wg, wu, wd, layer):
    rows, d_packed = xs.shape
    d = 2 * d_packed
    w_map = lambda j, blk, exp: (layer, exp[j], 0, 0)
    return pl.pallas_call(
        _expert_ffn_kernel,
        out_shape=jax.ShapeDtypeStruct((rows, d_packed), jnp.int32),
        grid_spec=pltpu.PrefetchScalarGridSpec(
            num_scalar_prefetch=2,
            grid=(blk.shape[0],),
            in_specs=[pl.BlockSpec((MOE_TILE, d_packed), lambda j, blk, exp: (blk[j], 0)),
                      pl.BlockSpec((None, None, d, EXPERT_HIDDEN), w_map),
                      pl.BlockSpec((None, None, d, EXPERT_HIDDEN), w_map),
                      pl.BlockSpec((None, None, EXPERT_HIDDEN, d), w_map)],
            out_specs=pl.BlockSpec((MOE_TILE, d_packed), lambda j, blk, exp: (blk[j], 0)),
            scratch_shapes=[pltpu.VMEM((d, EXPERT_HIDDEN), BF16),
                            pltpu.VMEM((d, EXPERT_HIDDEN), BF16),
                            pltpu.VMEM((EXPERT_HIDDEN, d), BF16)]),
        compiler_params=_cparams(1),
        name="expert_ffn",
    )(blk, exp, xs, wg, wu, wd)


def _combine_kernel(x1_ref, y1_ref, y2_ref, route_ref, mod_ref, fgain_ref, *rest, final):
    o_ref = rest[-1]
    route = route_ref[...]
    y = route[:, 2:3] * _unpack_bf16_pairs(y1_ref[...]) + route[:, 3:4] * _unpack_bf16_pairs(y2_ref[...])
    x2 = x1_ref[...] + mod_ref[5:6, :] * y
    if final:
        ms = jnp.mean(x2 * x2, axis=-1, keepdims=True)
        x2 = (x2 * lax.rsqrt(ms + EPS)) * fgain_ref[...]
    o_ref[...] = x2


def _combine(x1, ys, route, modsel, fgain, prev, b0, nb, n_lat_tiles, final):
    b, rows, d = x1.shape
    n_t = rows // TILE
    tok = lambda bi, ti: (b0 + bi, ti, 0)
    in_specs = [pl.BlockSpec((None, TILE, d), tok),
                pl.BlockSpec((TILE, d // 2), lambda bi, ti: (bi * n_t + ti, 0)),
                pl.BlockSpec((TILE, d // 2), lambda bi, ti: ((nb + bi) * n_t + ti, 0)),
                pl.BlockSpec((None, TILE, LANES), tok),
                pl.BlockSpec((None, 6, d),
                             lambda bi, ti: (2 * (b0 + bi) + (ti >= n_lat_tiles).astype(jnp.int32), 0, 0)),
                pl.BlockSpec((1, d), lambda bi, ti: (0, 0))]
    args = [x1, ys, ys, route, modsel, fgain]
    aliases = {}
    if prev is not None:
        in_specs.append(pl.BlockSpec(memory_space=pl.ANY))
        args.append(prev)
        aliases = {len(args) - 1: 0}
    return pl.pallas_call(
        functools.partial(_combine_kernel, final=final),
        out_shape=jax.ShapeDtypeStruct((b, rows, d), F32),
        grid=(nb, n_t),
        in_specs=in_specs,
        out_specs=pl.BlockSpec((None, TILE, d), tok),
        input_output_aliases=aliases,
        compiler_params=_cparams(2),
        name="moe_combine",
    )(*args)


def _routed_moe(tok, route, cnt, x1, wg, wu, wd, layer, modsel, fgain, n_lat_tiles, final):
    b, rows, d = x1.shape
    n_groups = cnt.shape[0] // ROUTE_ROWS
    nb = b // n_groups
    n = nb * rows
    flat = route.reshape(b * rows, LANES)
    tok_flat = tok.reshape(b * rows, tok.shape[2])
    out = None
    for g in range(n_groups):
        part = flat[g * n:(g + 1) * n]
        idx = jnp.concatenate([part[:, 0], part[:, 1]]).astype(jnp.int32)
        xs = _sc_scatter_rows(tok_flat, g * n, n, idx, N_EXPERTS * n)

        e0 = ROUTE_ROWS * g + N_GROUPS
        counts = cnt[e0:e0 + N_EXPERTS, 0].astype(jnp.int32)
        tiles = (counts + MOE_TILE - 1) // MOE_TILE
        ends = jnp.cumsum(tiles)
        n_sched = 2 * n // MOE_TILE + N_EXPERTS
        j = jnp.minimum(jnp.arange(n_sched, dtype=jnp.int32), ends[-1] - 1)
        exp = jnp.sum((j[:, None] >= ends[None, :]).astype(jnp.int32), axis=1)
        blk = exp * (n // MOE_TILE) + j - (ends - tiles)[exp]

        ys = _expert_ffn(xs, blk, exp, wg, wu, wd, layer)
        yg = _sc_gather_rows(ys, idx)
        out = _combine(x1, yg, route, modsel, fgain, out, g * nb, nb, n_lat_tiles, final)
    return out


def _rope_tables(n_lat):
    t = jnp.arange(n_lat)
    row = (t // GRID_W).astype(F32)
    col = (t % GRID_W).astype(F32)

    def cs(dim):
        quarter = dim // 4
        freqs = ROPE_THETA ** (-jnp.arange(quarter, dtype=F32) / quarter)
        ang = jnp.concatenate([row[:, None] * freqs, col[:, None] * freqs], axis=-1)
        cos = jnp.tile(jnp.cos(ang), (1, 2 * LANES // dim))
        sin = jnp.tile(jnp.sin(ang), (1, 2 * LANES // dim))
        cos = jnp.concatenate([cos, jnp.ones((CTX_LEN, LANES), F32)], axis=0)
        sin = jnp.concatenate([sin, jnp.zeros((CTX_LEN, LANES), F32)], axis=0)
        return cos, sin

    cos_b, sin_b = cs(DIFF_QK_DIM)
    cos_c, sin_c = cs(HEAD_DIM)
    return jnp.concatenate([cos_b, sin_b, cos_c, sin_c], axis=1)


def _reordered_w_in(w_in):
    o_c = 3 * W_A + 3 * W_B
    heads = [w_in[:, o_c + h * HEAD_DIM:o_c + (h + 1) * HEAD_DIM] for h in GQA_Q_ORDER]
    return jnp.concatenate([w_in[:, :o_c]] + heads + [w_in[:, o_c + W_C:]], axis=1).astype(BF16)


def kernel(x, c, ctx, c_ctx, w_mod, b_mod, norm_attn, norm_ffn, w_in, w_out, na_rpb, diff_lambda_q1, diff_lambda_k1, diff_lambda_q2, diff_lambda_k2, diff_subln, gqa_q_norm, gqa_k_norm, router_group_w, router_group_b, router_expert_w, router_expert_b, w_gate, w_up, w_down, final_norm):
    b, s, d = x.shape
    assert d == D_MODEL and ctx.shape[1] == CTX_LEN and s % (NA_QROWS * GRID_W) == 0
    rows = s // GRID_W
    assert rows >= 2 * NA_QROWS
    t_all = s + CTX_LEN
    n_lat_tiles = s // TILE

    assert b + 1 <= SUBLANES
    c_rows = jnp.zeros((SUBLANES, d), F32).at[:b].set(c).at[b].set(c_ctx)
    mod = _modulation(c_rows, w_mod, b_mod)

    tab = _rope_tables(s)
    hidx = np.arange(HEAD_DIM)
    partner = np.where(hidx < HEAD_DIM // 2, hidx + HEAD_DIM // 2, hidx - HEAD_DIM // 2)
    blk = np.arange(W_C) // HEAD_DIM
    ones = jnp.asarray((blk[:, None] == blk[None, :]).astype(np.float32), BF16)
    dummy_aux = jnp.zeros((SUBLANES, LANES), F32)

    x_lat, x_ctx, ctx_blk = x, ctx, 0
    for l in range(DEPTH):
        ctx_out = l < DEPTH - 1
        lam_init = 0.8 - 0.6 * math.exp(-0.3 * l)
        m_lat = mod[l, :b].reshape(b, 1, 6, d)
        m_ctx = jnp.broadcast_to(mod[l, b].reshape(1, 1, 6, d), (b, 1, 6, d))
        modsel = jnp.concatenate([m_lat, m_ctx], axis=1).reshape(2 * b, 6, d)

        gq = jnp.stack([jnp.tile(gqa_q_norm[l], GQA_Q_HEADS), jnp.tile(gqa_q_norm[l][partner], GQA_Q_HEADS)])
        gk = jnp.stack([jnp.tile(gqa_k_norm[l], GQA_KV_HEADS), jnp.tile(gqa_k_norm[l][partner], GQA_KV_HEADS)])
        qa, ka, va, qb, kb, vb, qc, kc, vc = _in_projection(
            x_lat, x_ctx, ctx_blk, modsel, norm_attn[l][None], _reordered_w_in(w_in[l]), tab, gq, gk, ones,
            n_lat_tiles)

        n_qt = n_lat_tiles + 1 if ctx_out else n_lat_tiles
        pad = lambda v: jnp.pad(v, (0, LANES - v.shape[0]))
        aux = jnp.stack([pad(diff_lambda_q1[l]), pad(diff_lambda_k1[l]), pad(diff_lambda_q2[l]),
                         pad(diff_lambda_k2[l]), jnp.tile(diff_subln[l], 2),
                         jnp.zeros((LANES,), F32), jnp.zeros((LANES,), F32), jnp.zeros((LANES,), F32)])
        group_a = dict(n_qblk=1, n_sub=2, n_hp=NA_HEADS // 2, n_lat=s, mode="plain")
        group_b = dict(n_qblk=1, n_sub=4, n_hp=DIFF_HEADS // 2, n_lat=s, mode="diff", lam_init=lam_init)
        group_c = dict(n_qblk=3, n_sub=2, n_hp=1, n_lat=s, mode="plain")
        oa = _neighbourhood_attention(qa, ka, va, _na_bias_table(na_rpb[l], rows), s)
        ob = _flash(qb, kb, vb, aux, queries="latent", **group_b)
        oc = _flash(qc, kc, vc, dummy_aux, queries="latent", **group_c)
        mixed_lat = mixed_ctx = (oa, ob, oc)
        if ctx_out:
            mixed_ctx = (_flash(qa, ka, va, dummy_aux, queries="context", **group_a),
                         _flash(qb, kb, vb, aux, queries="context", **group_b),
                         _flash(qc, kc, vc, dummy_aux, queries="context", **group_c))

        w_o = w_out[l]
        o_c = W_A + W_B
        w_oc = jnp.concatenate([w_o[o_c + h * HEAD_DIM:o_c + (h + 1) * HEAD_DIM] for h in GQA_Q_ORDER], axis=0)
        wr = jnp.zeros((d, LANES), F32)
        wr = wr.at[:, :N_GROUPS].set(router_group_w[l]).at[:, N_GROUPS:N_GROUPS + N_EXPERTS].set(router_expert_w[l])
        wrh, wrl = _split_bf16(wr)
        br = jnp.zeros((1, LANES), F32)
        br = br.at[0, :N_GROUPS].set(router_group_b[l]).at[0, N_GROUPS:N_GROUPS + N_EXPERTS].set(router_expert_b[l])
        x1, tok, route, cnt = _out_projection(
            x_lat, x_ctx, ctx_blk, mixed_lat, mixed_ctx, w_o[:W_A].astype(BF16), w_o[W_A:W_A + W_B].astype(BF16),
            w_oc.astype(BF16), modsel, norm_ffn[l][None], wrh, wrl, br, n_qt, n_lat_tiles)
        xs = _routed_moe(tok, route, cnt, x1, w_gate, w_up, w_down, l, modsel, final_norm[None],
                         n_lat_tiles, final=not ctx_out)
        x_lat, x_ctx, ctx_blk = xs, xs, n_lat_tiles
    return xs
```

```python
import functools
import math

import numpy as np
import jax
import jax.numpy as jnp
from jax import lax
from jax.experimental import pallas as pl
from jax.experimental.pallas import tpu as pltpu
from jax.experimental.pallas import tpu_sc as plsc

F32 = jnp.float32
BF16 = jnp.bfloat16

D_MODEL = 1024
DEPTH = 2
GRID_W = 64
CTX_LEN = 256
HEAD_DIM = 64
NA_HEADS = 6
NA_WIN_H = 8
NA_WIN_W = 16
DIFF_HEADS = 4
DIFF_QK_DIM = 32
GQA_Q_HEADS = 6
GQA_KV_HEADS = 2
N_GROUPS = 4
EXPERTS_PER_GROUP = 4
N_EXPERTS = 16
EXPERT_HIDDEN = 512
ROPE_THETA = 10000.0
EPS = 1e-6
W_A = NA_HEADS * HEAD_DIM
W_B = DIFF_HEADS * 2 * DIFF_QK_DIM
W_C = GQA_Q_HEADS * HEAD_DIM
W_KC = GQA_KV_HEADS * HEAD_DIM
IN_WIDTH = 3 * W_A + 3 * W_B + W_C + 2 * W_KC

LANES = 128
TILE = CTX_LEN
NA_QROWS = 8
NA_KROWS = 16
NA_PARTS = 2
NEG = -1e30
LOG2E = 1.4426950408889634
HI16 = -65536
VMEM_LIMIT = 56 * 1024 * 1024
FLASH_TK = 512
PAIRS_PER_STEP = 2
FLASH_UNROLL_MAX_ROWS = 1536
FLASH_TILES_PER_STEP = 2
FLASH_PAIR_MAX_ROWS = 1024
SUBLANES = 8
ROUTE_ROWS = 32
MOE_TILE = 512
MOE_GROUPS = 2
SC_ROWS = 32
SC_BUFS = 4
SC_CORES = 2
SC_SUBCORES = 16

GQA_Q_ORDER = (0, 3, 1, 4, 2, 5)


def _cparams(n_axes):
    return pltpu.CompilerParams(dimension_semantics=("arbitrary",) * n_axes,
                                vmem_limit_bytes=VMEM_LIMIT)


def _split_bf16(a):
    hi = a.astype(BF16)
    lo = (a - hi.astype(F32)).astype(BF16)
    return hi, lo


def _dot(a, b):
    return jnp.dot(a, b, preferred_element_type=F32)


def _pack_bf16_pairs(t):
    bits = lax.bitcast_convert_type(t.astype(BF16).astype(F32), jnp.int32)
    half_d = bits.shape[1] // 2
    return lax.shift_right_logical(bits[:, :half_d], 16) | (bits[:, half_d:] & HI16)


def _unpack_bf16_pairs(w):
    return jnp.concatenate([lax.bitcast_convert_type(lax.shift_left(w, 16), F32),
                            lax.bitcast_convert_type(w & HI16, F32)], axis=1)


def _dot_nt(a, b):
    return lax.dot_general(a, b, (((1,), (1,)), ((), ())), preferred_element_type=F32)


def _mod_kernel(c_ref, w_ref, b_ref, o_ref):
    c = c_ref[...]
    a = c * jax.nn.sigmoid(c)
    a_hi, a_lo = _split_bf16(a)
    w_hi, w_lo = _split_bf16(w_ref[...])
    o_ref[...] = _dot(a_hi, w_hi) + _dot(a_lo, w_hi) + _dot(a_hi, w_lo) + b_ref[...]


def _modulation(c_rows, w_mod, b_mod):
    depth, d, n = w_mod.shape
    bn = 1536
    return pl.pallas_call(
        _mod_kernel,
        out_shape=jax.ShapeDtypeStruct((depth, SUBLANES, n), F32),
        grid=(depth, n // bn),
        in_specs=[pl.BlockSpec((SUBLANES, d), lambda l, j: (0, 0)),
                  pl.BlockSpec((None, d, bn), lambda l, j: (l, 0, j)),
                  pl.BlockSpec((None, 1, bn), lambda l, j: (l, 0, j))],
        out_specs=pl.BlockSpec((None, SUBLANES, bn), lambda l, j: (l, 0, j)),
        compiler_params=_cparams(2),
        name="adaln_mod",
    )(c_rows, w_mod, b_mod.reshape(depth, 1, n))


def _head_mean_sq(t, ones):
    hi, lo = _split_bf16(t * t)
    return (_dot(hi, ones) + _dot(lo, ones)) * (1.0 / HEAD_DIM)


def _rotate_half(p, head):
    w = p.shape[1]
    half = head // 2
    lane = lax.broadcasted_iota(jnp.int32, (1, w), 1)
    first = (lane & (head - 1)) < half
    from_right = pltpu.roll(p, w - half, 1)
    from_left = pltpu.roll(p, half, 1)
    return jnp.where(first, -from_right, from_left)


def _inproj_kernel(x_ref, xc_ref, mod_ref, gain_ref, w_ref, tab_ref, gq_ref, gk_ref, ones_ref,
                   qa_ref, ka_ref, va_ref, qb_ref, kb_ref, vb_ref, qc_ref, kc_ref, vc_ref,
                   *, n_lat_tiles):
    x = jnp.where(pl.program_id(1) == n_lat_tiles, xc_ref[...], x_ref[...])
    mod = mod_ref[...]
    ms = jnp.mean(x * x, axis=-1, keepdims=True)
    h = (x * lax.rsqrt(ms + EPS)) * gain_ref[...]
    h = h * (1.0 + mod[1:2]) + mod[0:1]
    hb = h.astype(BF16)

    def proj(a, b):
        return _dot(hb, w_ref[:, a:b])

    pa = proj(0, 3 * W_A)
    qa_ref[...] = (pa[:, :W_A] * (HEAD_DIM ** -0.5 * LOG2E)).astype(BF16)
    ka_ref[...] = pa[:, W_A:2 * W_A].astype(BF16)
    va_ref[...] = pa[:, 2 * W_A:].astype(BF16)

    tab = tab_ref[...]
    cos_b = jnp.concatenate([tab[:, 0:LANES]] * 2, axis=1)
    sin_b = jnp.concatenate([tab[:, LANES:2 * LANES]] * 2, axis=1)
    cos_c1 = tab[:, 2 * LANES:3 * LANES]
    sin_c1 = tab[:, 3 * LANES:4 * LANES]
    cos_c = jnp.concatenate([cos_c1] * 3, axis=1)
    sin_c = jnp.concatenate([sin_c1] * 3, axis=1)

    o_b = 3 * W_A
    pb = proj(o_b, o_b + 3 * W_B)
    qb = pb[:, :W_B]
    kb = pb[:, W_B:2 * W_B]
    qb = qb * cos_b + _rotate_half(qb, DIFF_QK_DIM) * sin_b
    qb_ref[...] = (qb * (DIFF_QK_DIM ** -0.5 * LOG2E)).astype(BF16)
    kb_ref[...] = (kb * cos_b + _rotate_half(kb, DIFF_QK_DIM) * sin_b).astype(BF16)
    vb_ref[...] = pb[:, 2 * W_B:].astype(BF16)

    o_c = o_b + 3 * W_B
    pc = proj(o_c, IN_WIDTH)
    ones = ones_ref[...]
    qc = pc[:, :W_C]
    kc = pc[:, W_C:W_C + W_KC]
    nq = lax.rsqrt(_head_mean_sq(qc, ones) + EPS)
    nk = lax.rsqrt(_head_mean_sq(kc, ones[:W_KC, :W_KC]) + EPS)
    gq = gq_ref[...]
    gk = gk_ref[...]
    q = nq * (qc * gq[0:1] * cos_c + _rotate_half(qc, HEAD_DIM) * gq[1:2] * sin_c)
    qc_ref[...] = (q * (HEAD_DIM ** -0.5 * LOG2E)).astype(BF16)
    k = nk * (kc * gk[0:1] * cos_c1 + _rotate_half(kc, HEAD_DIM) * gk[1:2] * sin_c1)
    kc_ref[...] = k.astype(BF16)
    vc_ref[...] = pc[:, W_C + W_KC:].astype(BF16)


def _token_specs(d, n_lat_tiles, ctx_blk):
    return [pl.BlockSpec((None, TILE, d), lambda bi, ti: (bi, jnp.minimum(ti, n_lat_tiles - 1), 0)),
            pl.BlockSpec((None, TILE, d), lambda bi, ti: (bi, ctx_blk, 0))]


def _in_projection(x_lat, x_ctx, ctx_blk, modsel, gain, w_ext, tab, gq, gk, ones, n_lat_tiles):
    b, _, d = x_lat.shape
    n_tiles = n_lat_tiles + 1
    t_all = n_tiles * TILE
    widths = (W_A, W_A, W_A, W_B, W_B, W_B, W_C, W_KC, W_KC)
    tok = lambda bi, ti: (bi, ti, 0)
    const2 = lambda bi, ti: (0, 0)
    return pl.pallas_call(
        functools.partial(_inproj_kernel, n_lat_tiles=n_lat_tiles),
        out_shape=[jax.ShapeDtypeStruct((b, t_all, w), BF16) for w in widths],
        grid=(b, n_tiles),
        in_specs=_token_specs(d, n_lat_tiles, ctx_blk) + [
                  pl.BlockSpec((None, 6, d), lambda bi, ti: (2 * bi + (ti >= n_lat_tiles).astype(jnp.int32), 0, 0)),
                  pl.BlockSpec((1, d), const2),
                  pl.BlockSpec((d, IN_WIDTH), const2),
                  pl.BlockSpec((TILE, 4 * LANES), lambda bi, ti: (ti, 0)),
                  pl.BlockSpec((2, W_C), const2),
                  pl.BlockSpec((2, W_KC), const2),
                  pl.BlockSpec((W_C, W_C), const2)],
        out_specs=[pl.BlockSpec((None, TILE, w), tok) for w in widths],
        compiler_params=_cparams(2),
        name="in_projection",
    )(x_lat, x_ctx, modsel, gain, w_ext, tab, gq, gk, ones)


def _flash_kernel(q_ref, k_ref, v_ref, aux_ref, o_ref, va_ref, vb_ref, qs_ref, acc_ref, m_ref,
                  s0_ref, s1_ref, mb0_ref, mb1_ref, *,
                  n_qblk, n_sub, tk, n_lat_blocks, pairs_per_step, ctx_start, queries, mode, lam_init,
                  prepare_values=True):
    sub_w = LANES // n_sub
    half = LANES // 2
    lane = lax.broadcasted_iota(jnp.int32, (1, LANES), 1)
    lower = lane < half
    n_pieces = n_qblk * n_sub
    ma = (n_pieces // 2) * TILE
    m_rows = n_pieces * TILE

    if prepare_values:
        @pl.when(pl.program_id(2) == 0)
        def _():
            v = v_ref[...].astype(F32)
            va_ref[...] = jnp.where(lower, v, 1.0).astype(BF16)
            vb_ref[...] = jnp.where(lower, 1.0, v).astype(BF16)

    ia, ib = 0, n_pieces // 2
    for blk in range(n_qblk):
        qf = q_ref[:, blk * LANES:(blk + 1) * LANES].astype(F32)
        for sub in range(n_sub):
            msk = (lane >= sub * sub_w) & (lane < (sub + 1) * sub_w)
            piece = jnp.where(msk, qf, 0.0).astype(BF16)
            if sub * sub_w < half:
                qs_ref[ia * TILE:(ia + 1) * TILE, :] = piece
                ia += 1
            else:
                qs_ref[ib * TILE:(ib + 1) * TILE, :] = piece
                ib += 1

    s_bufs = (s0_ref, s1_ref)
    mb_bufs = (mb0_ref, mb1_ref)

    def scores(start, size, slot):
        s = _dot_nt(qs_ref[...], k_ref[pl.ds(start, size), :])
        s_bufs[slot][:, :size] = s
        mb = jnp.max(s, axis=-1, keepdims=True)
        mb_bufs[slot][...] = jnp.broadcast_to(mb, (m_rows, LANES))

    def accumulate(start, size, slot, first):
        mb = mb_bufs[slot][...]
        if first:
            m_new = mb
        else:
            m_old = m_ref[...]
            m_new = jnp.maximum(m_old, mb)
        s_ref = s_bufs[slot]
        cols = [s_ref[:, c * LANES:(c + 1) * LANES] - m_new for c in range(size // LANES)]
        p = jnp.concatenate([jnp.exp2(d.astype(BF16)) for d in cols], axis=1)
        pva = _dot(p[:ma], va_ref[pl.ds(start, size), :])
        pvb = _dot(p[ma:], vb_ref[pl.ds(start, size), :])
        if first:
            acc_ref[:ma, :] = pva
            acc_ref[ma:, :] = pvb
        else:
            alpha = jnp.exp2(m_old - m_new)
            acc_ref[:ma, :] = alpha[:ma] * acc_ref[:ma, :] + pva
            acc_ref[ma:, :] = alpha[ma:] * acc_ref[ma:, :] + pvb
        m_ref[...] = m_new

    def lat(j):
        return pl.multiple_of(j * tk, tk)

    def latent_queries():
        scores(ctx_start, CTX_LEN, 0)
        scores(lat(0), tk, 1)
        accumulate(ctx_start, CTX_LEN, 0, True)

        def pair(i):
            scores(lat(2 * i + 1), tk, 0)
            accumulate(lat(2 * i), tk, 1, False)
            scores(lat(2 * i + 2), tk, 1)
            accumulate(lat(2 * i + 1), tk, 0, False)

        def body(i, carry):
            for u in range(pairs_per_step):
                pair(i * pairs_per_step + u)
            return carry

        n_pairs = (n_lat_blocks - 2) // 2
        n_steps = n_pairs // pairs_per_step
        lax.fori_loop(0, n_steps, body, 0)
        for i in range(n_steps * pairs_per_step, n_pairs):
            pair(i)
        scores(lat(n_lat_blocks - 1), tk, 0)
        accumulate(lat(n_lat_blocks - 2), tk, 1, False)
        accumulate(lat(n_lat_blocks - 1), tk, 0, False)

    def context_queries():
        scores(ctx_start, CTX_LEN, 0)
        accumulate(ctx_start, CTX_LEN, 0, True)

    if queries == "latent":
        latent_queries()
    else:
        context_queries()

    acc = acc_ref[...]
    r = acc / pltpu.roll(acc, half, 1)
    ra, rb = r[:ma], r[ma:]
    if mode == "plain":
        for i in range(n_pieces // 2):
            o = jnp.where(lower, ra[i * TILE:(i + 1) * TILE], rb[i * TILE:(i + 1) * TILE])
            o_ref[:, i * LANES:(i + 1) * LANES] = o.astype(BF16)
    else:
        aux = aux_ref[...]
        l1 = jnp.sum(aux[0:1] * aux[1:2], axis=-1, keepdims=True)
        l2 = jnp.sum(aux[2:3] * aux[3:4], axis=-1, keepdims=True)
        lam = jnp.exp(l1) - jnp.exp(l2) + lam_init
        oa = ra[:TILE] - lam * ra[TILE:]
        ob = rb[:TILE] - lam * rb[TILE:]
        o = jnp.where(lower, oa, ob)
        sq = o * o
        ss_a = jnp.sum(jnp.where(lower, sq, 0.0), axis=-1, keepdims=True)
        ss_b = jnp.sum(jnp.where(lower, 0.0, sq), axis=-1, keepdims=True)
        ms = jnp.where(lower, ss_a, ss_b) * (1.0 / HEAD_DIM)
        o = (o * lax.rsqrt(ms + EPS)) * aux[4:5]
        o_ref[...] = (o * (1.0 - lam_init)).astype(BF16)


def _flash_tiles_kernel(q_ref, k_ref, v_ref, aux_ref, o_ref, va_ref, vb_ref, *scratch, tiles, **static):
    for t in range(tiles):
        rows = pl.ds(t * TILE, TILE)
        _flash_kernel(q_ref.at[rows], k_ref, v_ref, aux_ref, o_ref.at[rows], va_ref, vb_ref,
                      *[r.at[t] for r in scratch], prepare_values=(t == 0), **static)


def _flash(q, k, v, aux, *, n_qblk, n_sub, n_hp, n_lat, queries, mode, lam_init=0.0):
    b, t_all, _ = q.shape
    qw = n_qblk * LANES
    tk = FLASH_TK
    assert n_lat % (2 * tk) == 0 and tk >= CTX_LEN
    m_rows = n_qblk * n_sub * TILE
    n_pairs = max((n_lat // tk - 2) // 2, 1)
    pairs = n_pairs if m_rows <= FLASH_UNROLL_MAX_ROWS else PAIRS_PER_STEP
    if queries == "latent":
        n_qt, qt_off, kv_rows, kv_blk, ctx_start = n_lat // TILE, 0, t_all, 0, n_lat
    else:
        n_qt, qt_off, kv_rows, kv_blk, ctx_start = 1, n_lat // TILE, CTX_LEN, n_lat // CTX_LEN, 0
    tiles = FLASH_TILES_PER_STEP if (queries == "latent" and m_rows <= FLASH_PAIR_MAX_ROWS
                                     and n_qt % FLASH_TILES_PER_STEP == 0) else 1
    kern = functools.partial(_flash_tiles_kernel, tiles=tiles, n_qblk=n_qblk, n_sub=n_sub, tk=tk,
                             n_lat_blocks=n_lat // tk, ctx_start=ctx_start, queries=queries,
                             pairs_per_step=pairs, mode=mode, lam_init=lam_init)
    return pl.pallas_call(
        kern,
        out_shape=jax.ShapeDtypeStruct((b, n_qt * TILE, n_hp * qw), BF16),
        grid=(b, n_hp, n_qt // tiles),
        in_specs=[pl.BlockSpec((None, tiles * TILE, qw), lambda bi, hp, qt: (bi, qt + qt_off, hp)),
                  pl.BlockSpec((None, kv_rows, LANES), lambda bi, hp, qt: (bi, kv_blk, hp)),
                  pl.BlockSpec((None, kv_rows, LANES), lambda bi, hp, qt: (bi, kv_blk, hp)),
                  pl.BlockSpec((SUBLANES, LANES), lambda bi, hp, qt: (0, 0))],
        out_specs=pl.BlockSpec((None, tiles * TILE, qw), lambda bi, hp, qt: (bi, qt, hp)),
        scratch_shapes=[pltpu.VMEM((kv_rows, LANES), BF16),
                        pltpu.VMEM((kv_rows, LANES), BF16),
                        pltpu.VMEM((tiles, m_rows, LANES), BF16),
                        pltpu.VMEM((tiles, m_rows, LANES), F32),
                        pltpu.VMEM((tiles, m_rows, LANES), F32),
                        pltpu.VMEM((tiles, m_rows, tk), F32),
                        pltpu.VMEM((tiles, m_rows, tk), F32),
                        pltpu.VMEM((tiles, m_rows, LANES), F32),
                        pltpu.VMEM((tiles, m_rows, LANES), F32)],
        compiler_params=_cparams(3),
        name="flash_" + mode,
    )(q, k, v, aux)


def _na_kernel(q_ref, k0, k1, k2, k3, v0, v1, v2, v3, kc_ref, vc_ref, bias_ref, o_ref, s_ref, m_ref):
    lane = lax.broadcasted_iota(jnp.int32, (1, LANES), 1)
    lower = lane < LANES // 2
    n_pair = NA_KROWS // 2
    rows_per_part = NA_QROWS // NA_PARTS
    half_q = rows_per_part * GRID_W
    no_bias = jnp.zeros((GRID_W, CTX_LEN), F32)

    def head_pair(hp):
        cols = slice(hp * LANES, (hp + 1) * LANES)
        qf = q_ref[:, cols].astype(F32)
        k_all = jnp.concatenate([r[:, cols] for r in (k0, k1, k2, k3, kc_ref)], axis=0)
        v_all = jnp.concatenate([r[:, cols] for r in (v0, v1, v2, v3, vc_ref)], axis=0).astype(F32)
        v_h = [jnp.where(lower, v_all, 1.0).astype(BF16), jnp.where(lower, 1.0, v_all).astype(BF16)]
        q_h = [jnp.where(lower, qf, 0.0).astype(BF16), jnp.where(lower, 0.0, qf).astype(BF16)]
        return q_h, k_all, v_h

    pairs = [head_pair(hp) for hp in range(NA_HEADS // 2)]
    items = [(hp, part) for hp in range(NA_HEADS // 2) for part in range(NA_PARTS)]

    def scores(n):
        hp, part = items[n]
        q_h, k_all, _ = pairs[hp]
        rows = slice(part * half_q, (part + 1) * half_q)
        qs = jnp.concatenate([q_h[0][rows], q_h[1][rows]], axis=0)
        bias = jnp.concatenate(
            [jnp.concatenate([bias_ref[2 * hp + hh, a * n_pair + j] for j in range(n_pair)] + [no_bias],
                             axis=1)
             for hh in range(2) for a in range(part * rows_per_part, (part + 1) * rows_per_part)],
            axis=0)
        s = _dot_nt(qs, k_all) + bias
        s_ref[n % 2] = s
        m_ref[n % 2] = jnp.broadcast_to(jnp.max(s, axis=-1, keepdims=True), (2 * half_q, LANES))

    def finish(n):
        hp, part = items[n]
        v_h = pairs[hp][2]
        s = s_ref[n % 2]
        m = m_ref[n % 2]
        p = jnp.concatenate([jnp.exp2((s[:, c * LANES:(c + 1) * LANES] - m).astype(BF16))
                             for c in range(s.shape[1] // LANES)], axis=1)
        o0 = _dot(p[:half_q], v_h[0])
        o1 = _dot(p[half_q:], v_h[1])
        o0 = o0 / pltpu.roll(o0, LANES // 2, 1)
        o1 = o1 / pltpu.roll(o1, LANES // 2, 1)
        o_ref[part * half_q:(part + 1) * half_q, hp * LANES:(hp + 1) * LANES] = (
            jnp.where(lower, o0, o1).astype(BF16))

    scores(0)
    for n in range(1, len(items)):
        scores(n)
        finish(n - 1)
    finish(len(items) - 1)


def _neighbourhood_attention(qa, ka, va, bias, n_lat):
    b = qa.shape[0]
    q_tok = NA_QROWS * GRID_W
    v_tok = q_tok // 2
    n_rb = n_lat // q_tok
    n_view = n_lat // v_tok
    ctx_blk = n_lat // v_tok

    def view(j):
        return lambda rb, bi: (bi, jnp.clip(2 * rb - 1 + j, 0, n_view - 1), 0)

    kv_specs = [pl.BlockSpec((None, v_tok, W_A), view(j)) for j in range(4)]
    ctx_spec = pl.BlockSpec((None, CTX_LEN, W_A), lambda rb, bi: (bi, ctx_blk, 0))

    def bias_map(rb, bi):
        pat = jnp.where(rb == 0, 0, jnp.where(rb == n_rb - 1, 2, 1))
        return (0, pat, 0, 0, 0)

    part_rows = 2 * q_tok // NA_PARTS
    return pl.pallas_call(
        _na_kernel,
        out_shape=jax.ShapeDtypeStruct((b, n_lat, W_A), BF16),
        grid=(n_rb, b),
        in_specs=[pl.BlockSpec((None, q_tok, W_A), lambda rb, bi: (bi, rb, 0))]
                 + kv_specs + kv_specs + [ctx_spec, ctx_spec,
                 pl.BlockSpec((NA_HEADS, None, NA_QROWS * NA_KROWS // 2, GRID_W, 2 * GRID_W), bias_map)],
        out_specs=pl.BlockSpec((None, q_tok, W_A), lambda rb, bi: (bi, rb, 0)),
        scratch_shapes=[pltpu.VMEM((2, part_rows, NA_KROWS * GRID_W + CTX_LEN), F32),
                        pltpu.VMEM((2, part_rows, LANES), F32)],
        compiler_params=_cparams(2),
        name="neighbourhood_attention",
    )(qa, ka, ka, ka, ka, va, va, va, va, ka, va, bias)


def _na_bias_table(rpb, rows):
    cols = np.arange(GRID_W)
    c0 = np.clip(cols - NA_WIN_W // 2, 0, GRID_W - NA_WIN_W)
    cc = cols[None, :]
    col_ok = (cc >= c0[:, None]) & (cc < c0[:, None] + NA_WIN_W)
    dc = np.clip(cc - cols[:, None] + (NA_WIN_W - 1), 0, 2 * NA_WIN_W - 2)
    e = jnp.where(col_ok[None, None], (rpb.astype(F32) * LOG2E)[:, :, dc], NEG)
    e = jnp.concatenate([e, jnp.full_like(e[:, :1], NEG)], axis=1)
    a = np.arange(NA_QROWS)[:, None]
    i = np.arange(NA_KROWS)[None, :]
    pats = []
    for r_base in (0, NA_QROWS, rows - NA_QROWS):
        r = r_base + a
        key_row = r_base - NA_WIN_H // 2 + i
        r0 = np.clip(r - NA_WIN_H // 2, 0, rows - NA_WIN_H)
        ok = (key_row >= r0) & (key_row < r0 + NA_WIN_H) & (key_row >= 0) & (key_row < rows)
        dr = np.where(ok, key_row - r + (NA_WIN_H - 1), 2 * NA_WIN_H - 1)
        pats.append(dr)
    dr_all = np.stack(pats)
    pairs = dr_all.reshape(-1, 2)
    uniq, inv = np.unique(pairs, axis=0, return_inverse=True)
    pair_blocks = jnp.concatenate([e[:, uniq[:, 0]], e[:, uniq[:, 1]]], axis=-1)
    t = pair_blocks[:, inv.reshape(-1)]
    return t.reshape(NA_HEADS, 3, NA_QROWS * NA_KROWS // 2, GRID_W, 2 * GRID_W)


def _outproj_kernel(x_ref, xc_ref, oa_ref, ob_ref, oc_ref, oac_ref, obc_ref, occ_ref, wa_ref, wb_ref, wc_ref,
                    mod_ref, gain_ref, wrh_ref, wrl_ref, br_ref, tri_ref,
                    x1_ref, tok_ref, route_ref, cnt_ref, run_ref, *, region, group_batches, n_lat_tiles):
    mod = mod_ref[...]
    is_ctx = pl.program_id(1) == n_lat_tiles
    def tile(lat_ref, ctx_ref):
        return jnp.where(is_ctx, ctx_ref[...], lat_ref[...])

    y = (_dot(tile(oa_ref, oac_ref), wa_ref[...]) + _dot(tile(ob_ref, obc_ref), wb_ref[...])
         + _dot(tile(oc_ref, occ_ref), wc_ref[...]))
    x1 = tile(x_ref, xc_ref) + mod[2:3] * y
    x1_ref[...] = x1
    ms = jnp.mean(x1 * x1, axis=-1, keepdims=True)
    t = (x1 * lax.rsqrt(ms + EPS)) * gain_ref[...]
    t = t * (1.0 + mod[4:5]) + mod[3:4]
    tok_ref[...] = _pack_bf16_pairs(t)

    t_hi, t_lo = _split_bf16(t)
    wrh = wrh_ref[...]
    logits = _dot(t_hi, wrh) + _dot(t_lo, wrh) + _dot(t_hi, wrl_ref[...]) + br_ref[...]

    lt = logits.T[:ROUTE_ROWS]
    row = lax.broadcasted_iota(jnp.int32, lt.shape, 0)
    row_f = row.astype(F32)
    is_g = row < N_GROUPS
    gl = jnp.where(is_g, lt, NEG)
    gmax = jnp.max(gl, axis=0, keepdims=True)
    g_sel = jnp.min(jnp.where(gl == gmax, row_f, 1e9), axis=0, keepdims=True)
    p_grp = 1.0 / jnp.sum(jnp.where(is_g, jnp.exp(gl - gmax), 0.0), axis=0, keepdims=True)
    grp_of_row = lax.shift_right_arithmetic(row - N_GROUPS, 2).astype(F32)
    in_grp = (row >= N_GROUPS) & (row < N_GROUPS + N_EXPERTS) & (grp_of_row == g_sel)
    el = jnp.where(in_grp, lt, NEG)
    v1 = jnp.max(el, axis=0, keepdims=True)
    i1 = jnp.min(jnp.where(el == v1, row_f, 1e9), axis=0, keepdims=True)
    el2 = jnp.where(row_f == i1, NEG, el)
    v2 = jnp.max(el2, axis=0, keepdims=True)
    i2 = jnp.min(jnp.where(el2 == v2, row_f, 1e9), axis=0, keepdims=True)
    e2 = jnp.exp(v2 - v1)
    den = 1.0 + e2
    w1 = p_grp / den
    w2 = p_grp * e2 / den

    @pl.when((lax.rem(pl.program_id(0), group_batches) == 0) & (pl.program_id(1) == 0))
    def _():
        run_ref[...] = jnp.zeros(run_ref.shape, F32)

    ind = jnp.where(row_f == i1, 1.0, 0.0) + jnp.where(row_f == i2, 1.0, 0.0)
    rank = _dot(ind.astype(BF16), tri_ref[...]) + run_ref[:, 0:1]

    def pick(m, r):
        return jnp.sum(jnp.where(row_f == r, m, 0.0), axis=0, keepdims=True)

    pos1 = (i1 - N_GROUPS) * region + pick(rank, i1)
    pos2 = (i2 - N_GROUPS) * region + pick(rank, i2)
    r = lax.broadcasted_iota(jnp.int32, (LANES, lt.shape[1]), 0)
    record = jnp.where(r == 0, pos1, jnp.where(r == 1, pos2, jnp.where(r == 2, w1, jnp.where(r == 3, w2, 0.0))))
    route_ref[...] = record.T
    run = run_ref[...] + jnp.sum(ind, axis=1, keepdims=True)
    run_ref[...] = run
    cnt_ref[...] = run


def _out_projection(x_lat, x_ctx, ctx_blk, mixed_lat, mixed_ctx, wa, wb, wc, modsel, gain, wrh, wrl, br,
                    n_tiles, n_lat_tiles):
    b, _, d = x_lat.shape
    tok = lambda bi, ti: (bi, ti, 0)
    const2 = lambda bi, ti: (0, 0)
    rows = n_tiles * TILE
    nb = b // MOE_GROUPS if b % MOE_GROUPS == 0 else b
    tri = jnp.asarray(np.triu(np.ones((TILE, TILE), np.float32), 1), BF16)
    lat_tile = lambda bi, ti: (bi, jnp.minimum(ti, n_lat_tiles - 1), 0)
    return pl.pallas_call(
        functools.partial(_outproj_kernel, region=nb * rows, group_batches=nb, n_lat_tiles=n_lat_tiles),
        out_shape=[jax.ShapeDtypeStruct((b, rows, d), F32),
                   jax.ShapeDtypeStruct((b, rows, d // 2), jnp.int32),
                   jax.ShapeDtypeStruct((b, rows, LANES), F32),
                   jax.ShapeDtypeStruct((ROUTE_ROWS * (b // nb), LANES), F32)],
        grid=(b, n_tiles),
        in_specs=_token_specs(d, n_lat_tiles, ctx_blk)
                 + [pl.BlockSpec((None, TILE, w), lat_tile) for w in (W_A, W_B, W_C)]
                 + [pl.BlockSpec((None, TILE, w), lambda bi, ti: (bi, 0, 0)) for w in (W_A, W_B, W_C)] + [
                  pl.BlockSpec((W_A, d), const2),
                  pl.BlockSpec((W_B, d), const2),
                  pl.BlockSpec((W_C, d), const2),
                  pl.BlockSpec((None, 6, d), lambda bi, ti: (2 * bi + (ti >= n_lat_tiles).astype(jnp.int32), 0, 0)),
                  pl.BlockSpec((1, d), const2),
                  pl.BlockSpec((d, LANES), const2),
                  pl.BlockSpec((d, LANES), const2),
                  pl.BlockSpec((1, LANES), const2),
                  pl.BlockSpec((TILE, TILE), const2)],
        out_specs=[pl.BlockSpec((None, TILE, d), tok),
                   pl.BlockSpec((None, TILE, d // 2), tok),
                   pl.BlockSpec((None, TILE, LANES), tok),
                   pl.BlockSpec((ROUTE_ROWS, LANES), lambda bi, ti: (bi // nb, 0))],
        scratch_shapes=[pltpu.VMEM((ROUTE_ROWS, LANES), F32)],
        compiler_params=_cparams(2),
        name="out_projection",
    )(x_lat, x_ctx, *mixed_lat, *mixed_ctx, wa, wb, wc, modsel, gain, wrh, wrl, br, tri)


def _sc_mesh():
    return plsc.VectorSubcoreMesh(core_axis_name="core", subcore_axis_name="subcore")


def _sc_worker_base(per_worker):
    wid = lax.axis_index("subcore") * SC_CORES + lax.axis_index("core")
    return wid * per_worker


def _sc_scratch(d, dtype):
    return ([pltpu.VMEM((SC_ROWS,), jnp.int32)] * SC_BUFS + [pltpu.VMEM((SC_ROWS, d), dtype)] * SC_BUFS
            + [pltpu.SemaphoreType.DMA] * (2 * SC_BUFS))


def _sc_split(scratch):
    return (scratch[:SC_BUFS], scratch[SC_BUFS:2 * SC_BUFS], scratch[2 * SC_BUFS:3 * SC_BUFS],
            scratch[3 * SC_BUFS:])


def _sc_chunk_loop(per_worker, group):
    chunks = per_worker // SC_ROWS
    full = chunks // SC_BUFS * SC_BUFS

    @pl.loop(0, full, step=SC_BUFS)
    def _(c):
        group(c, SC_BUFS)

    if chunks > full:
        group(full, chunks - full)


def _sc_scatter_rows(x, row_off, n, idx, n_out):
    d = x.shape[1]
    per_worker = 2 * n // (SC_CORES * SC_SUBCORES)
    assert per_worker % SC_ROWS == 0 and n % SC_ROWS == 0

    @functools.partial(pl.kernel, out_type=jax.ShapeDtypeStruct((n_out, d), x.dtype),
                       mesh=_sc_mesh(), scratch_types=_sc_scratch(d, x.dtype))
    def scatter(x_hbm, i_hbm, o_hbm, *scratch):
        idx_v, rows_v, sem_in, sem_out = _sc_split(scratch)
        base = _sc_worker_base(per_worker)

        def group(c, n_bufs):
            reads = []
            for u in range(n_bufs):
                a = pl.multiple_of(base + (c + u) * SC_ROWS, SC_ROWS)
                t = pl.multiple_of(row_off + lax.rem(a, n), SC_ROWS)
                pltpu.sync_copy(i_hbm.at[pl.ds(a, SC_ROWS)], idx_v[u])
                reads.append(pltpu.async_copy(x_hbm.at[pl.ds(t, SC_ROWS)], rows_v[u], sem_in[u]))
            writes = []
            for u in range(n_bufs):
                reads[u].wait()
                writes.append(pltpu.async_copy(rows_v[u], o_hbm.at[idx_v[u]], sem_out[u]))
            for w in writes:
                w.wait()

        _sc_chunk_loop(per_worker, group)

    return scatter(x, idx)


def _sc_gather_rows(src, idx):
    m = idx.shape[0]
    d = src.shape[1]
    per_worker = m // (SC_CORES * SC_SUBCORES)
    assert per_worker % SC_ROWS == 0

    @functools.partial(pl.kernel, out_type=jax.ShapeDtypeStruct((m, d), src.dtype),
                       mesh=_sc_mesh(), scratch_types=_sc_scratch(d, src.dtype))
    def gather(s_hbm, i_hbm, o_hbm, *scratch):
        idx_v, rows_v, sem_in, sem_out = _sc_split(scratch)
        base = _sc_worker_base(per_worker)

        def group(c, n_bufs):
            offs, reads = [], []
            for u in range(n_bufs):
                a = pl.multiple_of(base + (c + u) * SC_ROWS, SC_ROWS)
                offs.append(a)
                pltpu.sync_copy(i_hbm.at[pl.ds(a, SC_ROWS)], idx_v[u])
                reads.append(pltpu.async_copy(s_hbm.at[idx_v[u]], rows_v[u], sem_in[u]))
            writes = []
            for u in range(n_bufs):
                reads[u].wait()
                writes.append(pltpu.async_copy(rows_v[u], o_hbm.at[pl.ds(offs[u], SC_ROWS)], sem_out[u]))
            for w in writes:
                w.wait()

        _sc_chunk_loop(per_worker, group)

    return gather(src, idx)


def _expert_ffn_kernel(blk_ref, exp_ref, x_ref, wg_ref, wu_ref, wd_ref, y_ref, wgb_ref, wub_ref, wdb_ref):
    j = pl.program_id(0)

    @pl.when((j == 0) | (exp_ref[j] != exp_ref[jnp.maximum(j - 1, 0)]))
    def _():
        wgb_ref[...] = wg_ref[...].astype(BF16)
        wub_ref[...] = wu_ref[...].astype(BF16)
        wdb_ref[...] = wd_ref[...].astype(BF16)

    @pl.when((j == 0) | (blk_ref[j] != blk_ref[jnp.maximum(j - 1, 0)]))
    def _():
        x = _unpack_bf16_pairs(x_ref[...]).astype(BF16)
        hid = jax.nn.silu(_dot(x, wgb_ref[...])) * _dot(x, wub_ref[...])
        y_ref[...] = _pack_bf16_pairs(_dot(hid.astype(BF16), wdb_ref[...]))


def _expert_ffn(xs, blk, exp, wg, wu, wd, layer):
    rows, d_packed = xs.shape
    d = 2 * d_packed
    w_map = lambda j, blk, exp: (layer, exp[j], 0, 0)
    return pl.pallas_call(
        _expert_ffn_kernel,
        out_shape=jax.ShapeDtypeStruct((rows, d_packed), jnp.int32),
        grid_spec=pltpu.PrefetchScalarGridSpec(
            num_scalar_prefetch=2,
            grid=(blk.shape[0],),
            in_specs=[pl.BlockSpec((MOE_TILE, d_packed), lambda j, blk, exp: (blk[j], 0)),
                      pl.BlockSpec((None, None, d, EXPERT_HIDDEN), w_map),
                      pl.BlockSpec((None, None, d, EXPERT_HIDDEN), w_map),
                      pl.BlockSpec((None, None, EXPERT_HIDDEN, d), w_map)],
            out_specs=pl.BlockSpec((MOE_TILE, d_packed), lambda j, blk, exp: (blk[j], 0)),
            scratch_shapes=[pltpu.VMEM((d, EXPERT_HIDDEN), BF16),
                            pltpu.VMEM((d, EXPERT_HIDDEN), BF16),
                            pltpu.VMEM((EXPERT_HIDDEN, d), BF16)]),
        compiler_params=_cparams(1),
        name="expert_ffn",
    )(blk, exp, xs, wg, wu, wd)


def _combine_kernel(x1_ref, y1_ref, y2_ref, route_ref, mod_ref, fgain_ref, *rest, final):
    o_ref = rest[-1]
    route = route_ref[...]
    y = route[:, 2:3] * _unpack_bf16_pairs(y1_ref[...]) + route[:, 3:4] * _unpack_bf16_pairs(y2_ref[...])
    x2 = x1_ref[...] + mod_ref[5:6, :] * y
    if final:
        ms = jnp.mean(x2 * x2, axis=-1, keepdims=True)
        x2 = (x2 * lax.rsqrt(ms + EPS)) * fgain_ref[...]
    o_ref[...] = x2


def _combine(x1, ys, route, modsel, fgain, prev, b0, nb, n_lat_tiles, final):
    b, rows, d = x1.shape
    n_t = rows // TILE
    tok = lambda bi, ti: (b0 + bi, ti, 0)
    in_specs = [pl.BlockSpec((None, TILE, d), tok),
                pl.BlockSpec((TILE, d // 2), lambda bi, ti: (bi * n_t + ti, 0)),
                pl.BlockSpec((TILE, d // 2), lambda bi, ti: ((nb + bi) * n_t + ti, 0)),
                pl.BlockSpec((None, TILE, LANES), tok),
                pl.BlockSpec((None, 6, d),
                             lambda bi, ti: (2 * (b0 + bi) + (ti >= n_lat_tiles).astype(jnp.int32), 0, 0)),
                pl.BlockSpec((1, d), lambda bi, ti: (0, 0))]
    args = [x1, ys, ys, route, modsel, fgain]
    aliases = {}
    if prev is not None:
        in_specs.append(pl.BlockSpec(memory_space=pl.ANY))
        args.append(prev)
        aliases = {len(args) - 1: 0}
    return pl.pallas_call(
        functools.partial(_combine_kernel, final=final),
        out_shape=jax.ShapeDtypeStruct((b, rows, d), F32),
        grid=(nb, n_t),
        in_specs=in_specs,
        out_specs=pl.BlockSpec((None, TILE, d), tok),
        input_output_aliases=aliases,
        compiler_params=_cparams(2),
        name="moe_combine",
    )(*args)


def _routed_moe(tok, route, cnt, x1, wg, wu, wd, layer, modsel, fgain, n_lat_tiles, final):
    b, rows, d = x1.shape
    n_groups = cnt.shape[0] // ROUTE_ROWS
    nb = b // n_groups
    n = nb * rows
    flat = route.reshape(b * rows, LANES)
    tok_flat = tok.reshape(b * rows, tok.shape[2])
    out = None
    for g in range(n_groups):
        part = flat[g * n:(g + 1) * n]
        idx = jnp.concatenate([part[:, 0], part[:, 1]]).astype(jnp.int32)
        xs = _sc_scatter_rows(tok_flat, g * n, n, idx, N_EXPERTS * n)

        e0 = ROUTE_ROWS * g + N_GROUPS
        counts = cnt[e0:e0 + N_EXPERTS, 0].astype(jnp.int32)
        tiles = (counts + MOE_TILE - 1) // MOE_TILE
        ends = jnp.cumsum(tiles)
        n_sched = 2 * n // MOE_TILE + N_EXPERTS
        j = jnp.minimum(jnp.arange(n_sched, dtype=jnp.int32), ends[-1] - 1)
        exp = jnp.sum((j[:, None] >= ends[None, :]).astype(jnp.int32), axis=1)
        blk = exp * (n // MOE_TILE) + j - (ends - tiles)[exp]

        ys = _expert_ffn(xs, blk, exp, wg, wu, wd, layer)
        yg = _sc_gather_rows(ys, idx)
        out = _combine(x1, yg, route, modsel, fgain, out, g * nb, nb, n_lat_tiles, final)
    return out


def _rope_tables(n_lat):
    t = jnp.arange(n_lat)
    row = (t // GRID_W).astype(F32)
    col = (t % GRID_W).astype(F32)

    def cs(dim):
        quarter = dim // 4
        freqs = ROPE_THETA ** (-jnp.arange(quarter, dtype=F32) / quarter)
        ang = jnp.concatenate([row[:, None] * freqs, col[:, None] * freqs], axis=-1)
        cos = jnp.tile(jnp.cos(ang), (1, 2 * LANES // dim))
        sin = jnp.tile(jnp.sin(ang), (1, 2 * LANES // dim))
        cos = jnp.concatenate([cos, jnp.ones((CTX_LEN, LANES), F32)], axis=0)
        sin = jnp.concatenate([sin, jnp.zeros((CTX_LEN, LANES), F32)], axis=0)
        return cos, sin

    cos_b, sin_b = cs(DIFF_QK_DIM)
    cos_c, sin_c = cs(HEAD_DIM)
    return jnp.concatenate([cos_b, sin_b, cos_c, sin_c], axis=1)


def _reordered_w_in(w_in):
    o_c = 3 * W_A + 3 * W_B
    heads = [w_in[:, o_c + h * HEAD_DIM:o_c + (h + 1) * HEAD_DIM] for h in GQA_Q_ORDER]
    return jnp.concatenate([w_in[:, :o_c]] + heads + [w_in[:, o_c + W_C:]], axis=1).astype(BF16)


def kernel(x, c, ctx, c_ctx, w_mod, b_mod, norm_attn, norm_ffn, w_in, w_out, na_rpb, diff_lambda_q1, diff_lambda_k1, diff_lambda_q2, diff_lambda_k2, diff_subln, gqa_q_norm, gqa_k_norm, router_group_w, router_group_b, router_expert_w, router_expert_b, w_gate, w_up, w_down, final_norm):
    b, s, d = x.shape
    assert d == D_MODEL and ctx.shape[1] == CTX_LEN and s % (NA_QROWS * GRID_W) == 0
    rows = s // GRID_W
    assert rows >= 2 * NA_QROWS
    t_all = s + CTX_LEN
    n_lat_tiles = s // TILE

    assert b + 1 <= SUBLANES
    c_rows = jnp.zeros((SUBLANES, d), F32).at[:b].set(c).at[b].set(c_ctx)
    mod = _modulation(c_rows, w_mod, b_mod)

    tab = _rope_tables(s)
    hidx = np.arange(HEAD_DIM)
    partner = np.where(hidx < HEAD_DIM // 2, hidx + HEAD_DIM // 2, hidx - HEAD_DIM // 2)
    blk = np.arange(W_C) // HEAD_DIM
    ones = jnp.asarray((blk[:, None] == blk[None, :]).astype(np.float32), BF16)
    dummy_aux = jnp.zeros((SUBLANES, LANES), F32)

    x_lat, x_ctx, ctx_blk = x, ctx, 0
    for l in range(DEPTH):
        ctx_out = l < DEPTH - 1
        lam_init = 0.8 - 0.6 * math.exp(-0.3 * l)
        m_lat = mod[l, :b].reshape(b, 1, 6, d)
        m_ctx = jnp.broadcast_to(mod[l, b].reshape(1, 1, 6, d), (b, 1, 6, d))
        modsel = jnp.concatenate([m_lat, m_ctx], axis=1).reshape(2 * b, 6, d)

        gq = jnp.stack([jnp.tile(gqa_q_norm[l], GQA_Q_HEADS), jnp.tile(gqa_q_norm[l][partner], GQA_Q_HEADS)])
        gk = jnp.stack([jnp.tile(gqa_k_norm[l], GQA_KV_HEADS), jnp.tile(gqa_k_norm[l][partner], GQA_KV_HEADS)])
        qa, ka, va, qb, kb, vb, qc, kc, vc = _in_projection(
            x_lat, x_ctx, ctx_blk, modsel, norm_attn[l][None], _reordered_w_in(w_in[l]), tab, gq, gk, ones,
            n_lat_tiles)

        n_qt = n_lat_tiles + 1 if ctx_out else n_lat_tiles
        pad = lambda v: jnp.pad(v, (0, LANES - v.shape[0]))
        aux = jnp.stack([pad(diff_lambda_q1[l]), pad(diff_lambda_k1[l]), pad(diff_lambda_q2[l]),
                         pad(diff_lambda_k2[l]), jnp.tile(diff_subln[l], 2),
                         jnp.zeros((LANES,), F32), jnp.zeros((LANES,), F32), jnp.zeros((LANES,), F32)])
        group_a = dict(n_qblk=1, n_sub=2, n_hp=NA_HEADS // 2, n_lat=s, mode="plain")
        group_b = dict(n_qblk=1, n_sub=4, n_hp=DIFF_HEADS // 2, n_lat=s, mode="diff", lam_init=lam_init)
        group_c = dict(n_qblk=3, n_sub=2, n_hp=1, n_lat=s, mode="plain")
        oa = _neighbourhood_attention(qa, ka, va, _na_bias_table(na_rpb[l], rows), s)
        ob = _flash(qb, kb, vb, aux, queries="latent", **group_b)
        oc = _flash(qc, kc, vc, dummy_aux, queries="latent", **group_c)
        mixed_lat = mixed_ctx = (oa, ob, oc)
        if ctx_out:
            mixed_ctx = (_flash(qa, ka, va, dummy_aux, queries="context", **group_a),
                         _flash(qb, kb, vb, aux, queries="context", **group_b),
                         _flash(qc, kc, vc, dummy_aux, queries="context", **group_c))

        w_o = w_out[l]
        o_c = W_A + W_B
        w_oc = jnp.concatenate([w_o[o_c + h * HEAD_DIM:o_c + (h + 1) * HEAD_DIM] for h in GQA_Q_ORDER], axis=0)
        wr = jnp.zeros((d, LANES), F32)
        wr = wr.at[:, :N_GROUPS].set(router_group_w[l]).at[:, N_GROUPS:N_GROUPS + N_EXPERTS].set(router_expert_w[l])
        wrh, wrl = _split_bf16(wr)
        br = jnp.zeros((1, LANES), F32)
        br = br.at[0, :N_GROUPS].set(router_group_b[l]).at[0, N_GROUPS:N_GROUPS + N_EXPERTS].set(router_expert_b[l])
        x1, tok, route, cnt = _out_projection(
            x_lat, x_ctx, ctx_blk, mixed_lat, mixed_ctx, w_o[:W_A].astype(BF16), w_o[W_A:W_A + W_B].astype(BF16),
            w_oc.astype(BF16), modsel, norm_ffn[l][None], wrh, wrl, br, n_qt, n_lat_tiles)
        xs = _routed_moe(tok, route, cnt, x1, w_gate, w_up, w_down, l, modsel, final_norm[None],
                         n_lat_tiles, final=not ctx_out)
        x_lat, x_ctx, ctx_blk = xs, xs, n_lat_tiles
    return xs
```

```python
import functools
import math

import numpy as np
import jax
import jax.numpy as jnp
from jax import lax
from jax.experimental import pallas as pl
from jax.experimental.pallas import tpu as pltpu
from jax.experimental.pallas import tpu_sc as plsc

F32 = jnp.float32
BF16 = jnp.bfloat16

D_MODEL = 1024
DEPTH = 2
GRID_W = 64
CTX_LEN = 256
HEAD_DIM = 64
NA_HEADS = 6
NA_WIN_H = 8
NA_WIN_W = 16
DIFF_HEADS = 4
DIFF_QK_DIM = 32
GQA_Q_HEADS = 6
GQA_KV_HEADS = 2
N_GROUPS = 4
EXPERTS_PER_GROUP = 4
N_EXPERTS = 16
EXPERT_HIDDEN = 512
ROPE_THETA = 10000.0
EPS = 1e-6
W_A = NA_HEADS * HEAD_DIM
W_B = DIFF_HEADS * 2 * DIFF_QK_DIM
W_C = GQA_Q_HEADS * HEAD_DIM
W_KC = GQA_KV_HEADS * HEAD_DIM
IN_WIDTH = 3 * W_A + 3 * W_B + W_C + 2 * W_KC

LANES = 128
TILE = CTX_LEN
NA_QROWS = 8
NA_KROWS = 16
NA_PARTS = 2
NEG = -1e30
LOG2E = 1.4426950408889634
HI16 = -65536
VMEM_LIMIT = 56 * 1024 * 1024
FLASH_TK = 512
PAIRS_PER_STEP = 2
FLASH_UNROLL_MAX_ROWS = 1536
SUBLANES = 8
ROUTE_ROWS = 32
MOE_TILE = 512
MOE_GROUPS = 1
SC_ROWS = 32
SC_BUFS = 4
SC_CORES = 2
SC_SUBCORES = 16

GQA_Q_ORDER = (0, 3, 1, 4, 2, 5)


def _cparams(n_axes):
    return pltpu.CompilerParams(dimension_semantics=("arbitrary",) * n_axes,
                                vmem_limit_bytes=VMEM_LIMIT)


def _split_bf16(a):
    hi = a.astype(BF16)
    lo = (a - hi.astype(F32)).astype(BF16)
    return hi, lo


def _dot(a, b):
    return jnp.dot(a, b, preferred_element_type=F32)


def _pack_bf16_pairs(t):
    bits = lax.bitcast_convert_type(t.astype(BF16).astype(F32), jnp.int32)
    half_d = bits.shape[1] // 2
    return lax.shift_right_logical(bits[:, :half_d], 16) | (bits[:, half_d:] & HI16)


def _unpack_bf16_pairs(w):
    return jnp.concatenate([lax.bitcast_convert_type(lax.shift_left(w, 16), F32),
                            lax.bitcast_convert_type(w & HI16, F32)], axis=1)


def _dot_nt(a, b):
    return lax.dot_general(a, b, (((1,), (1,)), ((), ())), preferred_element_type=F32)


def _mod_kernel(c_ref, w_ref, b_ref, o_ref):
    c = c_ref[...]
    a = c * jax.nn.sigmoid(c)
    a_hi, a_lo = _split_bf16(a)
    w_hi, w_lo = _split_bf16(w_ref[...])
    o_ref[...] = _dot(a_hi, w_hi) + _dot(a_lo, w_hi) + _dot(a_hi, w_lo) + b_ref[...]


def _modulation(c_rows, w_mod, b_mod):
    depth, d, n = w_mod.shape
    bn = 1536
    return pl.pallas_call(
        _mod_kernel,
        out_shape=jax.ShapeDtypeStruct((depth, SUBLANES, n), F32),
        grid=(depth, n // bn),
        in_specs=[pl.BlockSpec((SUBLANES, d), lambda l, j: (0, 0)),
                  pl.BlockSpec((None, d, bn), lambda l, j: (l, 0, j)),
                  pl.BlockSpec((None, 1, bn), lambda l, j: (l, 0, j))],
        out_specs=pl.BlockSpec((None, SUBLANES, bn), lambda l, j: (l, 0, j)),
        compiler_params=_cparams(2),
        name="adaln_mod",
    )(c_rows, w_mod, b_mod.reshape(depth, 1, n))


def _head_mean_sq(t, ones):
    hi, lo = _split_bf16(t * t)
    return (_dot(hi, ones) + _dot(lo, ones)) * (1.0 / HEAD_DIM)


def _rotate_half(p, head):
    w = p.shape[1]
    half = head // 2
    lane = lax.broadcasted_iota(jnp.int32, (1, w), 1)
    first = (lane & (head - 1)) < half
    from_right = pltpu.roll(p, w - half, 1)
    from_left = pltpu.roll(p, half, 1)
    return jnp.where(first, -from_right, from_left)


def _inproj_kernel(x_ref, xc_ref, mod_ref, gain_ref, w_ref, tab_ref, gq_ref, gk_ref, ones_ref,
                   qa_ref, ka_ref, va_ref, qb_ref, kb_ref, vb_ref, qc_ref, kc_ref, vc_ref,
                   *, n_lat_tiles):
    x = jnp.where(pl.program_id(1) == n_lat_tiles, xc_ref[...], x_ref[...])
    mod = mod_ref[...]
    ms = jnp.mean(x * x, axis=-1, keepdims=True)
    h = (x * lax.rsqrt(ms + EPS)) * gain_ref[...]
    h = h * (1.0 + mod[1:2]) + mod[0:1]
    hb = h.astype(BF16)

    def proj(a, b):
        return _dot(hb, w_ref[:, a:b])

    pa = proj(0, 3 * W_A)
    qa_ref[...] = (pa[:, :W_A] * (HEAD_DIM ** -0.5 * LOG2E)).astype(BF16)
    ka_ref[...] = pa[:, W_A:2 * W_A].astype(BF16)
    va_ref[...] = pa[:, 2 * W_A:].astype(BF16)

    tab = tab_ref[...]
    cos_b = jnp.concatenate([tab[:, 0:LANES]] * 2, axis=1)
    sin_b = jnp.concatenate([tab[:, LANES:2 * LANES]] * 2, axis=1)
    cos_c1 = tab[:, 2 * LANES:3 * LANES]
    sin_c1 = tab[:, 3 * LANES:4 * LANES]
    cos_c = jnp.concatenate([cos_c1] * 3, axis=1)
    sin_c = jnp.concatenate([sin_c1] * 3, axis=1)

    o_b = 3 * W_A
    pb = proj(o_b, o_b + 3 * W_B)
    qb = pb[:, :W_B]
    kb = pb[:, W_B:2 * W_B]
    qb = qb * cos_b + _rotate_half(qb, DIFF_QK_DIM) * sin_b
    qb_ref[...] = (qb * (DIFF_QK_DIM ** -0.5 * LOG2E)).astype(BF16)
    kb_ref[...] = (kb * cos_b + _rotate_half(kb, DIFF_QK_DIM) * sin_b).astype(BF16)
    vb_ref[...] = pb[:, 2 * W_B:].astype(BF16)

    o_c = o_b + 3 * W_B
    pc = proj(o_c, IN_WIDTH)
    ones = ones_ref[...]
    qc = pc[:, :W_C]
    kc = pc[:, W_C:W_C + W_KC]
    nq = lax.rsqrt(_head_mean_sq(qc, ones) + EPS)
    nk = lax.rsqrt(_head_mean_sq(kc, ones[:W_KC, :W_KC]) + EPS)
    gq = gq_ref[...]
    gk = gk_ref[...]
    q = nq * (qc * gq[0:1] * cos_c + _rotate_half(qc, HEAD_DIM) * gq[1:2] * sin_c)
    qc_ref[...] = (q * (HEAD_DIM ** -0.5 * LOG2E)).astype(BF16)
    k = nk * (kc * gk[0:1] * cos_c1 + _rotate_half(kc, HEAD_DIM) * gk[1:2] * sin_c1)
    kc_ref[...] = k.astype(BF16)
    vc_ref[...] = pc[:, W_C + W_KC:].astype(BF16)


def _token_specs(d, n_lat_tiles, ctx_blk):
    return [pl.BlockSpec((None, TILE, d), lambda bi, ti: (bi, jnp.minimum(ti, n_lat_tiles - 1), 0)),
            pl.BlockSpec((None, TILE, d), lambda bi, ti: (bi, ctx_blk, 0))]


def _in_projection(x_lat, x_ctx, ctx_blk, modsel, gain, w_ext, tab, gq, gk, ones, n_lat_tiles):
    b, _, d = x_lat.shape
    n_tiles = n_lat_tiles + 1
    t_all = n_tiles * TILE
    widths = (W_A, W_A, W_A, W_B, W_B, W_B, W_C, W_KC, W_KC)
    tok = lambda bi, ti: (bi, ti, 0)
    const2 = lambda bi, ti: (0, 0)
    return pl.pallas_call(
        functools.partial(_inproj_kernel, n_lat_tiles=n_lat_tiles),
        out_shape=[jax.ShapeDtypeStruct((b, t_all, w), BF16) for w in widths],
        grid=(b, n_tiles),
        in_specs=_token_specs(d, n_lat_tiles, ctx_blk) + [
                  pl.BlockSpec((None, 6, d), lambda bi, ti: (2 * bi + (ti >= n_lat_tiles).astype(jnp.int32), 0, 0)),
                  pl.BlockSpec((1, d), const2),
                  pl.BlockSpec((d, IN_WIDTH), const2),
                  pl.BlockSpec((TILE, 4 * LANES), lambda bi, ti: (ti, 0)),
                  pl.BlockSpec((2, W_C), const2),
                  pl.BlockSpec((2, W_KC), const2),
                  pl.BlockSpec((W_C, W_C), const2)],
        out_specs=[pl.BlockSpec((None, TILE, w), tok) for w in widths],
        compiler_params=_cparams(2),
        name="in_projection",
    )(x_lat, x_ctx, modsel, gain, w_ext, tab, gq, gk, ones)


def _flash_kernel(q_ref, k_ref, v_ref, aux_ref, o_ref, va_ref, vb_ref, qs_ref, acc_ref, m_ref,
                  s0_ref, s1_ref, mb0_ref, mb1_ref, *,
                  n_qblk, n_sub, tk, n_lat_blocks, pairs_per_step, ctx_start, queries, mode, lam_init):
    sub_w = LANES // n_sub
    half = LANES // 2
    lane = lax.broadcasted_iota(jnp.int32, (1, LANES), 1)
    lower = lane < half
    n_pieces = n_qblk * n_sub
    ma = (n_pieces // 2) * TILE
    m_rows = n_pieces * TILE

    @pl.when(pl.program_id(2) == 0)
    def _():
        v = v_ref[...].astype(F32)
        va_ref[...] = jnp.where(lower, v, 1.0).astype(BF16)
        vb_ref[...] = jnp.where(lower, 1.0, v).astype(BF16)

    ia, ib = 0, n_pieces // 2
    for blk in range(n_qblk):
        qf = q_ref[:, blk * LANES:(blk + 1) * LANES].astype(F32)
        for sub in range(n_sub):
            msk = (lane >= sub * sub_w) & (lane < (sub + 1) * sub_w)
            piece = jnp.where(msk, qf, 0.0).astype(BF16)
            if sub * sub_w < half:
                qs_ref[ia * TILE:(ia + 1) * TILE, :] = piece
                ia += 1
            else:
                qs_ref[ib * TILE:(ib + 1) * TILE, :] = piece
                ib += 1

    s_bufs = (s0_ref, s1_ref)
    mb_bufs = (mb0_ref, mb1_ref)

    def scores(start, size, slot):
        s = _dot_nt(qs_ref[...], k_ref[pl.ds(start, size), :])
        s_bufs[slot][:, :size] = s
        mb = jnp.max(s, axis=-1, keepdims=True)
        mb_bufs[slot][...] = jnp.broadcast_to(mb, (m_rows, LANES))

    def accumulate(start, size, slot, first):
        mb = mb_bufs[slot][...]
        if first:
            m_new = mb
        else:
            m_old = m_ref[...]
            m_new = jnp.maximum(m_old, mb)
        s_ref = s_bufs[slot]
        cols = [s_ref[:, c * LANES:(c + 1) * LANES] - m_new for c in range(size // LANES)]
        p = jnp.concatenate([jnp.exp2(d.astype(BF16)) for d in cols], axis=1)
        pva = _dot(p[:ma], va_ref[pl.ds(start, size), :])
        pvb = _dot(p[ma:], vb_ref[pl.ds(start, size), :])
        if first:
            acc_ref[:ma, :] = pva
            acc_ref[ma:, :] = pvb
        else:
            alpha = jnp.exp2(m_old - m_new)
            acc_ref[:ma, :] = alpha[:ma] * acc_ref[:ma, :] + pva
            acc_ref[ma:, :] = alpha[ma:] * acc_ref[ma:, :] + pvb
        m_ref[...] = m_new

    def lat(j):
        return pl.multiple_of(j * tk, tk)

    def latent_queries():
        scores(ctx_start, CTX_LEN, 0)
        scores(lat(0), tk, 1)
        accumulate(ctx_start, CTX_LEN, 0, True)

        def pair(i):
            scores(lat(2 * i + 1), tk, 0)
            accumulate(lat(2 * i), tk, 1, False)
            scores(lat(2 * i + 2), tk, 1)
            accumulate(lat(2 * i + 1), tk, 0, False)

        def body(i, carry):
            for u in range(pairs_per_step):
                pair(i * pairs_per_step + u)
            return carry

        n_pairs = (n_lat_blocks - 2) // 2
        n_steps = n_pairs // pairs_per_step
        lax.fori_loop(0, n_steps, body, 0)
        for i in range(n_steps * pairs_per_step, n_pairs):
            pair(i)
        scores(lat(n_lat_blocks - 1), tk, 0)
        accumulate(lat(n_lat_blocks - 2), tk, 1, False)
        accumulate(lat(n_lat_blocks - 1), tk, 0, False)

    def context_queries():
        scores(ctx_start, CTX_LEN, 0)
        accumulate(ctx_start, CTX_LEN, 0, True)

    if queries == "latent":
        latent_queries()
    else:
        context_queries()

    acc = acc_ref[...]
    r = acc / pltpu.roll(acc, half, 1)
    ra, rb = r[:ma], r[ma:]
    if mode == "plain":
        for i in range(n_pieces // 2):
            o = jnp.where(lower, ra[i * TILE:(i + 1) * TILE], rb[i * TILE:(i + 1) * TILE])
            o_ref[:, i * LANES:(i + 1) * LANES] = o.astype(BF16)
    else:
        aux = aux_ref[...]
        l1 = jnp.sum(aux[0:1] * aux[1:2], axis=-1, keepdims=True)
        l2 = jnp.sum(aux[2:3] * aux[3:4], axis=-1, keepdims=True)
        lam = jnp.exp(l1) - jnp.exp(l2) + lam_init
        oa = ra[:TILE] - lam * ra[TILE:]
        ob = rb[:TILE] - lam * rb[TILE:]
        o = jnp.where(lower, oa, ob)
        sq = o * o
        ss_a = jnp.sum(jnp.where(lower, sq, 0.0), axis=-1, keepdims=True)
        ss_b = jnp.sum(jnp.where(lower, 0.0, sq), axis=-1, keepdims=True)
        ms = jnp.where(lower, ss_a, ss_b) * (1.0 / HEAD_DIM)
        o = (o * lax.rsqrt(ms + EPS)) * aux[4:5]
        o_ref[...] = (o * (1.0 - lam_init)).astype(BF16)


def _flash(q, k, v, aux, *, n_qblk, n_sub, n_hp, n_lat, queries, mode, lam_init=0.0):
    b, t_all, _ = q.shape
    qw = n_qblk * LANES
    tk = FLASH_TK
    assert n_lat % (2 * tk) == 0 and tk >= CTX_LEN
    m_rows = n_qblk * n_sub * TILE
    n_pairs = max((n_lat // tk - 2) // 2, 1)
    pairs = n_pairs if m_rows <= FLASH_UNROLL_MAX_ROWS else PAIRS_PER_STEP
    if queries == "latent":
        n_qt, qt_off, kv_rows, kv_blk, ctx_start = n_lat // TILE, 0, t_all, 0, n_lat
    else:
        n_qt, qt_off, kv_rows, kv_blk, ctx_start = 1, n_lat // TILE, CTX_LEN, n_lat // CTX_LEN, 0
    kern = functools.partial(_flash_kernel, n_qblk=n_qblk, n_sub=n_sub, tk=tk,
                             n_lat_blocks=n_lat // tk, ctx_start=ctx_start, queries=queries,
                             pairs_per_step=pairs, mode=mode, lam_init=lam_init)
    return pl.pallas_call(
        kern,
        out_shape=jax.ShapeDtypeStruct((b, n_qt * TILE, n_hp * qw), BF16),
        grid=(b, n_hp, n_qt),
        in_specs=[pl.BlockSpec((None, TILE, qw), lambda bi, hp, qt: (bi, qt + qt_off, hp)),
                  pl.BlockSpec((None, kv_rows, LANES), lambda bi, hp, qt: (bi, kv_blk, hp)),
                  pl.BlockSpec((None, kv_rows, LANES), lambda bi, hp, qt: (bi, kv_blk, hp)),
                  pl.BlockSpec((SUBLANES, LANES), lambda bi, hp, qt: (0, 0))],
        out_specs=pl.BlockSpec((None, TILE, qw), lambda bi, hp, qt: (bi, qt, hp)),
        scratch_shapes=[pltpu.VMEM((kv_rows, LANES), BF16),
                        pltpu.VMEM((kv_rows, LANES), BF16),
                        pltpu.VMEM((m_rows, LANES), BF16),
                        pltpu.VMEM((m_rows, LANES), F32),
                        pltpu.VMEM((m_rows, LANES), F32),
                        pltpu.VMEM((m_rows, tk), F32),
                        pltpu.VMEM((m_rows, tk), F32),
                        pltpu.VMEM((m_rows, LANES), F32),
                        pltpu.VMEM((m_rows, LANES), F32)],
        compiler_params=_cparams(3),
        name="flash_" + mode,
    )(q, k, v, aux)


def _na_kernel(q_ref, k0, k1, k2, k3, v0, v1, v2, v3, kc_ref, vc_ref, bias_ref, o_ref, s_ref, m_ref):
    lane = lax.broadcasted_iota(jnp.int32, (1, LANES), 1)
    lower = lane < LANES // 2
    n_pair = NA_KROWS // 2
    rows_per_part = NA_QROWS // NA_PARTS
    half_q = rows_per_part * GRID_W
    no_bias = jnp.zeros((GRID_W, CTX_LEN), F32)

    def head_pair(hp):
        cols = slice(hp * LANES, (hp + 1) * LANES)
        qf = q_ref[:, cols].astype(F32)
        k_all = jnp.concatenate([r[:, cols] for r in (k0, k1, k2, k3, kc_ref)], axis=0)
        v_all = jnp.concatenate([r[:, cols] for r in (v0, v1, v2, v3, vc_ref)], axis=0).astype(F32)
        v_h = [jnp.where(lower, v_all, 1.0).astype(BF16), jnp.where(lower, 1.0, v_all).astype(BF16)]
        q_h = [jnp.where(lower, qf, 0.0).astype(BF16), jnp.where(lower, 0.0, qf).astype(BF16)]
        return q_h, k_all, v_h

    pairs = [head_pair(hp) for hp in range(NA_HEADS // 2)]
    items = [(hp, part) for hp in range(NA_HEADS // 2) for part in range(NA_PARTS)]

    def scores(n):
        hp, part = items[n]
        q_h, k_all, _ = pairs[hp]
        rows = slice(part * half_q, (part + 1) * half_q)
        qs = jnp.concatenate([q_h[0][rows], q_h[1][rows]], axis=0)
        bias = jnp.concatenate(
            [jnp.concatenate([bias_ref[2 * hp + hh, a * n_pair + j] for j in range(n_pair)] + [no_bias],
                             axis=1)
             for hh in range(2) for a in range(part * rows_per_part, (part + 1) * rows_per_part)],
            axis=0)
        s = _dot_nt(qs, k_all) + bias
        s_ref[n % 2] = s
        m_ref[n % 2] = jnp.broadcast_to(jnp.max(s, axis=-1, keepdims=True), (2 * half_q, LANES))

    def finish(n):
        hp, part = items[n]
        v_h = pairs[hp][2]
        s = s_ref[n % 2]
        m = m_ref[n % 2]
        p = jnp.concatenate([jnp.exp2((s[:, c * LANES:(c + 1) * LANES] - m).astype(BF16))
                             for c in range(s.shape[1] // LANES)], axis=1)
        o0 = _dot(p[:half_q], v_h[0])
        o1 = _dot(p[half_q:], v_h[1])
        o0 = o0 / pltpu.roll(o0, LANES // 2, 1)
        o1 = o1 / pltpu.roll(o1, LANES // 2, 1)
        o_ref[part * half_q:(part + 1) * half_q, hp * LANES:(hp + 1) * LANES] = (
            jnp.where(lower, o0, o1).astype(BF16))

    scores(0)
    for n in range(1, len(items)):
        scores(n)
        finish(n - 1)
    finish(len(items) - 1)


def _neighbourhood_attention(qa, ka, va, bias, n_lat):
    b = qa.shape[0]
    q_tok = NA_QROWS * GRID_W
    v_tok = q_tok // 2
    n_rb = n_lat // q_tok
    n_view = n_lat // v_tok
    ctx_blk = n_lat // v_tok

    def view(j):
        return lambda rb, bi: (bi, jnp.clip(2 * rb - 1 + j, 0, n_view - 1), 0)

    kv_specs = [pl.BlockSpec((None, v_tok, W_A), view(j)) for j in range(4)]
    ctx_spec = pl.BlockSpec((None, CTX_LEN, W_A), lambda rb, bi: (bi, ctx_blk, 0))

    def bias_map(rb, bi):
        pat = jnp.where(rb == 0, 0, jnp.where(rb == n_rb - 1, 2, 1))
        return (0, pat, 0, 0, 0)

    part_rows = 2 * q_tok // NA_PARTS
    return pl.pallas_call(
        _na_kernel,
        out_shape=jax.ShapeDtypeStruct((b, n_lat, W_A), BF16),
        grid=(n_rb, b),
        in_specs=[pl.BlockSpec((None, q_tok, W_A), lambda rb, bi: (bi, rb, 0))]
                 + kv_specs + kv_specs + [ctx_spec, ctx_spec,
                 pl.BlockSpec((NA_HEADS, None, NA_QROWS * NA_KROWS // 2, GRID_W, 2 * GRID_W), bias_map)],
        out_specs=pl.BlockSpec((None, q_tok, W_A), lambda rb, bi: (bi, rb, 0)),
        scratch_shapes=[pltpu.VMEM((2, part_rows, NA_KROWS * GRID_W + CTX_LEN), F32),
                        pltpu.VMEM((2, part_rows, LANES), F32)],
        compiler_params=_cparams(2),
        name="neighbourhood_attention",
    )(qa, ka, ka, ka, ka, va, va, va, va, ka, va, bias)


def _na_bias_table(rpb, rows):
    cols = np.arange(GRID_W)
    c0 = np.clip(cols - NA_WIN_W // 2, 0, GRID_W - NA_WIN_W)
    cc = cols[None, :]
    col_ok = (cc >= c0[:, None]) & (cc < c0[:, None] + NA_WIN_W)
    dc = np.clip(cc - cols[:, None] + (NA_WIN_W - 1), 0, 2 * NA_WIN_W - 2)
    e = jnp.where(col_ok[None, None], (rpb.astype(F32) * LOG2E)[:, :, dc], NEG)
    e = jnp.concatenate([e, jnp.full_like(e[:, :1], NEG)], axis=1)
    a = np.arange(NA_QROWS)[:, None]
    i = np.arange(NA_KROWS)[None, :]
    pats = []
    for r_base in (0, NA_QROWS, rows - NA_QROWS):
        r = r_base + a
        key_row = r_base - NA_WIN_H // 2 + i
        r0 = np.clip(r - NA_WIN_H // 2, 0, rows - NA_WIN_H)
        ok = (key_row >= r0) & (key_row < r0 + NA_WIN_H) & (key_row >= 0) & (key_row < rows)
        dr = np.where(ok, key_row - r + (NA_WIN_H - 1), 2 * NA_WIN_H - 1)
        pats.append(dr)
    dr_all = np.stack(pats)
    pairs = dr_all.reshape(-1, 2)
    uniq, inv = np.unique(pairs, axis=0, return_inverse=True)
    pair_blocks = jnp.concatenate([e[:, uniq[:, 0]], e[:, uniq[:, 1]]], axis=-1)
    t = pair_blocks[:, inv.reshape(-1)]
    return t.reshape(NA_HEADS, 3, NA_QROWS * NA_KROWS // 2, GRID_W, 2 * GRID_W)


def _outproj_kernel(x_ref, xc_ref, oa_ref, ob_ref, oc_ref, oac_ref, obc_ref, occ_ref, wa_ref, wb_ref, wc_ref,
                    mod_ref, gain_ref, wrh_ref, wrl_ref, br_ref, tri_ref,
                    x1_ref, tok_ref, route_ref, cnt_ref, run_ref, *, region, group_batches, n_lat_tiles):
    mod = mod_ref[...]
    is_ctx = pl.program_id(1) == n_lat_tiles
    def tile(lat_ref, ctx_ref):
        return jnp.where(is_ctx, ctx_ref[...], lat_ref[...])

    y = (_dot(tile(oa_ref, oac_ref), wa_ref[...]) + _dot(tile(ob_ref, obc_ref), wb_ref[...])
         + _dot(tile(oc_ref, occ_ref), wc_ref[...]))
    x1 = tile(x_ref, xc_ref) + mod[2:3] * y
    x1_ref[...] = x1
    ms = jnp.mean(x1 * x1, axis=-1, keepdims=True)
    t = (x1 * lax.rsqrt(ms + EPS)) * gain_ref[...]
    t = t * (1.0 + mod[4:5]) + mod[3:4]
    tok_ref[...] = _pack_bf16_pairs(t)

    t_hi, t_lo = _split_bf16(t)
    wrh = wrh_ref[...]
    logits = _dot(t_hi, wrh) + _dot(t_lo, wrh) + _dot(t_hi, wrl_ref[...]) + br_ref[...]

    lt = logits.T[:ROUTE_ROWS]
    row = lax.broadcasted_iota(jnp.int32, lt.shape, 0)
    row_f = row.astype(F32)
    is_g = row < N_GROUPS
    gl = jnp.where(is_g, lt, NEG)
    gmax = jnp.max(gl, axis=0, keepdims=True)
    g_sel = jnp.min(jnp.where(gl == gmax, row_f, 1e9), axis=0, keepdims=True)
    p_grp = 1.0 / jnp.sum(jnp.where(is_g, jnp.exp(gl - gmax), 0.0), axis=0, keepdims=True)
    grp_of_row = lax.shift_right_arithmetic(row - N_GROUPS, 2).astype(F32)
    in_grp = (row >= N_GROUPS) & (row < N_GROUPS + N_EXPERTS) & (grp_of_row == g_sel)
    el = jnp.where(in_grp, lt, NEG)
    v1 = jnp.max(el, axis=0, keepdims=True)
    i1 = jnp.min(jnp.where(el == v1, row_f, 1e9), axis=0, keepdims=True)
    el2 = jnp.where(row_f == i1, NEG, el)
    v2 = jnp.max(el2, axis=0, keepdims=True)
    i2 = jnp.min(jnp.where(el2 == v2, row_f, 1e9), axis=0, keepdims=True)
    e2 = jnp.exp(v2 - v1)
    den = 1.0 + e2
    w1 = p_grp / den
    w2 = p_grp * e2 / den

    @pl.when((lax.rem(pl.program_id(0), group_batches) == 0) & (pl.program_id(1) == 0))
    def _():
        run_ref[...] = jnp.zeros(run_ref.shape, F32)

    ind = jnp.where(row_f == i1, 1.0, 0.0) + jnp.where(row_f == i2, 1.0, 0.0)
    rank = _dot(ind.astype(BF16), tri_ref[...]) + run_ref[:, 0:1]

    def pick(m, r):
        return jnp.sum(jnp.where(row_f == r, m, 0.0), axis=0, keepdims=True)

    pos1 = (i1 - N_GROUPS) * region + pick(rank, i1)
    pos2 = (i2 - N_GROUPS) * region + pick(rank, i2)
    r = lax.broadcasted_iota(jnp.int32, (LANES, lt.shape[1]), 0)
    record = jnp.where(r == 0, pos1, jnp.where(r == 1, pos2, jnp.where(r == 2, w1, jnp.where(r == 3, w2, 0.0))))
    route_ref[...] = record.T
    run = run_ref[...] + jnp.sum(ind, axis=1, keepdims=True)
    run_ref[...] = run
    cnt_ref[...] = run


def _out_projection(x_lat, x_ctx, ctx_blk, mixed_lat, mixed_ctx, wa, wb, wc, modsel, gain, wrh, wrl, br,
                    n_tiles, n_lat_tiles):
    b, _, d = x_lat.shape
    tok = lambda bi, ti: (bi, ti, 0)
    const2 = lambda bi, ti: (0, 0)
    rows = n_tiles * TILE
    nb = b // MOE_GROUPS if b % MOE_GROUPS == 0 else b
    tri = jnp.asarray(np.triu(np.ones((TILE, TILE), np.float32), 1), BF16)
    lat_tile = lambda bi, ti: (bi, jnp.minimum(ti, n_lat_tiles - 1), 0)
    return pl.pallas_call(
        functools.partial(_outproj_kernel, region=nb * rows, group_batches=nb, n_lat_tiles=n_lat_tiles),
        out_shape=[jax.ShapeDtypeStruct((b, rows, d), F32),
                   jax.ShapeDtypeStruct((b, rows, d // 2), jnp.int32),
                   jax.ShapeDtypeStruct((b, rows, LANES), F32),
                   jax.ShapeDtypeStruct((ROUTE_ROWS * (b // nb), LANES), F32)],
        grid=(b, n_tiles),
        in_specs=_token_specs(d, n_lat_tiles, ctx_blk)
                 + [pl.BlockSpec((None, TILE, w), lat_tile) for w in (W_A, W_B, W_C)]
                 + [pl.BlockSpec((None, TILE, w), lambda bi, ti: (bi, 0, 0)) for w in (W_A, W_B, W_C)] + [
                  pl.BlockSpec((W_A, d), const2),
                  pl.BlockSpec((W_B, d), const2),
                  pl.BlockSpec((W_C, d), const2),
                  pl.BlockSpec((None, 6, d), lambda bi, ti: (2 * bi + (ti >= n_lat_tiles).astype(jnp.int32), 0, 0)),
                  pl.BlockSpec((1, d), const2),
                  pl.BlockSpec((d, LANES), const2),
                  pl.BlockSpec((d, LANES), const2),
                  pl.BlockSpec((1, LANES), const2),
                  pl.BlockSpec((TILE, TILE), const2)],
        out_specs=[pl.BlockSpec((None, TILE, d), tok),
                   pl.BlockSpec((None, TILE, d // 2), tok),
                   pl.BlockSpec((None, TILE, LANES), tok),
                   pl.BlockSpec((ROUTE_ROWS, LANES), lambda bi, ti: (bi // nb, 0))],
        scratch_shapes=[pltpu.VMEM((ROUTE_ROWS, LANES), F32)],
        compiler_params=_cparams(2),
        name="out_projection",
    )(x_lat, x_ctx, *mixed_lat, *mixed_ctx, wa, wb, wc, modsel, gain, wrh, wrl, br, tri)


def _sc_mesh():
    return plsc.VectorSubcoreMesh(core_axis_name="core", subcore_axis_name="subcore")


def _sc_worker_base(per_worker):
    wid = lax.axis_index("subcore") * SC_CORES + lax.axis_index("core")
    return wid * per_worker


def _sc_scratch(d, dtype):
    return ([pltpu.VMEM((SC_ROWS,), jnp.int32)] * SC_BUFS + [pltpu.VMEM((SC_ROWS, d), dtype)] * SC_BUFS
            + [pltpu.SemaphoreType.DMA] * (2 * SC_BUFS))


def _sc_split(scratch):
    return (scratch[:SC_BUFS], scratch[SC_BUFS:2 * SC_BUFS], scratch[2 * SC_BUFS:3 * SC_BUFS],
            scratch[3 * SC_BUFS:])


def _sc_chunk_loop(per_worker, group):
    chunks = per_worker // SC_ROWS
    full = chunks // SC_BUFS * SC_BUFS

    @pl.loop(0, full, step=SC_BUFS)
    def _(c):
        group(c, SC_BUFS)

    if chunks > full:
        group(full, chunks - full)


def _sc_scatter_rows(x, row_off, n, idx, n_out):
    d = x.shape[1]
    per_worker = 2 * n // (SC_CORES * SC_SUBCORES)
    assert per_worker % SC_ROWS == 0 and n % SC_ROWS == 0

    @functools.partial(pl.kernel, out_type=jax.ShapeDtypeStruct((n_out, d), x.dtype),
                       mesh=_sc_mesh(), scratch_types=_sc_scratch(d, x.dtype))
    def scatter(x_hbm, i_hbm, o_hbm, *scratch):
        idx_v, rows_v, sem_in, sem_out = _sc_split(scratch)
        base = _sc_worker_base(per_worker)

        def group(c, n_bufs):
            reads = []
            for u in range(n_bufs):
                a = pl.multiple_of(base + (c + u) * SC_ROWS, SC_ROWS)
                t = pl.multiple_of(row_off + lax.rem(a, n), SC_ROWS)
                pltpu.sync_copy(i_hbm.at[pl.ds(a, SC_ROWS)], idx_v[u])
                reads.append(pltpu.async_copy(x_hbm.at[pl.ds(t, SC_ROWS)], rows_v[u], sem_in[u]))
            writes = []
            for u in range(n_bufs):
                reads[u].wait()
                writes.append(pltpu.async_copy(rows_v[u], o_hbm.at[idx_v[u]], sem_out[u]))
            for w in writes:
                w.wait()

        _sc_chunk_loop(per_worker, group)

    return scatter(x, idx)


def _sc_gather_rows(src, idx):
    m = idx.shape[0]
    d = src.shape[1]
    per_worker = m // (SC_CORES * SC_SUBCORES)
    assert per_worker % SC_ROWS == 0

    @functools.partial(pl.kernel, out_type=jax.ShapeDtypeStruct((m, d), src.dtype),
                       mesh=_sc_mesh(), scratch_types=_sc_scratch(d, src.dtype))
    def gather(s_hbm, i_hbm, o_hbm, *scratch):
        idx_v, rows_v, sem_in, sem_out = _sc_split(scratch)
        base = _sc_worker_base(per_worker)

        def group(c, n_bufs):
            offs, reads = [], []
            for u in range(n_bufs):
                a = pl.multiple_of(base + (c + u) * SC_ROWS, SC_ROWS)
                offs.append(a)
                pltpu.sync_copy(i_hbm.at[pl.ds(a, SC_ROWS)], idx_v[u])
                reads.append(pltpu.async_copy(s_hbm.at[idx_v[u]], rows_v[u], sem_in[u]))
            writes = []
            for u in range(n_bufs):
                reads[u].wait()
                writes.append(pltpu.async_copy(rows_v[u], o_hbm.at[pl.ds(offs[u], SC_ROWS)], sem_out[u]))
            for w in writes:
                w.wait()

        _sc_chunk_loop(per_worker, group)

    return gather(src, idx)


def _expert_ffn_kernel(blk_ref, exp_ref, x_ref, wg_ref, wu_ref, wd_ref, y_ref, wgb_ref, wub_ref, wdb_ref):
    j = pl.program_id(0)

    @pl.when((j == 0) | (exp_ref[j] != exp_ref[jnp.maximum(j - 1, 0)]))
    def _():
        wgb_ref[...] = wg_ref[...].astype(BF16)
        wub_ref[...] = wu_ref[...].astype(BF16)
        wdb_ref[...] = wd_ref[...].astype(BF16)

    @pl.when((j == 0) | (blk_ref[j] != blk_ref[jnp.maximum(j - 1, 0)]))
    def _():
        x = _unpack_bf16_pairs(x_ref[...]).astype(BF16)
        hid = jax.nn.silu(_dot(x, wgb_ref[...])) * _dot(x, wub_ref[...])
        y_ref[...] = _pack_bf16_pairs(_dot(hid.astype(BF16), wdb_ref[...]))


def _expert_ffn(xs, blk, exp, wg, wu, wd, layer):
    rows, d_packed = xs.shape
    d = 2 * d_packed
    w_map = lambda j, blk, exp: (layer, exp[j], 0, 0)
    return pl.pallas_call(
        _expert_ffn_kernel,
        out_shape=jax.ShapeDtypeStruct((rows, d_packed), jnp.int32),
        grid_spec=pltpu.PrefetchScalarGridSpec(
            num_scalar_prefetch=2,
            grid=(blk.shape[0],),
            in_specs=[pl.BlockSpec((MOE_TILE, d_packed), lambda j, blk, exp: (blk[j], 0)),
                      pl.BlockSpec((None, None, d, EXPERT_HIDDEN), w_map),
                      pl.BlockSpec((None, None, d, EXPERT_HIDDEN), w_map),
                      pl.BlockSpec((None, None, EXPERT_HIDDEN, d), w_map)],
            out_specs=pl.BlockSpec((MOE_TILE, d_packed), lambda j, blk, exp: (blk[j], 0)),
            scratch_shapes=[pltpu.VMEM((d, EXPERT_HIDDEN), BF16),
                            pltpu.VMEM((d, EXPERT_HIDDEN), BF16),
                            pltpu.VMEM((EXPERT_HIDDEN, d), BF16)]),
        compiler_params=_cparams(1),
        name="expert_ffn",
    )(blk, exp, xs, wg, wu, wd)


def _combine_kernel(x1_ref, y1_ref, y2_ref, route_ref, mod_ref, fgain_ref, *rest, final):
    o_ref = rest[-1]
    route = route_ref[...]
    y = route[:, 2:3] * _unpack_bf16_pairs(y1_ref[...]) + route[:, 3:4] * _unpack_bf16_pairs(y2_ref[...])
    x2 = x1_ref[...] + mod_ref[5:6, :] * y
    if final:
        ms = jnp.mean(x2 * x2, axis=-1, keepdims=True)
        x2 = (x2 * lax.rsqrt(ms + EPS)) * fgain_ref[...]
    o_ref[...] = x2


def _combine(x1, ys, route, modsel, fgain, prev, b0, nb, n_lat_tiles, final):
    b, rows, d = x1.shape
    n_t = rows // TILE
    tok = lambda bi, ti: (b0 + bi, ti, 0)
    in_specs = [pl.BlockSpec((None, TILE, d), tok),
                pl.BlockSpec((TILE, d // 2), lambda bi, ti: (bi * n_t + ti, 0)),
                pl.BlockSpec((TILE, d // 2), lambda bi, ti: ((nb + bi) * n_t + ti, 0)),
                pl.BlockSpec((None, TILE, LANES), tok),
                pl.BlockSpec((None, 6, d),
                             lambda bi, ti: (2 * (b0 + bi) + (ti >= n_lat_tiles).astype(jnp.int32), 0, 0)),
                pl.BlockSpec((1, d), lambda bi, ti: (0, 0))]
    args = [x1, ys, ys, route, modsel, fgain]
    aliases = {}
    if prev is not None:
        in_specs.append(pl.BlockSpec(memory_space=pl.ANY))
        args.append(prev)
        aliases = {len(args) - 1: 0}
    return pl.pallas_call(
        functools.partial(_combine_kernel, final=final),
        out_shape=jax.ShapeDtypeStruct((b, rows, d), F32),
        grid=(nb, n_t),
        in_specs=in_specs,
        out_specs=pl.BlockSpec((None, TILE, d), tok),
        input_output_aliases=aliases,
        compiler_params=_cparams(2),
        name="moe_combine",
    )(*args)


def _routed_moe(tok, route, cnt, x1, wg, wu, wd, layer, modsel, fgain, n_lat_tiles, final):
    b, rows, d = x1.shape
    n_groups = cnt.shape[0] // ROUTE_ROWS
    nb = b // n_groups
    n = nb * rows
    flat = route.reshape(b * rows, LANES)
    tok_flat = tok.reshape(b * rows, tok.shape[2])
    out = None
    for g in range(n_groups):
        part = flat[g * n:(g + 1) * n]
        idx = jnp.concatenate([part[:, 0], part[:, 1]]).astype(jnp.int32)
        xs = _sc_scatter_rows(tok_flat, g * n, n, idx, N_EXPERTS * n)

        e0 = ROUTE_ROWS * g + N_GROUPS
        counts = cnt[e0:e0 + N_EXPERTS, 0].astype(jnp.int32)
        tiles = (counts + MOE_TILE - 1) // MOE_TILE
        ends = jnp.cumsum(tiles)
        n_sched = 2 * n // MOE_TILE + N_EXPERTS
        j = jnp.minimum(jnp.arange(n_sched, dtype=jnp.int32), ends[-1] - 1)
        exp = jnp.sum((j[:, None] >= ends[None, :]).astype(jnp.int32), axis=1)
        blk = exp * (n // MOE_TILE) + j - (ends - tiles)[exp]

        ys = _expert_ffn(xs, blk, exp, wg, wu, wd, layer)
        yg = _sc_gather_rows(ys, idx)
        out = _combine(x1, yg, route, modsel, fgain, out, g * nb, nb, n_lat_tiles, final)
    return out


def _rope_tables(n_lat):
    t = jnp.arange(n_lat)
    row = (t // GRID_W).astype(F32)
    col = (t % GRID_W).astype(F32)

    def cs(dim):
        quarter = dim // 4
        freqs = ROPE_THETA ** (-jnp.arange(quarter, dtype=F32) / quarter)
        ang = jnp.concatenate([row[:, None] * freqs, col[:, None] * freqs], axis=-1)
        cos = jnp.tile(jnp.cos(ang), (1, 2 * LANES // dim))
        sin = jnp.tile(jnp.sin(ang), (1, 2 * LANES // dim))
        cos = jnp.concatenate([cos, jnp.ones((CTX_LEN, LANES), F32)], axis=0)
        sin = jnp.concatenate([sin, jnp.zeros((CTX_LEN, LANES), F32)], axis=0)
        return cos, sin

    cos_b, sin_b = cs(DIFF_QK_DIM)
    cos_c, sin_c = cs(HEAD_DIM)
    return jnp.concatenate([cos_b, sin_b, cos_c, sin_c], axis=1)


def _reordered_w_in(w_in):
    o_c = 3 * W_A + 3 * W_B
    heads = [w_in[:, o_c + h * HEAD_DIM:o_c + (h + 1) * HEAD_DIM] for h in GQA_Q_ORDER]
    return jnp.concatenate([w_in[:, :o_c]] + heads + [w_in[:, o_c + W_C:]], axis=1).astype(BF16)


def kernel(x, c, ctx, c_ctx, w_mod, b_mod, norm_attn, norm_ffn, w_in, w_out, na_rpb, diff_lambda_q1, diff_lambda_k1, diff_lambda_q2, diff_lambda_k2, diff_subln, gqa_q_norm, gqa_k_norm, router_group_w, router_group_b, router_expert_w, router_expert_b, w_gate, w_up, w_down, final_norm):
    b, s, d = x.shape
    assert d == D_MODEL and ctx.shape[1] == CTX_LEN and s % (NA_QROWS * GRID_W) == 0
    rows = s // GRID_W
    assert rows >= 2 * NA_QROWS
    t_all = s + CTX_LEN
    n_lat_tiles = s // TILE

    assert b + 1 <= SUBLANES
    c_rows = jnp.zeros((SUBLANES, d), F32).at[:b].set(c).at[b].set(c_ctx)
    mod = _modulation(c_rows, w_mod, b_mod)

    tab = _rope_tables(s)
    hidx = np.arange(HEAD_DIM)
    partner = np.where(hidx < HEAD_DIM // 2, hidx + HEAD_DIM // 2, hidx - HEAD_DIM // 2)
    blk = np.arange(W_C) // HEAD_DIM
    ones = jnp.asarray((blk[:, None] == blk[None, :]).astype(np.float32), BF16)
    dummy_aux = jnp.zeros((SUBLANES, LANES), F32)

    x_lat, x_ctx, ctx_blk = x, ctx, 0
    for l in range(DEPTH):
        ctx_out = l < DEPTH - 1
        lam_init = 0.8 - 0.6 * math.exp(-0.3 * l)
        m_lat = mod[l, :b].reshape(b, 1, 6, d)
        m_ctx = jnp.broadcast_to(mod[l, b].reshape(1, 1, 6, d), (b, 1, 6, d))
        modsel = jnp.concatenate([m_lat, m_ctx], axis=1).reshape(2 * b, 6, d)

        gq = jnp.stack([jnp.tile(gqa_q_norm[l], GQA_Q_HEADS), jnp.tile(gqa_q_norm[l][partner], GQA_Q_HEADS)])
        gk = jnp.stack([jnp.tile(gqa_k_norm[l], GQA_KV_HEADS), jnp.tile(gqa_k_norm[l][partner], GQA_KV_HEADS)])
        qa, ka, va, qb, kb, vb, qc, kc, vc = _in_projection(
            x_lat, x_ctx, ctx_blk, modsel, norm_attn[l][None], _reordered_w_in(w_in[l]), tab, gq, gk, ones,
            n_lat_tiles)

        n_qt = n_lat_tiles + 1 if ctx_out else n_lat_tiles
        pad = lambda v: jnp.pad(v, (0, LANES - v.shape[0]))
        aux = jnp.stack([pad(diff_lambda_q1[l]), pad(diff_lambda_k1[l]), pad(diff_lambda_q2[l]),
                         pad(diff_lambda_k2[l]), jnp.tile(diff_subln[l], 2),
                         jnp.zeros((LANES,), F32), jnp.zeros((LANES,), F32), jnp.zeros((LANES,), F32)])
        group_a = dict(n_qblk=1, n_sub=2, n_hp=NA_HEADS // 2, n_lat=s, mode="plain")
        group_b = dict(n_qblk=1, n_sub=4, n_hp=DIFF_HEADS // 2, n_lat=s, mode="diff", lam_init=lam_init)
        group_c = dict(n_qblk=3, n_sub=2, n_hp=1, n_lat=s, mode="plain")
        oa = _neighbourhood_attention(qa, ka, va, _na_bias_table(na_rpb[l], rows), s)
        ob = _flash(qb, kb, vb, aux, queries="latent", **group_b)
        oc = _flash(qc, kc, vc, dummy_aux, queries="latent", **group_c)
        mixed_lat = mixed_ctx = (oa, ob, oc)
        if ctx_out:
            mixed_ctx = (_flash(qa, ka, va, dummy_aux, queries="context", **group_a),
                         _flash(qb, kb, vb, aux, queries="context", **group_b),
                         _flash(qc, kc, vc, dummy_aux, queries="context", **group_c))

        w_o = w_out[l]
        o_c = W_A + W_B
        w_oc = jnp.concatenate([w_o[o_c + h * HEAD_DIM:o_c + (h + 1) * HEAD_DIM] for h in GQA_Q_ORDER], axis=0)
        wr = jnp.zeros((d, LANES), F32)
        wr = wr.at[:, :N_GROUPS].set(router_group_w[l]).at[:, N_GROUPS:N_GROUPS + N_EXPERTS].set(router_expert_w[l])
        wrh, wrl = _split_bf16(wr)
        br = jnp.zeros((1, LANES), F32)
        br = br.at[0, :N_GROUPS].set(router_group_b[l]).at[0, N_GROUPS:N_GROUPS + N_EXPERTS].set(router_expert_b[l])
        x1, tok, route, cnt = _out_projection(
            x_lat, x_ctx, ctx_blk, mixed_lat, mixed_ctx, w_o[:W_A].astype(BF16), w_o[W_A:W_A + W_B].astype(BF16),
            w_oc.astype(BF16), modsel, norm_ffn[l][None], wrh, wrl, br, n_qt, n_lat_tiles)
        xs = _routed_moe(tok, route, cnt, x1, w_gate, w_up, w_down, l, modsel, final_norm[None],
                         n_lat_tiles, final=not ctx_out)
        x_lat, x_ctx, ctx_blk = xs, xs, n_lat_tiles
    return xs
```

```python
import functools
import math

import numpy as np
import jax
import jax.numpy as jnp
from jax import lax
from jax.experimental import pallas as pl
from jax.experimental.pallas import tpu as pltpu
from jax.experimental.pallas import tpu_sc as plsc

F32 = jnp.float32
BF16 = jnp.bfloat16

D_MODEL = 1024
DEPTH = 2
GRID_W = 64
CTX_LEN = 256
HEAD_DIM = 64
NA_HEADS = 6
NA_WIN_H = 8
NA_WIN_W = 16
DIFF_HEADS = 4
DIFF_QK_DIM = 32
GQA_Q_HEADS = 6
GQA_KV_HEADS = 2
N_GROUPS = 4
EXPERTS_PER_GROUP = 4
N_EXPERTS = 16
EXPERT_HIDDEN = 512
ROPE_THETA = 10000.0
EPS = 1e-6
W_A = NA_HEADS * HEAD_DIM
W_B = DIFF_HEADS * 2 * DIFF_QK_DIM
W_C = GQA_Q_HEADS * HEAD_DIM
W_KC = GQA_KV_HEADS * HEAD_DIM
IN_WIDTH = 3 * W_A + 3 * W_B + W_C + 2 * W_KC

LANES = 128
TILE = CTX_LEN
NA_QROWS = 8
NA_KROWS = 16
NA_PARTS = 2
NEG = -1e30
LOG2E = 1.4426950408889634
HI16 = -65536
VMEM_LIMIT = 56 * 1024 * 1024
FLASH_TK = 512
PV_CHUNK = 256
PAIRS_PER_STEP = 2
FLASH_UNROLL_MAX_ROWS = 1536
SUBLANES = 8
ROUTE_ROWS = 32
MOE_TILE = 512
MOE_GROUPS = 2
SC_ROWS = 32
SC_BUFS = 4
SC_CORES = 2
SC_SUBCORES = 16

GQA_Q_ORDER = (0, 3, 1, 4, 2, 5)


def _cparams(n_axes):
    return pltpu.CompilerParams(dimension_semantics=("arbitrary",) * n_axes,
                                vmem_limit_bytes=VMEM_LIMIT)


def _split_bf16(a):
    hi = a.astype(BF16)
    lo = (a - hi.astype(F32)).astype(BF16)
    return hi, lo


def _dot(a, b):
    return jnp.dot(a, b, preferred_element_type=F32)


def _pack_bf16_pairs(t):
    bits = lax.bitcast_convert_type(t.astype(BF16).astype(F32), jnp.int32)
    half_d = bits.shape[1] // 2
    return lax.shift_right_logical(bits[:, :half_d], 16) | (bits[:, half_d:] & HI16)


def _unpack_bf16_pairs(w):
    return jnp.concatenate([lax.bitcast_convert_type(lax.shift_left(w, 16), F32),
                            lax.bitcast_convert_type(w & HI16, F32)], axis=1)


def _dot_nt(a, b):
    return lax.dot_general(a, b, (((1,), (1,)), ((), ())), preferred_element_type=F32)


def _mod_kernel(c_ref, w_ref, b_ref, o_ref):
    c = c_ref[...]
    a = c * jax.nn.sigmoid(c)
    a_hi, a_lo = _split_bf16(a)
    w_hi, w_lo = _split_bf16(w_ref[...])
    o_ref[...] = _dot(a_hi, w_hi) + _dot(a_lo, w_hi) + _dot(a_hi, w_lo) + b_ref[...]


def _modulation(c_rows, w_mod, b_mod):
    depth, d, n = w_mod.shape
    bn = 1536
    return pl.pallas_call(
        _mod_kernel,
        out_shape=jax.ShapeDtypeStruct((depth, SUBLANES, n), F32),
        grid=(depth, n // bn),
        in_specs=[pl.BlockSpec((SUBLANES, d), lambda l, j: (0, 0)),
                  pl.BlockSpec((None, d, bn), lambda l, j: (l, 0, j)),
                  pl.BlockSpec((None, 1, bn), lambda l, j: (l, 0, j))],
        out_specs=pl.BlockSpec((None, SUBLANES, bn), lambda l, j: (l, 0, j)),
        compiler_params=_cparams(2),
        name="adaln_mod",
    )(c_rows, w_mod, b_mod.reshape(depth, 1, n))


def _head_mean_sq(t, ones):
    hi, lo = _split_bf16(t * t)
    return (_dot(hi, ones) + _dot(lo, ones)) * (1.0 / HEAD_DIM)


def _rotate_half(p, head):
    w = p.shape[1]
    half = head // 2
    lane = lax.broadcasted_iota(jnp.int32, (1, w), 1)
    first = (lane & (head - 1)) < half
    from_right = pltpu.roll(p, w - half, 1)
    from_left = pltpu.roll(p, half, 1)
    return jnp.where(first, -from_right, from_left)


def _inproj_kernel(x_ref, xc_ref, mod_ref, gain_ref, w_ref, tab_ref, gq_ref, gk_ref, ones_ref,
                   qa_ref, ka_ref, va_ref, qb_ref, kb_ref, vb_ref, qc_ref, kc_ref, vc_ref,
                   *, n_lat_tiles):
    x = jnp.where(pl.program_id(1) == n_lat_tiles, xc_ref[...], x_ref[...])
    mod = mod_ref[...]
    ms = jnp.mean(x * x, axis=-1, keepdims=True)
    h = (x * lax.rsqrt(ms + EPS)) * gain_ref[...]
    h = h * (1.0 + mod[1:2]) + mod[0:1]
    hb = h.astype(BF16)

    def proj(a, b):
        return _dot(hb, w_ref[:, a:b])

    pa = proj(0, 3 * W_A)
    qa_ref[...] = (pa[:, :W_A] * (HEAD_DIM ** -0.5 * LOG2E)).astype(BF16)
    ka_ref[...] = pa[:, W_A:2 * W_A].astype(BF16)
    va_ref[...] = pa[:, 2 * W_A:].astype(BF16)

    tab = tab_ref[...]
    cos_b = jnp.concatenate([tab[:, 0:LANES]] * 2, axis=1)
    sin_b = jnp.concatenate([tab[:, LANES:2 * LANES]] * 2, axis=1)
    cos_c1 = tab[:, 2 * LANES:3 * LANES]
    sin_c1 = tab[:, 3 * LANES:4 * LANES]
    cos_c = jnp.concatenate([cos_c1] * 3, axis=1)
    sin_c = jnp.concatenate([sin_c1] * 3, axis=1)

    o_b = 3 * W_A
    pb = proj(o_b, o_b + 3 * W_B)
    qb = pb[:, :W_B]
    kb = pb[:, W_B:2 * W_B]
    qb = qb * cos_b + _rotate_half(qb, DIFF_QK_DIM) * sin_b
    qb_ref[...] = (qb * (DIFF_QK_DIM ** -0.5 * LOG2E)).astype(BF16)
    kb_ref[...] = (kb * cos_b + _rotate_half(kb, DIFF_QK_DIM) * sin_b).astype(BF16)
    vb_ref[...] = pb[:, 2 * W_B:].astype(BF16)

    o_c = o_b + 3 * W_B
    pc = proj(o_c, IN_WIDTH)
    ones = ones_ref[...]
    qc = pc[:, :W_C]
    kc = pc[:, W_C:W_C + W_KC]
    nq = lax.rsqrt(_head_mean_sq(qc, ones) + EPS)
    nk = lax.rsqrt(_head_mean_sq(kc, ones[:W_KC, :W_KC]) + EPS)
    gq = gq_ref[...]
    gk = gk_ref[...]
    q = nq * (qc * gq[0:1] * cos_c + _rotate_half(qc, HEAD_DIM) * gq[1:2] * sin_c)
    qc_ref[...] = (q * (HEAD_DIM ** -0.5 * LOG2E)).astype(BF16)
    k = nk * (kc * gk[0:1] * cos_c1 + _rotate_half(kc, HEAD_DIM) * gk[1:2] * sin_c1)
    kc_ref[...] = k.astype(BF16)
    vc_ref[...] = pc[:, W_C + W_KC:].astype(BF16)


def _token_specs(d, n_lat_tiles, ctx_blk):
    return [pl.BlockSpec((None, TILE, d), lambda bi, ti: (bi, jnp.minimum(ti, n_lat_tiles - 1), 0)),
            pl.BlockSpec((None, TILE, d), lambda bi, ti: (bi, ctx_blk, 0))]


def _in_projection(x_lat, x_ctx, ctx_blk, modsel, gain, w_ext, tab, gq, gk, ones, n_lat_tiles):
    b, _, d = x_lat.shape
    n_tiles = n_lat_tiles + 1
    t_all = n_tiles * TILE
    widths = (W_A, W_A, W_A, W_B, W_B, W_B, W_C, W_KC, W_KC)
    tok = lambda bi, ti: (bi, ti, 0)
    const2 = lambda bi, ti: (0, 0)
    return pl.pallas_call(
        functools.partial(_inproj_kernel, n_lat_tiles=n_lat_tiles),
        out_shape=[jax.ShapeDtypeStruct((b, t_all, w), BF16) for w in widths],
        grid=(b, n_tiles),
        in_specs=_token_specs(d, n_lat_tiles, ctx_blk) + [
                  pl.BlockSpec((None, 6, d), lambda bi, ti: (2 * bi + (ti >= n_lat_tiles).astype(jnp.int32), 0, 0)),
                  pl.BlockSpec((1, d), const2),
                  pl.BlockSpec((d, IN_WIDTH), const2),
                  pl.BlockSpec((TILE, 4 * LANES), lambda bi, ti: (ti, 0)),
                  pl.BlockSpec((2, W_C), const2),
                  pl.BlockSpec((2, W_KC), const2),
                  pl.BlockSpec((W_C, W_C), const2)],
        out_specs=[pl.BlockSpec((None, TILE, w), tok) for w in widths],
        compiler_params=_cparams(2),
        name="in_projection",
    )(x_lat, x_ctx, modsel, gain, w_ext, tab, gq, gk, ones)


def _flash_kernel(q_ref, k_ref, v_ref, aux_ref, o_ref, va_ref, vb_ref, qs_ref, acc_ref, m_ref,
                  s0_ref, s1_ref, mb0_ref, mb1_ref, *,
                  n_qblk, n_sub, tk, n_lat_blocks, pairs_per_step, ctx_start, queries, mode, lam_init):
    sub_w = LANES // n_sub
    half = LANES // 2
    lane = lax.broadcasted_iota(jnp.int32, (1, LANES), 1)
    lower = lane < half
    n_pieces = n_qblk * n_sub
    ma = (n_pieces // 2) * TILE
    m_rows = n_pieces * TILE

    @pl.when(pl.program_id(2) == 0)
    def _():
        v = v_ref[...].astype(F32)
        va_ref[...] = jnp.where(lower, v, 1.0).astype(BF16)
        vb_ref[...] = jnp.where(lower, 1.0, v).astype(BF16)

    ia, ib = 0, n_pieces // 2
    for blk in range(n_qblk):
        qf = q_ref[:, blk * LANES:(blk + 1) * LANES].astype(F32)
        for sub in range(n_sub):
            msk = (lane >= sub * sub_w) & (lane < (sub + 1) * sub_w)
            piece = jnp.where(msk, qf, 0.0).astype(BF16)
            if sub * sub_w < half:
                qs_ref[ia * TILE:(ia + 1) * TILE, :] = piece
                ia += 1
            else:
                qs_ref[ib * TILE:(ib + 1) * TILE, :] = piece
                ib += 1

    s_bufs = (s0_ref, s1_ref)
    mb_bufs = (mb0_ref, mb1_ref)

    def scores(start, size, slot):
        s = _dot_nt(qs_ref[...], k_ref[pl.ds(start, size), :])
        s_bufs[slot][:, :size] = s
        mb = jnp.max(s, axis=-1, keepdims=True)
        mb_bufs[slot][...] = jnp.broadcast_to(mb, (m_rows, LANES))

    def accumulate(start, size, slot, first):
        mb = mb_bufs[slot][...]
        if first:
            m_new = mb
        else:
            m_old = m_ref[...]
            m_new = jnp.maximum(m_old, mb)
        s_ref = s_bufs[slot]
        pva = pvb = None
        for c0 in range(0, size, PV_CHUNK):
            p = jnp.concatenate(
                [jnp.exp2((s_ref[:, c:c + LANES] - m_new).astype(BF16)) for c in range(c0, c0 + PV_CHUNK, LANES)],
                axis=1)
            a = _dot(p[:ma], va_ref[pl.ds(start + c0, PV_CHUNK), :])
            b = _dot(p[ma:], vb_ref[pl.ds(start + c0, PV_CHUNK), :])
            pva = a if pva is None else pva + a
            pvb = b if pvb is None else pvb + b
        if first:
            acc_ref[:ma, :] = pva
            acc_ref[ma:, :] = pvb
        else:
            alpha = jnp.exp2(m_old - m_new)
            acc_ref[:ma, :] = alpha[:ma] * acc_ref[:ma, :] + pva
            acc_ref[ma:, :] = alpha[ma:] * acc_ref[ma:, :] + pvb
        m_ref[...] = m_new

    def lat(j):
        return pl.multiple_of(j * tk, tk)

    def latent_queries():
        scores(ctx_start, CTX_LEN, 0)
        scores(lat(0), tk, 1)
        accumulate(ctx_start, CTX_LEN, 0, True)

        def pair(i):
            scores(lat(2 * i + 1), tk, 0)
            accumulate(lat(2 * i), tk, 1, False)
            scores(lat(2 * i + 2), tk, 1)
            accumulate(lat(2 * i + 1), tk, 0, False)

        def body(i, carry):
            for u in range(pairs_per_step):
                pair(i * pairs_per_step + u)
            return carry

        n_pairs = (n_lat_blocks - 2) // 2
        n_steps = n_pairs // pairs_per_step
        lax.fori_loop(0, n_steps, body, 0)
        for i in range(n_steps * pairs_per_step, n_pairs):
            pair(i)
        scores(lat(n_lat_blocks - 1), tk, 0)
        accumulate(lat(n_lat_blocks - 2), tk, 1, False)
        accumulate(lat(n_lat_blocks - 1), tk, 0, False)

    def context_queries():
        scores(ctx_start, CTX_LEN, 0)
        accumulate(ctx_start, CTX_LEN, 0, True)

    if queries == "latent":
        latent_queries()
    else:
        context_queries()

    acc = acc_ref[...]
    r = acc / pltpu.roll(acc, half, 1)
    ra, rb = r[:ma], r[ma:]
    if mode == "plain":
        for i in range(n_pieces // 2):
            o = jnp.where(lower, ra[i * TILE:(i + 1) * TILE], rb[i * TILE:(i + 1) * TILE])
            o_ref[:, i * LANES:(i + 1) * LANES] = o.astype(BF16)
    else:
        aux = aux_ref[...]
        l1 = jnp.sum(aux[0:1] * aux[1:2], axis=-1, keepdims=True)
        l2 = jnp.sum(aux[2:3] * aux[3:4], axis=-1, keepdims=True)
        lam = jnp.exp(l1) - jnp.exp(l2) + lam_init
        oa = ra[:TILE] - lam * ra[TILE:]
        ob = rb[:TILE] - lam * rb[TILE:]
        o = jnp.where(lower, oa, ob)
        sq = o * o
        ss_a = jnp.sum(jnp.where(lower, sq, 0.0), axis=-1, keepdims=True)
        ss_b = jnp.sum(jnp.where(lower, 0.0, sq), axis=-1, keepdims=True)
        ms = jnp.where(lower, ss_a, ss_b) * (1.0 / HEAD_DIM)
        o = (o * lax.rsqrt(ms + EPS)) * aux[4:5]
        o_ref[...] = (o * (1.0 - lam_init)).astype(BF16)


def _flash(q, k, v, aux, *, n_qblk, n_sub, n_hp, n_lat, queries, mode, lam_init=0.0):
    b, t_all, _ = q.shape
    qw = n_qblk * LANES
    tk = FLASH_TK
    assert n_lat % (2 * tk) == 0 and tk >= CTX_LEN
    m_rows = n_qblk * n_sub * TILE
    n_pairs = max((n_lat // tk - 2) // 2, 1)
    pairs = n_pairs if m_rows <= FLASH_UNROLL_MAX_ROWS else PAIRS_PER_STEP
    if queries == "latent":
        n_qt, qt_off, kv_rows, kv_blk, ctx_start = n_lat // TILE, 0, t_all, 0, n_lat
    else:
        n_qt, qt_off, kv_rows, kv_blk, ctx_start = 1, n_lat // TILE, CTX_LEN, n_lat // CTX_LEN, 0
    kern = functools.partial(_flash_kernel, n_qblk=n_qblk, n_sub=n_sub, tk=tk,
                             n_lat_blocks=n_lat // tk, ctx_start=ctx_start, queries=queries,
                             pairs_per_step=pairs, mode=mode, lam_init=lam_init)
    return pl.pallas_call(
        kern,
        out_shape=jax.ShapeDtypeStruct((b, n_qt * TILE, n_hp * qw), BF16),
        grid=(b, n_hp, n_qt),
        in_specs=[pl.BlockSpec((None, TILE, qw), lambda bi, hp, qt: (bi, qt + qt_off, hp)),
                  pl.BlockSpec((None, kv_rows, LANES), lambda bi, hp, qt: (bi, kv_blk, hp)),
                  pl.BlockSpec((None, kv_rows, LANES), lambda bi, hp, qt: (bi, kv_blk, hp)),
                  pl.BlockSpec((SUBLANES, LANES), lambda bi, hp, qt: (0, 0))],
        out_specs=pl.BlockSpec((None, TILE, qw), lambda bi, hp, qt: (bi, qt, hp)),
        scratch_shapes=[pltpu.VMEM((kv_rows, LANES), BF16),
                        pltpu.VMEM((kv_rows, LANES), BF16),
                        pltpu.VMEM((m_rows, LANES), BF16),
                        pltpu.VMEM((m_rows, LANES), F32),
                        pltpu.VMEM((m_rows, LANES), F32),
                        pltpu.VMEM((m_rows, tk), F32),
                        pltpu.VMEM((m_rows, tk), F32),
                        pltpu.VMEM((m_rows, LANES), F32),
                        pltpu.VMEM((m_rows, LANES), F32)],
        compiler_params=_cparams(3),
        name="flash_" + mode,
    )(q, k, v, aux)


def _na_kernel(q_ref, k0, k1, k2, k3, v0, v1, v2, v3, kc_ref, vc_ref, bias_ref, o_ref, s_ref, m_ref):
    lane = lax.broadcasted_iota(jnp.int32, (1, LANES), 1)
    lower = lane < LANES // 2
    n_pair = NA_KROWS // 2
    rows_per_part = NA_QROWS // NA_PARTS
    half_q = rows_per_part * GRID_W
    no_bias = jnp.zeros((GRID_W, CTX_LEN), F32)

    def head_pair(hp):
        cols = slice(hp * LANES, (hp + 1) * LANES)
        qf = q_ref[:, cols].astype(F32)
        k_all = jnp.concatenate([r[:, cols] for r in (k0, k1, k2, k3, kc_ref)], axis=0)
        v_all = jnp.concatenate([r[:, cols] for r in (v0, v1, v2, v3, vc_ref)], axis=0).astype(F32)
        v_h = [jnp.where(lower, v_all, 1.0).astype(BF16), jnp.where(lower, 1.0, v_all).astype(BF16)]
        q_h = [jnp.where(lower, qf, 0.0).astype(BF16), jnp.where(lower, 0.0, qf).astype(BF16)]
        return q_h, k_all, v_h

    pairs = [head_pair(hp) for hp in range(NA_HEADS // 2)]
    items = [(hp, part) for hp in range(NA_HEADS // 2) for part in range(NA_PARTS)]

    def scores(n):
        hp, part = items[n]
        q_h, k_all, _ = pairs[hp]
        rows = slice(part * half_q, (part + 1) * half_q)
        qs = jnp.concatenate([q_h[0][rows], q_h[1][rows]], axis=0)
        bias = jnp.concatenate(
            [jnp.concatenate([bias_ref[2 * hp + hh, a * n_pair + j] for j in range(n_pair)] + [no_bias],
                             axis=1)
             for hh in range(2) for a in range(part * rows_per_part, (part + 1) * rows_per_part)],
            axis=0)
        s = _dot_nt(qs, k_all) + bias
        s_ref[n % 2] = s
        m_ref[n % 2] = jnp.broadcast_to(jnp.max(s, axis=-1, keepdims=True), (2 * half_q, LANES))

    def finish(n):
        hp, part = items[n]
        v_h = pairs[hp][2]
        s = s_ref[n % 2]
        m = m_ref[n % 2]
        p = jnp.concatenate([jnp.exp2((s[:, c * LANES:(c + 1) * LANES] - m).astype(BF16))
                             for c in range(s.shape[1] // LANES)], axis=1)
        o0 = _dot(p[:half_q], v_h[0])
        o1 = _dot(p[half_q:], v_h[1])
        o0 = o0 / pltpu.roll(o0, LANES // 2, 1)
        o1 = o1 / pltpu.roll(o1, LANES // 2, 1)
        o_ref[part * half_q:(part + 1) * half_q, hp * LANES:(hp + 1) * LANES] = (
            jnp.where(lower, o0, o1).astype(BF16))

    scores(0)
    for n in range(1, len(items)):
        scores(n)
        finish(n - 1)
    finish(len(items) - 1)


def _neighbourhood_attention(qa, ka, va, bias, n_lat):
    b = qa.shape[0]
    q_tok = NA_QROWS * GRID_W
    v_tok = q_tok // 2
    n_rb = n_lat // q_tok
    n_view = n_lat // v_tok
    ctx_blk = n_lat // v_tok

    def view(j):
        return lambda rb, bi: (bi, jnp.clip(2 * rb - 1 + j, 0, n_view - 1), 0)

    kv_specs = [pl.BlockSpec((None, v_tok, W_A), view(j)) for j in range(4)]
    ctx_spec = pl.BlockSpec((None, CTX_LEN, W_A), lambda rb, bi: (bi, ctx_blk, 0))

    def bias_map(rb, bi):
        pat = jnp.where(rb == 0, 0, jnp.where(rb == n_rb - 1, 2, 1))
        return (0, pat, 0, 0, 0)

    part_rows = 2 * q_tok // NA_PARTS
    return pl.pallas_call(
        _na_kernel,
        out_shape=jax.ShapeDtypeStruct((b, n_lat, W_A), BF16),
        grid=(n_rb, b),
        in_specs=[pl.BlockSpec((None, q_tok, W_A), lambda rb, bi: (bi, rb, 0))]
                 + kv_specs + kv_specs + [ctx_spec, ctx_spec,
                 pl.BlockSpec((NA_HEADS, None, NA_QROWS * NA_KROWS // 2, GRID_W, 2 * GRID_W), bias_map)],
        out_specs=pl.BlockSpec((None, q_tok, W_A), lambda rb, bi: (bi, rb, 0)),
        scratch_shapes=[pltpu.VMEM((2, part_rows, NA_KROWS * GRID_W + CTX_LEN), F32),
                        pltpu.VMEM((2, part_rows, LANES), F32)],
        compiler_params=_cparams(2),
        name="neighbourhood_attention",
    )(qa, ka, ka, ka, ka, va, va, va, va, ka, va, bias)


def _na_bias_table(rpb, rows):
    cols = np.arange(GRID_W)
    c0 = np.clip(cols - NA_WIN_W // 2, 0, GRID_W - NA_WIN_W)
    cc = cols[None, :]
    col_ok = (cc >= c0[:, None]) & (cc < c0[:, None] + NA_WIN_W)
    dc = np.clip(cc - cols[:, None] + (NA_WIN_W - 1), 0, 2 * NA_WIN_W - 2)
    e = jnp.where(col_ok[None, None], (rpb.astype(F32) * LOG2E)[:, :, dc], NEG)
    e = jnp.concatenate([e, jnp.full_like(e[:, :1], NEG)], axis=1)
    a = np.arange(NA_QROWS)[:, None]
    i = np.arange(NA_KROWS)[None, :]
    pats = []
    for r_base in (0, NA_QROWS, rows - NA_QROWS):
        r = r_base + a
        key_row = r_base - NA_WIN_H // 2 + i
        r0 = np.clip(r - NA_WIN_H // 2, 0, rows - NA_WIN_H)
        ok = (key_row >= r0) & (key_row < r0 + NA_WIN_H) & (key_row >= 0) & (key_row < rows)
        dr = np.where(ok, key_row - r + (NA_WIN_H - 1), 2 * NA_WIN_H - 1)
        pats.append(dr)
    dr_all = np.stack(pats)
    pairs = dr_all.reshape(-1, 2)
    uniq, inv = np.unique(pairs, axis=0, return_inverse=True)
    pair_blocks = jnp.concatenate([e[:, uniq[:, 0]], e[:, uniq[:, 1]]], axis=-1)
    t = pair_blocks[:, inv.reshape(-1)]
    return t.reshape(NA_HEADS, 3, NA_QROWS * NA_KROWS // 2, GRID_W, 2 * GRID_W)


def _outproj_kernel(x_ref, xc_ref, oa_ref, ob_ref, oc_ref, oac_ref, obc_ref, occ_ref, wa_ref, wb_ref, wc_ref,
                    mod_ref, gain_ref, wrh_ref, wrl_ref, br_ref, tri_ref,
                    x1_ref, tok_ref, route_ref, cnt_ref, run_ref, *, region, group_batches, n_lat_tiles):
    mod = mod_ref[...]
    is_ctx = pl.program_id(1) == n_lat_tiles
    def tile(lat_ref, ctx_ref):
        return jnp.where(is_ctx, ctx_ref[...], lat_ref[...])

    y = (_dot(tile(oa_ref, oac_ref), wa_ref[...]) + _dot(tile(ob_ref, obc_ref), wb_ref[...])
         + _dot(tile(oc_ref, occ_ref), wc_ref[...]))
    x1 = tile(x_ref, xc_ref) + mod[2:3] * y
    x1_ref[...] = x1
    ms = jnp.mean(x1 * x1, axis=-1, keepdims=True)
    t = (x1 * lax.rsqrt(ms + EPS)) * gain_ref[...]
    t = t * (1.0 + mod[4:5]) + mod[3:4]
    tok_ref[...] = _pack_bf16_pairs(t)

    t_hi, t_lo = _split_bf16(t)
    wrh = wrh_ref[...]
    logits = _dot(t_hi, wrh) + _dot(t_lo, wrh) + _dot(t_hi, wrl_ref[...]) + br_ref[...]

    lt = logits.T[:ROUTE_ROWS]
    row = lax.broadcasted_iota(jnp.int32, lt.shape, 0)
    row_f = row.astype(F32)
    is_g = row < N_GROUPS
    gl = jnp.where(is_g, lt, NEG)
    gmax = jnp.max(gl, axis=0, keepdims=True)
    g_sel = jnp.min(jnp.where(gl == gmax, row_f, 1e9), axis=0, keepdims=True)
    p_grp = 1.0 / jnp.sum(jnp.where(is_g, jnp.exp(gl - gmax), 0.0), axis=0, keepdims=True)
    grp_of_row = lax.shift_right_arithmetic(row - N_GROUPS, 2).astype(F32)
    in_grp = (row >= N_GROUPS) & (row < N_GROUPS + N_EXPERTS) & (grp_of_row == g_sel)
    el = jnp.where(in_grp, lt, NEG)
    v1 = jnp.max(el, axis=0, keepdims=True)
    i1 = jnp.min(jnp.where(el == v1, row_f, 1e9), axis=0, keepdims=True)
    el2 = jnp.where(row_f == i1, NEG, el)
    v2 = jnp.max(el2, axis=0, keepdims=True)
    i2 = jnp.min(jnp.where(el2 == v2, row_f, 1e9), axis=0, keepdims=True)
    e2 = jnp.exp(v2 - v1)
    den = 1.0 + e2
    w1 = p_grp / den
    w2 = p_grp * e2 / den

    @pl.when((lax.rem(pl.program_id(0), group_batches) == 0) & (pl.program_id(1) == 0))
    def _():
        run_ref[...] = jnp.zeros(run_ref.shape, F32)

    ind = jnp.where(row_f == i1, 1.0, 0.0) + jnp.where(row_f == i2, 1.0, 0.0)
    rank = _dot(ind.astype(BF16), tri_ref[...]) + run_ref[:, 0:1]

    def pick(m, r):
        return jnp.sum(jnp.where(row_f == r, m, 0.0), axis=0, keepdims=True)

    pos1 = (i1 - N_GROUPS) * region + pick(rank, i1)
    pos2 = (i2 - N_GROUPS) * region + pick(rank, i2)
    r = lax.broadcasted_iota(jnp.int32, (LANES, lt.shape[1]), 0)
    record = jnp.where(r == 0, pos1, jnp.where(r == 1, pos2, jnp.where(r == 2, w1, jnp.where(r == 3, w2, 0.0))))
    route_ref[...] = record.T
    run = run_ref[...] + jnp.sum(ind, axis=1, keepdims=True)
    run_ref[...] = run
    cnt_ref[...] = run


def _out_projection(x_lat, x_ctx, ctx_blk, mixed_lat, mixed_ctx, wa, wb, wc, modsel, gain, wrh, wrl, br,
                    n_tiles, n_lat_tiles):
    b, _, d = x_lat.shape
    tok = lambda bi, ti: (bi, ti, 0)
    const2 = lambda bi, ti: (0, 0)
    rows = n_tiles * TILE
    nb = b // MOE_GROUPS if b % MOE_GROUPS == 0 else b
    tri = jnp.asarray(np.triu(np.ones((TILE, TILE), np.float32), 1), BF16)
    lat_tile = lambda bi, ti: (bi, jnp.minimum(ti, n_lat_tiles - 1), 0)
    return pl.pallas_call(
        functools.partial(_outproj_kernel, region=nb * rows, group_batches=nb, n_lat_tiles=n_lat_tiles),
        out_shape=[jax.ShapeDtypeStruct((b, rows, d), F32),
                   jax.ShapeDtypeStruct((b, rows, d // 2), jnp.int32),
                   jax.ShapeDtypeStruct((b, rows, LANES), F32),
                   jax.ShapeDtypeStruct((ROUTE_ROWS * (b // nb), LANES), F32)],
        grid=(b, n_tiles),
        in_specs=_token_specs(d, n_lat_tiles, ctx_blk)
                 + [pl.BlockSpec((None, TILE, w), lat_tile) for w in (W_A, W_B, W_C)]
                 + [pl.BlockSpec((None, TILE, w), lambda bi, ti: (bi, 0, 0)) for w in (W_A, W_B, W_C)] + [
                  pl.BlockSpec((W_A, d), const2),
                  pl.BlockSpec((W_B, d), const2),
                  pl.BlockSpec((W_C, d), const2),
                  pl.BlockSpec((None, 6, d), lambda bi, ti: (2 * bi + (ti >= n_lat_tiles).astype(jnp.int32), 0, 0)),
                  pl.BlockSpec((1, d), const2),
                  pl.BlockSpec((d, LANES), const2),
                  pl.BlockSpec((d, LANES), const2),
                  pl.BlockSpec((1, LANES), const2),
                  pl.BlockSpec((TILE, TILE), const2)],
        out_specs=[pl.BlockSpec((None, TILE, d), tok),
                   pl.BlockSpec((None, TILE, d // 2), tok),
                   pl.BlockSpec((None, TILE, LANES), tok),
                   pl.BlockSpec((ROUTE_ROWS, LANES), lambda bi, ti: (bi // nb, 0))],
        scratch_shapes=[pltpu.VMEM((ROUTE_ROWS, LANES), F32)],
        compiler_params=_cparams(2),
        name="out_projection",
    )(x_lat, x_ctx, *mixed_lat, *mixed_ctx, wa, wb, wc, modsel, gain, wrh, wrl, br, tri)


def _sc_mesh():
    return plsc.VectorSubcoreMesh(core_axis_name="core", subcore_axis_name="subcore")


def _sc_worker_base(per_worker):
    wid = lax.axis_index("subcore") * SC_CORES + lax.axis_index("core")
    return wid * per_worker


def _sc_scratch(d, dtype):
    return ([pltpu.VMEM((SC_ROWS,), jnp.int32)] * SC_BUFS + [pltpu.VMEM((SC_ROWS, d), dtype)] * SC_BUFS
            + [pltpu.SemaphoreType.DMA] * (2 * SC_BUFS))


def _sc_split(scratch):
    return (scratch[:SC_BUFS], scratch[SC_BUFS:2 * SC_BUFS], scratch[2 * SC_BUFS:3 * SC_BUFS],
            scratch[3 * SC_BUFS:])


def _sc_chunk_loop(per_worker, group):
    chunks = per_worker // SC_ROWS
    full = chunks // SC_BUFS * SC_BUFS

    @pl.loop(0, full, step=SC_BUFS)
    def _(c):
        group(c, SC_BUFS)

    if chunks > full:
        group(full, chunks - full)


def _sc_scatter_rows(x, row_off, n, idx, n_out):
    d = x.shape[1]
    per_worker = 2 * n // (SC_CORES * SC_SUBCORES)
    assert per_worker % SC_ROWS == 0 and n % SC_ROWS == 0

    @functools.partial(pl.kernel, out_type=jax.ShapeDtypeStruct((n_out, d), x.dtype),
                       mesh=_sc_mesh(), scratch_types=_sc_scratch(d, x.dtype))
    def scatter(x_hbm, i_hbm, o_hbm, *scratch):
        idx_v, rows_v, sem_in, sem_out = _sc_split(scratch)
        base = _sc_worker_base(per_worker)

        def group(c, n_bufs):
            reads = []
            for u in range(n_bufs):
                a = pl.multiple_of(base + (c + u) * SC_ROWS, SC_ROWS)
                t = pl.multiple_of(row_off + lax.rem(a, n), SC_ROWS)
                pltpu.sync_copy(i_hbm.at[pl.ds(a, SC_ROWS)], idx_v[u])
                reads.append(pltpu.async_copy(x_hbm.at[pl.ds(t, SC_ROWS)], rows_v[u], sem_in[u]))
            writes = []
            for u in range(n_bufs):
                reads[u].wait()
                writes.append(pltpu.async_copy(rows_v[u], o_hbm.at[idx_v[u]], sem_out[u]))
            for w in writes:
                w.wait()

        _sc_chunk_loop(per_worker, group)

    return scatter(x, idx)


def _sc_gather_rows(src, idx):
    m = idx.shape[0]
    d = src.shape[1]
    per_worker = m // (SC_CORES * SC_SUBCORES)
    assert per_worker % SC_ROWS == 0

    @functools.partial(pl.kernel, out_type=jax.ShapeDtypeStruct((m, d), src.dtype),
                       mesh=_sc_mesh(), scratch_types=_sc_scratch(d, src.dtype))
    def gather(s_hbm, i_hbm, o_hbm, *scratch):
        idx_v, rows_v, sem_in, sem_out = _sc_split(scratch)
        base = _sc_worker_base(per_worker)

        def group(c, n_bufs):
            offs, reads = [], []
            for u in range(n_bufs):
                a = pl.multiple_of(base + (c + u) * SC_ROWS, SC_ROWS)
                offs.append(a)
                pltpu.sync_copy(i_hbm.at[pl.ds(a, SC_ROWS)], idx_v[u])
                reads.append(pltpu.async_copy(s_hbm.at[idx_v[u]], rows_v[u], sem_in[u]))
            writes = []
            for u in range(n_bufs):
                reads[u].wait()
                writes.append(pltpu.async_copy(rows_v[u], o_hbm.at[pl.ds(offs[u], SC_ROWS)], sem_out[u]))
            for w in writes:
                w.wait()

        _sc_chunk_loop(per_worker, group)

    return gather(src, idx)


def _expert_ffn_kernel(blk_ref, exp_ref, x_ref, wg_ref, wu_ref, wd_ref, y_ref, wgb_ref, wub_ref, wdb_ref):
    j = pl.program_id(0)

    @pl.when((j == 0) | (exp_ref[j] != exp_ref[jnp.maximum(j - 1, 0)]))
    def _():
        wgb_ref[...] = wg_ref[...].astype(BF16)
        wub_ref[...] = wu_ref[...].astype(BF16)
        wdb_ref[...] = wd_ref[...].astype(BF16)

    @pl.when((j == 0) | (blk_ref[j] != blk_ref[jnp.maximum(j - 1, 0)]))
    def _():
        x = _unpack_bf16_pairs(x_ref[...]).astype(BF16)
        hid = jax.nn.silu(_dot(x, wgb_ref[...])) * _dot(x, wub_ref[...])
        y_ref[...] = _pack_bf16_pairs(_dot(hid.astype(BF16), wdb_ref[...]))


def _expert_ffn(xs, blk, exp, wg, wu, wd, layer):
    rows, d_packed = xs.shape
    d = 2 * d_packed
    w_map = lambda j, blk, exp: (layer, exp[j], 0, 0)
    return pl.pallas_call(
        _expert_ffn_kernel,
        out_shape=jax.ShapeDtypeStruct((rows, d_packed), jnp.int32),
        grid_spec=pltpu.PrefetchScalarGridSpec(
            num_scalar_prefetch=2,
            grid=(blk.shape[0],),
            in_specs=[pl.BlockSpec((MOE_TILE, d_packed), lambda j, blk, exp: (blk[j], 0)),
                      pl.BlockSpec((None, None, d, EXPERT_HIDDEN), w_map),
                      pl.BlockSpec((None, None, d, EXPERT_HIDDEN), w_map),
                      pl.BlockSpec((None, None, EXPERT_HIDDEN, d), w_map)],
            out_specs=pl.BlockSpec((MOE_TILE, d_packed), lambda j, blk, exp: (blk[j], 0)),
            scratch_shapes=[pltpu.VMEM((d, EXPERT_HIDDEN), BF16),
                            pltpu.VMEM((d, EXPERT_HIDDEN), BF16),
                            pltpu.VMEM((EXPERT_HIDDEN, d), BF16)]),
        compiler_params=_cparams(1),
        name="expert_ffn",
    )(blk, exp, xs, wg, wu, wd)


def _combine_kernel(x1_ref, y1_ref, y2_ref, route_ref, mod_ref, fgain_ref, *rest, final):
    o_ref = rest[-1]
    route = route_ref[...]
    y = route[:, 2:3] * _unpack_bf16_pairs(y1_ref[...]) + route[:, 3:4] * _unpack_bf16_pairs(y2_ref[...])
    x2 = x1_ref[...] + mod_ref[5:6, :] * y
    if final:
        ms = jnp.mean(x2 * x2, axis=-1, keepdims=True)
        x2 = (x2 * lax.rsqrt(ms + EPS)) * fgain_ref[...]
    o_ref[...] = x2


def _combine(x1, ys, route, modsel, fgain, prev, b0, nb, n_lat_tiles, final):
    b, rows, d = x1.shape
    n_t = rows // TILE
    tok = lambda bi, ti: (b0 + bi, ti, 0)
    in_specs = [pl.BlockSpec((None, TILE, d), tok),
                pl.BlockSpec((TILE, d // 2), lambda bi, ti: (bi * n_t + ti, 0)),
                pl.BlockSpec((TILE, d // 2), lambda bi, ti: ((nb + bi) * n_t + ti, 0)),
                pl.BlockSpec((None, TILE, LANES), tok),
                pl.BlockSpec((None, 6, d),
                             lambda bi, ti: (2 * (b0 + bi) + (ti >= n_lat_tiles).astype(jnp.int32), 0, 0)),
                pl.BlockSpec((1, d), lambda bi, ti: (0, 0))]
    args = [x1, ys, ys, route, modsel, fgain]
    aliases = {}
    if prev is not None:
        in_specs.append(pl.BlockSpec(memory_space=pl.ANY))
        args.append(prev)
        aliases = {len(args) - 1: 0}
    return pl.pallas_call(
        functools.partial(_combine_kernel, final=final),
        out_shape=jax.ShapeDtypeStruct((b, rows, d), F32),
        grid=(nb, n_t),
        in_specs=in_specs,
        out_specs=pl.BlockSpec((None, TILE, d), tok),
        input_output_aliases=aliases,
        compiler_params=_cparams(2),
        name="moe_combine",
    )(*args)


def _routed_moe(tok, route, cnt, x1, wg, wu, wd, layer, modsel, fgain, n_lat_tiles, final):
    b, rows, d = x1.shape
    n_groups = cnt.shape[0] // ROUTE_ROWS
    nb = b // n_groups
    n = nb * rows
    flat = route.reshape(b * rows, LANES)
    tok_flat = tok.reshape(b * rows, tok.shape[2])
    out = None
    for g in range(n_groups):
        part = flat[g * n:(g + 1) * n]
        idx = jnp.concatenate([part[:, 0], part[:, 1]]).astype(jnp.int32)
        xs = _sc_scatter_rows(tok_flat, g * n, n, idx, N_EXPERTS * n)

        e0 = ROUTE_ROWS * g + N_GROUPS
        counts = cnt[e0:e0 + N_EXPERTS, 0].astype(jnp.int32)
        tiles = (counts + MOE_TILE - 1) // MOE_TILE
        ends = jnp.cumsum(tiles)
        n_sched = 2 * n // MOE_TILE + N_EXPERTS
        j = jnp.minimum(jnp.arange(n_sched, dtype=jnp.int32), ends[-1] - 1)
        exp = jnp.sum((j[:, None] >= ends[None, :]).astype(jnp.int32), axis=1)
        blk = exp * (n // MOE_TILE) + j - (ends - tiles)[exp]

        ys = _expert_ffn(xs, blk, exp, wg, wu, wd, layer)
        yg = _sc_gather_rows(ys, idx)
        out = _combine(x1, yg, route, modsel, fgain, out, g * nb, nb, n_lat_tiles, final)
    return out


def _rope_tables(n_lat):
    t = jnp.arange(n_lat)
    row = (t // GRID_W).astype(F32)
    col = (t % GRID_W).astype(F32)

    def cs(dim):
        quarter = dim // 4
        freqs = ROPE_THETA ** (-jnp.arange(quarter, dtype=F32) / quarter)
        ang = jnp.concatenate([row[:, None] * freqs, col[:, None] * freqs], axis=-1)
        cos = jnp.tile(jnp.cos(ang), (1, 2 * LANES // dim))
        sin = jnp.tile(jnp.sin(ang), (1, 2 * LANES // dim))
        cos = jnp.concatenate([cos, jnp.ones((CTX_LEN, LANES), F32)], axis=0)
        sin = jnp.concatenate([sin, jnp.zeros((CTX_LEN, LANES), F32)], axis=0)
        return cos, sin

    cos_b, sin_b = cs(DIFF_QK_DIM)
    cos_c, sin_c = cs(HEAD_DIM)
    return jnp.concatenate([cos_b, sin_b, cos_c, sin_c], axis=1)


def _reordered_w_in(w_in):
    o_c = 3 * W_A + 3 * W_B
    heads = [w_in[:, o_c + h * HEAD_DIM:o_c + (h + 1) * HEAD_DIM] for h in GQA_Q_ORDER]
    return jnp.concatenate([w_in[:, :o_c]] + heads + [w_in[:, o_c + W_C:]], axis=1).astype(BF16)


def kernel(x, c, ctx, c_ctx, w_mod, b_mod, norm_attn, norm_ffn, w_in, w_out, na_rpb, diff_lambda_q1, diff_lambda_k1, diff_lambda_q2, diff_lambda_k2, diff_subln, gqa_q_norm, gqa_k_norm, router_group_w, router_group_b, router_expert_w, router_expert_b, w_gate, w_up, w_down, final_norm):
    b, s, d = x.shape
    assert d == D_MODEL and ctx.shape[1] == CTX_LEN and s % (NA_QROWS * GRID_W) == 0
    rows = s // GRID_W
    assert rows >= 2 * NA_QROWS
    t_all = s + CTX_LEN
    n_lat_tiles = s // TILE

    assert b + 1 <= SUBLANES
    c_rows = jnp.zeros((SUBLANES, d), F32).at[:b].set(c).at[b].set(c_ctx)
    mod = _modulation(c_rows, w_mod, b_mod)

    tab = _rope_tables(s)
    hidx = np.arange(HEAD_DIM)
    partner = np.where(hidx < HEAD_DIM // 2, hidx + HEAD_DIM // 2, hidx - HEAD_DIM // 2)
    blk = np.arange(W_C) // HEAD_DIM
    ones = jnp.asarray((blk[:, None] == blk[None, :]).astype(np.float32), BF16)
    dummy_aux = jnp.zeros((SUBLANES, LANES), F32)

    x_lat, x_ctx, ctx_blk = x, ctx, 0
    for l in range(DEPTH):
        ctx_out = l < DEPTH - 1
        lam_init = 0.8 - 0.6 * math.exp(-0.3 * l)
        m_lat = mod[l, :b].reshape(b, 1, 6, d)
        m_ctx = jnp.broadcast_to(mod[l, b].reshape(1, 1, 6, d), (b, 1, 6, d))
        modsel = jnp.concatenate([m_lat, m_ctx], axis=1).reshape(2 * b, 6, d)

        gq = jnp.stack([jnp.tile(gqa_q_norm[l], GQA_Q_HEADS), jnp.tile(gqa_q_norm[l][partner], GQA_Q_HEADS)])
        gk = jnp.stack([jnp.tile(gqa_k_norm[l], GQA_KV_HEADS), jnp.tile(gqa_k_norm[l][partner], GQA_KV_HEADS)])
        qa, ka, va, qb, kb, vb, qc, kc, vc = _in_projection(
            x_lat, x_ctx, ctx_blk, modsel, norm_attn[l][None], _reordered_w_in(w_in[l]), tab, gq, gk, ones,
            n_lat_tiles)

        n_qt = n_lat_tiles + 1 if ctx_out else n_lat_tiles
        pad = lambda v: jnp.pad(v, (0, LANES - v.shape[0]))
        aux = jnp.stack([pad(diff_lambda_q1[l]), pad(diff_lambda_k1[l]), pad(diff_lambda_q2[l]),
                         pad(diff_lambda_k2[l]), jnp.tile(diff_subln[l], 2),
                         jnp.zeros((LANES,), F32), jnp.zeros((LANES,), F32), jnp.zeros((LANES,), F32)])
        group_a = dict(n_qblk=1, n_sub=2, n_hp=NA_HEADS // 2, n_lat=s, mode="plain")
        group_b = dict(n_qblk=1, n_sub=4, n_hp=DIFF_HEADS // 2, n_lat=s, mode="diff", lam_init=lam_init)
        group_c = dict(n_qblk=3, n_sub=2, n_hp=1, n_lat=s, mode="plain")
        oa = _neighbourhood_attention(qa, ka, va, _na_bias_table(na_rpb[l], rows), s)
        ob = _flash(qb, kb, vb, aux, queries="latent", **group_b)
        oc = _flash(qc, kc, vc, dummy_aux, queries="latent", **group_c)
        mixed_lat = mixed_ctx = (oa, ob, oc)
        if ctx_out:
            mixed_ctx = (_flash(qa, ka, va, dummy_aux, queries="context", **group_a),
                         _flash(qb, kb, vb, aux, queries="context", **group_b),
                         _flash(qc, kc, vc, dummy_aux, queries="context", **group_c))

        w_o = w_out[l]
        o_c = W_A + W_B
        w_oc = jnp.concatenate([w_o[o_c + h * HEAD_DIM:o_c + (h + 1) * HEAD_DIM] for h in GQA_Q_ORDER], axis=0)
        wr = jnp.zeros((d, LANES), F32)
        wr = wr.at[:, :N_GROUPS].set(router_group_w[l]).at[:, N_GROUPS:N_GROUPS + N_EXPERTS].set(router_expert_w[l])
        wrh, wrl = _split_bf16(wr)
        br = jnp.zeros((1, LANES), F32)
        br = br.at[0, :N_GROUPS].set(router_group_b[l]).at[0, N_GROUPS:N_GROUPS + N_EXPERTS].set(router_expert_b[l])
        x1, tok, route, cnt = _out_projection(
            x_lat, x_ctx, ctx_blk, mixed_lat, mixed_ctx, w_o[:W_A].astype(BF16), w_o[W_A:W_A + W_B].astype(BF16),
            w_oc.astype(BF16), modsel, norm_ffn[l][None], wrh, wrl, br, n_qt, n_lat_tiles)
        xs = _routed_moe(tok, route, cnt, x1, w_gate, w_up, w_down, l, modsel, final_norm[None],
                         n_lat_tiles, final=not ctx_out)
        x_lat, x_ctx, ctx_blk = xs, xs, n_lat_tiles
    return xs
```
